```python
import jax, jax.numpy as jnp
from jax import lax
import numpy as np

D_MODEL = 1024
BATCH = 32
SEQ = 2048
DEPTH = 1

CHUNK = 64
RET_HEADS = 4
RET_DK = 128
RET_DV = 256
RET_QK_WIDTH = RET_HEADS * RET_DK
RET_V_WIDTH = RET_HEADS * RET_DV
LRU_WIDTH = 1024
LRU_BLOCKS = 4
LRU_BLOCK = LRU_WIDTH // LRU_BLOCKS
LRU_CONV = 4
LRU_C = 8.0
LRU_MIN_RAD = 0.9
LRU_MAX_RAD = 0.999
D_FF = 3 * D_MODEL
FFN_CONV = 3
ROPE_BASE = 10000.0
RMS_EPS = 1e-6
GN_EPS = 1e-6

IN_SIZES = (RET_QK_WIDTH, RET_QK_WIDTH, RET_V_WIDTH, RET_V_WIDTH,
            LRU_WIDTH, LRU_WIDTH, D_MODEL, D_MODEL)
D_IN = sum(IN_SIZES)
SPLIT_POINTS = tuple(sum(IN_SIZES[:i + 1]) for i in range(len(IN_SIZES) - 1))

kernel_name = "chunk_causal_retention_rglru_gated_hybrid"


def rms_norm(x, w):
    x32 = x.astype(jnp.float32)
    y = x32 * lax.rsqrt(jnp.mean(x32 * x32, axis=-1, keepdims=True) + RMS_EPS)
    return (y * w.astype(jnp.float32)).astype(x.dtype)


def causal_depthwise_conv(x, w, b):
    k_width, channels = w.shape
    y = lax.conv_general_dilated(
        x, w[:, None, :].astype(x.dtype), window_strides=(1,),
        padding=[(k_width - 1, 0)], dimension_numbers=("NWC", "WIO", "NWC"),
        feature_group_count=channels)
    return y + b.astype(x.dtype)


def rotary(x, positions):
    half = x.shape[-1] // 2
    inv_freq = ROPE_BASE ** (-jnp.arange(half, dtype=jnp.float32) / half)
    ang = positions.astype(jnp.float32)[..., None] * inv_freq
    cos = jnp.cos(ang)[:, :, None, :]
    sin = jnp.sin(ang)[:, :, None, :]
    x32 = x.astype(jnp.float32)
    x1, x2 = x32[..., :half], x32[..., half:]
    return jnp.concatenate([x1 * cos - x2 * sin, x1 * sin + x2 * cos], axis=-1).astype(x.dtype)


def chunkwise_retention(q, k, v):
    bsz, seq, heads, dk = q.shape
    dv = v.shape[-1]
    n_chunks = seq // CHUNK
    log_gamma = jnp.log1p(-jnp.power(2.0, -5.0 - jnp.arange(heads, dtype=jnp.float32)))
    idx = jnp.arange(CHUNK, dtype=jnp.float32)
    dist = jnp.abs(idx[:, None] - idx[None, :])
    intra_decay = jnp.exp(log_gamma[:, None, None] * dist)
    q_decay = jnp.exp(log_gamma[:, None] * (idx + 1.0))
    k_decay = jnp.exp(log_gamma[:, None] * (CHUNK - 1.0 - idx))
    chunk_decay = jnp.exp(log_gamma * CHUNK)

    def to_chunks(t):
        return t.astype(jnp.float32).reshape(bsz, n_chunks, CHUNK, heads, -1).transpose(1, 0, 3, 2, 4)

    qc, kc, vc = to_chunks(q), to_chunks(k), to_chunks(v)

    def step(state, inp):
        qi, ki, vi = inp
        scores = jnp.einsum("bhid,bhjd->bhij", qi, ki) * intra_decay
        out = (jnp.einsum("bhij,bhjv->bhiv", scores, vi)
               + jnp.einsum("bhid,bhdv->bhiv", qi * q_decay[:, :, None], state))
        state = (state * chunk_decay[:, None, None]
                 + jnp.einsum("bhjd,bhjv->bhdv", ki * k_decay[:, :, None], vi))
        return state, out

    state0 = jnp.zeros((bsz, heads, dk, dv), jnp.float32)
    _, out = lax.scan(step, state0, (qc, kc, vc))
    return out.transpose(1, 0, 3, 2, 4).reshape(bsz, seq, heads, dv)


def head_group_norm(o, w):
    bsz, seq, heads, dv = o.shape
    mu = jnp.mean(o, axis=-1, keepdims=True)
    var = jnp.mean(jnp.square(o - mu), axis=-1, keepdims=True)
    y = (o - mu) * lax.rsqrt(var + GN_EPS)
    return y.reshape(bsz, seq, heads * dv) * w.astype(jnp.float32)


def _linear_recurrence_combine(e1, e2):
    a1, b1 = e1
    a2, b2 = e2
    return a1 * a2, a2 * b1 + b2


def rg_lru(x, w_r, b_r, w_i, b_i, lam):
    bsz, seq, width = x.shape
    x32 = x.astype(jnp.float32)
    xb = x32.reshape(bsz, seq, LRU_BLOCKS, LRU_BLOCK)
    r = jax.nn.sigmoid(jnp.einsum("bsni,nij->bsnj", xb, w_r.astype(jnp.float32))
                       + b_r.astype(jnp.float32)).reshape(bsz, seq, width)
    i = jax.nn.sigmoid(jnp.einsum("bsni,nij->bsnj", xb, w_i.astype(jnp.float32))
                       + b_i.astype(jnp.float32)).reshape(bsz, seq, width)
    log_a = -LRU_C * r * jax.nn.softplus(-lam.astype(jnp.float32))
    a = jnp.exp(log_a)
    b = jnp.sqrt(-jnp.expm1(2.0 * log_a)) * (i * x32)
    _, h = lax.associative_scan(_linear_recurrence_combine, (a, b), axis=1)
    return h


def _fwd_setup_inputs(seed: int = 0) -> dict:
    key = jax.random.key(seed)
    ks = jax.random.split(key, 24)

    def nrm(k, shape, scale):
        return jax.random.normal(k, shape, jnp.float32) * scale

    def gain(k, shape):
        return 1.0 + 0.02 * jax.random.normal(k, shape, jnp.float32)

    x = jax.random.normal(ks[0], (BATCH, SEQ, D_MODEL), jnp.float32)
    start = jax.random.randint(ks[1], (BATCH, 1), 0, 64) * CHUNK
    positions = (start + jnp.arange(SEQ)[None, :]).astype(jnp.int32)

    u = jax.random.uniform(ks[12], (DEPTH, LRU_WIDTH), jnp.float32,
                           LRU_MIN_RAD ** 2, LRU_MAX_RAD ** 2)
    a0 = jnp.sqrt(u)
    lru_lambda = jnp.log(a0) - jnp.log1p(-a0)

    return {
        "x": x,
        "positions": positions,
        "norm1_w": gain(ks[2], (DEPTH, D_MODEL)),
        "w_in": nrm(ks[3], (DEPTH, D_MODEL, D_IN), D_MODEL ** -0.5),
        "merge_gate_b": nrm(ks[4], (DEPTH, 2, D_MODEL), 0.02),
        "ret_gn_w": gain(ks[5], (DEPTH, RET_V_WIDTH)),
        "w_ret_o": nrm(ks[6], (DEPTH, RET_V_WIDTH, D_MODEL), RET_V_WIDTH ** -0.5),
        "lru_conv_w": nrm(ks[7], (DEPTH, LRU_CONV, LRU_WIDTH), LRU_CONV ** -0.5),
        "lru_conv_b": nrm(ks[8], (DEPTH, LRU_WIDTH), 0.02),
        "lru_w_r": nrm(ks[9], (DEPTH, LRU_BLOCKS, LRU_BLOCK, LRU_BLOCK), LRU_BLOCK ** -0.5),
        "lru_b_r": nrm(ks[10], (DEPTH, LRU_BLOCKS, LRU_BLOCK), 0.02),
        "lru_w_i": nrm(ks[11], (DEPTH, LRU_BLOCKS, LRU_BLOCK, LRU_BLOCK), LRU_BLOCK ** -0.5),
        "lru_b_i": nrm(ks[13], (DEPTH, LRU_BLOCKS, LRU_BLOCK), 0.02),
        "lru_lambda": lru_lambda,
        "w_lru_o": nrm(ks[14], (DEPTH, LRU_WIDTH, D_MODEL), LRU_WIDTH ** -0.5),
        "w_out": nrm(ks[15], (DEPTH, D_MODEL, D_MODEL), D_MODEL ** -0.5),
        "norm2_w": gain(ks[16], (DEPTH, D_MODEL)),
        "ffn_w_up": nrm(ks[17], (DEPTH, D_MODEL, 2 * D_FF), D_MODEL ** -0.5),
        "ffn_conv_w": nrm(ks[18], (DEPTH, FFN_CONV, D_FF), FFN_CONV ** -0.5),
        "ffn_conv_b": nrm(ks[19], (DEPTH, D_FF), 0.02),
        "ffn_w_down": nrm(ks[20], (DEPTH, D_FF, D_MODEL), D_FF ** -0.5),
        "norm_f_w": gain(ks[21], (D_MODEL,)),
    }


def _fwd_reference(x, positions, norm1_w, w_in, merge_gate_b, ret_gn_w, w_ret_o,
              lru_conv_w, lru_conv_b, lru_w_r, lru_b_r, lru_w_i, lru_b_i, lru_lambda,
              w_lru_o, w_out, norm2_w, ffn_w_up, ffn_conv_w, ffn_conv_b, ffn_w_down,
              norm_f_w):
    bsz, seq, _ = x.shape
    for l in range(DEPTH):
        h = rms_norm(x, norm1_w[l])
        proj = h @ w_in[l]
        q, k, v, g_ret, x_lru, y_lru, gate_ret, gate_lru = jnp.split(proj, SPLIT_POINTS, axis=-1)

        q = rotary(q.reshape(bsz, seq, RET_HEADS, RET_DK), positions)
        k = rotary(k.reshape(bsz, seq, RET_HEADS, RET_DK), positions) * (RET_DK ** -0.5)
        o = chunkwise_retention(q, k, v.reshape(bsz, seq, RET_HEADS, RET_DV))
        o = head_group_norm(o, ret_gn_w[l])
        y_a = (o * jax.nn.silu(g_ret.astype(jnp.float32))).astype(x.dtype) @ w_ret_o[l]

        xc = causal_depthwise_conv(x_lru, lru_conv_w[l], lru_conv_b[l])
        hl = rg_lru(xc, lru_w_r[l], lru_b_r[l], lru_w_i[l], lru_b_i[l], lru_lambda[l])
        y_b = (hl * jax.nn.gelu(y_lru.astype(jnp.float32))).astype(x.dtype) @ w_lru_o[l]

        mix = (jax.nn.sigmoid(gate_ret + merge_gate_b[l, 0]) * y_a
               + jax.nn.sigmoid(gate_lru + merge_gate_b[l, 1]) * y_b)
        x = x + mix @ w_out[l]

        h = rms_norm(x, norm2_w[l])
        up = h @ ffn_w_up[l]
        gate, val = jnp.split(up, [D_FF], axis=-1)
        gate = causal_depthwise_conv(gate, ffn_conv_w[l], ffn_conv_b[l])
        x = x + (jax.nn.gelu(gate) * val) @ ffn_w_down[l]
    return rms_norm(x, norm_f_w)


import jax as _jax
import jax.numpy as _jnp

TWIN_FORMAT = 'train_step'
FWD_PARAMS = ['x', 'positions', 'norm1_w', 'w_in', 'merge_gate_b', 'ret_gn_w', 'w_ret_o', 'lru_conv_w', 'lru_conv_b', 'lru_w_r', 'lru_b_r', 'lru_w_i', 'lru_b_i', 'lru_lambda', 'w_lru_o', 'w_out', 'norm2_w', 'ffn_w_up', 'ffn_conv_w', 'ffn_conv_b', 'ffn_w_down', 'norm_f_w']
TWIN_WEIGHTS = ['norm1_w', 'w_in', 'merge_gate_b', 'ret_gn_w', 'w_ret_o', 'lru_conv_w', 'lru_conv_b', 'lru_w_r', 'lru_b_r', 'lru_w_i', 'lru_b_i', 'lru_lambda', 'w_lru_o', 'w_out', 'norm2_w', 'ffn_w_up', 'ffn_conv_w', 'ffn_conv_b', 'ffn_w_down', 'norm_f_w']
TWIN_DIFF_INPUT = 'x'
TWIN_INPUTS = ['x', 'positions', 'norm1_w', 'w_in', 'merge_gate_b', 'ret_gn_w', 'w_ret_o', 'lru_conv_w', 'lru_conv_b', 'lru_w_r', 'lru_b_r', 'lru_w_i', 'lru_b_i', 'lru_lambda', 'w_lru_o', 'w_out', 'norm2_w', 'ffn_w_up', 'ffn_conv_w', 'ffn_conv_b', 'ffn_w_down', 'norm_f_w', 'loss_target', 'm_norm1_w', 'm_w_in', 'm_merge_gate_b', 'm_ret_gn_w', 'm_w_ret_o', 'm_lru_conv_w', 'm_lru_conv_b', 'm_lru_w_r', 'm_lru_b_r', 'm_lru_w_i', 'm_lru_b_i', 'm_lru_lambda', 'm_w_lru_o', 'm_w_out', 'm_norm2_w', 'm_ffn_w_up', 'm_ffn_conv_w', 'm_ffn_conv_b', 'm_ffn_w_down', 'm_norm_f_w', 'v_norm1_w', 'v_w_in', 'v_merge_gate_b', 'v_ret_gn_w', 'v_w_ret_o', 'v_lru_conv_w', 'v_lru_conv_b', 'v_lru_w_r', 'v_lru_b_r', 'v_lru_w_i', 'v_lru_b_i', 'v_lru_lambda', 'v_w_lru_o', 'v_w_out', 'v_norm2_w', 'v_ffn_w_up', 'v_ffn_conv_w', 'v_ffn_conv_b', 'v_ffn_w_down', 'v_norm_f_w']
TWIN_OUTPUTS = ['loss', 'grad_x', 'grad_norm1_w', 'grad_w_in', 'grad_merge_gate_b', 'grad_ret_gn_w', 'grad_w_ret_o', 'grad_lru_conv_w', 'grad_lru_conv_b', 'grad_lru_w_r', 'grad_lru_b_r', 'grad_lru_w_i', 'grad_lru_b_i', 'grad_lru_lambda', 'grad_w_lru_o', 'grad_w_out', 'grad_norm2_w', 'grad_ffn_w_up', 'grad_ffn_conv_w', 'grad_ffn_conv_b', 'grad_ffn_w_down', 'grad_norm_f_w', 'delta_norm1_w', 'delta_w_in', 'delta_merge_gate_b', 'delta_ret_gn_w', 'delta_w_ret_o', 'delta_lru_conv_w', 'delta_lru_conv_b', 'delta_lru_w_r', 'delta_lru_b_r', 'delta_lru_w_i', 'delta_lru_b_i', 'delta_lru_lambda', 'delta_w_lru_o', 'delta_w_out', 'delta_norm2_w', 'delta_ffn_w_up', 'delta_ffn_conv_w', 'delta_ffn_conv_b', 'delta_ffn_w_down', 'delta_norm_f_w', 'new_m_norm1_w', 'new_m_w_in', 'new_m_merge_gate_b', 'new_m_ret_gn_w', 'new_m_w_ret_o', 'new_m_lru_conv_w', 'new_m_lru_conv_b', 'new_m_lru_w_r', 'new_m_lru_b_r', 'new_m_lru_w_i', 'new_m_lru_b_i', 'new_m_lru_lambda', 'new_m_w_lru_o', 'new_m_w_out', 'new_m_norm2_w', 'new_m_ffn_w_up', 'new_m_ffn_conv_w', 'new_m_ffn_conv_b', 'new_m_ffn_w_down', 'new_m_norm_f_w', 'new_v_norm1_w', 'new_v_w_in', 'new_v_merge_gate_b', 'new_v_ret_gn_w', 'new_v_w_ret_o', 'new_v_lru_conv_w', 'new_v_lru_conv_b', 'new_v_lru_w_r', 'new_v_lru_b_r', 'new_v_lru_w_i', 'new_v_lru_b_i', 'new_v_lru_lambda', 'new_v_w_lru_o', 'new_v_w_out', 'new_v_norm2_w', 'new_v_ffn_w_up', 'new_v_ffn_conv_w', 'new_v_ffn_conv_b', 'new_v_ffn_w_down', 'new_v_norm_f_w']
TWIN_LEAF_KINDS = {'loss': 'loss', 'grad_x': 'grad_x', 'grad_norm1_w': 'grad_w', 'grad_w_in': 'grad_w', 'grad_merge_gate_b': 'grad_w', 'grad_ret_gn_w': 'grad_w', 'grad_w_ret_o': 'grad_w', 'grad_lru_conv_w': 'grad_w', 'grad_lru_conv_b': 'grad_w', 'grad_lru_w_r': 'grad_w', 'grad_lru_b_r': 'grad_w', 'grad_lru_w_i': 'grad_w', 'grad_lru_b_i': 'grad_w', 'grad_lru_lambda': 'grad_w', 'grad_w_lru_o': 'grad_w', 'grad_w_out': 'grad_w', 'grad_norm2_w': 'grad_w', 'grad_ffn_w_up': 'grad_w', 'grad_ffn_conv_w': 'grad_w', 'grad_ffn_conv_b': 'grad_w', 'grad_ffn_w_down': 'grad_w', 'grad_norm_f_w': 'grad_w', 'delta_norm1_w': 'delta_w', 'delta_w_in': 'delta_w', 'delta_merge_gate_b': 'delta_w', 'delta_ret_gn_w': 'delta_w', 'delta_w_ret_o': 'delta_w', 'delta_lru_conv_w': 'delta_w', 'delta_lru_conv_b': 'delta_w', 'delta_lru_w_r': 'delta_w', 'delta_lru_b_r': 'delta_w', 'delta_lru_w_i': 'delta_w', 'delta_lru_b_i': 'delta_w', 'delta_lru_lambda': 'delta_w', 'delta_w_lru_o': 'delta_w', 'delta_w_out': 'delta_w', 'delta_norm2_w': 'delta_w', 'delta_ffn_w_up': 'delta_w', 'delta_ffn_conv_w': 'delta_w', 'delta_ffn_conv_b': 'delta_w', 'delta_ffn_w_down': 'delta_w', 'delta_norm_f_w': 'delta_w', 'new_m_norm1_w': 'new_m', 'new_m_w_in': 'new_m', 'new_m_merge_gate_b': 'new_m', 'new_m_ret_gn_w': 'new_m', 'new_m_w_ret_o': 'new_m', 'new_m_lru_conv_w': 'new_m', 'new_m_lru_conv_b': 'new_m', 'new_m_lru_w_r': 'new_m', 'new_m_lru_b_r': 'new_m', 'new_m_lru_w_i': 'new_m', 'new_m_lru_b_i': 'new_m', 'new_m_lru_lambda': 'new_m', 'new_m_w_lru_o': 'new_m', 'new_m_w_out': 'new_m', 'new_m_norm2_w': 'new_m', 'new_m_ffn_w_up': 'new_m', 'new_m_ffn_conv_w': 'new_m', 'new_m_ffn_conv_b': 'new_m', 'new_m_ffn_w_down': 'new_m', 'new_m_norm_f_w': 'new_m', 'new_v_norm1_w': 'new_v', 'new_v_w_in': 'new_v', 'new_v_merge_gate_b': 'new_v', 'new_v_ret_gn_w': 'new_v', 'new_v_w_ret_o': 'new_v', 'new_v_lru_conv_w': 'new_v', 'new_v_lru_conv_b': 'new_v', 'new_v_lru_w_r': 'new_v', 'new_v_lru_b_r': 'new_v', 'new_v_lru_w_i': 'new_v', 'new_v_lru_b_i': 'new_v', 'new_v_lru_lambda': 'new_v', 'new_v_w_lru_o': 'new_v', 'new_v_w_out': 'new_v', 'new_v_norm2_w': 'new_v', 'new_v_ffn_w_up': 'new_v', 'new_v_ffn_conv_w': 'new_v', 'new_v_ffn_conv_b': 'new_v', 'new_v_ffn_w_down': 'new_v', 'new_v_norm_f_w': 'new_v'}


def _forward(args):
    return _fwd_reference(*[args[k] for k in FWD_PARAMS])


def _output_shape():
    out = _jax.eval_shape(lambda: _forward(_fwd_setup_inputs(0)))
    return out.shape, out.dtype

N_MICROBATCH = 1
ADAM_LR = 0.001
ADAM_B1 = 0.9
ADAM_B2 = 0.999
ADAM_EPS = 1e-08
ADAM_WD = 0.01
ADAM_STEP = 10
PER_EXAMPLE_BATCH_AXIS = {'x': 0, 'positions': 0, 'loss_target': 0}
SHARED_INPUTS = []
_WEIGHT_DTYPES = {'norm1_w': _jnp.float32, 'w_in': _jnp.float32, 'merge_gate_b': _jnp.float32, 'ret_gn_w': _jnp.float32, 'w_ret_o': _jnp.float32, 'lru_conv_w': _jnp.float32, 'lru_conv_b': _jnp.float32, 'lru_w_r': _jnp.float32, 'lru_b_r': _jnp.float32, 'lru_w_i': _jnp.float32, 'lru_b_i': _jnp.float32, 'lru_lambda': _jnp.float32, 'w_lru_o': _jnp.float32, 'w_out': _jnp.float32, 'norm2_w': _jnp.float32, 'ffn_w_up': _jnp.float32, 'ffn_conv_w': _jnp.float32, 'ffn_conv_b': _jnp.float32, 'ffn_w_down': _jnp.float32, 'norm_f_w': _jnp.float32}
MOMENT_SCALE = {'norm1_w': 2.225011e-01, 'w_in': 7.444626e-02, 'merge_gate_b': 2.792776e-02, 'ret_gn_w': 9.732056e-02, 'w_ret_o': 8.644551e-02, 'lru_conv_w': 6.119784e-02, 'lru_conv_b': 2.989096e-01, 'lru_w_r': 1.303420e-02, 'lru_b_r': 1.195048e-02, 'lru_w_i': 2.202353e-02, 'lru_b_i': 2.095139e-02, 'lru_lambda': 2.405759e-02, 'w_lru_o': 5.290262e-02, 'w_out': 1.007924e-01, 'norm2_w': 1.922827e-01, 'ffn_w_up': 7.698685e-02, 'ffn_conv_w': 8.232941e-02, 'ffn_conv_b': 8.099220e-02, 'ffn_w_down': 1.316116e-01, 'norm_f_w': 6.402338e+01}


def _to_microbatches(a, axis):
    t = _jnp.moveaxis(a, axis, 0)
    t = t.reshape((N_MICROBATCH, t.shape[0] // N_MICROBATCH) + t.shape[1:])
    return _jnp.moveaxis(t, 1, axis + 1)


def setup_inputs(seed: int = 0) -> dict:
    inp = _fwd_setup_inputs(seed)
    key = _jax.random.fold_in(_jax.random.key(seed), 7919)
    shape, _ = _output_shape()
    out = dict(inp)
    out["loss_target"] = _jax.random.normal(_jax.random.fold_in(key, 0), shape, _jnp.float32)
    for i, name in enumerate(TWIN_WEIGHTS):
        w = inp[name].astype(_jnp.float32)
        if MOMENT_SCALE is None:
            s = _jnp.sqrt(_jnp.mean(_jnp.square(w)) + 1e-30)
        else:
            s = MOMENT_SCALE[name]
        km, kv = _jax.random.split(_jax.random.fold_in(key, i + 1))
        out[name] = w
        out["m_" + name] = s * _jax.random.normal(km, w.shape, _jnp.float32)
        out["v_" + name] = (s * s) * _jax.random.uniform(kv, w.shape, _jnp.float32, 0.5, 1.5)
    if N_MICROBATCH > 1:
        for name, axis in PER_EXAMPLE_BATCH_AXIS.items():
            out[name] = _to_microbatches(out[name], axis)
    return {'x': out['x'], 'positions': out['positions'], 'norm1_w': out['norm1_w'], 'w_in': out['w_in'], 'merge_gate_b': out['merge_gate_b'], 'ret_gn_w': out['ret_gn_w'], 'w_ret_o': out['w_ret_o'], 'lru_conv_w': out['lru_conv_w'], 'lru_conv_b': out['lru_conv_b'], 'lru_w_r': out['lru_w_r'], 'lru_b_r': out['lru_b_r'], 'lru_w_i': out['lru_w_i'], 'lru_b_i': out['lru_b_i'], 'lru_lambda': out['lru_lambda'], 'w_lru_o': out['w_lru_o'], 'w_out': out['w_out'], 'norm2_w': out['norm2_w'], 'ffn_w_up': out['ffn_w_up'], 'ffn_conv_w': out['ffn_conv_w'], 'ffn_conv_b': out['ffn_conv_b'], 'ffn_w_down': out['ffn_w_down'], 'norm_f_w': out['norm_f_w'], 'loss_target': out['loss_target'], 'm_norm1_w': out['m_norm1_w'], 'm_w_in': out['m_w_in'], 'm_merge_gate_b': out['m_merge_gate_b'], 'm_ret_gn_w': out['m_ret_gn_w'], 'm_w_ret_o': out['m_w_ret_o'], 'm_lru_conv_w': out['m_lru_conv_w'], 'm_lru_conv_b': out['m_lru_conv_b'], 'm_lru_w_r': out['m_lru_w_r'], 'm_lru_b_r': out['m_lru_b_r'], 'm_lru_w_i': out['m_lru_w_i'], 'm_lru_b_i': out['m_lru_b_i'], 'm_lru_lambda': out['m_lru_lambda'], 'm_w_lru_o': out['m_w_lru_o'], 'm_w_out': out['m_w_out'], 'm_norm2_w': out['m_norm2_w'], 'm_ffn_w_up': out['m_ffn_w_up'], 'm_ffn_conv_w': out['m_ffn_conv_w'], 'm_ffn_conv_b': out['m_ffn_conv_b'], 'm_ffn_w_down': out['m_ffn_w_down'], 'm_norm_f_w': out['m_norm_f_w'], 'v_norm1_w': out['v_norm1_w'], 'v_w_in': out['v_w_in'], 'v_merge_gate_b': out['v_merge_gate_b'], 'v_ret_gn_w': out['v_ret_gn_w'], 'v_w_ret_o': out['v_w_ret_o'], 'v_lru_conv_w': out['v_lru_conv_w'], 'v_lru_conv_b': out['v_lru_conv_b'], 'v_lru_w_r': out['v_lru_w_r'], 'v_lru_b_r': out['v_lru_b_r'], 'v_lru_w_i': out['v_lru_w_i'], 'v_lru_b_i': out['v_lru_b_i'], 'v_lru_lambda': out['v_lru_lambda'], 'v_w_lru_o': out['v_w_lru_o'], 'v_w_out': out['v_w_out'], 'v_norm2_w': out['v_norm2_w'], 'v_ffn_w_up': out['v_ffn_w_up'], 'v_ffn_conv_w': out['v_ffn_conv_w'], 'v_ffn_conv_b': out['v_ffn_conv_b'], 'v_ffn_w_down': out['v_ffn_w_down'], 'v_norm_f_w': out['v_norm_f_w']}


def _loss(weights, diff, rest, loss_target):
    with _jax.named_scope("forward"):
        args = {**rest, TWIN_DIFF_INPUT: diff, **{k: w.astype(_WEIGHT_DTYPES[k]) for k, w in weights.items()}}
        y = _forward(args)
    with _jax.named_scope("loss_head"):
        err = _jnp.square(y.astype(_jnp.float32) - loss_target)
        return 0.5 * _jnp.sum(_jnp.mean(err, axis=-1)) if err.ndim else 0.5 * err


def _adamw(w, g, m, v):
    m = ADAM_B1 * m + (1.0 - ADAM_B1) * g
    v = ADAM_B2 * v + (1.0 - ADAM_B2) * _jnp.square(g)
    m_hat = m / (1.0 - ADAM_B1 ** ADAM_STEP)
    v_hat = v / (1.0 - ADAM_B2 ** ADAM_STEP)
    delta = -ADAM_LR * (m_hat / (_jnp.sqrt(v_hat) + ADAM_EPS) + ADAM_WD * w)
    return delta, m, v


def reference(x, positions, norm1_w, w_in, merge_gate_b, ret_gn_w, w_ret_o, lru_conv_w, lru_conv_b, lru_w_r, lru_b_r, lru_w_i, lru_b_i, lru_lambda, w_lru_o, w_out, norm2_w, ffn_w_up, ffn_conv_w, ffn_conv_b, ffn_w_down, norm_f_w, loss_target, m_norm1_w, m_w_in, m_merge_gate_b, m_ret_gn_w, m_w_ret_o, m_lru_conv_w, m_lru_conv_b, m_lru_w_r, m_lru_b_r, m_lru_w_i, m_lru_b_i, m_lru_lambda, m_w_lru_o, m_w_out, m_norm2_w, m_ffn_w_up, m_ffn_conv_w, m_ffn_conv_b, m_ffn_w_down, m_norm_f_w, v_norm1_w, v_w_in, v_merge_gate_b, v_ret_gn_w, v_w_ret_o, v_lru_conv_w, v_lru_conv_b, v_lru_w_r, v_lru_b_r, v_lru_w_i, v_lru_b_i, v_lru_lambda, v_w_lru_o, v_w_out, v_norm2_w, v_ffn_w_up, v_ffn_conv_w, v_ffn_conv_b, v_ffn_w_down, v_norm_f_w):
    given = dict(x=x, positions=positions, norm1_w=norm1_w, w_in=w_in, merge_gate_b=merge_gate_b, ret_gn_w=ret_gn_w, w_ret_o=w_ret_o, lru_conv_w=lru_conv_w, lru_conv_b=lru_conv_b, lru_w_r=lru_w_r, lru_b_r=lru_b_r, lru_w_i=lru_w_i, lru_b_i=lru_b_i, lru_lambda=lru_lambda, w_lru_o=w_lru_o, w_out=w_out, norm2_w=norm2_w, ffn_w_up=ffn_w_up, ffn_conv_w=ffn_conv_w, ffn_conv_b=ffn_conv_b, ffn_w_down=ffn_w_down, norm_f_w=norm_f_w, loss_target=loss_target, m_norm1_w=m_norm1_w, m_w_in=m_w_in, m_merge_gate_b=m_merge_gate_b, m_ret_gn_w=m_ret_gn_w, m_w_ret_o=m_w_ret_o, m_lru_conv_w=m_lru_conv_w, m_lru_conv_b=m_lru_conv_b, m_lru_w_r=m_lru_w_r, m_lru_b_r=m_lru_b_r, m_lru_w_i=m_lru_w_i, m_lru_b_i=m_lru_b_i, m_lru_lambda=m_lru_lambda, m_w_lru_o=m_w_lru_o, m_w_out=m_w_out, m_norm2_w=m_norm2_w, m_ffn_w_up=m_ffn_w_up, m_ffn_conv_w=m_ffn_conv_w, m_ffn_conv_b=m_ffn_conv_b, m_ffn_w_down=m_ffn_w_down, m_norm_f_w=m_norm_f_w, v_norm1_w=v_norm1_w, v_w_in=v_w_in, v_merge_gate_b=v_merge_gate_b, v_ret_gn_w=v_ret_gn_w, v_w_ret_o=v_w_ret_o, v_lru_conv_w=v_lru_conv_w, v_lru_conv_b=v_lru_conv_b, v_lru_w_r=v_lru_w_r, v_lru_b_r=v_lru_b_r, v_lru_w_i=v_lru_w_i, v_lru_b_i=v_lru_b_i, v_lru_lambda=v_lru_lambda, v_w_lru_o=v_w_lru_o, v_w_out=v_w_out, v_norm2_w=v_norm2_w, v_ffn_w_up=v_ffn_w_up, v_ffn_conv_w=v_ffn_conv_w, v_ffn_conv_b=v_ffn_conv_b, v_ffn_w_down=v_ffn_w_down, v_norm_f_w=v_norm_f_w)
    weights = {n: given[n] for n in TWIN_WEIGHTS}
    shared = {n: given[n] for n in SHARED_INPUTS}
    per_example = {n: given[n] for n in ['x', 'positions']}
    grad_fn = _jax.value_and_grad(_loss, argnums=(0, 1))

    def one_microbatch(ex, loss_target):
        ex = dict(ex)
        diff = ex.pop(TWIN_DIFF_INPUT)
        return grad_fn(weights, diff, {**shared, **ex}, loss_target)

    if N_MICROBATCH == 1:
        loss, (grad_w, grad_x) = one_microbatch(per_example, given["loss_target"])
    else:
        def body(carry, xs):
            loss_sum, grad_sum = carry
            l_k, (gw_k, gx_k) = one_microbatch(xs[0], xs[1])
            with _jax.named_scope("update"):
                return (loss_sum + l_k, _jax.tree.map(_jnp.add, grad_sum, gw_k)), gx_k

        init = (_jnp.zeros((), _jnp.float32), _jax.tree.map(_jnp.zeros_like, weights))
        (loss, grad_w), grad_x = _jax.lax.scan(body, init, (per_example, given["loss_target"]))
    with _jax.named_scope("update"):
        delta_w, new_m, new_v = {}, {}, {}
        for n in TWIN_WEIGHTS:
            delta_w[n], new_m[n], new_v[n] = _adamw(weights[n], grad_w[n], given["m_" + n], given["v_" + n])
    return (loss, grad_x, *[grad_w[n] for n in TWIN_WEIGHTS], *[delta_w[n] for n in TWIN_WEIGHTS],
            *[new_m[n] for n in TWIN_WEIGHTS], *[new_v[n] for n in TWIN_WEIGHTS])
```

```python
import functools
import math

import numpy as np
import jax
import jax.numpy as jnp
from jax import lax
from jax.experimental import pallas as pl
from jax.experimental.pallas import tpu as pltpu

F32 = jnp.float32
BF16 = jnp.bfloat16

D_MODEL = 1024
RET_HEADS = 4
RET_DK = 128
RET_DV = 256
LRU_BLOCKS = 4
LRU_BLOCK = 256
LRU_C = 8.0
D_FF = 3072
D_IN = 7168
ROPE_BASE = 10000.0
RMS_EPS = 1e-6
GN_EPS = 1e-6
ADAM_LR, ADAM_B1, ADAM_B2, ADAM_EPS, ADAM_WD, ADAM_STEP = 0.001, 0.9, 0.999, 1e-08, 0.01, 10

N_CHIPS = 4
N_DEV = 8
SEQ_T = 256
REF_CHUNK = 64
COL = 1024
VMEM_LIMIT_BYTES = 56 * 1024 * 1024
MESH = pl.DeviceIdType.MESH
GELU_K = math.sqrt(2.0 / math.pi)
GELU_C = 0.044715


def _call(body, *, name, out_shape, grid=None, in_specs=None, out_specs=None, scratch=(), aliases=None):
    kwargs = dict(name=name, out_shape=out_shape, scratch_shapes=list(scratch),
                  compiler_params=pltpu.CompilerParams(vmem_limit_bytes=VMEM_LIMIT_BYTES))
    if grid is not None:
        kwargs["grid"] = grid
    if in_specs is not None:
        kwargs["in_specs"] = in_specs
    if out_specs is not None:
        kwargs["out_specs"] = out_specs
    if aliases:
        kwargs["input_output_aliases"] = aliases
    return pl.pallas_call(body, **kwargs)


def _dot(a, b):
    return jnp.dot(a, b, preferred_element_type=F32)


def _dot_nt(a, b):
    return lax.dot_general(a, b, (((1,), (1,)), ((), ())), preferred_element_type=F32)


def _dot_tn(a, b):
    return lax.dot_general(a, b, (((0,), (0,)), ((), ())), preferred_element_type=F32)


def _bf(x):
    return x.astype(BF16)


def _sigmoid(x):
    return 1.0 / (1.0 + jnp.exp(-x))


def _gelu(x):
    return 0.5 * x * (1.0 + jnp.tanh(GELU_K * (x + GELU_C * x * x * x)))


def _gelu_and_grad(x):
    t = jnp.tanh(GELU_K * (x + GELU_C * x * x * x))
    g = 0.5 * x * (1.0 + t)
    dg = 0.5 * (1.0 + t) + 0.5 * x * (1.0 - t * t) * (GELU_K * (1.0 + 3.0 * GELU_C * x * x))
    return g, dg


def _rms(x):
    r = lax.rsqrt(jnp.mean(x * x, axis=-1, keepdims=True) + RMS_EPS)
    return x * r, r


def _rms_bwd(dy, x, nw):
    xh, r = _rms(x)
    g = dy * nw
    dx = r * (g - xh * jnp.mean(g * xh, axis=-1, keepdims=True))
    return dx, jnp.sum(dy * xh, axis=0, keepdims=True)


def _row_acc(ref, row, val):
    ref[row:row + 1, :] = ref[row:row + 1, :] + val


def _shift_down(x, j, prev8):
    if j == 0:
        return x
    r = pltpu.roll(x, j, 0)
    rh = pltpu.roll(prev8, j, 0)
    row = lax.broadcasted_iota(jnp.int32, rh.shape, 0)
    top = jnp.where(row < j, rh, r[0:8])
    return jnp.concatenate([top, r[8:]], axis=0)


def _shift_up(x, j, next8):
    if j == 0:
        return x
    n = x.shape[0]
    r = pltpu.roll(x, n - j, 0)
    rh = pltpu.roll(next8, 8 - j, 0)
    row = lax.broadcasted_iota(jnp.int32, rh.shape, 0)
    bot = jnp.where(row >= 8 - j, rh, r[n - 8:])
    return jnp.concatenate([r[:n - 8], bot], axis=0)


def _scan_fwd(a, b):
    n = a.shape[0]
    row = lax.broadcasted_iota(jnp.int32, a.shape, 0)
    s = 1
    while s < n:
        keep = row >= s
        ar = pltpu.roll(a, s, 0)
        br = pltpu.roll(b, s, 0)
        b = jnp.where(keep, a * br + b, b)
        a = jnp.where(keep, a * ar, a)
        s *= 2
    return a, b


def _scan_bwd(a, b):
    n = a.shape[0]
    row = lax.broadcasted_iota(jnp.int32, a.shape, 0)
    s = 1
    while s < n:
        keep = row < n - s
        ar = pltpu.roll(a, n - s, 0)
        br = pltpu.roll(b, n - s, 0)
        b = jnp.where(keep, a * br + b, b)
        a = jnp.where(keep, a * ar, a)
        s *= 2
    return a, b


def _norm_matmul(x, nw, w, *, name, tm, tn):
    m, d = x.shape
    n = w.shape[1]

    def body(x_ref, nw_ref, w_ref, o_ref, h_ref, h_sc):
        @pl.when(pl.program_id(1) == 0)
        def _():
            xh, _ = _rms(x_ref[...])
            h = _bf(xh * nw_ref[...])
            h_sc[...] = h
            h_ref[...] = h

        o_ref[...] = _dot(h_sc[...], w_ref[...])

    return _call(
        body, name=name, grid=(m // tm, n // tn),
        in_specs=[pl.BlockSpec((tm, d), lambda i, j: (i, 0)),
                  pl.BlockSpec((1, d), lambda i, j: (0, 0)),
                  pl.BlockSpec((d, tn), lambda i, j: (0, j))],
        out_specs=[pl.BlockSpec((tm, tn), lambda i, j: (i, j)),
                   pl.BlockSpec((tm, d), lambda i, j: (i, 0))],
        out_shape=[jax.ShapeDtypeStruct((m, n), F32), jax.ShapeDtypeStruct((m, d), BF16)],
        scratch=[pltpu.VMEM((tm, d), BF16)],
    )(x, nw, w)


def _mm_nt(a, w, *, name, tm, out_dtype):
    m, k = a.shape
    n = w.shape[0]

    def body(a_ref, w_ref, o_ref):
        o_ref[...] = _dot_nt(_bf(a_ref[...]), w_ref[...]).astype(out_dtype)

    return _call(
        body, name=name, grid=(m // tm, n // COL),
        in_specs=[pl.BlockSpec((tm, k), lambda i, j: (i, 0)),
                  pl.BlockSpec((COL, k), lambda i, j: (j, 0))],
        out_specs=pl.BlockSpec((tm, COL), lambda i, j: (i, j)),
        out_shape=jax.ShapeDtypeStruct((m, n), out_dtype),
    )(a, w)


def _piece_layout(pieces):
    offs, nblk, o = [], [], 0
    for p in pieces:
        offs.append(o)
        nblk.append(p.shape[1] // COL)
        o += p.shape[1] // COL
    return offs, nblk, o


def _mm_tn(a, pieces, *, name, tm, out_dtype=BF16):
    m, k = a.shape
    offs, nblk, nn = _piece_layout(pieces)

    def piece_spec(o, nb):
        def idx(ki, nj, mi):
            use = jnp.logical_and(nj >= o, nj < o + nb)
            return (jnp.where(use, mi, 0), jnp.clip(nj - o, 0, nb - 1))
        return pl.BlockSpec((tm, COL), idx)

    def body(a_ref, *rest):
        p_refs, o_ref, acc = rest[:len(pieces)], rest[len(pieces)], rest[len(pieces) + 1]
        nj, mi = pl.program_id(1), pl.program_id(2)

        @pl.when(mi == 0)
        def _():
            acc[...] = jnp.zeros_like(acc)

        for p_ref, o, nb in zip(p_refs, offs, nblk):
            @pl.when(jnp.logical_and(nj >= o, nj < o + nb))
            def _(p_ref=p_ref):
                acc[...] += _dot_tn(_bf(a_ref[...]), _bf(p_ref[...]))

        @pl.when(mi == pl.num_programs(2) - 1)
        def _():
            o_ref[...] = acc[...].astype(out_dtype)

    return _call(
        body, name=name, grid=(k // COL, nn, m // tm),
        in_specs=[pl.BlockSpec((tm, COL), lambda ki, nj, mi: (mi, ki))]
        + [piece_spec(o, nb) for o, nb in zip(offs, nblk)],
        out_specs=pl.BlockSpec((COL, COL), lambda ki, nj, mi: (ki, nj)),
        out_shape=jax.ShapeDtypeStruct((k, nn * COL), out_dtype),
        scratch=[pltpu.VMEM((COL, COL), F32)],
    )(a, *pieces)


def _mm_nt_normbwd(pieces, w, x, nw, dres, *, name, tm):
    m, d = x.shape
    offs, nblk, nk = _piece_layout(pieces)

    def piece_spec(o, nb):
        return pl.BlockSpec((tm, COL), lambda i, k: (i, jnp.clip(k - o, 0, nb - 1)))

    def body(*refs):
        p_refs = refs[:len(pieces)]
        w_ref, x_ref, nw_ref, dres_ref, dx_ref, dnw_ref, acc = refs[len(pieces):]
        i, k = pl.program_id(0), pl.program_id(1)

        @pl.when(jnp.logical_and(i == 0, k == 0))
        def _():
            dnw_ref[...] = jnp.zeros_like(dnw_ref)

        @pl.when(k == 0)
        def _():
            acc[...] = jnp.zeros_like(acc)

        for p_ref, o, nb in zip(p_refs, offs, nblk):
            @pl.when(jnp.logical_and(k >= o, k < o + nb))
            def _(p_ref=p_ref):
                acc[...] += _dot_nt(_bf(p_ref[...]), w_ref[...])

        @pl.when(k == nk - 1)
        def _():
            dx, dnw = _rms_bwd(acc[...], x_ref[...], nw_ref[...])
            dx_ref[...] = dres_ref[...] + dx
            _row_acc(dnw_ref, 0, dnw)

    return _call(
        body, name=name, grid=(m // tm, nk),
        in_specs=[piece_spec(o, nb) for o, nb in zip(offs, nblk)]
        + [pl.BlockSpec((d, COL), lambda i, k: (0, k)),
           pl.BlockSpec((tm, d), lambda i, k: (i, 0)),
           pl.BlockSpec((1, d), lambda i, k: (0, 0)),
           pl.BlockSpec((tm, d), lambda i, k: (i, 0))],
        out_specs=[pl.BlockSpec((tm, d), lambda i, k: (i, 0)),
                   pl.BlockSpec((8, d), lambda i, k: (0, 0))],
        out_shape=[jax.ShapeDtypeStruct((m, d), F32), jax.ShapeDtypeStruct((8, d), F32)],
        scratch=[pltpu.VMEM((tm, d), F32)],
    )(*pieces, w, x, nw, dres)


def _rope_tables(pos3, invf):
    b, s, _ = pos3.shape

    def body(pos_ref, invf_ref, cos_ref, sin_ref):
        ang = pos_ref[...].astype(F32) * invf_ref[...]
        lane = lax.broadcasted_iota(jnp.int32, ang.shape, 1)
        cos_ref[...] = jnp.cos(ang)
        sin_ref[...] = jnp.where(lane < RET_DK // 2, -1.0, 1.0) * jnp.sin(ang)

    spec = pl.BlockSpec((None, SEQ_T, RET_DK), lambda i, c: (i, c, 0))
    return _call(
        body, name="rope_tables", grid=(b, s // SEQ_T),
        in_specs=[pl.BlockSpec((None, SEQ_T, 1), lambda i, c: (i, c, 0)),
                  pl.BlockSpec((1, RET_DK), lambda i, c: (0, 0))],
        out_specs=[spec, spec],
        out_shape=[jax.ShapeDtypeStruct((b, s, RET_DK), F32)] * 2,
    )(pos3, invf)


def _log_gamma(h):
    return float(np.log1p(-np.power(np.float32(2.0), np.float32(-5.0 - h))).astype(np.float32))


def _decay_matrix(h):
    lg = _log_gamma(h)
    n = lax.broadcasted_iota(jnp.int32, (SEQ_T, SEQ_T), 0)
    m = lax.broadcasted_iota(jnp.int32, (SEQ_T, SEQ_T), 1)
    same = (n // REF_CHUNK) == (m // REF_CHUNK)
    dist = jnp.where(same, jnp.abs(n - m), n - m).astype(F32)
    return jnp.where(jnp.logical_or(same, m < n), jnp.exp(lg * dist), 0.0)


def _decay_vectors(h):
    lg = _log_gamma(h)
    idx = lax.broadcasted_iota(jnp.int32, (SEQ_T, 1), 0).astype(F32)
    qd = jnp.exp(lg * (idx + 1.0))
    kd = jnp.exp(lg * (SEQ_T - 1.0 - idx))
    return qd, kd, math.exp(lg * SEQ_T)


def _rotate(x, cos, sin):
    return x * cos + pltpu.roll(x, RET_DK // 2, 1) * sin


def _rotate_bwd(d, cos, sin):
    return d * cos + pltpu.roll(d * sin, RET_DK // 2, 1)


def _ret_head(p_ref, cos, sin, h):
    q = p_ref[:, h * RET_DK:(h + 1) * RET_DK]
    k = p_ref[:, 512 + h * RET_DK:512 + (h + 1) * RET_DK]
    v = p_ref[:, 1024 + h * RET_DV:1024 + (h + 1) * RET_DV]
    g = p_ref[:, 2048 + h * RET_DV:2048 + (h + 1) * RET_DV]
    qr = _rotate(q, cos, sin)
    kr = _rotate(k, cos, sin) * (RET_DK ** -0.5)
    return qr, kr, v, g


def _group_norm(o):
    mu = jnp.mean(o, axis=-1, keepdims=True)
    oc = o - mu
    rstd = lax.rsqrt(jnp.mean(oc * oc, axis=-1, keepdims=True) + GN_EPS)
    return oc * rstd, rstd


def _retention_fwd(proj3, cos, sin, gnw):
    b, s, _ = proj3.shape
    nc = s // SEQ_T

    def body(p_ref, cos_ref, sin_ref, gnw_ref, a_ref, st_ref, state, wtab):
        c = pl.program_id(1)

        @pl.when(jnp.logical_and(pl.program_id(0) == 0, c == 0))
        def _():
            for h in range(RET_HEADS):
                wtab[h] = _decay_matrix(h)

        @pl.when(c == 0)
        def _():
            state[...] = jnp.zeros_like(state)

        cs, sn = cos_ref[...], sin_ref[...]
        for h in range(RET_HEADS):
            qd, kd, gt = _decay_vectors(h)
            qr, kr, v, g = _ret_head(p_ref, cs, sn, h)
            st = state[h]
            st_ref[h] = st
            p = _dot_nt(_bf(qr), _bf(kr)) * wtab[h]
            o = _dot(_bf(p), _bf(v)) + _dot(_bf(qr * qd), _bf(st))
            state[h] = st * gt + _dot_tn(_bf(kr * kd), _bf(v))
            on, _ = _group_norm(o)
            gw = gnw_ref[:, h * RET_DV:(h + 1) * RET_DV]
            a_ref[:, h * RET_DV:(h + 1) * RET_DV] = _bf(on * gw * (g * _sigmoid(g)))

    tab = pl.BlockSpec((None, SEQ_T, RET_DK), lambda i, c: (i, c, 0))
    return _call(
        body, name="retention_fwd", grid=(b, nc),
        in_specs=[pl.BlockSpec((None, SEQ_T, 3072), lambda i, c: (i, c, 0)), tab, tab,
                  pl.BlockSpec((1, D_MODEL), lambda i, c: (0, 0))],
        out_specs=[pl.BlockSpec((None, SEQ_T, D_MODEL), lambda i, c: (i, c, 0)),
                   pl.BlockSpec((None, None, RET_HEADS, RET_DK, RET_DV), lambda i, c: (i, c, 0, 0, 0))],
        out_shape=[jax.ShapeDtypeStruct((b, s, D_MODEL), BF16),
                   jax.ShapeDtypeStruct((b, nc, RET_HEADS, RET_DK, RET_DV), F32)],
        scratch=[pltpu.VMEM((RET_HEADS, RET_DK, RET_DV), F32),
                 pltpu.VMEM((RET_HEADS, SEQ_T, SEQ_T), F32)],
    )(proj3, cos, sin, gnw)


def _retention_bwd(proj3, cos, sin, gnw, states, da3):
    b, s, _ = proj3.shape
    nc = s // SEQ_T

    def body(p_ref, cos_ref, sin_ref, gnw_ref, st_ref, da_ref, d_ref, dgn_ref, dstate, wtab):
        c = pl.program_id(1)

        @pl.when(jnp.logical_and(pl.program_id(0) == 0, c == 0))
        def _():
            dgn_ref[...] = jnp.zeros_like(dgn_ref)
            for h in range(RET_HEADS):
                wtab[h] = _decay_matrix(h)

        @pl.when(c == 0)
        def _():
            dstate[...] = jnp.zeros_like(dstate)

        cs, sn = cos_ref[...], sin_ref[...]
        for h in range(RET_HEADS):
            qd, kd, gt = _decay_vectors(h)
            qr, kr, v, g = _ret_head(p_ref, cs, sn, h)
            st, dst, w = st_ref[h], dstate[h], wtab[h]
            qb, kb, vb = _bf(qr), _bf(kr), _bf(v)
            p = _dot_nt(qb, kb) * w
            o = _dot(_bf(p), vb) + _dot(_bf(qr * qd), _bf(st))
            on, rstd = _group_norm(o)
            gw = gnw_ref[:, h * RET_DV:(h + 1) * RET_DV]
            da = da_ref[:, h * RET_DV:(h + 1) * RET_DV].astype(F32)
            sg = _sigmoid(g)
            silu = g * sg
            dg = da * on * gw * (sg * (1.0 + g * (1.0 - sg)))
            dgn_ref[0:1, h * RET_DV:(h + 1) * RET_DV] += jnp.sum(da * silu * on, axis=0, keepdims=True)
            don = da * silu * gw
            do = rstd * (don - jnp.mean(don, axis=-1, keepdims=True)
                         - on * jnp.mean(don * on, axis=-1, keepdims=True))
            dob = _bf(do)
            dp = _dot_nt(dob, vb) * w
            dqr = _dot(_bf(dp), kb) + _dot_nt(dob, _bf(st)) * qd
            dkr = _dot_tn(_bf(dp), qb) + _dot_nt(vb, _bf(dst)) * kd
            dv = _dot_tn(_bf(p), dob) + _dot(_bf(kr * kd), _bf(dst))
            dstate[h] = dst * gt + _dot_tn(_bf(qr * qd), dob)
            d_ref[:, h * RET_DK:(h + 1) * RET_DK] = _bf(_rotate_bwd(dqr, cs, sn))
            d_ref[:, 512 + h * RET_DK:512 + (h + 1) * RET_DK] = _bf(_rotate_bwd(dkr, cs, sn) * (RET_DK ** -0.5))
            d_ref[:, 1024 + h * RET_DV:1024 + (h + 1) * RET_DV] = _bf(dv)
            d_ref[:, 2048 + h * RET_DV:2048 + (h + 1) * RET_DV] = _bf(dg)

    rev = lambda i, c: (i, nc - 1 - c, 0)
    tab = pl.BlockSpec((None, SEQ_T, RET_DK), rev)
    return _call(
        body, name="retention_bwd", grid=(b, nc),
        in_specs=[pl.BlockSpec((None, SEQ_T, 3072), rev), tab, tab,
                  pl.BlockSpec((1, D_MODEL), lambda i, c: (0, 0)),
                  pl.BlockSpec((None, None, RET_HEADS, RET_DK, RET_DV), lambda i, c: (i, nc - 1 - c, 0, 0, 0)),
                  pl.BlockSpec((None, SEQ_T, D_MODEL), rev)],
        out_specs=[pl.BlockSpec((None, SEQ_T, 3072), rev),
                   pl.BlockSpec((8, D_MODEL), lambda i, c: (0, 0))],
        out_shape=[jax.ShapeDtypeStruct((b, s, 3072), BF16), jax.ShapeDtypeStruct((8, D_MODEL), F32)],
        scratch=[pltpu.VMEM((RET_HEADS, RET_DK, RET_DV), F32),
                 pltpu.VMEM((RET_HEADS, SEQ_T, SEQ_T), F32)],
    )(proj3, cos, sin, gnw, states, da3)


def _softplus_neg(lam):
    z = -lam
    u = jnp.exp(-jnp.abs(z))
    log1p_u = jnp.where(u < 0.01, u * (1.0 - u * (0.5 - u * (1.0 / 3.0))), jnp.log(1.0 + u))
    return jnp.maximum(z, 0.0) + log1p_u


def _lru_gates(x, prev8, cw_ref, cb_ref, wr_ref, br_ref, wi_ref, bi_ref, lam_ref):
    xc = cb_ref[...] + sum(cw_ref[pl.ds(j, 1), :] * _shift_down(x, 3 - j, prev8) for j in range(4))
    rs, is_ = [], []
    for n in range(LRU_BLOCKS):
        xb = _bf(xc[:, n * LRU_BLOCK:(n + 1) * LRU_BLOCK])
        cols = slice(n * LRU_BLOCK, (n + 1) * LRU_BLOCK)
        rs.append(_sigmoid(_dot(xb, wr_ref[n]) + br_ref[:, cols]))
        is_.append(_sigmoid(_dot(xb, wi_ref[n]) + bi_ref[:, cols]))
    r = jnp.concatenate(rs, axis=1)
    i = jnp.concatenate(is_, axis=1)
    sp = _softplus_neg(lam_ref[...])
    la = -LRU_C * r * sp
    a = jnp.exp(la)
    s = jnp.sqrt(-jnp.tanh(la) * (a * a + 1.0))
    return xc, r, i, a, s, sp


_LRU_PARAM_SPECS = [
    pl.BlockSpec((4, D_MODEL), lambda i, c: (0, 0)),
    pl.BlockSpec((1, D_MODEL), lambda i, c: (0, 0)),
    pl.BlockSpec((LRU_BLOCKS, LRU_BLOCK, LRU_BLOCK), lambda i, c: (0, 0, 0)),
    pl.BlockSpec((1, D_MODEL), lambda i, c: (0, 0)),
    pl.BlockSpec((LRU_BLOCKS, LRU_BLOCK, LRU_BLOCK), lambda i, c: (0, 0, 0)),
    pl.BlockSpec((1, D_MODEL), lambda i, c: (0, 0)),
    pl.BlockSpec((1, D_MODEL), lambda i, c: (0, 0)),
]


def _lru_fwd(proj3, params):
    b, s, _ = proj3.shape
    nc = s // SEQ_T

    def body(x_ref, y_ref, cw, cb, wr, br, wi, bi, lam, o_ref, h_ref, xprev, hprev):
        @pl.when(pl.program_id(1) == 0)
        def _():
            xprev[...] = jnp.zeros_like(xprev)
            hprev[...] = jnp.zeros_like(hprev)

        x = x_ref[...]
        xc, r, i, a, s_, _ = _lru_gates(x, xprev[...], cw, cb, wr, br, wi, bi, lam)
        xprev[...] = x[SEQ_T - 8:]
        acum, bcum = _scan_fwd(a, s_ * (i * xc))
        h = acum * hprev[7:8, :] + bcum
        hprev[...] = h[SEQ_T - 8:]
        h_ref[...] = h
        o_ref[...] = _bf(h * _gelu(y_ref[...]))

    return _call(
        body, name="lru_fwd", grid=(b, nc),
        in_specs=[pl.BlockSpec((None, SEQ_T, D_MODEL), lambda i, c: (i, c, 3)),
                  pl.BlockSpec((None, SEQ_T, D_MODEL), lambda i, c: (i, c, 4))] + _LRU_PARAM_SPECS,
        out_specs=[pl.BlockSpec((None, SEQ_T, D_MODEL), lambda i, c: (i, c, 0))] * 2,
        out_shape=[jax.ShapeDtypeStruct((b, s, D_MODEL), BF16), jax.ShapeDtypeStruct((b, s, D_MODEL), F32)],
        scratch=[pltpu.VMEM((8, D_MODEL), F32), pltpu.VMEM((8, D_MODEL), F32)],
    )(proj3, proj3, *params)


def _lru_bwd(proj3, params, hseq, db3):
    b, s, _ = proj3.shape
    nc = s // SEQ_T
    blk8 = SEQ_T // 8

    def body(x_ref, y_ref, xp_ref, h_ref, hp_ref, db_ref, cw, cb, wr, br, wi, bi, lam,
             d_ref, dwr_ref, dwi_ref, sm_ref, gnext, anext, dxcnext):
        c = pl.program_id(1)
        first_chunk = c == nc - 1

        @pl.when(jnp.logical_and(pl.program_id(0) == 0, c == 0))
        def _():
            dwr_ref[...] = jnp.zeros_like(dwr_ref)
            dwi_ref[...] = jnp.zeros_like(dwi_ref)
            sm_ref[...] = jnp.zeros_like(sm_ref)

        @pl.when(c == 0)
        def _():
            gnext[...] = jnp.zeros_like(gnext)
            anext[...] = jnp.zeros_like(anext)
            dxcnext[...] = jnp.zeros_like(dxcnext)

        x, y, h = x_ref[...], y_ref[...], h_ref[...]
        keep_prev = jnp.where(first_chunk, 0.0, 1.0)
        xprev = xp_ref[...] * keep_prev
        hprev = hp_ref[...] * keep_prev
        xc, r, i, a, s_, sp = _lru_gates(x, xprev, cw, cb, wr, br, wi, bi, lam)
        db = db_ref[...].astype(F32)
        gy, dgy = _gelu_and_grad(y)
        dy = db * h * dgy
        a_up = _shift_up(a, 1, anext[...])
        acum, ucum = _scan_bwd(a_up, db * gy)
        g = ucum + acum * gnext[0:1, :]
        gnext[...] = g[0:8]
        anext[...] = a[0:8]
        da = g * _shift_down(h, 1, hprev)
        ixc = i * xc
        dla = da * a - (g * ixc) * (a * a) / s_
        di = g * s_ * xc
        dxc = g * s_ * i
        dzr = dla * (-LRU_C * sp) * r * (1.0 - r)
        dzi = di * i * (1.0 - i)
        lam_v = lam[...]
        _row_acc(sm_ref, 7, jnp.sum(dla * (LRU_C * r), axis=0, keepdims=True) * _sigmoid(-lam_v))
        _row_acc(sm_ref, 5, jnp.sum(dzr, axis=0, keepdims=True))
        _row_acc(sm_ref, 6, jnp.sum(dzi, axis=0, keepdims=True))
        parts = []
        for n in range(LRU_BLOCKS):
            cols = slice(n * LRU_BLOCK, (n + 1) * LRU_BLOCK)
            xb, zr, zi = _bf(xc[:, cols]), _bf(dzr[:, cols]), _bf(dzi[:, cols])
            parts.append(dxc[:, cols] + _dot_nt(zr, wr[n]) + _dot_nt(zi, wi[n]))
            dwr_ref[n] += _dot_tn(xb, zr)
            dwi_ref[n] += _dot_tn(xb, zi)
        dxc = jnp.concatenate(parts, axis=1)
        _row_acc(sm_ref, 4, jnp.sum(dxc, axis=0, keepdims=True))
        nxt = dxcnext[...]
        dx = jnp.zeros_like(x)
        for j in range(4):
            dx = dx + cw[pl.ds(j, 1), :] * _shift_up(dxc, 3 - j, nxt)
            _row_acc(sm_ref, j, jnp.sum(dxc * _shift_down(x, 3 - j, xprev), axis=0, keepdims=True))
        dxcnext[...] = dxc[0:8]
        d_ref[:, 0:D_MODEL] = _bf(dx)
        d_ref[:, D_MODEL:2 * D_MODEL] = _bf(dy)

    rev = lambda col: (lambda i, c: (i, nc - 1 - c, col))
    prev = lambda col: (lambda i, c: (i, jnp.maximum((nc - 1 - c) * blk8 - 1, 0), col))
    return _call(
        body, name="lru_bwd", grid=(b, nc),
        in_specs=[pl.BlockSpec((None, SEQ_T, D_MODEL), rev(3)),
                  pl.BlockSpec((None, SEQ_T, D_MODEL), rev(4)),
                  pl.BlockSpec((None, 8, D_MODEL), prev(3)),
                  pl.BlockSpec((None, SEQ_T, D_MODEL), rev(0)),
                  pl.BlockSpec((None, 8, D_MODEL), prev(0)),
                  pl.BlockSpec((None, SEQ_T, D_MODEL), rev(0))] + _LRU_PARAM_SPECS,
        out_specs=[pl.BlockSpec((None, SEQ_T, 2 * D_MODEL), rev(0)),
                   pl.BlockSpec((LRU_BLOCKS, LRU_BLOCK, LRU_BLOCK), lambda i, c: (0, 0, 0)),
                   pl.BlockSpec((LRU_BLOCKS, LRU_BLOCK, LRU_BLOCK), lambda i, c: (0, 0, 0)),
                   pl.BlockSpec((8, D_MODEL), lambda i, c: (0, 0))],
        out_shape=[jax.ShapeDtypeStruct((b, s, 2 * D_MODEL), BF16),
                   jax.ShapeDtypeStruct((LRU_BLOCKS, LRU_BLOCK, LRU_BLOCK), F32),
                   jax.ShapeDtypeStruct((LRU_BLOCKS, LRU_BLOCK, LRU_BLOCK), F32),
                   jax.ShapeDtypeStruct((8, D_MODEL), F32)],
        scratch=[pltpu.VMEM((8, D_MODEL), F32)] * 3,
    )(proj3, proj3, proj3, hseq, hseq, db3, *params)


def _merge_parts(a_ref, b_ref, gr_ref, gl_ref, mgb_ref, wro_ref, wlo_ref):
    ya = _dot(a_ref[...], wro_ref[...])
    yb = _dot(b_ref[...], wlo_ref[...])
    sa = _sigmoid(gr_ref[...] + mgb_ref[0:1, :])
    sb = _sigmoid(gl_ref[...] + mgb_ref[1:2, :])
    return ya, yb, sa, sb


def _merge_specs(tm):
    row = lambda col: pl.BlockSpec((tm, D_MODEL), lambda i: (i, col))
    full = pl.BlockSpec((D_MODEL, D_MODEL), lambda i: (0, 0))
    return row, full


def _merge_fwd(a_in, b_in, proj, mgb, wro, wlo, wout, x, *, tm):
    m = x.shape[0]
    row, full = _merge_specs(tm)

    def body(a_ref, b_ref, gr_ref, gl_ref, mgb_ref, wro_ref, wlo_ref, wout_ref, x_ref, o_ref):
        ya, yb, sa, sb = _merge_parts(a_ref, b_ref, gr_ref, gl_ref, mgb_ref, wro_ref, wlo_ref)
        o_ref[...] = x_ref[...] + _dot(_bf(sa * ya + sb * yb), wout_ref[...])

    return _call(
        body, name="merge_fwd", grid=(m // tm,),
        in_specs=[row(0), row(0), row(5), row(6), pl.BlockSpec((2, D_MODEL), lambda i: (0, 0)),
                  full, full, full, row(0)],
        out_specs=row(0),
        out_shape=jax.ShapeDtypeStruct((m, D_MODEL), F32),
    )(a_in, b_in, proj, proj, mgb, wro, wlo, wout, x)


def _merge_bwd(a_in, b_in, proj, mgb, wro, wlo, wout, dx2, *, tm):
    m = dx2.shape[0]
    row, full = _merge_specs(tm)

    def body(a_ref, b_ref, gr_ref, gl_ref, mgb_ref, wro_ref, wlo_ref, wout_ref, dx_ref,
             mix_ref, dya_ref, dyb_ref, da_ref, db_ref, dg_ref, sm_ref):
        @pl.when(pl.program_id(0) == 0)
        def _():
            sm_ref[...] = jnp.zeros_like(sm_ref)

        ya, yb, sa, sb = _merge_parts(a_ref, b_ref, gr_ref, gl_ref, mgb_ref, wro_ref, wlo_ref)
        mix_ref[...] = _bf(sa * ya + sb * yb)
        dmix = _dot_nt(_bf(dx_ref[...]), wout_ref[...])
        dya, dyb = _bf(dmix * sa), _bf(dmix * sb)
        dya_ref[...] = dya
        dyb_ref[...] = dyb
        dga = dmix * ya * sa * (1.0 - sa)
        dgb = dmix * yb * sb * (1.0 - sb)
        dg_ref[:, 0:D_MODEL] = _bf(dga)
        dg_ref[:, D_MODEL:2 * D_MODEL] = _bf(dgb)
        _row_acc(sm_ref, 0, jnp.sum(dga, axis=0, keepdims=True))
        _row_acc(sm_ref, 1, jnp.sum(dgb, axis=0, keepdims=True))
        da_ref[...] = _bf(_dot_nt(dya, wro_ref[...]))
        db_ref[...] = _bf(_dot_nt(dyb, wlo_ref[...]))

    act = jax.ShapeDtypeStruct((m, D_MODEL), BF16)
    return _call(
        body, name="merge_bwd", grid=(m // tm,),
        in_specs=[row(0), row(0), row(5), row(6), pl.BlockSpec((2, D_MODEL), lambda i: (0, 0)),
                  full, full, full, row(0)],
        out_specs=[row(0)] * 5 + [pl.BlockSpec((tm, 2 * D_MODEL), lambda i: (i, 0)),
                                  pl.BlockSpec((8, D_MODEL), lambda i: (0, 0))],
        out_shape=[act] * 5 + [jax.ShapeDtypeStruct((m, 2 * D_MODEL), BF16),
                               jax.ShapeDtypeStruct((8, D_MODEL), F32)],
    )(a_in, b_in, proj, proj, mgb, wro, wlo, wout, dx2)


def _ffn_conv(gate, prev8, cw_ref, cb_ref):
    return cb_ref[...] + sum(cw_ref[pl.ds(j, 1), :] * _shift_down(gate, 2 - j, prev8) for j in range(3))


def _ffn_act_fwd(up3, cw, cb):
    b, s, _ = up3.shape

    def body(g_ref, v_ref, cw_ref, cb_ref, o_ref, gprev):
        @pl.when(pl.program_id(1) == 0)
        def _():
            gprev[...] = jnp.zeros_like(gprev)

        gate = g_ref[...]
        gc = _ffn_conv(gate, gprev[...], cw_ref, cb_ref)
        gprev[...] = gate[SEQ_T - 8:]
        o_ref[...] = _bf(_gelu(gc) * v_ref[...])

    return _call(
        body, name="ffn_act_fwd", grid=(b, s // SEQ_T),
        in_specs=[pl.BlockSpec((None, SEQ_T, D_FF), lambda i, c: (i, c, 0)),
                  pl.BlockSpec((None, SEQ_T, D_FF), lambda i, c: (i, c, 1)),
                  pl.BlockSpec((3, D_FF), lambda i, c: (0, 0)),
                  pl.BlockSpec((1, D_FF), lambda i, c: (0, 0))],
        out_specs=pl.BlockSpec((None, SEQ_T, D_FF), lambda i, c: (i, c, 0)),
        out_shape=jax.ShapeDtypeStruct((b, s, D_FF), BF16),
        scratch=[pltpu.VMEM((8, D_FF), F32)],
    )(up3, up3, cw, cb)


def _ffn_act_bwd(up3, cw, cb, df3):
    b, s, _ = up3.shape
    nc = s // SEQ_T
    blk8 = SEQ_T // 8

    def body(g_ref, v_ref, gp_ref, df_ref, cw_ref, cb_ref, dg_ref, dv_ref, sm_ref, dgcnext):
        c = pl.program_id(1)

        @pl.when(jnp.logical_and(pl.program_id(0) == 0, c == 0))
        def _():
            sm_ref[...] = jnp.zeros_like(sm_ref)

        @pl.when(c == 0)
        def _():
            dgcnext[...] = jnp.zeros_like(dgcnext)

        gate = g_ref[...]
        gprev = gp_ref[...] * jnp.where(c == nc - 1, 0.0, 1.0)
        gc = _ffn_conv(gate, gprev, cw_ref, cb_ref)
        act, dact = _gelu_and_grad(gc)
        df = df_ref[...].astype(F32)
        dv_ref[...] = _bf(df * act)
        dgc = df * v_ref[...] * dact
        nxt = dgcnext[...]
        dgate = jnp.zeros_like(gate)
        for j in range(3):
            dgate = dgate + cw_ref[pl.ds(j, 1), :] * _shift_up(dgc, 2 - j, nxt)
            _row_acc(sm_ref, j, jnp.sum(dgc * _shift_down(gate, 2 - j, gprev), axis=0, keepdims=True))
        _row_acc(sm_ref, 3, jnp.sum(dgc, axis=0, keepdims=True))
        dgcnext[...] = dgc[0:8]
        dg_ref[...] = _bf(dgate)

    rev = lambda col: (lambda i, c: (i, nc - 1 - c, col))
    return _call(
        body, name="ffn_act_bwd", grid=(b, nc),
        in_specs=[pl.BlockSpec((None, SEQ_T, D_FF), rev(0)),
                  pl.BlockSpec((None, SEQ_T, D_FF), rev(1)),
                  pl.BlockSpec((None, 8, D_FF), lambda i, c: (i, jnp.maximum((nc - 1 - c) * blk8 - 1, 0), 0)),
                  pl.BlockSpec((None, SEQ_T, D_FF), rev(0)),
                  pl.BlockSpec((3, D_FF), lambda i, c: (0, 0)),
                  pl.BlockSpec((1, D_FF), lambda i, c: (0, 0))],
        out_specs=[pl.BlockSpec((None, SEQ_T, D_FF), rev(0))] * 2
        + [pl.BlockSpec((8, D_FF), lambda i, c: (0, 0))],
        out_shape=[jax.ShapeDtypeStruct((b, s, D_FF), BF16)] * 2 + [jax.ShapeDtypeStruct((8, D_FF), F32)],
        scratch=[pltpu.VMEM((8, D_FF), F32)],
    )(up3, up3, up3, df3, cw, cb)


def _ffn_down_loss(f, wd, x2, nfw, target, *, tm):
    m, kf = f.shape
    nt = m // tm

    def body(f_ref, wd_ref, x_ref, nw_ref, t_ref, loss_ref, dx_ref, dnw_ref, lsum):
        i = pl.program_id(0)

        @pl.when(i == 0)
        def _():
            dnw_ref[...] = jnp.zeros_like(dnw_ref)
            lsum[...] = jnp.zeros_like(lsum)

        x3 = x_ref[...] + _dot(f_ref[...], wd_ref[...])
        nw = nw_ref[...]
        xh, _ = _rms(x3)
        err = xh * nw - t_ref[...]
        lsum[...] += jnp.sum(err * err, axis=0, keepdims=True)
        dx, dnw = _rms_bwd(err * (1.0 / D_MODEL), x3, nw)
        dx_ref[...] = dx
        _row_acc(dnw_ref, 0, dnw)

        @pl.when(i == nt - 1)
        def _():
            loss_ref[...] = jnp.sum(lsum[...], axis=1, keepdims=True) * (0.5 / D_MODEL)

    row = pl.BlockSpec((tm, D_MODEL), lambda i: (i, 0))
    return _call(
        body, name="ffn_down_loss", grid=(nt,),
        in_specs=[pl.BlockSpec((tm, kf), lambda i: (i, 0)),
                  pl.BlockSpec((kf, D_MODEL), lambda i: (0, 0)),
                  row, pl.BlockSpec((1, D_MODEL), lambda i: (0, 0)), row],
        out_specs=[pl.BlockSpec((1, 1), lambda i: (0, 0)), row,
                   pl.BlockSpec((8, D_MODEL), lambda i: (0, 0))],
        out_shape=[jax.ShapeDtypeStruct((1, 1), F32), jax.ShapeDtypeStruct((m, D_MODEL), F32),
                   jax.ShapeDtypeStruct((8, D_MODEL), F32)],
        scratch=[pltpu.VMEM((1, D_MODEL), F32)],
    )(f, wd, x2, nfw, target)


def _adamw(w, gs, m, v, *, name):
    rows, cols = w.shape
    tr = rows if rows <= 256 else 256
    ng = len(gs)

    def body(w_ref, *rest):
        g_refs, (m_ref, v_ref, g_out, d_out, m_out, v_out) = rest[:ng], rest[ng:]
        g = g_refs[0][...]
        for r in g_refs[1:]:
            g = g + r[...]
        mn = ADAM_B1 * m_ref[...] + (1.0 - ADAM_B1) * g
        vn = ADAM_B2 * v_ref[...] + (1.0 - ADAM_B2) * (g * g)
        m_hat = mn / (1.0 - ADAM_B1 ** ADAM_STEP)
        v_hat = vn / (1.0 - ADAM_B2 ** ADAM_STEP)
        g_out[...] = g
        d_out[...] = -ADAM_LR * (m_hat / (jnp.sqrt(v_hat) + ADAM_EPS) + ADAM_WD * w_ref[...])
        m_out[...] = mn
        v_out[...] = vn

    spec = pl.BlockSpec((tr, cols), lambda i: (i, 0))
    return _call(
        body, name=name, grid=(rows // tr,),
        in_specs=[spec] * (3 + ng), out_specs=[spec] * 4,
        out_shape=[jax.ShapeDtypeStruct((rows, cols), F32)] * 4,
    )(w, *gs, m, v)


def _mesh_pos():
    x, y, c = lax.axis_index("x"), lax.axis_index("y"), lax.axis_index("c")
    return x, y, c


def _other_chips(x, y, c):
    return [((1 - x, y, c), 2 * (1 - x) + y), ((x, 1 - y, c), 2 * x + 1 - y),
            ((1 - x, 1 - y, c), 2 * (1 - x) + 1 - y)]


def _shard_of(ref, axis, size, chip):
    idx = [slice(None)] * len(ref.shape)
    idx[axis] = pl.ds(pl.multiple_of(chip * size, size), size)
    return ref.at[tuple(idx)]


def _gather_weights(shards, axes, small):
    nt = len(shards)
    full_shapes = []
    for sh, ax in zip(shards, axes):
        shape = list(sh.shape)
        shape[ax] *= N_CHIPS
        full_shapes.append(jax.ShapeDtypeStruct(tuple(shape), sh.dtype))
    full_shapes.append(jax.ShapeDtypeStruct((N_CHIPS,) + small.shape, small.dtype))

    def body(*refs):
        srcs, dsts = refs[:nt + 1], refs[nt + 1:2 * nt + 2]
        send_sems, recv_sems, local_sems = refs[2 * nt + 2:]
        x, y, c = _mesh_pos()
        mine = 2 * x + y
        peers = _other_chips(x, y, c)

        def slot(t, chip):
            if t == nt:
                return dsts[t].at[chip]
            return _shard_of(dsts[t], axes[t], shards[t].shape[axes[t]], chip)

        started = []
        for t in range(nt + 1):
            loc = pltpu.make_async_copy(srcs[t], slot(t, mine), local_sems.at[t])
            loc.start()
            started.append((loc, True))
            for k, (dev, _) in enumerate(peers):
                cp = pltpu.make_async_remote_copy(
                    src_ref=srcs[t], dst_ref=slot(t, mine), send_sem=send_sems.at[3 * t + k],
                    recv_sem=recv_sems.at[3 * t + k], device_id=dev, device_id_type=MESH)
                cp.start()
                started.append((cp, False))
        for t in range(nt + 1):
            for k, (dev, chip) in enumerate(peers):
                pltpu.make_async_remote_copy(
                    src_ref=srcs[t], dst_ref=slot(t, chip), send_sem=send_sems.at[3 * t + k],
                    recv_sem=recv_sems.at[3 * t + k], device_id=dev, device_id_type=MESH).wait_recv()
        for cp, local in started:
            if local:
                cp.wait()
            else:
                cp.wait_send()

    any_spec = pl.BlockSpec(memory_space=pl.ANY)
    return _call(
        body, name="gather_weights",
        in_specs=[any_spec] * (nt + 1), out_specs=[any_spec] * (nt + 1), out_shape=full_shapes,
        scratch=[pltpu.SemaphoreType.DMA((3 * (nt + 1),)), pltpu.SemaphoreType.DMA((3 * (nt + 1),)),
                 pltpu.SemaphoreType.DMA((nt + 1,))],
    )(*shards, small)


def _exchange_grads(grads, axes):
    nt = len(grads)
    sizes = [g.shape[ax] // N_CHIPS for g, ax in zip(grads, axes)]
    out_shapes = []
    for g, ax, sz in zip(grads, axes, sizes):
        shape = list(g.shape)
        shape[ax] = sz
        out_shapes.append(jax.ShapeDtypeStruct((N_CHIPS,) + tuple(shape), g.dtype))

    def body(*refs):
        srcs, dsts = refs[:nt], refs[nt:2 * nt]
        send_sems, recv_sems, local_sems = refs[2 * nt:]
        x, y, c = _mesh_pos()
        mine = 2 * x + y
        peers = _other_chips(x, y, c)
        started = []
        for t in range(nt):
            loc = pltpu.make_async_copy(_shard_of(srcs[t], axes[t], sizes[t], mine), dsts[t].at[mine],
                                        local_sems.at[t])
            loc.start()
            started.append((loc, True))
            for k, (dev, chip) in enumerate(peers):
                cp = pltpu.make_async_remote_copy(
                    src_ref=_shard_of(srcs[t], axes[t], sizes[t], chip), dst_ref=dsts[t].at[mine],
                    send_sem=send_sems.at[3 * t + k], recv_sem=recv_sems.at[3 * t + k],
                    device_id=dev, device_id_type=MESH)
                cp.start()
                started.append((cp, False))
        for t in range(nt):
            for k, (dev, chip) in enumerate(peers):
                pltpu.make_async_remote_copy(
                    src_ref=_shard_of(srcs[t], axes[t], sizes[t], chip), dst_ref=dsts[t].at[chip],
                    send_sem=send_sems.at[3 * t + k], recv_sem=recv_sems.at[3 * t + k],
                    device_id=dev, device_id_type=MESH).wait_recv()
        for cp, local in started:
            if local:
                cp.wait()
            else:
                cp.wait_send()

    any_spec = pl.BlockSpec(memory_space=pl.ANY)
    return _call(
        body, name="exchange_grads",
        in_specs=[any_spec] * nt, out_specs=[any_spec] * nt, out_shape=out_shapes,
        scratch=[pltpu.SemaphoreType.DMA((3 * nt,)), pltpu.SemaphoreType.DMA((3 * nt,)),
                 pltpu.SemaphoreType.DMA((nt,))],
    )(*grads)


def _sum_chips(parts, *, name):
    _, rows, cols = parts.shape
    tr = rows if rows <= 256 else 256

    def body(p_ref, o_ref):
        acc = p_ref[0].astype(F32)
        for q in range(1, N_CHIPS):
            acc = acc + p_ref[q].astype(F32)
        o_ref[...] = acc

    return _call(
        body, name=name, grid=(rows // tr,),
        in_specs=[pl.BlockSpec((N_CHIPS, tr, cols), lambda i: (0, i, 0))],
        out_specs=pl.BlockSpec((tr, cols), lambda i: (i, 0)),
        out_shape=jax.ShapeDtypeStruct((rows, cols), F32),
    )(parts)


def _swap_with_sibling(sums):
    nt = len(sums)

    def body(*refs):
        srcs, dsts = refs[:nt], refs[nt:2 * nt]
        send_sems, recv_sems = refs[2 * nt:]
        x, y, c = _mesh_pos()
        copies = [pltpu.make_async_remote_copy(
            src_ref=srcs[t], dst_ref=dsts[t], send_sem=send_sems.at[t], recv_sem=recv_sems.at[t],
            device_id=(x, y, 1 - c), device_id_type=MESH) for t in range(nt)]
        for cp in copies:
            cp.start()
        for cp in copies:
            cp.wait_recv()
        for cp in copies:
            cp.wait_send()

    any_spec = pl.BlockSpec(memory_space=pl.ANY)
    return _call(
        body, name="swap_with_sibling",
        in_specs=[any_spec] * nt, out_specs=[any_spec] * nt,
        out_shape=[jax.ShapeDtypeStruct(s.shape, s.dtype) for s in sums],
        scratch=[pltpu.SemaphoreType.DMA((nt,)), pltpu.SemaphoreType.DMA((nt,))],
    )(*sums)


def _allreduce_small(pack):
    rows, cols = pack.shape

    def body(p_ref, o_ref, slots, send_sems, recv_sems):
        x, y, c = _mesh_pos()
        me = 4 * x + 2 * y + c
        slots[me] = p_ref[...]
        copies = []
        for r in range(1, N_DEV):
            fx, fy, fc = (r >> 2) & 1, (r >> 1) & 1, r & 1
            dev = ((1 - x) if fx else x, (1 - y) if fy else y, (1 - c) if fc else c)
            cp = pltpu.make_async_remote_copy(
                src_ref=p_ref, dst_ref=slots.at[me], send_sem=send_sems.at[r - 1],
                recv_sem=recv_sems.at[r - 1], device_id=dev, device_id_type=MESH)
            cp.start()
            copies.append(cp)
        for cp in copies:
            cp.wait_recv()
        for cp in copies:
            cp.wait_send()
        acc = slots[0]
        for d in range(1, N_DEV):
            acc = acc + slots[d]
        o_ref[...] = acc

    vmem = pl.BlockSpec(memory_space=pltpu.VMEM)
    return _call(
        body, name="allreduce_small", in_specs=[vmem], out_specs=vmem,
        out_shape=jax.ShapeDtypeStruct((rows, cols), F32),
        scratch=[pltpu.VMEM((N_DEV, rows, cols), F32), pltpu.SemaphoreType.DMA((N_DEV - 1,)),
                 pltpu.SemaphoreType.DMA((N_DEV - 1,))],
    )(pack)


def _pad_rows(a, rows=8):
    return jnp.pad(a, ((0, rows - a.shape[0]), (0, 0)))


def kernel(x, positions, norm1_w, w_in, merge_gate_b, ret_gn_w, w_ret_o, lru_conv_w, lru_conv_b, lru_w_r, lru_b_r, lru_w_i, lru_b_i, lru_lambda, w_lru_o, w_out, norm2_w, ffn_w_up, ffn_conv_w, ffn_conv_b, ffn_w_down, norm_f_w, loss_target, m_norm1_w, m_w_in, m_merge_gate_b, m_ret_gn_w, m_w_ret_o, m_lru_conv_w, m_lru_conv_b, m_lru_w_r, m_lru_b_r, m_lru_w_i, m_lru_b_i, m_lru_lambda, m_w_lru_o, m_w_out, m_norm2_w, m_ffn_w_up, m_ffn_conv_w, m_ffn_conv_b, m_ffn_w_down, m_norm_f_w, v_norm1_w, v_w_in, v_merge_gate_b, v_ret_gn_w, v_w_ret_o, v_lru_conv_w, v_lru_conv_b, v_lru_w_r, v_lru_b_r, v_lru_w_i, v_lru_b_i, v_lru_lambda, v_w_lru_o, v_w_out, v_norm2_w, v_ffn_w_up, v_ffn_conv_w, v_ffn_conv_b, v_ffn_w_down, v_norm_f_w):
    names = ["norm1_w", "w_in", "merge_gate_b", "ret_gn_w", "w_ret_o", "lru_conv_w", "lru_conv_b", "lru_w_r",
             "lru_b_r", "lru_w_i", "lru_b_i", "lru_lambda", "w_lru_o", "w_out", "norm2_w", "ffn_w_up",
             "ffn_conv_w", "ffn_conv_b", "ffn_w_down", "norm_f_w"]
    w_args = dict(zip(names, [norm1_w, w_in, merge_gate_b, ret_gn_w, w_ret_o, lru_conv_w, lru_conv_b, lru_w_r,
                              lru_b_r, lru_w_i, lru_b_i, lru_lambda, w_lru_o, w_out, norm2_w, ffn_w_up,
                              ffn_conv_w, ffn_conv_b, ffn_w_down, norm_f_w]))
    m_args = dict(zip(names, [m_norm1_w, m_w_in, m_merge_gate_b, m_ret_gn_w, m_w_ret_o, m_lru_conv_w,
                              m_lru_conv_b, m_lru_w_r, m_lru_b_r, m_lru_w_i, m_lru_b_i, m_lru_lambda, m_w_lru_o,
                              m_w_out, m_norm2_w, m_ffn_w_up, m_ffn_conv_w, m_ffn_conv_b, m_ffn_w_down,
                              m_norm_f_w]))
    v_args = dict(zip(names, [v_norm1_w, v_w_in, v_merge_gate_b, v_ret_gn_w, v_w_ret_o, v_lru_conv_w,
                              v_lru_conv_b, v_lru_w_r, v_lru_b_r, v_lru_w_i, v_lru_b_i, v_lru_lambda, v_w_lru_o,
                              v_w_out, v_norm2_w, v_ffn_w_up, v_ffn_conv_w, v_ffn_conv_b, v_ffn_w_down,
                              v_norm_f_w]))

    bsz, seq, d = x.shape
    m = bsz * seq
    tm = min(512, m)
    chip = 2 * lax.axis_index("x") + lax.axis_index("y")

    big = ["w_in", "w_ret_o", "w_lru_o", "w_out", "lru_w_r", "lru_w_i", "ffn_w_up", "ffn_w_down"]
    big_axis = dict(w_in=1, w_ret_o=0, w_lru_o=0, w_out=0, lru_w_r=1, lru_w_i=1, ffn_w_up=1, ffn_w_down=0)
    shards = [w_args[n][0].astype(BF16) for n in big]
    small_pack = jnp.concatenate([
        jnp.pad(merge_gate_b[0], ((0, 6), (0, 512))),
        jnp.pad(lru_conv_w[0], ((0, 4), (0, 512))),
        jnp.pad(lru_b_r[0], ((0, 4), (0, 704))),
        jnp.pad(lru_b_i[0], ((0, 4), (0, 704))),
        jnp.pad(ffn_conv_w[0], ((0, 5), (0, 0))),
    ], axis=0)
    gathered = _gather_weights(shards, [big_axis[n] for n in big], small_pack)
    wb = dict(zip(big, gathered[:-1]))
    sp = gathered[-1]
    mgb = jnp.transpose(sp[:, 0:2, 0:256], (1, 0, 2)).reshape(2, D_MODEL)
    lcw = jnp.transpose(sp[:, 8:12, 0:256], (1, 0, 2)).reshape(4, D_MODEL)
    lbr = jnp.transpose(sp[:, 16:20, 0:64], (1, 0, 2)).reshape(1, D_MODEL)
    lbi = jnp.transpose(sp[:, 24:28, 0:64], (1, 0, 2)).reshape(1, D_MODEL)
    fcw = jnp.transpose(sp[:, 32:35, :], (1, 0, 2)).reshape(3, D_FF)
    lru_params = (lcw, lru_conv_b, wb["lru_w_r"], lbr, wb["lru_w_i"], lbi, lru_lambda)
    nfw = norm_f_w.reshape(1, D_MODEL)

    x2d = x.reshape(m, d)
    half = RET_DK // 2
    inv_freq = ROPE_BASE ** (-jnp.arange(half, dtype=F32) / half)
    cos, sin = _rope_tables(positions.reshape(bsz, seq, 1), jnp.concatenate([inv_freq, inv_freq]).reshape(1, RET_DK))
    proj, h1 = _norm_matmul(x2d, norm1_w, wb["w_in"], name="in_proj", tm=tm, tn=512)
    proj3 = proj.reshape(bsz, seq, D_IN)
    a_in3, states = _retention_fwd(proj3, cos, sin, ret_gn_w)
    b_in3, hseq = _lru_fwd(proj3, lru_params)
    a_in, b_in = a_in3.reshape(m, d), b_in3.reshape(m, d)
    x2 = _merge_fwd(a_in, b_in, proj, mgb, wb["w_ret_o"], wb["w_lru_o"], wb["w_out"], x2d, tm=tm)
    up, h2 = _norm_matmul(x2, norm2_w, wb["ffn_w_up"], name="ffn_up", tm=tm, tn=512)
    up3 = up.reshape(bsz, seq, 2 * D_FF)
    f3 = _ffn_act_fwd(up3, fcw, ffn_conv_b)
    f = f3.reshape(m, D_FF)
    loss_dev, dx3, sm_nf = _ffn_down_loss(f, wb["ffn_w_down"], x2, nfw, loss_target.reshape(m, d), tm=tm)
    loss = lax.psum(loss_dev[0, 0], ("x", "y", "c"))

    g_full = {}
    df = _mm_nt(dx3, wb["ffn_w_down"], name="ffn_down_dx", tm=tm, out_dtype=BF16)
    g_full["ffn_w_down"] = _mm_tn(f, [dx3], name="ffn_down_dw", tm=tm)
    dgate3, dval3, sm_ffn = _ffn_act_bwd(up3, fcw, ffn_conv_b, df.reshape(bsz, seq, D_FF))
    dup = [dgate3.reshape(m, D_FF), dval3.reshape(m, D_FF)]
    dx2, sm_n2 = _mm_nt_normbwd(dup, wb["ffn_w_up"], x2, norm2_w, dx3, name="ffn_up_dx", tm=tm)
    g_full["ffn_w_up"] = _mm_tn(h2, dup, name="ffn_up_dw", tm=tm)
    mix, dya, dyb, da_in, db_in, dgates, sm_mg = _merge_bwd(
        a_in, b_in, proj, mgb, wb["w_ret_o"], wb["w_lru_o"], wb["w_out"], dx2, tm=tm)
    g_full["w_out"] = _mm_tn(mix, [dx2], name="out_dw", tm=tm)
    g_full["w_ret_o"] = _mm_tn(a_in, [dya], name="ret_o_dw", tm=tm)
    g_full["w_lru_o"] = _mm_tn(b_in, [dyb], name="lru_o_dw", tm=tm)
    dlru3, dwr, dwi, sm_lru = _lru_bwd(proj3, lru_params, hseq, db_in.reshape(bsz, seq, d))
    g_full["lru_w_r"], g_full["lru_w_i"] = dwr.astype(BF16), dwi.astype(BF16)
    dret3, sm_gn = _retention_bwd(proj3, cos, sin, ret_gn_w, states, da_in.reshape(bsz, seq, d))
    dproj = [dret3.reshape(m, 3072), dlru3.reshape(m, 2048), dgates]
    grad_x, sm_n1 = _mm_nt_normbwd(dproj, wb["w_in"], x2d, norm1_w, dx2, name="in_proj_dx", tm=tm)
    g_full["w_in"] = _mm_tn(h1, dproj, name="in_proj_dw", tm=tm)

    parts = _exchange_grads([g_full[n] for n in big], [big_axis[n] for n in big])
    sums = []
    for n, p in zip(big, parts):
        if p.ndim == 4:
            p = p.reshape(N_CHIPS, -1, p.shape[-1])
        sums.append(_sum_chips(p, name="sum_" + n))
    sib = _swap_with_sibling(sums)
    pack = jnp.concatenate(
        [sm_n1, sm_mg, sm_gn, sm_lru, sm_n2, sm_ffn[:, 0:1024], sm_ffn[:, 1024:2048], sm_ffn[:, 2048:3072], sm_nf],
        axis=0)
    tot = _allreduce_small(pack)
    ffn_sm = jnp.concatenate([tot[40:48], tot[48:56], tot[56:64]], axis=1)
    g_small = {
        "norm1_w": tot[0:1], "merge_gate_b": tot[8:10], "ret_gn_w": tot[16:17], "lru_conv_w": tot[24:28],
        "lru_conv_b": tot[28:29], "lru_b_r": tot[29:30].reshape(4, 256), "lru_b_i": tot[30:31].reshape(4, 256),
        "lru_lambda": tot[31:32], "norm2_w": tot[32:33], "ffn_conv_w": ffn_sm[0:3], "ffn_conv_b": ffn_sm[3:4],
        "norm_f_w": tot[64:65],
    }
    small_shard = dict(merge_gate_b=256, lru_conv_w=256, lru_b_r=64, lru_b_i=64, ffn_conv_w=768)

    outs = {}
    for n, s_own, s_sib in zip(big, sums, sib):
        shape = w_args[n].shape
        w2 = w_args[n].reshape(s_own.shape)
        outs[n] = [o.reshape(shape) for o in _adamw(
            w2, [s_own, s_sib], m_args[n].reshape(s_own.shape), v_args[n].reshape(s_own.shape), name="adamw_" + n)]
    for n, g in g_small.items():
        shape = w_args[n].shape
        if n in small_shard:
            g = lax.dynamic_slice_in_dim(g, chip * small_shard[n], small_shard[n], axis=1)
        w2 = w_args[n].reshape(g.shape)
        outs[n] = [o.reshape(shape) for o in _adamw(
            w2, [g], m_args[n].reshape(g.shape), v_args[n].reshape(g.shape), name="adamw_" + n)]

    result = [loss, grad_x.reshape(bsz, seq, d)]
    for k in range(4):
        result += [outs[n][k] for n in names]
    return tuple(result)
```

```python
import functools
import math

import numpy as np
import jax
import jax.numpy as jnp
from jax import lax
from jax.experimental import pallas as pl
from jax.experimental.pallas import tpu as pltpu

F32 = jnp.float32
BF16 = jnp.bfloat16

D_MODEL = 1024
RET_HEADS = 4
RET_DK = 128
RET_DV = 256
LRU_BLOCKS = 4
LRU_BLOCK = 256
LRU_C = 8.0
D_FF = 3072
D_IN = 7168
ROPE_BASE = 10000.0
RMS_EPS = 1e-6
GN_EPS = 1e-6
ADAM_LR, ADAM_B1, ADAM_B2, ADAM_EPS, ADAM_WD, ADAM_STEP = 0.001, 0.9, 0.999, 1e-08, 0.01, 10

N_CHIPS = 4
N_DEV = 8
SEQ_T = 256
REF_CHUNK = 64
COL = 1024
MM_ROWS = 1024
MM_COLS = 1024
FUSED_ROWS = 512
VMEM_LIMIT_BYTES = 56 * 1024 * 1024
MESH = pl.DeviceIdType.MESH
GELU_K = math.sqrt(2.0 / math.pi)
GELU_C = 0.044715


class _Comm:
    def __init__(self, ins, outs, sems, start, finish, aliases=None):
        self.ins, self.outs, self.sems = list(ins), list(outs), list(sems)
        self.start, self.finish, self.aliases = start, finish, dict(aliases or {})


def _call(body, *, name, out_shape=(), grid=None, in_specs=(), out_specs=(), scratch=(), comm=None):
    single = not isinstance(out_shape, (list, tuple))
    out_shape = [out_shape] if single else list(out_shape)
    out_specs = [out_specs] if single else list(out_specs)
    in_specs, scratch = list(in_specs), list(scratch)
    n_in, n_out, n_scr = len(in_specs), len(out_shape), len(scratch)
    kwargs = dict(name=name, compiler_params=pltpu.CompilerParams(vmem_limit_bytes=VMEM_LIMIT_BYTES))
    if grid is not None:
        kwargs["grid"] = grid
    if comm is None:
        fn = pl.pallas_call(body, out_shape=out_shape, in_specs=in_specs, out_specs=out_specs,
                            scratch_shapes=scratch, **kwargs)
        return (lambda *args: fn(*args)[0]) if single else fn

    any_spec = pl.BlockSpec(memory_space=pl.ANY)
    n_cin, n_cout = len(comm.ins), len(comm.outs)

    def wrapped(*refs):
        ins, refs = refs[:n_in], refs[n_in:]
        cins, refs = refs[:n_cin], refs[n_cin:]
        outs, refs = refs[:n_out], refs[n_out:]
        couts, refs = refs[:n_cout], refs[n_cout:]
        scr, csems = refs[:n_scr], refs[n_scr:]
        if grid is None:
            comm.start(cins, couts, csems)
            comm.finish(cins, couts, csems)
            return
        ids = [pl.program_id(a) for a in range(len(grid))]
        first = functools.reduce(jnp.logical_and, [i == 0 for i in ids])
        last = functools.reduce(jnp.logical_and, [i == g - 1 for i, g in zip(ids, grid)])
        pl.when(first)(lambda: comm.start(cins, couts, csems))
        body(*ins, *outs, *scr)
        pl.when(last)(lambda: comm.finish(cins, couts, csems))

    fn = pl.pallas_call(
        wrapped, out_shape=out_shape + comm.outs, in_specs=in_specs + [any_spec] * n_cin,
        out_specs=out_specs + [any_spec] * n_cout, scratch_shapes=scratch + comm.sems,
        input_output_aliases={n_in + i: n_out + o for i, o in comm.aliases.items()}, **kwargs)

    def run(*args):
        res = fn(*args, *comm.ins)
        own = res[0] if single else list(res[:n_out])
        return own, list(res[n_out:])

    return run


def _dot(a, b):
    return jnp.dot(a, b, preferred_element_type=F32)


def _dot_nt(a, b):
    return lax.dot_general(a, b, (((1,), (1,)), ((), ())), preferred_element_type=F32)


def _dot_tn(a, b):
    return lax.dot_general(a, b, (((0,), (0,)), ((), ())), preferred_element_type=F32)


def _bf(x):
    return x.astype(BF16)


def _sigmoid(x):
    return 1.0 / (1.0 + jnp.exp(-x))


def _gelu(x):
    return 0.5 * x * (1.0 + jnp.tanh(GELU_K * (x + GELU_C * x * x * x)))


def _gelu_and_grad(x):
    t = jnp.tanh(GELU_K * (x + GELU_C * x * x * x))
    g = 0.5 * x * (1.0 + t)
    dg = 0.5 * (1.0 + t) + 0.5 * x * (1.0 - t * t) * (GELU_K * (1.0 + 3.0 * GELU_C * x * x))
    return g, dg


def _rms(x):
    r = lax.rsqrt(jnp.mean(x * x, axis=-1, keepdims=True) + RMS_EPS)
    return x * r, r


def _rms_bwd(dy, x, nw):
    xh, r = _rms(x)
    g = dy * nw
    dx = r * (g - xh * jnp.mean(g * xh, axis=-1, keepdims=True))
    return dx, jnp.sum(dy * xh, axis=0, keepdims=True)


def _row_acc(ref, row, val):
    ref[row:row + 1, :] = ref[row:row + 1, :] + val


def _shift_down(x, j, prev8):
    if j == 0:
        return x
    r = pltpu.roll(x, j, 0)
    rh = pltpu.roll(prev8, j, 0)
    row = lax.broadcasted_iota(jnp.int32, rh.shape, 0)
    top = jnp.where(row < j, rh, r[0:8])
    return jnp.concatenate([top, r[8:]], axis=0)


def _shift_up(x, j, next8):
    if j == 0:
        return x
    n = x.shape[0]
    r = pltpu.roll(x, n - j, 0)
    rh = pltpu.roll(next8, 8 - j, 0)
    row = lax.broadcasted_iota(jnp.int32, rh.shape, 0)
    bot = jnp.where(row >= 8 - j, rh, r[n - 8:])
    return jnp.concatenate([r[:n - 8], bot], axis=0)


def _scan_fwd(a, b):
    n = a.shape[0]
    row = lax.broadcasted_iota(jnp.int32, a.shape, 0)
    s = 1
    while s < n:
        keep = row >= s
        ar = pltpu.roll(a, s, 0)
        br = pltpu.roll(b, s, 0)
        b = jnp.where(keep, a * br + b, b)
        a = jnp.where(keep, a * ar, a)
        s *= 2
    return a, b


def _scan_bwd(a, b):
    n = a.shape[0]
    row = lax.broadcasted_iota(jnp.int32, a.shape, 0)
    s = 1
    while s < n:
        keep = row < n - s
        ar = pltpu.roll(a, n - s, 0)
        br = pltpu.roll(b, n - s, 0)
        b = jnp.where(keep, a * br + b, b)
        a = jnp.where(keep, a * ar, a)
        s *= 2
    return a, b


def _norm_matmul(x, nw, w, *, name, tm, tn, comm=None):
    m, d = x.shape
    n = w.shape[1]

    def body(x_ref, nw_ref, w_ref, o_ref, h_ref, h_sc):
        @pl.when(pl.program_id(1) == 0)
        def _():
            xh, _ = _rms(x_ref[...])
            h = _bf(xh * nw_ref[...])
            h_sc[...] = h
            h_ref[...] = h

        o_ref[...] = _dot(h_sc[...], w_ref[...])

    return _call(
        body, name=name, grid=(m // tm, n // tn), comm=comm,
        in_specs=[pl.BlockSpec((tm, d), lambda i, j: (i, 0)),
                  pl.BlockSpec((1, d), lambda i, j: (0, 0)),
                  pl.BlockSpec((d, tn), lambda i, j: (0, j))],
        out_specs=[pl.BlockSpec((tm, tn), lambda i, j: (i, j)),
                   pl.BlockSpec((tm, d), lambda i, j: (i, 0))],
        out_shape=[jax.ShapeDtypeStruct((m, n), F32), jax.ShapeDtypeStruct((m, d), BF16)],
        scratch=[pltpu.VMEM((tm, d), BF16)],
    )(x, nw, w)


def _mm_nt(a, w, *, name, tm, out_dtype):
    m, k = a.shape
    n = w.shape[0]

    def body(a_ref, w_ref, o_ref):
        o_ref[...] = _dot_nt(_bf(a_ref[...]), w_ref[...]).astype(out_dtype)

    return _call(
        body, name=name, grid=(m // tm, n // COL),
        in_specs=[pl.BlockSpec((tm, k), lambda i, j: (i, 0)),
                  pl.BlockSpec((COL, k), lambda i, j: (j, 0))],
        out_specs=pl.BlockSpec((tm, COL), lambda i, j: (i, j)),
        out_shape=jax.ShapeDtypeStruct((m, n), out_dtype),
    )(a, w)


def _piece_layout(pieces):
    offs, nblk, o = [], [], 0
    for p in pieces:
        offs.append(o)
        nblk.append(p.shape[1] // COL)
        o += p.shape[1] // COL
    return offs, nblk, o


def _mm_tn(a, pieces, *, name, tm, out_dtype=BF16):
    m, k = a.shape
    offs, nblk, nn = _piece_layout(pieces)

    def piece_spec(o, nb):
        def idx(ki, nj, mi):
            use = jnp.logical_and(nj >= o, nj < o + nb)
            return (jnp.where(use, mi, 0), jnp.clip(nj - o, 0, nb - 1))
        return pl.BlockSpec((tm, COL), idx)

    def body(a_ref, *rest):
        p_refs, o_ref, acc = rest[:len(pieces)], rest[len(pieces)], rest[len(pieces) + 1]
        nj, mi = pl.program_id(1), pl.program_id(2)

        @pl.when(mi == 0)
        def _():
            acc[...] = jnp.zeros_like(acc)

        for p_ref, o, nb in zip(p_refs, offs, nblk):
            @pl.when(jnp.logical_and(nj >= o, nj < o + nb))
            def _(p_ref=p_ref):
                acc[...] += _dot_tn(_bf(a_ref[...]), _bf(p_ref[...]))

        @pl.when(mi == pl.num_programs(2) - 1)
        def _():
            o_ref[...] = acc[...].astype(out_dtype)

    return _call(
        body, name=name, grid=(k // COL, nn, m // tm),
        in_specs=[pl.BlockSpec((tm, COL), lambda ki, nj, mi: (mi, ki))]
        + [piece_spec(o, nb) for o, nb in zip(offs, nblk)],
        out_specs=pl.BlockSpec((COL, COL), lambda ki, nj, mi: (ki, nj)),
        out_shape=jax.ShapeDtypeStruct((k, nn * COL), out_dtype),
        scratch=[pltpu.VMEM((COL, COL), F32)],
    )(a, *pieces)


def _mm_nt_normbwd(pieces, w, x, nw, dres, *, name, tm, comm=None):
    m, d = x.shape
    offs, nblk, nk = _piece_layout(pieces)

    def piece_spec(o, nb):
        return pl.BlockSpec((tm, COL), lambda i, k: (i, jnp.clip(k - o, 0, nb - 1)))

    def body(*refs):
        p_refs = refs[:len(pieces)]
        w_ref, x_ref, nw_ref, dres_ref, dx_ref, dnw_ref, acc = refs[len(pieces):]
        i, k = pl.program_id(0), pl.program_id(1)

        @pl.when(jnp.logical_and(i == 0, k == 0))
        def _():
            dnw_ref[...] = jnp.zeros_like(dnw_ref)

        @pl.when(k == 0)
        def _():
            acc[...] = jnp.zeros_like(acc)

        for p_ref, o, nb in zip(p_refs, offs, nblk):
            @pl.when(jnp.logical_and(k >= o, k < o + nb))
            def _(p_ref=p_ref):
                acc[...] += _dot_nt(_bf(p_ref[...]), w_ref[...])

        @pl.when(k == nk - 1)
        def _():
            dx, dnw = _rms_bwd(acc[...], x_ref[...], nw_ref[...])
            dx_ref[...] = dres_ref[...] + dx
            _row_acc(dnw_ref, 0, dnw)

    return _call(
        body, name=name, grid=(m // tm, nk), comm=comm,
        in_specs=[piece_spec(o, nb) for o, nb in zip(offs, nblk)]
        + [pl.BlockSpec((d, COL), lambda i, k: (0, k)),
           pl.BlockSpec((tm, d), lambda i, k: (i, 0)),
           pl.BlockSpec((1, d), lambda i, k: (0, 0)),
           pl.BlockSpec((tm, d), lambda i, k: (i, 0))],
        out_specs=[pl.BlockSpec((tm, d), lambda i, k: (i, 0)),
                   pl.BlockSpec((8, d), lambda i, k: (0, 0))],
        out_shape=[jax.ShapeDtypeStruct((m, d), F32), jax.ShapeDtypeStruct((8, d), F32)],
        scratch=[pltpu.VMEM((tm, d), F32)],
    )(*pieces, w, x, nw, dres)


def _rope_tables(pos3, invf):
    b, s, _ = pos3.shape

    def body(pos_ref, invf_ref, cos_ref, sin_ref):
        ang = pos_ref[...].astype(F32) * invf_ref[...]
        lane = lax.broadcasted_iota(jnp.int32, ang.shape, 1)
        cos_ref[...] = jnp.cos(ang)
        sin_ref[...] = jnp.where(lane < RET_DK // 2, -1.0, 1.0) * jnp.sin(ang)

    spec = pl.BlockSpec((None, SEQ_T, RET_DK), lambda i, c: (i, c, 0))
    return _call(
        body, name="rope_tables", grid=(b, s // SEQ_T),
        in_specs=[pl.BlockSpec((None, SEQ_T, 1), lambda i, c: (i, c, 0)),
                  pl.BlockSpec((1, RET_DK), lambda i, c: (0, 0))],
        out_specs=[spec, spec],
        out_shape=[jax.ShapeDtypeStruct((b, s, RET_DK), F32)] * 2,
    )(pos3, invf)


def _log_gamma(h):
    return float(np.log1p(-np.power(np.float32(2.0), np.float32(-5.0 - h))).astype(np.float32))


def _decay_matrix(h):
    lg = _log_gamma(h)
    n = lax.broadcasted_iota(jnp.int32, (SEQ_T, SEQ_T), 0)
    m = lax.broadcasted_iota(jnp.int32, (SEQ_T, SEQ_T), 1)
    same = (n // REF_CHUNK) == (m // REF_CHUNK)
    dist = jnp.where(same, jnp.abs(n - m), n - m).astype(F32)
    return jnp.where(jnp.logical_or(same, m < n), jnp.exp(lg * dist), 0.0)


def _decay_vectors(h):
    lg = _log_gamma(h)
    idx = lax.broadcasted_iota(jnp.int32, (SEQ_T, 1), 0).astype(F32)
    qd = jnp.exp(lg * (idx + 1.0))
    kd = jnp.exp(lg * (SEQ_T - 1.0 - idx))
    return qd, kd, math.exp(lg * SEQ_T)


def _rotate(x, cos, sin):
    return x * cos + pltpu.roll(x, RET_DK // 2, 1) * sin


def _rotate_bwd(d, cos, sin):
    return d * cos + pltpu.roll(d * sin, RET_DK // 2, 1)


def _ret_head(p_ref, cos, sin, h):
    q = p_ref[:, h * RET_DK:(h + 1) * RET_DK]
    k = p_ref[:, 512 + h * RET_DK:512 + (h + 1) * RET_DK]
    v = p_ref[:, 1024 + h * RET_DV:1024 + (h + 1) * RET_DV]
    g = p_ref[:, 2048 + h * RET_DV:2048 + (h + 1) * RET_DV]
    qr = _rotate(q, cos, sin)
    kr = _rotate(k, cos, sin) * (RET_DK ** -0.5)
    return qr, kr, v, g


def _group_norm(o):
    mu = jnp.mean(o, axis=-1, keepdims=True)
    oc = o - mu
    rstd = lax.rsqrt(jnp.mean(oc * oc, axis=-1, keepdims=True) + GN_EPS)
    return oc * rstd, rstd


def _retention_fwd(proj3, cos, sin, gnw, comm=None):
    b, s, _ = proj3.shape
    nc = s // SEQ_T

    def body(p_ref, cos_ref, sin_ref, gnw_ref, a_ref, st_ref, state, wtab):
        c = pl.program_id(1)

        @pl.when(jnp.logical_and(pl.program_id(0) == 0, c == 0))
        def _():
            for h in range(RET_HEADS):
                wtab[h] = _decay_matrix(h)

        @pl.when(c == 0)
        def _():
            state[...] = jnp.zeros_like(state)

        cs, sn = cos_ref[...], sin_ref[...]
        for h in range(RET_HEADS):
            qd, kd, gt = _decay_vectors(h)
            qr, kr, v, g = _ret_head(p_ref, cs, sn, h)
            st = state[h]
            st_ref[h] = st
            p = _dot_nt(_bf(qr), _bf(kr)) * wtab[h]
            o = _dot(_bf(p), _bf(v)) + _dot(_bf(qr * qd), _bf(st))
            state[h] = st * gt + _dot_tn(_bf(kr * kd), _bf(v))
            on, _ = _group_norm(o)
            gw = gnw_ref[:, h * RET_DV:(h + 1) * RET_DV]
            a_ref[:, h * RET_DV:(h + 1) * RET_DV] = _bf(on * gw * (g * _sigmoid(g)))

    tab = pl.BlockSpec((None, SEQ_T, RET_DK), lambda i, c: (i, c, 0))
    return _call(
        body, name="retention_fwd", grid=(b, nc), comm=comm,
        in_specs=[pl.BlockSpec((None, SEQ_T, 3072), lambda i, c: (i, c, 0)), tab, tab,
                  pl.BlockSpec((1, D_MODEL), lambda i, c: (0, 0))],
        out_specs=[pl.BlockSpec((None, SEQ_T, D_MODEL), lambda i, c: (i, c, 0)),
                   pl.BlockSpec((None, None, RET_HEADS, RET_DK, RET_DV), lambda i, c: (i, c, 0, 0, 0))],
        out_shape=[jax.ShapeDtypeStruct((b, s, D_MODEL), BF16),
                   jax.ShapeDtypeStruct((b, nc, RET_HEADS, RET_DK, RET_DV), F32)],
        scratch=[pltpu.VMEM((RET_HEADS, RET_DK, RET_DV), F32),
                 pltpu.VMEM((RET_HEADS, SEQ_T, SEQ_T), F32)],
    )(proj3, cos, sin, gnw)


def _retention_bwd(proj3, cos, sin, gnw, states, da3, comm=None):
    b, s, _ = proj3.shape
    nc = s // SEQ_T

    def body(p_ref, cos_ref, sin_ref, gnw_ref, st_ref, da_ref, d_ref, dgn_ref, dstate, wtab):
        c = pl.program_id(1)

        @pl.when(jnp.logical_and(pl.program_id(0) == 0, c == 0))
        def _():
            dgn_ref[...] = jnp.zeros_like(dgn_ref)
            for h in range(RET_HEADS):
                wtab[h] = _decay_matrix(h)

        @pl.when(c == 0)
        def _():
            dstate[...] = jnp.zeros_like(dstate)

        cs, sn = cos_ref[...], sin_ref[...]
        for h in range(RET_HEADS):
            qd, kd, gt = _decay_vectors(h)
            qr, kr, v, g = _ret_head(p_ref, cs, sn, h)
            st, dst, w = st_ref[h], dstate[h], wtab[h]
            qb, kb, vb = _bf(qr), _bf(kr), _bf(v)
            p = _dot_nt(qb, kb) * w
            o = _dot(_bf(p), vb) + _dot(_bf(qr * qd), _bf(st))
            on, rstd = _group_norm(o)
            gw = gnw_ref[:, h * RET_DV:(h + 1) * RET_DV]
            da = da_ref[:, h * RET_DV:(h + 1) * RET_DV].astype(F32)
            sg = _sigmoid(g)
            silu = g * sg
            dg = da * on * gw * (sg * (1.0 + g * (1.0 - sg)))
            dgn_ref[0:1, h * RET_DV:(h + 1) * RET_DV] += jnp.sum(da * silu * on, axis=0, keepdims=True)
            don = da * silu * gw
            do = rstd * (don - jnp.mean(don, axis=-1, keepdims=True)
                         - on * jnp.mean(don * on, axis=-1, keepdims=True))
            dob = _bf(do)
            dp = _dot_nt(dob, vb) * w
            dqr = _dot(_bf(dp), kb) + _dot_nt(dob, _bf(st)) * qd
            dkr = _dot_tn(_bf(dp), qb) + _dot_nt(vb, _bf(dst)) * kd
            dv = _dot_tn(_bf(p), dob) + _dot(_bf(kr * kd), _bf(dst))
            dstate[h] = dst * gt + _dot_tn(_bf(qr * qd), dob)
            d_ref[:, h * RET_DK:(h + 1) * RET_DK] = _bf(_rotate_bwd(dqr, cs, sn))
            d_ref[:, 512 + h * RET_DK:512 + (h + 1) * RET_DK] = _bf(_rotate_bwd(dkr, cs, sn) * (RET_DK ** -0.5))
            d_ref[:, 1024 + h * RET_DV:1024 + (h + 1) * RET_DV] = _bf(dv)
            d_ref[:, 2048 + h * RET_DV:2048 + (h + 1) * RET_DV] = _bf(dg)

    rev = lambda i, c: (i, nc - 1 - c, 0)
    tab = pl.BlockSpec((None, SEQ_T, RET_DK), rev)
    return _call(
        body, name="retention_bwd", grid=(b, nc), comm=comm,
        in_specs=[pl.BlockSpec((None, SEQ_T, 3072), rev), tab, tab,
                  pl.BlockSpec((1, D_MODEL), lambda i, c: (0, 0)),
                  pl.BlockSpec((None, None, RET_HEADS, RET_DK, RET_DV), lambda i, c: (i, nc - 1 - c, 0, 0, 0)),
                  pl.BlockSpec((None, SEQ_T, D_MODEL), rev)],
        out_specs=[pl.BlockSpec((None, SEQ_T, 3072), rev),
                   pl.BlockSpec((8, D_MODEL), lambda i, c: (0, 0))],
        out_shape=[jax.ShapeDtypeStruct((b, s, 3072), BF16), jax.ShapeDtypeStruct((8, D_MODEL), F32)],
        scratch=[pltpu.VMEM((RET_HEADS, RET_DK, RET_DV), F32),
                 pltpu.VMEM((RET_HEADS, SEQ_T, SEQ_T), F32)],
    )(proj3, cos, sin, gnw, states, da3)


def _softplus_neg(lam):
    z = -lam
    u = jnp.exp(-jnp.abs(z))
    log1p_u = jnp.where(u < 0.01, u * (1.0 - u * (0.5 - u * (1.0 / 3.0))), jnp.log(1.0 + u))
    return jnp.maximum(z, 0.0) + log1p_u


def _lru_gates(x, prev8, cw_ref, cb_ref, wr_ref, br_ref, wi_ref, bi_ref, lam_ref):
    xc = cb_ref[...] + sum(cw_ref[pl.ds(j, 1), :] * _shift_down(x, 3 - j, prev8) for j in range(4))
    rs, is_ = [], []
    for n in range(LRU_BLOCKS):
        xb = _bf(xc[:, n * LRU_BLOCK:(n + 1) * LRU_BLOCK])
        cols = slice(n * LRU_BLOCK, (n + 1) * LRU_BLOCK)
        rs.append(_sigmoid(_dot(xb, wr_ref[n]) + br_ref[:, cols]))
        is_.append(_sigmoid(_dot(xb, wi_ref[n]) + bi_ref[:, cols]))
    r = jnp.concatenate(rs, axis=1)
    i = jnp.concatenate(is_, axis=1)
    sp = _softplus_neg(lam_ref[...])
    la = -LRU_C * r * sp
    a = jnp.exp(la)
    s = jnp.sqrt(-jnp.tanh(la) * (a * a + 1.0))
    return xc, r, i, a, s, sp


_LRU_PARAM_SPECS = [
    pl.BlockSpec((4, D_MODEL), lambda i, c: (0, 0)),
    pl.BlockSpec((1, D_MODEL), lambda i, c: (0, 0)),
    pl.BlockSpec((LRU_BLOCKS, LRU_BLOCK, LRU_BLOCK), lambda i, c: (0, 0, 0)),
    pl.BlockSpec((1, D_MODEL), lambda i, c: (0, 0)),
    pl.BlockSpec((LRU_BLOCKS, LRU_BLOCK, LRU_BLOCK), lambda i, c: (0, 0, 0)),
    pl.BlockSpec((1, D_MODEL), lambda i, c: (0, 0)),
    pl.BlockSpec((1, D_MODEL), lambda i, c: (0, 0)),
]


def _lru_fwd(proj3, params):
    b, s, _ = proj3.shape
    nc = s // SEQ_T

    def body(x_ref, y_ref, cw, cb, wr, br, wi, bi, lam, o_ref, h_ref, xprev, hprev):
        @pl.when(pl.program_id(1) == 0)
        def _():
            xprev[...] = jnp.zeros_like(xprev)
            hprev[...] = jnp.zeros_like(hprev)

        x = x_ref[...]
        xc, r, i, a, s_, _ = _lru_gates(x, xprev[...], cw, cb, wr, br, wi, bi, lam)
        xprev[...] = x[SEQ_T - 8:]
        acum, bcum = _scan_fwd(a, s_ * (i * xc))
        h = acum * hprev[7:8, :] + bcum
        hprev[...] = h[SEQ_T - 8:]
        h_ref[...] = h
        o_ref[...] = _bf(h * _gelu(y_ref[...]))

    return _call(
        body, name="lru_fwd", grid=(b, nc),
        in_specs=[pl.BlockSpec((None, SEQ_T, D_MODEL), lambda i, c: (i, c, 3)),
                  pl.BlockSpec((None, SEQ_T, D_MODEL), lambda i, c: (i, c, 4))] + _LRU_PARAM_SPECS,
        out_specs=[pl.BlockSpec((None, SEQ_T, D_MODEL), lambda i, c: (i, c, 0))] * 2,
        out_shape=[jax.ShapeDtypeStruct((b, s, D_MODEL), BF16), jax.ShapeDtypeStruct((b, s, D_MODEL), F32)],
        scratch=[pltpu.VMEM((8, D_MODEL), F32), pltpu.VMEM((8, D_MODEL), F32)],
    )(proj3, proj3, *params)


def _lru_bwd(proj3, params, hseq, db3, comm=None):
    b, s, _ = proj3.shape
    nc = s // SEQ_T
    blk8 = SEQ_T // 8

    def body(x_ref, y_ref, xp_ref, h_ref, hp_ref, db_ref, cw, cb, wr, br, wi, bi, lam,
             d_ref, dwr_ref, dwi_ref, sm_ref, gnext, anext, dxcnext):
        c = pl.program_id(1)
        first_chunk = c == nc - 1

        @pl.when(jnp.logical_and(pl.program_id(0) == 0, c == 0))
        def _():
            dwr_ref[...] = jnp.zeros_like(dwr_ref)
            dwi_ref[...] = jnp.zeros_like(dwi_ref)
            sm_ref[...] = jnp.zeros_like(sm_ref)

        @pl.when(c == 0)
        def _():
            gnext[...] = jnp.zeros_like(gnext)
            anext[...] = jnp.zeros_like(anext)
            dxcnext[...] = jnp.zeros_like(dxcnext)

        x, y, h = x_ref[...], y_ref[...], h_ref[...]
        keep_prev = jnp.where(first_chunk, 0.0, 1.0)
        xprev = xp_ref[...] * keep_prev
        hprev = hp_ref[...] * keep_prev
        xc, r, i, a, s_, sp = _lru_gates(x, xprev, cw, cb, wr, br, wi, bi, lam)
        db = db_ref[...].astype(F32)
        gy, dgy = _gelu_and_grad(y)
        dy = db * h * dgy
        a_up = _shift_up(a, 1, anext[...])
        acum, ucum = _scan_bwd(a_up, db * gy)
        g = ucum + acum * gnext[0:1, :]
        gnext[...] = g[0:8]
        anext[...] = a[0:8]
        da = g * _shift_down(h, 1, hprev)
        ixc = i * xc
        dla = da * a - (g * ixc) * (a * a) / s_
        di = g * s_ * xc
        dxc = g * s_ * i
        dzr = dla * (-LRU_C * sp) * r * (1.0 - r)
        dzi = di * i * (1.0 - i)
        lam_v = lam[...]
        _row_acc(sm_ref, 7, jnp.sum(dla * (LRU_C * r), axis=0, keepdims=True) * _sigmoid(-lam_v))
        _row_acc(sm_ref, 5, jnp.sum(dzr, axis=0, keepdims=True))
        _row_acc(sm_ref, 6, jnp.sum(dzi, axis=0, keepdims=True))
        parts = []
        for n in range(LRU_BLOCKS):
            cols = slice(n * LRU_BLOCK, (n + 1) * LRU_BLOCK)
            xb, zr, zi = _bf(xc[:, cols]), _bf(dzr[:, cols]), _bf(dzi[:, cols])
            parts.append(dxc[:, cols] + _dot_nt(zr, wr[n]) + _dot_nt(zi, wi[n]))
            dwr_ref[n] += _dot_tn(xb, zr)
            dwi_ref[n] += _dot_tn(xb, zi)
        dxc = jnp.concatenate(parts, axis=1)
        _row_acc(sm_ref, 4, jnp.sum(dxc, axis=0, keepdims=True))
        nxt = dxcnext[...]
        dx = jnp.zeros_like(x)
        for j in range(4):
            dx = dx + cw[pl.ds(j, 1), :] * _shift_up(dxc, 3 - j, nxt)
            _row_acc(sm_ref, j, jnp.sum(dxc * _shift_down(x, 3 - j, xprev), axis=0, keepdims=True))
        dxcnext[...] = dxc[0:8]
        d_ref[:, 0:D_MODEL] = _bf(dx)
        d_ref[:, D_MODEL:2 * D_MODEL] = _bf(dy)

    rev = lambda col: (lambda i, c: (i, nc - 1 - c, col))
    prev = lambda col: (lambda i, c: (i, jnp.maximum((nc - 1 - c) * blk8 - 1, 0), col))
    return _call(
        body, name="lru_bwd", grid=(b, nc), comm=comm,
        in_specs=[pl.BlockSpec((None, SEQ_T, D_MODEL), rev(3)),
                  pl.BlockSpec((None, SEQ_T, D_MODEL), rev(4)),
                  pl.BlockSpec((None, 8, D_MODEL), prev(3)),
                  pl.BlockSpec((None, SEQ_T, D_MODEL), rev(0)),
                  pl.BlockSpec((None, 8, D_MODEL), prev(0)),
                  pl.BlockSpec((None, SEQ_T, D_MODEL), rev(0))] + _LRU_PARAM_SPECS,
        out_specs=[pl.BlockSpec((None, SEQ_T, 2 * D_MODEL), rev(0)),
                   pl.BlockSpec((LRU_BLOCKS, LRU_BLOCK, LRU_BLOCK), lambda i, c: (0, 0, 0)),
                   pl.BlockSpec((LRU_BLOCKS, LRU_BLOCK, LRU_BLOCK), lambda i, c: (0, 0, 0)),
                   pl.BlockSpec((8, D_MODEL), lambda i, c: (0, 0))],
        out_shape=[jax.ShapeDtypeStruct((b, s, 2 * D_MODEL), BF16),
                   jax.ShapeDtypeStruct((LRU_BLOCKS, LRU_BLOCK, LRU_BLOCK), F32),
                   jax.ShapeDtypeStruct((LRU_BLOCKS, LRU_BLOCK, LRU_BLOCK), F32),
                   jax.ShapeDtypeStruct((8, D_MODEL), F32)],
        scratch=[pltpu.VMEM((8, D_MODEL), F32)] * 3,
    )(proj3, proj3, proj3, hseq, hseq, db3, *params)


def _merge_parts(a_ref, b_ref, gr_ref, gl_ref, mgb_ref, wro_ref, wlo_ref):
    ya = _dot(a_ref[...], wro_ref[...])
    yb = _dot(b_ref[...], wlo_ref[...])
    sa = _sigmoid(gr_ref[...] + mgb_ref[0:1, :])
    sb = _sigmoid(gl_ref[...] + mgb_ref[1:2, :])
    return ya, yb, sa, sb


def _merge_specs(tm):
    row = lambda col: pl.BlockSpec((tm, D_MODEL), lambda i: (i, col))
    full = pl.BlockSpec((D_MODEL, D_MODEL), lambda i: (0, 0))
    return row, full


def _merge_fwd(a_in, b_in, proj, mgb, wro, wlo, wout, x, *, tm):
    m = x.shape[0]
    row, full = _merge_specs(tm)

    def body(a_ref, b_ref, gr_ref, gl_ref, mgb_ref, wro_ref, wlo_ref, wout_ref, x_ref, o_ref):
        ya, yb, sa, sb = _merge_parts(a_ref, b_ref, gr_ref, gl_ref, mgb_ref, wro_ref, wlo_ref)
        o_ref[...] = x_ref[...] + _dot(_bf(sa * ya + sb * yb), wout_ref[...])

    return _call(
        body, name="merge_fwd", grid=(m // tm,),
        in_specs=[row(0), row(0), row(5), row(6), pl.BlockSpec((2, D_MODEL), lambda i: (0, 0)),
                  full, full, full, row(0)],
        out_specs=row(0),
        out_shape=jax.ShapeDtypeStruct((m, D_MODEL), F32),
    )(a_in, b_in, proj, proj, mgb, wro, wlo, wout, x)


def _merge_bwd(a_in, b_in, proj, mgb, wro, wlo, wout, dx2, *, tm, comm=None):
    m = dx2.shape[0]
    row, full = _merge_specs(tm)

    def body(a_ref, b_ref, gr_ref, gl_ref, mgb_ref, wro_ref, wlo_ref, wout_ref, dx_ref,
             mix_ref, dya_ref, dyb_ref, da_ref, db_ref, dg_ref, sm_ref):
        @pl.when(pl.program_id(0) == 0)
        def _():
            sm_ref[...] = jnp.zeros_like(sm_ref)

        ya, yb, sa, sb = _merge_parts(a_ref, b_ref, gr_ref, gl_ref, mgb_ref, wro_ref, wlo_ref)
        mix_ref[...] = _bf(sa * ya + sb * yb)
        dmix = _dot_nt(_bf(dx_ref[...]), wout_ref[...])
        dya, dyb = _bf(dmix * sa), _bf(dmix * sb)
        dya_ref[...] = dya
        dyb_ref[...] = dyb
        dga = dmix * ya * sa * (1.0 - sa)
        dgb = dmix * yb * sb * (1.0 - sb)
        dg_ref[:, 0:D_MODEL] = _bf(dga)
        dg_ref[:, D_MODEL:2 * D_MODEL] = _bf(dgb)
        _row_acc(sm_ref, 0, jnp.sum(dga, axis=0, keepdims=True))
        _row_acc(sm_ref, 1, jnp.sum(dgb, axis=0, keepdims=True))
        da_ref[...] = _bf(_dot_nt(dya, wro_ref[...]))
        db_ref[...] = _bf(_dot_nt(dyb, wlo_ref[...]))

    act = jax.ShapeDtypeStruct((m, D_MODEL), BF16)
    return _call(
        body, name="merge_bwd", grid=(m // tm,), comm=comm,
        in_specs=[row(0), row(0), row(5), row(6), pl.BlockSpec((2, D_MODEL), lambda i: (0, 0)),
                  full, full, full, row(0)],
        out_specs=[row(0)] * 5 + [pl.BlockSpec((tm, 2 * D_MODEL), lambda i: (i, 0)),
                                  pl.BlockSpec((8, D_MODEL), lambda i: (0, 0))],
        out_shape=[act] * 5 + [jax.ShapeDtypeStruct((m, 2 * D_MODEL), BF16),
                               jax.ShapeDtypeStruct((8, D_MODEL), F32)],
    )(a_in, b_in, proj, proj, mgb, wro, wlo, wout, dx2)


def _ffn_conv(gate, prev8, cw_ref, cb_ref):
    return cb_ref[...] + sum(cw_ref[pl.ds(j, 1), :] * _shift_down(gate, 2 - j, prev8) for j in range(3))


def _ffn_act_fwd(up3, cw, cb):
    b, s, _ = up3.shape

    def body(g_ref, v_ref, cw_ref, cb_ref, o_ref, gprev):
        @pl.when(pl.program_id(1) == 0)
        def _():
            gprev[...] = jnp.zeros_like(gprev)

        gate = g_ref[...]
        gc = _ffn_conv(gate, gprev[...], cw_ref, cb_ref)
        gprev[...] = gate[SEQ_T - 8:]
        o_ref[...] = _bf(_gelu(gc) * v_ref[...])

    return _call(
        body, name="ffn_act_fwd", grid=(b, s // SEQ_T),
        in_specs=[pl.BlockSpec((None, SEQ_T, D_FF), lambda i, c: (i, c, 0)),
                  pl.BlockSpec((None, SEQ_T, D_FF), lambda i, c: (i, c, 1)),
                  pl.BlockSpec((3, D_FF), lambda i, c: (0, 0)),
                  pl.BlockSpec((1, D_FF), lambda i, c: (0, 0))],
        out_specs=pl.BlockSpec((None, SEQ_T, D_FF), lambda i, c: (i, c, 0)),
        out_shape=jax.ShapeDtypeStruct((b, s, D_FF), BF16),
        scratch=[pltpu.VMEM((8, D_FF), F32)],
    )(up3, up3, cw, cb)


def _ffn_act_bwd(up3, cw, cb, df3, comm=None):
    b, s, _ = up3.shape
    nc = s // SEQ_T
    blk8 = SEQ_T // 8

    def body(g_ref, v_ref, gp_ref, df_ref, cw_ref, cb_ref, dg_ref, dv_ref, sm_ref, dgcnext):
        c = pl.program_id(1)

        @pl.when(jnp.logical_and(pl.program_id(0) == 0, c == 0))
        def _():
            sm_ref[...] = jnp.zeros_like(sm_ref)

        @pl.when(c == 0)
        def _():
            dgcnext[...] = jnp.zeros_like(dgcnext)

        gate = g_ref[...]
        gprev = gp_ref[...] * jnp.where(c == nc - 1, 0.0, 1.0)
        gc = _ffn_conv(gate, gprev, cw_ref, cb_ref)
        act, dact = _gelu_and_grad(gc)
        df = df_ref[...].astype(F32)
        dv_ref[...] = _bf(df * act)
        dgc = df * v_ref[...] * dact
        nxt = dgcnext[...]
        dgate = jnp.zeros_like(gate)
        for j in range(3):
            dgate = dgate + cw_ref[pl.ds(j, 1), :] * _shift_up(dgc, 2 - j, nxt)
            _row_acc(sm_ref, j, jnp.sum(dgc * _shift_down(gate, 2 - j, gprev), axis=0, keepdims=True))
        _row_acc(sm_ref, 3, jnp.sum(dgc, axis=0, keepdims=True))
        dgcnext[...] = dgc[0:8]
        dg_ref[...] = _bf(dgate)

    rev = lambda col: (lambda i, c: (i, nc - 1 - c, col))
    return _call(
        body, name="ffn_act_bwd", grid=(b, nc), comm=comm,
        in_specs=[pl.BlockSpec((None, SEQ_T, D_FF), rev(0)),
                  pl.BlockSpec((None, SEQ_T, D_FF), rev(1)),
                  pl.BlockSpec((None, 8, D_FF), lambda i, c: (i, jnp.maximum((nc - 1 - c) * blk8 - 1, 0), 0)),
                  pl.BlockSpec((None, SEQ_T, D_FF), rev(0)),
                  pl.BlockSpec((3, D_FF), lambda i, c: (0, 0)),
                  pl.BlockSpec((1, D_FF), lambda i, c: (0, 0))],
        out_specs=[pl.BlockSpec((None, SEQ_T, D_FF), rev(0))] * 2
        + [pl.BlockSpec((8, D_FF), lambda i, c: (0, 0))],
        out_shape=[jax.ShapeDtypeStruct((b, s, D_FF), BF16)] * 2 + [jax.ShapeDtypeStruct((8, D_FF), F32)],
        scratch=[pltpu.VMEM((8, D_FF), F32)],
    )(up3, up3, up3, df3, cw, cb)


def _ffn_down_loss(f, wd, x2, nfw, target, *, tm):
    m, kf = f.shape
    nt = m // tm

    def body(f_ref, wd_ref, x_ref, nw_ref, t_ref, loss_ref, dx_ref, dnw_ref, lsum):
        i = pl.program_id(0)

        @pl.when(i == 0)
        def _():
            dnw_ref[...] = jnp.zeros_like(dnw_ref)
            lsum[...] = jnp.zeros_like(lsum)

        x3 = x_ref[...] + _dot(f_ref[...], wd_ref[...])
        nw = nw_ref[...]
        xh, _ = _rms(x3)
        err = xh * nw - t_ref[...]
        lsum[...] += jnp.sum(err * err, axis=0, keepdims=True)
        dx, dnw = _rms_bwd(err * (1.0 / D_MODEL), x3, nw)
        dx_ref[...] = dx
        _row_acc(dnw_ref, 0, dnw)

        @pl.when(i == nt - 1)
        def _():
            loss_ref[...] = jnp.sum(lsum[...], axis=1, keepdims=True) * (0.5 / D_MODEL)

    row = pl.BlockSpec((tm, D_MODEL), lambda i: (i, 0))
    return _call(
        body, name="ffn_down_loss", grid=(nt,),
        in_specs=[pl.BlockSpec((tm, kf), lambda i: (i, 0)),
                  pl.BlockSpec((kf, D_MODEL), lambda i: (0, 0)),
                  row, pl.BlockSpec((1, D_MODEL), lambda i: (0, 0)), row],
        out_specs=[pl.BlockSpec((1, 1), lambda i: (0, 0)), row,
                   pl.BlockSpec((8, D_MODEL), lambda i: (0, 0))],
        out_shape=[jax.ShapeDtypeStruct((1, 1), F32), jax.ShapeDtypeStruct((m, D_MODEL), F32),
                   jax.ShapeDtypeStruct((8, D_MODEL), F32)],
        scratch=[pltpu.VMEM((1, D_MODEL), F32)],
    )(f, wd, x2, nfw, target)


def _row_tile(rows):
    return next((t for t in (256, 128, 64, 32, 16, 8) if rows % t == 0), rows)


def _adamw(w, gs, m, v, *, name):
    rows, cols = w.shape
    tr = _row_tile(rows)
    ng = len(gs)

    def body(w_ref, *rest):
        g_refs, (m_ref, v_ref, g_out, d_out, m_out, v_out) = rest[:ng], rest[ng:]
        g = g_refs[0][...]
        for r in g_refs[1:]:
            g = g + r[...]
        mn = ADAM_B1 * m_ref[...] + (1.0 - ADAM_B1) * g
        vn = ADAM_B2 * v_ref[...] + (1.0 - ADAM_B2) * (g * g)
        m_hat = mn / (1.0 - ADAM_B1 ** ADAM_STEP)
        v_hat = vn / (1.0 - ADAM_B2 ** ADAM_STEP)
        g_out[...] = g
        d_out[...] = -ADAM_LR * (m_hat / (jnp.sqrt(v_hat) + ADAM_EPS) + ADAM_WD * w_ref[...])
        m_out[...] = mn
        v_out[...] = vn

    spec = pl.BlockSpec((tr, cols), lambda i: (i, 0))
    return _call(
        body, name=name, grid=(rows // tr,),
        in_specs=[spec] * (3 + ng), out_specs=[spec] * 4,
        out_shape=[jax.ShapeDtypeStruct((rows, cols), F32)] * 4,
    )(w, *gs, m, v)


def _mesh_pos():
    x, y, c = lax.axis_index("x"), lax.axis_index("y"), lax.axis_index("c")
    return x, y, c


def _other_chips(x, y, c):
    return [((1 - x, y, c), 2 * (1 - x) + y), ((x, 1 - y, c), 2 * x + 1 - y),
            ((1 - x, 1 - y, c), 2 * (1 - x) + 1 - y)]


def _region(ref, axis, size, half_axis, chip, core=None):
    idx = [slice(None)] * len(ref.shape)
    if core is None:
        idx[axis] = pl.ds(pl.multiple_of(chip * size, size), size)
    elif half_axis == axis:
        h = size // 2
        idx[axis] = pl.ds(pl.multiple_of(chip * size + core * h, h), h)
    else:
        idx[axis] = pl.ds(pl.multiple_of(chip * size, size), size)
        h = ref.shape[half_axis] // 2
        idx[half_axis] = pl.ds(pl.multiple_of(core * h, h), h)
    return ref.at[tuple(idx)]


def _half(ref, half_axis, core):
    idx = [slice(None)] * len(ref.shape)
    h = ref.shape[half_axis] // 2
    idx[half_axis] = pl.ds(pl.multiple_of(core * h, h), h)
    return ref.at[tuple(idx)]


class _Copy:
    def __init__(self, make):
        self._make = make

    def start(self):
        self._make().start()

    def wait(self):
        self._make().wait()

    def wait_send(self):
        self._make().wait_send()

    def wait_recv(self):
        self._make().wait_recv()


def _remote(src, dst, send_sem, recv_sem, dev):
    return _Copy(lambda: pltpu.make_async_remote_copy(
        src_ref=src, dst_ref=dst, send_sem=send_sem, recv_sem=recv_sem, device_id=dev, device_id_type=MESH))


def _local(src, dst, sem):
    return _Copy(lambda: pltpu.make_async_copy(src, dst, sem))


def _dma_sems(n):
    return pltpu.SemaphoreType.DMA((n,))


def _ici_leg(srcs, dsts, layout, n_whole, sems):
    send_sems, recv_sems, local_sems = sems
    x, y, c = _mesh_pos()
    mine = 2 * x + y
    local, sends, recvs = [], [], []
    for t, (src, dst) in enumerate(zip(srcs, dsts)):
        if t < len(srcs) - n_whole:
            ax, hx = layout[t]
            size = src.shape[ax]
            part = _half(src, hx, c)
            place = lambda chip, core=None, dst=dst, ax=ax, hx=hx, size=size: _region(dst, ax, size, hx, chip, core)
            own, landing = place(mine), (lambda chip: place(chip, c))
        else:
            part, own, landing = src, dst.at[mine], (lambda chip, dst=dst: dst.at[chip])
        local.append(_local(src, own, local_sems.at[t]))
        for k, (dev, chip) in enumerate(_other_chips(x, y, c)):
            sends.append(_remote(part, landing(mine), send_sems.at[3 * t + k], recv_sems.at[3 * t + k], dev))
            recvs.append(_remote(part, landing(chip), send_sems.at[3 * t + k], recv_sems.at[3 * t + k], dev))
    return local, sends, recvs


def _d2d_leg(srcs, dsts, layout, sizes, sems):
    send_sems, recv_sems = sems
    x, y, c = _mesh_pos()
    sends, recvs = [], []
    for t, (src, dst) in enumerate(zip(srcs, dsts)):
        ax, hx = layout[t]
        for k, (_, chip) in enumerate(_other_chips(x, y, c)):
            sem = (send_sems.at[3 * t + k], recv_sems.at[3 * t + k])
            sends.append(_remote(_region(src, ax, sizes[t], hx, chip, c),
                                 _region(dst, ax, sizes[t], hx, chip, c), *sem, (x, y, 1 - c)))
            recvs.append(_remote(_region(src, ax, sizes[t], hx, chip, 1 - c),
                                 _region(dst, ax, sizes[t], hx, chip, 1 - c), *sem, (x, y, 1 - c)))
    return sends, recvs


def _full_shapes(shards, layout, whole):
    shapes = []
    for sh, (ax, _) in zip(shards, layout):
        shape = list(sh.shape)
        shape[ax] *= N_CHIPS
        shapes.append(jax.ShapeDtypeStruct(tuple(shape), sh.dtype))
    return shapes + [jax.ShapeDtypeStruct((N_CHIPS,) + w.shape, w.dtype) for w in whole]


def _gather_ici(shards, layout, whole=()):
    n = len(shards) + len(whole)

    def start(ins, outs, sems):
        local, sends, _ = _ici_leg(ins, outs, layout, len(whole), sems)
        for cp in local + sends:
            cp.start()

    def finish(ins, outs, sems):
        local, sends, recvs = _ici_leg(ins, outs, layout, len(whole), sems)
        for cp in recvs:
            cp.wait_recv()
        for cp in sends:
            cp.wait_send()
        for cp in local:
            cp.wait()

    return _Comm(list(shards) + list(whole), _full_shapes(shards, layout, whole),
                 [_dma_sems(3 * n), _dma_sems(3 * n), _dma_sems(n)], start, finish)


def _gather_d2d(bufs, layout, sizes):
    n = len(bufs)

    def start(ins, outs, sems):
        for cp in _d2d_leg(ins, outs, layout, sizes, sems)[0]:
            cp.start()

    def finish(ins, outs, sems):
        sends, recvs = _d2d_leg(ins, outs, layout, sizes, sems)
        for cp in recvs:
            cp.wait_recv()
        for cp in sends:
            cp.wait_send()

    return _Comm(bufs, [jax.ShapeDtypeStruct(b.shape, b.dtype) for b in bufs],
                 [_dma_sems(3 * n), _dma_sems(3 * n)], start, finish, aliases={i: i for i in range(n)})


def _gather_both(shards, layout, whole):
    n, ns = len(shards) + len(whole), len(shards)
    sizes = [sh.shape[ax] for sh, (ax, _) in zip(shards, layout)]

    def start(ins, outs, sems):
        local, sends, _ = _ici_leg(ins, outs, layout, len(whole), sems[:3])
        for cp in local + sends:
            cp.start()

    def finish(ins, outs, sems):
        local, sends, recvs = _ici_leg(ins, outs, layout, len(whole), sems[:3])
        for cp in recvs:
            cp.wait_recv()
        onward, arriving = _d2d_leg(outs[:ns], outs[:ns], layout, sizes, sems[3:])
        for cp in onward:
            cp.start()
        for cp in arriving:
            cp.wait_recv()
        for cp in sends + onward:
            cp.wait_send()
        for cp in local:
            cp.wait()

    return _Comm(list(shards) + list(whole), _full_shapes(shards, layout, whole),
                 [_dma_sems(3 * n), _dma_sems(3 * n), _dma_sems(n), _dma_sems(3 * ns), _dma_sems(3 * ns)],
                 start, finish)


def _exchange(grads, layout):
    n = len(grads)
    others = N_DEV - 1
    sizes = [g.shape[ax] // N_CHIPS for g, (ax, _) in zip(grads, layout)]
    out_shapes = []
    for g, (ax, hx), sz in zip(grads, layout, sizes):
        shape = list(g.shape)
        shape[ax] = sz
        shape[hx] //= 2
        out_shapes.append(jax.ShapeDtypeStruct((N_DEV,) + tuple(shape), g.dtype))

    def copies(ins, outs, sems):
        send_sems, recv_sems, local_sems = sems
        x, y, c = _mesh_pos()
        me = 4 * x + 2 * y + c
        local, sends, recvs = [], [], []
        for t, (src, dst) in enumerate(zip(ins, outs)):
            ax, hx = layout[t]
            local.append(_local(_region(src, ax, sizes[t], hx, 2 * x + y, c), dst.at[me], local_sems.at[t]))
            for r in range(1, N_DEV):
                px = (1 - x) if r & 4 else x
                py = (1 - y) if r & 2 else y
                pc = (1 - c) if r & 1 else c
                sem = (send_sems.at[others * t + r - 1], recv_sems.at[others * t + r - 1])
                part = _region(src, ax, sizes[t], hx, 2 * px + py, pc)
                sends.append(_remote(part, dst.at[me], *sem, (px, py, pc)))
                recvs.append(_remote(part, dst.at[4 * px + 2 * py + pc], *sem, (px, py, pc)))
        return local, sends, recvs

    def start(ins, outs, sems):
        local, sends, _ = copies(ins, outs, sems)
        for cp in local + sends:
            cp.start()

    def finish(ins, outs, sems):
        local, sends, recvs = copies(ins, outs, sems)
        for cp in recvs:
            cp.wait_recv()
        for cp in sends:
            cp.wait_send()
        for cp in local:
            cp.wait()

    return _Comm(grads, out_shapes, [_dma_sems(others * n), _dma_sems(others * n), _dma_sems(n)], start, finish)


def _sum_devices(parts, *, name):
    _, rows, cols = parts.shape
    tr = _row_tile(rows)

    def body(p_ref, o_ref):
        acc = p_ref[0].astype(F32)
        for q in range(1, N_DEV):
            acc = acc + p_ref[q].astype(F32)
        o_ref[...] = acc

    return _call(
        body, name=name, grid=(rows // tr,),
        in_specs=[pl.BlockSpec((N_DEV, tr, cols), lambda i: (0, i, 0))],
        out_specs=pl.BlockSpec((tr, cols), lambda i: (i, 0)),
        out_shape=jax.ShapeDtypeStruct((rows, cols), F32),
    )(parts)


def _join_halves(sums):
    n = len(sums)

    def copies(ins, outs, sems):
        send_sems, recv_sems, local_sems = sems
        x, y, c = _mesh_pos()
        local, sends, recvs = [], [], []
        for t, (src, dst) in enumerate(zip(ins, outs)):
            h = src.shape[0]
            rows = lambda core, dst=dst, h=h: dst.at[pl.ds(pl.multiple_of(core * h, h), h)]
            local.append(_local(src, rows(c), local_sems.at[t]))
            sends.append(_remote(src, rows(c), send_sems.at[t], recv_sems.at[t], (x, y, 1 - c)))
            recvs.append(_remote(src, rows(1 - c), send_sems.at[t], recv_sems.at[t], (x, y, 1 - c)))
        return local, sends, recvs

    def start(ins, outs, sems):
        local, sends, _ = copies(ins, outs, sems)
        for cp in local + sends:
            cp.start()

    def finish(ins, outs, sems):
        local, sends, recvs = copies(ins, outs, sems)
        for cp in recvs:
            cp.wait_recv()
        for cp in sends:
            cp.wait_send()
        for cp in local:
            cp.wait()

    comm = _Comm(sums, [jax.ShapeDtypeStruct((2 * s.shape[0], s.shape[1]), s.dtype) for s in sums],
                 [_dma_sems(n), _dma_sems(n), _dma_sems(n)], start, finish)
    return _call(None, name="join_halves", comm=comm)()[1]


def _allreduce_small(pack):
    rows, cols = pack.shape

    def body(p_ref, o_ref, slots, send_sems, recv_sems):
        x, y, c = _mesh_pos()
        me = 4 * x + 2 * y + c
        slots[me] = p_ref[...]
        copies = []
        for r in range(1, N_DEV):
            fx, fy, fc = (r >> 2) & 1, (r >> 1) & 1, r & 1
            dev = ((1 - x) if fx else x, (1 - y) if fy else y, (1 - c) if fc else c)
            cp = pltpu.make_async_remote_copy(
                src_ref=p_ref, dst_ref=slots.at[me], send_sem=send_sems.at[r - 1],
                recv_sem=recv_sems.at[r - 1], device_id=dev, device_id_type=MESH)
            cp.start()
            copies.append(cp)
        for cp in copies:
            cp.wait_recv()
        for cp in copies:
            cp.wait_send()
        acc = slots[0]
        for d in range(1, N_DEV):
            acc = acc + slots[d]
        o_ref[...] = acc

    vmem = pl.BlockSpec(memory_space=pltpu.VMEM)
    return _call(
        body, name="allreduce_small", in_specs=[vmem], out_specs=vmem,
        out_shape=jax.ShapeDtypeStruct((rows, cols), F32),
        scratch=[pltpu.VMEM((N_DEV, rows, cols), F32), pltpu.SemaphoreType.DMA((N_DEV - 1,)),
                 pltpu.SemaphoreType.DMA((N_DEV - 1,))],
    )(pack)


def _pad_rows(a, rows=8):
    return jnp.pad(a, ((0, rows - a.shape[0]), (0, 0)))


def kernel(x, positions, norm1_w, w_in, merge_gate_b, ret_gn_w, w_ret_o, lru_conv_w, lru_conv_b, lru_w_r, lru_b_r, lru_w_i, lru_b_i, lru_lambda, w_lru_o, w_out, norm2_w, ffn_w_up, ffn_conv_w, ffn_conv_b, ffn_w_down, norm_f_w, loss_target, m_norm1_w, m_w_in, m_merge_gate_b, m_ret_gn_w, m_w_ret_o, m_lru_conv_w, m_lru_conv_b, m_lru_w_r, m_lru_b_r, m_lru_w_i, m_lru_b_i, m_lru_lambda, m_w_lru_o, m_w_out, m_norm2_w, m_ffn_w_up, m_ffn_conv_w, m_ffn_conv_b, m_ffn_w_down, m_norm_f_w, v_norm1_w, v_w_in, v_merge_gate_b, v_ret_gn_w, v_w_ret_o, v_lru_conv_w, v_lru_conv_b, v_lru_w_r, v_lru_b_r, v_lru_w_i, v_lru_b_i, v_lru_lambda, v_w_lru_o, v_w_out, v_norm2_w, v_ffn_w_up, v_ffn_conv_w, v_ffn_conv_b, v_ffn_w_down, v_norm_f_w):
    names = ["norm1_w", "w_in", "merge_gate_b", "ret_gn_w", "w_ret_o", "lru_conv_w", "lru_conv_b", "lru_w_r",
             "lru_b_r", "lru_w_i", "lru_b_i", "lru_lambda", "w_lru_o", "w_out", "norm2_w", "ffn_w_up",
             "ffn_conv_w", "ffn_conv_b", "ffn_w_down", "norm_f_w"]
    w_args = dict(zip(names, [norm1_w, w_in, merge_gate_b, ret_gn_w, w_ret_o, lru_conv_w, lru_conv_b, lru_w_r,
                              lru_b_r, lru_w_i, lru_b_i, lru_lambda, w_lru_o, w_out, norm2_w, ffn_w_up,
                              ffn_conv_w, ffn_conv_b, ffn_w_down, norm_f_w]))
    m_args = dict(zip(names, [m_norm1_w, m_w_in, m_merge_gate_b, m_ret_gn_w, m_w_ret_o, m_lru_conv_w,
                              m_lru_conv_b, m_lru_w_r, m_lru_b_r, m_lru_w_i, m_lru_b_i, m_lru_lambda, m_w_lru_o,
                              m_w_out, m_norm2_w, m_ffn_w_up, m_ffn_conv_w, m_ffn_conv_b, m_ffn_w_down,
                              m_norm_f_w]))
    v_args = dict(zip(names, [v_norm1_w, v_w_in, v_merge_gate_b, v_ret_gn_w, v_w_ret_o, v_lru_conv_w,
                              v_lru_conv_b, v_lru_w_r, v_lru_b_r, v_lru_w_i, v_lru_b_i, v_lru_lambda, v_w_lru_o,
                              v_w_out, v_norm2_w, v_ffn_w_up, v_ffn_conv_w, v_ffn_conv_b, v_ffn_w_down,
                              v_norm_f_w]))

    bsz, seq, d = x.shape
    m = bsz * seq
    tm = min(MM_ROWS, m)
    tm_fused = min(FUSED_ROWS, m)
    chip = 2 * lax.axis_index("x") + lax.axis_index("y")

    big = ["w_in", "w_ret_o", "w_lru_o", "w_out", "lru_w_r", "lru_w_i", "ffn_w_up", "ffn_w_down"]
    cut = dict(w_in=(1, 0), w_ret_o=(0, 0), w_lru_o=(0, 0), w_out=(0, 0), lru_w_r=(1, 0), lru_w_i=(1, 0),
               ffn_w_up=(1, 0), ffn_w_down=(0, 0))
    later = big[1:]
    shard = {n: w_args[n][0].astype(BF16) for n in big}
    small_pack = jnp.concatenate([
        jnp.pad(merge_gate_b[0], ((0, 6), (0, 512))),
        jnp.pad(lru_conv_w[0], ((0, 4), (0, 512))),
        jnp.pad(lru_b_r[0], ((0, 4), (0, 704))),
        jnp.pad(lru_b_i[0], ((0, 4), (0, 704))),
        jnp.pad(ffn_conv_w[0], ((0, 5), (0, 0))),
    ], axis=0)
    _, (w_in_full, sp) = _call(None, name="gather_w_in",
                               comm=_gather_both([shard["w_in"]], [cut["w_in"]], [small_pack]))()
    wb = {"w_in": w_in_full}
    mgb = jnp.transpose(sp[:, 0:2, 0:256], (1, 0, 2)).reshape(2, D_MODEL)
    lcw = jnp.transpose(sp[:, 8:12, 0:256], (1, 0, 2)).reshape(4, D_MODEL)
    lbr = jnp.transpose(sp[:, 16:20, 0:64], (1, 0, 2)).reshape(1, D_MODEL)
    lbi = jnp.transpose(sp[:, 24:28, 0:64], (1, 0, 2)).reshape(1, D_MODEL)
    fcw = jnp.transpose(sp[:, 32:35, :], (1, 0, 2)).reshape(3, D_FF)
    nfw = norm_f_w.reshape(1, D_MODEL)

    x2d = x.reshape(m, d)
    half = RET_DK // 2
    inv_freq = ROPE_BASE ** (-jnp.arange(half, dtype=F32) / half)
    cos, sin = _rope_tables(positions.reshape(bsz, seq, 1), jnp.concatenate([inv_freq, inv_freq]).reshape(1, RET_DK))
    later_cut = [cut[n] for n in later]
    (proj, h1), bufs = _norm_matmul(x2d, norm1_w, wb["w_in"], name="in_proj", tm=tm, tn=MM_COLS,
                                    comm=_gather_ici([shard[n] for n in later], later_cut))
    proj3 = proj.reshape(bsz, seq, D_IN)
    (a_in3, states), bufs = _retention_fwd(
        proj3, cos, sin, ret_gn_w,
        comm=_gather_d2d(bufs, later_cut, [shard[n].shape[cut[n][0]] for n in later]))
    wb.update(zip(later, bufs))
    lru_params = (lcw, lru_conv_b, wb["lru_w_r"], lbr, wb["lru_w_i"], lbi, lru_lambda)
    b_in3, hseq = _lru_fwd(proj3, lru_params)
    a_in, b_in = a_in3.reshape(m, d), b_in3.reshape(m, d)
    x2 = _merge_fwd(a_in, b_in, proj, mgb, wb["w_ret_o"], wb["w_lru_o"], wb["w_out"], x2d, tm=tm_fused)
    up, h2 = _norm_matmul(x2, norm2_w, wb["ffn_w_up"], name="ffn_up", tm=tm, tn=MM_COLS)
    up3 = up.reshape(bsz, seq, 2 * D_FF)
    f3 = _ffn_act_fwd(up3, fcw, ffn_conv_b)
    f = f3.reshape(m, D_FF)
    loss_dev, dx3, sm_nf = _ffn_down_loss(f, wb["ffn_w_down"], x2, nfw, loss_target.reshape(m, d), tm=tm_fused)
    loss = lax.psum(loss_dev[0, 0], ("x", "y", "c"))

    def send(*ns):
        return _exchange([g_full[n] for n in ns], [cut[n] for n in ns])

    g_full, parts = {}, {}
    df = _mm_nt(dx3, wb["ffn_w_down"], name="ffn_down_dx", tm=tm, out_dtype=BF16)
    g_full["ffn_w_down"] = _mm_tn(f, [dx3], name="ffn_down_dw", tm=tm)
    (dgate3, dval3, sm_ffn), (parts["ffn_w_down"],) = _ffn_act_bwd(
        up3, fcw, ffn_conv_b, df.reshape(bsz, seq, D_FF), comm=send("ffn_w_down"))
    dup = [dgate3.reshape(m, D_FF), dval3.reshape(m, D_FF)]
    g_full["ffn_w_up"] = _mm_tn(h2, dup, name="ffn_up_dw", tm=tm)
    (dx2, sm_n2), (parts["ffn_w_up"],) = _mm_nt_normbwd(
        dup, wb["ffn_w_up"], x2, norm2_w, dx3, name="ffn_up_dx", tm=tm, comm=send("ffn_w_up"))
    mix, dya, dyb, da_in, db_in, dgates, sm_mg = _merge_bwd(
        a_in, b_in, proj, mgb, wb["w_ret_o"], wb["w_lru_o"], wb["w_out"], dx2, tm=tm_fused)
    g_full["w_out"] = _mm_tn(mix, [dx2], name="out_dw", tm=tm)
    g_full["w_ret_o"] = _mm_tn(a_in, [dya], name="ret_o_dw", tm=tm)
    g_full["w_lru_o"] = _mm_tn(b_in, [dyb], name="lru_o_dw", tm=tm)
    (dlru3, dwr, dwi, sm_lru), (parts["w_out"], parts["w_ret_o"], parts["w_lru_o"]) = _lru_bwd(
        proj3, lru_params, hseq, db_in.reshape(bsz, seq, d), comm=send("w_out", "w_ret_o", "w_lru_o"))
    g_full["lru_w_r"], g_full["lru_w_i"] = dwr.astype(BF16), dwi.astype(BF16)
    (dret3, sm_gn), (parts["lru_w_r"], parts["lru_w_i"]) = _retention_bwd(
        proj3, cos, sin, ret_gn_w, states, da_in.reshape(bsz, seq, d), comm=send("lru_w_r", "lru_w_i"))
    dproj = [dret3.reshape(m, 3072), dlru3.reshape(m, 2048), dgates]
    g_full["w_in"] = _mm_tn(h1, dproj, name="in_proj_dw", tm=tm)
    (grad_x, sm_n1), (parts["w_in"],) = _mm_nt_normbwd(
        dproj, wb["w_in"], x2d, norm1_w, dx2, name="in_proj_dx", tm=tm, comm=send("w_in"))

    sums = []
    for n in big:
        p = parts[n]
        sums.append(_sum_devices(p.reshape(N_DEV, -1, p.shape[-1]), name="sum_" + n))
    reduced = _join_halves(sums)
    pack = jnp.concatenate(
        [sm_n1, sm_mg, sm_gn, sm_lru, sm_n2, sm_ffn[:, 0:1024], sm_ffn[:, 1024:2048], sm_ffn[:, 2048:3072], sm_nf],
        axis=0)
    tot = _allreduce_small(pack)
    ffn_sm = jnp.concatenate([tot[40:48], tot[48:56], tot[56:64]], axis=1)
    g_small = {
        "norm1_w": tot[0:1], "merge_gate_b": tot[8:10], "ret_gn_w": tot[16:17], "lru_conv_w": tot[24:28],
        "lru_conv_b": tot[28:29], "lru_b_r": tot[29:30].reshape(4, 256), "lru_b_i": tot[30:31].reshape(4, 256),
        "lru_lambda": tot[31:32], "norm2_w": tot[32:33], "ffn_conv_w": ffn_sm[0:3], "ffn_conv_b": ffn_sm[3:4],
        "norm_f_w": tot[64:65],
    }
    small_shard = dict(merge_gate_b=256, lru_conv_w=256, lru_b_r=64, lru_b_i=64, ffn_conv_w=768)

    outs = {}
    for n, g in zip(big, reduced):
        shape = w_args[n].shape
        outs[n] = [o.reshape(shape) for o in _adamw(
            w_args[n].reshape(g.shape), [g], m_args[n].reshape(g.shape), v_args[n].reshape(g.shape),
            name="adamw_" + n)]
    for n, g in g_small.items():
        shape = w_args[n].shape
        if n in small_shard:
            g = lax.dynamic_slice_in_dim(g, chip * small_shard[n], small_shard[n], axis=1)
        w2 = w_args[n].reshape(g.shape)
        outs[n] = [o.reshape(shape) for o in _adamw(
            w2, [g], m_args[n].reshape(g.shape), v_args[n].reshape(g.shape), name="adamw_" + n)]

    result = [loss, grad_x.reshape(bsz, seq, d)]
    for k in range(4):
        result += [outs[n][k] for n in names]
    return tuple(result)
```

```python
import functools
import math

import numpy as np
import jax
import jax.numpy as jnp
from jax import lax
from jax.experimental import pallas as pl
from jax.experimental.pallas import tpu as pltpu

F32 = jnp.float32
BF16 = jnp.bfloat16

D_MODEL = 1024
RET_HEADS = 4
RET_DK = 128
RET_DV = 256
LRU_BLOCKS = 4
LRU_BLOCK = 256
LRU_C = 8.0
D_FF = 3072
D_IN = 7168
ROPE_BASE = 10000.0
RMS_EPS = 1e-6
GN_EPS = 1e-6
ADAM_LR, ADAM_B1, ADAM_B2, ADAM_EPS, ADAM_WD, ADAM_STEP = 0.001, 0.9, 0.999, 1e-08, 0.01, 10

N_CHIPS = 4
N_DEV = 8
SEQ_T = 256
REF_CHUNK = 64
COL = 1024
MM_ROWS = 1024
MM_COLS = 1024
FUSED_ROWS = 512
VMEM_LIMIT_BYTES = 56 * 1024 * 1024
MESH = pl.DeviceIdType.MESH
GELU_K = math.sqrt(2.0 / math.pi)
GELU_C = 0.044715


class _Comm:
    def __init__(self, ins, outs, sems, start, finish, aliases=None):
        self.ins, self.outs, self.sems = list(ins), list(outs), list(sems)
        self.start, self.finish, self.aliases = start, finish, dict(aliases or {})


def _call(body, *, name, out_shape=(), grid=None, in_specs=(), out_specs=(), scratch=(), comm=None, prefetch=0):
    single = not isinstance(out_shape, (list, tuple))
    out_shape = [out_shape] if single else list(out_shape)
    out_specs = [out_specs] if single else list(out_specs)
    in_specs, scratch = list(in_specs), list(scratch)
    n_in, n_out, n_scr = len(in_specs), len(out_shape), len(scratch)
    kwargs = dict(name=name, compiler_params=pltpu.CompilerParams(vmem_limit_bytes=VMEM_LIMIT_BYTES))
    if prefetch:
        assert comm is None
        spec = pltpu.PrefetchScalarGridSpec(num_scalar_prefetch=prefetch, grid=grid, in_specs=in_specs,
                                            out_specs=out_specs, scratch_shapes=scratch)
        fn = pl.pallas_call(body, out_shape=out_shape, grid_spec=spec, **kwargs)
        return (lambda *args: fn(*args)[0]) if single else fn
    if grid is not None:
        kwargs["grid"] = grid
    if comm is None:
        fn = pl.pallas_call(body, out_shape=out_shape, in_specs=in_specs, out_specs=out_specs,
                            scratch_shapes=scratch, **kwargs)
        return (lambda *args: fn(*args)[0]) if single else fn

    any_spec = pl.BlockSpec(memory_space=pl.ANY)
    n_cin, n_cout = len(comm.ins), len(comm.outs)

    def wrapped(*refs):
        ins, refs = refs[:n_in], refs[n_in:]
        cins, refs = refs[:n_cin], refs[n_cin:]
        outs, refs = refs[:n_out], refs[n_out:]
        couts, refs = refs[:n_cout], refs[n_cout:]
        scr, csems = refs[:n_scr], refs[n_scr:]
        if grid is None:
            comm.start(cins, couts, csems)
            comm.finish(cins, couts, csems)
            return
        ids = [pl.program_id(a) for a in range(len(grid))]
        first = functools.reduce(jnp.logical_and, [i == 0 for i in ids])
        last = functools.reduce(jnp.logical_and, [i == g - 1 for i, g in zip(ids, grid)])
        pl.when(first)(lambda: comm.start(cins, couts, csems))
        body(*ins, *outs, *scr)
        pl.when(last)(lambda: comm.finish(cins, couts, csems))

    fn = pl.pallas_call(
        wrapped, out_shape=out_shape + comm.outs, in_specs=in_specs + [any_spec] * n_cin,
        out_specs=out_specs + [any_spec] * n_cout, scratch_shapes=scratch + comm.sems,
        input_output_aliases={n_in + i: n_out + o for i, o in comm.aliases.items()}, **kwargs)

    def run(*args):
        res = fn(*args, *comm.ins)
        own = res[0] if single else list(res[:n_out])
        return own, list(res[n_out:])

    return run


def _dot(a, b):
    return jnp.dot(a, b, preferred_element_type=F32)


def _dot_nt(a, b):
    return lax.dot_general(a, b, (((1,), (1,)), ((), ())), preferred_element_type=F32)


def _dot_tn(a, b):
    return lax.dot_general(a, b, (((0,), (0,)), ((), ())), preferred_element_type=F32)


def _bf(x):
    return x.astype(BF16)


def _sigmoid(x):
    return 1.0 / (1.0 + jnp.exp(-x))


def _gelu(x):
    return 0.5 * x * (1.0 + jnp.tanh(GELU_K * (x + GELU_C * x * x * x)))


def _gelu_and_grad(x):
    t = jnp.tanh(GELU_K * (x + GELU_C * x * x * x))
    g = 0.5 * x * (1.0 + t)
    dg = 0.5 * (1.0 + t) + 0.5 * x * (1.0 - t * t) * (GELU_K * (1.0 + 3.0 * GELU_C * x * x))
    return g, dg


def _rms(x):
    r = lax.rsqrt(jnp.mean(x * x, axis=-1, keepdims=True) + RMS_EPS)
    return x * r, r


def _rms_bwd(dy, x, nw):
    xh, r = _rms(x)
    g = dy * nw
    dx = r * (g - xh * jnp.mean(g * xh, axis=-1, keepdims=True))
    return dx, jnp.sum(dy * xh, axis=0, keepdims=True)


def _row_acc(ref, row, val):
    ref[row:row + 1, :] = ref[row:row + 1, :] + val


def _shift_down(x, j, prev8):
    if j == 0:
        return x
    r = pltpu.roll(x, j, 0)
    rh = pltpu.roll(prev8, j, 0)
    row = lax.broadcasted_iota(jnp.int32, rh.shape, 0)
    top = jnp.where(row < j, rh, r[0:8])
    return jnp.concatenate([top, r[8:]], axis=0)


def _shift_up(x, j, next8):
    if j == 0:
        return x
    n = x.shape[0]
    r = pltpu.roll(x, n - j, 0)
    rh = pltpu.roll(next8, 8 - j, 0)
    row = lax.broadcasted_iota(jnp.int32, rh.shape, 0)
    bot = jnp.where(row >= 8 - j, rh, r[n - 8:])
    return jnp.concatenate([r[:n - 8], bot], axis=0)


def _scan_fwd(a, b):
    n = a.shape[0]
    row = lax.broadcasted_iota(jnp.int32, a.shape, 0)
    s = 1
    while s < n:
        keep = row >= s
        ar = pltpu.roll(a, s, 0)
        br = pltpu.roll(b, s, 0)
        b = jnp.where(keep, a * br + b, b)
        a = jnp.where(keep, a * ar, a)
        s *= 2
    return a, b


def _scan_bwd(a, b):
    n = a.shape[0]
    row = lax.broadcasted_iota(jnp.int32, a.shape, 0)
    s = 1
    while s < n:
        keep = row < n - s
        ar = pltpu.roll(a, n - s, 0)
        br = pltpu.roll(b, n - s, 0)
        b = jnp.where(keep, a * br + b, b)
        a = jnp.where(keep, a * ar, a)
        s *= 2
    return a, b


def _norm_matmul(x, nw, w, *, name, tm, tn, comm=None):
    m, d = x.shape
    n = w.shape[1]

    def body(x_ref, nw_ref, w_ref, o_ref, h_ref, h_sc):
        @pl.when(pl.program_id(1) == 0)
        def _():
            xh, _ = _rms(x_ref[...])
            h = _bf(xh * nw_ref[...])
            h_sc[...] = h
            h_ref[...] = h

        o_ref[...] = _dot(h_sc[...], w_ref[...])

    return _call(
        body, name=name, grid=(m // tm, n // tn), comm=comm,
        in_specs=[pl.BlockSpec((tm, d), lambda i, j: (i, 0)),
                  pl.BlockSpec((1, d), lambda i, j: (0, 0)),
                  pl.BlockSpec((d, tn), lambda i, j: (0, j))],
        out_specs=[pl.BlockSpec((tm, tn), lambda i, j: (i, j)),
                   pl.BlockSpec((tm, d), lambda i, j: (i, 0))],
        out_shape=[jax.ShapeDtypeStruct((m, n), F32), jax.ShapeDtypeStruct((m, d), BF16)],
        scratch=[pltpu.VMEM((tm, d), BF16)],
    )(x, nw, w)


def _mm_nt(a, w, *, name, tm, out_dtype):
    m, k = a.shape
    n = w.shape[0]

    def body(a_ref, w_ref, o_ref):
        o_ref[...] = _dot_nt(_bf(a_ref[...]), w_ref[...]).astype(out_dtype)

    return _call(
        body, name=name, grid=(m // tm, n // COL),
        in_specs=[pl.BlockSpec((tm, k), lambda i, j: (i, 0)),
                  pl.BlockSpec((COL, k), lambda i, j: (j, 0))],
        out_specs=pl.BlockSpec((tm, COL), lambda i, j: (i, j)),
        out_shape=jax.ShapeDtypeStruct((m, n), out_dtype),
    )(a, w)


def _piece_layout(pieces):
    offs, nblk, o = [], [], 0
    for p in pieces:
        offs.append(o)
        nblk.append(p.shape[1] // COL)
        o += p.shape[1] // COL
    return offs, nblk, o


def _mm_tn(a, pieces, *, name, tm, out_dtype=BF16):
    m, k = a.shape
    offs, nblk, nn = _piece_layout(pieces)

    def piece_spec(o, nb):
        def idx(ki, nj, mi):
            use = jnp.logical_and(nj >= o, nj < o + nb)
            return (jnp.where(use, mi, 0), jnp.clip(nj - o, 0, nb - 1))
        return pl.BlockSpec((tm, COL), idx)

    def body(a_ref, *rest):
        p_refs, o_ref, acc = rest[:len(pieces)], rest[len(pieces)], rest[len(pieces) + 1]
        nj, mi = pl.program_id(1), pl.program_id(2)

        @pl.when(mi == 0)
        def _():
            acc[...] = jnp.zeros_like(acc)

        for p_ref, o, nb in zip(p_refs, offs, nblk):
            @pl.when(jnp.logical_and(nj >= o, nj < o + nb))
            def _(p_ref=p_ref):
                acc[...] += _dot_tn(_bf(a_ref[...]), _bf(p_ref[...]))

        @pl.when(mi == pl.num_programs(2) - 1)
        def _():
            o_ref[...] = acc[...].astype(out_dtype)

    return _call(
        body, name=name, grid=(k // COL, nn, m // tm),
        in_specs=[pl.BlockSpec((tm, COL), lambda ki, nj, mi: (mi, ki))]
        + [piece_spec(o, nb) for o, nb in zip(offs, nblk)],
        out_specs=pl.BlockSpec((COL, COL), lambda ki, nj, mi: (ki, nj)),
        out_shape=jax.ShapeDtypeStruct((k, nn * COL), out_dtype),
        scratch=[pltpu.VMEM((COL, COL), F32)],
    )(a, *pieces)


def _mm_nt_normbwd(pieces, w, x, nw, dres, *, name, tm, comm=None):
    m, d = x.shape
    offs, nblk, nk = _piece_layout(pieces)

    def piece_spec(o, nb):
        return pl.BlockSpec((tm, COL), lambda i, k: (i, jnp.clip(k - o, 0, nb - 1)))

    def body(*refs):
        p_refs = refs[:len(pieces)]
        w_ref, x_ref, nw_ref, dres_ref, dx_ref, dnw_ref, acc = refs[len(pieces):]
        i, k = pl.program_id(0), pl.program_id(1)

        @pl.when(jnp.logical_and(i == 0, k == 0))
        def _():
            dnw_ref[...] = jnp.zeros_like(dnw_ref)

        @pl.when(k == 0)
        def _():
            acc[...] = jnp.zeros_like(acc)

        for p_ref, o, nb in zip(p_refs, offs, nblk):
            @pl.when(jnp.logical_and(k >= o, k < o + nb))
            def _(p_ref=p_ref):
                acc[...] += _dot_nt(_bf(p_ref[...]), w_ref[...])

        @pl.when(k == nk - 1)
        def _():
            dx, dnw = _rms_bwd(acc[...], x_ref[...], nw_ref[...])
            dx_ref[...] = dres_ref[...] + dx
            _row_acc(dnw_ref, 0, dnw)

    return _call(
        body, name=name, grid=(m // tm, nk), comm=comm,
        in_specs=[piece_spec(o, nb) for o, nb in zip(offs, nblk)]
        + [pl.BlockSpec((d, COL), lambda i, k: (0, k)),
           pl.BlockSpec((tm, d), lambda i, k: (i, 0)),
           pl.BlockSpec((1, d), lambda i, k: (0, 0)),
           pl.BlockSpec((tm, d), lambda i, k: (i, 0))],
        out_specs=[pl.BlockSpec((tm, d), lambda i, k: (i, 0)),
                   pl.BlockSpec((8, d), lambda i, k: (0, 0))],
        out_shape=[jax.ShapeDtypeStruct((m, d), F32), jax.ShapeDtypeStruct((8, d), F32)],
        scratch=[pltpu.VMEM((tm, d), F32)],
    )(*pieces, w, x, nw, dres)


def _rope_tables(pos3, invf):
    b, s, _ = pos3.shape

    def body(pos_ref, invf_ref, cos_ref, sin_ref):
        ang = pos_ref[...].astype(F32) * invf_ref[...]
        lane = lax.broadcasted_iota(jnp.int32, ang.shape, 1)
        cos_ref[...] = jnp.cos(ang)
        sin_ref[...] = jnp.where(lane < RET_DK // 2, -1.0, 1.0) * jnp.sin(ang)

    spec = pl.BlockSpec((None, SEQ_T, RET_DK), lambda i, c: (i, c, 0))
    return _call(
        body, name="rope_tables", grid=(b, s // SEQ_T),
        in_specs=[pl.BlockSpec((None, SEQ_T, 1), lambda i, c: (i, c, 0)),
                  pl.BlockSpec((1, RET_DK), lambda i, c: (0, 0))],
        out_specs=[spec, spec],
        out_shape=[jax.ShapeDtypeStruct((b, s, RET_DK), F32)] * 2,
    )(pos3, invf)


def _log_gamma(h):
    return float(np.log1p(-np.power(np.float32(2.0), np.float32(-5.0 - h))).astype(np.float32))


def _decay_matrix(h):
    lg = _log_gamma(h)
    n = lax.broadcasted_iota(jnp.int32, (SEQ_T, SEQ_T), 0)
    m = lax.broadcasted_iota(jnp.int32, (SEQ_T, SEQ_T), 1)
    same = (n // REF_CHUNK) == (m // REF_CHUNK)
    dist = jnp.where(same, jnp.abs(n - m), n - m).astype(F32)
    return jnp.where(jnp.logical_or(same, m < n), jnp.exp(lg * dist), 0.0)


def _decay_vectors(h):
    lg = _log_gamma(h)
    idx = lax.broadcasted_iota(jnp.int32, (SEQ_T, 1), 0).astype(F32)
    qd = jnp.exp(lg * (idx + 1.0))
    kd = jnp.exp(lg * (SEQ_T - 1.0 - idx))
    return qd, kd, math.exp(lg * SEQ_T)


def _rotate(x, cos, sin):
    return x * cos + pltpu.roll(x, RET_DK // 2, 1) * sin


def _rotate_bwd(d, cos, sin):
    return d * cos + pltpu.roll(d * sin, RET_DK // 2, 1)


def _ret_head(p_ref, cos, sin, h):
    q = p_ref[:, h * RET_DK:(h + 1) * RET_DK]
    k = p_ref[:, 512 + h * RET_DK:512 + (h + 1) * RET_DK]
    v = p_ref[:, 1024 + h * RET_DV:1024 + (h + 1) * RET_DV]
    g = p_ref[:, 2048 + h * RET_DV:2048 + (h + 1) * RET_DV]
    qr = _rotate(q, cos, sin)
    kr = _rotate(k, cos, sin) * (RET_DK ** -0.5)
    return qr, kr, v, g


def _group_norm(o):
    mu = jnp.mean(o, axis=-1, keepdims=True)
    oc = o - mu
    rstd = lax.rsqrt(jnp.mean(oc * oc, axis=-1, keepdims=True) + GN_EPS)
    return oc * rstd, rstd


def _retention_fwd(proj3, cos, sin, gnw, comm=None):
    b, s, _ = proj3.shape
    nc = s // SEQ_T

    def body(p_ref, cos_ref, sin_ref, gnw_ref, a_ref, st_ref, state, wtab):
        c = pl.program_id(1)

        @pl.when(jnp.logical_and(pl.program_id(0) == 0, c == 0))
        def _():
            for h in range(RET_HEADS):
                wtab[h] = _decay_matrix(h)

        @pl.when(c == 0)
        def _():
            state[...] = jnp.zeros_like(state)

        cs, sn = cos_ref[...], sin_ref[...]
        for h in range(RET_HEADS):
            qd, kd, gt = _decay_vectors(h)
            qr, kr, v, g = _ret_head(p_ref, cs, sn, h)
            st = state[h]
            st_ref[h] = st
            p = _dot_nt(_bf(qr), _bf(kr)) * wtab[h]
            o = _dot(_bf(p), _bf(v)) + _dot(_bf(qr * qd), _bf(st))
            state[h] = st * gt + _dot_tn(_bf(kr * kd), _bf(v))
            on, _ = _group_norm(o)
            gw = gnw_ref[:, h * RET_DV:(h + 1) * RET_DV]
            a_ref[:, h * RET_DV:(h + 1) * RET_DV] = _bf(on * gw * (g * _sigmoid(g)))

    tab = pl.BlockSpec((None, SEQ_T, RET_DK), lambda i, c: (i, c, 0))
    return _call(
        body, name="retention_fwd", grid=(b, nc), comm=comm,
        in_specs=[pl.BlockSpec((None, SEQ_T, 3072), lambda i, c: (i, c, 0)), tab, tab,
                  pl.BlockSpec((1, D_MODEL), lambda i, c: (0, 0))],
        out_specs=[pl.BlockSpec((None, SEQ_T, D_MODEL), lambda i, c: (i, c, 0)),
                   pl.BlockSpec((None, None, RET_HEADS, RET_DK, RET_DV), lambda i, c: (i, c, 0, 0, 0))],
        out_shape=[jax.ShapeDtypeStruct((b, s, D_MODEL), BF16),
                   jax.ShapeDtypeStruct((b, nc, RET_HEADS, RET_DK, RET_DV), F32)],
        scratch=[pltpu.VMEM((RET_HEADS, RET_DK, RET_DV), F32),
                 pltpu.VMEM((RET_HEADS, SEQ_T, SEQ_T), F32)],
    )(proj3, cos, sin, gnw)


def _retention_bwd(proj3, cos, sin, gnw, states, da3, comm=None):
    b, s, _ = proj3.shape
    nc = s // SEQ_T

    def body(p_ref, cos_ref, sin_ref, gnw_ref, st_ref, da_ref, d_ref, dgn_ref, dstate, wtab):
        c = pl.program_id(1)

        @pl.when(jnp.logical_and(pl.program_id(0) == 0, c == 0))
        def _():
            dgn_ref[...] = jnp.zeros_like(dgn_ref)
            for h in range(RET_HEADS):
                wtab[h] = _decay_matrix(h)

        @pl.when(c == 0)
        def _():
            dstate[...] = jnp.zeros_like(dstate)

        cs, sn = cos_ref[...], sin_ref[...]
        for h in range(RET_HEADS):
            qd, kd, gt = _decay_vectors(h)
            qr, kr, v, g = _ret_head(p_ref, cs, sn, h)
            st, dst, w = st_ref[h], dstate[h], wtab[h]
            qb, kb, vb = _bf(qr), _bf(kr), _bf(v)
            p = _dot_nt(qb, kb) * w
            o = _dot(_bf(p), vb) + _dot(_bf(qr * qd), _bf(st))
            on, rstd = _group_norm(o)
            gw = gnw_ref[:, h * RET_DV:(h + 1) * RET_DV]
            da = da_ref[:, h * RET_DV:(h + 1) * RET_DV].astype(F32)
            sg = _sigmoid(g)
            silu = g * sg
            dg = da * on * gw * (sg * (1.0 + g * (1.0 - sg)))
            dgn_ref[0:1, h * RET_DV:(h + 1) * RET_DV] += jnp.sum(da * silu * on, axis=0, keepdims=True)
            don = da * silu * gw
            do = rstd * (don - jnp.mean(don, axis=-1, keepdims=True)
                         - on * jnp.mean(don * on, axis=-1, keepdims=True))
            dob = _bf(do)
            dp = _dot_nt(dob, vb) * w
            dqr = _dot(_bf(dp), kb) + _dot_nt(dob, _bf(st)) * qd
            dkr = _dot_tn(_bf(dp), qb) + _dot_nt(vb, _bf(dst)) * kd
            dv = _dot_tn(_bf(p), dob) + _dot(_bf(kr * kd), _bf(dst))
            dstate[h] = dst * gt + _dot_tn(_bf(qr * qd), dob)
            d_ref[:, h * RET_DK:(h + 1) * RET_DK] = _bf(_rotate_bwd(dqr, cs, sn))
            d_ref[:, 512 + h * RET_DK:512 + (h + 1) * RET_DK] = _bf(_rotate_bwd(dkr, cs, sn) * (RET_DK ** -0.5))
            d_ref[:, 1024 + h * RET_DV:1024 + (h + 1) * RET_DV] = _bf(dv)
            d_ref[:, 2048 + h * RET_DV:2048 + (h + 1) * RET_DV] = _bf(dg)

    rev = lambda i, c: (i, nc - 1 - c, 0)
    tab = pl.BlockSpec((None, SEQ_T, RET_DK), rev)
    return _call(
        body, name="retention_bwd", grid=(b, nc), comm=comm,
        in_specs=[pl.BlockSpec((None, SEQ_T, 3072), rev), tab, tab,
                  pl.BlockSpec((1, D_MODEL), lambda i, c: (0, 0)),
                  pl.BlockSpec((None, None, RET_HEADS, RET_DK, RET_DV), lambda i, c: (i, nc - 1 - c, 0, 0, 0)),
                  pl.BlockSpec((None, SEQ_T, D_MODEL), rev)],
        out_specs=[pl.BlockSpec((None, SEQ_T, 3072), rev),
                   pl.BlockSpec((8, D_MODEL), lambda i, c: (0, 0))],
        out_shape=[jax.ShapeDtypeStruct((b, s, 3072), BF16), jax.ShapeDtypeStruct((8, D_MODEL), F32)],
        scratch=[pltpu.VMEM((RET_HEADS, RET_DK, RET_DV), F32),
                 pltpu.VMEM((RET_HEADS, SEQ_T, SEQ_T), F32)],
    )(proj3, cos, sin, gnw, states, da3)


def _softplus_neg(lam):
    z = -lam
    u = jnp.exp(-jnp.abs(z))
    log1p_u = jnp.where(u < 0.01, u * (1.0 - u * (0.5 - u * (1.0 / 3.0))), jnp.log(1.0 + u))
    return jnp.maximum(z, 0.0) + log1p_u


def _lru_gates(x, prev8, cw_ref, cb_ref, wr_ref, br_ref, wi_ref, bi_ref, lam_ref):
    xc = cb_ref[...] + sum(cw_ref[pl.ds(j, 1), :] * _shift_down(x, 3 - j, prev8) for j in range(4))
    rs, is_ = [], []
    for n in range(LRU_BLOCKS):
        xb = _bf(xc[:, n * LRU_BLOCK:(n + 1) * LRU_BLOCK])
        cols = slice(n * LRU_BLOCK, (n + 1) * LRU_BLOCK)
        rs.append(_sigmoid(_dot(xb, wr_ref[n]) + br_ref[:, cols]))
        is_.append(_sigmoid(_dot(xb, wi_ref[n]) + bi_ref[:, cols]))
    r = jnp.concatenate(rs, axis=1)
    i = jnp.concatenate(is_, axis=1)
    sp = _softplus_neg(lam_ref[...])
    la = -LRU_C * r * sp
    a = jnp.exp(la)
    s = jnp.sqrt(-jnp.tanh(la) * (a * a + 1.0))
    return xc, r, i, a, s, sp


_LRU_PARAM_SPECS = [
    pl.BlockSpec((4, D_MODEL), lambda i, c: (0, 0)),
    pl.BlockSpec((1, D_MODEL), lambda i, c: (0, 0)),
    pl.BlockSpec((LRU_BLOCKS, LRU_BLOCK, LRU_BLOCK), lambda i, c: (0, 0, 0)),
    pl.BlockSpec((1, D_MODEL), lambda i, c: (0, 0)),
    pl.BlockSpec((LRU_BLOCKS, LRU_BLOCK, LRU_BLOCK), lambda i, c: (0, 0, 0)),
    pl.BlockSpec((1, D_MODEL), lambda i, c: (0, 0)),
    pl.BlockSpec((1, D_MODEL), lambda i, c: (0, 0)),
]


def _lru_fwd(proj3, params):
    b, s, _ = proj3.shape
    nc = s // SEQ_T

    def body(x_ref, y_ref, cw, cb, wr, br, wi, bi, lam, o_ref, h_ref, xprev, hprev):
        @pl.when(pl.program_id(1) == 0)
        def _():
            xprev[...] = jnp.zeros_like(xprev)
            hprev[...] = jnp.zeros_like(hprev)

        x = x_ref[...]
        xc, r, i, a, s_, _ = _lru_gates(x, xprev[...], cw, cb, wr, br, wi, bi, lam)
        xprev[...] = x[SEQ_T - 8:]
        acum, bcum = _scan_fwd(a, s_ * (i * xc))
        h = acum * hprev[7:8, :] + bcum
        hprev[...] = h[SEQ_T - 8:]
        h_ref[...] = h
        o_ref[...] = _bf(h * _gelu(y_ref[...]))

    return _call(
        body, name="lru_fwd", grid=(b, nc),
        in_specs=[pl.BlockSpec((None, SEQ_T, D_MODEL), lambda i, c: (i, c, 3)),
                  pl.BlockSpec((None, SEQ_T, D_MODEL), lambda i, c: (i, c, 4))] + _LRU_PARAM_SPECS,
        out_specs=[pl.BlockSpec((None, SEQ_T, D_MODEL), lambda i, c: (i, c, 0))] * 2,
        out_shape=[jax.ShapeDtypeStruct((b, s, D_MODEL), BF16), jax.ShapeDtypeStruct((b, s, D_MODEL), F32)],
        scratch=[pltpu.VMEM((8, D_MODEL), F32), pltpu.VMEM((8, D_MODEL), F32)],
    )(proj3, proj3, *params)


def _lru_bwd(proj3, params, hseq, db3, comm=None):
    b, s, _ = proj3.shape
    nc = s // SEQ_T
    blk8 = SEQ_T // 8

    def body(x_ref, y_ref, xp_ref, h_ref, hp_ref, db_ref, cw, cb, wr, br, wi, bi, lam,
             d_ref, dwr_ref, dwi_ref, sm_ref, gnext, anext, dxcnext):
        c = pl.program_id(1)
        first_chunk = c == nc - 1

        @pl.when(jnp.logical_and(pl.program_id(0) == 0, c == 0))
        def _():
            dwr_ref[...] = jnp.zeros_like(dwr_ref)
            dwi_ref[...] = jnp.zeros_like(dwi_ref)
            sm_ref[...] = jnp.zeros_like(sm_ref)

        @pl.when(c == 0)
        def _():
            gnext[...] = jnp.zeros_like(gnext)
            anext[...] = jnp.zeros_like(anext)
            dxcnext[...] = jnp.zeros_like(dxcnext)

        x, y, h = x_ref[...], y_ref[...], h_ref[...]
        keep_prev = jnp.where(first_chunk, 0.0, 1.0)
        xprev = xp_ref[...] * keep_prev
        hprev = hp_ref[...] * keep_prev
        xc, r, i, a, s_, sp = _lru_gates(x, xprev, cw, cb, wr, br, wi, bi, lam)
        db = db_ref[...].astype(F32)
        gy, dgy = _gelu_and_grad(y)
        dy = db * h * dgy
        a_up = _shift_up(a, 1, anext[...])
        acum, ucum = _scan_bwd(a_up, db * gy)
        g = ucum + acum * gnext[0:1, :]
        gnext[...] = g[0:8]
        anext[...] = a[0:8]
        da = g * _shift_down(h, 1, hprev)
        ixc = i * xc
        dla = da * a - (g * ixc) * (a * a) / s_
        di = g * s_ * xc
        dxc = g * s_ * i
        dzr = dla * (-LRU_C * sp) * r * (1.0 - r)
        dzi = di * i * (1.0 - i)
        lam_v = lam[...]
        _row_acc(sm_ref, 7, jnp.sum(dla * (LRU_C * r), axis=0, keepdims=True) * _sigmoid(-lam_v))
        _row_acc(sm_ref, 5, jnp.sum(dzr, axis=0, keepdims=True))
        _row_acc(sm_ref, 6, jnp.sum(dzi, axis=0, keepdims=True))
        parts = []
        for n in range(LRU_BLOCKS):
            cols = slice(n * LRU_BLOCK, (n + 1) * LRU_BLOCK)
            xb, zr, zi = _bf(xc[:, cols]), _bf(dzr[:, cols]), _bf(dzi[:, cols])
            parts.append(dxc[:, cols] + _dot_nt(zr, wr[n]) + _dot_nt(zi, wi[n]))
            dwr_ref[n] += _dot_tn(xb, zr)
            dwi_ref[n] += _dot_tn(xb, zi)
        dxc = jnp.concatenate(parts, axis=1)
        _row_acc(sm_ref, 4, jnp.sum(dxc, axis=0, keepdims=True))
        nxt = dxcnext[...]
        dx = jnp.zeros_like(x)
        for j in range(4):
            dx = dx + cw[pl.ds(j, 1), :] * _shift_up(dxc, 3 - j, nxt)
            _row_acc(sm_ref, j, jnp.sum(dxc * _shift_down(x, 3 - j, xprev), axis=0, keepdims=True))
        dxcnext[...] = dxc[0:8]
        d_ref[:, 0:D_MODEL] = _bf(dx)
        d_ref[:, D_MODEL:2 * D_MODEL] = _bf(dy)

    rev = lambda col: (lambda i, c: (i, nc - 1 - c, col))
    prev = lambda col: (lambda i, c: (i, jnp.maximum((nc - 1 - c) * blk8 - 1, 0), col))
    return _call(
        body, name="lru_bwd", grid=(b, nc), comm=comm,
        in_specs=[pl.BlockSpec((None, SEQ_T, D_MODEL), rev(3)),
                  pl.BlockSpec((None, SEQ_T, D_MODEL), rev(4)),
                  pl.BlockSpec((None, 8, D_MODEL), prev(3)),
                  pl.BlockSpec((None, SEQ_T, D_MODEL), rev(0)),
                  pl.BlockSpec((None, 8, D_MODEL), prev(0)),
                  pl.BlockSpec((None, SEQ_T, D_MODEL), rev(0))] + _LRU_PARAM_SPECS,
        out_specs=[pl.BlockSpec((None, SEQ_T, 2 * D_MODEL), rev(0)),
                   pl.BlockSpec((LRU_BLOCKS, LRU_BLOCK, LRU_BLOCK), lambda i, c: (0, 0, 0)),
                   pl.BlockSpec((LRU_BLOCKS, LRU_BLOCK, LRU_BLOCK), lambda i, c: (0, 0, 0)),
                   pl.BlockSpec((8, D_MODEL), lambda i, c: (0, 0))],
        out_shape=[jax.ShapeDtypeStruct((b, s, 2 * D_MODEL), BF16),
                   jax.ShapeDtypeStruct((LRU_BLOCKS, LRU_BLOCK, LRU_BLOCK), F32),
                   jax.ShapeDtypeStruct((LRU_BLOCKS, LRU_BLOCK, LRU_BLOCK), F32),
                   jax.ShapeDtypeStruct((8, D_MODEL), F32)],
        scratch=[pltpu.VMEM((8, D_MODEL), F32)] * 3,
    )(proj3, proj3, proj3, hseq, hseq, db3, *params)


def _merge_parts(a_ref, b_ref, gr_ref, gl_ref, mgb_ref, wro_ref, wlo_ref):
    ya = _dot(a_ref[...], wro_ref[...])
    yb = _dot(b_ref[...], wlo_ref[...])
    sa = _sigmoid(gr_ref[...] + mgb_ref[0:1, :])
    sb = _sigmoid(gl_ref[...] + mgb_ref[1:2, :])
    return ya, yb, sa, sb


def _merge_specs(tm):
    row = lambda col: pl.BlockSpec((tm, D_MODEL), lambda i: (i, col))
    full = pl.BlockSpec((D_MODEL, D_MODEL), lambda i: (0, 0))
    return row, full


def _merge_fwd(a_in, b_in, proj, mgb, wro, wlo, wout, x, *, tm):
    m = x.shape[0]
    row, full = _merge_specs(tm)

    def body(a_ref, b_ref, gr_ref, gl_ref, mgb_ref, wro_ref, wlo_ref, wout_ref, x_ref, o_ref):
        ya, yb, sa, sb = _merge_parts(a_ref, b_ref, gr_ref, gl_ref, mgb_ref, wro_ref, wlo_ref)
        o_ref[...] = x_ref[...] + _dot(_bf(sa * ya + sb * yb), wout_ref[...])

    return _call(
        body, name="merge_fwd", grid=(m // tm,),
        in_specs=[row(0), row(0), row(5), row(6), pl.BlockSpec((2, D_MODEL), lambda i: (0, 0)),
                  full, full, full, row(0)],
        out_specs=row(0),
        out_shape=jax.ShapeDtypeStruct((m, D_MODEL), F32),
    )(a_in, b_in, proj, proj, mgb, wro, wlo, wout, x)


def _merge_bwd(a_in, b_in, proj, mgb, wro, wlo, wout, dx2, *, tm, comm=None):
    m = dx2.shape[0]
    row, full = _merge_specs(tm)

    def body(a_ref, b_ref, gr_ref, gl_ref, mgb_ref, wro_ref, wlo_ref, wout_ref, dx_ref,
             mix_ref, dya_ref, dyb_ref, da_ref, db_ref, dg_ref, sm_ref):
        @pl.when(pl.program_id(0) == 0)
        def _():
            sm_ref[...] = jnp.zeros_like(sm_ref)

        ya, yb, sa, sb = _merge_parts(a_ref, b_ref, gr_ref, gl_ref, mgb_ref, wro_ref, wlo_ref)
        mix_ref[...] = _bf(sa * ya + sb * yb)
        dmix = _dot_nt(_bf(dx_ref[...]), wout_ref[...])
        dya, dyb = _bf(dmix * sa), _bf(dmix * sb)
        dya_ref[...] = dya
        dyb_ref[...] = dyb
        dga = dmix * ya * sa * (1.0 - sa)
        dgb = dmix * yb * sb * (1.0 - sb)
        dg_ref[:, 0:D_MODEL] = _bf(dga)
        dg_ref[:, D_MODEL:2 * D_MODEL] = _bf(dgb)
        _row_acc(sm_ref, 0, jnp.sum(dga, axis=0, keepdims=True))
        _row_acc(sm_ref, 1, jnp.sum(dgb, axis=0, keepdims=True))
        da_ref[...] = _bf(_dot_nt(dya, wro_ref[...]))
        db_ref[...] = _bf(_dot_nt(dyb, wlo_ref[...]))

    act = jax.ShapeDtypeStruct((m, D_MODEL), BF16)
    return _call(
        body, name="merge_bwd", grid=(m // tm,), comm=comm,
        in_specs=[row(0), row(0), row(5), row(6), pl.BlockSpec((2, D_MODEL), lambda i: (0, 0)),
                  full, full, full, row(0)],
        out_specs=[row(0)] * 5 + [pl.BlockSpec((tm, 2 * D_MODEL), lambda i: (i, 0)),
                                  pl.BlockSpec((8, D_MODEL), lambda i: (0, 0))],
        out_shape=[act] * 5 + [jax.ShapeDtypeStruct((m, 2 * D_MODEL), BF16),
                               jax.ShapeDtypeStruct((8, D_MODEL), F32)],
    )(a_in, b_in, proj, proj, mgb, wro, wlo, wout, dx2)


def _ffn_conv(gate, prev8, cw_ref, cb_ref):
    return cb_ref[...] + sum(cw_ref[pl.ds(j, 1), :] * _shift_down(gate, 2 - j, prev8) for j in range(3))


def _ffn_act_fwd(up3, cw, cb):
    b, s, _ = up3.shape

    def body(g_ref, v_ref, cw_ref, cb_ref, o_ref, gprev):
        @pl.when(pl.program_id(1) == 0)
        def _():
            gprev[...] = jnp.zeros_like(gprev)

        gate = g_ref[...]
        gc = _ffn_conv(gate, gprev[...], cw_ref, cb_ref)
        gprev[...] = gate[SEQ_T - 8:]
        o_ref[...] = _bf(_gelu(gc) * v_ref[...])

    return _call(
        body, name="ffn_act_fwd", grid=(b, s // SEQ_T),
        in_specs=[pl.BlockSpec((None, SEQ_T, D_FF), lambda i, c: (i, c, 0)),
                  pl.BlockSpec((None, SEQ_T, D_FF), lambda i, c: (i, c, 1)),
                  pl.BlockSpec((3, D_FF), lambda i, c: (0, 0)),
                  pl.BlockSpec((1, D_FF), lambda i, c: (0, 0))],
        out_specs=pl.BlockSpec((None, SEQ_T, D_FF), lambda i, c: (i, c, 0)),
        out_shape=jax.ShapeDtypeStruct((b, s, D_FF), BF16),
        scratch=[pltpu.VMEM((8, D_FF), F32)],
    )(up3, up3, cw, cb)


def _ffn_act_bwd(up3, cw, cb, df3, comm=None):
    b, s, _ = up3.shape
    nc = s // SEQ_T
    blk8 = SEQ_T // 8

    def body(g_ref, v_ref, gp_ref, df_ref, cw_ref, cb_ref, dg_ref, dv_ref, sm_ref, dgcnext):
        c = pl.program_id(1)

        @pl.when(jnp.logical_and(pl.program_id(0) == 0, c == 0))
        def _():
            sm_ref[...] = jnp.zeros_like(sm_ref)

        @pl.when(c == 0)
        def _():
            dgcnext[...] = jnp.zeros_like(dgcnext)

        gate = g_ref[...]
        gprev = gp_ref[...] * jnp.where(c == nc - 1, 0.0, 1.0)
        gc = _ffn_conv(gate, gprev, cw_ref, cb_ref)
        act, dact = _gelu_and_grad(gc)
        df = df_ref[...].astype(F32)
        dv_ref[...] = _bf(df * act)
        dgc = df * v_ref[...] * dact
        nxt = dgcnext[...]
        dgate = jnp.zeros_like(gate)
        for j in range(3):
            dgate = dgate + cw_ref[pl.ds(j, 1), :] * _shift_up(dgc, 2 - j, nxt)
            _row_acc(sm_ref, j, jnp.sum(dgc * _shift_down(gate, 2 - j, gprev), axis=0, keepdims=True))
        _row_acc(sm_ref, 3, jnp.sum(dgc, axis=0, keepdims=True))
        dgcnext[...] = dgc[0:8]
        dg_ref[...] = _bf(dgate)

    rev = lambda col: (lambda i, c: (i, nc - 1 - c, col))
    return _call(
        body, name="ffn_act_bwd", grid=(b, nc), comm=comm,
        in_specs=[pl.BlockSpec((None, SEQ_T, D_FF), rev(0)),
                  pl.BlockSpec((None, SEQ_T, D_FF), rev(1)),
                  pl.BlockSpec((None, 8, D_FF), lambda i, c: (i, jnp.maximum((nc - 1 - c) * blk8 - 1, 0), 0)),
                  pl.BlockSpec((None, SEQ_T, D_FF), rev(0)),
                  pl.BlockSpec((3, D_FF), lambda i, c: (0, 0)),
                  pl.BlockSpec((1, D_FF), lambda i, c: (0, 0))],
        out_specs=[pl.BlockSpec((None, SEQ_T, D_FF), rev(0))] * 2
        + [pl.BlockSpec((8, D_FF), lambda i, c: (0, 0))],
        out_shape=[jax.ShapeDtypeStruct((b, s, D_FF), BF16)] * 2 + [jax.ShapeDtypeStruct((8, D_FF), F32)],
        scratch=[pltpu.VMEM((8, D_FF), F32)],
    )(up3, up3, up3, df3, cw, cb)


def _ffn_down_loss(f, wd, x2, nfw, target, *, tm):
    m, kf = f.shape
    nt = m // tm

    def body(f_ref, wd_ref, x_ref, nw_ref, t_ref, loss_ref, dx_ref, dnw_ref, lsum):
        i = pl.program_id(0)

        @pl.when(i == 0)
        def _():
            dnw_ref[...] = jnp.zeros_like(dnw_ref)
            lsum[...] = jnp.zeros_like(lsum)

        x3 = x_ref[...] + _dot(f_ref[...], wd_ref[...])
        nw = nw_ref[...]
        xh, _ = _rms(x3)
        err = xh * nw - t_ref[...]
        lsum[...] += jnp.sum(err * err, axis=0, keepdims=True)
        dx, dnw = _rms_bwd(err * (1.0 / D_MODEL), x3, nw)
        dx_ref[...] = dx
        _row_acc(dnw_ref, 0, dnw)

        @pl.when(i == nt - 1)
        def _():
            loss_ref[...] = jnp.sum(lsum[...], axis=1, keepdims=True) * (0.5 / D_MODEL)

    row = pl.BlockSpec((tm, D_MODEL), lambda i: (i, 0))
    return _call(
        body, name="ffn_down_loss", grid=(nt,),
        in_specs=[pl.BlockSpec((tm, kf), lambda i: (i, 0)),
                  pl.BlockSpec((kf, D_MODEL), lambda i: (0, 0)),
                  row, pl.BlockSpec((1, D_MODEL), lambda i: (0, 0)), row],
        out_specs=[pl.BlockSpec((1, 1), lambda i: (0, 0)), row,
                   pl.BlockSpec((8, D_MODEL), lambda i: (0, 0))],
        out_shape=[jax.ShapeDtypeStruct((1, 1), F32), jax.ShapeDtypeStruct((m, D_MODEL), F32),
                   jax.ShapeDtypeStruct((8, D_MODEL), F32)],
        scratch=[pltpu.VMEM((1, D_MODEL), F32)],
    )(f, wd, x2, nfw, target)


def _row_tile(rows):
    return next((t for t in (256, 128, 64, 32, 16, 8) if rows % t == 0), rows)


def _adamw(w, gs, m, v, *, name):
    rows, cols = w.shape
    tr = _row_tile(rows)
    ng = len(gs)

    def body(w_ref, *rest):
        g_refs, (m_ref, v_ref, g_out, d_out, m_out, v_out) = rest[:ng], rest[ng:]
        g = g_refs[0][...]
        for r in g_refs[1:]:
            g = g + r[...]
        mn = ADAM_B1 * m_ref[...] + (1.0 - ADAM_B1) * g
        vn = ADAM_B2 * v_ref[...] + (1.0 - ADAM_B2) * (g * g)
        m_hat = mn / (1.0 - ADAM_B1 ** ADAM_STEP)
        v_hat = vn / (1.0 - ADAM_B2 ** ADAM_STEP)
        g_out[...] = g
        d_out[...] = -ADAM_LR * (m_hat / (jnp.sqrt(v_hat) + ADAM_EPS) + ADAM_WD * w_ref[...])
        m_out[...] = mn
        v_out[...] = vn

    spec = pl.BlockSpec((tr, cols), lambda i: (i, 0))
    return _call(
        body, name=name, grid=(rows // tr,),
        in_specs=[spec] * (3 + ng), out_specs=[spec] * 4,
        out_shape=[jax.ShapeDtypeStruct((rows, cols), F32)] * 4,
    )(w, *gs, m, v)


def _mesh_pos():
    x, y, c = lax.axis_index("x"), lax.axis_index("y"), lax.axis_index("c")
    return x, y, c


def _other_chips(x, y, c):
    return [((1 - x, y, c), 2 * (1 - x) + y), ((x, 1 - y, c), 2 * x + 1 - y),
            ((1 - x, 1 - y, c), 2 * (1 - x) + 1 - y)]


def _region(ref, axis, size, half_axis, chip, core=None):
    idx = [slice(None)] * len(ref.shape)
    if core is None:
        idx[axis] = pl.ds(pl.multiple_of(chip * size, size), size)
    elif half_axis == axis:
        h = size // 2
        idx[axis] = pl.ds(pl.multiple_of(chip * size + core * h, h), h)
    else:
        idx[axis] = pl.ds(pl.multiple_of(chip * size, size), size)
        h = ref.shape[half_axis] // 2
        idx[half_axis] = pl.ds(pl.multiple_of(core * h, h), h)
    return ref.at[tuple(idx)]


def _half(ref, half_axis, core):
    idx = [slice(None)] * len(ref.shape)
    h = ref.shape[half_axis] // 2
    idx[half_axis] = pl.ds(pl.multiple_of(core * h, h), h)
    return ref.at[tuple(idx)]


class _Copy:
    def __init__(self, make):
        self._make = make

    def start(self):
        self._make().start()

    def wait(self):
        self._make().wait()

    def wait_send(self):
        self._make().wait_send()

    def wait_recv(self):
        self._make().wait_recv()


def _remote(src, dst, send_sem, recv_sem, dev):
    return _Copy(lambda: pltpu.make_async_remote_copy(
        src_ref=src, dst_ref=dst, send_sem=send_sem, recv_sem=recv_sem, device_id=dev, device_id_type=MESH))


def _local(src, dst, sem):
    return _Copy(lambda: pltpu.make_async_copy(src, dst, sem))


def _dma_sems(n):
    return pltpu.SemaphoreType.DMA((n,))


def _place_shard(w, chip, axis, *, name):
    shape = list(w.shape)
    shape[axis] *= N_CHIPS
    if w.ndim == 3:
        block, grid = (1,) + w.shape[1:], (w.shape[0],)
        in_map, out_map = (lambda i, chip: (i, 0, 0)), (lambda i, chip: (i, chip[0], 0))
    else:
        tr = _row_tile(w.shape[0])
        nt = w.shape[0] // tr
        block, grid = (tr, w.shape[1]), (nt,)
        in_map = lambda i, chip: (i, 0)
        out_map = (lambda i, chip: (chip[0] * nt + i, 0)) if axis == 0 else (lambda i, chip: (i, chip[0]))

    def body(chip_ref, w_ref, o_ref):
        o_ref[...] = _bf(w_ref[...])

    return _call(body, name=name, grid=grid, prefetch=1, in_specs=[pl.BlockSpec(block, in_map)],
                 out_specs=pl.BlockSpec(block, out_map),
                 out_shape=jax.ShapeDtypeStruct(tuple(shape), BF16))(chip, w)


def _ici_leg(srcs, dsts, layout, sizes, n_whole, sems):
    send_sems, recv_sems, local_sems = sems
    x, y, c = _mesh_pos()
    mine = 2 * x + y
    n_big = len(srcs) - n_whole
    local, sends, recvs = [], [], []
    for t, (src, dst) in enumerate(zip(srcs, dsts)):
        if t < n_big:
            ax, hx = layout[t]
            part = _region(src, ax, sizes[t], hx, mine, c)
            landing = lambda chip, dst=dst, ax=ax, hx=hx, size=sizes[t]: _region(dst, ax, size, hx, chip, c)
        else:
            part, landing = src, (lambda chip, dst=dst: dst.at[chip])
            local.append(_local(src, dst.at[mine], local_sems.at[t - n_big]))
        for k, (dev, chip) in enumerate(_other_chips(x, y, c)):
            sends.append(_remote(part, landing(mine), send_sems.at[3 * t + k], recv_sems.at[3 * t + k], dev))
            recvs.append(_remote(part, landing(chip), send_sems.at[3 * t + k], recv_sems.at[3 * t + k], dev))
    return local, sends, recvs


def _d2d_leg(srcs, dsts, layout, sizes, sems):
    send_sems, recv_sems = sems
    x, y, c = _mesh_pos()
    sends, recvs = [], []
    for t, (src, dst) in enumerate(zip(srcs, dsts)):
        ax, hx = layout[t]
        for k, (_, chip) in enumerate(_other_chips(x, y, c)):
            sem = (send_sems.at[3 * t + k], recv_sems.at[3 * t + k])
            sends.append(_remote(_region(src, ax, sizes[t], hx, chip, c),
                                 _region(dst, ax, sizes[t], hx, chip, c), *sem, (x, y, 1 - c)))
            recvs.append(_remote(_region(src, ax, sizes[t], hx, chip, 1 - c),
                                 _region(dst, ax, sizes[t], hx, chip, 1 - c), *sem, (x, y, 1 - c)))
    return sends, recvs


def _gather_shapes(bufs, whole):
    return ([jax.ShapeDtypeStruct(b.shape, b.dtype) for b in bufs]
            + [jax.ShapeDtypeStruct((N_CHIPS,) + w.shape, w.dtype) for w in whole])


def _gather_ici(bufs, layout):
    n = len(bufs)
    sizes = [b.shape[ax] // N_CHIPS for b, (ax, _) in zip(bufs, layout)]

    def start(ins, outs, sems):
        for cp in _ici_leg(ins, outs, layout, sizes, 0, (*sems, None))[1]:
            cp.start()

    def finish(ins, outs, sems):
        _, sends, recvs = _ici_leg(ins, outs, layout, sizes, 0, (*sems, None))
        for cp in recvs:
            cp.wait_recv()
        for cp in sends:
            cp.wait_send()

    return _Comm(bufs, _gather_shapes(bufs, ()), [_dma_sems(3 * n), _dma_sems(3 * n)], start, finish,
                 aliases={i: i for i in range(n)})


def _gather_d2d(bufs, layout, sizes):
    n = len(bufs)

    def start(ins, outs, sems):
        for cp in _d2d_leg(ins, outs, layout, sizes, sems)[0]:
            cp.start()

    def finish(ins, outs, sems):
        sends, recvs = _d2d_leg(ins, outs, layout, sizes, sems)
        for cp in recvs:
            cp.wait_recv()
        for cp in sends:
            cp.wait_send()

    return _Comm(bufs, [jax.ShapeDtypeStruct(b.shape, b.dtype) for b in bufs],
                 [_dma_sems(3 * n), _dma_sems(3 * n)], start, finish, aliases={i: i for i in range(n)})


def _gather_both(bufs, layout, whole):
    n, nb = len(bufs) + len(whole), len(bufs)
    sizes = [b.shape[ax] // N_CHIPS for b, (ax, _) in zip(bufs, layout)]

    def start(ins, outs, sems):
        local, sends, _ = _ici_leg(ins, outs, layout, sizes, len(whole), sems[:3])
        for cp in local + sends:
            cp.start()

    def finish(ins, outs, sems):
        local, sends, recvs = _ici_leg(ins, outs, layout, sizes, len(whole), sems[:3])
        for cp in recvs:
            cp.wait_recv()
        onward, arriving = _d2d_leg(outs[:nb], outs[:nb], layout, sizes, sems[3:])
        for cp in onward:
            cp.start()
        for cp in arriving:
            cp.wait_recv()
        for cp in sends + onward:
            cp.wait_send()
        for cp in local:
            cp.wait()

    return _Comm(list(bufs) + list(whole), _gather_shapes(bufs, whole),
                 [_dma_sems(3 * n), _dma_sems(3 * n), _dma_sems(len(whole)), _dma_sems(3 * nb), _dma_sems(3 * nb)],
                 start, finish, aliases={i: i for i in range(nb)})


def _exchange(grads, layout):
    n = len(grads)
    others = N_DEV - 1
    sizes = [g.shape[ax] // N_CHIPS for g, (ax, _) in zip(grads, layout)]
    out_shapes = []
    for g, (ax, hx), sz in zip(grads, layout, sizes):
        shape = list(g.shape)
        shape[ax] = sz
        shape[hx] //= 2
        out_shapes.append(jax.ShapeDtypeStruct((others,) + tuple(shape), g.dtype))

    def copies(ins, outs, sems):
        send_sems, recv_sems = sems
        x, y, c = _mesh_pos()
        sends, recvs = [], []
        for t, (src, dst) in enumerate(zip(ins, outs)):
            ax, hx = layout[t]
            for r in range(1, N_DEV):
                px = (1 - x) if r & 4 else x
                py = (1 - y) if r & 2 else y
                pc = (1 - c) if r & 1 else c
                sem = (send_sems.at[others * t + r - 1], recv_sems.at[others * t + r - 1])
                part = _region(src, ax, sizes[t], hx, 2 * px + py, pc)
                sends.append(_remote(part, dst.at[r - 1], *sem, (px, py, pc)))
                recvs.append(_remote(part, dst.at[r - 1], *sem, (px, py, pc)))
        return sends, recvs

    def start(ins, outs, sems):
        for cp in copies(ins, outs, sems)[0]:
            cp.start()

    def finish(ins, outs, sems):
        sends, recvs = copies(ins, outs, sems)
        for cp in recvs:
            cp.wait_recv()
        for cp in sends:
            cp.wait_send()

    return _Comm(grads, out_shapes, [_dma_sems(others * n), _dma_sems(others * n)], start, finish)


def _reduce_half(g, parts, pos, cut, *, name):
    ax, _ = cut
    others = parts.shape[0]
    if g.ndim == 3:
        nb, rows, cols = g.shape
        hb = nb // 2
        block, grid, out_shape = (1, rows // N_CHIPS, cols), (hb,), (nb, rows // N_CHIPS, cols)
        g_map = lambda i, pos: (pos[1] * hb + i, pos[0], 0)
        o_map = lambda i, pos: (pos[1] * hb + i, 0, 0)
        p_map = lambda i, pos: (0, i, 0, 0)
    elif ax == 1:
        rows, cols = g.shape
        tr = _row_tile(rows // 2)
        nt = rows // 2 // tr
        block, grid, out_shape = (tr, cols // N_CHIPS), (nt,), (rows, cols // N_CHIPS)
        g_map = lambda i, pos: (pos[1] * nt + i, pos[0])
        o_map = lambda i, pos: (pos[1] * nt + i, 0)
        p_map = lambda i, pos: (0, i, 0)
    else:
        rows, cols = g.shape
        tr = _row_tile(rows // N_CHIPS // 2)
        nt = rows // N_CHIPS // 2 // tr
        block, grid, out_shape = (tr, cols), (nt,), (rows // N_CHIPS, cols)
        g_map = lambda i, pos: (pos[0] * 2 * nt + pos[1] * nt + i, 0)
        o_map = lambda i, pos: (pos[1] * nt + i, 0)
        p_map = lambda i, pos: (0, i, 0)

    def body(pos_ref, g_ref, p_ref, o_ref):
        acc = g_ref[...].astype(F32)
        for r in range(others):
            acc = acc + p_ref[r].astype(F32)
        o_ref[...] = acc

    return _call(
        body, name=name, grid=grid, prefetch=1,
        in_specs=[pl.BlockSpec(block, g_map), pl.BlockSpec((others,) + block, p_map)],
        out_specs=pl.BlockSpec(block, o_map), out_shape=jax.ShapeDtypeStruct(out_shape, F32),
    )(pos, g, parts)


def _join_halves(bufs):
    n = len(bufs)

    def copies(ins, outs, sems):
        send_sems, recv_sems = sems
        x, y, c = _mesh_pos()
        sends = [_remote(_half(src, 0, c), _half(dst, 0, c), send_sems.at[t], recv_sems.at[t], (x, y, 1 - c))
                 for t, (src, dst) in enumerate(zip(ins, outs))]
        recvs = [_remote(_half(src, 0, 1 - c), _half(dst, 0, 1 - c), send_sems.at[t], recv_sems.at[t],
                         (x, y, 1 - c)) for t, (src, dst) in enumerate(zip(ins, outs))]
        return sends, recvs

    def start(ins, outs, sems):
        for cp in copies(ins, outs, sems)[0]:
            cp.start()

    def finish(ins, outs, sems):
        sends, recvs = copies(ins, outs, sems)
        for cp in recvs:
            cp.wait_recv()
        for cp in sends:
            cp.wait_send()

    comm = _Comm(bufs, [jax.ShapeDtypeStruct(b.shape, b.dtype) for b in bufs],
                 [_dma_sems(n), _dma_sems(n)], start, finish, aliases={i: i for i in range(n)})
    return _call(None, name="join_halves", comm=comm)()[1]


def _allreduce_small(pack):
    rows, cols = pack.shape

    def body(p_ref, o_ref, slots, send_sems, recv_sems):
        x, y, c = _mesh_pos()
        me = 4 * x + 2 * y + c
        slots[me] = p_ref[...]
        copies = []
        for r in range(1, N_DEV):
            fx, fy, fc = (r >> 2) & 1, (r >> 1) & 1, r & 1
            dev = ((1 - x) if fx else x, (1 - y) if fy else y, (1 - c) if fc else c)
            cp = pltpu.make_async_remote_copy(
                src_ref=p_ref, dst_ref=slots.at[me], send_sem=send_sems.at[r - 1],
                recv_sem=recv_sems.at[r - 1], device_id=dev, device_id_type=MESH)
            cp.start()
            copies.append(cp)
        for cp in copies:
            cp.wait_recv()
        for cp in copies:
            cp.wait_send()
        acc = slots[0]
        for d in range(1, N_DEV):
            acc = acc + slots[d]
        o_ref[...] = acc

    vmem = pl.BlockSpec(memory_space=pltpu.VMEM)
    return _call(
        body, name="allreduce_small", in_specs=[vmem], out_specs=vmem,
        out_shape=jax.ShapeDtypeStruct((rows, cols), F32),
        scratch=[pltpu.VMEM((N_DEV, rows, cols), F32), pltpu.SemaphoreType.DMA((N_DEV - 1,)),
                 pltpu.SemaphoreType.DMA((N_DEV - 1,))],
    )(pack)


def _pad_rows(a, rows=8):
    return jnp.pad(a, ((0, rows - a.shape[0]), (0, 0)))


def kernel(x, positions, norm1_w, w_in, merge_gate_b, ret_gn_w, w_ret_o, lru_conv_w, lru_conv_b, lru_w_r, lru_b_r, lru_w_i, lru_b_i, lru_lambda, w_lru_o, w_out, norm2_w, ffn_w_up, ffn_conv_w, ffn_conv_b, ffn_w_down, norm_f_w, loss_target, m_norm1_w, m_w_in, m_merge_gate_b, m_ret_gn_w, m_w_ret_o, m_lru_conv_w, m_lru_conv_b, m_lru_w_r, m_lru_b_r, m_lru_w_i, m_lru_b_i, m_lru_lambda, m_w_lru_o, m_w_out, m_norm2_w, m_ffn_w_up, m_ffn_conv_w, m_ffn_conv_b, m_ffn_w_down, m_norm_f_w, v_norm1_w, v_w_in, v_merge_gate_b, v_ret_gn_w, v_w_ret_o, v_lru_conv_w, v_lru_conv_b, v_lru_w_r, v_lru_b_r, v_lru_w_i, v_lru_b_i, v_lru_lambda, v_w_lru_o, v_w_out, v_norm2_w, v_ffn_w_up, v_ffn_conv_w, v_ffn_conv_b, v_ffn_w_down, v_norm_f_w):
    names = ["norm1_w", "w_in", "merge_gate_b", "ret_gn_w", "w_ret_o", "lru_conv_w", "lru_conv_b", "lru_w_r",
             "lru_b_r", "lru_w_i", "lru_b_i", "lru_lambda", "w_lru_o", "w_out", "norm2_w", "ffn_w_up",
             "ffn_conv_w", "ffn_conv_b", "ffn_w_down", "norm_f_w"]
    w_args = dict(zip(names, [norm1_w, w_in, merge_gate_b, ret_gn_w, w_ret_o, lru_conv_w, lru_conv_b, lru_w_r,
                              lru_b_r, lru_w_i, lru_b_i, lru_lambda, w_lru_o, w_out, norm2_w, ffn_w_up,
                              ffn_conv_w, ffn_conv_b, ffn_w_down, norm_f_w]))
    m_args = dict(zip(names, [m_norm1_w, m_w_in, m_merge_gate_b, m_ret_gn_w, m_w_ret_o, m_lru_conv_w,
                              m_lru_conv_b, m_lru_w_r, m_lru_b_r, m_lru_w_i, m_lru_b_i, m_lru_lambda, m_w_lru_o,
                              m_w_out, m_norm2_w, m_ffn_w_up, m_ffn_conv_w, m_ffn_conv_b, m_ffn_w_down,
                              m_norm_f_w]))
    v_args = dict(zip(names, [v_norm1_w, v_w_in, v_merge_gate_b, v_ret_gn_w, v_w_ret_o, v_lru_conv_w,
                              v_lru_conv_b, v_lru_w_r, v_lru_b_r, v_lru_w_i, v_lru_b_i, v_lru_lambda, v_w_lru_o,
                              v_w_out, v_norm2_w, v_ffn_w_up, v_ffn_conv_w, v_ffn_conv_b, v_ffn_w_down,
                              v_norm_f_w]))

    bsz, seq, d = x.shape
    m = bsz * seq
    tm = min(MM_ROWS, m)
    tm_fused = min(FUSED_ROWS, m)
    chip = 2 * lax.axis_index("x") + lax.axis_index("y")

    big = ["w_in", "w_ret_o", "w_lru_o", "w_out", "lru_w_r", "lru_w_i", "ffn_w_up", "ffn_w_down"]
    cut = dict(w_in=(1, 0), w_ret_o=(0, 0), w_lru_o=(0, 0), w_out=(0, 0), lru_w_r=(1, 0), lru_w_i=(1, 0),
               ffn_w_up=(1, 0), ffn_w_down=(0, 0))
    later = big[1:]
    core = lax.axis_index("c")
    chip1 = jnp.reshape(chip, (1,)).astype(jnp.int32)
    pos = jnp.stack([chip, core]).astype(jnp.int32)
    placed = {n: _place_shard(w_args[n][0], chip1, cut[n][0], name="place_" + n) for n in big}
    small_pack = jnp.concatenate([
        jnp.pad(merge_gate_b[0], ((0, 6), (0, 512))),
        jnp.pad(lru_conv_w[0], ((0, 4), (0, 512))),
        jnp.pad(lru_b_r[0], ((0, 4), (0, 704))),
        jnp.pad(lru_b_i[0], ((0, 4), (0, 704))),
        jnp.pad(ffn_conv_w[0], ((0, 5), (0, 0))),
    ], axis=0)
    _, (w_in_full, sp) = _call(None, name="gather_w_in",
                               comm=_gather_both([placed["w_in"]], [cut["w_in"]], [small_pack]))()
    wb = {"w_in": w_in_full}
    mgb = jnp.transpose(sp[:, 0:2, 0:256], (1, 0, 2)).reshape(2, D_MODEL)
    lcw = jnp.transpose(sp[:, 8:12, 0:256], (1, 0, 2)).reshape(4, D_MODEL)
    lbr = jnp.transpose(sp[:, 16:20, 0:64], (1, 0, 2)).reshape(1, D_MODEL)
    lbi = jnp.transpose(sp[:, 24:28, 0:64], (1, 0, 2)).reshape(1, D_MODEL)
    fcw = jnp.transpose(sp[:, 32:35, :], (1, 0, 2)).reshape(3, D_FF)
    nfw = norm_f_w.reshape(1, D_MODEL)

    x2d = x.reshape(m, d)
    half = RET_DK // 2
    inv_freq = ROPE_BASE ** (-jnp.arange(half, dtype=F32) / half)
    cos, sin = _rope_tables(positions.reshape(bsz, seq, 1), jnp.concatenate([inv_freq, inv_freq]).reshape(1, RET_DK))
    later_cut = [cut[n] for n in later]
    (proj, h1), bufs = _norm_matmul(x2d, norm1_w, wb["w_in"], name="in_proj", tm=tm, tn=MM_COLS,
                                    comm=_gather_ici([placed[n] for n in later], later_cut))
    proj3 = proj.reshape(bsz, seq, D_IN)
    (a_in3, states), bufs = _retention_fwd(
        proj3, cos, sin, ret_gn_w,
        comm=_gather_d2d(bufs, later_cut, [w_args[n].shape[1 + cut[n][0]] for n in later]))
    wb.update(zip(later, bufs))
    lru_params = (lcw, lru_conv_b, wb["lru_w_r"], lbr, wb["lru_w_i"], lbi, lru_lambda)
    b_in3, hseq = _lru_fwd(proj3, lru_params)
    a_in, b_in = a_in3.reshape(m, d), b_in3.reshape(m, d)
    x2 = _merge_fwd(a_in, b_in, proj, mgb, wb["w_ret_o"], wb["w_lru_o"], wb["w_out"], x2d, tm=tm_fused)
    up, h2 = _norm_matmul(x2, norm2_w, wb["ffn_w_up"], name="ffn_up", tm=tm, tn=MM_COLS)
    up3 = up.reshape(bsz, seq, 2 * D_FF)
    f3 = _ffn_act_fwd(up3, fcw, ffn_conv_b)
    f = f3.reshape(m, D_FF)
    loss_dev, dx3, sm_nf = _ffn_down_loss(f, wb["ffn_w_down"], x2, nfw, loss_target.reshape(m, d), tm=tm_fused)
    loss = lax.psum(loss_dev[0, 0], ("x", "y", "c"))

    def send(*ns):
        return _exchange([g_full[n] for n in ns], [cut[n] for n in ns])

    g_full, parts = {}, {}
    df = _mm_nt(dx3, wb["ffn_w_down"], name="ffn_down_dx", tm=tm, out_dtype=BF16)
    g_full["ffn_w_down"] = _mm_tn(f, [dx3], name="ffn_down_dw", tm=tm)
    (dgate3, dval3, sm_ffn), (parts["ffn_w_down"],) = _ffn_act_bwd(
        up3, fcw, ffn_conv_b, df.reshape(bsz, seq, D_FF), comm=send("ffn_w_down"))
    dup = [dgate3.reshape(m, D_FF), dval3.reshape(m, D_FF)]
    g_full["ffn_w_up"] = _mm_tn(h2, dup, name="ffn_up_dw", tm=tm)
    (dx2, sm_n2), (parts["ffn_w_up"],) = _mm_nt_normbwd(
        dup, wb["ffn_w_up"], x2, norm2_w, dx3, name="ffn_up_dx", tm=tm, comm=send("ffn_w_up"))
    mix, dya, dyb, da_in, db_in, dgates, sm_mg = _merge_bwd(
        a_in, b_in, proj, mgb, wb["w_ret_o"], wb["w_lru_o"], wb["w_out"], dx2, tm=tm_fused)
    g_full["w_out"] = _mm_tn(mix, [dx2], name="out_dw", tm=tm)
    g_full["w_ret_o"] = _mm_tn(a_in, [dya], name="ret_o_dw", tm=tm)
    g_full["w_lru_o"] = _mm_tn(b_in, [dyb], name="lru_o_dw", tm=tm)
    (dlru3, dwr, dwi, sm_lru), (parts["w_out"], parts["w_ret_o"], parts["w_lru_o"]) = _lru_bwd(
        proj3, lru_params, hseq, db_in.reshape(bsz, seq, d), comm=send("w_out", "w_ret_o", "w_lru_o"))
    g_full["lru_w_r"], g_full["lru_w_i"] = dwr.astype(BF16), dwi.astype(BF16)
    (dret3, sm_gn), (parts["lru_w_r"], parts["lru_w_i"]) = _retention_bwd(
        proj3, cos, sin, ret_gn_w, states, da_in.reshape(bsz, seq, d), comm=send("lru_w_r", "lru_w_i"))
    dproj = [dret3.reshape(m, 3072), dlru3.reshape(m, 2048), dgates]
    g_full["w_in"] = _mm_tn(h1, dproj, name="in_proj_dw", tm=tm)
    (grad_x, sm_n1), (parts["w_in"],) = _mm_nt_normbwd(
        dproj, wb["w_in"], x2d, norm1_w, dx2, name="in_proj_dx", tm=tm, comm=send("w_in"))

    reduced = _join_halves([_reduce_half(g_full[n], parts[n], pos, cut[n], name="sum_" + n) for n in big])
    pack = jnp.concatenate(
        [sm_n1, sm_mg, sm_gn, sm_lru, sm_n2, sm_ffn[:, 0:1024], sm_ffn[:, 1024:2048], sm_ffn[:, 2048:3072], sm_nf],
        axis=0)
    tot = _allreduce_small(pack)
    ffn_sm = jnp.concatenate([tot[40:48], tot[48:56], tot[56:64]], axis=1)
    g_small = {
        "norm1_w": tot[0:1], "merge_gate_b": tot[8:10], "ret_gn_w": tot[16:17], "lru_conv_w": tot[24:28],
        "lru_conv_b": tot[28:29], "lru_b_r": tot[29:30].reshape(4, 256), "lru_b_i": tot[30:31].reshape(4, 256),
        "lru_lambda": tot[31:32], "norm2_w": tot[32:33], "ffn_conv_w": ffn_sm[0:3], "ffn_conv_b": ffn_sm[3:4],
        "norm_f_w": tot[64:65],
    }
    small_shard = dict(merge_gate_b=256, lru_conv_w=256, lru_b_r=64, lru_b_i=64, ffn_conv_w=768)

    outs = {}
    for n, g in zip(big, reduced):
        shape = w_args[n].shape
        g = g.reshape(-1, g.shape[-1])
        outs[n] = [o.reshape(shape) for o in _adamw(
            w_args[n].reshape(g.shape), [g], m_args[n].reshape(g.shape), v_args[n].reshape(g.shape),
            name="adamw_" + n)]
    for n, g in g_small.items():
        shape = w_args[n].shape
        if n in small_shard:
            g = lax.dynamic_slice_in_dim(g, chip * small_shard[n], small_shard[n], axis=1)
        w2 = w_args[n].reshape(g.shape)
        outs[n] = [o.reshape(shape) for o in _adamw(
            w2, [g], m_args[n].reshape(g.shape), v_args[n].reshape(g.shape), name="adamw_" + n)]

    result = [loss, grad_x.reshape(bsz, seq, d)]
    for k in range(4):
        result += [outs[n][k] for n in names]
    return tuple(result)
```

```python
import functools
import math

import numpy as np
import jax
import jax.numpy as jnp
from jax import lax
from jax.experimental import pallas as pl
from jax.experimental.pallas import tpu as pltpu

F32 = jnp.float32
BF16 = jnp.bfloat16

D_MODEL = 1024
RET_HEADS = 4
RET_DK = 128
RET_DV = 256
LRU_BLOCKS = 4
LRU_BLOCK = 256
LRU_C = 8.0
D_FF = 3072
D_IN = 7168
ROPE_BASE = 10000.0
RMS_EPS = 1e-6
GN_EPS = 1e-6
ADAM_LR, ADAM_B1, ADAM_B2, ADAM_EPS, ADAM_WD, ADAM_STEP = 0.001, 0.9, 0.999, 1e-08, 0.01, 10

N_CHIPS = 4
N_DEV = 8
SEQ_T = 256
REF_CHUNK = 64
COL = 1024
MM_ROWS = 1024
MM_COLS = 1024
FUSED_ROWS = 512
VMEM_LIMIT_BYTES = 56 * 1024 * 1024
MESH = pl.DeviceIdType.MESH
ROW_N1, ROW_MGB, ROW_GN, ROW_N2, ROW_NF = 0, 1, 3, 4, 5
GELU_K = math.sqrt(2.0 / math.pi)
GELU_C = 0.044715


class _Comm:
    def __init__(self, ins, outs, sems, start, finish, aliases=None):
        self.ins, self.outs, self.sems = list(ins), list(outs), list(sems)
        self.start, self.finish, self.aliases = start, finish, dict(aliases or {})


def _call(body, *, name, out_shape=(), grid=None, in_specs=(), out_specs=(), scratch=(), comm=None, prefetch=0):
    single = not isinstance(out_shape, (list, tuple))
    out_shape = [out_shape] if single else list(out_shape)
    out_specs = [out_specs] if single else list(out_specs)
    in_specs, scratch = list(in_specs), list(scratch)
    n_in, n_out, n_scr = len(in_specs), len(out_shape), len(scratch)
    kwargs = dict(name=name, compiler_params=pltpu.CompilerParams(vmem_limit_bytes=VMEM_LIMIT_BYTES))
    if prefetch:
        assert comm is None
        spec = pltpu.PrefetchScalarGridSpec(num_scalar_prefetch=prefetch, grid=grid, in_specs=in_specs,
                                            out_specs=out_specs, scratch_shapes=scratch)
        fn = pl.pallas_call(body, out_shape=out_shape, grid_spec=spec, **kwargs)
        return (lambda *args: fn(*args)[0]) if single else fn
    if grid is not None:
        kwargs["grid"] = grid
    if comm is None:
        fn = pl.pallas_call(body, out_shape=out_shape, in_specs=in_specs, out_specs=out_specs,
                            scratch_shapes=scratch, **kwargs)
        return (lambda *args: fn(*args)[0]) if single else fn

    any_spec = pl.BlockSpec(memory_space=pl.ANY)
    n_cin, n_cout = len(comm.ins), len(comm.outs)

    def wrapped(*refs):
        ins, refs = refs[:n_in], refs[n_in:]
        cins, refs = refs[:n_cin], refs[n_cin:]
        outs, refs = refs[:n_out], refs[n_out:]
        couts, refs = refs[:n_cout], refs[n_cout:]
        scr, csems = refs[:n_scr], refs[n_scr:]
        if grid is None:
            comm.start(cins, couts, csems)
            comm.finish(cins, couts, csems)
            return
        ids = [pl.program_id(a) for a in range(len(grid))]
        first = functools.reduce(jnp.logical_and, [i == 0 for i in ids])
        last = functools.reduce(jnp.logical_and, [i == g - 1 for i, g in zip(ids, grid)])
        pl.when(first)(lambda: comm.start(cins, couts, csems))
        body(*ins, *outs, *scr)
        pl.when(last)(lambda: comm.finish(cins, couts, csems))

    fn = pl.pallas_call(
        wrapped, out_shape=out_shape + comm.outs, in_specs=in_specs + [any_spec] * n_cin,
        out_specs=out_specs + [any_spec] * n_cout, scratch_shapes=scratch + comm.sems,
        input_output_aliases={n_in + i: n_out + o for i, o in comm.aliases.items()}, **kwargs)

    def run(*args):
        res = fn(*args, *comm.ins)
        own = res[0] if single else list(res[:n_out])
        return own, list(res[n_out:])

    return run


def _dot(a, b):
    return jnp.dot(a, b, preferred_element_type=F32)


def _dot_nt(a, b):
    return lax.dot_general(a, b, (((1,), (1,)), ((), ())), preferred_element_type=F32)


def _dot_tn(a, b):
    return lax.dot_general(a, b, (((0,), (0,)), ((), ())), preferred_element_type=F32)


def _bf(x):
    return x.astype(BF16)


def _sigmoid(x):
    return 1.0 / (1.0 + jnp.exp(-x))


def _gelu(x):
    return 0.5 * x * (1.0 + jnp.tanh(GELU_K * (x + GELU_C * x * x * x)))


def _gelu_and_grad(x):
    x2 = x * x
    t = jnp.tanh(x * (GELU_K * GELU_C * x2 + GELU_K))
    hx = 0.5 * x
    g = hx + hx * t
    dg = 0.5 + 0.5 * t + hx * (1.0 - t * t) * (3.0 * GELU_K * GELU_C * x2 + GELU_K)
    return g, dg


def _rms(x):
    r = lax.rsqrt(jnp.mean(x * x, axis=-1, keepdims=True) + RMS_EPS)
    return x * r, r


def _rms_bwd(dy, x, nw):
    xh, r = _rms(x)
    g = dy * nw
    dx = r * (g - xh * jnp.mean(g * xh, axis=-1, keepdims=True))
    return dx, jnp.sum(dy * xh, axis=0, keepdims=True)


def _row_acc(ref, row, val):
    ref[row:row + 1, :] = ref[row:row + 1, :] + val


def _shift_down(x, j, prev8):
    if j == 0:
        return x
    n = x.shape[0] // 8
    row = lax.broadcasted_iota(jnp.int32, prev8.shape, 0)
    turned = [pltpu.roll(prev8, j, 0)] + [pltpu.roll(x[8 * k:8 * k + 8], j, 0) for k in range(n)]
    return jnp.concatenate([jnp.where(row < j, turned[k], turned[k + 1]) for k in range(n)], axis=0)


def _shift_up(x, j, next8):
    if j == 0:
        return x
    n = x.shape[0] // 8
    row = lax.broadcasted_iota(jnp.int32, next8.shape, 0)
    turned = [pltpu.roll(x[8 * k:8 * k + 8], 8 - j, 0) for k in range(n)] + [pltpu.roll(next8, 8 - j, 0)]
    return jnp.concatenate([jnp.where(row >= 8 - j, turned[k + 1], turned[k]) for k in range(n)], axis=0)


def _scan_fwd(a, b, carry):
    row = lax.broadcasted_iota(jnp.int32, (8, a.shape[1]), 0)
    out = []
    for k in range(a.shape[0] // 8):
        ak, bk = a[8 * k:8 * k + 8], b[8 * k:8 * k + 8]
        for s in (1, 2, 4):
            keep = row >= s
            ar, br = pltpu.roll(ak, s, 0), pltpu.roll(bk, s, 0)
            bk = jnp.where(keep, ak * br + bk, bk)
            ak = jnp.where(keep, ak * ar, ak)
        hk = ak * carry + bk
        carry = hk[7:8]
        out.append(hk)
    return jnp.concatenate(out, axis=0)


def _scan_bwd(a, b, carry):
    row = lax.broadcasted_iota(jnp.int32, (8, a.shape[1]), 0)
    out = []
    for k in reversed(range(a.shape[0] // 8)):
        ak, bk = a[8 * k:8 * k + 8], b[8 * k:8 * k + 8]
        for s in (1, 2, 4):
            keep = row < 8 - s
            ar, br = pltpu.roll(ak, 8 - s, 0), pltpu.roll(bk, 8 - s, 0)
            bk = jnp.where(keep, ak * br + bk, bk)
            ak = jnp.where(keep, ak * ar, ak)
        gk = bk + ak * carry
        carry = gk[0:1]
        out.append(gk)
    return jnp.concatenate(out[::-1], axis=0)


def _norm_matmul(x, nw, w, *, name, tm, tn, comm=None):
    m, d = x.shape
    n = w.shape[1]

    def body(x_ref, nw_ref, w_ref, o_ref, h_ref, h_sc):
        @pl.when(pl.program_id(1) == 0)
        def _():
            xh, _ = _rms(x_ref[...])
            h = _bf(xh * nw_ref[...])
            h_sc[...] = h
            h_ref[...] = h

        o_ref[...] = _dot(h_sc[...], w_ref[...])

    return _call(
        body, name=name, grid=(m // tm, n // tn), comm=comm,
        in_specs=[pl.BlockSpec((tm, d), lambda i, j: (i, 0)),
                  pl.BlockSpec((1, d), lambda i, j: (0, 0)),
                  pl.BlockSpec((d, tn), lambda i, j: (0, j))],
        out_specs=[pl.BlockSpec((tm, tn), lambda i, j: (i, j)),
                   pl.BlockSpec((tm, d), lambda i, j: (i, 0))],
        out_shape=[jax.ShapeDtypeStruct((m, n), F32), jax.ShapeDtypeStruct((m, d), BF16)],
        scratch=[pltpu.VMEM((tm, d), BF16)],
    )(x, nw, w)


def _mm_nt(a, w, *, name, tm, out_dtype):
    m, k = a.shape
    n = w.shape[0]

    def body(a_ref, w_ref, o_ref):
        o_ref[...] = _dot_nt(_bf(a_ref[...]), w_ref[...]).astype(out_dtype)

    return _call(
        body, name=name, grid=(m // tm, n // COL),
        in_specs=[pl.BlockSpec((tm, k), lambda i, j: (i, 0)),
                  pl.BlockSpec((COL, k), lambda i, j: (j, 0))],
        out_specs=pl.BlockSpec((tm, COL), lambda i, j: (i, j)),
        out_shape=jax.ShapeDtypeStruct((m, n), out_dtype),
    )(a, w)


def _piece_layout(pieces):
    offs, nblk, o = [], [], 0
    for p in pieces:
        offs.append(o)
        nblk.append(p.shape[1] // COL)
        o += p.shape[1] // COL
    return offs, nblk, o


def _mm_tn(a, pieces, *, name, tm, out_dtype=BF16):
    m, k = a.shape
    offs, nblk, nn = _piece_layout(pieces)

    def piece_spec(o, nb):
        def idx(ki, nj, mi):
            use = jnp.logical_and(nj >= o, nj < o + nb)
            return (jnp.where(use, mi, 0), jnp.clip(nj - o, 0, nb - 1))
        return pl.BlockSpec((tm, COL), idx)

    def body(a_ref, *rest):
        p_refs, o_ref, acc = rest[:len(pieces)], rest[len(pieces)], rest[len(pieces) + 1]
        nj, mi = pl.program_id(1), pl.program_id(2)

        @pl.when(mi == 0)
        def _():
            acc[...] = jnp.zeros_like(acc)

        for p_ref, o, nb in zip(p_refs, offs, nblk):
            @pl.when(jnp.logical_and(nj >= o, nj < o + nb))
            def _(p_ref=p_ref):
                acc[...] += _dot_tn(_bf(a_ref[...]), _bf(p_ref[...]))

        @pl.when(mi == pl.num_programs(2) - 1)
        def _():
            o_ref[...] = acc[...].astype(out_dtype)

    return _call(
        body, name=name, grid=(k // COL, nn, m // tm),
        in_specs=[pl.BlockSpec((tm, COL), lambda ki, nj, mi: (mi, ki))]
        + [piece_spec(o, nb) for o, nb in zip(offs, nblk)],
        out_specs=pl.BlockSpec((COL, COL), lambda ki, nj, mi: (ki, nj)),
        out_shape=jax.ShapeDtypeStruct((k, nn * COL), out_dtype),
        scratch=[pltpu.VMEM((COL, COL), F32)],
    )(a, *pieces)


def _mm_nt_normbwd(pieces, w, x, nw, dres, *, name, tm, row, comm=None):
    m, d = x.shape
    offs, nblk, nk = _piece_layout(pieces)

    def piece_spec(o, nb):
        return pl.BlockSpec((tm, COL), lambda i, k: (i, jnp.clip(k - o, 0, nb - 1)))

    def body(*refs):
        p_refs = refs[:len(pieces)]
        w_ref, x_ref, nw_ref, dres_ref, dx_ref, dnw_ref, acc = refs[len(pieces):]
        i, k = pl.program_id(0), pl.program_id(1)

        @pl.when(jnp.logical_and(i == 0, k == 0))
        def _():
            dnw_ref[...] = jnp.zeros_like(dnw_ref)

        @pl.when(k == 0)
        def _():
            acc[...] = jnp.zeros_like(acc)

        for p_ref, o, nb in zip(p_refs, offs, nblk):
            @pl.when(jnp.logical_and(k >= o, k < o + nb))
            def _(p_ref=p_ref):
                acc[...] += _dot_nt(_bf(p_ref[...]), w_ref[...])

        @pl.when(k == nk - 1)
        def _():
            dx, dnw = _rms_bwd(acc[...], x_ref[...], nw_ref[...])
            dx_ref[...] = dres_ref[...] + dx
            _row_acc(dnw_ref, row, dnw)

    return _call(
        body, name=name, grid=(m // tm, nk), comm=comm,
        in_specs=[piece_spec(o, nb) for o, nb in zip(offs, nblk)]
        + [pl.BlockSpec((d, COL), lambda i, k: (0, k)),
           pl.BlockSpec((tm, d), lambda i, k: (i, 0)),
           pl.BlockSpec((1, d), lambda i, k: (0, 0)),
           pl.BlockSpec((tm, d), lambda i, k: (i, 0))],
        out_specs=[pl.BlockSpec((tm, d), lambda i, k: (i, 0)),
                   pl.BlockSpec((8, d), lambda i, k: (0, 0))],
        out_shape=[jax.ShapeDtypeStruct((m, d), F32), jax.ShapeDtypeStruct((8, d), F32)],
        scratch=[pltpu.VMEM((tm, d), F32)],
    )(*pieces, w, x, nw, dres)


def _rope_tables(pos3, invf):
    b, s, _ = pos3.shape

    def body(pos_ref, invf_ref, cos_ref, sin_ref):
        ang = pos_ref[...].astype(F32) * invf_ref[...]
        lane = lax.broadcasted_iota(jnp.int32, ang.shape, 1)
        cos_ref[...] = jnp.cos(ang)
        sin_ref[...] = jnp.where(lane < RET_DK // 2, -1.0, 1.0) * jnp.sin(ang)

    spec = pl.BlockSpec((None, SEQ_T, RET_DK), lambda i, c: (i, c, 0))
    return _call(
        body, name="rope_tables", grid=(b, s // SEQ_T),
        in_specs=[pl.BlockSpec((None, SEQ_T, 1), lambda i, c: (i, c, 0)),
                  pl.BlockSpec((1, RET_DK), lambda i, c: (0, 0))],
        out_specs=[spec, spec],
        out_shape=[jax.ShapeDtypeStruct((b, s, RET_DK), F32)] * 2,
    )(pos3, invf)


def _log_gamma(h):
    return float(np.log1p(-np.power(np.float32(2.0), np.float32(-5.0 - h))).astype(np.float32))


def _decay_matrix(h):
    lg = _log_gamma(h)
    n = lax.broadcasted_iota(jnp.int32, (SEQ_T, SEQ_T), 0)
    m = lax.broadcasted_iota(jnp.int32, (SEQ_T, SEQ_T), 1)
    same = (n // REF_CHUNK) == (m // REF_CHUNK)
    dist = jnp.where(same, jnp.abs(n - m), n - m).astype(F32)
    return jnp.where(jnp.logical_or(same, m < n), jnp.exp(lg * dist), 0.0)


def _decay_vectors(h):
    lg = _log_gamma(h)
    idx = lax.broadcasted_iota(jnp.int32, (SEQ_T, 1), 0).astype(F32)
    qd = jnp.exp(lg * (idx + 1.0))
    kd = jnp.exp(lg * (SEQ_T - 1.0 - idx))
    return qd, kd, math.exp(lg * SEQ_T)


def _rotate(x, cos, sin):
    return x * cos + pltpu.roll(x, RET_DK // 2, 1) * sin


def _rotate_bwd(d, cos, sin):
    return d * cos + pltpu.roll(d * sin, RET_DK // 2, 1)


def _ret_head(p_ref, cos, sin, h):
    q = p_ref[:, h * RET_DK:(h + 1) * RET_DK]
    k = p_ref[:, 512 + h * RET_DK:512 + (h + 1) * RET_DK]
    v = p_ref[:, 1024 + h * RET_DV:1024 + (h + 1) * RET_DV]
    g = p_ref[:, 2048 + h * RET_DV:2048 + (h + 1) * RET_DV]
    qr = _rotate(q, cos, sin)
    kr = _rotate(k, cos, sin) * (RET_DK ** -0.5)
    return qr, kr, v, g


def _group_norm(o):
    mu = jnp.mean(o, axis=-1, keepdims=True)
    oc = o - mu
    rstd = lax.rsqrt(jnp.mean(oc * oc, axis=-1, keepdims=True) + GN_EPS)
    return oc * rstd, rstd


def _retention_fwd(proj3, cos, sin, gnw, comm=None):
    b, s, _ = proj3.shape
    nc = s // SEQ_T

    def body(p_ref, cos_ref, sin_ref, gnw_ref, a_ref, st_ref, state, wtab):
        c = pl.program_id(1)

        @pl.when(jnp.logical_and(pl.program_id(0) == 0, c == 0))
        def _():
            for h in range(RET_HEADS):
                wtab[h] = _decay_matrix(h)

        @pl.when(c == 0)
        def _():
            state[...] = jnp.zeros_like(state)

        cs, sn = cos_ref[...], sin_ref[...]
        for h in range(RET_HEADS):
            qd, kd, gt = _decay_vectors(h)
            qr, kr, v, g = _ret_head(p_ref, cs, sn, h)
            st = state[h]
            st_ref[h] = st
            p = _dot_nt(_bf(qr), _bf(kr)) * wtab[h]
            o = _dot(_bf(p), _bf(v)) + _dot(_bf(qr * qd), _bf(st))
            state[h] = st * gt + _dot_tn(_bf(kr * kd), _bf(v))
            on, _ = _group_norm(o)
            gw = gnw_ref[:, h * RET_DV:(h + 1) * RET_DV]
            a_ref[:, h * RET_DV:(h + 1) * RET_DV] = _bf(on * gw * (g * _sigmoid(g)))

    tab = pl.BlockSpec((None, SEQ_T, RET_DK), lambda i, c: (i, c, 0))
    return _call(
        body, name="retention_fwd", grid=(b, nc), comm=comm,
        in_specs=[pl.BlockSpec((None, SEQ_T, 3072), lambda i, c: (i, c, 0)), tab, tab,
                  pl.BlockSpec((1, D_MODEL), lambda i, c: (0, 0))],
        out_specs=[pl.BlockSpec((None, SEQ_T, D_MODEL), lambda i, c: (i, c, 0)),
                   pl.BlockSpec((None, None, RET_HEADS, RET_DK, RET_DV), lambda i, c: (i, c, 0, 0, 0))],
        out_shape=[jax.ShapeDtypeStruct((b, s, D_MODEL), BF16),
                   jax.ShapeDtypeStruct((b, nc, RET_HEADS, RET_DK, RET_DV), F32)],
        scratch=[pltpu.VMEM((RET_HEADS, RET_DK, RET_DV), F32),
                 pltpu.VMEM((RET_HEADS, SEQ_T, SEQ_T), F32)],
    )(proj3, cos, sin, gnw)


def _retention_bwd(proj3, cos, sin, gnw, states, da3, comm=None):
    b, s, _ = proj3.shape
    nc = s // SEQ_T

    def body(p_ref, cos_ref, sin_ref, gnw_ref, st_ref, da_ref, d_ref, dgn_ref, dstate, wtab):
        c = pl.program_id(1)

        @pl.when(jnp.logical_and(pl.program_id(0) == 0, c == 0))
        def _():
            dgn_ref[...] = jnp.zeros_like(dgn_ref)
            for h in range(RET_HEADS):
                wtab[h] = _decay_matrix(h)

        @pl.when(c == 0)
        def _():
            dstate[...] = jnp.zeros_like(dstate)

        cs, sn = cos_ref[...], sin_ref[...]
        for h in range(RET_HEADS):
            qd, kd, gt = _decay_vectors(h)
            qr, kr, v, g = _ret_head(p_ref, cs, sn, h)
            st, dst, w = st_ref[h], dstate[h], wtab[h]
            qb, kb, vb = _bf(qr), _bf(kr), _bf(v)
            p = _dot_nt(qb, kb) * w
            o = _dot(_bf(p), vb) + _dot(_bf(qr * qd), _bf(st))
            on, rstd = _group_norm(o)
            gw = gnw_ref[:, h * RET_DV:(h + 1) * RET_DV]
            da = da_ref[:, h * RET_DV:(h + 1) * RET_DV].astype(F32)
            sg = _sigmoid(g)
            silu = g * sg
            dg = da * on * gw * (sg * (1.0 + g * (1.0 - sg)))
            dgn_ref[ROW_GN:ROW_GN + 1, h * RET_DV:(h + 1) * RET_DV] += jnp.sum(da * silu * on, axis=0, keepdims=True)
            don = da * silu * gw
            do = rstd * (don - jnp.mean(don, axis=-1, keepdims=True)
                         - on * jnp.mean(don * on, axis=-1, keepdims=True))
            dob = _bf(do)
            dp = _dot_nt(dob, vb) * w
            dqr = _dot(_bf(dp), kb) + _dot_nt(dob, _bf(st)) * qd
            dkr = _dot_tn(_bf(dp), qb) + _dot_nt(vb, _bf(dst)) * kd
            dv = _dot_tn(_bf(p), dob) + _dot(_bf(kr * kd), _bf(dst))
            dstate[h] = dst * gt + _dot_tn(_bf(qr * qd), dob)
            d_ref[:, h * RET_DK:(h + 1) * RET_DK] = _bf(_rotate_bwd(dqr, cs, sn))
            d_ref[:, 512 + h * RET_DK:512 + (h + 1) * RET_DK] = _bf(_rotate_bwd(dkr, cs, sn) * (RET_DK ** -0.5))
            d_ref[:, 1024 + h * RET_DV:1024 + (h + 1) * RET_DV] = _bf(dv)
            d_ref[:, 2048 + h * RET_DV:2048 + (h + 1) * RET_DV] = _bf(dg)

    rev = lambda i, c: (i, nc - 1 - c, 0)
    tab = pl.BlockSpec((None, SEQ_T, RET_DK), rev)
    return _call(
        body, name="retention_bwd", grid=(b, nc), comm=comm,
        in_specs=[pl.BlockSpec((None, SEQ_T, 3072), rev), tab, tab,
                  pl.BlockSpec((1, D_MODEL), lambda i, c: (0, 0)),
                  pl.BlockSpec((None, None, RET_HEADS, RET_DK, RET_DV), lambda i, c: (i, nc - 1 - c, 0, 0, 0)),
                  pl.BlockSpec((None, SEQ_T, D_MODEL), rev)],
        out_specs=[pl.BlockSpec((None, SEQ_T, 3072), rev),
                   pl.BlockSpec((8, D_MODEL), lambda i, c: (0, 0))],
        out_shape=[jax.ShapeDtypeStruct((b, s, 3072), BF16), jax.ShapeDtypeStruct((8, D_MODEL), F32)],
        scratch=[pltpu.VMEM((RET_HEADS, RET_DK, RET_DV), F32),
                 pltpu.VMEM((RET_HEADS, SEQ_T, SEQ_T), F32)],
    )(proj3, cos, sin, gnw, states, da3)


def _softplus_neg(lam):
    z = -lam
    u = jnp.exp(-jnp.abs(z))
    log1p_u = jnp.where(u < 0.01, u * (1.0 - u * (0.5 - u * (1.0 / 3.0))), jnp.log(1.0 + u))
    return jnp.maximum(z, 0.0) + log1p_u


def _lru_coeffs(xc, wr_ref, br_ref, wi_ref, bi_ref, lam_ref):
    rs, is_ = [], []
    for n in range(LRU_BLOCKS):
        xb = _bf(xc[:, n * LRU_BLOCK:(n + 1) * LRU_BLOCK])
        cols = slice(n * LRU_BLOCK, (n + 1) * LRU_BLOCK)
        rs.append(_sigmoid(_dot(xb, wr_ref[n]) + br_ref[:, cols]))
        is_.append(_sigmoid(_dot(xb, wi_ref[n]) + bi_ref[:, cols]))
    r = jnp.concatenate(rs, axis=1)
    i = jnp.concatenate(is_, axis=1)
    sp = _softplus_neg(lam_ref[...])
    la = -LRU_C * r * sp
    a = jnp.exp(la)
    s = jnp.sqrt(-jnp.tanh(la) * (a * a + 1.0))
    return r, i, a, s, sp


_LRU_PARAM_SPECS = [
    pl.BlockSpec((4, D_MODEL), lambda i, c: (0, 0)),
    pl.BlockSpec((1, D_MODEL), lambda i, c: (0, 0)),
    pl.BlockSpec((LRU_BLOCKS, LRU_BLOCK, LRU_BLOCK), lambda i, c: (0, 0, 0)),
    pl.BlockSpec((1, D_MODEL), lambda i, c: (0, 0)),
    pl.BlockSpec((LRU_BLOCKS, LRU_BLOCK, LRU_BLOCK), lambda i, c: (0, 0, 0)),
    pl.BlockSpec((1, D_MODEL), lambda i, c: (0, 0)),
    pl.BlockSpec((1, D_MODEL), lambda i, c: (0, 0)),
]


def _lru_fwd(proj3, params):
    b, s, _ = proj3.shape
    nc = s // SEQ_T

    def body(x_ref, y_ref, cw, cb, wr, br, wi, bi, lam, o_ref, h_ref, xc_ref, gy_ref, hdg_ref, xprev, hprev):
        @pl.when(pl.program_id(1) == 0)
        def _():
            xprev[...] = jnp.zeros_like(xprev)
            hprev[...] = jnp.zeros_like(hprev)

        x = x_ref[...]
        prev8 = xprev[...]
        xc = cb[...] + sum(cw[j:j + 1, :] * _shift_down(x, 3 - j, prev8) for j in range(4))
        xprev[...] = x[SEQ_T - 8:]
        xc_ref[...] = xc
        _, i, a, s_, _ = _lru_coeffs(xc, wr, br, wi, bi, lam)
        h = _scan_fwd(a, s_ * (i * xc), hprev[7:8, :])
        hprev[...] = h[SEQ_T - 8:]
        h_ref[...] = h
        gy, dgy = _gelu_and_grad(y_ref[...])
        o_ref[...] = _bf(h * gy)
        gy_ref[...] = _bf(gy)
        hdg_ref[...] = _bf(h * dgy)

    out = pl.BlockSpec((None, SEQ_T, D_MODEL), lambda i, c: (i, c, 0))
    half, full = jax.ShapeDtypeStruct((b, s, D_MODEL), BF16), jax.ShapeDtypeStruct((b, s, D_MODEL), F32)
    return _call(
        body, name="lru_fwd", grid=(b, nc),
        in_specs=[pl.BlockSpec((None, SEQ_T, D_MODEL), lambda i, c: (i, c, 3)),
                  pl.BlockSpec((None, SEQ_T, D_MODEL), lambda i, c: (i, c, 4))] + _LRU_PARAM_SPECS,
        out_specs=[out] * 5, out_shape=[half, full, full, half, half],
        scratch=[pltpu.VMEM((8, D_MODEL), F32), pltpu.VMEM((8, D_MODEL), F32)],
    )(proj3, proj3, *params)


def _lru_bwd(proj3, params, kept, db3, comm=None):
    b, s, _ = proj3.shape
    nc = s // SEQ_T
    blk8 = SEQ_T // 8
    hseq, xcseq, gyseq, hdgseq = kept

    def body(x_ref, xc_ref, h_ref, hp_ref, gy_ref, hdg_ref, db_ref, cw, cb, wr, br, wi, bi, lam,
             d_ref, dwr_ref, dwi_ref, sm_ref, gnext, anext, dxcnext):
        c = pl.program_id(1)
        first_chunk = c == nc - 1

        @pl.when(jnp.logical_and(pl.program_id(0) == 0, c == 0))
        def _():
            dwr_ref[...] = jnp.zeros_like(dwr_ref)
            dwi_ref[...] = jnp.zeros_like(dwi_ref)
            sm_ref[...] = jnp.zeros_like(sm_ref)

        @pl.when(c == 0)
        def _():
            gnext[...] = jnp.zeros_like(gnext)
            anext[...] = jnp.zeros_like(anext)
            dxcnext[...] = jnp.zeros_like(dxcnext)

        x, xc, h = x_ref[...], xc_ref[...], h_ref[...]
        hprev = hp_ref[...] * jnp.where(first_chunk, 0.0, 1.0)
        r, i, a, s_, sp = _lru_coeffs(xc, wr, br, wi, bi, lam)
        db = db_ref[...].astype(F32)
        dy = db * hdg_ref[...].astype(F32)
        a_up = _shift_up(a, 1, anext[...])
        g = _scan_bwd(a_up, db * gy_ref[...].astype(F32), gnext[0:1, :])
        gnext[...] = g[0:8]
        anext[...] = a[0:8]
        da = g * _shift_down(h, 1, hprev)
        ixc = i * xc
        dla = da * a - (g * ixc) * (a * a) / s_
        di = g * s_ * xc
        dxc = g * s_ * i
        dzr = dla * (-LRU_C * sp) * r * (1.0 - r)
        dzi = di * i * (1.0 - i)
        lam_v = lam[...]
        _row_acc(sm_ref, 7, jnp.sum(dla * (LRU_C * r), axis=0, keepdims=True) * _sigmoid(-lam_v))
        _row_acc(sm_ref, 5, jnp.sum(dzr, axis=0, keepdims=True))
        _row_acc(sm_ref, 6, jnp.sum(dzi, axis=0, keepdims=True))
        parts = []
        for n in range(LRU_BLOCKS):
            cols = slice(n * LRU_BLOCK, (n + 1) * LRU_BLOCK)
            xb, zr, zi = _bf(xc[:, cols]), _bf(dzr[:, cols]), _bf(dzi[:, cols])
            parts.append(dxc[:, cols] + _dot_nt(zr, wr[n]) + _dot_nt(zi, wi[n]))
            dwr_ref[n] += _dot_tn(xb, zr)
            dwi_ref[n] += _dot_tn(xb, zi)
        dxc = jnp.concatenate(parts, axis=1)
        _row_acc(sm_ref, 4, jnp.sum(dxc, axis=0, keepdims=True))
        nxt = dxcnext[...]
        dx = jnp.zeros_like(x)
        for j in range(4):
            ahead = _shift_up(dxc, 3 - j, nxt)
            dx = dx + cw[j:j + 1, :] * ahead
            _row_acc(sm_ref, j, jnp.sum(ahead * x, axis=0, keepdims=True))
        dxcnext[...] = dxc[0:8]
        d_ref[:, 0:D_MODEL] = _bf(dx)
        d_ref[:, D_MODEL:2 * D_MODEL] = _bf(dy)

    rev = lambda col: (lambda i, c: (i, nc - 1 - c, col))
    prev = lambda col: (lambda i, c: (i, jnp.maximum((nc - 1 - c) * blk8 - 1, 0), col))
    return _call(
        body, name="lru_bwd", grid=(b, nc), comm=comm,
        in_specs=[pl.BlockSpec((None, SEQ_T, D_MODEL), rev(3)),
                  pl.BlockSpec((None, SEQ_T, D_MODEL), rev(0)),
                  pl.BlockSpec((None, SEQ_T, D_MODEL), rev(0)),
                  pl.BlockSpec((None, 8, D_MODEL), prev(0)),
                  pl.BlockSpec((None, SEQ_T, D_MODEL), rev(0)),
                  pl.BlockSpec((None, SEQ_T, D_MODEL), rev(0)),
                  pl.BlockSpec((None, SEQ_T, D_MODEL), rev(0))] + _LRU_PARAM_SPECS,
        out_specs=[pl.BlockSpec((None, SEQ_T, 2 * D_MODEL), rev(0)),
                   pl.BlockSpec((LRU_BLOCKS, LRU_BLOCK, LRU_BLOCK), lambda i, c: (0, 0, 0)),
                   pl.BlockSpec((LRU_BLOCKS, LRU_BLOCK, LRU_BLOCK), lambda i, c: (0, 0, 0)),
                   pl.BlockSpec((8, D_MODEL), lambda i, c: (0, 0))],
        out_shape=[jax.ShapeDtypeStruct((b, s, 2 * D_MODEL), BF16),
                   jax.ShapeDtypeStruct((LRU_BLOCKS, LRU_BLOCK, LRU_BLOCK), F32),
                   jax.ShapeDtypeStruct((LRU_BLOCKS, LRU_BLOCK, LRU_BLOCK), F32),
                   jax.ShapeDtypeStruct((8, D_MODEL), F32)],
        scratch=[pltpu.VMEM((8, D_MODEL), F32)] * 3,
    )(proj3, xcseq, hseq, hseq, gyseq, hdgseq, db3, *params)


def _merge_parts(a_ref, b_ref, gr_ref, gl_ref, mgb_ref, wro_ref, wlo_ref):
    ya = _dot(a_ref[...], wro_ref[...])
    yb = _dot(b_ref[...], wlo_ref[...])
    sa = _sigmoid(gr_ref[...] + mgb_ref[0:1, :])
    sb = _sigmoid(gl_ref[...] + mgb_ref[1:2, :])
    return ya, yb, sa, sb


def _merge_specs(tm):
    row = lambda col: pl.BlockSpec((tm, D_MODEL), lambda i: (i, col))
    full = pl.BlockSpec((D_MODEL, D_MODEL), lambda i: (0, 0))
    return row, full


def _merge_fwd(a_in, b_in, proj, mgb, wro, wlo, wout, x, *, tm):
    m = x.shape[0]
    row, full = _merge_specs(tm)

    def body(a_ref, b_ref, gr_ref, gl_ref, mgb_ref, wro_ref, wlo_ref, wout_ref, x_ref, o_ref):
        ya, yb, sa, sb = _merge_parts(a_ref, b_ref, gr_ref, gl_ref, mgb_ref, wro_ref, wlo_ref)
        o_ref[...] = x_ref[...] + _dot(_bf(sa * ya + sb * yb), wout_ref[...])

    return _call(
        body, name="merge_fwd", grid=(m // tm,),
        in_specs=[row(0), row(0), row(5), row(6), pl.BlockSpec((2, D_MODEL), lambda i: (0, 0)),
                  full, full, full, row(0)],
        out_specs=row(0),
        out_shape=jax.ShapeDtypeStruct((m, D_MODEL), F32),
    )(a_in, b_in, proj, proj, mgb, wro, wlo, wout, x)


def _merge_bwd(a_in, b_in, proj, mgb, wro, wlo, wout, dx2, *, tm, comm=None):
    m = dx2.shape[0]
    row, full = _merge_specs(tm)

    def body(a_ref, b_ref, gr_ref, gl_ref, mgb_ref, wro_ref, wlo_ref, wout_ref, dx_ref,
             mix_ref, dya_ref, dyb_ref, da_ref, db_ref, dg_ref, sm_ref):
        @pl.when(pl.program_id(0) == 0)
        def _():
            sm_ref[...] = jnp.zeros_like(sm_ref)

        ya, yb, sa, sb = _merge_parts(a_ref, b_ref, gr_ref, gl_ref, mgb_ref, wro_ref, wlo_ref)
        mix_ref[...] = _bf(sa * ya + sb * yb)
        dmix = _dot_nt(_bf(dx_ref[...]), wout_ref[...])
        dya, dyb = _bf(dmix * sa), _bf(dmix * sb)
        dya_ref[...] = dya
        dyb_ref[...] = dyb
        dga = dmix * ya * sa * (1.0 - sa)
        dgb = dmix * yb * sb * (1.0 - sb)
        dg_ref[:, 0:D_MODEL] = _bf(dga)
        dg_ref[:, D_MODEL:2 * D_MODEL] = _bf(dgb)
        _row_acc(sm_ref, ROW_MGB, jnp.sum(dga, axis=0, keepdims=True))
        _row_acc(sm_ref, ROW_MGB + 1, jnp.sum(dgb, axis=0, keepdims=True))
        da_ref[...] = _bf(_dot_nt(dya, wro_ref[...]))
        db_ref[...] = _bf(_dot_nt(dyb, wlo_ref[...]))

    act = jax.ShapeDtypeStruct((m, D_MODEL), BF16)
    return _call(
        body, name="merge_bwd", grid=(m // tm,), comm=comm,
        in_specs=[row(0), row(0), row(5), row(6), pl.BlockSpec((2, D_MODEL), lambda i: (0, 0)),
                  full, full, full, row(0)],
        out_specs=[row(0)] * 5 + [pl.BlockSpec((tm, 2 * D_MODEL), lambda i: (i, 0)),
                                  pl.BlockSpec((8, D_MODEL), lambda i: (0, 0))],
        out_shape=[act] * 5 + [jax.ShapeDtypeStruct((m, 2 * D_MODEL), BF16),
                               jax.ShapeDtypeStruct((8, D_MODEL), F32)],
    )(a_in, b_in, proj, proj, mgb, wro, wlo, wout, dx2)


def _ffn_act_fwd(up3, cw, cb):
    b, s, _ = up3.shape

    def body(g_ref, v_ref, cw_ref, cb_ref, o_ref, act_ref, q_ref, gprev):
        @pl.when(pl.program_id(1) == 0)
        def _():
            gprev[...] = jnp.zeros_like(gprev)

        gate, val = g_ref[...], v_ref[...]
        prev8 = gprev[...]
        gc = cb_ref[...] + sum(cw_ref[j:j + 1, :] * _shift_down(gate, 2 - j, prev8) for j in range(3))
        gprev[...] = gate[SEQ_T - 8:]
        act, dact = _gelu_and_grad(gc)
        o_ref[...] = _bf(act * val)
        act_ref[...] = _bf(act)
        q_ref[...] = _bf(dact * val)

    out = pl.BlockSpec((None, SEQ_T, D_FF), lambda i, c: (i, c, 0))
    return _call(
        body, name="ffn_act_fwd", grid=(b, s // SEQ_T),
        in_specs=[pl.BlockSpec((None, SEQ_T, D_FF), lambda i, c: (i, c, 0)),
                  pl.BlockSpec((None, SEQ_T, D_FF), lambda i, c: (i, c, 1)),
                  pl.BlockSpec((3, D_FF), lambda i, c: (0, 0)),
                  pl.BlockSpec((1, D_FF), lambda i, c: (0, 0))],
        out_specs=[out] * 3,
        out_shape=[jax.ShapeDtypeStruct((b, s, D_FF), BF16)] * 3,
        scratch=[pltpu.VMEM((8, D_FF), F32)],
    )(up3, up3, cw, cb)


def _ffn_act_bwd(up3, act3, q3, cw, df3, comm=None):
    b, s, _ = up3.shape
    nc = s // SEQ_T

    def body(g_ref, act_ref, q_ref, df_ref, cw_ref, dg_ref, dv_ref, sm_ref, dgcnext):
        c = pl.program_id(1)

        @pl.when(jnp.logical_and(pl.program_id(0) == 0, c == 0))
        def _():
            sm_ref[...] = jnp.zeros_like(sm_ref)

        @pl.when(c == 0)
        def _():
            dgcnext[...] = jnp.zeros_like(dgcnext)

        gate = g_ref[...]
        df = df_ref[...].astype(F32)
        dv_ref[...] = _bf(df * act_ref[...].astype(F32))
        dgc = df * q_ref[...].astype(F32)
        nxt = dgcnext[...]
        dgate = jnp.zeros_like(gate)
        for j in range(3):
            ahead = _shift_up(dgc, 2 - j, nxt)
            dgate = dgate + cw_ref[j:j + 1, :] * ahead
            _row_acc(sm_ref, j, jnp.sum(ahead * gate, axis=0, keepdims=True))
        _row_acc(sm_ref, 3, jnp.sum(dgc, axis=0, keepdims=True))
        dgcnext[...] = dgc[0:8]
        dg_ref[...] = _bf(dgate)

    rev = pl.BlockSpec((None, SEQ_T, D_FF), lambda i, c: (i, nc - 1 - c, 0))
    return _call(
        body, name="ffn_act_bwd", grid=(b, nc), comm=comm,
        in_specs=[rev, rev, rev, rev, pl.BlockSpec((3, D_FF), lambda i, c: (0, 0))],
        out_specs=[rev, rev, pl.BlockSpec((8, D_FF), lambda i, c: (0, 0))],
        out_shape=[jax.ShapeDtypeStruct((b, s, D_FF), BF16)] * 2 + [jax.ShapeDtypeStruct((8, D_FF), F32)],
        scratch=[pltpu.VMEM((8, D_FF), F32)],
    )(up3, act3, q3, df3, cw)


def _ffn_down_loss(f, wd, x2, nfw, target, *, tm):
    m, kf = f.shape
    nt = m // tm

    def body(f_ref, wd_ref, x_ref, nw_ref, t_ref, loss_ref, dx_ref, dnw_ref, lsum):
        i = pl.program_id(0)

        @pl.when(i == 0)
        def _():
            dnw_ref[...] = jnp.zeros_like(dnw_ref)
            lsum[...] = jnp.zeros_like(lsum)

        x3 = x_ref[...] + _dot(f_ref[...], wd_ref[...])
        nw = nw_ref[...]
        xh, _ = _rms(x3)
        err = xh * nw - t_ref[...]
        lsum[...] += jnp.sum(err * err, axis=0, keepdims=True)
        dx, dnw = _rms_bwd(err * (1.0 / D_MODEL), x3, nw)
        dx_ref[...] = dx
        _row_acc(dnw_ref, ROW_NF, dnw)

        @pl.when(i == nt - 1)
        def _():
            loss_ref[...] = jnp.sum(lsum[...], axis=1, keepdims=True) * (0.5 / D_MODEL)

    row = pl.BlockSpec((tm, D_MODEL), lambda i: (i, 0))
    return _call(
        body, name="ffn_down_loss", grid=(nt,),
        in_specs=[pl.BlockSpec((tm, kf), lambda i: (i, 0)),
                  pl.BlockSpec((kf, D_MODEL), lambda i: (0, 0)),
                  row, pl.BlockSpec((1, D_MODEL), lambda i: (0, 0)), row],
        out_specs=[pl.BlockSpec((1, 1), lambda i: (0, 0)), row,
                   pl.BlockSpec((8, D_MODEL), lambda i: (0, 0))],
        out_shape=[jax.ShapeDtypeStruct((1, 1), F32), jax.ShapeDtypeStruct((m, D_MODEL), F32),
                   jax.ShapeDtypeStruct((8, D_MODEL), F32)],
        scratch=[pltpu.VMEM((1, D_MODEL), F32)],
    )(f, wd, x2, nfw, target)


def _row_tile(rows):
    return next((t for t in (256, 128, 64, 32, 16, 8) if rows % t == 0), rows)


def _adamw(w, gs, m, v, *, name):
    rows, cols = w.shape
    tr = _row_tile(rows)
    ng = len(gs)

    def body(w_ref, *rest):
        g_refs, (m_ref, v_ref, g_out, d_out, m_out, v_out) = rest[:ng], rest[ng:]
        g = g_refs[0][...]
        for r in g_refs[1:]:
            g = g + r[...]
        mn = ADAM_B1 * m_ref[...] + (1.0 - ADAM_B1) * g
        vn = ADAM_B2 * v_ref[...] + (1.0 - ADAM_B2) * (g * g)
        m_hat = mn / (1.0 - ADAM_B1 ** ADAM_STEP)
        v_hat = vn / (1.0 - ADAM_B2 ** ADAM_STEP)
        g_out[...] = g
        d_out[...] = -ADAM_LR * (m_hat / (jnp.sqrt(v_hat) + ADAM_EPS) + ADAM_WD * w_ref[...])
        m_out[...] = mn
        v_out[...] = vn

    spec = pl.BlockSpec((tr, cols), lambda i: (i, 0))
    return _call(
        body, name=name, grid=(rows // tr,),
        in_specs=[spec] * (3 + ng), out_specs=[spec] * 4,
        out_shape=[jax.ShapeDtypeStruct((rows, cols), F32)] * 4,
    )(w, *gs, m, v)


def _mesh_pos():
    x, y, c = lax.axis_index("x"), lax.axis_index("y"), lax.axis_index("c")
    return x, y, c


def _other_chips(x, y, c):
    return [((1 - x, y, c), 2 * (1 - x) + y), ((x, 1 - y, c), 2 * x + 1 - y),
            ((1 - x, 1 - y, c), 2 * (1 - x) + 1 - y)]


def _region(ref, axis, size, half_axis, chip, core=None):
    idx = [slice(None)] * len(ref.shape)
    if core is None:
        idx[axis] = pl.ds(pl.multiple_of(chip * size, size), size)
    elif half_axis == axis:
        h = size // 2
        idx[axis] = pl.ds(pl.multiple_of(chip * size + core * h, h), h)
    else:
        idx[axis] = pl.ds(pl.multiple_of(chip * size, size), size)
        h = ref.shape[half_axis] // 2
        idx[half_axis] = pl.ds(pl.multiple_of(core * h, h), h)
    return ref.at[tuple(idx)]


def _half(ref, half_axis, core):
    idx = [slice(None)] * len(ref.shape)
    h = ref.shape[half_axis] // 2
    idx[half_axis] = pl.ds(pl.multiple_of(core * h, h), h)
    return ref.at[tuple(idx)]


class _Copy:
    def __init__(self, make):
        self._make = make

    def start(self):
        self._make().start()

    def wait(self):
        self._make().wait()

    def wait_send(self):
        self._make().wait_send()

    def wait_recv(self):
        self._make().wait_recv()


def _remote(src, dst, send_sem, recv_sem, dev):
    return _Copy(lambda: pltpu.make_async_remote_copy(
        src_ref=src, dst_ref=dst, send_sem=send_sem, recv_sem=recv_sem, device_id=dev, device_id_type=MESH))


def _local(src, dst, sem):
    return _Copy(lambda: pltpu.make_async_copy(src, dst, sem))


def _dma_sems(n):
    return pltpu.SemaphoreType.DMA((n,))


def _place_shard(w, chip, axis, *, name):
    shape = list(w.shape)
    shape[axis] *= N_CHIPS
    if w.ndim == 3:
        block, grid = (1,) + w.shape[1:], (w.shape[0],)
        in_map, out_map = (lambda i, chip: (i, 0, 0)), (lambda i, chip: (i, chip[0], 0))
    else:
        tr = _row_tile(w.shape[0])
        nt = w.shape[0] // tr
        block, grid = (tr, w.shape[1]), (nt,)
        in_map = lambda i, chip: (i, 0)
        out_map = (lambda i, chip: (chip[0] * nt + i, 0)) if axis == 0 else (lambda i, chip: (i, chip[0]))

    def body(chip_ref, w_ref, o_ref):
        o_ref[...] = _bf(w_ref[...])

    return _call(body, name=name, grid=grid, prefetch=1, in_specs=[pl.BlockSpec(block, in_map)],
                 out_specs=pl.BlockSpec(block, out_map),
                 out_shape=jax.ShapeDtypeStruct(tuple(shape), BF16))(chip, w)


def _ici_leg(srcs, dsts, layout, sizes, n_whole, sems):
    send_sems, recv_sems, local_sems = sems
    x, y, c = _mesh_pos()
    mine = 2 * x + y
    n_big = len(srcs) - n_whole
    local, sends, recvs = [], [], []
    for t, (src, dst) in enumerate(zip(srcs, dsts)):
        if t < n_big:
            ax, hx = layout[t]
            part = _region(src, ax, sizes[t], hx, mine, c)
            landing = lambda chip, dst=dst, ax=ax, hx=hx, size=sizes[t]: _region(dst, ax, size, hx, chip, c)
        else:
            part, landing = src, (lambda chip, dst=dst: dst.at[chip])
            local.append(_local(src, dst.at[mine], local_sems.at[t - n_big]))
        for k, (dev, chip) in enumerate(_other_chips(x, y, c)):
            sends.append(_remote(part, landing(mine), send_sems.at[3 * t + k], recv_sems.at[3 * t + k], dev))
            recvs.append(_remote(part, landing(chip), send_sems.at[3 * t + k], recv_sems.at[3 * t + k], dev))
    return local, sends, recvs


def _d2d_leg(srcs, dsts, layout, sizes, sems):
    send_sems, recv_sems = sems
    x, y, c = _mesh_pos()
    sends, recvs = [], []
    for t, (src, dst) in enumerate(zip(srcs, dsts)):
        ax, hx = layout[t]
        for k, (_, chip) in enumerate(_other_chips(x, y, c)):
            sem = (send_sems.at[3 * t + k], recv_sems.at[3 * t + k])
            sends.append(_remote(_region(src, ax, sizes[t], hx, chip, c),
                                 _region(dst, ax, sizes[t], hx, chip, c), *sem, (x, y, 1 - c)))
            recvs.append(_remote(_region(src, ax, sizes[t], hx, chip, 1 - c),
                                 _region(dst, ax, sizes[t], hx, chip, 1 - c), *sem, (x, y, 1 - c)))
    return sends, recvs


def _gather_shapes(bufs, whole):
    return ([jax.ShapeDtypeStruct(b.shape, b.dtype) for b in bufs]
            + [jax.ShapeDtypeStruct((N_CHIPS,) + w.shape, w.dtype) for w in whole])


def _gather_ici(bufs, layout):
    n = len(bufs)
    sizes = [b.shape[ax] // N_CHIPS for b, (ax, _) in zip(bufs, layout)]

    def start(ins, outs, sems):
        for cp in _ici_leg(ins, outs, layout, sizes, 0, (*sems, None))[1]:
            cp.start()

    def finish(ins, outs, sems):
        _, sends, recvs = _ici_leg(ins, outs, layout, sizes, 0, (*sems, None))
        for cp in recvs:
            cp.wait_recv()
        for cp in sends:
            cp.wait_send()

    return _Comm(bufs, _gather_shapes(bufs, ()), [_dma_sems(3 * n), _dma_sems(3 * n)], start, finish,
                 aliases={i: i for i in range(n)})


def _gather_d2d(bufs, layout, sizes):
    n = len(bufs)

    def start(ins, outs, sems):
        for cp in _d2d_leg(ins, outs, layout, sizes, sems)[0]:
            cp.start()

    def finish(ins, outs, sems):
        sends, recvs = _d2d_leg(ins, outs, layout, sizes, sems)
        for cp in recvs:
            cp.wait_recv()
        for cp in sends:
            cp.wait_send()

    return _Comm(bufs, [jax.ShapeDtypeStruct(b.shape, b.dtype) for b in bufs],
                 [_dma_sems(3 * n), _dma_sems(3 * n)], start, finish, aliases={i: i for i in range(n)})


def _gather_both(bufs, layout, whole):
    n, nb = len(bufs) + len(whole), len(bufs)
    sizes = [b.shape[ax] // N_CHIPS for b, (ax, _) in zip(bufs, layout)]

    def start(ins, outs, sems):
        local, sends, _ = _ici_leg(ins, outs, layout, sizes, len(whole), sems[:3])
        for cp in local + sends:
            cp.start()

    def finish(ins, outs, sems):
        local, sends, recvs = _ici_leg(ins, outs, layout, sizes, len(whole), sems[:3])
        for cp in recvs:
            cp.wait_recv()
        onward, arriving = _d2d_leg(outs[:nb], outs[:nb], layout, sizes, sems[3:])
        for cp in onward:
            cp.start()
        for cp in arriving:
            cp.wait_recv()
        for cp in sends + onward:
            cp.wait_send()
        for cp in local:
            cp.wait()

    return _Comm(list(bufs) + list(whole), _gather_shapes(bufs, whole),
                 [_dma_sems(3 * n), _dma_sems(3 * n), _dma_sems(len(whole)), _dma_sems(3 * nb), _dma_sems(3 * nb)],
                 start, finish, aliases={i: i for i in range(nb)})


def _exchange(grads, layout):
    n = len(grads)
    others = N_DEV - 1
    sizes = [g.shape[ax] // N_CHIPS for g, (ax, _) in zip(grads, layout)]
    out_shapes = []
    for g, (ax, hx), sz in zip(grads, layout, sizes):
        shape = list(g.shape)
        shape[ax] = sz
        shape[hx] //= 2
        out_shapes.append(jax.ShapeDtypeStruct((others,) + tuple(shape), g.dtype))

    def copies(ins, outs, sems):
        send_sems, recv_sems = sems
        x, y, c = _mesh_pos()
        sends, recvs = [], []
        for t, (src, dst) in enumerate(zip(ins, outs)):
            ax, hx = layout[t]
            for r in range(1, N_DEV):
                px = (1 - x) if r & 4 else x
                py = (1 - y) if r & 2 else y
                pc = (1 - c) if r & 1 else c
                sem = (send_sems.at[others * t + r - 1], recv_sems.at[others * t + r - 1])
                part = _region(src, ax, sizes[t], hx, 2 * px + py, pc)
                sends.append(_remote(part, dst.at[r - 1], *sem, (px, py, pc)))
                recvs.append(_remote(part, dst.at[r - 1], *sem, (px, py, pc)))
        return sends, recvs

    def start(ins, outs, sems):
        for cp in copies(ins, outs, sems)[0]:
            cp.start()

    def finish(ins, outs, sems):
        sends, recvs = copies(ins, outs, sems)
        for cp in recvs:
            cp.wait_recv()
        for cp in sends:
            cp.wait_send()

    return _Comm(grads, out_shapes, [_dma_sems(others * n), _dma_sems(others * n)], start, finish)


def _reduce_half(g, parts, pos, cut, *, name):
    ax, _ = cut
    others = parts.shape[0]
    if g.ndim == 3:
        nb, rows, cols = g.shape
        hb = nb // 2
        block, grid, out_shape = (1, rows // N_CHIPS, cols), (hb,), (nb, rows // N_CHIPS, cols)
        g_map = lambda i, pos: (pos[1] * hb + i, pos[0], 0)
        o_map = lambda i, pos: (pos[1] * hb + i, 0, 0)
        p_map = lambda i, pos: (0, i, 0, 0)
    elif ax == 1:
        rows, cols = g.shape
        tr = _row_tile(rows // 2)
        nt = rows // 2 // tr
        block, grid, out_shape = (tr, cols // N_CHIPS), (nt,), (rows, cols // N_CHIPS)
        g_map = lambda i, pos: (pos[1] * nt + i, pos[0])
        o_map = lambda i, pos: (pos[1] * nt + i, 0)
        p_map = lambda i, pos: (0, i, 0)
    else:
        rows, cols = g.shape
        tr = _row_tile(rows // N_CHIPS // 2)
        nt = rows // N_CHIPS // 2 // tr
        block, grid, out_shape = (tr, cols), (nt,), (rows // N_CHIPS, cols)
        g_map = lambda i, pos: (pos[0] * 2 * nt + pos[1] * nt + i, 0)
        o_map = lambda i, pos: (pos[1] * nt + i, 0)
        p_map = lambda i, pos: (0, i, 0)

    def body(pos_ref, g_ref, p_ref, o_ref):
        acc = g_ref[...].astype(F32)
        for r in range(others):
            acc = acc + p_ref[r].astype(F32)
        o_ref[...] = acc

    return _call(
        body, name=name, grid=grid, prefetch=1,
        in_specs=[pl.BlockSpec(block, g_map), pl.BlockSpec((others,) + block, p_map)],
        out_specs=pl.BlockSpec(block, o_map), out_shape=jax.ShapeDtypeStruct(out_shape, F32),
    )(pos, g, parts)


def _join_halves(bufs):
    n = len(bufs)

    def copies(ins, outs, sems):
        send_sems, recv_sems = sems
        x, y, c = _mesh_pos()
        sends = [_remote(_half(src, 0, c), _half(dst, 0, c), send_sems.at[t], recv_sems.at[t], (x, y, 1 - c))
                 for t, (src, dst) in enumerate(zip(ins, outs))]
        recvs = [_remote(_half(src, 0, 1 - c), _half(dst, 0, 1 - c), send_sems.at[t], recv_sems.at[t],
                         (x, y, 1 - c)) for t, (src, dst) in enumerate(zip(ins, outs))]
        return sends, recvs

    def start(ins, outs, sems):
        for cp in copies(ins, outs, sems)[0]:
            cp.start()

    def finish(ins, outs, sems):
        sends, recvs = copies(ins, outs, sems)
        for cp in recvs:
            cp.wait_recv()
        for cp in sends:
            cp.wait_send()

    comm = _Comm(bufs, [jax.ShapeDtypeStruct(b.shape, b.dtype) for b in bufs],
                 [_dma_sems(n), _dma_sems(n)], start, finish, aliases={i: i for i in range(n)})
    return _call(None, name="join_halves", comm=comm)()[1]


def _allreduce_small(pack):
    rows, cols = pack.shape

    def body(p_ref, o_ref, slots, send_sems, recv_sems):
        x, y, c = _mesh_pos()
        me = 4 * x + 2 * y + c
        slots[me] = p_ref[...]
        copies = []
        for r in range(1, N_DEV):
            fx, fy, fc = (r >> 2) & 1, (r >> 1) & 1, r & 1
            dev = ((1 - x) if fx else x, (1 - y) if fy else y, (1 - c) if fc else c)
            cp = pltpu.make_async_remote_copy(
                src_ref=p_ref, dst_ref=slots.at[me], send_sem=send_sems.at[r - 1],
                recv_sem=recv_sems.at[r - 1], device_id=dev, device_id_type=MESH)
            cp.start()
            copies.append(cp)
        for cp in copies:
            cp.wait_recv()
        for cp in copies:
            cp.wait_send()
        acc = slots[0]
        for d in range(1, N_DEV):
            acc = acc + slots[d]
        o_ref[...] = acc

    vmem = pl.BlockSpec(memory_space=pltpu.VMEM)
    return _call(
        body, name="allreduce_small", in_specs=[vmem], out_specs=vmem,
        out_shape=jax.ShapeDtypeStruct((rows, cols), F32),
        scratch=[pltpu.VMEM((N_DEV, rows, cols), F32), pltpu.SemaphoreType.DMA((N_DEV - 1,)),
                 pltpu.SemaphoreType.DMA((N_DEV - 1,))],
    )(pack)


def _pad_rows(a, rows=8):
    return jnp.pad(a, ((0, rows - a.shape[0]), (0, 0)))


def kernel(x, positions, norm1_w, w_in, merge_gate_b, ret_gn_w, w_ret_o, lru_conv_w, lru_conv_b, lru_w_r, lru_b_r, lru_w_i, lru_b_i, lru_lambda, w_lru_o, w_out, norm2_w, ffn_w_up, ffn_conv_w, ffn_conv_b, ffn_w_down, norm_f_w, loss_target, m_norm1_w, m_w_in, m_merge_gate_b, m_ret_gn_w, m_w_ret_o, m_lru_conv_w, m_lru_conv_b, m_lru_w_r, m_lru_b_r, m_lru_w_i, m_lru_b_i, m_lru_lambda, m_w_lru_o, m_w_out, m_norm2_w, m_ffn_w_up, m_ffn_conv_w, m_ffn_conv_b, m_ffn_w_down, m_norm_f_w, v_norm1_w, v_w_in, v_merge_gate_b, v_ret_gn_w, v_w_ret_o, v_lru_conv_w, v_lru_conv_b, v_lru_w_r, v_lru_b_r, v_lru_w_i, v_lru_b_i, v_lru_lambda, v_w_lru_o, v_w_out, v_norm2_w, v_ffn_w_up, v_ffn_conv_w, v_ffn_conv_b, v_ffn_w_down, v_norm_f_w):
    names = ["norm1_w", "w_in", "merge_gate_b", "ret_gn_w", "w_ret_o", "lru_conv_w", "lru_conv_b", "lru_w_r",
             "lru_b_r", "lru_w_i", "lru_b_i", "lru_lambda", "w_lru_o", "w_out", "norm2_w", "ffn_w_up",
             "ffn_conv_w", "ffn_conv_b", "ffn_w_down", "norm_f_w"]
    w_args = dict(zip(names, [norm1_w, w_in, merge_gate_b, ret_gn_w, w_ret_o, lru_conv_w, lru_conv_b, lru_w_r,
                              lru_b_r, lru_w_i, lru_b_i, lru_lambda, w_lru_o, w_out, norm2_w, ffn_w_up,
                              ffn_conv_w, ffn_conv_b, ffn_w_down, norm_f_w]))
    m_args = dict(zip(names, [m_norm1_w, m_w_in, m_merge_gate_b, m_ret_gn_w, m_w_ret_o, m_lru_conv_w,
                              m_lru_conv_b, m_lru_w_r, m_lru_b_r, m_lru_w_i, m_lru_b_i, m_lru_lambda, m_w_lru_o,
                              m_w_out, m_norm2_w, m_ffn_w_up, m_ffn_conv_w, m_ffn_conv_b, m_ffn_w_down,
                              m_norm_f_w]))
    v_args = dict(zip(names, [v_norm1_w, v_w_in, v_merge_gate_b, v_ret_gn_w, v_w_ret_o, v_lru_conv_w,
                              v_lru_conv_b, v_lru_w_r, v_lru_b_r, v_lru_w_i, v_lru_b_i, v_lru_lambda, v_w_lru_o,
                              v_w_out, v_norm2_w, v_ffn_w_up, v_ffn_conv_w, v_ffn_conv_b, v_ffn_w_down,
                              v_norm_f_w]))

    bsz, seq, d = x.shape
    m = bsz * seq
    tm = min(MM_ROWS, m)
    tm_fused = min(FUSED_ROWS, m)
    chip = 2 * lax.axis_index("x") + lax.axis_index("y")

    big = ["w_in", "w_ret_o", "w_lru_o", "w_out", "lru_w_r", "lru_w_i", "ffn_w_up", "ffn_w_down"]
    cut = dict(w_in=(1, 0), w_ret_o=(0, 0), w_lru_o=(0, 0), w_out=(0, 0), lru_w_r=(1, 0), lru_w_i=(1, 0),
               ffn_w_up=(1, 0), ffn_w_down=(0, 0))
    later = big[1:]
    core = lax.axis_index("c")
    chip1 = jnp.reshape(chip, (1,)).astype(jnp.int32)
    pos = jnp.stack([chip, core]).astype(jnp.int32)
    placed = {n: _place_shard(w_args[n][0], chip1, cut[n][0], name="place_" + n) for n in big}
    small_pack = jnp.concatenate([
        jnp.pad(merge_gate_b[0], ((0, 6), (0, 512))),
        jnp.pad(lru_conv_w[0], ((0, 4), (0, 512))),
        jnp.pad(lru_b_r[0], ((0, 4), (0, 704))),
        jnp.pad(lru_b_i[0], ((0, 4), (0, 704))),
        jnp.pad(ffn_conv_w[0], ((0, 5), (0, 0))),
    ], axis=0)
    _, (w_in_full, sp) = _call(None, name="gather_w_in",
                               comm=_gather_both([placed["w_in"]], [cut["w_in"]], [small_pack]))()
    wb = {"w_in": w_in_full}
    mgb = jnp.transpose(sp[:, 0:2, 0:256], (1, 0, 2)).reshape(2, D_MODEL)
    lcw = jnp.transpose(sp[:, 8:12, 0:256], (1, 0, 2)).reshape(4, D_MODEL)
    lbr = jnp.transpose(sp[:, 16:20, 0:64], (1, 0, 2)).reshape(1, D_MODEL)
    lbi = jnp.transpose(sp[:, 24:28, 0:64], (1, 0, 2)).reshape(1, D_MODEL)
    fcw = jnp.transpose(sp[:, 32:35, :], (1, 0, 2)).reshape(3, D_FF)
    nfw = norm_f_w.reshape(1, D_MODEL)

    x2d = x.reshape(m, d)
    half = RET_DK // 2
    inv_freq = ROPE_BASE ** (-jnp.arange(half, dtype=F32) / half)
    cos, sin = _rope_tables(positions.reshape(bsz, seq, 1), jnp.concatenate([inv_freq, inv_freq]).reshape(1, RET_DK))
    later_cut = [cut[n] for n in later]
    (proj, h1), bufs = _norm_matmul(x2d, norm1_w, wb["w_in"], name="in_proj", tm=tm, tn=MM_COLS,
                                    comm=_gather_ici([placed[n] for n in later], later_cut))
    proj3 = proj.reshape(bsz, seq, D_IN)
    (a_in3, states), bufs = _retention_fwd(
        proj3, cos, sin, ret_gn_w,
        comm=_gather_d2d(bufs, later_cut, [w_args[n].shape[1 + cut[n][0]] for n in later]))
    wb.update(zip(later, bufs))
    lru_params = (lcw, lru_conv_b, wb["lru_w_r"], lbr, wb["lru_w_i"], lbi, lru_lambda)
    b_in3, *lru_kept = _lru_fwd(proj3, lru_params)
    a_in, b_in = a_in3.reshape(m, d), b_in3.reshape(m, d)
    x2 = _merge_fwd(a_in, b_in, proj, mgb, wb["w_ret_o"], wb["w_lru_o"], wb["w_out"], x2d, tm=tm_fused)
    up, h2 = _norm_matmul(x2, norm2_w, wb["ffn_w_up"], name="ffn_up", tm=tm, tn=MM_COLS)
    up3 = up.reshape(bsz, seq, 2 * D_FF)
    f3, act3, q3 = _ffn_act_fwd(up3, fcw, ffn_conv_b)
    f = f3.reshape(m, D_FF)
    loss_dev, dx3, sm_nf = _ffn_down_loss(f, wb["ffn_w_down"], x2, nfw, loss_target.reshape(m, d), tm=tm_fused)
    loss = lax.psum(loss_dev[0, 0], ("x", "y", "c"))

    def send(*ns):
        return _exchange([g_full[n] for n in ns], [cut[n] for n in ns])

    g_full, parts = {}, {}
    df = _mm_nt(dx3, wb["ffn_w_down"], name="ffn_down_dx", tm=tm, out_dtype=BF16)
    g_full["ffn_w_down"] = _mm_tn(f, [dx3], name="ffn_down_dw", tm=tm)
    (dgate3, dval3, sm_ffn), (parts["ffn_w_down"],) = _ffn_act_bwd(
        up3, act3, q3, fcw, df.reshape(bsz, seq, D_FF), comm=send("ffn_w_down"))
    dup = [dgate3.reshape(m, D_FF), dval3.reshape(m, D_FF)]
    g_full["ffn_w_up"] = _mm_tn(h2, dup, name="ffn_up_dw", tm=tm)
    (dx2, sm_n2), (parts["ffn_w_up"],) = _mm_nt_normbwd(
        dup, wb["ffn_w_up"], x2, norm2_w, dx3, name="ffn_up_dx", tm=tm, row=ROW_N2, comm=send("ffn_w_up"))
    mix, dya, dyb, da_in, db_in, dgates, sm_mg = _merge_bwd(
        a_in, b_in, proj, mgb, wb["w_ret_o"], wb["w_lru_o"], wb["w_out"], dx2, tm=tm_fused)
    g_full["w_out"] = _mm_tn(mix, [dx2], name="out_dw", tm=tm)
    g_full["w_ret_o"] = _mm_tn(a_in, [dya], name="ret_o_dw", tm=tm)
    g_full["w_lru_o"] = _mm_tn(b_in, [dyb], name="lru_o_dw", tm=tm)
    (dlru3, dwr, dwi, sm_lru), (parts["w_out"], parts["w_ret_o"], parts["w_lru_o"]) = _lru_bwd(
        proj3, lru_params, lru_kept, db_in.reshape(bsz, seq, d), comm=send("w_out", "w_ret_o", "w_lru_o"))
    g_full["lru_w_r"], g_full["lru_w_i"] = dwr.astype(BF16), dwi.astype(BF16)
    (dret3, sm_gn), (parts["lru_w_r"], parts["lru_w_i"]) = _retention_bwd(
        proj3, cos, sin, ret_gn_w, states, da_in.reshape(bsz, seq, d), comm=send("lru_w_r", "lru_w_i"))
    dproj = [dret3.reshape(m, 3072), dlru3.reshape(m, 2048), dgates]
    g_full["w_in"] = _mm_tn(h1, dproj, name="in_proj_dw", tm=tm)
    (grad_x, sm_n1), (parts["w_in"],) = _mm_nt_normbwd(
        dproj, wb["w_in"], x2d, norm1_w, dx2, name="in_proj_dx", tm=tm, row=ROW_N1, comm=send("w_in"))

    reduced = _join_halves([_reduce_half(g_full[n], parts[n], pos, cut[n], name="sum_" + n) for n in big])
    misc = sm_n1 + sm_mg + sm_gn + sm_n2 + sm_nf
    pack = jnp.concatenate(
        [misc, sm_lru, sm_ffn[:, 0:1024], sm_ffn[:, 1024:2048], sm_ffn[:, 2048:3072]], axis=0)
    tot = _allreduce_small(pack)
    ffn_sm = jnp.concatenate([tot[16:24], tot[24:32], tot[32:40]], axis=1)
    g_small = {
        "norm1_w": tot[ROW_N1:ROW_N1 + 1], "merge_gate_b": tot[ROW_MGB:ROW_MGB + 2],
        "ret_gn_w": tot[ROW_GN:ROW_GN + 1], "norm2_w": tot[ROW_N2:ROW_N2 + 1], "norm_f_w": tot[ROW_NF:ROW_NF + 1],
        "lru_conv_w": tot[8:12], "lru_conv_b": tot[12:13], "lru_b_r": tot[13:14].reshape(4, 256),
        "lru_b_i": tot[14:15].reshape(4, 256), "lru_lambda": tot[15:16],
        "ffn_conv_w": ffn_sm[0:3], "ffn_conv_b": ffn_sm[3:4],
    }
    small_shard = dict(merge_gate_b=256, lru_conv_w=256, lru_b_r=64, lru_b_i=64, ffn_conv_w=768)

    outs = {}
    for n, g in zip(big, reduced):
        shape = w_args[n].shape
        g = g.reshape(-1, g.shape[-1])
        outs[n] = [o.reshape(shape) for o in _adamw(
            w_args[n].reshape(g.shape), [g], m_args[n].reshape(g.shape), v_args[n].reshape(g.shape),
            name="adamw_" + n)]
    for n, g in g_small.items():
        shape = w_args[n].shape
        if n in small_shard:
            g = lax.dynamic_slice_in_dim(g, chip * small_shard[n], small_shard[n], axis=1)
        w2 = w_args[n].reshape(g.shape)
        outs[n] = [o.reshape(shape) for o in _adamw(
            w2, [g], m_args[n].reshape(g.shape), v_args[n].reshape(g.shape), name="adamw_" + n)]

    result = [loss, grad_x.reshape(bsz, seq, d)]
    for k in range(4):
        result += [outs[n][k] for n in names]
    return tuple(result)
```

```python
import functools
import math

import numpy as np
import jax
import jax.numpy as jnp
from jax import lax
from jax.experimental import pallas as pl
from jax.experimental.pallas import tpu as pltpu

F32 = jnp.float32
BF16 = jnp.bfloat16

D_MODEL = 1024
RET_HEADS = 4
RET_DK = 128
RET_DV = 256
LRU_BLOCKS = 4
LRU_BLOCK = 256
LRU_C = 8.0
D_FF = 3072
D_IN = 7168
ROPE_BASE = 10000.0
RMS_EPS = 1e-6
GN_EPS = 1e-6
ADAM_LR, ADAM_B1, ADAM_B2, ADAM_EPS, ADAM_WD, ADAM_STEP = 0.001, 0.9, 0.999, 1e-08, 0.01, 10

N_CHIPS = 4
N_DEV = 8
SEQ_T = 256
REF_CHUNK = 64
COL = 1024
MM_ROWS = 1024
TALL_ROWS, TALL_COLS = 2048, 512
FUSED_ROWS = 512
VMEM_LIMIT_BYTES = 56 * 1024 * 1024
MESH = pl.DeviceIdType.MESH
ROW_N1, ROW_MGB, ROW_GN, ROW_N2, ROW_NF = 0, 1, 3, 4, 5
GELU_K = math.sqrt(2.0 / math.pi)
GELU_C = 0.044715


class _Comm:
    def __init__(self, ins, outs, sems, start, finish, aliases=None):
        self.ins, self.outs, self.sems = list(ins), list(outs), list(sems)
        self.start, self.finish, self.aliases = start, finish, dict(aliases or {})


def _call(body, *, name, out_shape=(), grid=None, in_specs=(), out_specs=(), scratch=(), comm=None, prefetch=0,
          aliases=None):
    single = not isinstance(out_shape, (list, tuple))
    out_shape = [out_shape] if single else list(out_shape)
    out_specs = [out_specs] if single else list(out_specs)
    in_specs, scratch = list(in_specs), list(scratch)
    n_in, n_out, n_scr = len(in_specs), len(out_shape), len(scratch)
    kwargs = dict(name=name, compiler_params=pltpu.CompilerParams(vmem_limit_bytes=VMEM_LIMIT_BYTES))
    if prefetch:
        assert comm is None
        spec = pltpu.PrefetchScalarGridSpec(num_scalar_prefetch=prefetch, grid=grid, in_specs=in_specs,
                                            out_specs=out_specs, scratch_shapes=scratch)
        fn = pl.pallas_call(body, out_shape=out_shape, grid_spec=spec, input_output_aliases=dict(aliases or {}),
                            **kwargs)
        return (lambda *args: fn(*args)[0]) if single else fn
    if grid is not None:
        kwargs["grid"] = grid
    if comm is None:
        fn = pl.pallas_call(body, out_shape=out_shape, in_specs=in_specs, out_specs=out_specs,
                            scratch_shapes=scratch, **kwargs)
        return (lambda *args: fn(*args)[0]) if single else fn

    any_spec = pl.BlockSpec(memory_space=pl.ANY)
    n_cin, n_cout = len(comm.ins), len(comm.outs)

    def wrapped(*refs):
        ins, refs = refs[:n_in], refs[n_in:]
        cins, refs = refs[:n_cin], refs[n_cin:]
        outs, refs = refs[:n_out], refs[n_out:]
        couts, refs = refs[:n_cout], refs[n_cout:]
        scr, csems = refs[:n_scr], refs[n_scr:]
        if grid is None:
            comm.start(cins, couts, csems)
            comm.finish(cins, couts, csems)
            return
        ids = [pl.program_id(a) for a in range(len(grid))]
        first = functools.reduce(jnp.logical_and, [i == 0 for i in ids])
        last = functools.reduce(jnp.logical_and, [i == g - 1 for i, g in zip(ids, grid)])
        pl.when(first)(lambda: comm.start(cins, couts, csems))
        body(*ins, *outs, *scr)
        pl.when(last)(lambda: comm.finish(cins, couts, csems))

    fn = pl.pallas_call(
        wrapped, out_shape=out_shape + comm.outs, in_specs=in_specs + [any_spec] * n_cin,
        out_specs=out_specs + [any_spec] * n_cout, scratch_shapes=scratch + comm.sems,
        input_output_aliases={n_in + i: n_out + o for i, o in comm.aliases.items()}, **kwargs)

    def run(*args):
        res = fn(*args, *comm.ins)
        own = res[0] if single else list(res[:n_out])
        return own, list(res[n_out:])

    return run


def _dot(a, b):
    return jnp.dot(a, b, preferred_element_type=F32)


def _dot_nt(a, b):
    return lax.dot_general(a, b, (((1,), (1,)), ((), ())), preferred_element_type=F32)


def _dot_tn(a, b):
    return lax.dot_general(a, b, (((0,), (0,)), ((), ())), preferred_element_type=F32)


def _bf(x):
    return x.astype(BF16)


def _sigmoid(x):
    return 1.0 / (1.0 + jnp.exp(-x))


def _gelu(x):
    return 0.5 * x * (1.0 + jnp.tanh(GELU_K * (x + GELU_C * x * x * x)))


def _gelu_and_grad(x):
    x2 = x * x
    t = jnp.tanh(x * (GELU_K * GELU_C * x2 + GELU_K))
    hx = 0.5 * x
    g = hx + hx * t
    dg = 0.5 + 0.5 * t + hx * (1.0 - t * t) * (3.0 * GELU_K * GELU_C * x2 + GELU_K)
    return g, dg


def _rms(x):
    r = lax.rsqrt(jnp.mean(x * x, axis=-1, keepdims=True) + RMS_EPS)
    return x * r, r


def _rms_bwd(dy, x, nw):
    xh, r = _rms(x)
    g = dy * nw
    dx = r * (g - xh * jnp.mean(g * xh, axis=-1, keepdims=True))
    return dx, jnp.sum(dy * xh, axis=0, keepdims=True)


def _row_acc(ref, row, val):
    ref[row:row + 1, :] = ref[row:row + 1, :] + val


def _shift_down(x, j, prev8):
    if j == 0:
        return x
    n = x.shape[0] // 8
    row = lax.broadcasted_iota(jnp.int32, prev8.shape, 0)
    turned = [pltpu.roll(prev8, j, 0)] + [pltpu.roll(x[8 * k:8 * k + 8], j, 0) for k in range(n)]
    return jnp.concatenate([jnp.where(row < j, turned[k], turned[k + 1]) for k in range(n)], axis=0)


def _shift_up(x, j, next8):
    if j == 0:
        return x
    n = x.shape[0] // 8
    row = lax.broadcasted_iota(jnp.int32, next8.shape, 0)
    turned = [pltpu.roll(x[8 * k:8 * k + 8], 8 - j, 0) for k in range(n)] + [pltpu.roll(next8, 8 - j, 0)]
    return jnp.concatenate([jnp.where(row >= 8 - j, turned[k + 1], turned[k]) for k in range(n)], axis=0)


def _scan_fwd(a, b, carry):
    row = lax.broadcasted_iota(jnp.int32, (8, a.shape[1]), 0)
    out = []
    for k in range(a.shape[0] // 8):
        ak, bk = a[8 * k:8 * k + 8], b[8 * k:8 * k + 8]
        for s in (1, 2, 4):
            keep = row >= s
            ar, br = pltpu.roll(ak, s, 0), pltpu.roll(bk, s, 0)
            bk = jnp.where(keep, ak * br + bk, bk)
            ak = jnp.where(keep, ak * ar, ak)
        hk = ak * carry + bk
        carry = hk[7:8]
        out.append(hk)
    return jnp.concatenate(out, axis=0)


def _scan_bwd(a, b, carry):
    row = lax.broadcasted_iota(jnp.int32, (8, a.shape[1]), 0)
    out = []
    for k in reversed(range(a.shape[0] // 8)):
        ak, bk = a[8 * k:8 * k + 8], b[8 * k:8 * k + 8]
        for s in (1, 2, 4):
            keep = row < 8 - s
            ar, br = pltpu.roll(ak, 8 - s, 0), pltpu.roll(bk, 8 - s, 0)
            bk = jnp.where(keep, ak * br + bk, bk)
            ak = jnp.where(keep, ak * ar, ak)
        gk = bk + ak * carry
        carry = gk[0:1]
        out.append(gk)
    return jnp.concatenate(out[::-1], axis=0)


def _norm_matmul(x, nw, w, *, name, tm, tn, comm=None):
    m, d = x.shape
    n = w.shape[1]

    def body(x_ref, nw_ref, w_ref, o_ref, h_ref, h_sc):
        @pl.when(pl.program_id(1) == 0)
        def _():
            xh, _ = _rms(x_ref[...])
            h = _bf(xh * nw_ref[...])
            h_sc[...] = h
            h_ref[...] = h

        o_ref[...] = _dot(h_sc[...], w_ref[...])

    return _call(
        body, name=name, grid=(m // tm, n // tn), comm=comm,
        in_specs=[pl.BlockSpec((tm, d), lambda i, j: (i, 0)),
                  pl.BlockSpec((1, d), lambda i, j: (0, 0)),
                  pl.BlockSpec((d, tn), lambda i, j: (0, j))],
        out_specs=[pl.BlockSpec((tm, tn), lambda i, j: (i, j)),
                   pl.BlockSpec((tm, d), lambda i, j: (i, 0))],
        out_shape=[jax.ShapeDtypeStruct((m, n), F32), jax.ShapeDtypeStruct((m, d), BF16)],
        scratch=[pltpu.VMEM((tm, d), BF16)],
    )(x, nw, w)


def _mm_nt(a, w, *, name, tm, out_dtype):
    m, k = a.shape
    n = w.shape[0]

    def body(a_ref, w_ref, o_ref):
        o_ref[...] = _dot_nt(_bf(a_ref[...]), w_ref[...]).astype(out_dtype)

    return _call(
        body, name=name, grid=(m // tm, n // COL),
        in_specs=[pl.BlockSpec((tm, k), lambda i, j: (i, 0)),
                  pl.BlockSpec((COL, k), lambda i, j: (j, 0))],
        out_specs=pl.BlockSpec((tm, COL), lambda i, j: (i, j)),
        out_shape=jax.ShapeDtypeStruct((m, n), out_dtype),
    )(a, w)


def _piece_layout(pieces):
    offs, nblk, o = [], [], 0
    for p in pieces:
        offs.append(o)
        nblk.append(p.shape[1] // COL)
        o += p.shape[1] // COL
    return offs, nblk, o


def _mm_tn(a, pieces, *, name, tm, out_dtype=BF16):
    m, k = a.shape
    offs, nblk, nn = _piece_layout(pieces)

    def piece_spec(o, nb):
        def idx(ki, nj, mi):
            use = jnp.logical_and(nj >= o, nj < o + nb)
            return (jnp.where(use, mi, 0), jnp.clip(nj - o, 0, nb - 1))
        return pl.BlockSpec((tm, COL), idx)

    def body(a_ref, *rest):
        p_refs, o_ref, acc = rest[:len(pieces)], rest[len(pieces)], rest[len(pieces) + 1]
        nj, mi = pl.program_id(1), pl.program_id(2)

        @pl.when(mi == 0)
        def _():
            acc[...] = jnp.zeros_like(acc)

        for p_ref, o, nb in zip(p_refs, offs, nblk):
            @pl.when(jnp.logical_and(nj >= o, nj < o + nb))
            def _(p_ref=p_ref):
                acc[...] += _dot_tn(_bf(a_ref[...]), _bf(p_ref[...]))

        @pl.when(mi == pl.num_programs(2) - 1)
        def _():
            o_ref[...] = acc[...].astype(out_dtype)

    return _call(
        body, name=name, grid=(k // COL, nn, m // tm),
        in_specs=[pl.BlockSpec((tm, COL), lambda ki, nj, mi: (mi, ki))]
        + [piece_spec(o, nb) for o, nb in zip(offs, nblk)],
        out_specs=pl.BlockSpec((COL, COL), lambda ki, nj, mi: (ki, nj)),
        out_shape=jax.ShapeDtypeStruct((k, nn * COL), out_dtype),
        scratch=[pltpu.VMEM((COL, COL), F32)],
    )(a, *pieces)


def _mm_nt_normbwd(pieces, w, x, nw, dres, *, name, tm, row, comm=None):
    m, d = x.shape
    offs, nblk, nk = _piece_layout(pieces)

    def piece_spec(o, nb):
        return pl.BlockSpec((tm, COL), lambda i, k: (i, jnp.clip(k - o, 0, nb - 1)))

    def body(*refs):
        p_refs = refs[:len(pieces)]
        w_ref, x_ref, nw_ref, dres_ref, dx_ref, dnw_ref, acc = refs[len(pieces):]
        i, k = pl.program_id(0), pl.program_id(1)

        @pl.when(jnp.logical_and(i == 0, k == 0))
        def _():
            dnw_ref[...] = jnp.zeros_like(dnw_ref)

        @pl.when(k == 0)
        def _():
            acc[...] = jnp.zeros_like(acc)

        for p_ref, o, nb in zip(p_refs, offs, nblk):
            @pl.when(jnp.logical_and(k >= o, k < o + nb))
            def _(p_ref=p_ref):
                acc[...] += _dot_nt(_bf(p_ref[...]), w_ref[...])

        @pl.when(k == nk - 1)
        def _():
            dx, dnw = _rms_bwd(acc[...], x_ref[...], nw_ref[...])
            dx_ref[...] = dres_ref[...] + dx
            _row_acc(dnw_ref, row, dnw)

    return _call(
        body, name=name, grid=(m // tm, nk), comm=comm,
        in_specs=[piece_spec(o, nb) for o, nb in zip(offs, nblk)]
        + [pl.BlockSpec((d, COL), lambda i, k: (0, k)),
           pl.BlockSpec((tm, d), lambda i, k: (i, 0)),
           pl.BlockSpec((1, d), lambda i, k: (0, 0)),
           pl.BlockSpec((tm, d), lambda i, k: (i, 0))],
        out_specs=[pl.BlockSpec((tm, d), lambda i, k: (i, 0)),
                   pl.BlockSpec((8, d), lambda i, k: (0, 0))],
        out_shape=[jax.ShapeDtypeStruct((m, d), F32), jax.ShapeDtypeStruct((8, d), F32)],
        scratch=[pltpu.VMEM((tm, d), F32)],
    )(*pieces, w, x, nw, dres)


def _rope_tables(pos3, invf):
    b, s, _ = pos3.shape

    def body(pos_ref, invf_ref, cos_ref, sin_ref):
        ang = pos_ref[...].astype(F32) * invf_ref[...]
        lane = lax.broadcasted_iota(jnp.int32, ang.shape, 1)
        cos_ref[...] = jnp.cos(ang)
        sin_ref[...] = jnp.where(lane < RET_DK // 2, -1.0, 1.0) * jnp.sin(ang)

    spec = pl.BlockSpec((None, SEQ_T, RET_DK), lambda i, c: (i, c, 0))
    return _call(
        body, name="rope_tables", grid=(b, s // SEQ_T),
        in_specs=[pl.BlockSpec((None, SEQ_T, 1), lambda i, c: (i, c, 0)),
                  pl.BlockSpec((1, RET_DK), lambda i, c: (0, 0))],
        out_specs=[spec, spec],
        out_shape=[jax.ShapeDtypeStruct((b, s, RET_DK), F32)] * 2,
    )(pos3, invf)


def _log_gamma(h):
    return float(np.log1p(-np.power(np.float32(2.0), np.float32(-5.0 - h))).astype(np.float32))


def _decay_matrix(h):
    lg = _log_gamma(h)
    n = lax.broadcasted_iota(jnp.int32, (SEQ_T, SEQ_T), 0)
    m = lax.broadcasted_iota(jnp.int32, (SEQ_T, SEQ_T), 1)
    same = (n // REF_CHUNK) == (m // REF_CHUNK)
    dist = jnp.where(same, jnp.abs(n - m), n - m).astype(F32)
    return jnp.where(jnp.logical_or(same, m < n), jnp.exp(lg * dist), 0.0)


def _decay_vectors(h):
    lg = _log_gamma(h)
    idx = lax.broadcasted_iota(jnp.int32, (SEQ_T, 1), 0).astype(F32)
    qd = jnp.exp(lg * (idx + 1.0))
    kd = jnp.exp(lg * (SEQ_T - 1.0 - idx))
    return qd, kd, math.exp(lg * SEQ_T)


def _rotate(x, cos, sin):
    return x * cos + pltpu.roll(x, RET_DK // 2, 1) * sin


def _rotate_bwd(d, cos, sin):
    return d * cos + pltpu.roll(d * sin, RET_DK // 2, 1)


def _ret_head(p_ref, cos, sin, h):
    q = p_ref[:, h * RET_DK:(h + 1) * RET_DK]
    k = p_ref[:, 512 + h * RET_DK:512 + (h + 1) * RET_DK]
    v = p_ref[:, 1024 + h * RET_DV:1024 + (h + 1) * RET_DV]
    g = p_ref[:, 2048 + h * RET_DV:2048 + (h + 1) * RET_DV]
    qr = _rotate(q, cos, sin)
    kr = _rotate(k, cos, sin) * (RET_DK ** -0.5)
    return qr, kr, v, g


def _group_norm(o):
    mu = jnp.mean(o, axis=-1, keepdims=True)
    oc = o - mu
    rstd = lax.rsqrt(jnp.mean(oc * oc, axis=-1, keepdims=True) + GN_EPS)
    return oc * rstd, rstd


def _retention_fwd(proj3, cos, sin, gnw, comm=None):
    b, s, _ = proj3.shape
    nc = s // SEQ_T

    def body(p_ref, cos_ref, sin_ref, gnw_ref, a_ref, st_ref, state, wtab):
        c = pl.program_id(1)

        @pl.when(jnp.logical_and(pl.program_id(0) == 0, c == 0))
        def _():
            for h in range(RET_HEADS):
                wtab[h] = _decay_matrix(h)

        @pl.when(c == 0)
        def _():
            state[...] = jnp.zeros_like(state)

        cs, sn = cos_ref[...], sin_ref[...]
        for h in range(RET_HEADS):
            qd, kd, gt = _decay_vectors(h)
            qr, kr, v, g = _ret_head(p_ref, cs, sn, h)
            st = state[h]
            st_ref[h] = st
            p = _dot_nt(_bf(qr), _bf(kr)) * wtab[h]
            o = _dot(_bf(p), _bf(v)) + _dot(_bf(qr * qd), _bf(st))
            state[h] = st * gt + _dot_tn(_bf(kr * kd), _bf(v))
            on, _ = _group_norm(o)
            gw = gnw_ref[:, h * RET_DV:(h + 1) * RET_DV]
            a_ref[:, h * RET_DV:(h + 1) * RET_DV] = _bf(on * gw * (g * _sigmoid(g)))

    tab = pl.BlockSpec((None, SEQ_T, RET_DK), lambda i, c: (i, c, 0))
    return _call(
        body, name="retention_fwd", grid=(b, nc), comm=comm,
        in_specs=[pl.BlockSpec((None, SEQ_T, 3072), lambda i, c: (i, c, 0)), tab, tab,
                  pl.BlockSpec((1, D_MODEL), lambda i, c: (0, 0))],
        out_specs=[pl.BlockSpec((None, SEQ_T, D_MODEL), lambda i, c: (i, c, 0)),
                   pl.BlockSpec((None, None, RET_HEADS, RET_DK, RET_DV), lambda i, c: (i, c, 0, 0, 0))],
        out_shape=[jax.ShapeDtypeStruct((b, s, D_MODEL), BF16),
                   jax.ShapeDtypeStruct((b, nc, RET_HEADS, RET_DK, RET_DV), F32)],
        scratch=[pltpu.VMEM((RET_HEADS, RET_DK, RET_DV), F32),
                 pltpu.VMEM((RET_HEADS, SEQ_T, SEQ_T), F32)],
    )(proj3, cos, sin, gnw)


def _retention_bwd(proj3, cos, sin, gnw, states, da3, comm=None):
    b, s, _ = proj3.shape
    nc = s // SEQ_T

    def body(p_ref, cos_ref, sin_ref, gnw_ref, st_ref, da_ref, d_ref, dgn_ref, dstate, wtab):
        c = pl.program_id(1)

        @pl.when(jnp.logical_and(pl.program_id(0) == 0, c == 0))
        def _():
            dgn_ref[...] = jnp.zeros_like(dgn_ref)
            for h in range(RET_HEADS):
                wtab[h] = _decay_matrix(h)

        @pl.when(c == 0)
        def _():
            dstate[...] = jnp.zeros_like(dstate)

        cs, sn = cos_ref[...], sin_ref[...]
        for h in range(RET_HEADS):
            qd, kd, gt = _decay_vectors(h)
            qr, kr, v, g = _ret_head(p_ref, cs, sn, h)
            st, dst, w = st_ref[h], dstate[h], wtab[h]
            qb, kb, vb = _bf(qr), _bf(kr), _bf(v)
            p = _dot_nt(qb, kb) * w
            o = _dot(_bf(p), vb) + _dot(_bf(qr * qd), _bf(st))
            on, rstd = _group_norm(o)
            gw = gnw_ref[:, h * RET_DV:(h + 1) * RET_DV]
            da = da_ref[:, h * RET_DV:(h + 1) * RET_DV].astype(F32)
            sg = _sigmoid(g)
            silu = g * sg
            dg = da * on * gw * (sg * (1.0 + g * (1.0 - sg)))
            dgn_ref[ROW_GN:ROW_GN + 1, h * RET_DV:(h + 1) * RET_DV] += jnp.sum(da * silu * on, axis=0, keepdims=True)
            don = da * silu * gw
            do = rstd * (don - jnp.mean(don, axis=-1, keepdims=True)
                         - on * jnp.mean(don * on, axis=-1, keepdims=True))
            dob = _bf(do)
            dp = _dot_nt(dob, vb) * w
            dqr = _dot(_bf(dp), kb) + _dot_nt(dob, _bf(st)) * qd
            dkr = _dot_tn(_bf(dp), qb) + _dot_nt(vb, _bf(dst)) * kd
            dv = _dot_tn(_bf(p), dob) + _dot(_bf(kr * kd), _bf(dst))
            dstate[h] = dst * gt + _dot_tn(_bf(qr * qd), dob)
            d_ref[:, h * RET_DK:(h + 1) * RET_DK] = _bf(_rotate_bwd(dqr, cs, sn))
            d_ref[:, 512 + h * RET_DK:512 + (h + 1) * RET_DK] = _bf(_rotate_bwd(dkr, cs, sn) * (RET_DK ** -0.5))
            d_ref[:, 1024 + h * RET_DV:1024 + (h + 1) * RET_DV] = _bf(dv)
            d_ref[:, 2048 + h * RET_DV:2048 + (h + 1) * RET_DV] = _bf(dg)

    rev = lambda i, c: (i, nc - 1 - c, 0)
    tab = pl.BlockSpec((None, SEQ_T, RET_DK), rev)
    return _call(
        body, name="retention_bwd", grid=(b, nc), comm=comm,
        in_specs=[pl.BlockSpec((None, SEQ_T, 3072), rev), tab, tab,
                  pl.BlockSpec((1, D_MODEL), lambda i, c: (0, 0)),
                  pl.BlockSpec((None, None, RET_HEADS, RET_DK, RET_DV), lambda i, c: (i, nc - 1 - c, 0, 0, 0)),
                  pl.BlockSpec((None, SEQ_T, D_MODEL), rev)],
        out_specs=[pl.BlockSpec((None, SEQ_T, 3072), rev),
                   pl.BlockSpec((8, D_MODEL), lambda i, c: (0, 0))],
        out_shape=[jax.ShapeDtypeStruct((b, s, 3072), BF16), jax.ShapeDtypeStruct((8, D_MODEL), F32)],
        scratch=[pltpu.VMEM((RET_HEADS, RET_DK, RET_DV), F32),
                 pltpu.VMEM((RET_HEADS, SEQ_T, SEQ_T), F32)],
    )(proj3, cos, sin, gnw, states, da3)


def _softplus_neg(lam):
    z = -lam
    u = jnp.exp(-jnp.abs(z))
    log1p_u = jnp.where(u < 0.01, u * (1.0 - u * (0.5 - u * (1.0 / 3.0))), jnp.log(1.0 + u))
    return jnp.maximum(z, 0.0) + log1p_u


def _lru_coeffs(xc, wr_ref, br_ref, wi_ref, bi_ref, lam_ref):
    rs, is_ = [], []
    for n in range(LRU_BLOCKS):
        xb = _bf(xc[:, n * LRU_BLOCK:(n + 1) * LRU_BLOCK])
        cols = slice(n * LRU_BLOCK, (n + 1) * LRU_BLOCK)
        rs.append(_sigmoid(_dot(xb, wr_ref[n]) + br_ref[:, cols]))
        is_.append(_sigmoid(_dot(xb, wi_ref[n]) + bi_ref[:, cols]))
    r = jnp.concatenate(rs, axis=1)
    i = jnp.concatenate(is_, axis=1)
    sp = _softplus_neg(lam_ref[...])
    la = -LRU_C * r * sp
    a = jnp.exp(la)
    s = jnp.sqrt(-jnp.tanh(la) * (a * a + 1.0))
    return r, i, a, s, sp


_LRU_PARAM_SPECS = [
    pl.BlockSpec((4, D_MODEL), lambda i, c: (0, 0)),
    pl.BlockSpec((1, D_MODEL), lambda i, c: (0, 0)),
    pl.BlockSpec((LRU_BLOCKS, LRU_BLOCK, LRU_BLOCK), lambda i, c: (0, 0, 0)),
    pl.BlockSpec((1, D_MODEL), lambda i, c: (0, 0)),
    pl.BlockSpec((LRU_BLOCKS, LRU_BLOCK, LRU_BLOCK), lambda i, c: (0, 0, 0)),
    pl.BlockSpec((1, D_MODEL), lambda i, c: (0, 0)),
    pl.BlockSpec((1, D_MODEL), lambda i, c: (0, 0)),
]


def _lru_fwd(proj3, params):
    b, s, _ = proj3.shape
    nc = s // SEQ_T

    def body(x_ref, y_ref, cw, cb, wr, br, wi, bi, lam, o_ref, h_ref, xc_ref, gy_ref, hdg_ref, xprev, hprev):
        @pl.when(pl.program_id(1) == 0)
        def _():
            xprev[...] = jnp.zeros_like(xprev)
            hprev[...] = jnp.zeros_like(hprev)

        x = x_ref[...]
        prev8 = xprev[...]
        xc = cb[...] + sum(cw[j:j + 1, :] * _shift_down(x, 3 - j, prev8) for j in range(4))
        xprev[...] = x[SEQ_T - 8:]
        xc_ref[...] = xc
        _, i, a, s_, _ = _lru_coeffs(xc, wr, br, wi, bi, lam)
        h = _scan_fwd(a, s_ * (i * xc), hprev[7:8, :])
        hprev[...] = h[SEQ_T - 8:]
        h_ref[...] = h
        gy, dgy = _gelu_and_grad(y_ref[...])
        o_ref[...] = _bf(h * gy)
        gy_ref[...] = _bf(gy)
        hdg_ref[...] = _bf(h * dgy)

    out = pl.BlockSpec((None, SEQ_T, D_MODEL), lambda i, c: (i, c, 0))
    half, full = jax.ShapeDtypeStruct((b, s, D_MODEL), BF16), jax.ShapeDtypeStruct((b, s, D_MODEL), F32)
    return _call(
        body, name="lru_fwd", grid=(b, nc),
        in_specs=[pl.BlockSpec((None, SEQ_T, D_MODEL), lambda i, c: (i, c, 3)),
                  pl.BlockSpec((None, SEQ_T, D_MODEL), lambda i, c: (i, c, 4))] + _LRU_PARAM_SPECS,
        out_specs=[out] * 5, out_shape=[half, full, full, half, half],
        scratch=[pltpu.VMEM((8, D_MODEL), F32), pltpu.VMEM((8, D_MODEL), F32)],
    )(proj3, proj3, *params)


def _lru_bwd(proj3, params, kept, db3, comm=None):
    b, s, _ = proj3.shape
    nc = s // SEQ_T
    blk8 = SEQ_T // 8
    hseq, xcseq, gyseq, hdgseq = kept

    def body(x_ref, xc_ref, h_ref, hp_ref, gy_ref, hdg_ref, db_ref, cw, cb, wr, br, wi, bi, lam,
             d_ref, dwr_ref, dwi_ref, sm_ref, gnext, anext, dxcnext):
        c = pl.program_id(1)
        first_chunk = c == nc - 1

        @pl.when(jnp.logical_and(pl.program_id(0) == 0, c == 0))
        def _():
            dwr_ref[...] = jnp.zeros_like(dwr_ref)
            dwi_ref[...] = jnp.zeros_like(dwi_ref)
            sm_ref[...] = jnp.zeros_like(sm_ref)

        @pl.when(c == 0)
        def _():
            gnext[...] = jnp.zeros_like(gnext)
            anext[...] = jnp.zeros_like(anext)
            dxcnext[...] = jnp.zeros_like(dxcnext)

        x, xc, h = x_ref[...], xc_ref[...], h_ref[...]
        hprev = hp_ref[...] * jnp.where(first_chunk, 0.0, 1.0)
        r, i, a, s_, sp = _lru_coeffs(xc, wr, br, wi, bi, lam)
        db = db_ref[...].astype(F32)
        dy = db * hdg_ref[...].astype(F32)
        a_up = _shift_up(a, 1, anext[...])
        g = _scan_bwd(a_up, db * gy_ref[...].astype(F32), gnext[0:1, :])
        gnext[...] = g[0:8]
        anext[...] = a[0:8]
        da = g * _shift_down(h, 1, hprev)
        ixc = i * xc
        dla = da * a - (g * ixc) * (a * a) / s_
        di = g * s_ * xc
        dxc = g * s_ * i
        dzr = dla * (-LRU_C * sp) * r * (1.0 - r)
        dzi = di * i * (1.0 - i)
        lam_v = lam[...]
        _row_acc(sm_ref, 7, jnp.sum(dla * (LRU_C * r), axis=0, keepdims=True) * _sigmoid(-lam_v))
        _row_acc(sm_ref, 5, jnp.sum(dzr, axis=0, keepdims=True))
        _row_acc(sm_ref, 6, jnp.sum(dzi, axis=0, keepdims=True))
        parts = []
        for n in range(LRU_BLOCKS):
            cols = slice(n * LRU_BLOCK, (n + 1) * LRU_BLOCK)
            xb, zr, zi = _bf(xc[:, cols]), _bf(dzr[:, cols]), _bf(dzi[:, cols])
            parts.append(dxc[:, cols] + _dot_nt(zr, wr[n]) + _dot_nt(zi, wi[n]))
            dwr_ref[n] += _dot_tn(xb, zr)
            dwi_ref[n] += _dot_tn(xb, zi)
        dxc = jnp.concatenate(parts, axis=1)
        _row_acc(sm_ref, 4, jnp.sum(dxc, axis=0, keepdims=True))
        nxt = dxcnext[...]
        dx = jnp.zeros_like(x)
        for j in range(4):
            ahead = _shift_up(dxc, 3 - j, nxt)
            dx = dx + cw[j:j + 1, :] * ahead
            _row_acc(sm_ref, j, jnp.sum(ahead * x, axis=0, keepdims=True))
        dxcnext[...] = dxc[0:8]
        d_ref[:, 0:D_MODEL] = _bf(dx)
        d_ref[:, D_MODEL:2 * D_MODEL] = _bf(dy)

    rev = lambda col: (lambda i, c: (i, nc - 1 - c, col))
    prev = lambda col: (lambda i, c: (i, jnp.maximum((nc - 1 - c) * blk8 - 1, 0), col))
    return _call(
        body, name="lru_bwd", grid=(b, nc), comm=comm,
        in_specs=[pl.BlockSpec((None, SEQ_T, D_MODEL), rev(3)),
                  pl.BlockSpec((None, SEQ_T, D_MODEL), rev(0)),
                  pl.BlockSpec((None, SEQ_T, D_MODEL), rev(0)),
                  pl.BlockSpec((None, 8, D_MODEL), prev(0)),
                  pl.BlockSpec((None, SEQ_T, D_MODEL), rev(0)),
                  pl.BlockSpec((None, SEQ_T, D_MODEL), rev(0)),
                  pl.BlockSpec((None, SEQ_T, D_MODEL), rev(0))] + _LRU_PARAM_SPECS,
        out_specs=[pl.BlockSpec((None, SEQ_T, 2 * D_MODEL), rev(0)),
                   pl.BlockSpec((LRU_BLOCKS, LRU_BLOCK, LRU_BLOCK), lambda i, c: (0, 0, 0)),
                   pl.BlockSpec((LRU_BLOCKS, LRU_BLOCK, LRU_BLOCK), lambda i, c: (0, 0, 0)),
                   pl.BlockSpec((8, D_MODEL), lambda i, c: (0, 0))],
        out_shape=[jax.ShapeDtypeStruct((b, s, 2 * D_MODEL), BF16),
                   jax.ShapeDtypeStruct((LRU_BLOCKS, LRU_BLOCK, LRU_BLOCK), F32),
                   jax.ShapeDtypeStruct((LRU_BLOCKS, LRU_BLOCK, LRU_BLOCK), F32),
                   jax.ShapeDtypeStruct((8, D_MODEL), F32)],
        scratch=[pltpu.VMEM((8, D_MODEL), F32)] * 3,
    )(proj3, xcseq, hseq, hseq, gyseq, hdgseq, db3, *params)


def _merge_parts(a_ref, b_ref, gr_ref, gl_ref, mgb_ref, wro_ref, wlo_ref):
    ya = _dot(a_ref[...], wro_ref[...])
    yb = _dot(b_ref[...], wlo_ref[...])
    sa = _sigmoid(gr_ref[...] + mgb_ref[0:1, :])
    sb = _sigmoid(gl_ref[...] + mgb_ref[1:2, :])
    return ya, yb, sa, sb


def _merge_specs(tm):
    row = lambda col: pl.BlockSpec((tm, D_MODEL), lambda i: (i, col))
    full = pl.BlockSpec((D_MODEL, D_MODEL), lambda i: (0, 0))
    return row, full


def _merge_fwd(a_in, b_in, proj, mgb, wro, wlo, wout, x, *, tm):
    m = x.shape[0]
    row, full = _merge_specs(tm)

    def body(a_ref, b_ref, gr_ref, gl_ref, mgb_ref, wro_ref, wlo_ref, wout_ref, x_ref, o_ref):
        ya, yb, sa, sb = _merge_parts(a_ref, b_ref, gr_ref, gl_ref, mgb_ref, wro_ref, wlo_ref)
        o_ref[...] = x_ref[...] + _dot(_bf(sa * ya + sb * yb), wout_ref[...])

    return _call(
        body, name="merge_fwd", grid=(m // tm,),
        in_specs=[row(0), row(0), row(5), row(6), pl.BlockSpec((2, D_MODEL), lambda i: (0, 0)),
                  full, full, full, row(0)],
        out_specs=row(0),
        out_shape=jax.ShapeDtypeStruct((m, D_MODEL), F32),
    )(a_in, b_in, proj, proj, mgb, wro, wlo, wout, x)


def _merge_bwd(a_in, b_in, proj, mgb, wro, wlo, wout, dx2, *, tm, comm=None):
    m = dx2.shape[0]
    row, full = _merge_specs(tm)

    def body(a_ref, b_ref, gr_ref, gl_ref, mgb_ref, wro_ref, wlo_ref, wout_ref, dx_ref,
             mix_ref, dya_ref, dyb_ref, da_ref, db_ref, dg_ref, sm_ref):
        @pl.when(pl.program_id(0) == 0)
        def _():
            sm_ref[...] = jnp.zeros_like(sm_ref)

        ya, yb, sa, sb = _merge_parts(a_ref, b_ref, gr_ref, gl_ref, mgb_ref, wro_ref, wlo_ref)
        mix_ref[...] = _bf(sa * ya + sb * yb)
        dmix = _dot_nt(_bf(dx_ref[...]), wout_ref[...])
        dya, dyb = _bf(dmix * sa), _bf(dmix * sb)
        dya_ref[...] = dya
        dyb_ref[...] = dyb
        dga = dmix * ya * sa * (1.0 - sa)
        dgb = dmix * yb * sb * (1.0 - sb)
        dg_ref[:, 0:D_MODEL] = _bf(dga)
        dg_ref[:, D_MODEL:2 * D_MODEL] = _bf(dgb)
        _row_acc(sm_ref, ROW_MGB, jnp.sum(dga, axis=0, keepdims=True))
        _row_acc(sm_ref, ROW_MGB + 1, jnp.sum(dgb, axis=0, keepdims=True))
        da_ref[...] = _bf(_dot_nt(dya, wro_ref[...]))
        db_ref[...] = _bf(_dot_nt(dyb, wlo_ref[...]))

    act = jax.ShapeDtypeStruct((m, D_MODEL), BF16)
    return _call(
        body, name="merge_bwd", grid=(m // tm,), comm=comm,
        in_specs=[row(0), row(0), row(5), row(6), pl.BlockSpec((2, D_MODEL), lambda i: (0, 0)),
                  full, full, full, row(0)],
        out_specs=[row(0)] * 5 + [pl.BlockSpec((tm, 2 * D_MODEL), lambda i: (i, 0)),
                                  pl.BlockSpec((8, D_MODEL), lambda i: (0, 0))],
        out_shape=[act] * 5 + [jax.ShapeDtypeStruct((m, 2 * D_MODEL), BF16),
                               jax.ShapeDtypeStruct((8, D_MODEL), F32)],
    )(a_in, b_in, proj, proj, mgb, wro, wlo, wout, dx2)


def _ffn_act_fwd(up3, cw, cb):
    b, s, _ = up3.shape

    def body(g_ref, v_ref, cw_ref, cb_ref, o_ref, act_ref, q_ref, gprev):
        @pl.when(pl.program_id(1) == 0)
        def _():
            gprev[...] = jnp.zeros_like(gprev)

        gate, val = g_ref[...], v_ref[...]
        prev8 = gprev[...]
        gc = cb_ref[...] + sum(cw_ref[j:j + 1, :] * _shift_down(gate, 2 - j, prev8) for j in range(3))
        gprev[...] = gate[SEQ_T - 8:]
        act, dact = _gelu_and_grad(gc)
        o_ref[...] = _bf(act * val)
        act_ref[...] = _bf(act)
        q_ref[...] = _bf(dact * val)

    out = pl.BlockSpec((None, SEQ_T, D_FF), lambda i, c: (i, c, 0))
    return _call(
        body, name="ffn_act_fwd", grid=(b, s // SEQ_T),
        in_specs=[pl.BlockSpec((None, SEQ_T, D_FF), lambda i, c: (i, c, 0)),
                  pl.BlockSpec((None, SEQ_T, D_FF), lambda i, c: (i, c, 1)),
                  pl.BlockSpec((3, D_FF), lambda i, c: (0, 0)),
                  pl.BlockSpec((1, D_FF), lambda i, c: (0, 0))],
        out_specs=[out] * 3,
        out_shape=[jax.ShapeDtypeStruct((b, s, D_FF), BF16)] * 3,
        scratch=[pltpu.VMEM((8, D_FF), F32)],
    )(up3, up3, cw, cb)


def _ffn_act_bwd(up3, act3, q3, cw, df3, comm=None):
    b, s, _ = up3.shape
    nc = s // SEQ_T

    def body(g_ref, act_ref, q_ref, df_ref, cw_ref, dg_ref, dv_ref, sm_ref, dgcnext):
        c = pl.program_id(1)

        @pl.when(jnp.logical_and(pl.program_id(0) == 0, c == 0))
        def _():
            sm_ref[...] = jnp.zeros_like(sm_ref)

        @pl.when(c == 0)
        def _():
            dgcnext[...] = jnp.zeros_like(dgcnext)

        gate = g_ref[...]
        df = df_ref[...].astype(F32)
        dv_ref[...] = _bf(df * act_ref[...].astype(F32))
        dgc = df * q_ref[...].astype(F32)
        nxt = dgcnext[...]
        dgate = jnp.zeros_like(gate)
        for j in range(3):
            ahead = _shift_up(dgc, 2 - j, nxt)
            dgate = dgate + cw_ref[j:j + 1, :] * ahead
            _row_acc(sm_ref, j, jnp.sum(ahead * gate, axis=0, keepdims=True))
        _row_acc(sm_ref, 3, jnp.sum(dgc, axis=0, keepdims=True))
        dgcnext[...] = dgc[0:8]
        dg_ref[...] = _bf(dgate)

    rev = pl.BlockSpec((None, SEQ_T, D_FF), lambda i, c: (i, nc - 1 - c, 0))
    return _call(
        body, name="ffn_act_bwd", grid=(b, nc), comm=comm,
        in_specs=[rev, rev, rev, rev, pl.BlockSpec((3, D_FF), lambda i, c: (0, 0))],
        out_specs=[rev, rev, pl.BlockSpec((8, D_FF), lambda i, c: (0, 0))],
        out_shape=[jax.ShapeDtypeStruct((b, s, D_FF), BF16)] * 2 + [jax.ShapeDtypeStruct((8, D_FF), F32)],
        scratch=[pltpu.VMEM((8, D_FF), F32)],
    )(up3, act3, q3, df3, cw)


def _ffn_down_loss(f, wd, x2, nfw, target, *, tm):
    m, kf = f.shape
    nt = m // tm

    def body(f_ref, wd_ref, x_ref, nw_ref, t_ref, loss_ref, dx_ref, dnw_ref, lsum):
        i = pl.program_id(0)

        @pl.when(i == 0)
        def _():
            dnw_ref[...] = jnp.zeros_like(dnw_ref)
            lsum[...] = jnp.zeros_like(lsum)

        x3 = x_ref[...] + _dot(f_ref[...], wd_ref[...])
        nw = nw_ref[...]
        xh, _ = _rms(x3)
        err = xh * nw - t_ref[...]
        lsum[...] += jnp.sum(err * err, axis=0, keepdims=True)
        dx, dnw = _rms_bwd(err * (1.0 / D_MODEL), x3, nw)
        dx_ref[...] = dx
        _row_acc(dnw_ref, ROW_NF, dnw)

        @pl.when(i == nt - 1)
        def _():
            loss_ref[...] = jnp.sum(lsum[...], axis=1, keepdims=True) * (0.5 / D_MODEL)

    row = pl.BlockSpec((tm, D_MODEL), lambda i: (i, 0))
    return _call(
        body, name="ffn_down_loss", grid=(nt,),
        in_specs=[pl.BlockSpec((tm, kf), lambda i: (i, 0)),
                  pl.BlockSpec((kf, D_MODEL), lambda i: (0, 0)),
                  row, pl.BlockSpec((1, D_MODEL), lambda i: (0, 0)), row],
        out_specs=[pl.BlockSpec((1, 1), lambda i: (0, 0)), row,
                   pl.BlockSpec((8, D_MODEL), lambda i: (0, 0))],
        out_shape=[jax.ShapeDtypeStruct((1, 1), F32), jax.ShapeDtypeStruct((m, D_MODEL), F32),
                   jax.ShapeDtypeStruct((8, D_MODEL), F32)],
        scratch=[pltpu.VMEM((1, D_MODEL), F32)],
    )(f, wd, x2, nfw, target)


def _row_tile(rows):
    return next((t for t in (256, 128, 64, 32, 16, 8) if rows % t == 0), rows)


def _adamw(w, gs, m, v, *, name):
    rows, cols = w.shape
    tr = _row_tile(rows)
    ng = len(gs)

    def body(w_ref, *rest):
        g_refs, (m_ref, v_ref, g_out, d_out, m_out, v_out) = rest[:ng], rest[ng:]
        g = g_refs[0][...]
        for r in g_refs[1:]:
            g = g + r[...]
        mn = ADAM_B1 * m_ref[...] + (1.0 - ADAM_B1) * g
        vn = ADAM_B2 * v_ref[...] + (1.0 - ADAM_B2) * (g * g)
        m_hat = mn / (1.0 - ADAM_B1 ** ADAM_STEP)
        v_hat = vn / (1.0 - ADAM_B2 ** ADAM_STEP)
        g_out[...] = g
        d_out[...] = -ADAM_LR * (m_hat / (jnp.sqrt(v_hat) + ADAM_EPS) + ADAM_WD * w_ref[...])
        m_out[...] = mn
        v_out[...] = vn

    spec = pl.BlockSpec((tr, cols), lambda i: (i, 0))
    return _call(
        body, name=name, grid=(rows // tr,),
        in_specs=[spec] * (3 + ng), out_specs=[spec] * 4,
        out_shape=[jax.ShapeDtypeStruct((rows, cols), F32)] * 4,
    )(w, *gs, m, v)


def _mesh_pos():
    x, y, c = lax.axis_index("x"), lax.axis_index("y"), lax.axis_index("c")
    return x, y, c


def _other_chips(x, y, c):
    return [((1 - x, y, c), 2 * (1 - x) + y), ((x, 1 - y, c), 2 * x + 1 - y),
            ((1 - x, 1 - y, c), 2 * (1 - x) + 1 - y)]


def _region(ref, axis, size, half_axis, chip, core=None):
    idx = [slice(None)] * len(ref.shape)
    if core is None:
        idx[axis] = pl.ds(pl.multiple_of(chip * size, size), size)
    elif half_axis == axis:
        h = size // 2
        idx[axis] = pl.ds(pl.multiple_of(chip * size + core * h, h), h)
    else:
        idx[axis] = pl.ds(pl.multiple_of(chip * size, size), size)
        h = ref.shape[half_axis] // 2
        idx[half_axis] = pl.ds(pl.multiple_of(core * h, h), h)
    return ref.at[tuple(idx)]


def _half(ref, half_axis, core):
    idx = [slice(None)] * len(ref.shape)
    h = ref.shape[half_axis] // 2
    idx[half_axis] = pl.ds(pl.multiple_of(core * h, h), h)
    return ref.at[tuple(idx)]


class _Copy:
    def __init__(self, make):
        self._make = make

    def start(self):
        self._make().start()

    def wait(self):
        self._make().wait()

    def wait_send(self):
        self._make().wait_send()

    def wait_recv(self):
        self._make().wait_recv()


def _remote(src, dst, send_sem, recv_sem, dev):
    return _Copy(lambda: pltpu.make_async_remote_copy(
        src_ref=src, dst_ref=dst, send_sem=send_sem, recv_sem=recv_sem, device_id=dev, device_id_type=MESH))


def _local(src, dst, sem):
    return _Copy(lambda: pltpu.make_async_copy(src, dst, sem))


def _dma_sems(n):
    return pltpu.SemaphoreType.DMA((n,))


def _place_shard(w, chip, axis, *, name):
    shape = list(w.shape)
    shape[axis] *= N_CHIPS
    if w.ndim == 3:
        block, grid = (1,) + w.shape[1:], (w.shape[0],)
        in_map, out_map = (lambda i, chip: (i, 0, 0)), (lambda i, chip: (i, chip[0], 0))
    else:
        tr = _row_tile(w.shape[0])
        nt = w.shape[0] // tr
        block, grid = (tr, w.shape[1]), (nt,)
        in_map = lambda i, chip: (i, 0)
        out_map = (lambda i, chip: (chip[0] * nt + i, 0)) if axis == 0 else (lambda i, chip: (i, chip[0]))

    def body(chip_ref, w_ref, o_ref):
        o_ref[...] = _bf(w_ref[...])

    return _call(body, name=name, grid=grid, prefetch=1, in_specs=[pl.BlockSpec(block, in_map)],
                 out_specs=pl.BlockSpec(block, out_map),
                 out_shape=jax.ShapeDtypeStruct(tuple(shape), BF16))(chip, w)


def _ici_leg(srcs, dsts, layout, sizes, n_whole, sems):
    send_sems, recv_sems, local_sems = sems
    x, y, c = _mesh_pos()
    mine = 2 * x + y
    n_big = len(srcs) - n_whole
    local, sends, recvs = [], [], []
    for t, (src, dst) in enumerate(zip(srcs, dsts)):
        if t < n_big:
            ax, hx = layout[t]
            part = _region(src, ax, sizes[t], hx, mine, c)
            landing = lambda chip, dst=dst, ax=ax, hx=hx, size=sizes[t]: _region(dst, ax, size, hx, chip, c)
        else:
            part, landing = src, (lambda chip, dst=dst: dst.at[chip])
            local.append(_local(src, dst.at[mine], local_sems.at[t - n_big]))
        for k, (dev, chip) in enumerate(_other_chips(x, y, c)):
            sends.append(_remote(part, landing(mine), send_sems.at[3 * t + k], recv_sems.at[3 * t + k], dev))
            recvs.append(_remote(part, landing(chip), send_sems.at[3 * t + k], recv_sems.at[3 * t + k], dev))
    return local, sends, recvs


def _d2d_leg(srcs, dsts, layout, sizes, sems):
    send_sems, recv_sems = sems
    x, y, c = _mesh_pos()
    sends, recvs = [], []
    for t, (src, dst) in enumerate(zip(srcs, dsts)):
        ax, hx = layout[t]
        for k, (_, chip) in enumerate(_other_chips(x, y, c)):
            sem = (send_sems.at[3 * t + k], recv_sems.at[3 * t + k])
            sends.append(_remote(_region(src, ax, sizes[t], hx, chip, c),
                                 _region(dst, ax, sizes[t], hx, chip, c), *sem, (x, y, 1 - c)))
            recvs.append(_remote(_region(src, ax, sizes[t], hx, chip, 1 - c),
                                 _region(dst, ax, sizes[t], hx, chip, 1 - c), *sem, (x, y, 1 - c)))
    return sends, recvs


def _gather_shapes(bufs, whole):
    return ([jax.ShapeDtypeStruct(b.shape, b.dtype) for b in bufs]
            + [jax.ShapeDtypeStruct((N_CHIPS,) + w.shape, w.dtype) for w in whole])


def _gather_d2d(bufs, layout, sizes):
    n = len(bufs)

    def start(ins, outs, sems):
        for cp in _d2d_leg(ins, outs, layout, sizes, sems)[0]:
            cp.start()

    def finish(ins, outs, sems):
        sends, recvs = _d2d_leg(ins, outs, layout, sizes, sems)
        for cp in recvs:
            cp.wait_recv()
        for cp in sends:
            cp.wait_send()

    return _Comm(bufs, [jax.ShapeDtypeStruct(b.shape, b.dtype) for b in bufs],
                 [_dma_sems(3 * n), _dma_sems(3 * n)], start, finish, aliases={i: i for i in range(n)})


def _norm_bf16(x, nw, *, name, tm):
    m, d = x.shape

    def body(x_ref, nw_ref, h_ref):
        h_ref[...] = _bf(_rms(x_ref[...])[0] * nw_ref[...])

    row = pl.BlockSpec((tm, d), lambda i: (i, 0))
    return _call(body, name=name, grid=(m // tm,), in_specs=[row, pl.BlockSpec((1, d), lambda i: (0, 0))],
                 out_specs=row, out_shape=jax.ShapeDtypeStruct((m, d), BF16))(x, nw)


def _in_proj_gather(h1, w_buf, later, later_cut, small, order, *, tm):
    m, d = h1.shape
    width = w_buf.shape[1] // N_CHIPS
    nr, nl = m // tm, len(later)
    sizes = [b.shape[ax] // N_CHIPS for b, (ax, _) in zip(later, later_cut)]

    def body(order_ref, h_ref, w_in, *rest):
        later_in, small_in = rest[:nl], rest[nl]
        o_ref, w_out = rest[nl + 1], rest[nl + 2]
        later_out, small_out = rest[nl + 3:2 * nl + 3], rest[2 * nl + 3]
        wv, load_sem, ici_send, ici_recv, d2d_send, d2d_recv, l_send, l_recv, l_local = rest[2 * nl + 4:]
        s, i = pl.program_id(0), pl.program_id(1)
        x, y, c = _mesh_pos()
        mine = 2 * x + y
        peers = _other_chips(x, y, c)
        part = lambda ref, chip, core=None: _region(ref, 1, width, 0, chip, core)

        def ici(k):
            dev, chip = peers[k]
            sem = (ici_send.at[k], ici_recv.at[k])
            return (_remote(part(w_in, mine, c), part(w_out, mine, c), *sem, dev),
                    _remote(part(w_in, chip, c), part(w_out, chip, c), *sem, dev))

        def d2d(k):
            chip, sem, sib = peers[k][1], (d2d_send.at[k], d2d_recv.at[k]), (x, y, 1 - c)
            return (_remote(part(w_out, chip, c), part(w_out, chip, c), *sem, sib),
                    _remote(part(w_out, chip, 1 - c), part(w_out, chip, 1 - c), *sem, sib))

        def load(src, chip, slot):
            cp = _local(part(src, chip), wv.at[slot], load_sem.at[slot])
            cp.start()
            cp.wait()

        def others():
            return _ici_leg(list(later_in) + [small_in], list(later_out) + [small_out], later_cut, sizes, 1,
                            (l_send, l_recv, l_local))

        @pl.when(jnp.logical_and(s == 0, i == 0))
        def _():
            for k in range(3):
                ici(k)[0].start()
            local, sends, _ = others()
            for cp in local + sends:
                cp.start()
            load(w_in, mine, 0)

        o_ref[...] = _dot(h_ref[...], wv[s % 2])

        @pl.when(i == nr - 1)
        def _():
            @pl.when(s == 0)
            def _():
                for k in (0, 1):
                    ici(k)[1].wait_recv()
                    d2d(k)[0].start()

            @pl.when(s == 1)
            def _():
                ici(2)[1].wait_recv()
                d2d(2)[0].start()

            for k in range(3):
                @pl.when(s == k)
                def _(k=k):
                    d2d(k)[1].wait_recv()
                    load(w_out, peers[k][1], (k + 1) % 2)

            @pl.when(s == 3)
            def _():
                for k in range(3):
                    ici(k)[0].wait_send()
                    d2d(k)[0].wait_send()
                local, sends, recvs = others()
                for cp in recvs:
                    cp.wait_recv()
                for cp in sends:
                    cp.wait_send()
                for cp in local:
                    cp.wait()

    any_spec = pl.BlockSpec(memory_space=pl.ANY)
    n_any = nl + 2
    outs = _call(
        body, name="in_proj", grid=(N_CHIPS, nr), prefetch=1,
        in_specs=[pl.BlockSpec((tm, d), lambda s, i, order: (i, 0))] + [any_spec] * n_any,
        out_specs=[pl.BlockSpec((tm, width), lambda s, i, order: (i, order[s]))] + [any_spec] * n_any,
        out_shape=[jax.ShapeDtypeStruct((m, w_buf.shape[1]), F32)] + _gather_shapes([w_buf] + list(later), [small]),
        scratch=[pltpu.VMEM((2, d, width), BF16), _dma_sems(2), _dma_sems(3), _dma_sems(3), _dma_sems(3),
                 _dma_sems(3), _dma_sems(3 * (nl + 1)), _dma_sems(3 * (nl + 1)), _dma_sems(1)],
        aliases={2 + t: 1 + t for t in range(nl + 1)},
    )(order, h1, w_buf, *later, small)
    return outs[0], outs[1], list(outs[2:2 + nl]), outs[2 + nl]


def _exchange(grads, layout):
    n = len(grads)
    others = N_DEV - 1
    sizes = [g.shape[ax] // N_CHIPS for g, (ax, _) in zip(grads, layout)]
    out_shapes = []
    for g, (ax, hx), sz in zip(grads, layout, sizes):
        shape = list(g.shape)
        shape[ax] = sz
        shape[hx] //= 2
        out_shapes.append(jax.ShapeDtypeStruct((others,) + tuple(shape), g.dtype))

    def copies(ins, outs, sems):
        send_sems, recv_sems = sems
        x, y, c = _mesh_pos()
        sends, recvs = [], []
        for t, (src, dst) in enumerate(zip(ins, outs)):
            ax, hx = layout[t]
            for r in range(1, N_DEV):
                px = (1 - x) if r & 4 else x
                py = (1 - y) if r & 2 else y
                pc = (1 - c) if r & 1 else c
                sem = (send_sems.at[others * t + r - 1], recv_sems.at[others * t + r - 1])
                part = _region(src, ax, sizes[t], hx, 2 * px + py, pc)
                sends.append(_remote(part, dst.at[r - 1], *sem, (px, py, pc)))
                recvs.append(_remote(part, dst.at[r - 1], *sem, (px, py, pc)))
        return sends, recvs

    def start(ins, outs, sems):
        for cp in copies(ins, outs, sems)[0]:
            cp.start()

    def finish(ins, outs, sems):
        sends, recvs = copies(ins, outs, sems)
        for cp in recvs:
            cp.wait_recv()
        for cp in sends:
            cp.wait_send()

    return _Comm(grads, out_shapes, [_dma_sems(others * n), _dma_sems(others * n)], start, finish)


def _reduce_half(g, parts, pos, cut, *, name):
    ax, _ = cut
    others = parts.shape[0]
    if g.ndim == 3:
        nb, rows, cols = g.shape
        hb = nb // 2
        block, grid, out_shape = (1, rows // N_CHIPS, cols), (hb,), (nb, rows // N_CHIPS, cols)
        g_map = lambda i, pos: (pos[1] * hb + i, pos[0], 0)
        o_map = lambda i, pos: (pos[1] * hb + i, 0, 0)
        p_map = lambda i, pos: (0, i, 0, 0)
    elif ax == 1:
        rows, cols = g.shape
        tr = _row_tile(rows // 2)
        nt = rows // 2 // tr
        block, grid, out_shape = (tr, cols // N_CHIPS), (nt,), (rows, cols // N_CHIPS)
        g_map = lambda i, pos: (pos[1] * nt + i, pos[0])
        o_map = lambda i, pos: (pos[1] * nt + i, 0)
        p_map = lambda i, pos: (0, i, 0)
    else:
        rows, cols = g.shape
        tr = _row_tile(rows // N_CHIPS // 2)
        nt = rows // N_CHIPS // 2 // tr
        block, grid, out_shape = (tr, cols), (nt,), (rows // N_CHIPS, cols)
        g_map = lambda i, pos: (pos[0] * 2 * nt + pos[1] * nt + i, 0)
        o_map = lambda i, pos: (pos[1] * nt + i, 0)
        p_map = lambda i, pos: (0, i, 0)

    def body(pos_ref, g_ref, p_ref, o_ref):
        acc = g_ref[...].astype(F32)
        for r in range(others):
            acc = acc + p_ref[r].astype(F32)
        o_ref[...] = acc

    return _call(
        body, name=name, grid=grid, prefetch=1,
        in_specs=[pl.BlockSpec(block, g_map), pl.BlockSpec((others,) + block, p_map)],
        out_specs=pl.BlockSpec(block, o_map), out_shape=jax.ShapeDtypeStruct(out_shape, F32),
    )(pos, g, parts)


def _join_halves(bufs):
    n = len(bufs)

    def copies(ins, outs, sems):
        send_sems, recv_sems = sems
        x, y, c = _mesh_pos()
        sends = [_remote(_half(src, 0, c), _half(dst, 0, c), send_sems.at[t], recv_sems.at[t], (x, y, 1 - c))
                 for t, (src, dst) in enumerate(zip(ins, outs))]
        recvs = [_remote(_half(src, 0, 1 - c), _half(dst, 0, 1 - c), send_sems.at[t], recv_sems.at[t],
                         (x, y, 1 - c)) for t, (src, dst) in enumerate(zip(ins, outs))]
        return sends, recvs

    def start(ins, outs, sems):
        for cp in copies(ins, outs, sems)[0]:
            cp.start()

    def finish(ins, outs, sems):
        sends, recvs = copies(ins, outs, sems)
        for cp in recvs:
            cp.wait_recv()
        for cp in sends:
            cp.wait_send()

    comm = _Comm(bufs, [jax.ShapeDtypeStruct(b.shape, b.dtype) for b in bufs],
                 [_dma_sems(n), _dma_sems(n)], start, finish, aliases={i: i for i in range(n)})
    return _call(None, name="join_halves", comm=comm)()[1]


def _allreduce_small(pack):
    rows, cols = pack.shape

    def body(p_ref, o_ref, slots, send_sems, recv_sems):
        x, y, c = _mesh_pos()
        me = 4 * x + 2 * y + c
        slots[me] = p_ref[...]
        copies = []
        for r in range(1, N_DEV):
            fx, fy, fc = (r >> 2) & 1, (r >> 1) & 1, r & 1
            dev = ((1 - x) if fx else x, (1 - y) if fy else y, (1 - c) if fc else c)
            cp = pltpu.make_async_remote_copy(
                src_ref=p_ref, dst_ref=slots.at[me], send_sem=send_sems.at[r - 1],
                recv_sem=recv_sems.at[r - 1], device_id=dev, device_id_type=MESH)
            cp.start()
            copies.append(cp)
        for cp in copies:
            cp.wait_recv()
        for cp in copies:
            cp.wait_send()
        acc = slots[0]
        for d in range(1, N_DEV):
            acc = acc + slots[d]
        o_ref[...] = acc

    vmem = pl.BlockSpec(memory_space=pltpu.VMEM)
    return _call(
        body, name="allreduce_small", in_specs=[vmem], out_specs=vmem,
        out_shape=jax.ShapeDtypeStruct((rows, cols), F32),
        scratch=[pltpu.VMEM((N_DEV, rows, cols), F32), pltpu.SemaphoreType.DMA((N_DEV - 1,)),
                 pltpu.SemaphoreType.DMA((N_DEV - 1,))],
    )(pack)


def _pad_rows(a, rows=8):
    return jnp.pad(a, ((0, rows - a.shape[0]), (0, 0)))


def kernel(x, positions, norm1_w, w_in, merge_gate_b, ret_gn_w, w_ret_o, lru_conv_w, lru_conv_b, lru_w_r, lru_b_r, lru_w_i, lru_b_i, lru_lambda, w_lru_o, w_out, norm2_w, ffn_w_up, ffn_conv_w, ffn_conv_b, ffn_w_down, norm_f_w, loss_target, m_norm1_w, m_w_in, m_merge_gate_b, m_ret_gn_w, m_w_ret_o, m_lru_conv_w, m_lru_conv_b, m_lru_w_r, m_lru_b_r, m_lru_w_i, m_lru_b_i, m_lru_lambda, m_w_lru_o, m_w_out, m_norm2_w, m_ffn_w_up, m_ffn_conv_w, m_ffn_conv_b, m_ffn_w_down, m_norm_f_w, v_norm1_w, v_w_in, v_merge_gate_b, v_ret_gn_w, v_w_ret_o, v_lru_conv_w, v_lru_conv_b, v_lru_w_r, v_lru_b_r, v_lru_w_i, v_lru_b_i, v_lru_lambda, v_w_lru_o, v_w_out, v_norm2_w, v_ffn_w_up, v_ffn_conv_w, v_ffn_conv_b, v_ffn_w_down, v_norm_f_w):
    names = ["norm1_w", "w_in", "merge_gate_b", "ret_gn_w", "w_ret_o", "lru_conv_w", "lru_conv_b", "lru_w_r",
             "lru_b_r", "lru_w_i", "lru_b_i", "lru_lambda", "w_lru_o", "w_out", "norm2_w", "ffn_w_up",
             "ffn_conv_w", "ffn_conv_b", "ffn_w_down", "norm_f_w"]
    w_args = dict(zip(names, [norm1_w, w_in, merge_gate_b, ret_gn_w, w_ret_o, lru_conv_w, lru_conv_b, lru_w_r,
                              lru_b_r, lru_w_i, lru_b_i, lru_lambda, w_lru_o, w_out, norm2_w, ffn_w_up,
                              ffn_conv_w, ffn_conv_b, ffn_w_down, norm_f_w]))
    m_args = dict(zip(names, [m_norm1_w, m_w_in, m_merge_gate_b, m_ret_gn_w, m_w_ret_o, m_lru_conv_w,
                              m_lru_conv_b, m_lru_w_r, m_lru_b_r, m_lru_w_i, m_lru_b_i, m_lru_lambda, m_w_lru_o,
                              m_w_out, m_norm2_w, m_ffn_w_up, m_ffn_conv_w, m_ffn_conv_b, m_ffn_w_down,
                              m_norm_f_w]))
    v_args = dict(zip(names, [v_norm1_w, v_w_in, v_merge_gate_b, v_ret_gn_w, v_w_ret_o, v_lru_conv_w,
                              v_lru_conv_b, v_lru_w_r, v_lru_b_r, v_lru_w_i, v_lru_b_i, v_lru_lambda, v_w_lru_o,
                              v_w_out, v_norm2_w, v_ffn_w_up, v_ffn_conv_w, v_ffn_conv_b, v_ffn_w_down,
                              v_norm_f_w]))

    bsz, seq, d = x.shape
    m = bsz * seq
    tm = min(MM_ROWS, m)
    tm_fused = min(FUSED_ROWS, m)
    tm_tall = min(TALL_ROWS, m)
    chip = 2 * lax.axis_index("x") + lax.axis_index("y")

    big = ["w_in", "w_ret_o", "w_lru_o", "w_out", "lru_w_r", "lru_w_i", "ffn_w_up", "ffn_w_down"]
    cut = dict(w_in=(1, 0), w_ret_o=(0, 0), w_lru_o=(0, 0), w_out=(0, 0), lru_w_r=(1, 0), lru_w_i=(1, 0),
               ffn_w_up=(1, 0), ffn_w_down=(0, 0))
    later = big[1:]
    core = lax.axis_index("c")
    chip1 = jnp.reshape(chip, (1,)).astype(jnp.int32)
    pos = jnp.stack([chip, core]).astype(jnp.int32)
    placed = {n: _place_shard(w_args[n][0], chip1, cut[n][0], name="place_" + n) for n in big}
    small_pack = jnp.concatenate([
        jnp.pad(merge_gate_b[0], ((0, 6), (0, 512))),
        jnp.pad(lru_conv_w[0], ((0, 4), (0, 512))),
        jnp.pad(lru_b_r[0], ((0, 4), (0, 704))),
        jnp.pad(lru_b_i[0], ((0, 4), (0, 704))),
        jnp.pad(ffn_conv_w[0], ((0, 5), (0, 0))),
    ], axis=0)
    x2d = x.reshape(m, d)
    mx, my = lax.axis_index("x"), lax.axis_index("y")
    order = jnp.stack([chip, 2 * (1 - mx) + my, 2 * mx + 1 - my, 2 * (1 - mx) + 1 - my]).astype(jnp.int32)
    later_cut = [cut[n] for n in later]
    h1 = _norm_bf16(x2d, norm1_w, name="norm1", tm=tm)
    proj, w_in_full, bufs, sp = _in_proj_gather(h1, placed["w_in"], [placed[n] for n in later], later_cut,
                                               small_pack, order, tm=tm)
    wb = {"w_in": w_in_full}
    mgb = jnp.transpose(sp[:, 0:2, 0:256], (1, 0, 2)).reshape(2, D_MODEL)
    lcw = jnp.transpose(sp[:, 8:12, 0:256], (1, 0, 2)).reshape(4, D_MODEL)
    lbr = jnp.transpose(sp[:, 16:20, 0:64], (1, 0, 2)).reshape(1, D_MODEL)
    lbi = jnp.transpose(sp[:, 24:28, 0:64], (1, 0, 2)).reshape(1, D_MODEL)
    fcw = jnp.transpose(sp[:, 32:35, :], (1, 0, 2)).reshape(3, D_FF)
    nfw = norm_f_w.reshape(1, D_MODEL)

    half = RET_DK // 2
    inv_freq = ROPE_BASE ** (-jnp.arange(half, dtype=F32) / half)
    cos, sin = _rope_tables(positions.reshape(bsz, seq, 1), jnp.concatenate([inv_freq, inv_freq]).reshape(1, RET_DK))
    proj3 = proj.reshape(bsz, seq, D_IN)
    (a_in3, states), bufs = _retention_fwd(
        proj3, cos, sin, ret_gn_w,
        comm=_gather_d2d(bufs, later_cut, [w_args[n].shape[1 + cut[n][0]] for n in later]))
    wb.update(zip(later, bufs))
    lru_params = (lcw, lru_conv_b, wb["lru_w_r"], lbr, wb["lru_w_i"], lbi, lru_lambda)
    b_in3, *lru_kept = _lru_fwd(proj3, lru_params)
    a_in, b_in = a_in3.reshape(m, d), b_in3.reshape(m, d)
    x2 = _merge_fwd(a_in, b_in, proj, mgb, wb["w_ret_o"], wb["w_lru_o"], wb["w_out"], x2d, tm=tm_fused)
    up, h2 = _norm_matmul(x2, norm2_w, wb["ffn_w_up"], name="ffn_up", tm=tm_tall, tn=TALL_COLS)
    up3 = up.reshape(bsz, seq, 2 * D_FF)
    f3, act3, q3 = _ffn_act_fwd(up3, fcw, ffn_conv_b)
    f = f3.reshape(m, D_FF)
    loss_dev, dx3, sm_nf = _ffn_down_loss(f, wb["ffn_w_down"], x2, nfw, loss_target.reshape(m, d), tm=tm)
    loss = lax.psum(loss_dev[0, 0], ("x", "y", "c"))

    def send(*ns):
        return _exchange([g_full[n] for n in ns], [cut[n] for n in ns])

    g_full, parts = {}, {}
    df = _mm_nt(dx3, wb["ffn_w_down"], name="ffn_down_dx", tm=tm_tall, out_dtype=BF16)
    g_full["ffn_w_down"] = _mm_tn(f, [dx3], name="ffn_down_dw", tm=tm)
    (dgate3, dval3, sm_ffn), (parts["ffn_w_down"],) = _ffn_act_bwd(
        up3, act3, q3, fcw, df.reshape(bsz, seq, D_FF), comm=send("ffn_w_down"))
    dup = [dgate3.reshape(m, D_FF), dval3.reshape(m, D_FF)]
    g_full["ffn_w_up"] = _mm_tn(h2, dup, name="ffn_up_dw", tm=tm)
    (dx2, sm_n2), (parts["ffn_w_up"],) = _mm_nt_normbwd(
        dup, wb["ffn_w_up"], x2, norm2_w, dx3, name="ffn_up_dx", tm=tm, row=ROW_N2, comm=send("ffn_w_up"))
    mix, dya, dyb, da_in, db_in, dgates, sm_mg = _merge_bwd(
        a_in, b_in, proj, mgb, wb["w_ret_o"], wb["w_lru_o"], wb["w_out"], dx2, tm=tm_fused)
    g_full["w_out"] = _mm_tn(mix, [dx2], name="out_dw", tm=tm)
    g_full["w_ret_o"] = _mm_tn(a_in, [dya], name="ret_o_dw", tm=tm)
    g_full["w_lru_o"] = _mm_tn(b_in, [dyb], name="lru_o_dw", tm=tm)
    (dlru3, dwr, dwi, sm_lru), (parts["w_out"], parts["w_ret_o"], parts["w_lru_o"]) = _lru_bwd(
        proj3, lru_params, lru_kept, db_in.reshape(bsz, seq, d), comm=send("w_out", "w_ret_o", "w_lru_o"))
    g_full["lru_w_r"], g_full["lru_w_i"] = dwr.astype(BF16), dwi.astype(BF16)
    (dret3, sm_gn), (parts["lru_w_r"], parts["lru_w_i"]) = _retention_bwd(
        proj3, cos, sin, ret_gn_w, states, da_in.reshape(bsz, seq, d), comm=send("lru_w_r", "lru_w_i"))
    dproj = [dret3.reshape(m, 3072), dlru3.reshape(m, 2048), dgates]
    g_full["w_in"] = _mm_tn(h1, dproj, name="in_proj_dw", tm=tm)
    (grad_x, sm_n1), (parts["w_in"],) = _mm_nt_normbwd(
        dproj, wb["w_in"], x2d, norm1_w, dx2, name="in_proj_dx", tm=tm, row=ROW_N1, comm=send("w_in"))

    reduced = _join_halves([_reduce_half(g_full[n], parts[n], pos, cut[n], name="sum_" + n) for n in big])
    misc = sm_n1 + sm_mg + sm_gn + sm_n2 + sm_nf
    pack = jnp.concatenate(
        [misc, sm_lru, sm_ffn[:, 0:1024], sm_ffn[:, 1024:2048], sm_ffn[:, 2048:3072]], axis=0)
    tot = _allreduce_small(pack)
    ffn_sm = jnp.concatenate([tot[16:24], tot[24:32], tot[32:40]], axis=1)
    g_small = {
        "norm1_w": tot[ROW_N1:ROW_N1 + 1], "merge_gate_b": tot[ROW_MGB:ROW_MGB + 2],
        "ret_gn_w": tot[ROW_GN:ROW_GN + 1], "norm2_w": tot[ROW_N2:ROW_N2 + 1], "norm_f_w": tot[ROW_NF:ROW_NF + 1],
        "lru_conv_w": tot[8:12], "lru_conv_b": tot[12:13], "lru_b_r": tot[13:14].reshape(4, 256),
        "lru_b_i": tot[14:15].reshape(4, 256), "lru_lambda": tot[15:16],
        "ffn_conv_w": ffn_sm[0:3], "ffn_conv_b": ffn_sm[3:4],
    }
    small_shard = dict(merge_gate_b=256, lru_conv_w=256, lru_b_r=64, lru_b_i=64, ffn_conv_w=768)

    outs = {}
    for n, g in zip(big, reduced):
        shape = w_args[n].shape
        g = g.reshape(-1, g.shape[-1])
        outs[n] = [o.reshape(shape) for o in _adamw(
            w_args[n].reshape(g.shape), [g], m_args[n].reshape(g.shape), v_args[n].reshape(g.shape),
            name="adamw_" + n)]
    for n, g in g_small.items():
        shape = w_args[n].shape
        if n in small_shard:
            g = lax.dynamic_slice_in_dim(g, chip * small_shard[n], small_shard[n], axis=1)
        w2 = w_args[n].reshape(g.shape)
        outs[n] = [o.reshape(shape) for o in _adamw(
            w2, [g], m_args[n].reshape(g.shape), v_args[n].reshape(g.shape), name="adamw_" + n)]

    result = [loss, grad_x.reshape(bsz, seq, d)]
    for k in range(4):
        result += [outs[n][k] for n in names]
    return tuple(result)
```

```python
import functools
import math

import numpy as np
import jax
import jax.numpy as jnp
from jax import lax
from jax.experimental import pallas as pl
from jax.experimental.pallas import tpu as pltpu

F32 = jnp.float32
BF16 = jnp.bfloat16

D_MODEL = 1024
RET_HEADS = 4
RET_DK = 128
RET_DV = 256
LRU_BLOCKS = 4
LRU_BLOCK = 256
LRU_C = 8.0
D_FF = 3072
D_IN = 7168
ROPE_BASE = 10000.0
RMS_EPS = 1e-6
GN_EPS = 1e-6
ADAM_LR, ADAM_B1, ADAM_B2, ADAM_EPS, ADAM_WD, ADAM_STEP = 0.001, 0.9, 0.999, 1e-08, 0.01, 10

N_CHIPS = 4
N_DEV = 8
SEQ_T = 256
REF_CHUNK = 64
COL = 1024
MM_ROWS = 1024
TALL_ROWS, TALL_COLS = 2048, 1024
FUSED_ROWS = 512
VMEM_LIMIT_BYTES = 56 * 1024 * 1024
MESH = pl.DeviceIdType.MESH
ROW_N1, ROW_MGB, ROW_GN, ROW_N2, ROW_NF = 0, 1, 3, 4, 5
GELU_K = math.sqrt(2.0 / math.pi)
GELU_C = 0.044715


class _Comm:
    def __init__(self, ins, outs, sems, start, finish, aliases=None):
        self.ins, self.outs, self.sems = list(ins), list(outs), list(sems)
        self.start, self.finish, self.aliases = start, finish, dict(aliases or {})


def _call(body, *, name, out_shape=(), grid=None, in_specs=(), out_specs=(), scratch=(), comm=None, prefetch=0,
          aliases=None):
    single = not isinstance(out_shape, (list, tuple))
    out_shape = [out_shape] if single else list(out_shape)
    out_specs = [out_specs] if single else list(out_specs)
    in_specs, scratch = list(in_specs), list(scratch)
    n_in, n_out, n_scr = len(in_specs), len(out_shape), len(scratch)
    kwargs = dict(name=name, compiler_params=pltpu.CompilerParams(vmem_limit_bytes=VMEM_LIMIT_BYTES))
    if prefetch:
        assert comm is None
        spec = pltpu.PrefetchScalarGridSpec(num_scalar_prefetch=prefetch, grid=grid, in_specs=in_specs,
                                            out_specs=out_specs, scratch_shapes=scratch)
        fn = pl.pallas_call(body, out_shape=out_shape, grid_spec=spec, input_output_aliases=dict(aliases or {}),
                            **kwargs)
        return (lambda *args: fn(*args)[0]) if single else fn
    if grid is not None:
        kwargs["grid"] = grid
    if comm is None:
        fn = pl.pallas_call(body, out_shape=out_shape, in_specs=in_specs, out_specs=out_specs,
                            scratch_shapes=scratch, **kwargs)
        return (lambda *args: fn(*args)[0]) if single else fn

    any_spec = pl.BlockSpec(memory_space=pl.ANY)
    n_cin, n_cout = len(comm.ins), len(comm.outs)

    def wrapped(*refs):
        ins, refs = refs[:n_in], refs[n_in:]
        cins, refs = refs[:n_cin], refs[n_cin:]
        outs, refs = refs[:n_out], refs[n_out:]
        couts, refs = refs[:n_cout], refs[n_cout:]
        scr, csems = refs[:n_scr], refs[n_scr:]
        if grid is None:
            comm.start(cins, couts, csems)
            comm.finish(cins, couts, csems)
            return
        ids = [pl.program_id(a) for a in range(len(grid))]
        first = functools.reduce(jnp.logical_and, [i == 0 for i in ids])
        last = functools.reduce(jnp.logical_and, [i == g - 1 for i, g in zip(ids, grid)])
        pl.when(first)(lambda: comm.start(cins, couts, csems))
        body(*ins, *outs, *scr)
        pl.when(last)(lambda: comm.finish(cins, couts, csems))

    fn = pl.pallas_call(
        wrapped, out_shape=out_shape + comm.outs, in_specs=in_specs + [any_spec] * n_cin,
        out_specs=out_specs + [any_spec] * n_cout, scratch_shapes=scratch + comm.sems,
        input_output_aliases={n_in + i: n_out + o for i, o in comm.aliases.items()}, **kwargs)

    def run(*args):
        res = fn(*args, *comm.ins)
        own = res[0] if single else list(res[:n_out])
        return own, list(res[n_out:])

    return run


def _dot(a, b):
    return jnp.dot(a, b, preferred_element_type=F32)


def _dot_nt(a, b):
    return lax.dot_general(a, b, (((1,), (1,)), ((), ())), preferred_element_type=F32)


def _dot_tn(a, b):
    return lax.dot_general(a, b, (((0,), (0,)), ((), ())), preferred_element_type=F32)


def _bf(x):
    return x.astype(BF16)


def _sigmoid(x):
    return 1.0 / (1.0 + jnp.exp(-x))


def _gelu(x):
    return 0.5 * x * (1.0 + jnp.tanh(GELU_K * (x + GELU_C * x * x * x)))


def _gelu_and_grad(x):
    x2 = x * x
    t = jnp.tanh(x * (GELU_K * GELU_C * x2 + GELU_K))
    hx = 0.5 * x
    g = hx + hx * t
    dg = 0.5 + 0.5 * t + hx * (1.0 - t * t) * (3.0 * GELU_K * GELU_C * x2 + GELU_K)
    return g, dg


def _rms(x):
    r = lax.rsqrt(jnp.mean(x * x, axis=-1, keepdims=True) + RMS_EPS)
    return x * r, r


def _rms_bwd(dy, x, nw):
    xh, r = _rms(x)
    g = dy * nw
    dx = r * (g - xh * jnp.mean(g * xh, axis=-1, keepdims=True))
    return dx, jnp.sum(dy * xh, axis=0, keepdims=True)


def _row_acc(ref, row, val):
    ref[row:row + 1, :] = ref[row:row + 1, :] + val


def _shift_down(x, j, prev8):
    if j == 0:
        return x
    n = x.shape[0] // 8
    row = lax.broadcasted_iota(jnp.int32, prev8.shape, 0)
    turned = [pltpu.roll(prev8, j, 0)] + [pltpu.roll(x[8 * k:8 * k + 8], j, 0) for k in range(n)]
    return jnp.concatenate([jnp.where(row < j, turned[k], turned[k + 1]) for k in range(n)], axis=0)


def _shift_up(x, j, next8):
    if j == 0:
        return x
    n = x.shape[0] // 8
    row = lax.broadcasted_iota(jnp.int32, next8.shape, 0)
    turned = [pltpu.roll(x[8 * k:8 * k + 8], 8 - j, 0) for k in range(n)] + [pltpu.roll(next8, 8 - j, 0)]
    return jnp.concatenate([jnp.where(row >= 8 - j, turned[k + 1], turned[k]) for k in range(n)], axis=0)


def _scan_fwd(a, b, carry):
    row = lax.broadcasted_iota(jnp.int32, (8, a.shape[1]), 0)
    out = []
    for k in range(a.shape[0] // 8):
        ak, bk = a[8 * k:8 * k + 8], b[8 * k:8 * k + 8]
        for s in (1, 2, 4):
            keep = row >= s
            ar, br = pltpu.roll(ak, s, 0), pltpu.roll(bk, s, 0)
            bk = jnp.where(keep, ak * br + bk, bk)
            ak = jnp.where(keep, ak * ar, ak)
        hk = ak * carry + bk
        carry = hk[7:8]
        out.append(hk)
    return jnp.concatenate(out, axis=0)


def _scan_bwd(a, b, carry):
    row = lax.broadcasted_iota(jnp.int32, (8, a.shape[1]), 0)
    out = []
    for k in reversed(range(a.shape[0] // 8)):
        ak, bk = a[8 * k:8 * k + 8], b[8 * k:8 * k + 8]
        for s in (1, 2, 4):
            keep = row < 8 - s
            ar, br = pltpu.roll(ak, 8 - s, 0), pltpu.roll(bk, 8 - s, 0)
            bk = jnp.where(keep, ak * br + bk, bk)
            ak = jnp.where(keep, ak * ar, ak)
        gk = bk + ak * carry
        carry = gk[0:1]
        out.append(gk)
    return jnp.concatenate(out[::-1], axis=0)


def _norm_matmul(x, nw, w, *, name, tm, tn):
    m, d = x.shape
    n = w.shape[1]

    def body(x_ref, nw_ref, w_ref, o_ref, h_ref, h_sc):
        @pl.when(pl.program_id(1) == 0)
        def _():
            xh, _ = _rms(x_ref[...])
            h = _bf(xh * nw_ref[...])
            h_sc[...] = h
            h_ref[...] = h

        o_ref[...] = _bf(_dot(h_sc[...], w_ref[...]))

    return _call(
        body, name=name, grid=(m // tm, n // tn),
        in_specs=[pl.BlockSpec((tm, d), lambda i, j: (i, 0)),
                  pl.BlockSpec((1, d), lambda i, j: (0, 0)),
                  pl.BlockSpec((d, tn), lambda i, j: (0, j))],
        out_specs=[pl.BlockSpec((tm, tn), lambda i, j: (i, j)),
                   pl.BlockSpec((tm, d), lambda i, j: (i, 0))],
        out_shape=[jax.ShapeDtypeStruct((m, n), BF16), jax.ShapeDtypeStruct((m, d), BF16)],
        scratch=[pltpu.VMEM((tm, d), BF16)],
    )(x, nw, w)


def _mm_nt(a, w, *, name, tm, out_dtype):
    m, k = a.shape
    n = w.shape[0]

    def body(a_ref, w_ref, o_ref):
        o_ref[...] = _dot_nt(_bf(a_ref[...]), w_ref[...]).astype(out_dtype)

    return _call(
        body, name=name, grid=(m // tm, n // COL),
        in_specs=[pl.BlockSpec((tm, k), lambda i, j: (i, 0)),
                  pl.BlockSpec((COL, k), lambda i, j: (j, 0))],
        out_specs=pl.BlockSpec((tm, COL), lambda i, j: (i, j)),
        out_shape=jax.ShapeDtypeStruct((m, n), out_dtype),
    )(a, w)


def _piece_layout(pieces):
    offs, nblk, o = [], [], 0
    for p in pieces:
        offs.append(o)
        nblk.append(p.shape[1] // COL)
        o += p.shape[1] // COL
    return offs, nblk, o


def _mm_tn(a, pieces, *, name, tm, out_dtype=BF16):
    m, k = a.shape
    offs, nblk, nn = _piece_layout(pieces)

    def piece_spec(o, nb):
        def idx(ki, nj, mi):
            use = jnp.logical_and(nj >= o, nj < o + nb)
            return (jnp.where(use, mi, 0), jnp.clip(nj - o, 0, nb - 1))
        return pl.BlockSpec((tm, COL), idx)

    def body(a_ref, *rest):
        p_refs, o_ref, acc = rest[:len(pieces)], rest[len(pieces)], rest[len(pieces) + 1]
        nj, mi = pl.program_id(1), pl.program_id(2)

        @pl.when(mi == 0)
        def _():
            acc[...] = jnp.zeros_like(acc)

        for p_ref, o, nb in zip(p_refs, offs, nblk):
            @pl.when(jnp.logical_and(nj >= o, nj < o + nb))
            def _(p_ref=p_ref):
                acc[...] += _dot_tn(_bf(a_ref[...]), _bf(p_ref[...]))

        @pl.when(mi == pl.num_programs(2) - 1)
        def _():
            o_ref[...] = acc[...].astype(out_dtype)

    return _call(
        body, name=name, grid=(k // COL, nn, m // tm),
        in_specs=[pl.BlockSpec((tm, COL), lambda ki, nj, mi: (mi, ki))]
        + [piece_spec(o, nb) for o, nb in zip(offs, nblk)],
        out_specs=pl.BlockSpec((COL, COL), lambda ki, nj, mi: (ki, nj)),
        out_shape=jax.ShapeDtypeStruct((k, nn * COL), out_dtype),
        scratch=[pltpu.VMEM((COL, COL), F32)],
    )(a, *pieces)


def _mm_nt_normbwd(pieces, w, x, nw, dres, *, name, tm, row, comm=None):
    m, d = x.shape
    offs, nblk, nk = _piece_layout(pieces)

    def piece_spec(o, nb):
        return pl.BlockSpec((tm, COL), lambda i, k: (i, jnp.clip(k - o, 0, nb - 1)))

    def body(*refs):
        p_refs = refs[:len(pieces)]
        w_ref, x_ref, nw_ref, dres_ref, dx_ref, dnw_ref, acc = refs[len(pieces):]
        i, k = pl.program_id(0), pl.program_id(1)

        @pl.when(jnp.logical_and(i == 0, k == 0))
        def _():
            dnw_ref[...] = jnp.zeros_like(dnw_ref)

        @pl.when(k == 0)
        def _():
            acc[...] = jnp.zeros_like(acc)

        for p_ref, o, nb in zip(p_refs, offs, nblk):
            @pl.when(jnp.logical_and(k >= o, k < o + nb))
            def _(p_ref=p_ref):
                acc[...] += _dot_nt(_bf(p_ref[...]), w_ref[...])

        @pl.when(k == nk - 1)
        def _():
            dx, dnw = _rms_bwd(acc[...], x_ref[...], nw_ref[...])
            dx_ref[...] = dres_ref[...] + dx
            _row_acc(dnw_ref, row, dnw)

    return _call(
        body, name=name, grid=(m // tm, nk), comm=comm,
        in_specs=[piece_spec(o, nb) for o, nb in zip(offs, nblk)]
        + [pl.BlockSpec((d, COL), lambda i, k: (0, k)),
           pl.BlockSpec((tm, d), lambda i, k: (i, 0)),
           pl.BlockSpec((1, d), lambda i, k: (0, 0)),
           pl.BlockSpec((tm, d), lambda i, k: (i, 0))],
        out_specs=[pl.BlockSpec((tm, d), lambda i, k: (i, 0)),
                   pl.BlockSpec((8, d), lambda i, k: (0, 0))],
        out_shape=[jax.ShapeDtypeStruct((m, d), F32), jax.ShapeDtypeStruct((8, d), F32)],
        scratch=[pltpu.VMEM((tm, d), F32)],
    )(*pieces, w, x, nw, dres)


def _rope_tables(pos3, invf):
    b, s, _ = pos3.shape

    def body(pos_ref, invf_ref, cos_ref, sin_ref):
        ang = pos_ref[...].astype(F32) * invf_ref[...]
        lane = lax.broadcasted_iota(jnp.int32, ang.shape, 1)
        cos_ref[...] = jnp.cos(ang)
        sin_ref[...] = jnp.where(lane < RET_DK // 2, -1.0, 1.0) * jnp.sin(ang)

    spec = pl.BlockSpec((None, SEQ_T, RET_DK), lambda i, c: (i, c, 0))
    return _call(
        body, name="rope_tables", grid=(b, s // SEQ_T),
        in_specs=[pl.BlockSpec((None, SEQ_T, 1), lambda i, c: (i, c, 0)),
                  pl.BlockSpec((1, RET_DK), lambda i, c: (0, 0))],
        out_specs=[spec, spec],
        out_shape=[jax.ShapeDtypeStruct((b, s, RET_DK), F32)] * 2,
    )(pos3, invf)


def _log_gamma(h):
    return float(np.log1p(-np.power(np.float32(2.0), np.float32(-5.0 - h))).astype(np.float32))


def _decay_matrix(h):
    lg = _log_gamma(h)
    n = lax.broadcasted_iota(jnp.int32, (SEQ_T, SEQ_T), 0)
    m = lax.broadcasted_iota(jnp.int32, (SEQ_T, SEQ_T), 1)
    same = (n // REF_CHUNK) == (m // REF_CHUNK)
    dist = jnp.where(same, jnp.abs(n - m), n - m).astype(F32)
    return jnp.where(jnp.logical_or(same, m < n), jnp.exp(lg * dist), 0.0)


def _decay_vectors(h):
    lg = _log_gamma(h)
    idx = lax.broadcasted_iota(jnp.int32, (SEQ_T, 1), 0).astype(F32)
    qd = jnp.exp(lg * (idx + 1.0))
    kd = jnp.exp(lg * (SEQ_T - 1.0 - idx))
    return qd, kd, math.exp(lg * SEQ_T)


def _rotate(x, cos, sin):
    return x * cos + pltpu.roll(x, RET_DK // 2, 1) * sin


def _rotate_bwd(d, cos, sin):
    return d * cos + pltpu.roll(d * sin, RET_DK // 2, 1)


def _ret_head(p_ref, cos, sin, h):
    q = p_ref[:, h * RET_DK:(h + 1) * RET_DK].astype(F32)
    k = p_ref[:, 512 + h * RET_DK:512 + (h + 1) * RET_DK].astype(F32)
    v = p_ref[:, 1024 + h * RET_DV:1024 + (h + 1) * RET_DV]
    g = p_ref[:, 2048 + h * RET_DV:2048 + (h + 1) * RET_DV].astype(F32)
    qr = _rotate(q, cos, sin)
    kr = _rotate(k, cos, sin) * (RET_DK ** -0.5)
    return qr, kr, v, g


def _group_norm(o):
    mu = jnp.mean(o, axis=-1, keepdims=True)
    oc = o - mu
    rstd = lax.rsqrt(jnp.mean(oc * oc, axis=-1, keepdims=True) + GN_EPS)
    return oc * rstd, rstd


def _retention_fwd(proj3, cos, sin, gnw, comm=None):
    b, s, _ = proj3.shape
    nc = s // SEQ_T

    def body(p_ref, cos_ref, sin_ref, gnw_ref, a_ref, st_ref, state, wtab):
        c = pl.program_id(1)

        @pl.when(jnp.logical_and(pl.program_id(0) == 0, c == 0))
        def _():
            for h in range(RET_HEADS):
                wtab[h] = _decay_matrix(h)

        @pl.when(c == 0)
        def _():
            state[...] = jnp.zeros_like(state)

        cs, sn = cos_ref[...], sin_ref[...]
        for h in range(RET_HEADS):
            qd, kd, gt = _decay_vectors(h)
            qr, kr, v, g = _ret_head(p_ref, cs, sn, h)
            st = state[h]
            st_ref[h] = st
            p = _dot_nt(_bf(qr), _bf(kr)) * wtab[h]
            o = _dot(_bf(p), _bf(v)) + _dot(_bf(qr * qd), _bf(st))
            state[h] = st * gt + _dot_tn(_bf(kr * kd), _bf(v))
            on, _ = _group_norm(o)
            gw = gnw_ref[:, h * RET_DV:(h + 1) * RET_DV]
            a_ref[:, h * RET_DV:(h + 1) * RET_DV] = _bf(on * gw * (g * _sigmoid(g)))

    tab = pl.BlockSpec((None, SEQ_T, RET_DK), lambda i, c: (i, c, 0))
    return _call(
        body, name="retention_fwd", grid=(b, nc), comm=comm,
        in_specs=[pl.BlockSpec((None, SEQ_T, 3072), lambda i, c: (i, c, 0)), tab, tab,
                  pl.BlockSpec((1, D_MODEL), lambda i, c: (0, 0))],
        out_specs=[pl.BlockSpec((None, SEQ_T, D_MODEL), lambda i, c: (i, c, 0)),
                   pl.BlockSpec((None, None, RET_HEADS, RET_DK, RET_DV), lambda i, c: (i, c, 0, 0, 0))],
        out_shape=[jax.ShapeDtypeStruct((b, s, D_MODEL), BF16),
                   jax.ShapeDtypeStruct((b, nc, RET_HEADS, RET_DK, RET_DV), F32)],
        scratch=[pltpu.VMEM((RET_HEADS, RET_DK, RET_DV), F32),
                 pltpu.VMEM((RET_HEADS, SEQ_T, SEQ_T), F32)],
    )(proj3, cos, sin, gnw)


def _retention_bwd(proj3, cos, sin, gnw, states, da3, comm=None):
    b, s, _ = proj3.shape
    nc = s // SEQ_T

    def body(p_ref, cos_ref, sin_ref, gnw_ref, st_ref, da_ref, d_ref, dgn_ref, dstate, wtab):
        c = pl.program_id(1)

        @pl.when(jnp.logical_and(pl.program_id(0) == 0, c == 0))
        def _():
            dgn_ref[...] = jnp.zeros_like(dgn_ref)
            for h in range(RET_HEADS):
                wtab[h] = _decay_matrix(h)

        @pl.when(c == 0)
        def _():
            dstate[...] = jnp.zeros_like(dstate)

        cs, sn = cos_ref[...], sin_ref[...]
        for h in range(RET_HEADS):
            qd, kd, gt = _decay_vectors(h)
            qr, kr, v, g = _ret_head(p_ref, cs, sn, h)
            st, dst, w = st_ref[h], dstate[h], wtab[h]
            qb, kb, vb = _bf(qr), _bf(kr), _bf(v)
            p = _dot_nt(qb, kb) * w
            o = _dot(_bf(p), vb) + _dot(_bf(qr * qd), _bf(st))
            on, rstd = _group_norm(o)
            gw = gnw_ref[:, h * RET_DV:(h + 1) * RET_DV]
            da = da_ref[:, h * RET_DV:(h + 1) * RET_DV].astype(F32)
            sg = _sigmoid(g)
            silu = g * sg
            dg = da * on * gw * (sg * (1.0 + g * (1.0 - sg)))
            dgn_ref[ROW_GN:ROW_GN + 1, h * RET_DV:(h + 1) * RET_DV] += jnp.sum(da * silu * on, axis=0, keepdims=True)
            don = da * silu * gw
            do = rstd * (don - jnp.mean(don, axis=-1, keepdims=True)
                         - on * jnp.mean(don * on, axis=-1, keepdims=True))
            dob = _bf(do)
            dp = _dot_nt(dob, vb) * w
            dqr = _dot(_bf(dp), kb) + _dot_nt(dob, _bf(st)) * qd
            dkr = _dot_tn(_bf(dp), qb) + _dot_nt(vb, _bf(dst)) * kd
            dv = _dot_tn(_bf(p), dob) + _dot(_bf(kr * kd), _bf(dst))
            dstate[h] = dst * gt + _dot_tn(_bf(qr * qd), dob)
            d_ref[:, h * RET_DK:(h + 1) * RET_DK] = _bf(_rotate_bwd(dqr, cs, sn))
            d_ref[:, 512 + h * RET_DK:512 + (h + 1) * RET_DK] = _bf(_rotate_bwd(dkr, cs, sn) * (RET_DK ** -0.5))
            d_ref[:, 1024 + h * RET_DV:1024 + (h + 1) * RET_DV] = _bf(dv)
            d_ref[:, 2048 + h * RET_DV:2048 + (h + 1) * RET_DV] = _bf(dg)

    rev = lambda i, c: (i, nc - 1 - c, 0)
    tab = pl.BlockSpec((None, SEQ_T, RET_DK), rev)
    return _call(
        body, name="retention_bwd", grid=(b, nc), comm=comm,
        in_specs=[pl.BlockSpec((None, SEQ_T, 3072), rev), tab, tab,
                  pl.BlockSpec((1, D_MODEL), lambda i, c: (0, 0)),
                  pl.BlockSpec((None, None, RET_HEADS, RET_DK, RET_DV), lambda i, c: (i, nc - 1 - c, 0, 0, 0)),
                  pl.BlockSpec((None, SEQ_T, D_MODEL), rev)],
        out_specs=[pl.BlockSpec((None, SEQ_T, 3072), rev),
                   pl.BlockSpec((8, D_MODEL), lambda i, c: (0, 0))],
        out_shape=[jax.ShapeDtypeStruct((b, s, 3072), BF16), jax.ShapeDtypeStruct((8, D_MODEL), F32)],
        scratch=[pltpu.VMEM((RET_HEADS, RET_DK, RET_DV), F32),
                 pltpu.VMEM((RET_HEADS, SEQ_T, SEQ_T), F32)],
    )(proj3, cos, sin, gnw, states, da3)


def _softplus_neg(lam):
    z = -lam
    u = jnp.exp(-jnp.abs(z))
    log1p_u = jnp.where(u < 0.01, u * (1.0 - u * (0.5 - u * (1.0 / 3.0))), jnp.log(1.0 + u))
    return jnp.maximum(z, 0.0) + log1p_u


def _lru_coeffs(xc, wr_ref, br_ref, wi_ref, bi_ref, lam_ref):
    rs, is_ = [], []
    for n in range(LRU_BLOCKS):
        xb = _bf(xc[:, n * LRU_BLOCK:(n + 1) * LRU_BLOCK])
        cols = slice(n * LRU_BLOCK, (n + 1) * LRU_BLOCK)
        rs.append(_sigmoid(_dot(xb, wr_ref[n]) + br_ref[:, cols]))
        is_.append(_sigmoid(_dot(xb, wi_ref[n]) + bi_ref[:, cols]))
    r = jnp.concatenate(rs, axis=1)
    i = jnp.concatenate(is_, axis=1)
    sp = _softplus_neg(lam_ref[...])
    la = -LRU_C * r * sp
    a = jnp.exp(la)
    s = jnp.sqrt(-jnp.tanh(la) * (a * a + 1.0))
    return r, i, a, s, sp


_LRU_PARAM_SPECS = [
    pl.BlockSpec((4, D_MODEL), lambda i, c: (0, 0)),
    pl.BlockSpec((1, D_MODEL), lambda i, c: (0, 0)),
    pl.BlockSpec((LRU_BLOCKS, LRU_BLOCK, LRU_BLOCK), lambda i, c: (0, 0, 0)),
    pl.BlockSpec((1, D_MODEL), lambda i, c: (0, 0)),
    pl.BlockSpec((LRU_BLOCKS, LRU_BLOCK, LRU_BLOCK), lambda i, c: (0, 0, 0)),
    pl.BlockSpec((1, D_MODEL), lambda i, c: (0, 0)),
    pl.BlockSpec((1, D_MODEL), lambda i, c: (0, 0)),
]


def _lru_fwd(proj3, params):
    b, s, _ = proj3.shape
    nc = s // SEQ_T

    def body(x_ref, y_ref, cw, cb, wr, br, wi, bi, lam, o_ref, h_ref, xc_ref, gy_ref, hdg_ref, xprev, hprev):
        @pl.when(pl.program_id(1) == 0)
        def _():
            xprev[...] = jnp.zeros_like(xprev)
            hprev[...] = jnp.zeros_like(hprev)

        x = x_ref[...].astype(F32)
        prev8 = xprev[...]
        xc = cb[...] + sum(cw[j:j + 1, :] * _shift_down(x, 3 - j, prev8) for j in range(4))
        xprev[...] = x[SEQ_T - 8:]
        xc_ref[...] = xc
        _, i, a, s_, _ = _lru_coeffs(xc, wr, br, wi, bi, lam)
        h = _scan_fwd(a, s_ * (i * xc), hprev[7:8, :])
        hprev[...] = h[SEQ_T - 8:]
        h_ref[...] = h
        gy, dgy = _gelu_and_grad(y_ref[...].astype(F32))
        o_ref[...] = _bf(h * gy)
        gy_ref[...] = _bf(gy)
        hdg_ref[...] = _bf(h * dgy)

    out = pl.BlockSpec((None, SEQ_T, D_MODEL), lambda i, c: (i, c, 0))
    half, full = jax.ShapeDtypeStruct((b, s, D_MODEL), BF16), jax.ShapeDtypeStruct((b, s, D_MODEL), F32)
    return _call(
        body, name="lru_fwd", grid=(b, nc),
        in_specs=[pl.BlockSpec((None, SEQ_T, D_MODEL), lambda i, c: (i, c, 3)),
                  pl.BlockSpec((None, SEQ_T, D_MODEL), lambda i, c: (i, c, 4))] + _LRU_PARAM_SPECS,
        out_specs=[out] * 5, out_shape=[half, full, full, half, half],
        scratch=[pltpu.VMEM((8, D_MODEL), F32), pltpu.VMEM((8, D_MODEL), F32)],
    )(proj3, proj3, *params)


def _lru_bwd(proj3, params, kept, db3, comm=None):
    b, s, _ = proj3.shape
    nc = s // SEQ_T
    blk8 = SEQ_T // 8
    hseq, xcseq, gyseq, hdgseq = kept

    def body(x_ref, xc_ref, h_ref, hp_ref, gy_ref, hdg_ref, db_ref, cw, cb, wr, br, wi, bi, lam,
             d_ref, dwr_ref, dwi_ref, sm_ref, gnext, anext, dxcnext):
        c = pl.program_id(1)
        first_chunk = c == nc - 1

        @pl.when(jnp.logical_and(pl.program_id(0) == 0, c == 0))
        def _():
            dwr_ref[...] = jnp.zeros_like(dwr_ref)
            dwi_ref[...] = jnp.zeros_like(dwi_ref)
            sm_ref[...] = jnp.zeros_like(sm_ref)

        @pl.when(c == 0)
        def _():
            gnext[...] = jnp.zeros_like(gnext)
            anext[...] = jnp.zeros_like(anext)
            dxcnext[...] = jnp.zeros_like(dxcnext)

        x, xc, h = x_ref[...].astype(F32), xc_ref[...], h_ref[...]
        hprev = hp_ref[...] * jnp.where(first_chunk, 0.0, 1.0)
        r, i, a, s_, sp = _lru_coeffs(xc, wr, br, wi, bi, lam)
        db = db_ref[...].astype(F32)
        dy = db * hdg_ref[...].astype(F32)
        a_up = _shift_up(a, 1, anext[...])
        g = _scan_bwd(a_up, db * gy_ref[...].astype(F32), gnext[0:1, :])
        gnext[...] = g[0:8]
        anext[...] = a[0:8]
        da = g * _shift_down(h, 1, hprev)
        ixc = i * xc
        dla = da * a - (g * ixc) * (a * a) / s_
        di = g * s_ * xc
        dxc = g * s_ * i
        dzr = dla * (-LRU_C * sp) * r * (1.0 - r)
        dzi = di * i * (1.0 - i)
        lam_v = lam[...]
        _row_acc(sm_ref, 7, jnp.sum(dla * (LRU_C * r), axis=0, keepdims=True) * _sigmoid(-lam_v))
        _row_acc(sm_ref, 5, jnp.sum(dzr, axis=0, keepdims=True))
        _row_acc(sm_ref, 6, jnp.sum(dzi, axis=0, keepdims=True))
        parts = []
        for n in range(LRU_BLOCKS):
            cols = slice(n * LRU_BLOCK, (n + 1) * LRU_BLOCK)
            xb, zr, zi = _bf(xc[:, cols]), _bf(dzr[:, cols]), _bf(dzi[:, cols])
            parts.append(dxc[:, cols] + _dot_nt(zr, wr[n]) + _dot_nt(zi, wi[n]))
            dwr_ref[n] += _dot_tn(xb, zr)
            dwi_ref[n] += _dot_tn(xb, zi)
        dxc = jnp.concatenate(parts, axis=1)
        _row_acc(sm_ref, 4, jnp.sum(dxc, axis=0, keepdims=True))
        nxt = dxcnext[...]
        dx = jnp.zeros_like(x)
        for j in range(4):
            ahead = _shift_up(dxc, 3 - j, nxt)
            dx = dx + cw[j:j + 1, :] * ahead
            _row_acc(sm_ref, j, jnp.sum(ahead * x, axis=0, keepdims=True))
        dxcnext[...] = dxc[0:8]
        d_ref[:, 0:D_MODEL] = _bf(dx)
        d_ref[:, D_MODEL:2 * D_MODEL] = _bf(dy)

    rev = lambda col: (lambda i, c: (i, nc - 1 - c, col))
    prev = lambda col: (lambda i, c: (i, jnp.maximum((nc - 1 - c) * blk8 - 1, 0), col))
    return _call(
        body, name="lru_bwd", grid=(b, nc), comm=comm,
        in_specs=[pl.BlockSpec((None, SEQ_T, D_MODEL), rev(3)),
                  pl.BlockSpec((None, SEQ_T, D_MODEL), rev(0)),
                  pl.BlockSpec((None, SEQ_T, D_MODEL), rev(0)),
                  pl.BlockSpec((None, 8, D_MODEL), prev(0)),
                  pl.BlockSpec((None, SEQ_T, D_MODEL), rev(0)),
                  pl.BlockSpec((None, SEQ_T, D_MODEL), rev(0)),
                  pl.BlockSpec((None, SEQ_T, D_MODEL), rev(0))] + _LRU_PARAM_SPECS,
        out_specs=[pl.BlockSpec((None, SEQ_T, 2 * D_MODEL), rev(0)),
                   pl.BlockSpec((LRU_BLOCKS, LRU_BLOCK, LRU_BLOCK), lambda i, c: (0, 0, 0)),
                   pl.BlockSpec((LRU_BLOCKS, LRU_BLOCK, LRU_BLOCK), lambda i, c: (0, 0, 0)),
                   pl.BlockSpec((8, D_MODEL), lambda i, c: (0, 0))],
        out_shape=[jax.ShapeDtypeStruct((b, s, 2 * D_MODEL), BF16),
                   jax.ShapeDtypeStruct((LRU_BLOCKS, LRU_BLOCK, LRU_BLOCK), F32),
                   jax.ShapeDtypeStruct((LRU_BLOCKS, LRU_BLOCK, LRU_BLOCK), F32),
                   jax.ShapeDtypeStruct((8, D_MODEL), F32)],
        scratch=[pltpu.VMEM((8, D_MODEL), F32)] * 3,
    )(proj3, xcseq, hseq, hseq, gyseq, hdgseq, db3, *params)


def _merge_parts(a_ref, b_ref, gr_ref, gl_ref, mgb_ref, wro_ref, wlo_ref):
    ya = _dot(a_ref[...], wro_ref[...])
    yb = _dot(b_ref[...], wlo_ref[...])
    sa = _sigmoid(gr_ref[...].astype(F32) + mgb_ref[0:1, :])
    sb = _sigmoid(gl_ref[...].astype(F32) + mgb_ref[1:2, :])
    return ya, yb, sa, sb


def _merge_specs(tm):
    row = lambda col: pl.BlockSpec((tm, D_MODEL), lambda i: (i, col))
    full = pl.BlockSpec((D_MODEL, D_MODEL), lambda i: (0, 0))
    return row, full


def _merge_fwd(a_in, b_in, proj, mgb, wro, wlo, wout, x, *, tm):
    m = x.shape[0]
    row, full = _merge_specs(tm)

    def body(a_ref, b_ref, gr_ref, gl_ref, mgb_ref, wro_ref, wlo_ref, wout_ref, x_ref,
             o_ref, mix_ref, ya_ref, yb_ref):
        ya, yb, sa, sb = _merge_parts(a_ref, b_ref, gr_ref, gl_ref, mgb_ref, wro_ref, wlo_ref)
        mix = _bf(sa * ya + sb * yb)
        o_ref[...] = x_ref[...] + _dot(mix, wout_ref[...])
        mix_ref[...] = mix
        ya_ref[...] = _bf(ya)
        yb_ref[...] = _bf(yb)

    act = jax.ShapeDtypeStruct((m, D_MODEL), BF16)
    return _call(
        body, name="merge_fwd", grid=(m // tm,),
        in_specs=[row(0), row(0), row(5), row(6), pl.BlockSpec((2, D_MODEL), lambda i: (0, 0)),
                  full, full, full, row(0)],
        out_specs=[row(0)] * 4,
        out_shape=[jax.ShapeDtypeStruct((m, D_MODEL), F32), act, act, act],
    )(a_in, b_in, proj, proj, mgb, wro, wlo, wout, x)


def _merge_bwd(ya, yb, proj, mgb, wro, wlo, wout, dx2, *, tm):
    m = dx2.shape[0]
    row, full = _merge_specs(tm)

    def body(ya_ref, yb_ref, gr_ref, gl_ref, mgb_ref, wro_ref, wlo_ref, wout_ref, dx_ref,
             dya_ref, dyb_ref, da_ref, db_ref, dg_ref, sm_ref):
        @pl.when(pl.program_id(0) == 0)
        def _():
            sm_ref[...] = jnp.zeros_like(sm_ref)

        ya, yb = ya_ref[...].astype(F32), yb_ref[...].astype(F32)
        sa = _sigmoid(gr_ref[...].astype(F32) + mgb_ref[0:1, :])
        sb = _sigmoid(gl_ref[...].astype(F32) + mgb_ref[1:2, :])
        dmix = _dot_nt(_bf(dx_ref[...]), wout_ref[...])
        dya, dyb = _bf(dmix * sa), _bf(dmix * sb)
        dya_ref[...] = dya
        dyb_ref[...] = dyb
        dga = dmix * ya * sa * (1.0 - sa)
        dgb = dmix * yb * sb * (1.0 - sb)
        dg_ref[:, 0:D_MODEL] = _bf(dga)
        dg_ref[:, D_MODEL:2 * D_MODEL] = _bf(dgb)
        _row_acc(sm_ref, ROW_MGB, jnp.sum(dga, axis=0, keepdims=True))
        _row_acc(sm_ref, ROW_MGB + 1, jnp.sum(dgb, axis=0, keepdims=True))
        da_ref[...] = _bf(_dot_nt(dya, wro_ref[...]))
        db_ref[...] = _bf(_dot_nt(dyb, wlo_ref[...]))

    act = jax.ShapeDtypeStruct((m, D_MODEL), BF16)
    return _call(
        body, name="merge_bwd", grid=(m // tm,),
        in_specs=[row(0), row(0), row(5), row(6), pl.BlockSpec((2, D_MODEL), lambda i: (0, 0)),
                  full, full, full, row(0)],
        out_specs=[row(0)] * 4 + [pl.BlockSpec((tm, 2 * D_MODEL), lambda i: (i, 0)),
                                  pl.BlockSpec((8, D_MODEL), lambda i: (0, 0))],
        out_shape=[act] * 4 + [jax.ShapeDtypeStruct((m, 2 * D_MODEL), BF16),
                               jax.ShapeDtypeStruct((8, D_MODEL), F32)],
    )(ya, yb, proj, proj, mgb, wro, wlo, wout, dx2)


def _ffn_act_fwd(up3, cw, cb):
    b, s, _ = up3.shape

    def body(g_ref, v_ref, cw_ref, cb_ref, o_ref, act_ref, q_ref, gprev):
        @pl.when(pl.program_id(1) == 0)
        def _():
            gprev[...] = jnp.zeros_like(gprev)

        gate, val = g_ref[...].astype(F32), v_ref[...].astype(F32)
        prev8 = gprev[...]
        gc = cb_ref[...] + sum(cw_ref[j:j + 1, :] * _shift_down(gate, 2 - j, prev8) for j in range(3))
        gprev[...] = gate[SEQ_T - 8:]
        act, dact = _gelu_and_grad(gc)
        o_ref[...] = _bf(act * val)
        act_ref[...] = _bf(act)
        q_ref[...] = _bf(dact * val)

    out = pl.BlockSpec((None, SEQ_T, D_FF), lambda i, c: (i, c, 0))
    return _call(
        body, name="ffn_act_fwd", grid=(b, s // SEQ_T),
        in_specs=[pl.BlockSpec((None, SEQ_T, D_FF), lambda i, c: (i, c, 0)),
                  pl.BlockSpec((None, SEQ_T, D_FF), lambda i, c: (i, c, 1)),
                  pl.BlockSpec((3, D_FF), lambda i, c: (0, 0)),
                  pl.BlockSpec((1, D_FF), lambda i, c: (0, 0))],
        out_specs=[out] * 3,
        out_shape=[jax.ShapeDtypeStruct((b, s, D_FF), BF16)] * 3,
        scratch=[pltpu.VMEM((8, D_FF), F32)],
    )(up3, up3, cw, cb)


def _ffn_act_bwd(up3, act3, q3, cw, df3, comm=None):
    b, s, _ = up3.shape
    nc = s // SEQ_T

    def body(g_ref, act_ref, q_ref, df_ref, cw_ref, dg_ref, dv_ref, sm_ref, dgcnext):
        c = pl.program_id(1)

        @pl.when(jnp.logical_and(pl.program_id(0) == 0, c == 0))
        def _():
            sm_ref[...] = jnp.zeros_like(sm_ref)

        @pl.when(c == 0)
        def _():
            dgcnext[...] = jnp.zeros_like(dgcnext)

        gate = g_ref[...].astype(F32)
        df = df_ref[...].astype(F32)
        dv_ref[...] = _bf(df * act_ref[...].astype(F32))
        dgc = df * q_ref[...].astype(F32)
        nxt = dgcnext[...]
        dgate = jnp.zeros_like(gate)
        for j in range(3):
            ahead = _shift_up(dgc, 2 - j, nxt)
            dgate = dgate + cw_ref[j:j + 1, :] * ahead
            _row_acc(sm_ref, j, jnp.sum(ahead * gate, axis=0, keepdims=True))
        _row_acc(sm_ref, 3, jnp.sum(dgc, axis=0, keepdims=True))
        dgcnext[...] = dgc[0:8]
        dg_ref[...] = _bf(dgate)

    rev = pl.BlockSpec((None, SEQ_T, D_FF), lambda i, c: (i, nc - 1 - c, 0))
    return _call(
        body, name="ffn_act_bwd", grid=(b, nc), comm=comm,
        in_specs=[rev, rev, rev, rev, pl.BlockSpec((3, D_FF), lambda i, c: (0, 0))],
        out_specs=[rev, rev, pl.BlockSpec((8, D_FF), lambda i, c: (0, 0))],
        out_shape=[jax.ShapeDtypeStruct((b, s, D_FF), BF16)] * 2 + [jax.ShapeDtypeStruct((8, D_FF), F32)],
        scratch=[pltpu.VMEM((8, D_FF), F32)],
    )(up3, act3, q3, df3, cw)


def _ffn_down_loss(f, wd, x2, nfw, target, *, tm):
    m, kf = f.shape
    nt = m // tm

    def body(f_ref, wd_ref, x_ref, nw_ref, t_ref, loss_ref, dx_ref, dnw_ref, lsum):
        i = pl.program_id(0)

        @pl.when(i == 0)
        def _():
            dnw_ref[...] = jnp.zeros_like(dnw_ref)
            lsum[...] = jnp.zeros_like(lsum)

        x3 = x_ref[...] + _dot(f_ref[...], wd_ref[...])
        nw = nw_ref[...]
        xh, _ = _rms(x3)
        err = xh * nw - t_ref[...]
        lsum[...] += jnp.sum(err * err, axis=0, keepdims=True)
        dx, dnw = _rms_bwd(err * (1.0 / D_MODEL), x3, nw)
        dx_ref[...] = dx
        _row_acc(dnw_ref, ROW_NF, dnw)

        @pl.when(i == nt - 1)
        def _():
            loss_ref[...] = jnp.sum(lsum[...], axis=1, keepdims=True) * (0.5 / D_MODEL)

    row = pl.BlockSpec((tm, D_MODEL), lambda i: (i, 0))
    return _call(
        body, name="ffn_down_loss", grid=(nt,),
        in_specs=[pl.BlockSpec((tm, kf), lambda i: (i, 0)),
                  pl.BlockSpec((kf, D_MODEL), lambda i: (0, 0)),
                  row, pl.BlockSpec((1, D_MODEL), lambda i: (0, 0)), row],
        out_specs=[pl.BlockSpec((1, 1), lambda i: (0, 0)), row,
                   pl.BlockSpec((8, D_MODEL), lambda i: (0, 0))],
        out_shape=[jax.ShapeDtypeStruct((1, 1), F32), jax.ShapeDtypeStruct((m, D_MODEL), F32),
                   jax.ShapeDtypeStruct((8, D_MODEL), F32)],
        scratch=[pltpu.VMEM((1, D_MODEL), F32)],
    )(f, wd, x2, nfw, target)


def _row_tile(rows):
    return next((t for t in (256, 128, 64, 32, 16, 8) if rows % t == 0), rows)


def _adamw(w, gs, m, v, *, name):
    rows, cols = w.shape
    tr = _row_tile(rows)
    ng = len(gs)

    def body(w_ref, *rest):
        g_refs, (m_ref, v_ref, g_out, d_out, m_out, v_out) = rest[:ng], rest[ng:]
        g = g_refs[0][...]
        for r in g_refs[1:]:
            g = g + r[...]
        mn = ADAM_B1 * m_ref[...] + (1.0 - ADAM_B1) * g
        vn = ADAM_B2 * v_ref[...] + (1.0 - ADAM_B2) * (g * g)
        m_hat = mn / (1.0 - ADAM_B1 ** ADAM_STEP)
        v_hat = vn / (1.0 - ADAM_B2 ** ADAM_STEP)
        g_out[...] = g
        d_out[...] = -ADAM_LR * (m_hat / (jnp.sqrt(v_hat) + ADAM_EPS) + ADAM_WD * w_ref[...])
        m_out[...] = mn
        v_out[...] = vn

    spec = pl.BlockSpec((tr, cols), lambda i: (i, 0))
    return _call(
        body, name=name, grid=(rows // tr,),
        in_specs=[spec] * (3 + ng), out_specs=[spec] * 4,
        out_shape=[jax.ShapeDtypeStruct((rows, cols), F32)] * 4,
    )(w, *gs, m, v)


def _mesh_pos():
    x, y, c = lax.axis_index("x"), lax.axis_index("y"), lax.axis_index("c")
    return x, y, c


def _other_chips(x, y, c):
    return [((1 - x, y, c), 2 * (1 - x) + y), ((x, 1 - y, c), 2 * x + 1 - y),
            ((1 - x, 1 - y, c), 2 * (1 - x) + 1 - y)]


def _region(ref, axis, size, half_axis, chip, core=None):
    idx = [slice(None)] * len(ref.shape)
    if core is None:
        idx[axis] = pl.ds(pl.multiple_of(chip * size, size), size)
    elif half_axis == axis:
        h = size // 2
        idx[axis] = pl.ds(pl.multiple_of(chip * size + core * h, h), h)
    else:
        idx[axis] = pl.ds(pl.multiple_of(chip * size, size), size)
        h = ref.shape[half_axis] // 2
        idx[half_axis] = pl.ds(pl.multiple_of(core * h, h), h)
    return ref.at[tuple(idx)]


def _half(ref, half_axis, core):
    idx = [slice(None)] * len(ref.shape)
    h = ref.shape[half_axis] // 2
    idx[half_axis] = pl.ds(pl.multiple_of(core * h, h), h)
    return ref.at[tuple(idx)]


class _Copy:
    def __init__(self, make):
        self._make = make

    def start(self):
        self._make().start()

    def wait(self):
        self._make().wait()

    def wait_send(self):
        self._make().wait_send()

    def wait_recv(self):
        self._make().wait_recv()


def _remote(src, dst, send_sem, recv_sem, dev):
    return _Copy(lambda: pltpu.make_async_remote_copy(
        src_ref=src, dst_ref=dst, send_sem=send_sem, recv_sem=recv_sem, device_id=dev, device_id_type=MESH))


def _local(src, dst, sem):
    return _Copy(lambda: pltpu.make_async_copy(src, dst, sem))


def _dma_sems(n):
    return pltpu.SemaphoreType.DMA((n,))


def _place_shard(w, chip, axis, *, name):
    shape = list(w.shape)
    shape[axis] *= N_CHIPS
    if w.ndim == 3:
        block, grid = (1,) + w.shape[1:], (w.shape[0],)
        in_map, out_map = (lambda i, chip: (i, 0, 0)), (lambda i, chip: (i, chip[0], 0))
    else:
        tr = _row_tile(w.shape[0])
        nt = w.shape[0] // tr
        block, grid = (tr, w.shape[1]), (nt,)
        in_map = lambda i, chip: (i, 0)
        out_map = (lambda i, chip: (chip[0] * nt + i, 0)) if axis == 0 else (lambda i, chip: (i, chip[0]))

    def body(chip_ref, w_ref, o_ref):
        o_ref[...] = _bf(w_ref[...])

    return _call(body, name=name, grid=grid, prefetch=1, in_specs=[pl.BlockSpec(block, in_map)],
                 out_specs=pl.BlockSpec(block, out_map),
                 out_shape=jax.ShapeDtypeStruct(tuple(shape), BF16))(chip, w)


def _ici_leg(srcs, dsts, layout, sizes, n_whole, sems):
    send_sems, recv_sems, local_sems = sems
    x, y, c = _mesh_pos()
    mine = 2 * x + y
    n_big = len(srcs) - n_whole
    local, sends, recvs = [], [], []
    for t, (src, dst) in enumerate(zip(srcs, dsts)):
        if t < n_big:
            ax, hx = layout[t]
            part = _region(src, ax, sizes[t], hx, mine, c)
            landing = lambda chip, dst=dst, ax=ax, hx=hx, size=sizes[t]: _region(dst, ax, size, hx, chip, c)
        else:
            part, landing = src, (lambda chip, dst=dst: dst.at[chip])
            local.append(_local(src, dst.at[mine], local_sems.at[t - n_big]))
        for k, (dev, chip) in enumerate(_other_chips(x, y, c)):
            sends.append(_remote(part, landing(mine), send_sems.at[3 * t + k], recv_sems.at[3 * t + k], dev))
            recvs.append(_remote(part, landing(chip), send_sems.at[3 * t + k], recv_sems.at[3 * t + k], dev))
    return local, sends, recvs


def _d2d_leg(srcs, dsts, layout, sizes, sems):
    send_sems, recv_sems = sems
    x, y, c = _mesh_pos()
    sends, recvs = [], []
    for t, (src, dst) in enumerate(zip(srcs, dsts)):
        ax, hx = layout[t]
        for k, (_, chip) in enumerate(_other_chips(x, y, c)):
            sem = (send_sems.at[3 * t + k], recv_sems.at[3 * t + k])
            sends.append(_remote(_region(src, ax, sizes[t], hx, chip, c),
                                 _region(dst, ax, sizes[t], hx, chip, c), *sem, (x, y, 1 - c)))
            recvs.append(_remote(_region(src, ax, sizes[t], hx, chip, 1 - c),
                                 _region(dst, ax, sizes[t], hx, chip, 1 - c), *sem, (x, y, 1 - c)))
    return sends, recvs


def _gather_shapes(bufs, whole):
    return ([jax.ShapeDtypeStruct(b.shape, b.dtype) for b in bufs]
            + [jax.ShapeDtypeStruct((N_CHIPS,) + w.shape, w.dtype) for w in whole])


def _gather_d2d(bufs, layout, sizes):
    n = len(bufs)

    def start(ins, outs, sems):
        for cp in _d2d_leg(ins, outs, layout, sizes, sems)[0]:
            cp.start()

    def finish(ins, outs, sems):
        sends, recvs = _d2d_leg(ins, outs, layout, sizes, sems)
        for cp in recvs:
            cp.wait_recv()
        for cp in sends:
            cp.wait_send()

    return _Comm(bufs, [jax.ShapeDtypeStruct(b.shape, b.dtype) for b in bufs],
                 [_dma_sems(3 * n), _dma_sems(3 * n)], start, finish, aliases={i: i for i in range(n)})


def _norm_bf16(x, nw, *, name, tm):
    m, d = x.shape

    def body(x_ref, nw_ref, h_ref):
        h_ref[...] = _bf(_rms(x_ref[...])[0] * nw_ref[...])

    row = pl.BlockSpec((tm, d), lambda i: (i, 0))
    return _call(body, name=name, grid=(m // tm,), in_specs=[row, pl.BlockSpec((1, d), lambda i: (0, 0))],
                 out_specs=row, out_shape=jax.ShapeDtypeStruct((m, d), BF16))(x, nw)


def _in_proj_gather(h1, w_buf, later, later_cut, small, order, *, tm):
    m, d = h1.shape
    width = w_buf.shape[1] // N_CHIPS
    nr, nl = m // tm, len(later)
    sizes = [b.shape[ax] // N_CHIPS for b, (ax, _) in zip(later, later_cut)]

    def body(order_ref, h_ref, w_in, *rest):
        later_in, small_in = rest[:nl], rest[nl]
        o_ref, w_out = rest[nl + 1], rest[nl + 2]
        later_out, small_out = rest[nl + 3:2 * nl + 3], rest[2 * nl + 3]
        wv, load_sem, ici_send, ici_recv, d2d_send, d2d_recv, l_send, l_recv, l_local = rest[2 * nl + 4:]
        s, i = pl.program_id(0), pl.program_id(1)
        x, y, c = _mesh_pos()
        mine = 2 * x + y
        peers = _other_chips(x, y, c)
        part = lambda ref, chip, core=None: _region(ref, 1, width, 0, chip, core)

        def ici(k):
            dev, chip = peers[k]
            sem = (ici_send.at[k], ici_recv.at[k])
            return (_remote(part(w_in, mine, c), part(w_out, mine, c), *sem, dev),
                    _remote(part(w_in, chip, c), part(w_out, chip, c), *sem, dev))

        def d2d(k):
            chip, sem, sib = peers[k][1], (d2d_send.at[k], d2d_recv.at[k]), (x, y, 1 - c)
            return (_remote(part(w_out, chip, c), part(w_out, chip, c), *sem, sib),
                    _remote(part(w_out, chip, 1 - c), part(w_out, chip, 1 - c), *sem, sib))

        def load(src, chip, slot):
            cp = _local(part(src, chip), wv.at[slot], load_sem.at[slot])
            cp.start()
            cp.wait()

        def others():
            return _ici_leg(list(later_in) + [small_in], list(later_out) + [small_out], later_cut, sizes, 1,
                            (l_send, l_recv, l_local))

        @pl.when(jnp.logical_and(s == 0, i == 0))
        def _():
            for k in range(3):
                ici(k)[0].start()
            local, sends, _ = others()
            for cp in local + sends:
                cp.start()
            load(w_in, mine, 0)

        o_ref[...] = _bf(_dot(h_ref[...], wv[s % 2]))

        @pl.when(i == nr - 1)
        def _():
            for k in range(3):
                @pl.when(s == k)
                def _(k=k):
                    ici(k)[1].wait_recv()
                    d2d(k)[0].start()
                    d2d(k)[1].wait_recv()
                    load(w_out, peers[k][1], (k + 1) % 2)

            @pl.when(s == 3)
            def _():
                for k in range(3):
                    ici(k)[0].wait_send()
                    d2d(k)[0].wait_send()
                local, sends, recvs = others()
                for cp in recvs:
                    cp.wait_recv()
                for cp in sends:
                    cp.wait_send()
                for cp in local:
                    cp.wait()

    any_spec = pl.BlockSpec(memory_space=pl.ANY)
    n_any = nl + 2
    outs = _call(
        body, name="in_proj", grid=(N_CHIPS, nr), prefetch=1,
        in_specs=[pl.BlockSpec((tm, d), lambda s, i, order: (i, 0))] + [any_spec] * n_any,
        out_specs=[pl.BlockSpec((tm, width), lambda s, i, order: (i, order[s]))] + [any_spec] * n_any,
        out_shape=[jax.ShapeDtypeStruct((m, w_buf.shape[1]), BF16)] + _gather_shapes([w_buf] + list(later), [small]),
        scratch=[pltpu.VMEM((2, d, width), BF16), _dma_sems(2), _dma_sems(3), _dma_sems(3), _dma_sems(3),
                 _dma_sems(3), _dma_sems(3 * (nl + 1)), _dma_sems(3 * (nl + 1)), _dma_sems(1)],
        aliases={2 + t: 1 + t for t in range(nl + 1)},
    )(order, h1, w_buf, *later, small)
    return outs[0], outs[1], list(outs[2:2 + nl]), outs[2 + nl]


def _exchange(grads, layout):
    n = len(grads)
    others = N_DEV - 1
    sizes = [g.shape[ax] // N_CHIPS for g, (ax, _) in zip(grads, layout)]
    out_shapes = []
    for g, (ax, hx), sz in zip(grads, layout, sizes):
        shape = list(g.shape)
        shape[ax] = sz
        shape[hx] //= 2
        out_shapes.append(jax.ShapeDtypeStruct((others,) + tuple(shape), g.dtype))

    def copies(ins, outs, sems):
        send_sems, recv_sems = sems
        x, y, c = _mesh_pos()
        sends, recvs = [], []
        for t, (src, dst) in enumerate(zip(ins, outs)):
            ax, hx = layout[t]
            for r in range(1, N_DEV):
                px = (1 - x) if r & 4 else x
                py = (1 - y) if r & 2 else y
                pc = (1 - c) if r & 1 else c
                sem = (send_sems.at[others * t + r - 1], recv_sems.at[others * t + r - 1])
                part = _region(src, ax, sizes[t], hx, 2 * px + py, pc)
                sends.append(_remote(part, dst.at[r - 1], *sem, (px, py, pc)))
                recvs.append(_remote(part, dst.at[r - 1], *sem, (px, py, pc)))
        return sends, recvs

    def start(ins, outs, sems):
        for cp in copies(ins, outs, sems)[0]:
            cp.start()

    def finish(ins, outs, sems):
        sends, recvs = copies(ins, outs, sems)
        for cp in recvs:
            cp.wait_recv()
        for cp in sends:
            cp.wait_send()

    return _Comm(grads, out_shapes, [_dma_sems(others * n), _dma_sems(others * n)], start, finish)


def _reduce_half(g, parts, pos, cut, *, name):
    ax, _ = cut
    others = parts.shape[0]
    if g.ndim == 3:
        nb, rows, cols = g.shape
        hb = nb // 2
        block, grid, out_shape = (1, rows // N_CHIPS, cols), (hb,), (nb, rows // N_CHIPS, cols)
        g_map = lambda i, pos: (pos[1] * hb + i, pos[0], 0)
        o_map = lambda i, pos: (pos[1] * hb + i, 0, 0)
        p_map = lambda i, pos: (0, i, 0, 0)
    elif ax == 1:
        rows, cols = g.shape
        tr = _row_tile(rows // 2)
        nt = rows // 2 // tr
        block, grid, out_shape = (tr, cols // N_CHIPS), (nt,), (rows, cols // N_CHIPS)
        g_map = lambda i, pos: (pos[1] * nt + i, pos[0])
        o_map = lambda i, pos: (pos[1] * nt + i, 0)
        p_map = lambda i, pos: (0, i, 0)
    else:
        rows, cols = g.shape
        tr = _row_tile(rows // N_CHIPS // 2)
        nt = rows // N_CHIPS // 2 // tr
        block, grid, out_shape = (tr, cols), (nt,), (rows // N_CHIPS, cols)
        g_map = lambda i, pos: (pos[0] * 2 * nt + pos[1] * nt + i, 0)
        o_map = lambda i, pos: (pos[1] * nt + i, 0)
        p_map = lambda i, pos: (0, i, 0)

    def body(pos_ref, g_ref, p_ref, o_ref):
        acc = g_ref[...].astype(F32)
        for r in range(others):
            acc = acc + p_ref[r].astype(F32)
        o_ref[...] = acc

    return _call(
        body, name=name, grid=grid, prefetch=1,
        in_specs=[pl.BlockSpec(block, g_map), pl.BlockSpec((others,) + block, p_map)],
        out_specs=pl.BlockSpec(block, o_map), out_shape=jax.ShapeDtypeStruct(out_shape, F32),
    )(pos, g, parts)


def _join_halves(bufs):
    n = len(bufs)

    def copies(ins, outs, sems):
        send_sems, recv_sems = sems
        x, y, c = _mesh_pos()
        sends = [_remote(_half(src, 0, c), _half(dst, 0, c), send_sems.at[t], recv_sems.at[t], (x, y, 1 - c))
                 for t, (src, dst) in enumerate(zip(ins, outs))]
        recvs = [_remote(_half(src, 0, 1 - c), _half(dst, 0, 1 - c), send_sems.at[t], recv_sems.at[t],
                         (x, y, 1 - c)) for t, (src, dst) in enumerate(zip(ins, outs))]
        return sends, recvs

    def start(ins, outs, sems):
        for cp in copies(ins, outs, sems)[0]:
            cp.start()

    def finish(ins, outs, sems):
        sends, recvs = copies(ins, outs, sems)
        for cp in recvs:
            cp.wait_recv()
        for cp in sends:
            cp.wait_send()

    comm = _Comm(bufs, [jax.ShapeDtypeStruct(b.shape, b.dtype) for b in bufs],
                 [_dma_sems(n), _dma_sems(n)], start, finish, aliases={i: i for i in range(n)})
    return _call(None, name="join_halves", comm=comm)()[1]


def _allreduce_small(pack):
    rows, cols = pack.shape

    def body(p_ref, o_ref, slots, send_sems, recv_sems):
        x, y, c = _mesh_pos()
        me = 4 * x + 2 * y + c
        slots[me] = p_ref[...]
        copies = []
        for r in range(1, N_DEV):
            fx, fy, fc = (r >> 2) & 1, (r >> 1) & 1, r & 1
            dev = ((1 - x) if fx else x, (1 - y) if fy else y, (1 - c) if fc else c)
            cp = pltpu.make_async_remote_copy(
                src_ref=p_ref, dst_ref=slots.at[me], send_sem=send_sems.at[r - 1],
                recv_sem=recv_sems.at[r - 1], device_id=dev, device_id_type=MESH)
            cp.start()
            copies.append(cp)
        for cp in copies:
            cp.wait_recv()
        for cp in copies:
            cp.wait_send()
        acc = slots[0]
        for d in range(1, N_DEV):
            acc = acc + slots[d]
        o_ref[...] = acc

    vmem = pl.BlockSpec(memory_space=pltpu.VMEM)
    return _call(
        body, name="allreduce_small", in_specs=[vmem], out_specs=vmem,
        out_shape=jax.ShapeDtypeStruct((rows, cols), F32),
        scratch=[pltpu.VMEM((N_DEV, rows, cols), F32), pltpu.SemaphoreType.DMA((N_DEV - 1,)),
                 pltpu.SemaphoreType.DMA((N_DEV - 1,))],
    )(pack)


def _pad_rows(a, rows=8):
    return jnp.pad(a, ((0, rows - a.shape[0]), (0, 0)))


def kernel(x, positions, norm1_w, w_in, merge_gate_b, ret_gn_w, w_ret_o, lru_conv_w, lru_conv_b, lru_w_r, lru_b_r, lru_w_i, lru_b_i, lru_lambda, w_lru_o, w_out, norm2_w, ffn_w_up, ffn_conv_w, ffn_conv_b, ffn_w_down, norm_f_w, loss_target, m_norm1_w, m_w_in, m_merge_gate_b, m_ret_gn_w, m_w_ret_o, m_lru_conv_w, m_lru_conv_b, m_lru_w_r, m_lru_b_r, m_lru_w_i, m_lru_b_i, m_lru_lambda, m_w_lru_o, m_w_out, m_norm2_w, m_ffn_w_up, m_ffn_conv_w, m_ffn_conv_b, m_ffn_w_down, m_norm_f_w, v_norm1_w, v_w_in, v_merge_gate_b, v_ret_gn_w, v_w_ret_o, v_lru_conv_w, v_lru_conv_b, v_lru_w_r, v_lru_b_r, v_lru_w_i, v_lru_b_i, v_lru_lambda, v_w_lru_o, v_w_out, v_norm2_w, v_ffn_w_up, v_ffn_conv_w, v_ffn_conv_b, v_ffn_w_down, v_norm_f_w):
    names = ["norm1_w", "w_in", "merge_gate_b", "ret_gn_w", "w_ret_o", "lru_conv_w", "lru_conv_b", "lru_w_r",
             "lru_b_r", "lru_w_i", "lru_b_i", "lru_lambda", "w_lru_o", "w_out", "norm2_w", "ffn_w_up",
             "ffn_conv_w", "ffn_conv_b", "ffn_w_down", "norm_f_w"]
    w_args = dict(zip(names, [norm1_w, w_in, merge_gate_b, ret_gn_w, w_ret_o, lru_conv_w, lru_conv_b, lru_w_r,
                              lru_b_r, lru_w_i, lru_b_i, lru_lambda, w_lru_o, w_out, norm2_w, ffn_w_up,
                              ffn_conv_w, ffn_conv_b, ffn_w_down, norm_f_w]))
    m_args = dict(zip(names, [m_norm1_w, m_w_in, m_merge_gate_b, m_ret_gn_w, m_w_ret_o, m_lru_conv_w,
                              m_lru_conv_b, m_lru_w_r, m_lru_b_r, m_lru_w_i, m_lru_b_i, m_lru_lambda, m_w_lru_o,
                              m_w_out, m_norm2_w, m_ffn_w_up, m_ffn_conv_w, m_ffn_conv_b, m_ffn_w_down,
                              m_norm_f_w]))
    v_args = dict(zip(names, [v_norm1_w, v_w_in, v_merge_gate_b, v_ret_gn_w, v_w_ret_o, v_lru_conv_w,
                              v_lru_conv_b, v_lru_w_r, v_lru_b_r, v_lru_w_i, v_lru_b_i, v_lru_lambda, v_w_lru_o,
                              v_w_out, v_norm2_w, v_ffn_w_up, v_ffn_conv_w, v_ffn_conv_b, v_ffn_w_down,
                              v_norm_f_w]))

    bsz, seq, d = x.shape
    m = bsz * seq
    tm = min(MM_ROWS, m)
    tm_fused = min(FUSED_ROWS, m)
    tm_tall = min(TALL_ROWS, m)
    chip = 2 * lax.axis_index("x") + lax.axis_index("y")

    big = ["w_in", "w_ret_o", "w_lru_o", "w_out", "lru_w_r", "lru_w_i", "ffn_w_up", "ffn_w_down"]
    cut = dict(w_in=(1, 0), w_ret_o=(0, 0), w_lru_o=(0, 0), w_out=(0, 0), lru_w_r=(1, 0), lru_w_i=(1, 0),
               ffn_w_up=(1, 0), ffn_w_down=(0, 0))
    later = big[1:]
    core = lax.axis_index("c")
    chip1 = jnp.reshape(chip, (1,)).astype(jnp.int32)
    pos = jnp.stack([chip, core]).astype(jnp.int32)
    placed = {n: _place_shard(w_args[n][0], chip1, cut[n][0], name="place_" + n) for n in big}
    small_pack = jnp.concatenate([
        jnp.pad(merge_gate_b[0], ((0, 6), (0, 512))),
        jnp.pad(lru_conv_w[0], ((0, 4), (0, 512))),
        jnp.pad(lru_b_r[0], ((0, 4), (0, 704))),
        jnp.pad(lru_b_i[0], ((0, 4), (0, 704))),
        jnp.pad(ffn_conv_w[0], ((0, 5), (0, 0))),
    ], axis=0)
    x2d = x.reshape(m, d)
    mx, my = lax.axis_index("x"), lax.axis_index("y")
    order = jnp.stack([chip, 2 * (1 - mx) + my, 2 * mx + 1 - my, 2 * (1 - mx) + 1 - my]).astype(jnp.int32)
    later_cut = [cut[n] for n in later]
    h1 = _norm_bf16(x2d, norm1_w, name="norm1", tm=tm)
    proj, w_in_full, bufs, sp = _in_proj_gather(h1, placed["w_in"], [placed[n] for n in later], later_cut,
                                               small_pack, order, tm=tm_tall)
    wb = {"w_in": w_in_full}
    mgb = jnp.transpose(sp[:, 0:2, 0:256], (1, 0, 2)).reshape(2, D_MODEL)
    lcw = jnp.transpose(sp[:, 8:12, 0:256], (1, 0, 2)).reshape(4, D_MODEL)
    lbr = jnp.transpose(sp[:, 16:20, 0:64], (1, 0, 2)).reshape(1, D_MODEL)
    lbi = jnp.transpose(sp[:, 24:28, 0:64], (1, 0, 2)).reshape(1, D_MODEL)
    fcw = jnp.transpose(sp[:, 32:35, :], (1, 0, 2)).reshape(3, D_FF)
    nfw = norm_f_w.reshape(1, D_MODEL)

    half = RET_DK // 2
    inv_freq = ROPE_BASE ** (-jnp.arange(half, dtype=F32) / half)
    cos, sin = _rope_tables(positions.reshape(bsz, seq, 1), jnp.concatenate([inv_freq, inv_freq]).reshape(1, RET_DK))
    proj3 = proj.reshape(bsz, seq, D_IN)
    (a_in3, states), bufs = _retention_fwd(
        proj3, cos, sin, ret_gn_w,
        comm=_gather_d2d(bufs, later_cut, [w_args[n].shape[1 + cut[n][0]] for n in later]))
    wb.update(zip(later, bufs))
    lru_params = (lcw, lru_conv_b, wb["lru_w_r"], lbr, wb["lru_w_i"], lbi, lru_lambda)
    b_in3, *lru_kept = _lru_fwd(proj3, lru_params)
    a_in, b_in = a_in3.reshape(m, d), b_in3.reshape(m, d)
    x2, mix, ya, yb = _merge_fwd(a_in, b_in, proj, mgb, wb["w_ret_o"], wb["w_lru_o"], wb["w_out"], x2d, tm=tm_fused)
    up, h2 = _norm_matmul(x2, norm2_w, wb["ffn_w_up"], name="ffn_up", tm=tm_tall, tn=TALL_COLS)
    up3 = up.reshape(bsz, seq, 2 * D_FF)
    f3, act3, q3 = _ffn_act_fwd(up3, fcw, ffn_conv_b)
    f = f3.reshape(m, D_FF)
    loss_dev, dx3, sm_nf = _ffn_down_loss(f, wb["ffn_w_down"], x2, nfw, loss_target.reshape(m, d), tm=tm)
    loss = lax.psum(loss_dev[0, 0], ("x", "y", "c"))

    def send(*ns):
        return _exchange([g_full[n] for n in ns], [cut[n] for n in ns])

    g_full, parts = {}, {}
    df = _mm_nt(dx3, wb["ffn_w_down"], name="ffn_down_dx", tm=tm_tall, out_dtype=BF16)
    g_full["ffn_w_down"] = _mm_tn(f, [dx3], name="ffn_down_dw", tm=tm)
    (dgate3, dval3, sm_ffn), (parts["ffn_w_down"],) = _ffn_act_bwd(
        up3, act3, q3, fcw, df.reshape(bsz, seq, D_FF), comm=send("ffn_w_down"))
    dup = [dgate3.reshape(m, D_FF), dval3.reshape(m, D_FF)]
    g_full["ffn_w_up"] = _mm_tn(h2, dup, name="ffn_up_dw", tm=tm)
    (dx2, sm_n2), (parts["ffn_w_up"],) = _mm_nt_normbwd(
        dup, wb["ffn_w_up"], x2, norm2_w, dx3, name="ffn_up_dx", tm=tm, row=ROW_N2, comm=send("ffn_w_up"))
    dya, dyb, da_in, db_in, dgates, sm_mg = _merge_bwd(
        ya, yb, proj, mgb, wb["w_ret_o"], wb["w_lru_o"], wb["w_out"], dx2, tm=tm_fused)
    g_full["w_out"] = _mm_tn(mix, [dx2], name="out_dw", tm=tm)
    g_full["w_ret_o"] = _mm_tn(a_in, [dya], name="ret_o_dw", tm=tm)
    g_full["w_lru_o"] = _mm_tn(b_in, [dyb], name="lru_o_dw", tm=tm)
    (dlru3, dwr, dwi, sm_lru), (parts["w_out"], parts["w_ret_o"], parts["w_lru_o"]) = _lru_bwd(
        proj3, lru_params, lru_kept, db_in.reshape(bsz, seq, d), comm=send("w_out", "w_ret_o", "w_lru_o"))
    g_full["lru_w_r"], g_full["lru_w_i"] = dwr.astype(BF16), dwi.astype(BF16)
    (dret3, sm_gn), (parts["lru_w_r"], parts["lru_w_i"]) = _retention_bwd(
        proj3, cos, sin, ret_gn_w, states, da_in.reshape(bsz, seq, d), comm=send("lru_w_r", "lru_w_i"))
    dproj = [dret3.reshape(m, 3072), dlru3.reshape(m, 2048), dgates]
    g_full["w_in"] = _mm_tn(h1, dproj, name="in_proj_dw", tm=tm)
    (grad_x, sm_n1), (parts["w_in"],) = _mm_nt_normbwd(
        dproj, wb["w_in"], x2d, norm1_w, dx2, name="in_proj_dx", tm=tm, row=ROW_N1, comm=send("w_in"))

    reduced = _join_halves([_reduce_half(g_full[n], parts[n], pos, cut[n], name="sum_" + n) for n in big])
    misc = sm_n1 + sm_mg + sm_gn + sm_n2 + sm_nf
    pack = jnp.concatenate(
        [misc, sm_lru, sm_ffn[:, 0:1024], sm_ffn[:, 1024:2048], sm_ffn[:, 2048:3072]], axis=0)
    tot = _allreduce_small(pack)
    ffn_sm = jnp.concatenate([tot[16:24], tot[24:32], tot[32:40]], axis=1)
    g_small = {
        "norm1_w": tot[ROW_N1:ROW_N1 + 1], "merge_gate_b": tot[ROW_MGB:ROW_MGB + 2],
        "ret_gn_w": tot[ROW_GN:ROW_GN + 1], "norm2_w": tot[ROW_N2:ROW_N2 + 1], "norm_f_w": tot[ROW_NF:ROW_NF + 1],
        "lru_conv_w": tot[8:12], "lru_conv_b": tot[12:13], "lru_b_r": tot[13:14].reshape(4, 256),
        "lru_b_i": tot[14:15].reshape(4, 256), "lru_lambda": tot[15:16],
        "ffn_conv_w": ffn_sm[0:3], "ffn_conv_b": ffn_sm[3:4],
    }
    small_shard = dict(merge_gate_b=256, lru_conv_w=256, lru_b_r=64, lru_b_i=64, ffn_conv_w=768)

    outs = {}
    for n, g in zip(big, reduced):
        shape = w_args[n].shape
        g = g.reshape(-1, g.shape[-1])
        outs[n] = [o.reshape(shape) for o in _adamw(
            w_args[n].reshape(g.shape), [g], m_args[n].reshape(g.shape), v_args[n].reshape(g.shape),
            name="adamw_" + n)]
    for n, g in g_small.items():
        shape = w_args[n].shape
        if n in small_shard:
            g = lax.dynamic_slice_in_dim(g, chip * small_shard[n], small_shard[n], axis=1)
        w2 = w_args[n].reshape(g.shape)
        outs[n] = [o.reshape(shape) for o in _adamw(
            w2, [g], m_args[n].reshape(g.shape), v_args[n].reshape(g.shape), name="adamw_" + n)]

    result = [loss, grad_x.reshape(bsz, seq, d)]
    for k in range(4):
        result += [outs[n][k] for n in names]
    return tuple(result)
```

```python
import functools
import math

import numpy as np
import jax
import jax.numpy as jnp
from jax import lax
from jax.experimental import pallas as pl
from jax.experimental.pallas import tpu as pltpu

F32 = jnp.float32
BF16 = jnp.bfloat16

D_MODEL = 1024
RET_HEADS = 4
RET_DK = 128
RET_DV = 256
LRU_BLOCKS = 4
LRU_BLOCK = 256
LRU_C = 8.0
D_FF = 3072
D_IN = 7168
ROPE_BASE = 10000.0
RMS_EPS = 1e-6
GN_EPS = 1e-6
ADAM_LR, ADAM_B1, ADAM_B2, ADAM_EPS, ADAM_WD, ADAM_STEP = 0.001, 0.9, 0.999, 1e-08, 0.01, 10

N_CHIPS = 4
N_DEV = 8
SEQ_T = 256
REF_CHUNK = 64
COL = 1024
MM_ROWS = 1024
TALL_ROWS, TALL_COLS = 2048, 1024
FUSED_ROWS = 512
VMEM_LIMIT_BYTES = 56 * 1024 * 1024
MESH = pl.DeviceIdType.MESH
ROW_N1, ROW_MGB, ROW_GN, ROW_N2, ROW_NF = 0, 1, 3, 4, 5
GELU_K = math.sqrt(2.0 / math.pi)
GELU_C = 0.044715


class _Comm:
    def __init__(self, ins, outs, sems, start, finish, aliases=None):
        self.ins, self.outs, self.sems = list(ins), list(outs), list(sems)
        self.start, self.finish, self.aliases = start, finish, dict(aliases or {})


def _call(body, *, name, out_shape=(), grid=None, in_specs=(), out_specs=(), scratch=(), comm=None, prefetch=0,
          aliases=None):
    single = not isinstance(out_shape, (list, tuple))
    out_shape = [out_shape] if single else list(out_shape)
    out_specs = [out_specs] if single else list(out_specs)
    in_specs, scratch = list(in_specs), list(scratch)
    n_in, n_out, n_scr = len(in_specs), len(out_shape), len(scratch)
    kwargs = dict(name=name, compiler_params=pltpu.CompilerParams(vmem_limit_bytes=VMEM_LIMIT_BYTES))
    if prefetch:
        assert comm is None
        spec = pltpu.PrefetchScalarGridSpec(num_scalar_prefetch=prefetch, grid=grid, in_specs=in_specs,
                                            out_specs=out_specs, scratch_shapes=scratch)
        fn = pl.pallas_call(body, out_shape=out_shape, grid_spec=spec, input_output_aliases=dict(aliases or {}),
                            **kwargs)
        return (lambda *args: fn(*args)[0]) if single else fn
    if grid is not None:
        kwargs["grid"] = grid
    if comm is None:
        fn = pl.pallas_call(body, out_shape=out_shape, in_specs=in_specs, out_specs=out_specs,
                            scratch_shapes=scratch, **kwargs)
        return (lambda *args: fn(*args)[0]) if single else fn

    any_spec = pl.BlockSpec(memory_space=pl.ANY)
    n_cin, n_cout = len(comm.ins), len(comm.outs)

    def wrapped(*refs):
        ins, refs = refs[:n_in], refs[n_in:]
        cins, refs = refs[:n_cin], refs[n_cin:]
        outs, refs = refs[:n_out], refs[n_out:]
        couts, refs = refs[:n_cout], refs[n_cout:]
        scr, csems = refs[:n_scr], refs[n_scr:]
        if grid is None:
            comm.start(cins, couts, csems)
            comm.finish(cins, couts, csems)
            return
        ids = [pl.program_id(a) for a in range(len(grid))]
        first = functools.reduce(jnp.logical_and, [i == 0 for i in ids])
        last = functools.reduce(jnp.logical_and, [i == g - 1 for i, g in zip(ids, grid)])
        pl.when(first)(lambda: comm.start(cins, couts, csems))
        body(*ins, *outs, *scr)
        pl.when(last)(lambda: comm.finish(cins, couts, csems))

    fn = pl.pallas_call(
        wrapped, out_shape=out_shape + comm.outs, in_specs=in_specs + [any_spec] * n_cin,
        out_specs=out_specs + [any_spec] * n_cout, scratch_shapes=scratch + comm.sems,
        input_output_aliases={n_in + i: n_out + o for i, o in comm.aliases.items()}, **kwargs)

    def run(*args):
        res = fn(*args, *comm.ins)
        own = res[0] if single else list(res[:n_out])
        return own, list(res[n_out:])

    return run


def _dot(a, b):
    return jnp.dot(a, b, preferred_element_type=F32)


def _dot_nt(a, b):
    return lax.dot_general(a, b, (((1,), (1,)), ((), ())), preferred_element_type=F32)


def _dot_tn(a, b):
    return lax.dot_general(a, b, (((0,), (0,)), ((), ())), preferred_element_type=F32)


def _bf(x):
    return x.astype(BF16)


def _sigmoid(x):
    return 1.0 / (1.0 + jnp.exp(-x))


def _gelu(x):
    return 0.5 * x * (1.0 + jnp.tanh(GELU_K * (x + GELU_C * x * x * x)))


def _gelu_and_grad(x):
    x2 = x * x
    t = jnp.tanh(x * (GELU_K * GELU_C * x2 + GELU_K))
    hx = 0.5 * x
    g = hx + hx * t
    dg = 0.5 + 0.5 * t + hx * (1.0 - t * t) * (3.0 * GELU_K * GELU_C * x2 + GELU_K)
    return g, dg


def _rms(x):
    r = lax.rsqrt(jnp.mean(x * x, axis=-1, keepdims=True) + RMS_EPS)
    return x * r, r


def _rms_bwd(dy, x, nw):
    xh, r = _rms(x)
    g = dy * nw
    dx = r * (g - xh * jnp.mean(g * xh, axis=-1, keepdims=True))
    return dx, jnp.sum(dy * xh, axis=0, keepdims=True)


def _row_acc(ref, row, val):
    ref[row:row + 1, :] = ref[row:row + 1, :] + val


def _shift_down(x, j, prev8):
    if j == 0:
        return x
    n = x.shape[0] // 8
    row = lax.broadcasted_iota(jnp.int32, prev8.shape, 0)
    turned = [pltpu.roll(prev8, j, 0)] + [pltpu.roll(x[8 * k:8 * k + 8], j, 0) for k in range(n)]
    return jnp.concatenate([jnp.where(row < j, turned[k], turned[k + 1]) for k in range(n)], axis=0)


def _shift_up(x, j, next8):
    if j == 0:
        return x
    n = x.shape[0] // 8
    row = lax.broadcasted_iota(jnp.int32, next8.shape, 0)
    turned = [pltpu.roll(x[8 * k:8 * k + 8], 8 - j, 0) for k in range(n)] + [pltpu.roll(next8, 8 - j, 0)]
    return jnp.concatenate([jnp.where(row >= 8 - j, turned[k + 1], turned[k]) for k in range(n)], axis=0)


def _scan_fwd(a, b, carry):
    row = lax.broadcasted_iota(jnp.int32, (8, a.shape[1]), 0)
    out = []
    for k in range(a.shape[0] // 8):
        ak, bk = a[8 * k:8 * k + 8], b[8 * k:8 * k + 8]
        for s in (1, 2, 4):
            keep = row >= s
            ar, br = pltpu.roll(ak, s, 0), pltpu.roll(bk, s, 0)
            bk = jnp.where(keep, ak * br + bk, bk)
            ak = jnp.where(keep, ak * ar, ak)
        hk = ak * carry + bk
        carry = hk[7:8]
        out.append(hk)
    return jnp.concatenate(out, axis=0)


def _scan_bwd(a, b, carry):
    row = lax.broadcasted_iota(jnp.int32, (8, a.shape[1]), 0)
    out = []
    for k in reversed(range(a.shape[0] // 8)):
        ak, bk = a[8 * k:8 * k + 8], b[8 * k:8 * k + 8]
        for s in (1, 2, 4):
            keep = row < 8 - s
            ar, br = pltpu.roll(ak, 8 - s, 0), pltpu.roll(bk, 8 - s, 0)
            bk = jnp.where(keep, ak * br + bk, bk)
            ak = jnp.where(keep, ak * ar, ak)
        gk = bk + ak * carry
        carry = gk[0:1]
        out.append(gk)
    return jnp.concatenate(out[::-1], axis=0)


def _norm_matmul(x, nw, w, *, name, tm, tn):
    m, d = x.shape
    n = w.shape[1]

    def body(x_ref, nw_ref, w_ref, o_ref, h_ref, h_sc):
        @pl.when(pl.program_id(1) == 0)
        def _():
            xh, _ = _rms(x_ref[...])
            h = _bf(xh * nw_ref[...])
            h_sc[...] = h
            h_ref[...] = h

        o_ref[...] = _bf(_dot(h_sc[...], w_ref[...]))

    return _call(
        body, name=name, grid=(m // tm, n // tn),
        in_specs=[pl.BlockSpec((tm, d), lambda i, j: (i, 0)),
                  pl.BlockSpec((1, d), lambda i, j: (0, 0)),
                  pl.BlockSpec((d, tn), lambda i, j: (0, j))],
        out_specs=[pl.BlockSpec((tm, tn), lambda i, j: (i, j)),
                   pl.BlockSpec((tm, d), lambda i, j: (i, 0))],
        out_shape=[jax.ShapeDtypeStruct((m, n), BF16), jax.ShapeDtypeStruct((m, d), BF16)],
        scratch=[pltpu.VMEM((tm, d), BF16)],
    )(x, nw, w)


def _mm_nt(a, w, *, name, tm, out_dtype):
    m, k = a.shape
    n = w.shape[0]

    def body(a_ref, w_ref, o_ref):
        o_ref[...] = _dot_nt(_bf(a_ref[...]), w_ref[...]).astype(out_dtype)

    return _call(
        body, name=name, grid=(m // tm, n // COL),
        in_specs=[pl.BlockSpec((tm, k), lambda i, j: (i, 0)),
                  pl.BlockSpec((COL, k), lambda i, j: (j, 0))],
        out_specs=pl.BlockSpec((tm, COL), lambda i, j: (i, j)),
        out_shape=jax.ShapeDtypeStruct((m, n), out_dtype),
    )(a, w)


def _piece_layout(pieces):
    offs, nblk, o = [], [], 0
    for p in pieces:
        offs.append(o)
        nblk.append(p.shape[1] // COL)
        o += p.shape[1] // COL
    return offs, nblk, o


def _mm_tn(a, pieces, *, name, tm, out_dtype=BF16):
    m, k = a.shape
    offs, nblk, nn = _piece_layout(pieces)

    def piece_spec(o, nb):
        def idx(ki, nj, mi):
            use = jnp.logical_and(nj >= o, nj < o + nb)
            return (jnp.where(use, mi, 0), jnp.clip(nj - o, 0, nb - 1))
        return pl.BlockSpec((tm, COL), idx)

    def body(a_ref, *rest):
        p_refs, o_ref, acc = rest[:len(pieces)], rest[len(pieces)], rest[len(pieces) + 1]
        nj, mi = pl.program_id(1), pl.program_id(2)

        @pl.when(mi == 0)
        def _():
            acc[...] = jnp.zeros_like(acc)

        for p_ref, o, nb in zip(p_refs, offs, nblk):
            @pl.when(jnp.logical_and(nj >= o, nj < o + nb))
            def _(p_ref=p_ref):
                acc[...] += _dot_tn(_bf(a_ref[...]), _bf(p_ref[...]))

        @pl.when(mi == pl.num_programs(2) - 1)
        def _():
            o_ref[...] = acc[...].astype(out_dtype)

    return _call(
        body, name=name, grid=(k // COL, nn, m // tm),
        in_specs=[pl.BlockSpec((tm, COL), lambda ki, nj, mi: (mi, ki))]
        + [piece_spec(o, nb) for o, nb in zip(offs, nblk)],
        out_specs=pl.BlockSpec((COL, COL), lambda ki, nj, mi: (ki, nj)),
        out_shape=jax.ShapeDtypeStruct((k, nn * COL), out_dtype),
        scratch=[pltpu.VMEM((COL, COL), F32)],
    )(a, *pieces)


def _mm_nt_normbwd(pieces, w, x, nw, dres, *, name, tm, row, comm=None):
    m, d = x.shape
    offs, nblk, nk = _piece_layout(pieces)

    def piece_spec(o, nb):
        return pl.BlockSpec((tm, COL), lambda i, k: (i, jnp.clip(k - o, 0, nb - 1)))

    def body(*refs):
        p_refs = refs[:len(pieces)]
        w_ref, x_ref, nw_ref, dres_ref, dx_ref, dnw_ref, acc = refs[len(pieces):]
        i, k = pl.program_id(0), pl.program_id(1)

        @pl.when(jnp.logical_and(i == 0, k == 0))
        def _():
            dnw_ref[...] = jnp.zeros_like(dnw_ref)

        @pl.when(k == 0)
        def _():
            acc[...] = jnp.zeros_like(acc)

        for p_ref, o, nb in zip(p_refs, offs, nblk):
            @pl.when(jnp.logical_and(k >= o, k < o + nb))
            def _(p_ref=p_ref):
                acc[...] += _dot_nt(_bf(p_ref[...]), w_ref[...])

        @pl.when(k == nk - 1)
        def _():
            dx, dnw = _rms_bwd(acc[...], x_ref[...], nw_ref[...])
            dx_ref[...] = dres_ref[...] + dx
            _row_acc(dnw_ref, row, dnw)

    return _call(
        body, name=name, grid=(m // tm, nk), comm=comm,
        in_specs=[piece_spec(o, nb) for o, nb in zip(offs, nblk)]
        + [pl.BlockSpec((d, COL), lambda i, k: (0, k)),
           pl.BlockSpec((tm, d), lambda i, k: (i, 0)),
           pl.BlockSpec((1, d), lambda i, k: (0, 0)),
           pl.BlockSpec((tm, d), lambda i, k: (i, 0))],
        out_specs=[pl.BlockSpec((tm, d), lambda i, k: (i, 0)),
                   pl.BlockSpec((8, d), lambda i, k: (0, 0))],
        out_shape=[jax.ShapeDtypeStruct((m, d), F32), jax.ShapeDtypeStruct((8, d), F32)],
        scratch=[pltpu.VMEM((tm, d), F32)],
    )(*pieces, w, x, nw, dres)


def _rope_tables(pos3, invf):
    b, s, _ = pos3.shape

    def body(pos_ref, invf_ref, cos_ref, sin_ref):
        ang = pos_ref[...].astype(F32) * invf_ref[...]
        lane = lax.broadcasted_iota(jnp.int32, ang.shape, 1)
        cos_ref[...] = jnp.cos(ang)
        sin_ref[...] = jnp.where(lane < RET_DK // 2, -1.0, 1.0) * jnp.sin(ang)

    spec = pl.BlockSpec((None, SEQ_T, RET_DK), lambda i, c: (i, c, 0))
    return _call(
        body, name="rope_tables", grid=(b, s // SEQ_T),
        in_specs=[pl.BlockSpec((None, SEQ_T, 1), lambda i, c: (i, c, 0)),
                  pl.BlockSpec((1, RET_DK), lambda i, c: (0, 0))],
        out_specs=[spec, spec],
        out_shape=[jax.ShapeDtypeStruct((b, s, RET_DK), F32)] * 2,
    )(pos3, invf)


def _log_gamma(h):
    return float(np.log1p(-np.power(np.float32(2.0), np.float32(-5.0 - h))).astype(np.float32))


def _decay_matrix(h):
    lg = _log_gamma(h)
    n = lax.broadcasted_iota(jnp.int32, (SEQ_T, SEQ_T), 0)
    m = lax.broadcasted_iota(jnp.int32, (SEQ_T, SEQ_T), 1)
    same = (n // REF_CHUNK) == (m // REF_CHUNK)
    dist = jnp.where(same, jnp.abs(n - m), n - m).astype(F32)
    return jnp.where(jnp.logical_or(same, m < n), jnp.exp(lg * dist), 0.0)


def _decay_vectors(h):
    lg = _log_gamma(h)
    idx = lax.broadcasted_iota(jnp.int32, (SEQ_T, 1), 0).astype(F32)
    qd = jnp.exp(lg * (idx + 1.0))
    kd = jnp.exp(lg * (SEQ_T - 1.0 - idx))
    return qd, kd, math.exp(lg * SEQ_T)


def _rotate(x, cos, sin):
    return x * cos + pltpu.roll(x, RET_DK // 2, 1) * sin


def _rotate_bwd(d, cos, sin):
    return d * cos + pltpu.roll(d * sin, RET_DK // 2, 1)


def _ret_head(p_ref, cos, sin, h):
    q = p_ref[:, h * RET_DK:(h + 1) * RET_DK].astype(F32)
    k = p_ref[:, 512 + h * RET_DK:512 + (h + 1) * RET_DK].astype(F32)
    v = p_ref[:, 1024 + h * RET_DV:1024 + (h + 1) * RET_DV]
    g = p_ref[:, 2048 + h * RET_DV:2048 + (h + 1) * RET_DV].astype(F32)
    qr = _rotate(q, cos, sin)
    kr = _rotate(k, cos, sin) * (RET_DK ** -0.5)
    return qr, kr, v, g


def _group_norm(o):
    mu = jnp.mean(o, axis=-1, keepdims=True)
    oc = o - mu
    rstd = lax.rsqrt(jnp.mean(oc * oc, axis=-1, keepdims=True) + GN_EPS)
    return oc * rstd, rstd


def _retention_fwd(proj3, cos, sin, gnw, comm=None):
    b, s, _ = proj3.shape
    nc = s // SEQ_T

    def body(p_ref, cos_ref, sin_ref, gnw_ref, a_ref, st_ref, state, wtab):
        c = pl.program_id(1)

        @pl.when(jnp.logical_and(pl.program_id(0) == 0, c == 0))
        def _():
            for h in range(RET_HEADS):
                wtab[h] = _decay_matrix(h)

        @pl.when(c == 0)
        def _():
            state[...] = jnp.zeros_like(state)

        cs, sn = cos_ref[...], sin_ref[...]
        for h in range(RET_HEADS):
            qd, kd, gt = _decay_vectors(h)
            qr, kr, v, g = _ret_head(p_ref, cs, sn, h)
            st = state[h]
            st_ref[h] = st
            p = _dot_nt(_bf(qr), _bf(kr)) * wtab[h]
            o = _dot(_bf(p), _bf(v)) + _dot(_bf(qr * qd), _bf(st))
            state[h] = st * gt + _dot_tn(_bf(kr * kd), _bf(v))
            on, _ = _group_norm(o)
            gw = gnw_ref[:, h * RET_DV:(h + 1) * RET_DV]
            a_ref[:, h * RET_DV:(h + 1) * RET_DV] = _bf(on * gw * (g * _sigmoid(g)))

    tab = pl.BlockSpec((None, SEQ_T, RET_DK), lambda i, c: (i, c, 0))
    return _call(
        body, name="retention_fwd", grid=(b, nc), comm=comm,
        in_specs=[pl.BlockSpec((None, SEQ_T, 3072), lambda i, c: (i, c, 0)), tab, tab,
                  pl.BlockSpec((1, D_MODEL), lambda i, c: (0, 0))],
        out_specs=[pl.BlockSpec((None, SEQ_T, D_MODEL), lambda i, c: (i, c, 0)),
                   pl.BlockSpec((None, None, RET_HEADS, RET_DK, RET_DV), lambda i, c: (i, c, 0, 0, 0))],
        out_shape=[jax.ShapeDtypeStruct((b, s, D_MODEL), BF16),
                   jax.ShapeDtypeStruct((b, nc, RET_HEADS, RET_DK, RET_DV), F32)],
        scratch=[pltpu.VMEM((RET_HEADS, RET_DK, RET_DV), F32),
                 pltpu.VMEM((RET_HEADS, SEQ_T, SEQ_T), F32)],
    )(proj3, cos, sin, gnw)


def _retention_bwd(proj3, cos, sin, gnw, states, da3, comm=None):
    b, s, _ = proj3.shape
    nc = s // SEQ_T

    def body(p_ref, cos_ref, sin_ref, gnw_ref, st_ref, da_ref, d_ref, dgn_ref, dstate, wtab):
        c = pl.program_id(1)

        @pl.when(jnp.logical_and(pl.program_id(0) == 0, c == 0))
        def _():
            dgn_ref[...] = jnp.zeros_like(dgn_ref)
            for h in range(RET_HEADS):
                wtab[h] = _decay_matrix(h)

        @pl.when(c == 0)
        def _():
            dstate[...] = jnp.zeros_like(dstate)

        cs, sn = cos_ref[...], sin_ref[...]
        for h in range(RET_HEADS):
            qd, kd, gt = _decay_vectors(h)
            qr, kr, v, g = _ret_head(p_ref, cs, sn, h)
            st, dst, w = st_ref[h], dstate[h], wtab[h]
            qb, kb, vb = _bf(qr), _bf(kr), _bf(v)
            p = _dot_nt(qb, kb) * w
            o = _dot(_bf(p), vb) + _dot(_bf(qr * qd), _bf(st))
            on, rstd = _group_norm(o)
            gw = gnw_ref[:, h * RET_DV:(h + 1) * RET_DV]
            da = da_ref[:, h * RET_DV:(h + 1) * RET_DV].astype(F32)
            sg = _sigmoid(g)
            silu = g * sg
            dg = da * on * gw * (sg * (1.0 + g * (1.0 - sg)))
            dgn_ref[ROW_GN:ROW_GN + 1, h * RET_DV:(h + 1) * RET_DV] += jnp.sum(da * silu * on, axis=0, keepdims=True)
            don = da * silu * gw
            do = rstd * (don - jnp.mean(don, axis=-1, keepdims=True)
                         - on * jnp.mean(don * on, axis=-1, keepdims=True))
            dob = _bf(do)
            dp = _dot_nt(dob, vb) * w
            dqr = _dot(_bf(dp), kb) + _dot_nt(dob, _bf(st)) * qd
            dkr = _dot_tn(_bf(dp), qb) + _dot_nt(vb, _bf(dst)) * kd
            dv = _dot_tn(_bf(p), dob) + _dot(_bf(kr * kd), _bf(dst))
            dstate[h] = dst * gt + _dot_tn(_bf(qr * qd), dob)
            d_ref[:, h * RET_DK:(h + 1) * RET_DK] = _bf(_rotate_bwd(dqr, cs, sn))
            d_ref[:, 512 + h * RET_DK:512 + (h + 1) * RET_DK] = _bf(_rotate_bwd(dkr, cs, sn) * (RET_DK ** -0.5))
            d_ref[:, 1024 + h * RET_DV:1024 + (h + 1) * RET_DV] = _bf(dv)
            d_ref[:, 2048 + h * RET_DV:2048 + (h + 1) * RET_DV] = _bf(dg)

    rev = lambda i, c: (i, nc - 1 - c, 0)
    tab = pl.BlockSpec((None, SEQ_T, RET_DK), rev)
    return _call(
        body, name="retention_bwd", grid=(b, nc), comm=comm,
        in_specs=[pl.BlockSpec((None, SEQ_T, 3072), rev), tab, tab,
                  pl.BlockSpec((1, D_MODEL), lambda i, c: (0, 0)),
                  pl.BlockSpec((None, None, RET_HEADS, RET_DK, RET_DV), lambda i, c: (i, nc - 1 - c, 0, 0, 0)),
                  pl.BlockSpec((None, SEQ_T, D_MODEL), rev)],
        out_specs=[pl.BlockSpec((None, SEQ_T, 3072), rev),
                   pl.BlockSpec((8, D_MODEL), lambda i, c: (0, 0))],
        out_shape=[jax.ShapeDtypeStruct((b, s, 3072), BF16), jax.ShapeDtypeStruct((8, D_MODEL), F32)],
        scratch=[pltpu.VMEM((RET_HEADS, RET_DK, RET_DV), F32),
                 pltpu.VMEM((RET_HEADS, SEQ_T, SEQ_T), F32)],
    )(proj3, cos, sin, gnw, states, da3)


def _softplus_neg(lam):
    z = -lam
    u = jnp.exp(-jnp.abs(z))
    log1p_u = jnp.where(u < 0.01, u * (1.0 - u * (0.5 - u * (1.0 / 3.0))), jnp.log(1.0 + u))
    return jnp.maximum(z, 0.0) + log1p_u


def _lru_coeffs(xc, wr_ref, br_ref, wi_ref, bi_ref, lam_ref):
    rs, is_ = [], []
    for n in range(LRU_BLOCKS):
        xb = _bf(xc[:, n * LRU_BLOCK:(n + 1) * LRU_BLOCK])
        cols = slice(n * LRU_BLOCK, (n + 1) * LRU_BLOCK)
        rs.append(_sigmoid(_dot(xb, wr_ref[n]) + br_ref[:, cols]))
        is_.append(_sigmoid(_dot(xb, wi_ref[n]) + bi_ref[:, cols]))
    r = jnp.concatenate(rs, axis=1)
    i = jnp.concatenate(is_, axis=1)
    sp = _softplus_neg(lam_ref[...])
    la = -LRU_C * r * sp
    a = jnp.exp(la)
    s = jnp.sqrt(-jnp.tanh(la) * (a * a + 1.0))
    return r, i, a, s, sp


_LRU_PARAM_SPECS = [
    pl.BlockSpec((4, D_MODEL), lambda i, c: (0, 0)),
    pl.BlockSpec((1, D_MODEL), lambda i, c: (0, 0)),
    pl.BlockSpec((LRU_BLOCKS, LRU_BLOCK, LRU_BLOCK), lambda i, c: (0, 0, 0)),
    pl.BlockSpec((1, D_MODEL), lambda i, c: (0, 0)),
    pl.BlockSpec((LRU_BLOCKS, LRU_BLOCK, LRU_BLOCK), lambda i, c: (0, 0, 0)),
    pl.BlockSpec((1, D_MODEL), lambda i, c: (0, 0)),
    pl.BlockSpec((1, D_MODEL), lambda i, c: (0, 0)),
]


def _lru_fwd(proj3, params, comm=None):
    b, s, _ = proj3.shape
    nc = s // SEQ_T

    def body(x_ref, y_ref, cw, cb, wr, br, wi, bi, lam, o_ref, h_ref, xc_ref, gy_ref, hdg_ref, xprev, hprev):
        @pl.when(pl.program_id(1) == 0)
        def _():
            xprev[...] = jnp.zeros_like(xprev)
            hprev[...] = jnp.zeros_like(hprev)

        x = x_ref[...].astype(F32)
        prev8 = xprev[...]
        xc = cb[...] + sum(cw[j:j + 1, :] * _shift_down(x, 3 - j, prev8) for j in range(4))
        xprev[...] = x[SEQ_T - 8:]
        xc_ref[...] = xc
        _, i, a, s_, _ = _lru_coeffs(xc, wr, br, wi, bi, lam)
        h = _scan_fwd(a, s_ * (i * xc), hprev[7:8, :])
        hprev[...] = h[SEQ_T - 8:]
        h_ref[...] = h
        gy, dgy = _gelu_and_grad(y_ref[...].astype(F32))
        o_ref[...] = _bf(h * gy)
        gy_ref[...] = _bf(gy)
        hdg_ref[...] = _bf(h * dgy)

    out = pl.BlockSpec((None, SEQ_T, D_MODEL), lambda i, c: (i, c, 0))
    half, full = jax.ShapeDtypeStruct((b, s, D_MODEL), BF16), jax.ShapeDtypeStruct((b, s, D_MODEL), F32)
    return _call(
        body, name="lru_fwd", grid=(b, nc), comm=comm,
        in_specs=[pl.BlockSpec((None, SEQ_T, D_MODEL), lambda i, c: (i, c, 3)),
                  pl.BlockSpec((None, SEQ_T, D_MODEL), lambda i, c: (i, c, 4))] + _LRU_PARAM_SPECS,
        out_specs=[out] * 5, out_shape=[half, full, full, half, half],
        scratch=[pltpu.VMEM((8, D_MODEL), F32), pltpu.VMEM((8, D_MODEL), F32)],
    )(proj3, proj3, *params)


def _lru_bwd(proj3, params, kept, db3, comm=None):
    b, s, _ = proj3.shape
    nc = s // SEQ_T
    blk8 = SEQ_T // 8
    hseq, xcseq, gyseq, hdgseq = kept

    def body(x_ref, xc_ref, h_ref, hp_ref, gy_ref, hdg_ref, db_ref, cw, cb, wr, br, wi, bi, lam,
             d_ref, dwr_ref, dwi_ref, sm_ref, gnext, anext, dxcnext):
        c = pl.program_id(1)
        first_chunk = c == nc - 1

        @pl.when(jnp.logical_and(pl.program_id(0) == 0, c == 0))
        def _():
            dwr_ref[...] = jnp.zeros_like(dwr_ref)
            dwi_ref[...] = jnp.zeros_like(dwi_ref)
            sm_ref[...] = jnp.zeros_like(sm_ref)

        @pl.when(c == 0)
        def _():
            gnext[...] = jnp.zeros_like(gnext)
            anext[...] = jnp.zeros_like(anext)
            dxcnext[...] = jnp.zeros_like(dxcnext)

        x, xc, h = x_ref[...].astype(F32), xc_ref[...], h_ref[...]
        hprev = hp_ref[...] * jnp.where(first_chunk, 0.0, 1.0)
        r, i, a, s_, sp = _lru_coeffs(xc, wr, br, wi, bi, lam)
        db = db_ref[...].astype(F32)
        dy = db * hdg_ref[...].astype(F32)
        a_up = _shift_up(a, 1, anext[...])
        g = _scan_bwd(a_up, db * gy_ref[...].astype(F32), gnext[0:1, :])
        gnext[...] = g[0:8]
        anext[...] = a[0:8]
        da = g * _shift_down(h, 1, hprev)
        ixc = i * xc
        dla = da * a - (g * ixc) * (a * a) / s_
        di = g * s_ * xc
        dxc = g * s_ * i
        dzr = dla * (-LRU_C * sp) * r * (1.0 - r)
        dzi = di * i * (1.0 - i)
        lam_v = lam[...]
        _row_acc(sm_ref, 7, jnp.sum(dla * (LRU_C * r), axis=0, keepdims=True) * _sigmoid(-lam_v))
        _row_acc(sm_ref, 5, jnp.sum(dzr, axis=0, keepdims=True))
        _row_acc(sm_ref, 6, jnp.sum(dzi, axis=0, keepdims=True))
        parts = []
        for n in range(LRU_BLOCKS):
            cols = slice(n * LRU_BLOCK, (n + 1) * LRU_BLOCK)
            xb, zr, zi = _bf(xc[:, cols]), _bf(dzr[:, cols]), _bf(dzi[:, cols])
            parts.append(dxc[:, cols] + _dot_nt(zr, wr[n]) + _dot_nt(zi, wi[n]))
            dwr_ref[n] += _dot_tn(xb, zr)
            dwi_ref[n] += _dot_tn(xb, zi)
        dxc = jnp.concatenate(parts, axis=1)
        _row_acc(sm_ref, 4, jnp.sum(dxc, axis=0, keepdims=True))
        nxt = dxcnext[...]
        dx = jnp.zeros_like(x)
        for j in range(4):
            ahead = _shift_up(dxc, 3 - j, nxt)
            dx = dx + cw[j:j + 1, :] * ahead
            _row_acc(sm_ref, j, jnp.sum(ahead * x, axis=0, keepdims=True))
        dxcnext[...] = dxc[0:8]
        d_ref[:, 0:D_MODEL] = _bf(dx)
        d_ref[:, D_MODEL:2 * D_MODEL] = _bf(dy)

    rev = lambda col: (lambda i, c: (i, nc - 1 - c, col))
    prev = lambda col: (lambda i, c: (i, jnp.maximum((nc - 1 - c) * blk8 - 1, 0), col))
    return _call(
        body, name="lru_bwd", grid=(b, nc), comm=comm,
        in_specs=[pl.BlockSpec((None, SEQ_T, D_MODEL), rev(3)),
                  pl.BlockSpec((None, SEQ_T, D_MODEL), rev(0)),
                  pl.BlockSpec((None, SEQ_T, D_MODEL), rev(0)),
                  pl.BlockSpec((None, 8, D_MODEL), prev(0)),
                  pl.BlockSpec((None, SEQ_T, D_MODEL), rev(0)),
                  pl.BlockSpec((None, SEQ_T, D_MODEL), rev(0)),
                  pl.BlockSpec((None, SEQ_T, D_MODEL), rev(0))] + _LRU_PARAM_SPECS,
        out_specs=[pl.BlockSpec((None, SEQ_T, 2 * D_MODEL), rev(0)),
                   pl.BlockSpec((LRU_BLOCKS, LRU_BLOCK, LRU_BLOCK), lambda i, c: (0, 0, 0)),
                   pl.BlockSpec((LRU_BLOCKS, LRU_BLOCK, LRU_BLOCK), lambda i, c: (0, 0, 0)),
                   pl.BlockSpec((8, D_MODEL), lambda i, c: (0, 0))],
        out_shape=[jax.ShapeDtypeStruct((b, s, 2 * D_MODEL), BF16),
                   jax.ShapeDtypeStruct((LRU_BLOCKS, LRU_BLOCK, LRU_BLOCK), F32),
                   jax.ShapeDtypeStruct((LRU_BLOCKS, LRU_BLOCK, LRU_BLOCK), F32),
                   jax.ShapeDtypeStruct((8, D_MODEL), F32)],
        scratch=[pltpu.VMEM((8, D_MODEL), F32)] * 3,
    )(proj3, xcseq, hseq, hseq, gyseq, hdgseq, db3, *params)


def _merge_parts(a_ref, b_ref, gr_ref, gl_ref, mgb_ref, wro_ref, wlo_ref):
    ya = _dot(a_ref[...], wro_ref[...])
    yb = _dot(b_ref[...], wlo_ref[...])
    sa = _sigmoid(gr_ref[...].astype(F32) + mgb_ref[0:1, :])
    sb = _sigmoid(gl_ref[...].astype(F32) + mgb_ref[1:2, :])
    return ya, yb, sa, sb


def _merge_specs(tm):
    row = lambda col: pl.BlockSpec((tm, D_MODEL), lambda i: (i, col))
    full = pl.BlockSpec((D_MODEL, D_MODEL), lambda i: (0, 0))
    return row, full


def _merge_fwd(a_in, b_in, proj, mgb, wro, wlo, wout, x, *, tm, comm=None):
    m = x.shape[0]
    row, full = _merge_specs(tm)

    def body(a_ref, b_ref, gr_ref, gl_ref, mgb_ref, wro_ref, wlo_ref, wout_ref, x_ref,
             o_ref, mix_ref, ya_ref, yb_ref):
        ya, yb, sa, sb = _merge_parts(a_ref, b_ref, gr_ref, gl_ref, mgb_ref, wro_ref, wlo_ref)
        mix = _bf(sa * ya + sb * yb)
        o_ref[...] = x_ref[...] + _dot(mix, wout_ref[...])
        mix_ref[...] = mix
        ya_ref[...] = _bf(ya)
        yb_ref[...] = _bf(yb)

    act = jax.ShapeDtypeStruct((m, D_MODEL), BF16)
    return _call(
        body, name="merge_fwd", grid=(m // tm,), comm=comm,
        in_specs=[row(0), row(0), row(5), row(6), pl.BlockSpec((2, D_MODEL), lambda i: (0, 0)),
                  full, full, full, row(0)],
        out_specs=[row(0)] * 4,
        out_shape=[jax.ShapeDtypeStruct((m, D_MODEL), F32), act, act, act],
    )(a_in, b_in, proj, proj, mgb, wro, wlo, wout, x)


def _merge_bwd(ya, yb, proj, mgb, wro, wlo, wout, dx2, *, tm):
    m = dx2.shape[0]
    row, full = _merge_specs(tm)

    def body(ya_ref, yb_ref, gr_ref, gl_ref, mgb_ref, wro_ref, wlo_ref, wout_ref, dx_ref,
             dya_ref, dyb_ref, da_ref, db_ref, dg_ref, sm_ref):
        @pl.when(pl.program_id(0) == 0)
        def _():
            sm_ref[...] = jnp.zeros_like(sm_ref)

        ya, yb = ya_ref[...].astype(F32), yb_ref[...].astype(F32)
        sa = _sigmoid(gr_ref[...].astype(F32) + mgb_ref[0:1, :])
        sb = _sigmoid(gl_ref[...].astype(F32) + mgb_ref[1:2, :])
        dmix = _dot_nt(_bf(dx_ref[...]), wout_ref[...])
        dya, dyb = _bf(dmix * sa), _bf(dmix * sb)
        dya_ref[...] = dya
        dyb_ref[...] = dyb
        dga = dmix * ya * sa * (1.0 - sa)
        dgb = dmix * yb * sb * (1.0 - sb)
        dg_ref[:, 0:D_MODEL] = _bf(dga)
        dg_ref[:, D_MODEL:2 * D_MODEL] = _bf(dgb)
        _row_acc(sm_ref, ROW_MGB, jnp.sum(dga, axis=0, keepdims=True))
        _row_acc(sm_ref, ROW_MGB + 1, jnp.sum(dgb, axis=0, keepdims=True))
        da_ref[...] = _bf(_dot_nt(dya, wro_ref[...]))
        db_ref[...] = _bf(_dot_nt(dyb, wlo_ref[...]))

    act = jax.ShapeDtypeStruct((m, D_MODEL), BF16)
    return _call(
        body, name="merge_bwd", grid=(m // tm,),
        in_specs=[row(0), row(0), row(5), row(6), pl.BlockSpec((2, D_MODEL), lambda i: (0, 0)),
                  full, full, full, row(0)],
        out_specs=[row(0)] * 4 + [pl.BlockSpec((tm, 2 * D_MODEL), lambda i: (i, 0)),
                                  pl.BlockSpec((8, D_MODEL), lambda i: (0, 0))],
        out_shape=[act] * 4 + [jax.ShapeDtypeStruct((m, 2 * D_MODEL), BF16),
                               jax.ShapeDtypeStruct((8, D_MODEL), F32)],
    )(ya, yb, proj, proj, mgb, wro, wlo, wout, dx2)


def _ffn_act_fwd(up3, cw, cb):
    b, s, _ = up3.shape

    def body(g_ref, v_ref, cw_ref, cb_ref, o_ref, act_ref, q_ref, gprev):
        @pl.when(pl.program_id(1) == 0)
        def _():
            gprev[...] = jnp.zeros_like(gprev)

        gate, val = g_ref[...].astype(F32), v_ref[...].astype(F32)
        prev8 = gprev[...]
        gc = cb_ref[...] + sum(cw_ref[j:j + 1, :] * _shift_down(gate, 2 - j, prev8) for j in range(3))
        gprev[...] = gate[SEQ_T - 8:]
        act, dact = _gelu_and_grad(gc)
        o_ref[...] = _bf(act * val)
        act_ref[...] = _bf(act)
        q_ref[...] = _bf(dact * val)

    out = pl.BlockSpec((None, SEQ_T, D_FF), lambda i, c: (i, c, 0))
    return _call(
        body, name="ffn_act_fwd", grid=(b, s // SEQ_T),
        in_specs=[pl.BlockSpec((None, SEQ_T, D_FF), lambda i, c: (i, c, 0)),
                  pl.BlockSpec((None, SEQ_T, D_FF), lambda i, c: (i, c, 1)),
                  pl.BlockSpec((3, D_FF), lambda i, c: (0, 0)),
                  pl.BlockSpec((1, D_FF), lambda i, c: (0, 0))],
        out_specs=[out] * 3,
        out_shape=[jax.ShapeDtypeStruct((b, s, D_FF), BF16)] * 3,
        scratch=[pltpu.VMEM((8, D_FF), F32)],
    )(up3, up3, cw, cb)


def _ffn_act_bwd(up3, act3, q3, cw, df3, comm=None):
    b, s, _ = up3.shape
    nc = s // SEQ_T

    def body(g_ref, act_ref, q_ref, df_ref, cw_ref, dg_ref, dv_ref, sm_ref, dgcnext):
        c = pl.program_id(1)

        @pl.when(jnp.logical_and(pl.program_id(0) == 0, c == 0))
        def _():
            sm_ref[...] = jnp.zeros_like(sm_ref)

        @pl.when(c == 0)
        def _():
            dgcnext[...] = jnp.zeros_like(dgcnext)

        gate = g_ref[...].astype(F32)
        df = df_ref[...].astype(F32)
        dv_ref[...] = _bf(df * act_ref[...].astype(F32))
        dgc = df * q_ref[...].astype(F32)
        nxt = dgcnext[...]
        dgate = jnp.zeros_like(gate)
        for j in range(3):
            ahead = _shift_up(dgc, 2 - j, nxt)
            dgate = dgate + cw_ref[j:j + 1, :] * ahead
            _row_acc(sm_ref, j, jnp.sum(ahead * gate, axis=0, keepdims=True))
        _row_acc(sm_ref, 3, jnp.sum(dgc, axis=0, keepdims=True))
        dgcnext[...] = dgc[0:8]
        dg_ref[...] = _bf(dgate)

    rev = pl.BlockSpec((None, SEQ_T, D_FF), lambda i, c: (i, nc - 1 - c, 0))
    return _call(
        body, name="ffn_act_bwd", grid=(b, nc), comm=comm,
        in_specs=[rev, rev, rev, rev, pl.BlockSpec((3, D_FF), lambda i, c: (0, 0))],
        out_specs=[rev, rev, pl.BlockSpec((8, D_FF), lambda i, c: (0, 0))],
        out_shape=[jax.ShapeDtypeStruct((b, s, D_FF), BF16)] * 2 + [jax.ShapeDtypeStruct((8, D_FF), F32)],
        scratch=[pltpu.VMEM((8, D_FF), F32)],
    )(up3, act3, q3, df3, cw)


def _ffn_down_loss(f, wd, x2, nfw, target, *, tm):
    m, kf = f.shape
    nt = m // tm

    def body(f_ref, wd_ref, x_ref, nw_ref, t_ref, loss_ref, dx_ref, dnw_ref, lsum):
        i = pl.program_id(0)

        @pl.when(i == 0)
        def _():
            dnw_ref[...] = jnp.zeros_like(dnw_ref)
            lsum[...] = jnp.zeros_like(lsum)

        x3 = x_ref[...] + _dot(f_ref[...], wd_ref[...])
        nw = nw_ref[...]
        xh, _ = _rms(x3)
        err = xh * nw - t_ref[...]
        lsum[...] += jnp.sum(err * err, axis=0, keepdims=True)
        dx, dnw = _rms_bwd(err * (1.0 / D_MODEL), x3, nw)
        dx_ref[...] = dx
        _row_acc(dnw_ref, ROW_NF, dnw)

        @pl.when(i == nt - 1)
        def _():
            loss_ref[...] = jnp.sum(lsum[...], axis=1, keepdims=True) * (0.5 / D_MODEL)

    row = pl.BlockSpec((tm, D_MODEL), lambda i: (i, 0))
    return _call(
        body, name="ffn_down_loss", grid=(nt,),
        in_specs=[pl.BlockSpec((tm, kf), lambda i: (i, 0)),
                  pl.BlockSpec((kf, D_MODEL), lambda i: (0, 0)),
                  row, pl.BlockSpec((1, D_MODEL), lambda i: (0, 0)), row],
        out_specs=[pl.BlockSpec((1, 1), lambda i: (0, 0)), row,
                   pl.BlockSpec((8, D_MODEL), lambda i: (0, 0))],
        out_shape=[jax.ShapeDtypeStruct((1, 1), F32), jax.ShapeDtypeStruct((m, D_MODEL), F32),
                   jax.ShapeDtypeStruct((8, D_MODEL), F32)],
        scratch=[pltpu.VMEM((1, D_MODEL), F32)],
    )(f, wd, x2, nfw, target)


def _row_tile(rows):
    return next((t for t in (256, 128, 64, 32, 16, 8) if rows % t == 0), rows)


def _adamw(w, gs, m, v, *, name):
    rows, cols = w.shape
    tr = _row_tile(rows)
    ng = len(gs)

    def body(w_ref, *rest):
        g_refs, (m_ref, v_ref, g_out, d_out, m_out, v_out) = rest[:ng], rest[ng:]
        g = g_refs[0][...]
        for r in g_refs[1:]:
            g = g + r[...]
        mn = ADAM_B1 * m_ref[...] + (1.0 - ADAM_B1) * g
        vn = ADAM_B2 * v_ref[...] + (1.0 - ADAM_B2) * (g * g)
        m_hat = mn / (1.0 - ADAM_B1 ** ADAM_STEP)
        v_hat = vn / (1.0 - ADAM_B2 ** ADAM_STEP)
        g_out[...] = g
        d_out[...] = -ADAM_LR * (m_hat / (jnp.sqrt(v_hat) + ADAM_EPS) + ADAM_WD * w_ref[...])
        m_out[...] = mn
        v_out[...] = vn

    spec = pl.BlockSpec((tr, cols), lambda i: (i, 0))
    return _call(
        body, name=name, grid=(rows // tr,),
        in_specs=[spec] * (3 + ng), out_specs=[spec] * 4,
        out_shape=[jax.ShapeDtypeStruct((rows, cols), F32)] * 4,
    )(w, *gs, m, v)


def _mesh_pos():
    x, y, c = lax.axis_index("x"), lax.axis_index("y"), lax.axis_index("c")
    return x, y, c


def _other_chips(x, y, c):
    return [((1 - x, y, c), 2 * (1 - x) + y), ((x, 1 - y, c), 2 * x + 1 - y),
            ((1 - x, 1 - y, c), 2 * (1 - x) + 1 - y)]


def _region(ref, axis, size, half_axis, chip, core=None):
    idx = [slice(None)] * len(ref.shape)
    if core is None:
        idx[axis] = pl.ds(pl.multiple_of(chip * size, size), size)
    elif half_axis == axis:
        h = size // 2
        idx[axis] = pl.ds(pl.multiple_of(chip * size + core * h, h), h)
    else:
        idx[axis] = pl.ds(pl.multiple_of(chip * size, size), size)
        h = ref.shape[half_axis] // 2
        idx[half_axis] = pl.ds(pl.multiple_of(core * h, h), h)
    return ref.at[tuple(idx)]


def _half(ref, half_axis, core):
    idx = [slice(None)] * len(ref.shape)
    h = ref.shape[half_axis] // 2
    idx[half_axis] = pl.ds(pl.multiple_of(core * h, h), h)
    return ref.at[tuple(idx)]


class _Copy:
    def __init__(self, make):
        self._make = make

    def start(self):
        self._make().start()

    def wait(self):
        self._make().wait()

    def wait_send(self):
        self._make().wait_send()

    def wait_recv(self):
        self._make().wait_recv()


def _remote(src, dst, send_sem, recv_sem, dev):
    return _Copy(lambda: pltpu.make_async_remote_copy(
        src_ref=src, dst_ref=dst, send_sem=send_sem, recv_sem=recv_sem, device_id=dev, device_id_type=MESH))


def _local(src, dst, sem):
    return _Copy(lambda: pltpu.make_async_copy(src, dst, sem))


def _dma_sems(n):
    return pltpu.SemaphoreType.DMA((n,))


def _place_shard(w, chip, axis, *, name):
    shape = list(w.shape)
    shape[axis] *= N_CHIPS
    if w.ndim == 3:
        block, grid = (1,) + w.shape[1:], (w.shape[0],)
        in_map, out_map = (lambda i, chip: (i, 0, 0)), (lambda i, chip: (i, chip[0], 0))
    else:
        tr = _row_tile(w.shape[0])
        nt = w.shape[0] // tr
        block, grid = (tr, w.shape[1]), (nt,)
        in_map = lambda i, chip: (i, 0)
        out_map = (lambda i, chip: (chip[0] * nt + i, 0)) if axis == 0 else (lambda i, chip: (i, chip[0]))

    def body(chip_ref, w_ref, o_ref):
        o_ref[...] = _bf(w_ref[...])

    return _call(body, name=name, grid=grid, prefetch=1, in_specs=[pl.BlockSpec(block, in_map)],
                 out_specs=pl.BlockSpec(block, out_map),
                 out_shape=jax.ShapeDtypeStruct(tuple(shape), BF16))(chip, w)


def _ici_leg(srcs, dsts, layout, sizes, n_whole, sems):
    send_sems, recv_sems, local_sems = sems
    x, y, c = _mesh_pos()
    mine = 2 * x + y
    n_big = len(srcs) - n_whole
    local, sends, recvs = [], [], []
    for t, (src, dst) in enumerate(zip(srcs, dsts)):
        if t < n_big:
            ax, hx = layout[t]
            part = _region(src, ax, sizes[t], hx, mine, c)
            landing = lambda chip, dst=dst, ax=ax, hx=hx, size=sizes[t]: _region(dst, ax, size, hx, chip, c)
        else:
            part, landing = src, (lambda chip, dst=dst: dst.at[chip])
            local.append(_local(src, dst.at[mine], local_sems.at[t - n_big]))
        for k, (dev, chip) in enumerate(_other_chips(x, y, c)):
            sends.append(_remote(part, landing(mine), send_sems.at[3 * t + k], recv_sems.at[3 * t + k], dev))
            recvs.append(_remote(part, landing(chip), send_sems.at[3 * t + k], recv_sems.at[3 * t + k], dev))
    return local, sends, recvs


def _d2d_leg(srcs, dsts, layout, sizes, sems):
    send_sems, recv_sems = sems
    x, y, c = _mesh_pos()
    sends, recvs = [], []
    for t, (src, dst) in enumerate(zip(srcs, dsts)):
        ax, hx = layout[t]
        for k, (_, chip) in enumerate(_other_chips(x, y, c)):
            sem = (send_sems.at[3 * t + k], recv_sems.at[3 * t + k])
            sends.append(_remote(_region(src, ax, sizes[t], hx, chip, c),
                                 _region(dst, ax, sizes[t], hx, chip, c), *sem, (x, y, 1 - c)))
            recvs.append(_remote(_region(src, ax, sizes[t], hx, chip, 1 - c),
                                 _region(dst, ax, sizes[t], hx, chip, 1 - c), *sem, (x, y, 1 - c)))
    return sends, recvs


def _gather_shapes(bufs, whole):
    return ([jax.ShapeDtypeStruct(b.shape, b.dtype) for b in bufs]
            + [jax.ShapeDtypeStruct((N_CHIPS,) + w.shape, w.dtype) for w in whole])


def _gather_ici(bufs, layout):
    n = len(bufs)
    sizes = [b.shape[ax] // N_CHIPS for b, (ax, _) in zip(bufs, layout)]

    def start(ins, outs, sems):
        for cp in _ici_leg(ins, outs, layout, sizes, 0, (*sems, None))[1]:
            cp.start()

    def finish(ins, outs, sems):
        _, sends, recvs = _ici_leg(ins, outs, layout, sizes, 0, (*sems, None))
        for cp in recvs:
            cp.wait_recv()
        for cp in sends:
            cp.wait_send()

    return _Comm(bufs, _gather_shapes(bufs, ()), [_dma_sems(3 * n), _dma_sems(3 * n)], start, finish,
                 aliases={i: i for i in range(n)})


def _both(a, b):
    ni, no, ns = len(a.ins), len(a.outs), len(a.sems)

    def start(ins, outs, sems):
        a.start(ins[:ni], outs[:no], sems[:ns])
        b.start(ins[ni:], outs[no:], sems[ns:])

    def finish(ins, outs, sems):
        a.finish(ins[:ni], outs[:no], sems[:ns])
        b.finish(ins[ni:], outs[no:], sems[ns:])

    aliases = {**a.aliases, **{ni + i: no + o for i, o in b.aliases.items()}}
    return _Comm(a.ins + b.ins, a.outs + b.outs, a.sems + b.sems, start, finish, aliases)


def _gather_d2d(bufs, layout, sizes):
    n = len(bufs)

    def start(ins, outs, sems):
        for cp in _d2d_leg(ins, outs, layout, sizes, sems)[0]:
            cp.start()

    def finish(ins, outs, sems):
        sends, recvs = _d2d_leg(ins, outs, layout, sizes, sems)
        for cp in recvs:
            cp.wait_recv()
        for cp in sends:
            cp.wait_send()

    return _Comm(bufs, [jax.ShapeDtypeStruct(b.shape, b.dtype) for b in bufs],
                 [_dma_sems(3 * n), _dma_sems(3 * n)], start, finish, aliases={i: i for i in range(n)})


def _norm_bf16(x, nw, *, name, tm):
    m, d = x.shape

    def body(x_ref, nw_ref, h_ref):
        h_ref[...] = _bf(_rms(x_ref[...])[0] * nw_ref[...])

    row = pl.BlockSpec((tm, d), lambda i: (i, 0))
    return _call(body, name=name, grid=(m // tm,), in_specs=[row, pl.BlockSpec((1, d), lambda i: (0, 0))],
                 out_specs=row, out_shape=jax.ShapeDtypeStruct((m, d), BF16))(x, nw)


def _in_proj_gather(h1, w_buf, later, later_cut, small, order, *, tm):
    m, d = h1.shape
    width = w_buf.shape[1] // N_CHIPS
    nr, nl = m // tm, len(later)
    sizes = [b.shape[ax] // N_CHIPS for b, (ax, _) in zip(later, later_cut)]

    def body(order_ref, h_ref, w_in, *rest):
        later_in, small_in = rest[:nl], rest[nl]
        o_ref, w_out = rest[nl + 1], rest[nl + 2]
        later_out, small_out = rest[nl + 3:2 * nl + 3], rest[2 * nl + 3]
        wv, load_sem, ici_send, ici_recv, d2d_send, d2d_recv, l_send, l_recv, l_local = rest[2 * nl + 4:]
        s, i = pl.program_id(0), pl.program_id(1)
        x, y, c = _mesh_pos()
        mine = 2 * x + y
        peers = _other_chips(x, y, c)
        part = lambda ref, chip, core=None: _region(ref, 1, width, 0, chip, core)

        def ici(k):
            dev, chip = peers[k]
            sem = (ici_send.at[k], ici_recv.at[k])
            return (_remote(part(w_in, mine, c), part(w_out, mine, c), *sem, dev),
                    _remote(part(w_in, chip, c), part(w_out, chip, c), *sem, dev))

        def d2d(k):
            chip, sem, sib = peers[k][1], (d2d_send.at[k], d2d_recv.at[k]), (x, y, 1 - c)
            return (_remote(part(w_out, chip, c), part(w_out, chip, c), *sem, sib),
                    _remote(part(w_out, chip, 1 - c), part(w_out, chip, 1 - c), *sem, sib))

        def load(src, chip, slot):
            cp = _local(part(src, chip), wv.at[slot], load_sem.at[slot])
            cp.start()
            cp.wait()

        def others():
            return _ici_leg(list(later_in) + [small_in], list(later_out) + [small_out], later_cut, sizes, 1,
                            (l_send, l_recv, l_local))

        @pl.when(jnp.logical_and(s == 0, i == 0))
        def _():
            for k in range(3):
                ici(k)[0].start()
            local, sends, _ = others()
            for cp in local + sends:
                cp.start()
            load(w_in, mine, 0)

        o_ref[...] = _bf(_dot(h_ref[...], wv[s % 2]))

        @pl.when(i == nr - 1)
        def _():
            for k in range(3):
                @pl.when(s == k)
                def _(k=k):
                    ici(k)[1].wait_recv()
                    d2d(k)[0].start()
                    d2d(k)[1].wait_recv()
                    load(w_out, peers[k][1], (k + 1) % 2)

            @pl.when(s == 3)
            def _():
                for k in range(3):
                    ici(k)[0].wait_send()
                    d2d(k)[0].wait_send()
                local, sends, recvs = others()
                for cp in recvs:
                    cp.wait_recv()
                for cp in sends:
                    cp.wait_send()
                for cp in local:
                    cp.wait()

    any_spec = pl.BlockSpec(memory_space=pl.ANY)
    n_any = nl + 2
    outs = _call(
        body, name="in_proj", grid=(N_CHIPS, nr), prefetch=1,
        in_specs=[pl.BlockSpec((tm, d), lambda s, i, order: (i, 0))] + [any_spec] * n_any,
        out_specs=[pl.BlockSpec((tm, width), lambda s, i, order: (i, order[s]))] + [any_spec] * n_any,
        out_shape=[jax.ShapeDtypeStruct((m, w_buf.shape[1]), BF16)] + _gather_shapes([w_buf] + list(later), [small]),
        scratch=[pltpu.VMEM((2, d, width), BF16), _dma_sems(2), _dma_sems(3), _dma_sems(3), _dma_sems(3),
                 _dma_sems(3), _dma_sems(3 * (nl + 1)), _dma_sems(3 * (nl + 1)), _dma_sems(1)],
        aliases={2 + t: 1 + t for t in range(nl + 1)},
    )(order, h1, w_buf, *later, small)
    return outs[0], outs[1], list(outs[2:2 + nl]), outs[2 + nl]


def _exchange(grads, layout):
    n = len(grads)
    others = N_DEV - 1
    sizes = [g.shape[ax] // N_CHIPS for g, (ax, _) in zip(grads, layout)]
    out_shapes = []
    for g, (ax, hx), sz in zip(grads, layout, sizes):
        shape = list(g.shape)
        shape[ax] = sz
        shape[hx] //= 2
        out_shapes.append(jax.ShapeDtypeStruct((others,) + tuple(shape), g.dtype))

    def copies(ins, outs, sems):
        send_sems, recv_sems = sems
        x, y, c = _mesh_pos()
        sends, recvs = [], []
        for t, (src, dst) in enumerate(zip(ins, outs)):
            ax, hx = layout[t]
            for r in range(1, N_DEV):
                px = (1 - x) if r & 4 else x
                py = (1 - y) if r & 2 else y
                pc = (1 - c) if r & 1 else c
                sem = (send_sems.at[others * t + r - 1], recv_sems.at[others * t + r - 1])
                part = _region(src, ax, sizes[t], hx, 2 * px + py, pc)
                sends.append(_remote(part, dst.at[r - 1], *sem, (px, py, pc)))
                recvs.append(_remote(part, dst.at[r - 1], *sem, (px, py, pc)))
        return sends, recvs

    def start(ins, outs, sems):
        for cp in copies(ins, outs, sems)[0]:
            cp.start()

    def finish(ins, outs, sems):
        sends, recvs = copies(ins, outs, sems)
        for cp in recvs:
            cp.wait_recv()
        for cp in sends:
            cp.wait_send()

    return _Comm(grads, out_shapes, [_dma_sems(others * n), _dma_sems(others * n)], start, finish)


def _reduce_half(g, parts, pos, cut, *, name):
    ax, _ = cut
    others = parts.shape[0]
    if g.ndim == 3:
        nb, rows, cols = g.shape
        hb = nb // 2
        block, grid, out_shape = (1, rows // N_CHIPS, cols), (hb,), (nb, rows // N_CHIPS, cols)
        g_map = lambda i, pos: (pos[1] * hb + i, pos[0], 0)
        o_map = lambda i, pos: (pos[1] * hb + i, 0, 0)
        p_map = lambda i, pos: (0, i, 0, 0)
    elif ax == 1:
        rows, cols = g.shape
        tr = _row_tile(rows // 2)
        nt = rows // 2 // tr
        block, grid, out_shape = (tr, cols // N_CHIPS), (nt,), (rows, cols // N_CHIPS)
        g_map = lambda i, pos: (pos[1] * nt + i, pos[0])
        o_map = lambda i, pos: (pos[1] * nt + i, 0)
        p_map = lambda i, pos: (0, i, 0)
    else:
        rows, cols = g.shape
        tr = _row_tile(rows // N_CHIPS // 2)
        nt = rows // N_CHIPS // 2 // tr
        block, grid, out_shape = (tr, cols), (nt,), (rows // N_CHIPS, cols)
        g_map = lambda i, pos: (pos[0] * 2 * nt + pos[1] * nt + i, 0)
        o_map = lambda i, pos: (pos[1] * nt + i, 0)
        p_map = lambda i, pos: (0, i, 0)

    def body(pos_ref, g_ref, p_ref, o_ref):
        acc = g_ref[...].astype(F32)
        for r in range(others):
            acc = acc + p_ref[r].astype(F32)
        o_ref[...] = acc

    return _call(
        body, name=name, grid=grid, prefetch=1,
        in_specs=[pl.BlockSpec(block, g_map), pl.BlockSpec((others,) + block, p_map)],
        out_specs=pl.BlockSpec(block, o_map), out_shape=jax.ShapeDtypeStruct(out_shape, F32),
    )(pos, g, parts)


def _join_halves(bufs):
    n = len(bufs)

    def copies(ins, outs, sems):
        send_sems, recv_sems = sems
        x, y, c = _mesh_pos()
        sends = [_remote(_half(src, 0, c), _half(dst, 0, c), send_sems.at[t], recv_sems.at[t], (x, y, 1 - c))
                 for t, (src, dst) in enumerate(zip(ins, outs))]
        recvs = [_remote(_half(src, 0, 1 - c), _half(dst, 0, 1 - c), send_sems.at[t], recv_sems.at[t],
                         (x, y, 1 - c)) for t, (src, dst) in enumerate(zip(ins, outs))]
        return sends, recvs

    def start(ins, outs, sems):
        for cp in copies(ins, outs, sems)[0]:
            cp.start()

    def finish(ins, outs, sems):
        sends, recvs = copies(ins, outs, sems)
        for cp in recvs:
            cp.wait_recv()
        for cp in sends:
            cp.wait_send()

    comm = _Comm(bufs, [jax.ShapeDtypeStruct(b.shape, b.dtype) for b in bufs],
                 [_dma_sems(n), _dma_sems(n)], start, finish, aliases={i: i for i in range(n)})
    return _call(None, name="join_halves", comm=comm)()[1]


def _allreduce_small(pack):
    rows, cols = pack.shape

    def body(p_ref, o_ref, slots, send_sems, recv_sems):
        x, y, c = _mesh_pos()
        me = 4 * x + 2 * y + c
        slots[me] = p_ref[...]
        copies = []
        for r in range(1, N_DEV):
            fx, fy, fc = (r >> 2) & 1, (r >> 1) & 1, r & 1
            dev = ((1 - x) if fx else x, (1 - y) if fy else y, (1 - c) if fc else c)
            cp = pltpu.make_async_remote_copy(
                src_ref=p_ref, dst_ref=slots.at[me], send_sem=send_sems.at[r - 1],
                recv_sem=recv_sems.at[r - 1], device_id=dev, device_id_type=MESH)
            cp.start()
            copies.append(cp)
        for cp in copies:
            cp.wait_recv()
        for cp in copies:
            cp.wait_send()
        acc = slots[0]
        for d in range(1, N_DEV):
            acc = acc + slots[d]
        o_ref[...] = acc

    vmem = pl.BlockSpec(memory_space=pltpu.VMEM)
    return _call(
        body, name="allreduce_small", in_specs=[vmem], out_specs=vmem,
        out_shape=jax.ShapeDtypeStruct((rows, cols), F32),
        scratch=[pltpu.VMEM((N_DEV, rows, cols), F32), pltpu.SemaphoreType.DMA((N_DEV - 1,)),
                 pltpu.SemaphoreType.DMA((N_DEV - 1,))],
    )(pack)


def _pad_rows(a, rows=8):
    return jnp.pad(a, ((0, rows - a.shape[0]), (0, 0)))


def kernel(x, positions, norm1_w, w_in, merge_gate_b, ret_gn_w, w_ret_o, lru_conv_w, lru_conv_b, lru_w_r, lru_b_r, lru_w_i, lru_b_i, lru_lambda, w_lru_o, w_out, norm2_w, ffn_w_up, ffn_conv_w, ffn_conv_b, ffn_w_down, norm_f_w, loss_target, m_norm1_w, m_w_in, m_merge_gate_b, m_ret_gn_w, m_w_ret_o, m_lru_conv_w, m_lru_conv_b, m_lru_w_r, m_lru_b_r, m_lru_w_i, m_lru_b_i, m_lru_lambda, m_w_lru_o, m_w_out, m_norm2_w, m_ffn_w_up, m_ffn_conv_w, m_ffn_conv_b, m_ffn_w_down, m_norm_f_w, v_norm1_w, v_w_in, v_merge_gate_b, v_ret_gn_w, v_w_ret_o, v_lru_conv_w, v_lru_conv_b, v_lru_w_r, v_lru_b_r, v_lru_w_i, v_lru_b_i, v_lru_lambda, v_w_lru_o, v_w_out, v_norm2_w, v_ffn_w_up, v_ffn_conv_w, v_ffn_conv_b, v_ffn_w_down, v_norm_f_w):
    names = ["norm1_w", "w_in", "merge_gate_b", "ret_gn_w", "w_ret_o", "lru_conv_w", "lru_conv_b", "lru_w_r",
             "lru_b_r", "lru_w_i", "lru_b_i", "lru_lambda", "w_lru_o", "w_out", "norm2_w", "ffn_w_up",
             "ffn_conv_w", "ffn_conv_b", "ffn_w_down", "norm_f_w"]
    w_args = dict(zip(names, [norm1_w, w_in, merge_gate_b, ret_gn_w, w_ret_o, lru_conv_w, lru_conv_b, lru_w_r,
                              lru_b_r, lru_w_i, lru_b_i, lru_lambda, w_lru_o, w_out, norm2_w, ffn_w_up,
                              ffn_conv_w, ffn_conv_b, ffn_w_down, norm_f_w]))
    m_args = dict(zip(names, [m_norm1_w, m_w_in, m_merge_gate_b, m_ret_gn_w, m_w_ret_o, m_lru_conv_w,
                              m_lru_conv_b, m_lru_w_r, m_lru_b_r, m_lru_w_i, m_lru_b_i, m_lru_lambda, m_w_lru_o,
                              m_w_out, m_norm2_w, m_ffn_w_up, m_ffn_conv_w, m_ffn_conv_b, m_ffn_w_down,
                              m_norm_f_w]))
    v_args = dict(zip(names, [v_norm1_w, v_w_in, v_merge_gate_b, v_ret_gn_w, v_w_ret_o, v_lru_conv_w,
                              v_lru_conv_b, v_lru_w_r, v_lru_b_r, v_lru_w_i, v_lru_b_i, v_lru_lambda, v_w_lru_o,
                              v_w_out, v_norm2_w, v_ffn_w_up, v_ffn_conv_w, v_ffn_conv_b, v_ffn_w_down,
                              v_norm_f_w]))

    bsz, seq, d = x.shape
    m = bsz * seq
    tm = min(MM_ROWS, m)
    tm_fused = min(FUSED_ROWS, m)
    tm_tall = min(TALL_ROWS, m)
    chip = 2 * lax.axis_index("x") + lax.axis_index("y")

    big = ["w_in", "w_ret_o", "w_lru_o", "w_out", "lru_w_r", "lru_w_i", "ffn_w_up", "ffn_w_down"]
    cut = dict(w_in=(1, 0), w_ret_o=(0, 0), w_lru_o=(0, 0), w_out=(0, 0), lru_w_r=(1, 0), lru_w_i=(1, 0),
               ffn_w_up=(1, 0), ffn_w_down=(0, 0))
    core = lax.axis_index("c")
    chip1 = jnp.reshape(chip, (1,)).astype(jnp.int32)
    pos = jnp.stack([chip, core]).astype(jnp.int32)
    placed = {n: _place_shard(w_args[n][0], chip1, cut[n][0], name="place_" + n) for n in big}
    small_pack = jnp.concatenate([
        jnp.pad(merge_gate_b[0], ((0, 6), (0, 512))),
        jnp.pad(lru_conv_w[0], ((0, 4), (0, 512))),
        jnp.pad(lru_b_r[0], ((0, 4), (0, 704))),
        jnp.pad(lru_b_i[0], ((0, 4), (0, 704))),
        jnp.pad(ffn_conv_w[0], ((0, 5), (0, 0))),
    ], axis=0)
    x2d = x.reshape(m, d)
    mx, my = lax.axis_index("x"), lax.axis_index("y")
    order = jnp.stack([chip, 2 * (1 - mx) + my, 2 * mx + 1 - my, 2 * (1 - mx) + 1 - my]).astype(jnp.int32)
    mixer = ["w_ret_o", "w_lru_o", "w_out", "lru_w_r", "lru_w_i"]
    cuts = lambda ns: [cut[n] for n in ns]
    sizes = lambda ns: [w_args[n].shape[1 + cut[n][0]] for n in ns]
    h1 = _norm_bf16(x2d, norm1_w, name="norm1", tm=tm)
    proj, w_in_full, bufs, sp = _in_proj_gather(h1, placed["w_in"], [placed[n] for n in mixer], cuts(mixer),
                                               small_pack, order, tm=tm_tall)
    wb = {"w_in": w_in_full}
    mgb = jnp.transpose(sp[:, 0:2, 0:256], (1, 0, 2)).reshape(2, D_MODEL)
    lcw = jnp.transpose(sp[:, 8:12, 0:256], (1, 0, 2)).reshape(4, D_MODEL)
    lbr = jnp.transpose(sp[:, 16:20, 0:64], (1, 0, 2)).reshape(1, D_MODEL)
    lbi = jnp.transpose(sp[:, 24:28, 0:64], (1, 0, 2)).reshape(1, D_MODEL)
    fcw = jnp.transpose(sp[:, 32:35, :], (1, 0, 2)).reshape(3, D_FF)
    nfw = norm_f_w.reshape(1, D_MODEL)

    half = RET_DK // 2
    inv_freq = ROPE_BASE ** (-jnp.arange(half, dtype=F32) / half)
    cos, sin = _rope_tables(positions.reshape(bsz, seq, 1), jnp.concatenate([inv_freq, inv_freq]).reshape(1, RET_DK))
    proj3 = proj.reshape(bsz, seq, D_IN)
    down, up_w = ["ffn_w_down"], ["ffn_w_up"]
    (a_in3, states), bufs = _retention_fwd(
        proj3, cos, sin, ret_gn_w,
        comm=_both(_gather_d2d(bufs, cuts(mixer), sizes(mixer)), _gather_ici([placed["ffn_w_down"]], cuts(down))))
    wb.update(zip(mixer, bufs[:len(mixer)]))
    lru_params = (lcw, lru_conv_b, wb["lru_w_r"], lbr, wb["lru_w_i"], lbi, lru_lambda)
    (b_in3, *lru_kept), (up_buf, wb["ffn_w_down"]) = _lru_fwd(
        proj3, lru_params,
        comm=_both(_gather_ici([placed["ffn_w_up"]], cuts(up_w)), _gather_d2d(bufs[len(mixer):], cuts(down), sizes(down))))
    a_in, b_in = a_in3.reshape(m, d), b_in3.reshape(m, d)
    (x2, mix, ya, yb), (wb["ffn_w_up"],) = _merge_fwd(
        a_in, b_in, proj, mgb, wb["w_ret_o"], wb["w_lru_o"], wb["w_out"], x2d, tm=tm_fused,
        comm=_gather_d2d([up_buf], cuts(up_w), sizes(up_w)))
    up, h2 = _norm_matmul(x2, norm2_w, wb["ffn_w_up"], name="ffn_up", tm=tm_tall, tn=TALL_COLS)
    up3 = up.reshape(bsz, seq, 2 * D_FF)
    f3, act3, q3 = _ffn_act_fwd(up3, fcw, ffn_conv_b)
    f = f3.reshape(m, D_FF)
    loss_dev, dx3, sm_nf = _ffn_down_loss(f, wb["ffn_w_down"], x2, nfw, loss_target.reshape(m, d), tm=tm)
    loss = lax.psum(loss_dev[0, 0], ("x", "y", "c"))

    def send(*ns):
        return _exchange([g_full[n] for n in ns], [cut[n] for n in ns])

    g_full, parts = {}, {}
    df = _mm_nt(dx3, wb["ffn_w_down"], name="ffn_down_dx", tm=tm_tall, out_dtype=BF16)
    g_full["ffn_w_down"] = _mm_tn(f, [dx3], name="ffn_down_dw", tm=tm)
    (dgate3, dval3, sm_ffn), (parts["ffn_w_down"],) = _ffn_act_bwd(
        up3, act3, q3, fcw, df.reshape(bsz, seq, D_FF), comm=send("ffn_w_down"))
    dup = [dgate3.reshape(m, D_FF), dval3.reshape(m, D_FF)]
    g_full["ffn_w_up"] = _mm_tn(h2, dup, name="ffn_up_dw", tm=tm)
    (dx2, sm_n2), (parts["ffn_w_up"],) = _mm_nt_normbwd(
        dup, wb["ffn_w_up"], x2, norm2_w, dx3, name="ffn_up_dx", tm=tm, row=ROW_N2, comm=send("ffn_w_up"))
    dya, dyb, da_in, db_in, dgates, sm_mg = _merge_bwd(
        ya, yb, proj, mgb, wb["w_ret_o"], wb["w_lru_o"], wb["w_out"], dx2, tm=tm_fused)
    g_full["w_out"] = _mm_tn(mix, [dx2], name="out_dw", tm=tm)
    g_full["w_ret_o"] = _mm_tn(a_in, [dya], name="ret_o_dw", tm=tm)
    g_full["w_lru_o"] = _mm_tn(b_in, [dyb], name="lru_o_dw", tm=tm)
    (dlru3, dwr, dwi, sm_lru), (parts["w_out"], parts["w_ret_o"], parts["w_lru_o"]) = _lru_bwd(
        proj3, lru_params, lru_kept, db_in.reshape(bsz, seq, d), comm=send("w_out", "w_ret_o", "w_lru_o"))
    g_full["lru_w_r"], g_full["lru_w_i"] = dwr.astype(BF16), dwi.astype(BF16)
    (dret3, sm_gn), (parts["lru_w_r"], parts["lru_w_i"]) = _retention_bwd(
        proj3, cos, sin, ret_gn_w, states, da_in.reshape(bsz, seq, d), comm=send("lru_w_r", "lru_w_i"))
    dproj = [dret3.reshape(m, 3072), dlru3.reshape(m, 2048), dgates]
    g_full["w_in"] = _mm_tn(h1, dproj, name="in_proj_dw", tm=tm)
    (grad_x, sm_n1), (parts["w_in"],) = _mm_nt_normbwd(
        dproj, wb["w_in"], x2d, norm1_w, dx2, name="in_proj_dx", tm=tm, row=ROW_N1, comm=send("w_in"))

    reduced = _join_halves([_reduce_half(g_full[n], parts[n], pos, cut[n], name="sum_" + n) for n in big])
    misc = sm_n1 + sm_mg + sm_gn + sm_n2 + sm_nf
    pack = jnp.concatenate(
        [misc, sm_lru, sm_ffn[:, 0:1024], sm_ffn[:, 1024:2048], sm_ffn[:, 2048:3072]], axis=0)
    tot = _allreduce_small(pack)
    ffn_sm = jnp.concatenate([tot[16:24], tot[24:32], tot[32:40]], axis=1)
    g_small = {
        "norm1_w": tot[ROW_N1:ROW_N1 + 1], "merge_gate_b": tot[ROW_MGB:ROW_MGB + 2],
        "ret_gn_w": tot[ROW_GN:ROW_GN + 1], "norm2_w": tot[ROW_N2:ROW_N2 + 1], "norm_f_w": tot[ROW_NF:ROW_NF + 1],
        "lru_conv_w": tot[8:12], "lru_conv_b": tot[12:13], "lru_b_r": tot[13:14].reshape(4, 256),
        "lru_b_i": tot[14:15].reshape(4, 256), "lru_lambda": tot[15:16],
        "ffn_conv_w": ffn_sm[0:3], "ffn_conv_b": ffn_sm[3:4],
    }
    small_shard = dict(merge_gate_b=256, lru_conv_w=256, lru_b_r=64, lru_b_i=64, ffn_conv_w=768)

    outs = {}
    for n, g in zip(big, reduced):
        shape = w_args[n].shape
        g = g.reshape(-1, g.shape[-1])
        outs[n] = [o.reshape(shape) for o in _adamw(
            w_args[n].reshape(g.shape), [g], m_args[n].reshape(g.shape), v_args[n].reshape(g.shape),
            name="adamw_" + n)]
    for n, g in g_small.items():
        shape = w_args[n].shape
        if n in small_shard:
            g = lax.dynamic_slice_in_dim(g, chip * small_shard[n], small_shard[n], axis=1)
        w2 = w_args[n].reshape(g.shape)
        outs[n] = [o.reshape(shape) for o in _adamw(
            w2, [g], m_args[n].reshape(g.shape), v_args[n].reshape(g.shape), name="adamw_" + n)]

    result = [loss, grad_x.reshape(bsz, seq, d)]
    for k in range(4):
        result += [outs[n][k] for n in names]
    return tuple(result)
```

```python
import functools
import math

import numpy as np
import jax
import jax.numpy as jnp
from jax import lax
from jax.experimental import pallas as pl
from jax.experimental.pallas import tpu as pltpu

F32 = jnp.float32
BF16 = jnp.bfloat16

D_MODEL = 1024
RET_HEADS = 4
RET_DK = 128
RET_DV = 256
LRU_BLOCKS = 4
LRU_BLOCK = 256
LRU_C = 8.0
D_FF = 3072
D_IN = 7168
ROPE_BASE = 10000.0
RMS_EPS = 1e-6
GN_EPS = 1e-6
ADAM_LR, ADAM_B1, ADAM_B2, ADAM_EPS, ADAM_WD, ADAM_STEP = 0.001, 0.9, 0.999, 1e-08, 0.01, 10

N_CHIPS = 4
N_DEV = 8
SEQ_T = 256
REF_CHUNK = 64
COL = 1024
MM_ROWS = 1024
TALL_ROWS, TALL_COLS = 2048, 1024
FUSED_ROWS = 512
VMEM_LIMIT_BYTES = 56 * 1024 * 1024
MESH = pl.DeviceIdType.MESH
ROW_N1, ROW_MGB, ROW_GN, ROW_N2, ROW_NF = 0, 1, 3, 4, 5
GELU_K = math.sqrt(2.0 / math.pi)
GELU_C = 0.044715


class _Comm:
    def __init__(self, ins, outs, sems, start, finish, aliases=None):
        self.ins, self.outs, self.sems = list(ins), list(outs), list(sems)
        self.start, self.finish, self.aliases = start, finish, dict(aliases or {})


def _call(body, *, name, out_shape=(), grid=None, in_specs=(), out_specs=(), scratch=(), comm=None, prefetch=0,
          aliases=None):
    single = not isinstance(out_shape, (list, tuple))
    out_shape = [out_shape] if single else list(out_shape)
    out_specs = [out_specs] if single else list(out_specs)
    in_specs, scratch = list(in_specs), list(scratch)
    n_in, n_out, n_scr = len(in_specs), len(out_shape), len(scratch)
    kwargs = dict(name=name, compiler_params=pltpu.CompilerParams(vmem_limit_bytes=VMEM_LIMIT_BYTES))
    if prefetch:
        assert comm is None
        spec = pltpu.PrefetchScalarGridSpec(num_scalar_prefetch=prefetch, grid=grid, in_specs=in_specs,
                                            out_specs=out_specs, scratch_shapes=scratch)
        fn = pl.pallas_call(body, out_shape=out_shape, grid_spec=spec, input_output_aliases=dict(aliases or {}),
                            **kwargs)
        return (lambda *args: fn(*args)[0]) if single else fn
    if grid is not None:
        kwargs["grid"] = grid
    if comm is None:
        fn = pl.pallas_call(body, out_shape=out_shape, in_specs=in_specs, out_specs=out_specs,
                            scratch_shapes=scratch, **kwargs)
        return (lambda *args: fn(*args)[0]) if single else fn

    any_spec = pl.BlockSpec(memory_space=pl.ANY)
    n_cin, n_cout = len(comm.ins), len(comm.outs)

    def wrapped(*refs):
        ins, refs = refs[:n_in], refs[n_in:]
        cins, refs = refs[:n_cin], refs[n_cin:]
        outs, refs = refs[:n_out], refs[n_out:]
        couts, refs = refs[:n_cout], refs[n_cout:]
        scr, csems = refs[:n_scr], refs[n_scr:]
        if grid is None:
            comm.start(cins, couts, csems)
            comm.finish(cins, couts, csems)
            return
        ids = [pl.program_id(a) for a in range(len(grid))]
        first = functools.reduce(jnp.logical_and, [i == 0 for i in ids])
        last = functools.reduce(jnp.logical_and, [i == g - 1 for i, g in zip(ids, grid)])
        pl.when(first)(lambda: comm.start(cins, couts, csems))
        body(*ins, *outs, *scr)
        pl.when(last)(lambda: comm.finish(cins, couts, csems))

    fn = pl.pallas_call(
        wrapped, out_shape=out_shape + comm.outs, in_specs=in_specs + [any_spec] * n_cin,
        out_specs=out_specs + [any_spec] * n_cout, scratch_shapes=scratch + comm.sems,
        input_output_aliases={n_in + i: n_out + o for i, o in comm.aliases.items()}, **kwargs)

    def run(*args):
        res = fn(*args, *comm.ins)
        own = res[0] if single else list(res[:n_out])
        return own, list(res[n_out:])

    return run


def _dot(a, b):
    return jnp.dot(a, b, preferred_element_type=F32)


def _dot_nt(a, b):
    return lax.dot_general(a, b, (((1,), (1,)), ((), ())), preferred_element_type=F32)


def _dot_tn(a, b):
    return lax.dot_general(a, b, (((0,), (0,)), ((), ())), preferred_element_type=F32)


def _bf(x):
    return x.astype(BF16)


def _sigmoid(x):
    return 1.0 / (1.0 + jnp.exp(-x))


def _gelu_and_grad(x):
    x2 = x * x
    s = _sigmoid(x * (2.0 * GELU_K * GELU_C * x2 + 2.0 * GELU_K))
    g = x * s
    dg = s + g * (1.0 - s) * (6.0 * GELU_K * GELU_C * x2 + 2.0 * GELU_K)
    return g, dg


def _rms(x):
    r = lax.rsqrt(jnp.mean(x * x, axis=-1, keepdims=True) + RMS_EPS)
    return x * r, r


def _rms_bwd(dy, x, nw):
    xh, r = _rms(x)
    g = dy * nw
    dx = r * (g - xh * jnp.mean(g * xh, axis=-1, keepdims=True))
    return dx, jnp.sum(dy * xh, axis=0, keepdims=True)


def _row_acc(ref, row, val):
    ref[row:row + 1, :] = ref[row:row + 1, :] + val


def _shift_down(x, j, prev8):
    if j == 0:
        return x
    n = x.shape[0] // 8
    row = lax.broadcasted_iota(jnp.int32, prev8.shape, 0)
    turned = [pltpu.roll(prev8, j, 0)] + [pltpu.roll(x[8 * k:8 * k + 8], j, 0) for k in range(n)]
    return jnp.concatenate([jnp.where(row < j, turned[k], turned[k + 1]) for k in range(n)], axis=0)


def _shift_up(x, j, next8):
    if j == 0:
        return x
    n = x.shape[0] // 8
    row = lax.broadcasted_iota(jnp.int32, next8.shape, 0)
    turned = [pltpu.roll(x[8 * k:8 * k + 8], 8 - j, 0) for k in range(n)] + [pltpu.roll(next8, 8 - j, 0)]
    return jnp.concatenate([jnp.where(row >= 8 - j, turned[k + 1], turned[k]) for k in range(n)], axis=0)


def _scan_fwd(a, b, carry):
    row = lax.broadcasted_iota(jnp.int32, (8, a.shape[1]), 0)
    out = []
    for k in range(a.shape[0] // 8):
        ak, bk = a[8 * k:8 * k + 8], b[8 * k:8 * k + 8]
        for s in (1, 2, 4):
            keep = row >= s
            ar, br = pltpu.roll(ak, s, 0), pltpu.roll(bk, s, 0)
            bk = jnp.where(keep, ak * br + bk, bk)
            ak = jnp.where(keep, ak * ar, ak)
        hk = ak * carry + bk
        carry = hk[7:8]
        out.append(hk)
    return jnp.concatenate(out, axis=0)


def _scan_bwd(a, b, carry):
    row = lax.broadcasted_iota(jnp.int32, (8, a.shape[1]), 0)
    out = []
    for k in reversed(range(a.shape[0] // 8)):
        ak, bk = a[8 * k:8 * k + 8], b[8 * k:8 * k + 8]
        for s in (1, 2, 4):
            keep = row < 8 - s
            ar, br = pltpu.roll(ak, 8 - s, 0), pltpu.roll(bk, 8 - s, 0)
            bk = jnp.where(keep, ak * br + bk, bk)
            ak = jnp.where(keep, ak * ar, ak)
        gk = bk + ak * carry
        carry = gk[0:1]
        out.append(gk)
    return jnp.concatenate(out[::-1], axis=0)


def _norm_matmul(x, nw, w, *, name, tm, tn):
    m, d = x.shape
    n = w.shape[1]

    def body(x_ref, nw_ref, w_ref, o_ref, h_ref, h_sc):
        @pl.when(pl.program_id(1) == 0)
        def _():
            xh, _ = _rms(x_ref[...])
            h = _bf(xh * nw_ref[...])
            h_sc[...] = h
            h_ref[...] = h

        o_ref[...] = _bf(_dot(h_sc[...], w_ref[...]))

    return _call(
        body, name=name, grid=(m // tm, n // tn),
        in_specs=[pl.BlockSpec((tm, d), lambda i, j: (i, 0)),
                  pl.BlockSpec((1, d), lambda i, j: (0, 0)),
                  pl.BlockSpec((d, tn), lambda i, j: (0, j))],
        out_specs=[pl.BlockSpec((tm, tn), lambda i, j: (i, j)),
                   pl.BlockSpec((tm, d), lambda i, j: (i, 0))],
        out_shape=[jax.ShapeDtypeStruct((m, n), BF16), jax.ShapeDtypeStruct((m, d), BF16)],
        scratch=[pltpu.VMEM((tm, d), BF16)],
    )(x, nw, w)


def _mm(a, w, *, name, tm, out_dtype):
    m, k = a.shape
    n = w.shape[1]

    def body(a_ref, w_ref, o_ref):
        o_ref[...] = _dot(_bf(a_ref[...]), w_ref[...]).astype(out_dtype)

    return _call(
        body, name=name, grid=(m // tm, n // COL),
        in_specs=[pl.BlockSpec((tm, k), lambda i, j: (i, 0)),
                  pl.BlockSpec((k, COL), lambda i, j: (0, j))],
        out_specs=pl.BlockSpec((tm, COL), lambda i, j: (i, j)),
        out_shape=jax.ShapeDtypeStruct((m, n), out_dtype),
    )(a, w)


def _piece_layout(pieces):
    offs, nblk, o = [], [], 0
    for p in pieces:
        offs.append(o)
        nblk.append(p.shape[1] // COL)
        o += p.shape[1] // COL
    return offs, nblk, o


def _mm_tn(a, pieces, *, name, tm, out_dtype=BF16):
    m, k = a.shape
    offs, nblk, nn = _piece_layout(pieces)

    def piece_spec(o, nb):
        def idx(ki, nj, mi):
            use = jnp.logical_and(nj >= o, nj < o + nb)
            return (jnp.where(use, mi, 0), jnp.clip(nj - o, 0, nb - 1))
        return pl.BlockSpec((tm, COL), idx)

    def body(a_ref, *rest):
        p_refs, o_ref, acc = rest[:len(pieces)], rest[len(pieces)], rest[len(pieces) + 1]
        nj, mi = pl.program_id(1), pl.program_id(2)

        @pl.when(mi == 0)
        def _():
            acc[...] = jnp.zeros_like(acc)

        for p_ref, o, nb in zip(p_refs, offs, nblk):
            @pl.when(jnp.logical_and(nj >= o, nj < o + nb))
            def _(p_ref=p_ref):
                acc[...] += _dot_tn(_bf(a_ref[...]), _bf(p_ref[...]))

        @pl.when(mi == pl.num_programs(2) - 1)
        def _():
            o_ref[...] = acc[...].astype(out_dtype)

    return _call(
        body, name=name, grid=(k // COL, nn, m // tm),
        in_specs=[pl.BlockSpec((tm, COL), lambda ki, nj, mi: (mi, ki))]
        + [piece_spec(o, nb) for o, nb in zip(offs, nblk)],
        out_specs=pl.BlockSpec((COL, COL), lambda ki, nj, mi: (ki, nj)),
        out_shape=jax.ShapeDtypeStruct((k, nn * COL), out_dtype),
        scratch=[pltpu.VMEM((COL, COL), F32)],
    )(a, *pieces)


def _mm_normbwd(pieces, wt, x, nw, dres, *, name, tm, row, comm=None):
    m, d = x.shape
    offs, nblk, nk = _piece_layout(pieces)

    def piece_spec(o, nb):
        return pl.BlockSpec((tm, COL), lambda i, k: (i, jnp.clip(k - o, 0, nb - 1)))

    def body(*refs):
        p_refs = refs[:len(pieces)]
        w_ref, x_ref, nw_ref, dres_ref, dx_ref, dnw_ref, acc = refs[len(pieces):]
        i, k = pl.program_id(0), pl.program_id(1)

        @pl.when(jnp.logical_and(i == 0, k == 0))
        def _():
            dnw_ref[...] = jnp.zeros_like(dnw_ref)

        @pl.when(k == 0)
        def _():
            acc[...] = jnp.zeros_like(acc)

        for p_ref, o, nb in zip(p_refs, offs, nblk):
            @pl.when(jnp.logical_and(k >= o, k < o + nb))
            def _(p_ref=p_ref):
                acc[...] += _dot(_bf(p_ref[...]), w_ref[...])

        @pl.when(k == nk - 1)
        def _():
            dx, dnw = _rms_bwd(acc[...], x_ref[...], nw_ref[...])
            dx_ref[...] = dres_ref[...] + dx
            _row_acc(dnw_ref, row, dnw)

    return _call(
        body, name=name, grid=(m // tm, nk), comm=comm,
        in_specs=[piece_spec(o, nb) for o, nb in zip(offs, nblk)]
        + [pl.BlockSpec((COL, d), lambda i, k: (k, 0)),
           pl.BlockSpec((tm, d), lambda i, k: (i, 0)),
           pl.BlockSpec((1, d), lambda i, k: (0, 0)),
           pl.BlockSpec((tm, d), lambda i, k: (i, 0))],
        out_specs=[pl.BlockSpec((tm, d), lambda i, k: (i, 0)),
                   pl.BlockSpec((8, d), lambda i, k: (0, 0))],
        out_shape=[jax.ShapeDtypeStruct((m, d), F32), jax.ShapeDtypeStruct((8, d), F32)],
        scratch=[pltpu.VMEM((tm, d), F32)],
    )(*pieces, wt, x, nw, dres)


def _rope_tables(pos3, invf):
    b, s, _ = pos3.shape

    def body(pos_ref, invf_ref, cos_ref, sin_ref):
        ang = pos_ref[...].astype(F32) * invf_ref[...]
        lane = lax.broadcasted_iota(jnp.int32, ang.shape, 1)
        cos_ref[...] = jnp.cos(ang)
        sin_ref[...] = jnp.where(lane < RET_DK // 2, -1.0, 1.0) * jnp.sin(ang)

    spec = pl.BlockSpec((None, SEQ_T, RET_DK), lambda i, c: (i, c, 0))
    return _call(
        body, name="rope_tables", grid=(b, s // SEQ_T),
        in_specs=[pl.BlockSpec((None, SEQ_T, 1), lambda i, c: (i, c, 0)),
                  pl.BlockSpec((1, RET_DK), lambda i, c: (0, 0))],
        out_specs=[spec, spec],
        out_shape=[jax.ShapeDtypeStruct((b, s, RET_DK), F32)] * 2,
    )(pos3, invf)


def _log_gamma(h):
    return float(np.log1p(-np.power(np.float32(2.0), np.float32(-5.0 - h))).astype(np.float32))


def _decay_matrix(h):
    lg = _log_gamma(h)
    n = lax.broadcasted_iota(jnp.int32, (SEQ_T, SEQ_T), 0)
    m = lax.broadcasted_iota(jnp.int32, (SEQ_T, SEQ_T), 1)
    same = (n // REF_CHUNK) == (m // REF_CHUNK)
    dist = jnp.where(same, jnp.abs(n - m), n - m).astype(F32)
    return jnp.where(jnp.logical_or(same, m < n), jnp.exp(lg * dist), 0.0)


def _decay_vectors(h):
    lg = _log_gamma(h)
    idx = lax.broadcasted_iota(jnp.int32, (SEQ_T, 1), 0).astype(F32)
    qd = jnp.exp(lg * (idx + 1.0))
    kd = jnp.exp(lg * (SEQ_T - 1.0 - idx))
    return qd, kd, math.exp(lg * SEQ_T)


def _rotate(x, cos, sin):
    return x * cos + pltpu.roll(x, RET_DK // 2, 1) * sin


def _rotate_bwd(d, cos, sin):
    return d * cos + pltpu.roll(d * sin, RET_DK // 2, 1)


def _ret_head(p_ref, cos, sin, h):
    q = p_ref[:, h * RET_DK:(h + 1) * RET_DK].astype(F32)
    k = p_ref[:, 512 + h * RET_DK:512 + (h + 1) * RET_DK].astype(F32)
    v = p_ref[:, 1024 + h * RET_DV:1024 + (h + 1) * RET_DV]
    g = p_ref[:, 2048 + h * RET_DV:2048 + (h + 1) * RET_DV].astype(F32)
    qr = _rotate(q, cos, sin)
    kr = _rotate(k, cos, sin) * (RET_DK ** -0.5)
    return qr, kr, v, g


def _group_norm(o):
    mu = jnp.mean(o, axis=-1, keepdims=True)
    oc = o - mu
    rstd = lax.rsqrt(jnp.mean(oc * oc, axis=-1, keepdims=True) + GN_EPS)
    return oc * rstd, rstd


def _retention_fwd(proj3, cos, sin, gnw, comm=None):
    b, s, _ = proj3.shape
    nc = s // SEQ_T

    def body(p_ref, cos_ref, sin_ref, gnw_ref, a_ref, st_ref, state, wtab):
        c = pl.program_id(1)

        @pl.when(jnp.logical_and(pl.program_id(0) == 0, c == 0))
        def _():
            for h in range(RET_HEADS):
                wtab[h] = _decay_matrix(h)

        @pl.when(c == 0)
        def _():
            state[...] = jnp.zeros_like(state)

        cs, sn = cos_ref[...], sin_ref[...]
        for h in range(RET_HEADS):
            qd, kd, gt = _decay_vectors(h)
            qr, kr, v, g = _ret_head(p_ref, cs, sn, h)
            st = state[h]
            st_ref[h] = st
            p = _dot_nt(_bf(qr), _bf(kr)) * wtab[h]
            o = _dot(_bf(p), _bf(v)) + _dot(_bf(qr * qd), _bf(st))
            state[h] = st * gt + _dot_tn(_bf(kr * kd), _bf(v))
            on, _ = _group_norm(o)
            gw = gnw_ref[:, h * RET_DV:(h + 1) * RET_DV]
            a_ref[:, h * RET_DV:(h + 1) * RET_DV] = _bf(on * gw * (g * _sigmoid(g)))

    tab = pl.BlockSpec((None, SEQ_T, RET_DK), lambda i, c: (i, c, 0))
    return _call(
        body, name="retention_fwd", grid=(b, nc), comm=comm,
        in_specs=[pl.BlockSpec((None, SEQ_T, 3072), lambda i, c: (i, c, 0)), tab, tab,
                  pl.BlockSpec((1, D_MODEL), lambda i, c: (0, 0))],
        out_specs=[pl.BlockSpec((None, SEQ_T, D_MODEL), lambda i, c: (i, c, 0)),
                   pl.BlockSpec((None, None, RET_HEADS, RET_DK, RET_DV), lambda i, c: (i, c, 0, 0, 0))],
        out_shape=[jax.ShapeDtypeStruct((b, s, D_MODEL), BF16),
                   jax.ShapeDtypeStruct((b, nc, RET_HEADS, RET_DK, RET_DV), F32)],
        scratch=[pltpu.VMEM((RET_HEADS, RET_DK, RET_DV), F32),
                 pltpu.VMEM((RET_HEADS, SEQ_T, SEQ_T), F32)],
    )(proj3, cos, sin, gnw)


def _retention_bwd(proj3, cos, sin, gnw, states, da3, comm=None):
    b, s, _ = proj3.shape
    nc = s // SEQ_T

    def body(p_ref, cos_ref, sin_ref, gnw_ref, st_ref, da_ref, d_ref, dgn_ref, dstate, wtab):
        c = pl.program_id(1)

        @pl.when(jnp.logical_and(pl.program_id(0) == 0, c == 0))
        def _():
            dgn_ref[...] = jnp.zeros_like(dgn_ref)
            for h in range(RET_HEADS):
                wtab[h] = _decay_matrix(h)

        @pl.when(c == 0)
        def _():
            dstate[...] = jnp.zeros_like(dstate)

        cs, sn = cos_ref[...], sin_ref[...]
        for h in range(RET_HEADS):
            qd, kd, gt = _decay_vectors(h)
            qr, kr, v, g = _ret_head(p_ref, cs, sn, h)
            st, dst, w = st_ref[h], dstate[h], wtab[h]
            qb, kb, vb = _bf(qr), _bf(kr), _bf(v)
            p = _dot_nt(qb, kb) * w
            o = _dot(_bf(p), vb) + _dot(_bf(qr * qd), _bf(st))
            on, rstd = _group_norm(o)
            gw = gnw_ref[:, h * RET_DV:(h + 1) * RET_DV]
            da = da_ref[:, h * RET_DV:(h + 1) * RET_DV].astype(F32)
            sg = _sigmoid(g)
            silu = g * sg
            dg = da * on * gw * (sg * (1.0 + g * (1.0 - sg)))
            dgn_ref[ROW_GN:ROW_GN + 1, h * RET_DV:(h + 1) * RET_DV] += jnp.sum(da * silu * on, axis=0, keepdims=True)
            don = da * silu * gw
            do = rstd * (don - jnp.mean(don, axis=-1, keepdims=True)
                         - on * jnp.mean(don * on, axis=-1, keepdims=True))
            dob = _bf(do)
            dp = _dot_nt(dob, vb) * w
            dqr = _dot(_bf(dp), kb) + _dot_nt(dob, _bf(st)) * qd
            dkr = _dot_tn(_bf(dp), qb) + _dot_nt(vb, _bf(dst)) * kd
            dv = _dot_tn(_bf(p), dob) + _dot(_bf(kr * kd), _bf(dst))
            dstate[h] = dst * gt + _dot_tn(_bf(qr * qd), dob)
            d_ref[:, h * RET_DK:(h + 1) * RET_DK] = _bf(_rotate_bwd(dqr, cs, sn))
            d_ref[:, 512 + h * RET_DK:512 + (h + 1) * RET_DK] = _bf(_rotate_bwd(dkr, cs, sn) * (RET_DK ** -0.5))
            d_ref[:, 1024 + h * RET_DV:1024 + (h + 1) * RET_DV] = _bf(dv)
            d_ref[:, 2048 + h * RET_DV:2048 + (h + 1) * RET_DV] = _bf(dg)

    rev = lambda i, c: (i, nc - 1 - c, 0)
    tab = pl.BlockSpec((None, SEQ_T, RET_DK), rev)
    return _call(
        body, name="retention_bwd", grid=(b, nc), comm=comm,
        in_specs=[pl.BlockSpec((None, SEQ_T, 3072), rev), tab, tab,
                  pl.BlockSpec((1, D_MODEL), lambda i, c: (0, 0)),
                  pl.BlockSpec((None, None, RET_HEADS, RET_DK, RET_DV), lambda i, c: (i, nc - 1 - c, 0, 0, 0)),
                  pl.BlockSpec((None, SEQ_T, D_MODEL), rev)],
        out_specs=[pl.BlockSpec((None, SEQ_T, 3072), rev),
                   pl.BlockSpec((8, D_MODEL), lambda i, c: (0, 0))],
        out_shape=[jax.ShapeDtypeStruct((b, s, 3072), BF16), jax.ShapeDtypeStruct((8, D_MODEL), F32)],
        scratch=[pltpu.VMEM((RET_HEADS, RET_DK, RET_DV), F32),
                 pltpu.VMEM((RET_HEADS, SEQ_T, SEQ_T), F32)],
    )(proj3, cos, sin, gnw, states, da3)


def _softplus_neg(lam):
    z = -lam
    u = jnp.exp(-jnp.abs(z))
    log1p_u = jnp.where(u < 0.01, u * (1.0 - u * (0.5 - u * (1.0 / 3.0))), jnp.log(1.0 + u))
    return jnp.maximum(z, 0.0) + log1p_u


def _lru_coeffs(xc, wr_ref, br_ref, wi_ref, bi_ref, lam_ref):
    rs, is_ = [], []
    for n in range(LRU_BLOCKS):
        xb = _bf(xc[:, n * LRU_BLOCK:(n + 1) * LRU_BLOCK])
        cols = slice(n * LRU_BLOCK, (n + 1) * LRU_BLOCK)
        rs.append(_sigmoid(_dot(xb, wr_ref[n]) + br_ref[:, cols]))
        is_.append(_sigmoid(_dot(xb, wi_ref[n]) + bi_ref[:, cols]))
    r = jnp.concatenate(rs, axis=1)
    i = jnp.concatenate(is_, axis=1)
    sp = _softplus_neg(lam_ref[...])
    la = -LRU_C * r * sp
    a = jnp.exp(la)
    s = jnp.sqrt(-jnp.tanh(la) * (a * a + 1.0))
    return r, i, a, s, sp


_LRU_PARAM_SPECS = [
    pl.BlockSpec((4, D_MODEL), lambda i, c: (0, 0)),
    pl.BlockSpec((1, D_MODEL), lambda i, c: (0, 0)),
    pl.BlockSpec((LRU_BLOCKS, LRU_BLOCK, LRU_BLOCK), lambda i, c: (0, 0, 0)),
    pl.BlockSpec((1, D_MODEL), lambda i, c: (0, 0)),
    pl.BlockSpec((LRU_BLOCKS, LRU_BLOCK, LRU_BLOCK), lambda i, c: (0, 0, 0)),
    pl.BlockSpec((1, D_MODEL), lambda i, c: (0, 0)),
    pl.BlockSpec((1, D_MODEL), lambda i, c: (0, 0)),
]


def _lru_fwd(proj3, params, comm=None):
    b, s, _ = proj3.shape
    nc = s // SEQ_T

    def body(x_ref, y_ref, cw, cb, wr, br, wi, bi, lam,
             o_ref, h_ref, xc_ref, a_ref, s_ref, gy_ref, hdg_ref, r_ref, i_ref, xprev, hprev):
        @pl.when(pl.program_id(1) == 0)
        def _():
            xprev[...] = jnp.zeros_like(xprev)
            hprev[...] = jnp.zeros_like(hprev)

        x = x_ref[...].astype(F32)
        prev8 = xprev[...]
        xc = cb[...] + sum(cw[j:j + 1, :] * _shift_down(x, 3 - j, prev8) for j in range(4))
        xprev[...] = x[SEQ_T - 8:]
        xc_ref[...] = xc
        r, i, a, s_, _ = _lru_coeffs(xc, wr, br, wi, bi, lam)
        a_ref[...] = a
        s_ref[...] = s_
        r_ref[...] = _bf(r)
        i_ref[...] = _bf(i)
        h = _scan_fwd(a, s_ * (i * xc), hprev[7:8, :])
        hprev[...] = h[SEQ_T - 8:]
        h_ref[...] = h
        gy, dgy = _gelu_and_grad(y_ref[...].astype(F32))
        o_ref[...] = _bf(h * gy)
        gy_ref[...] = _bf(gy)
        hdg_ref[...] = _bf(h * dgy)

    out = pl.BlockSpec((None, SEQ_T, D_MODEL), lambda i, c: (i, c, 0))
    half, full = jax.ShapeDtypeStruct((b, s, D_MODEL), BF16), jax.ShapeDtypeStruct((b, s, D_MODEL), F32)
    return _call(
        body, name="lru_fwd", grid=(b, nc), comm=comm,
        in_specs=[pl.BlockSpec((None, SEQ_T, D_MODEL), lambda i, c: (i, c, 3)),
                  pl.BlockSpec((None, SEQ_T, D_MODEL), lambda i, c: (i, c, 4))] + _LRU_PARAM_SPECS,
        out_specs=[out] * 9, out_shape=[half, full, full, full, full, half, half, half, half],
        scratch=[pltpu.VMEM((8, D_MODEL), F32), pltpu.VMEM((8, D_MODEL), F32)],
    )(proj3, proj3, *params)


def _lru_bwd(proj3, params, kept, db3, comm=None):
    b, s, _ = proj3.shape
    nc = s // SEQ_T
    blk8 = SEQ_T // 8
    hseq = kept[0]

    def body(x_ref, h_ref, xc_ref, a_ref, s_ref, gy_ref, hdg_ref, r_ref, i_ref, hp_ref, db_ref,
             cw, cb, wr, br, wi, bi, lam, d_ref, dwr_ref, dwi_ref, sm_ref, gnext, anext, dxcnext):
        c = pl.program_id(1)
        first_chunk = c == nc - 1

        @pl.when(jnp.logical_and(pl.program_id(0) == 0, c == 0))
        def _():
            dwr_ref[...] = jnp.zeros_like(dwr_ref)
            dwi_ref[...] = jnp.zeros_like(dwi_ref)
            sm_ref[...] = jnp.zeros_like(sm_ref)

        @pl.when(c == 0)
        def _():
            gnext[...] = jnp.zeros_like(gnext)
            anext[...] = jnp.zeros_like(anext)
            dxcnext[...] = jnp.zeros_like(dxcnext)

        x, xc, h = x_ref[...].astype(F32), xc_ref[...], h_ref[...]
        hprev = hp_ref[...] * jnp.where(first_chunk, 0.0, 1.0)
        r, i, a, s_ = r_ref[...].astype(F32), i_ref[...].astype(F32), a_ref[...], s_ref[...]
        sp = _softplus_neg(lam[...])
        db = db_ref[...].astype(F32)
        dy = db * hdg_ref[...].astype(F32)
        a_up = _shift_up(a, 1, anext[...])
        g = _scan_bwd(a_up, db * gy_ref[...].astype(F32), gnext[0:1, :])
        gnext[...] = g[0:8]
        anext[...] = a[0:8]
        da = g * _shift_down(h, 1, hprev)
        ixc = i * xc
        dla = da * a - (g * ixc) * (a * a) / s_
        di = g * s_ * xc
        dxc = g * s_ * i
        dzr = dla * (-LRU_C * sp) * r * (1.0 - r)
        dzi = di * i * (1.0 - i)
        lam_v = lam[...]
        _row_acc(sm_ref, 7, jnp.sum(dla * (LRU_C * r), axis=0, keepdims=True) * _sigmoid(-lam_v))
        _row_acc(sm_ref, 5, jnp.sum(dzr, axis=0, keepdims=True))
        _row_acc(sm_ref, 6, jnp.sum(dzi, axis=0, keepdims=True))
        parts = []
        for n in range(LRU_BLOCKS):
            cols = slice(n * LRU_BLOCK, (n + 1) * LRU_BLOCK)
            xb, zr, zi = _bf(xc[:, cols]), _bf(dzr[:, cols]), _bf(dzi[:, cols])
            parts.append(dxc[:, cols] + _dot_nt(zr, wr[n]) + _dot_nt(zi, wi[n]))
            dwr_ref[n] += _dot_tn(xb, zr)
            dwi_ref[n] += _dot_tn(xb, zi)
        dxc = jnp.concatenate(parts, axis=1)
        _row_acc(sm_ref, 4, jnp.sum(dxc, axis=0, keepdims=True))
        nxt = dxcnext[...]
        dx = jnp.zeros_like(x)
        for j in range(4):
            ahead = _shift_up(dxc, 3 - j, nxt)
            dx = dx + cw[j:j + 1, :] * ahead
            _row_acc(sm_ref, j, jnp.sum(ahead * x, axis=0, keepdims=True))
        dxcnext[...] = dxc[0:8]
        d_ref[:, 0:D_MODEL] = _bf(dx)
        d_ref[:, D_MODEL:2 * D_MODEL] = _bf(dy)

    rev = lambda col: (lambda i, c: (i, nc - 1 - c, col))
    prev = lambda col: (lambda i, c: (i, jnp.maximum((nc - 1 - c) * blk8 - 1, 0), col))
    return _call(
        body, name="lru_bwd", grid=(b, nc), comm=comm,
        in_specs=[pl.BlockSpec((None, SEQ_T, D_MODEL), rev(3))]
        + [pl.BlockSpec((None, SEQ_T, D_MODEL), rev(0))] * len(kept)
        + [pl.BlockSpec((None, 8, D_MODEL), prev(0)), pl.BlockSpec((None, SEQ_T, D_MODEL), rev(0))]
        + _LRU_PARAM_SPECS,
        out_specs=[pl.BlockSpec((None, SEQ_T, 2 * D_MODEL), rev(0)),
                   pl.BlockSpec((LRU_BLOCKS, LRU_BLOCK, LRU_BLOCK), lambda i, c: (0, 0, 0)),
                   pl.BlockSpec((LRU_BLOCKS, LRU_BLOCK, LRU_BLOCK), lambda i, c: (0, 0, 0)),
                   pl.BlockSpec((8, D_MODEL), lambda i, c: (0, 0))],
        out_shape=[jax.ShapeDtypeStruct((b, s, 2 * D_MODEL), BF16),
                   jax.ShapeDtypeStruct((LRU_BLOCKS, LRU_BLOCK, LRU_BLOCK), F32),
                   jax.ShapeDtypeStruct((LRU_BLOCKS, LRU_BLOCK, LRU_BLOCK), F32),
                   jax.ShapeDtypeStruct((8, D_MODEL), F32)],
        scratch=[pltpu.VMEM((8, D_MODEL), F32)] * 3,
    )(proj3, *kept, hseq, db3, *params)


def _merge_parts(a_ref, b_ref, gr_ref, gl_ref, mgb_ref, wro_ref, wlo_ref):
    ya = _dot(a_ref[...], wro_ref[...])
    yb = _dot(b_ref[...], wlo_ref[...])
    sa = _sigmoid(gr_ref[...].astype(F32) + mgb_ref[0:1, :])
    sb = _sigmoid(gl_ref[...].astype(F32) + mgb_ref[1:2, :])
    return ya, yb, sa, sb


def _merge_specs(tm):
    row = lambda col: pl.BlockSpec((tm, D_MODEL), lambda i: (i, col))
    full = pl.BlockSpec((D_MODEL, D_MODEL), lambda i: (0, 0))
    return row, full


def _merge_fwd(a_in, b_in, proj, mgb, wro, wlo, wout, x, *, tm, comm=None):
    m = x.shape[0]
    row, full = _merge_specs(tm)

    def body(a_ref, b_ref, gr_ref, gl_ref, mgb_ref, wro_ref, wlo_ref, wout_ref, x_ref,
             o_ref, mix_ref, ya_ref, yb_ref):
        ya, yb, sa, sb = _merge_parts(a_ref, b_ref, gr_ref, gl_ref, mgb_ref, wro_ref, wlo_ref)
        mix = _bf(sa * ya + sb * yb)
        o_ref[...] = x_ref[...] + _dot(mix, wout_ref[...])
        mix_ref[...] = mix
        ya_ref[...] = _bf(ya)
        yb_ref[...] = _bf(yb)

    act = jax.ShapeDtypeStruct((m, D_MODEL), BF16)
    return _call(
        body, name="merge_fwd", grid=(m // tm,), comm=comm,
        in_specs=[row(0), row(0), row(5), row(6), pl.BlockSpec((2, D_MODEL), lambda i: (0, 0)),
                  full, full, full, row(0)],
        out_specs=[row(0)] * 4,
        out_shape=[jax.ShapeDtypeStruct((m, D_MODEL), F32), act, act, act],
    )(a_in, b_in, proj, proj, mgb, wro, wlo, wout, x)


def _merge_bwd(ya, yb, proj, mgb, wro, wlo, wout, dx2, *, tm):
    m = dx2.shape[0]
    row, full = _merge_specs(tm)

    def body(ya_ref, yb_ref, gr_ref, gl_ref, mgb_ref, wro_ref, wlo_ref, wout_ref, dx_ref,
             dya_ref, dyb_ref, da_ref, db_ref, dg_ref, sm_ref):
        @pl.when(pl.program_id(0) == 0)
        def _():
            sm_ref[...] = jnp.zeros_like(sm_ref)

        ya, yb = ya_ref[...].astype(F32), yb_ref[...].astype(F32)
        sa = _sigmoid(gr_ref[...].astype(F32) + mgb_ref[0:1, :])
        sb = _sigmoid(gl_ref[...].astype(F32) + mgb_ref[1:2, :])
        dmix = _dot(_bf(dx_ref[...]), wout_ref[...])
        dya, dyb = _bf(dmix * sa), _bf(dmix * sb)
        dya_ref[...] = dya
        dyb_ref[...] = dyb
        dga = dmix * ya * sa * (1.0 - sa)
        dgb = dmix * yb * sb * (1.0 - sb)
        dg_ref[:, 0:D_MODEL] = _bf(dga)
        dg_ref[:, D_MODEL:2 * D_MODEL] = _bf(dgb)
        _row_acc(sm_ref, ROW_MGB, jnp.sum(dga, axis=0, keepdims=True))
        _row_acc(sm_ref, ROW_MGB + 1, jnp.sum(dgb, axis=0, keepdims=True))
        da_ref[...] = _bf(_dot(dya, wro_ref[...]))
        db_ref[...] = _bf(_dot(dyb, wlo_ref[...]))

    act = jax.ShapeDtypeStruct((m, D_MODEL), BF16)
    return _call(
        body, name="merge_bwd", grid=(m // tm,),
        in_specs=[row(0), row(0), row(5), row(6), pl.BlockSpec((2, D_MODEL), lambda i: (0, 0)),
                  full, full, full, row(0)],
        out_specs=[row(0)] * 4 + [pl.BlockSpec((tm, 2 * D_MODEL), lambda i: (i, 0)),
                                  pl.BlockSpec((8, D_MODEL), lambda i: (0, 0))],
        out_shape=[act] * 4 + [jax.ShapeDtypeStruct((m, 2 * D_MODEL), BF16),
                               jax.ShapeDtypeStruct((8, D_MODEL), F32)],
    )(ya, yb, proj, proj, mgb, wro, wlo, wout, dx2)


def _ffn_act_fwd(up3, cw, cb):
    b, s, _ = up3.shape

    def body(g_ref, v_ref, cw_ref, cb_ref, o_ref, act_ref, q_ref, gprev):
        @pl.when(pl.program_id(1) == 0)
        def _():
            gprev[...] = jnp.zeros_like(gprev)

        gate, val = g_ref[...].astype(F32), v_ref[...].astype(F32)
        prev8 = gprev[...]
        gc = cb_ref[...] + sum(cw_ref[j:j + 1, :] * _shift_down(gate, 2 - j, prev8) for j in range(3))
        gprev[...] = gate[SEQ_T - 8:]
        act, dact = _gelu_and_grad(gc)
        o_ref[...] = _bf(act * val)
        act_ref[...] = _bf(act)
        q_ref[...] = _bf(dact * val)

    out = pl.BlockSpec((None, SEQ_T, D_FF), lambda i, c: (i, c, 0))
    return _call(
        body, name="ffn_act_fwd", grid=(b, s // SEQ_T),
        in_specs=[pl.BlockSpec((None, SEQ_T, D_FF), lambda i, c: (i, c, 0)),
                  pl.BlockSpec((None, SEQ_T, D_FF), lambda i, c: (i, c, 1)),
                  pl.BlockSpec((3, D_FF), lambda i, c: (0, 0)),
                  pl.BlockSpec((1, D_FF), lambda i, c: (0, 0))],
        out_specs=[out] * 3,
        out_shape=[jax.ShapeDtypeStruct((b, s, D_FF), BF16)] * 3,
        scratch=[pltpu.VMEM((8, D_FF), F32)],
    )(up3, up3, cw, cb)


def _ffn_act_bwd(up3, act3, q3, cw, df3, comm=None):
    b, s, _ = up3.shape
    nc = s // SEQ_T

    def body(g_ref, act_ref, q_ref, df_ref, cw_ref, dg_ref, dv_ref, sm_ref, dgcnext):
        c = pl.program_id(1)

        @pl.when(jnp.logical_and(pl.program_id(0) == 0, c == 0))
        def _():
            sm_ref[...] = jnp.zeros_like(sm_ref)

        @pl.when(c == 0)
        def _():
            dgcnext[...] = jnp.zeros_like(dgcnext)

        gate = g_ref[...].astype(F32)
        df = df_ref[...].astype(F32)
        dv_ref[...] = _bf(df * act_ref[...].astype(F32))
        dgc = df * q_ref[...].astype(F32)
        nxt = dgcnext[...]
        dgate = jnp.zeros_like(gate)
        for j in range(3):
            ahead = _shift_up(dgc, 2 - j, nxt)
            dgate = dgate + cw_ref[j:j + 1, :] * ahead
            _row_acc(sm_ref, j, jnp.sum(ahead * gate, axis=0, keepdims=True))
        _row_acc(sm_ref, 3, jnp.sum(dgc, axis=0, keepdims=True))
        dgcnext[...] = dgc[0:8]
        dg_ref[...] = _bf(dgate)

    rev = pl.BlockSpec((None, SEQ_T, D_FF), lambda i, c: (i, nc - 1 - c, 0))
    return _call(
        body, name="ffn_act_bwd", grid=(b, nc), comm=comm,
        in_specs=[rev, rev, rev, rev, pl.BlockSpec((3, D_FF), lambda i, c: (0, 0))],
        out_specs=[rev, rev, pl.BlockSpec((8, D_FF), lambda i, c: (0, 0))],
        out_shape=[jax.ShapeDtypeStruct((b, s, D_FF), BF16)] * 2 + [jax.ShapeDtypeStruct((8, D_FF), F32)],
        scratch=[pltpu.VMEM((8, D_FF), F32)],
    )(up3, act3, q3, df3, cw)


def _ffn_down_loss(f, wd, x2, nfw, target, *, tm):
    m, kf = f.shape
    nt = m // tm

    def body(f_ref, wd_ref, x_ref, nw_ref, t_ref, loss_ref, dx_ref, dnw_ref, lsum):
        i = pl.program_id(0)

        @pl.when(i == 0)
        def _():
            dnw_ref[...] = jnp.zeros_like(dnw_ref)
            lsum[...] = jnp.zeros_like(lsum)

        x3 = x_ref[...] + _dot(f_ref[...], wd_ref[...])
        nw = nw_ref[...]
        xh, _ = _rms(x3)
        err = xh * nw - t_ref[...]
        lsum[...] += jnp.sum(err * err, axis=0, keepdims=True)
        dx, dnw = _rms_bwd(err * (1.0 / D_MODEL), x3, nw)
        dx_ref[...] = dx
        _row_acc(dnw_ref, ROW_NF, dnw)

        @pl.when(i == nt - 1)
        def _():
            loss_ref[...] = jnp.sum(lsum[...], axis=1, keepdims=True) * (0.5 / D_MODEL)

    row = pl.BlockSpec((tm, D_MODEL), lambda i: (i, 0))
    return _call(
        body, name="ffn_down_loss", grid=(nt,),
        in_specs=[pl.BlockSpec((tm, kf), lambda i: (i, 0)),
                  pl.BlockSpec((kf, D_MODEL), lambda i: (0, 0)),
                  row, pl.BlockSpec((1, D_MODEL), lambda i: (0, 0)), row],
        out_specs=[pl.BlockSpec((1, 1), lambda i: (0, 0)), row,
                   pl.BlockSpec((8, D_MODEL), lambda i: (0, 0))],
        out_shape=[jax.ShapeDtypeStruct((1, 1), F32), jax.ShapeDtypeStruct((m, D_MODEL), F32),
                   jax.ShapeDtypeStruct((8, D_MODEL), F32)],
        scratch=[pltpu.VMEM((1, D_MODEL), F32)],
    )(f, wd, x2, nfw, target)


def _row_tile(rows):
    return next((t for t in (256, 128, 64, 32, 16, 8) if rows % t == 0), rows)


def _adamw(w, gs, m, v, *, name):
    rows, cols = w.shape
    tr = _row_tile(rows)
    ng = len(gs)

    def body(w_ref, *rest):
        g_refs, (m_ref, v_ref, g_out, d_out, m_out, v_out) = rest[:ng], rest[ng:]
        g = g_refs[0][...]
        for r in g_refs[1:]:
            g = g + r[...]
        mn = ADAM_B1 * m_ref[...] + (1.0 - ADAM_B1) * g
        vn = ADAM_B2 * v_ref[...] + (1.0 - ADAM_B2) * (g * g)
        m_hat = mn / (1.0 - ADAM_B1 ** ADAM_STEP)
        v_hat = vn / (1.0 - ADAM_B2 ** ADAM_STEP)
        g_out[...] = g
        d_out[...] = -ADAM_LR * (m_hat / (jnp.sqrt(v_hat) + ADAM_EPS) + ADAM_WD * w_ref[...])
        m_out[...] = mn
        v_out[...] = vn

    spec = pl.BlockSpec((tr, cols), lambda i: (i, 0))
    return _call(
        body, name=name, grid=(rows // tr,),
        in_specs=[spec] * (3 + ng), out_specs=[spec] * 4,
        out_shape=[jax.ShapeDtypeStruct((rows, cols), F32)] * 4,
    )(w, *gs, m, v)


def _mesh_pos():
    x, y, c = lax.axis_index("x"), lax.axis_index("y"), lax.axis_index("c")
    return x, y, c


def _other_chips(x, y, c):
    return [((1 - x, y, c), 2 * (1 - x) + y), ((x, 1 - y, c), 2 * x + 1 - y),
            ((1 - x, 1 - y, c), 2 * (1 - x) + 1 - y)]


def _region(ref, axis, size, half_axis, chip, core=None):
    idx = [slice(None)] * len(ref.shape)
    if core is None:
        idx[axis] = pl.ds(pl.multiple_of(chip * size, size), size)
    elif half_axis == axis:
        h = size // 2
        idx[axis] = pl.ds(pl.multiple_of(chip * size + core * h, h), h)
    else:
        idx[axis] = pl.ds(pl.multiple_of(chip * size, size), size)
        h = ref.shape[half_axis] // 2
        idx[half_axis] = pl.ds(pl.multiple_of(core * h, h), h)
    return ref.at[tuple(idx)]


def _half(ref, half_axis, core):
    idx = [slice(None)] * len(ref.shape)
    h = ref.shape[half_axis] // 2
    idx[half_axis] = pl.ds(pl.multiple_of(core * h, h), h)
    return ref.at[tuple(idx)]


class _Copy:
    def __init__(self, make):
        self._make = make

    def start(self):
        self._make().start()

    def wait(self):
        self._make().wait()

    def wait_send(self):
        self._make().wait_send()

    def wait_recv(self):
        self._make().wait_recv()


def _remote(src, dst, send_sem, recv_sem, dev):
    return _Copy(lambda: pltpu.make_async_remote_copy(
        src_ref=src, dst_ref=dst, send_sem=send_sem, recv_sem=recv_sem, device_id=dev, device_id_type=MESH))


def _local(src, dst, sem):
    return _Copy(lambda: pltpu.make_async_copy(src, dst, sem))


def _dma_sems(n):
    return pltpu.SemaphoreType.DMA((n,))


def _place_shard(w, chip, axis, *, name):
    shape = list(w.shape)
    shape[axis] *= N_CHIPS
    if w.ndim == 3:
        block, grid = (1,) + w.shape[1:], (w.shape[0],)
        in_map, out_map = (lambda i, chip: (i, 0, 0)), (lambda i, chip: (i, chip[0], 0))
    else:
        tr = _row_tile(w.shape[0])
        nt = w.shape[0] // tr
        block, grid = (tr, w.shape[1]), (nt,)
        in_map = lambda i, chip: (i, 0)
        out_map = (lambda i, chip: (chip[0] * nt + i, 0)) if axis == 0 else (lambda i, chip: (i, chip[0]))

    def body(chip_ref, w_ref, o_ref):
        o_ref[...] = _bf(w_ref[...])

    return _call(body, name=name, grid=grid, prefetch=1, in_specs=[pl.BlockSpec(block, in_map)],
                 out_specs=pl.BlockSpec(block, out_map),
                 out_shape=jax.ShapeDtypeStruct(tuple(shape), BF16))(chip, w)


def _ici_leg(srcs, dsts, layout, sizes, n_whole, sems):
    send_sems, recv_sems, local_sems = sems
    x, y, c = _mesh_pos()
    mine = 2 * x + y
    n_big = len(srcs) - n_whole
    local, sends, recvs = [], [], []
    for t, (src, dst) in enumerate(zip(srcs, dsts)):
        if t < n_big:
            ax, hx = layout[t]
            part = _region(src, ax, sizes[t], hx, mine, c)
            landing = lambda chip, dst=dst, ax=ax, hx=hx, size=sizes[t]: _region(dst, ax, size, hx, chip, c)
        else:
            part, landing = src, (lambda chip, dst=dst: dst.at[chip])
            local.append(_local(src, dst.at[mine], local_sems.at[t - n_big]))
        for k, (dev, chip) in enumerate(_other_chips(x, y, c)):
            sends.append(_remote(part, landing(mine), send_sems.at[3 * t + k], recv_sems.at[3 * t + k], dev))
            recvs.append(_remote(part, landing(chip), send_sems.at[3 * t + k], recv_sems.at[3 * t + k], dev))
    return local, sends, recvs


def _d2d_leg(srcs, dsts, layout, sizes, sems):
    send_sems, recv_sems = sems
    x, y, c = _mesh_pos()
    sends, recvs = [], []
    for t, (src, dst) in enumerate(zip(srcs, dsts)):
        ax, hx = layout[t]
        for k, (_, chip) in enumerate(_other_chips(x, y, c)):
            sem = (send_sems.at[3 * t + k], recv_sems.at[3 * t + k])
            sends.append(_remote(_region(src, ax, sizes[t], hx, chip, c),
                                 _region(dst, ax, sizes[t], hx, chip, c), *sem, (x, y, 1 - c)))
            recvs.append(_remote(_region(src, ax, sizes[t], hx, chip, 1 - c),
                                 _region(dst, ax, sizes[t], hx, chip, 1 - c), *sem, (x, y, 1 - c)))
    return sends, recvs


def _gather_shapes(bufs, whole):
    return ([jax.ShapeDtypeStruct(b.shape, b.dtype) for b in bufs]
            + [jax.ShapeDtypeStruct((N_CHIPS,) + w.shape, w.dtype) for w in whole])


def _gather_ici(bufs, layout):
    n = len(bufs)
    sizes = [b.shape[ax] // N_CHIPS for b, (ax, _) in zip(bufs, layout)]

    def start(ins, outs, sems):
        for cp in _ici_leg(ins, outs, layout, sizes, 0, (*sems, None))[1]:
            cp.start()

    def finish(ins, outs, sems):
        _, sends, recvs = _ici_leg(ins, outs, layout, sizes, 0, (*sems, None))
        for cp in recvs:
            cp.wait_recv()
        for cp in sends:
            cp.wait_send()

    return _Comm(bufs, _gather_shapes(bufs, ()), [_dma_sems(3 * n), _dma_sems(3 * n)], start, finish,
                 aliases={i: i for i in range(n)})


def _both(a, b):
    ni, no, ns = len(a.ins), len(a.outs), len(a.sems)

    def start(ins, outs, sems):
        a.start(ins[:ni], outs[:no], sems[:ns])
        b.start(ins[ni:], outs[no:], sems[ns:])

    def finish(ins, outs, sems):
        a.finish(ins[:ni], outs[:no], sems[:ns])
        b.finish(ins[ni:], outs[no:], sems[ns:])

    aliases = {**a.aliases, **{ni + i: no + o for i, o in b.aliases.items()}}
    return _Comm(a.ins + b.ins, a.outs + b.outs, a.sems + b.sems, start, finish, aliases)


def _gather_d2d(bufs, layout, sizes):
    n = len(bufs)

    def start(ins, outs, sems):
        for cp in _d2d_leg(ins, outs, layout, sizes, sems)[0]:
            cp.start()

    def finish(ins, outs, sems):
        sends, recvs = _d2d_leg(ins, outs, layout, sizes, sems)
        for cp in recvs:
            cp.wait_recv()
        for cp in sends:
            cp.wait_send()

    return _Comm(bufs, [jax.ShapeDtypeStruct(b.shape, b.dtype) for b in bufs],
                 [_dma_sems(3 * n), _dma_sems(3 * n)], start, finish, aliases={i: i for i in range(n)})


def _norm_bf16(x, nw, *, name, tm):
    m, d = x.shape

    def body(x_ref, nw_ref, h_ref):
        h_ref[...] = _bf(_rms(x_ref[...])[0] * nw_ref[...])

    row = pl.BlockSpec((tm, d), lambda i: (i, 0))
    return _call(body, name=name, grid=(m // tm,), in_specs=[row, pl.BlockSpec((1, d), lambda i: (0, 0))],
                 out_specs=row, out_shape=jax.ShapeDtypeStruct((m, d), BF16))(x, nw)


def _in_proj_gather(h1, w_buf, later, later_cut, small, order, *, tm):
    m, d = h1.shape
    width = w_buf.shape[1] // N_CHIPS
    nr, nl = m // tm, len(later)
    sizes = [b.shape[ax] // N_CHIPS for b, (ax, _) in zip(later, later_cut)]

    def body(order_ref, h_ref, w_in, *rest):
        later_in, small_in = rest[:nl], rest[nl]
        o_ref, wt_ref, w_out = rest[nl + 1], rest[nl + 2], rest[nl + 3]
        later_out, small_out = rest[nl + 4:2 * nl + 4], rest[2 * nl + 4]
        wv, load_sem, ici_send, ici_recv, d2d_send, d2d_recv, l_send, l_recv, l_local = rest[2 * nl + 5:]
        s, i = pl.program_id(0), pl.program_id(1)
        x, y, c = _mesh_pos()
        mine = 2 * x + y
        peers = _other_chips(x, y, c)
        part = lambda ref, chip, core=None: _region(ref, 1, width, 0, chip, core)

        def ici(k):
            dev, chip = peers[k]
            sem = (ici_send.at[k], ici_recv.at[k])
            return (_remote(part(w_in, mine, c), part(w_out, mine, c), *sem, dev),
                    _remote(part(w_in, chip, c), part(w_out, chip, c), *sem, dev))

        def d2d(k):
            chip, sem, sib = peers[k][1], (d2d_send.at[k], d2d_recv.at[k]), (x, y, 1 - c)
            return (_remote(part(w_out, chip, c), part(w_out, chip, c), *sem, sib),
                    _remote(part(w_out, chip, 1 - c), part(w_out, chip, 1 - c), *sem, sib))

        def load(src, chip, slot):
            cp = _local(part(src, chip), wv.at[slot], load_sem.at[slot])
            cp.start()
            cp.wait()

        def others():
            return _ici_leg(list(later_in) + [small_in], list(later_out) + [small_out], later_cut, sizes, 1,
                            (l_send, l_recv, l_local))

        @pl.when(jnp.logical_and(s == 0, i == 0))
        def _():
            for k in range(3):
                ici(k)[0].start()
            local, sends, _ = others()
            for cp in local + sends:
                cp.start()
            load(w_in, mine, 0)

        o_ref[...] = _bf(_dot(h_ref[...], wv[s % 2]))

        @pl.when(i == 0)
        def _():
            wt_ref[...] = wv[s % 2].T

        @pl.when(i == nr - 1)
        def _():
            for k in range(3):
                @pl.when(s == k)
                def _(k=k):
                    ici(k)[1].wait_recv()
                    d2d(k)[0].start()
                    d2d(k)[1].wait_recv()
                    load(w_out, peers[k][1], (k + 1) % 2)

            @pl.when(s == 3)
            def _():
                for k in range(3):
                    ici(k)[0].wait_send()
                    d2d(k)[0].wait_send()
                local, sends, recvs = others()
                for cp in recvs:
                    cp.wait_recv()
                for cp in sends:
                    cp.wait_send()
                for cp in local:
                    cp.wait()

    any_spec = pl.BlockSpec(memory_space=pl.ANY)
    n_any = nl + 2
    outs = _call(
        body, name="in_proj", grid=(N_CHIPS, nr), prefetch=1,
        in_specs=[pl.BlockSpec((tm, d), lambda s, i, order: (i, 0))] + [any_spec] * n_any,
        out_specs=[pl.BlockSpec((tm, width), lambda s, i, order: (i, order[s])),
                   pl.BlockSpec((width, d), lambda s, i, order: (order[s], 0))] + [any_spec] * n_any,
        out_shape=[jax.ShapeDtypeStruct((m, w_buf.shape[1]), BF16), jax.ShapeDtypeStruct((w_buf.shape[1], d), BF16)]
        + _gather_shapes([w_buf] + list(later), [small]),
        scratch=[pltpu.VMEM((2, d, width), BF16), _dma_sems(2), _dma_sems(3), _dma_sems(3), _dma_sems(3),
                 _dma_sems(3), _dma_sems(3 * (nl + 1)), _dma_sems(3 * (nl + 1)), _dma_sems(1)],
        aliases={2 + t: 2 + t for t in range(nl + 1)},
    )(order, h1, w_buf, *later, small)
    return outs[0], outs[1], outs[2], list(outs[3:3 + nl]), outs[3 + nl]


def _exchange(grads, layout):
    n = len(grads)
    others = N_DEV - 1
    sizes = [g.shape[ax] // N_CHIPS for g, (ax, _) in zip(grads, layout)]
    out_shapes = []
    for g, (ax, hx), sz in zip(grads, layout, sizes):
        shape = list(g.shape)
        shape[ax] = sz
        shape[hx] //= 2
        out_shapes.append(jax.ShapeDtypeStruct((others,) + tuple(shape), g.dtype))

    def copies(ins, outs, sems):
        send_sems, recv_sems = sems
        x, y, c = _mesh_pos()
        sends, recvs = [], []
        for t, (src, dst) in enumerate(zip(ins, outs)):
            ax, hx = layout[t]
            for r in range(1, N_DEV):
                px = (1 - x) if r & 4 else x
                py = (1 - y) if r & 2 else y
                pc = (1 - c) if r & 1 else c
                sem = (send_sems.at[others * t + r - 1], recv_sems.at[others * t + r - 1])
                part = _region(src, ax, sizes[t], hx, 2 * px + py, pc)
                sends.append(_remote(part, dst.at[r - 1], *sem, (px, py, pc)))
                recvs.append(_remote(part, dst.at[r - 1], *sem, (px, py, pc)))
        return sends, recvs

    def start(ins, outs, sems):
        for cp in copies(ins, outs, sems)[0]:
            cp.start()

    def finish(ins, outs, sems):
        sends, recvs = copies(ins, outs, sems)
        for cp in recvs:
            cp.wait_recv()
        for cp in sends:
            cp.wait_send()

    return _Comm(grads, out_shapes, [_dma_sems(others * n), _dma_sems(others * n)], start, finish)


def _reduce_half(g, parts, pos, cut, *, name):
    ax, _ = cut
    others = parts.shape[0]
    if g.ndim == 3:
        nb, rows, cols = g.shape
        hb = nb // 2
        block, grid, out_shape = (1, rows // N_CHIPS, cols), (hb,), (nb, rows // N_CHIPS, cols)
        g_map = lambda i, pos: (pos[1] * hb + i, pos[0], 0)
        o_map = lambda i, pos: (pos[1] * hb + i, 0, 0)
        p_map = lambda i, pos: (0, i, 0, 0)
    elif ax == 1:
        rows, cols = g.shape
        tr = _row_tile(rows // 2)
        nt = rows // 2 // tr
        block, grid, out_shape = (tr, cols // N_CHIPS), (nt,), (rows, cols // N_CHIPS)
        g_map = lambda i, pos: (pos[1] * nt + i, pos[0])
        o_map = lambda i, pos: (pos[1] * nt + i, 0)
        p_map = lambda i, pos: (0, i, 0)
    else:
        rows, cols = g.shape
        tr = _row_tile(rows // N_CHIPS // 2)
        nt = rows // N_CHIPS // 2 // tr
        block, grid, out_shape = (tr, cols), (nt,), (rows // N_CHIPS, cols)
        g_map = lambda i, pos: (pos[0] * 2 * nt + pos[1] * nt + i, 0)
        o_map = lambda i, pos: (pos[1] * nt + i, 0)
        p_map = lambda i, pos: (0, i, 0)

    def body(pos_ref, g_ref, p_ref, o_ref):
        acc = g_ref[...].astype(F32)
        for r in range(others):
            acc = acc + p_ref[r].astype(F32)
        o_ref[...] = acc

    return _call(
        body, name=name, grid=grid, prefetch=1,
        in_specs=[pl.BlockSpec(block, g_map), pl.BlockSpec((others,) + block, p_map)],
        out_specs=pl.BlockSpec(block, o_map), out_shape=jax.ShapeDtypeStruct(out_shape, F32),
    )(pos, g, parts)


def _join_halves(bufs):
    n = len(bufs)

    def copies(ins, outs, sems):
        send_sems, recv_sems = sems
        x, y, c = _mesh_pos()
        sends = [_remote(_half(src, 0, c), _half(dst, 0, c), send_sems.at[t], recv_sems.at[t], (x, y, 1 - c))
                 for t, (src, dst) in enumerate(zip(ins, outs))]
        recvs = [_remote(_half(src, 0, 1 - c), _half(dst, 0, 1 - c), send_sems.at[t], recv_sems.at[t],
                         (x, y, 1 - c)) for t, (src, dst) in enumerate(zip(ins, outs))]
        return sends, recvs

    def start(ins, outs, sems):
        for cp in copies(ins, outs, sems)[0]:
            cp.start()

    def finish(ins, outs, sems):
        sends, recvs = copies(ins, outs, sems)
        for cp in recvs:
            cp.wait_recv()
        for cp in sends:
            cp.wait_send()

    comm = _Comm(bufs, [jax.ShapeDtypeStruct(b.shape, b.dtype) for b in bufs],
                 [_dma_sems(n), _dma_sems(n)], start, finish, aliases={i: i for i in range(n)})
    return _call(None, name="join_halves", comm=comm)()[1]


def _allreduce_small(pack):
    rows, cols = pack.shape

    def body(p_ref, o_ref, slots, send_sems, recv_sems):
        x, y, c = _mesh_pos()
        me = 4 * x + 2 * y + c
        slots[me] = p_ref[...]
        copies = []
        for r in range(1, N_DEV):
            fx, fy, fc = (r >> 2) & 1, (r >> 1) & 1, r & 1
            dev = ((1 - x) if fx else x, (1 - y) if fy else y, (1 - c) if fc else c)
            cp = pltpu.make_async_remote_copy(
                src_ref=p_ref, dst_ref=slots.at[me], send_sem=send_sems.at[r - 1],
                recv_sem=recv_sems.at[r - 1], device_id=dev, device_id_type=MESH)
            cp.start()
            copies.append(cp)
        for cp in copies:
            cp.wait_recv()
        for cp in copies:
            cp.wait_send()
        acc = slots[0]
        for d in range(1, N_DEV):
            acc = acc + slots[d]
        o_ref[...] = acc

    vmem = pl.BlockSpec(memory_space=pltpu.VMEM)
    return _call(
        body, name="allreduce_small", in_specs=[vmem], out_specs=vmem,
        out_shape=jax.ShapeDtypeStruct((rows, cols), F32),
        scratch=[pltpu.VMEM((N_DEV, rows, cols), F32), pltpu.SemaphoreType.DMA((N_DEV - 1,)),
                 pltpu.SemaphoreType.DMA((N_DEV - 1,))],
    )(pack)


def _pad_rows(a, rows=8):
    return jnp.pad(a, ((0, rows - a.shape[0]), (0, 0)))


def kernel(x, positions, norm1_w, w_in, merge_gate_b, ret_gn_w, w_ret_o, lru_conv_w, lru_conv_b, lru_w_r, lru_b_r, lru_w_i, lru_b_i, lru_lambda, w_lru_o, w_out, norm2_w, ffn_w_up, ffn_conv_w, ffn_conv_b, ffn_w_down, norm_f_w, loss_target, m_norm1_w, m_w_in, m_merge_gate_b, m_ret_gn_w, m_w_ret_o, m_lru_conv_w, m_lru_conv_b, m_lru_w_r, m_lru_b_r, m_lru_w_i, m_lru_b_i, m_lru_lambda, m_w_lru_o, m_w_out, m_norm2_w, m_ffn_w_up, m_ffn_conv_w, m_ffn_conv_b, m_ffn_w_down, m_norm_f_w, v_norm1_w, v_w_in, v_merge_gate_b, v_ret_gn_w, v_w_ret_o, v_lru_conv_w, v_lru_conv_b, v_lru_w_r, v_lru_b_r, v_lru_w_i, v_lru_b_i, v_lru_lambda, v_w_lru_o, v_w_out, v_norm2_w, v_ffn_w_up, v_ffn_conv_w, v_ffn_conv_b, v_ffn_w_down, v_norm_f_w):
    names = ["norm1_w", "w_in", "merge_gate_b", "ret_gn_w", "w_ret_o", "lru_conv_w", "lru_conv_b", "lru_w_r",
             "lru_b_r", "lru_w_i", "lru_b_i", "lru_lambda", "w_lru_o", "w_out", "norm2_w", "ffn_w_up",
             "ffn_conv_w", "ffn_conv_b", "ffn_w_down", "norm_f_w"]
    w_args = dict(zip(names, [norm1_w, w_in, merge_gate_b, ret_gn_w, w_ret_o, lru_conv_w, lru_conv_b, lru_w_r,
                              lru_b_r, lru_w_i, lru_b_i, lru_lambda, w_lru_o, w_out, norm2_w, ffn_w_up,
                              ffn_conv_w, ffn_conv_b, ffn_w_down, norm_f_w]))
    m_args = dict(zip(names, [m_norm1_w, m_w_in, m_merge_gate_b, m_ret_gn_w, m_w_ret_o, m_lru_conv_w,
                              m_lru_conv_b, m_lru_w_r, m_lru_b_r, m_lru_w_i, m_lru_b_i, m_lru_lambda, m_w_lru_o,
                              m_w_out, m_norm2_w, m_ffn_w_up, m_ffn_conv_w, m_ffn_conv_b, m_ffn_w_down,
                              m_norm_f_w]))
    v_args = dict(zip(names, [v_norm1_w, v_w_in, v_merge_gate_b, v_ret_gn_w, v_w_ret_o, v_lru_conv_w,
                              v_lru_conv_b, v_lru_w_r, v_lru_b_r, v_lru_w_i, v_lru_b_i, v_lru_lambda, v_w_lru_o,
                              v_w_out, v_norm2_w, v_ffn_w_up, v_ffn_conv_w, v_ffn_conv_b, v_ffn_w_down,
                              v_norm_f_w]))

    bsz, seq, d = x.shape
    m = bsz * seq
    tm = min(MM_ROWS, m)
    tm_fused = min(FUSED_ROWS, m)
    tm_tall = min(TALL_ROWS, m)
    chip = 2 * lax.axis_index("x") + lax.axis_index("y")

    big = ["w_in", "w_ret_o", "w_lru_o", "w_out", "lru_w_r", "lru_w_i", "ffn_w_up", "ffn_w_down"]
    cut = dict(w_in=(1, 0), w_ret_o=(0, 0), w_lru_o=(0, 0), w_out=(0, 0), lru_w_r=(1, 0), lru_w_i=(1, 0),
               ffn_w_up=(1, 0), ffn_w_down=(0, 0))
    core = lax.axis_index("c")
    chip1 = jnp.reshape(chip, (1,)).astype(jnp.int32)
    pos = jnp.stack([chip, core]).astype(jnp.int32)
    placed = {n: _place_shard(w_args[n][0], chip1, cut[n][0], name="place_" + n) for n in big}
    small_pack = jnp.concatenate([
        jnp.pad(merge_gate_b[0], ((0, 6), (0, 512))),
        jnp.pad(lru_conv_w[0], ((0, 4), (0, 512))),
        jnp.pad(lru_b_r[0], ((0, 4), (0, 704))),
        jnp.pad(lru_b_i[0], ((0, 4), (0, 704))),
        jnp.pad(ffn_conv_w[0], ((0, 5), (0, 0))),
    ], axis=0)
    x2d = x.reshape(m, d)
    mx, my = lax.axis_index("x"), lax.axis_index("y")
    order = jnp.stack([chip, 2 * (1 - mx) + my, 2 * mx + 1 - my, 2 * (1 - mx) + 1 - my]).astype(jnp.int32)
    mixer = ["w_ret_o", "w_lru_o", "w_out", "lru_w_r", "lru_w_i"]
    cuts = lambda ns: [cut[n] for n in ns]
    sizes = lambda ns: [w_args[n].shape[1 + cut[n][0]] for n in ns]
    h1 = _norm_bf16(x2d, norm1_w, name="norm1", tm=tm)
    proj, w_in_t, w_in_full, bufs, sp = _in_proj_gather(h1, placed["w_in"], [placed[n] for n in mixer], cuts(mixer),
                                               small_pack, order, tm=tm_tall)
    wb = {"w_in": w_in_full}
    mgb = jnp.transpose(sp[:, 0:2, 0:256], (1, 0, 2)).reshape(2, D_MODEL)
    lcw = jnp.transpose(sp[:, 8:12, 0:256], (1, 0, 2)).reshape(4, D_MODEL)
    lbr = jnp.transpose(sp[:, 16:20, 0:64], (1, 0, 2)).reshape(1, D_MODEL)
    lbi = jnp.transpose(sp[:, 24:28, 0:64], (1, 0, 2)).reshape(1, D_MODEL)
    fcw = jnp.transpose(sp[:, 32:35, :], (1, 0, 2)).reshape(3, D_FF)
    nfw = norm_f_w.reshape(1, D_MODEL)

    half = RET_DK // 2
    inv_freq = ROPE_BASE ** (-jnp.arange(half, dtype=F32) / half)
    cos, sin = _rope_tables(positions.reshape(bsz, seq, 1), jnp.concatenate([inv_freq, inv_freq]).reshape(1, RET_DK))
    proj3 = proj.reshape(bsz, seq, D_IN)
    down, up_w = ["ffn_w_down"], ["ffn_w_up"]
    (a_in3, states), bufs = _retention_fwd(
        proj3, cos, sin, ret_gn_w,
        comm=_both(_gather_d2d(bufs, cuts(mixer), sizes(mixer)), _gather_ici([placed["ffn_w_down"]], cuts(down))))
    wb.update(zip(mixer, bufs[:len(mixer)]))
    lru_params = (lcw, lru_conv_b, wb["lru_w_r"], lbr, wb["lru_w_i"], lbi, lru_lambda)
    (b_in3, *lru_kept), (up_buf, wb["ffn_w_down"]) = _lru_fwd(
        proj3, lru_params,
        comm=_both(_gather_ici([placed["ffn_w_up"]], cuts(up_w)), _gather_d2d(bufs[len(mixer):], cuts(down), sizes(down))))
    a_in, b_in = a_in3.reshape(m, d), b_in3.reshape(m, d)
    (x2, mix, ya, yb), (wb["ffn_w_up"],) = _merge_fwd(
        a_in, b_in, proj, mgb, wb["w_ret_o"], wb["w_lru_o"], wb["w_out"], x2d, tm=tm_fused,
        comm=_gather_d2d([up_buf], cuts(up_w), sizes(up_w)))
    up, h2 = _norm_matmul(x2, norm2_w, wb["ffn_w_up"], name="ffn_up", tm=tm_tall, tn=TALL_COLS)
    up3 = up.reshape(bsz, seq, 2 * D_FF)
    f3, act3, q3 = _ffn_act_fwd(up3, fcw, ffn_conv_b)
    f = f3.reshape(m, D_FF)
    loss_dev, dx3, sm_nf = _ffn_down_loss(f, wb["ffn_w_down"], x2, nfw, loss_target.reshape(m, d), tm=tm)
    loss = lax.psum(loss_dev[0, 0], ("x", "y", "c"))

    def send(*ns):
        return _exchange([g_full[n] for n in ns], [cut[n] for n in ns])

    g_full, parts = {}, {}
    wt = {n: jnp.transpose(wb[n]) for n in ("ffn_w_down", "ffn_w_up", "w_out", "w_ret_o", "w_lru_o")}
    df = _mm(dx3, wt["ffn_w_down"], name="ffn_down_dx", tm=tm_tall, out_dtype=BF16)
    g_full["ffn_w_down"] = _mm_tn(f, [dx3], name="ffn_down_dw", tm=tm_tall)
    (dgate3, dval3, sm_ffn), (parts["ffn_w_down"],) = _ffn_act_bwd(
        up3, act3, q3, fcw, df.reshape(bsz, seq, D_FF), comm=send("ffn_w_down"))
    dup = [dgate3.reshape(m, D_FF), dval3.reshape(m, D_FF)]
    g_full["ffn_w_up"] = _mm_tn(h2, dup, name="ffn_up_dw", tm=tm_tall)
    (dx2, sm_n2), (parts["ffn_w_up"],) = _mm_normbwd(
        dup, wt["ffn_w_up"], x2, norm2_w, dx3, name="ffn_up_dx", tm=tm, row=ROW_N2, comm=send("ffn_w_up"))
    dya, dyb, da_in, db_in, dgates, sm_mg = _merge_bwd(
        ya, yb, proj, mgb, wt["w_ret_o"], wt["w_lru_o"], wt["w_out"], dx2, tm=tm_fused)
    g_full["w_out"] = _mm_tn(mix, [dx2], name="out_dw", tm=tm_tall)
    g_full["w_ret_o"] = _mm_tn(a_in, [dya], name="ret_o_dw", tm=tm_tall)
    g_full["w_lru_o"] = _mm_tn(b_in, [dyb], name="lru_o_dw", tm=tm_tall)
    (dlru3, dwr, dwi, sm_lru), (parts["w_out"], parts["w_ret_o"], parts["w_lru_o"]) = _lru_bwd(
        proj3, lru_params, lru_kept, db_in.reshape(bsz, seq, d), comm=send("w_out", "w_ret_o", "w_lru_o"))
    g_full["lru_w_r"], g_full["lru_w_i"] = dwr.astype(BF16), dwi.astype(BF16)
    (dret3, sm_gn), (parts["lru_w_r"], parts["lru_w_i"]) = _retention_bwd(
        proj3, cos, sin, ret_gn_w, states, da_in.reshape(bsz, seq, d), comm=send("lru_w_r", "lru_w_i"))
    dproj = [dret3.reshape(m, 3072), dlru3.reshape(m, 2048), dgates]
    g_full["w_in"] = _mm_tn(h1, dproj, name="in_proj_dw", tm=tm_tall)
    (grad_x, sm_n1), (parts["w_in"],) = _mm_normbwd(
        dproj, w_in_t, x2d, norm1_w, dx2, name="in_proj_dx", tm=tm, row=ROW_N1, comm=send("w_in"))

    reduced = _join_halves([_reduce_half(g_full[n], parts[n], pos, cut[n], name="sum_" + n) for n in big])
    misc = sm_n1 + sm_mg + sm_gn + sm_n2 + sm_nf
    pack = jnp.concatenate(
        [misc, sm_lru, sm_ffn[:, 0:1024], sm_ffn[:, 1024:2048], sm_ffn[:, 2048:3072]], axis=0)
    tot = _allreduce_small(pack)
    ffn_sm = jnp.concatenate([tot[16:24], tot[24:32], tot[32:40]], axis=1)
    g_small = {
        "norm1_w": tot[ROW_N1:ROW_N1 + 1], "merge_gate_b": tot[ROW_MGB:ROW_MGB + 2],
        "ret_gn_w": tot[ROW_GN:ROW_GN + 1], "norm2_w": tot[ROW_N2:ROW_N2 + 1], "norm_f_w": tot[ROW_NF:ROW_NF + 1],
        "lru_conv_w": tot[8:12], "lru_conv_b": tot[12:13], "lru_b_r": tot[13:14].reshape(4, 256),
        "lru_b_i": tot[14:15].reshape(4, 256), "lru_lambda": tot[15:16],
        "ffn_conv_w": ffn_sm[0:3], "ffn_conv_b": ffn_sm[3:4],
    }
    small_shard = dict(merge_gate_b=256, lru_conv_w=256, lru_b_r=64, lru_b_i=64, ffn_conv_w=768)

    outs = {}
    for n, g in zip(big, reduced):
        shape = w_args[n].shape
        g = g.reshape(-1, g.shape[-1])
        outs[n] = [o.reshape(shape) for o in _adamw(
            w_args[n].reshape(g.shape), [g], m_args[n].reshape(g.shape), v_args[n].reshape(g.shape),
            name="adamw_" + n)]
    for n, g in g_small.items():
        shape = w_args[n].shape
        if n in small_shard:
            g = lax.dynamic_slice_in_dim(g, chip * small_shard[n], small_shard[n], axis=1)
        w2 = w_args[n].reshape(g.shape)
        outs[n] = [o.reshape(shape) for o in _adamw(
            w2, [g], m_args[n].reshape(g.shape), v_args[n].reshape(g.shape), name="adamw_" + n)]

    result = [loss, grad_x.reshape(bsz, seq, d)]
    for k in range(4):
        result += [outs[n][k] for n in names]
    return tuple(result)
```

```python
import functools
import math

import numpy as np
import jax
import jax.numpy as jnp
from jax import lax
from jax.experimental import pallas as pl
from jax.experimental.pallas import tpu as pltpu

F32 = jnp.float32
BF16 = jnp.bfloat16

D_MODEL = 1024
RET_HEADS = 4
RET_DK = 128
RET_DV = 256
LRU_BLOCKS = 4
LRU_BLOCK = 256
LRU_C = 8.0
D_FF = 3072
D_IN = 7168
ROPE_BASE = 10000.0
RMS_EPS = 1e-6
GN_EPS = 1e-6
ADAM_LR, ADAM_B1, ADAM_B2, ADAM_EPS, ADAM_WD, ADAM_STEP = 0.001, 0.9, 0.999, 1e-08, 0.01, 10

N_CHIPS = 4
N_DEV = 8
SEQ_T = 256
REF_CHUNK = 64
COL = 1024
MM_ROWS = 1024
TALL_ROWS, TALL_COLS = 2048, 1024
FUSED_ROWS = 512
VMEM_LIMIT_BYTES = 56 * 1024 * 1024
MESH = pl.DeviceIdType.MESH
ROW_N1, ROW_MGB, ROW_GN, ROW_N2, ROW_NF = 0, 1, 3, 4, 5
GELU_K = math.sqrt(2.0 / math.pi)
GELU_C = 0.044715


class _Comm:
    def __init__(self, ins, outs, sems, start, finish, aliases=None):
        self.ins, self.outs, self.sems = list(ins), list(outs), list(sems)
        self.start, self.finish, self.aliases = start, finish, dict(aliases or {})


def _call(body, *, name, out_shape=(), grid=None, in_specs=(), out_specs=(), scratch=(), comm=None, prefetch=0,
          aliases=None):
    single = not isinstance(out_shape, (list, tuple))
    out_shape = [out_shape] if single else list(out_shape)
    out_specs = [out_specs] if single else list(out_specs)
    in_specs, scratch = list(in_specs), list(scratch)
    n_in, n_out, n_scr = len(in_specs), len(out_shape), len(scratch)
    kwargs = dict(name=name, compiler_params=pltpu.CompilerParams(vmem_limit_bytes=VMEM_LIMIT_BYTES))
    if prefetch:
        assert comm is None
        spec = pltpu.PrefetchScalarGridSpec(num_scalar_prefetch=prefetch, grid=grid, in_specs=in_specs,
                                            out_specs=out_specs, scratch_shapes=scratch)
        fn = pl.pallas_call(body, out_shape=out_shape, grid_spec=spec, input_output_aliases=dict(aliases or {}),
                            **kwargs)
        return (lambda *args: fn(*args)[0]) if single else fn
    if grid is not None:
        kwargs["grid"] = grid
    if comm is None:
        fn = pl.pallas_call(body, out_shape=out_shape, in_specs=in_specs, out_specs=out_specs,
                            scratch_shapes=scratch, **kwargs)
        return (lambda *args: fn(*args)[0]) if single else fn

    any_spec = pl.BlockSpec(memory_space=pl.ANY)
    n_cin, n_cout = len(comm.ins), len(comm.outs)

    def wrapped(*refs):
        ins, refs = refs[:n_in], refs[n_in:]
        cins, refs = refs[:n_cin], refs[n_cin:]
        outs, refs = refs[:n_out], refs[n_out:]
        couts, refs = refs[:n_cout], refs[n_cout:]
        scr, csems = refs[:n_scr], refs[n_scr:]
        if grid is None:
            comm.start(cins, couts, csems)
            comm.finish(cins, couts, csems)
            return
        ids = [pl.program_id(a) for a in range(len(grid))]
        first = functools.reduce(jnp.logical_and, [i == 0 for i in ids])
        last = functools.reduce(jnp.logical_and, [i == g - 1 for i, g in zip(ids, grid)])
        pl.when(first)(lambda: comm.start(cins, couts, csems))
        body(*ins, *outs, *scr)
        pl.when(last)(lambda: comm.finish(cins, couts, csems))

    fn = pl.pallas_call(
        wrapped, out_shape=out_shape + comm.outs, in_specs=in_specs + [any_spec] * n_cin,
        out_specs=out_specs + [any_spec] * n_cout, scratch_shapes=scratch + comm.sems,
        input_output_aliases={n_in + i: n_out + o for i, o in comm.aliases.items()}, **kwargs)

    def run(*args):
        res = fn(*args, *comm.ins)
        own = res[0] if single else list(res[:n_out])
        return own, list(res[n_out:])

    return run


def _dot(a, b):
    return jnp.dot(a, b, preferred_element_type=F32)


def _dot_nt(a, b):
    return lax.dot_general(a, b, (((1,), (1,)), ((), ())), preferred_element_type=F32)


def _dot_tn(a, b):
    return lax.dot_general(a, b, (((0,), (0,)), ((), ())), preferred_element_type=F32)


def _bf(x):
    return x.astype(BF16)


def _sigmoid(x):
    return 1.0 / (1.0 + jnp.exp(-x))


def _gelu_and_grad(x):
    x2 = x * x
    s = _sigmoid(x * (2.0 * GELU_K * GELU_C * x2 + 2.0 * GELU_K))
    g = x * s
    dg = s + g * (1.0 - s) * (6.0 * GELU_K * GELU_C * x2 + 2.0 * GELU_K)
    return g, dg


def _rms(x):
    r = lax.rsqrt(jnp.mean(x * x, axis=-1, keepdims=True) + RMS_EPS)
    return x * r, r


def _rms_bwd(dy, x, nw):
    xh, r = _rms(x)
    g = dy * nw
    dx = r * (g - xh * jnp.mean(g * xh, axis=-1, keepdims=True))
    return dx, jnp.sum(dy * xh, axis=0, keepdims=True)


def _row_acc(ref, row, val):
    ref[row:row + 1, :] = ref[row:row + 1, :] + val


def _shift_down(x, j, prev8):
    if j == 0:
        return x
    n = x.shape[0] // 8
    row = lax.broadcasted_iota(jnp.int32, prev8.shape, 0)
    turned = [pltpu.roll(prev8, j, 0)] + [pltpu.roll(x[8 * k:8 * k + 8], j, 0) for k in range(n)]
    return jnp.concatenate([jnp.where(row < j, turned[k], turned[k + 1]) for k in range(n)], axis=0)


def _shift_up(x, j, next8):
    if j == 0:
        return x
    n = x.shape[0] // 8
    row = lax.broadcasted_iota(jnp.int32, next8.shape, 0)
    turned = [pltpu.roll(x[8 * k:8 * k + 8], 8 - j, 0) for k in range(n)] + [pltpu.roll(next8, 8 - j, 0)]
    return jnp.concatenate([jnp.where(row >= 8 - j, turned[k + 1], turned[k]) for k in range(n)], axis=0)


def _scan_fwd(a, b, carry):
    row = lax.broadcasted_iota(jnp.int32, (8, a.shape[1]), 0)
    out = []
    for k in range(a.shape[0] // 8):
        ak, bk = a[8 * k:8 * k + 8], b[8 * k:8 * k + 8]
        for s in (1, 2, 4):
            keep = row >= s
            ar, br = pltpu.roll(ak, s, 0), pltpu.roll(bk, s, 0)
            bk = jnp.where(keep, ak * br + bk, bk)
            ak = jnp.where(keep, ak * ar, ak)
        hk = ak * carry + bk
        carry = hk[7:8]
        out.append(hk)
    return jnp.concatenate(out, axis=0)


def _scan_bwd(a, b, carry):
    row = lax.broadcasted_iota(jnp.int32, (8, a.shape[1]), 0)
    out = []
    for k in reversed(range(a.shape[0] // 8)):
        ak, bk = a[8 * k:8 * k + 8], b[8 * k:8 * k + 8]
        for s in (1, 2, 4):
            keep = row < 8 - s
            ar, br = pltpu.roll(ak, 8 - s, 0), pltpu.roll(bk, 8 - s, 0)
            bk = jnp.where(keep, ak * br + bk, bk)
            ak = jnp.where(keep, ak * ar, ak)
        gk = bk + ak * carry
        carry = gk[0:1]
        out.append(gk)
    return jnp.concatenate(out[::-1], axis=0)


def _norm_matmul(x, nw, w, *, name, tm, tn):
    m, d = x.shape
    n = w.shape[1]

    def body(x_ref, nw_ref, w_ref, o_ref, h_ref, h_sc):
        @pl.when(pl.program_id(1) == 0)
        def _():
            xh, _ = _rms(x_ref[...])
            h = _bf(xh * nw_ref[...])
            h_sc[...] = h
            h_ref[...] = h

        o_ref[...] = _bf(_dot(h_sc[...], w_ref[...]))

    return _call(
        body, name=name, grid=(m // tm, n // tn),
        in_specs=[pl.BlockSpec((tm, d), lambda i, j: (i, 0)),
                  pl.BlockSpec((1, d), lambda i, j: (0, 0)),
                  pl.BlockSpec((d, tn), lambda i, j: (0, j))],
        out_specs=[pl.BlockSpec((tm, tn), lambda i, j: (i, j)),
                   pl.BlockSpec((tm, d), lambda i, j: (i, 0))],
        out_shape=[jax.ShapeDtypeStruct((m, n), BF16), jax.ShapeDtypeStruct((m, d), BF16)],
        scratch=[pltpu.VMEM((tm, d), BF16)],
    )(x, nw, w)


def _mm_nt(a, w, *, name, tm, out_dtype):
    m, k = a.shape
    n = w.shape[0]

    def body(a_ref, w_ref, o_ref):
        o_ref[...] = _dot_nt(_bf(a_ref[...]), w_ref[...]).astype(out_dtype)

    return _call(
        body, name=name, grid=(m // tm, n // COL),
        in_specs=[pl.BlockSpec((tm, k), lambda i, j: (i, 0)),
                  pl.BlockSpec((COL, k), lambda i, j: (j, 0))],
        out_specs=pl.BlockSpec((tm, COL), lambda i, j: (i, j)),
        out_shape=jax.ShapeDtypeStruct((m, n), out_dtype),
    )(a, w)


def _piece_layout(pieces):
    offs, nblk, o = [], [], 0
    for p in pieces:
        offs.append(o)
        nblk.append(p.shape[1] // COL)
        o += p.shape[1] // COL
    return offs, nblk, o


def _mm_tn(a, pieces, *, name, tm, out_dtype=BF16):
    m, k = a.shape
    offs, nblk, nn = _piece_layout(pieces)

    def piece_spec(o, nb):
        def idx(ki, nj, mi):
            use = jnp.logical_and(nj >= o, nj < o + nb)
            return (jnp.where(use, mi, 0), jnp.clip(nj - o, 0, nb - 1))
        return pl.BlockSpec((tm, COL), idx)

    def body(a_ref, *rest):
        p_refs, o_ref, acc = rest[:len(pieces)], rest[len(pieces)], rest[len(pieces) + 1]
        nj, mi = pl.program_id(1), pl.program_id(2)

        @pl.when(mi == 0)
        def _():
            acc[...] = jnp.zeros_like(acc)

        for p_ref, o, nb in zip(p_refs, offs, nblk):
            @pl.when(jnp.logical_and(nj >= o, nj < o + nb))
            def _(p_ref=p_ref):
                acc[...] += _dot_tn(_bf(a_ref[...]), _bf(p_ref[...]))

        @pl.when(mi == pl.num_programs(2) - 1)
        def _():
            o_ref[...] = acc[...].astype(out_dtype)

    return _call(
        body, name=name, grid=(k // COL, nn, m // tm),
        in_specs=[pl.BlockSpec((tm, COL), lambda ki, nj, mi: (mi, ki))]
        + [piece_spec(o, nb) for o, nb in zip(offs, nblk)],
        out_specs=pl.BlockSpec((COL, COL), lambda ki, nj, mi: (ki, nj)),
        out_shape=jax.ShapeDtypeStruct((k, nn * COL), out_dtype),
        scratch=[pltpu.VMEM((COL, COL), F32)],
    )(a, *pieces)


def _mm_nt_normbwd(pieces, w, x, nw, dres, *, name, tm, row, comm=None):
    m, d = x.shape
    offs, nblk, nk = _piece_layout(pieces)

    def piece_spec(o, nb):
        return pl.BlockSpec((tm, COL), lambda i, k: (i, jnp.clip(k - o, 0, nb - 1)))

    def body(*refs):
        p_refs = refs[:len(pieces)]
        w_ref, x_ref, nw_ref, dres_ref, dx_ref, dnw_ref, acc = refs[len(pieces):]
        i, k = pl.program_id(0), pl.program_id(1)

        @pl.when(jnp.logical_and(i == 0, k == 0))
        def _():
            dnw_ref[...] = jnp.zeros_like(dnw_ref)

        @pl.when(k == 0)
        def _():
            acc[...] = jnp.zeros_like(acc)

        for p_ref, o, nb in zip(p_refs, offs, nblk):
            @pl.when(jnp.logical_and(k >= o, k < o + nb))
            def _(p_ref=p_ref):
                acc[...] += _dot_nt(_bf(p_ref[...]), w_ref[...])

        @pl.when(k == nk - 1)
        def _():
            dx, dnw = _rms_bwd(acc[...], x_ref[...], nw_ref[...])
            dx_ref[...] = dres_ref[...] + dx
            _row_acc(dnw_ref, row, dnw)

    return _call(
        body, name=name, grid=(m // tm, nk), comm=comm,
        in_specs=[piece_spec(o, nb) for o, nb in zip(offs, nblk)]
        + [pl.BlockSpec((d, COL), lambda i, k: (0, k)),
           pl.BlockSpec((tm, d), lambda i, k: (i, 0)),
           pl.BlockSpec((1, d), lambda i, k: (0, 0)),
           pl.BlockSpec((tm, d), lambda i, k: (i, 0))],
        out_specs=[pl.BlockSpec((tm, d), lambda i, k: (i, 0)),
                   pl.BlockSpec((8, d), lambda i, k: (0, 0))],
        out_shape=[jax.ShapeDtypeStruct((m, d), F32), jax.ShapeDtypeStruct((8, d), F32)],
        scratch=[pltpu.VMEM((tm, d), F32)],
    )(*pieces, w, x, nw, dres)


def _rope_tables(pos3, invf):
    b, s, _ = pos3.shape

    def body(pos_ref, invf_ref, cos_ref, sin_ref):
        ang = pos_ref[...].astype(F32) * invf_ref[...]
        lane = lax.broadcasted_iota(jnp.int32, ang.shape, 1)
        cos_ref[...] = jnp.cos(ang)
        sin_ref[...] = jnp.where(lane < RET_DK // 2, -1.0, 1.0) * jnp.sin(ang)

    spec = pl.BlockSpec((None, SEQ_T, RET_DK), lambda i, c: (i, c, 0))
    return _call(
        body, name="rope_tables", grid=(b, s // SEQ_T),
        in_specs=[pl.BlockSpec((None, SEQ_T, 1), lambda i, c: (i, c, 0)),
                  pl.BlockSpec((1, RET_DK), lambda i, c: (0, 0))],
        out_specs=[spec, spec],
        out_shape=[jax.ShapeDtypeStruct((b, s, RET_DK), F32)] * 2,
    )(pos3, invf)


def _log_gamma(h):
    return float(np.log1p(-np.power(np.float32(2.0), np.float32(-5.0 - h))).astype(np.float32))


def _decay_matrix(h):
    lg = _log_gamma(h)
    n = lax.broadcasted_iota(jnp.int32, (SEQ_T, SEQ_T), 0)
    m = lax.broadcasted_iota(jnp.int32, (SEQ_T, SEQ_T), 1)
    same = (n // REF_CHUNK) == (m // REF_CHUNK)
    dist = jnp.where(same, jnp.abs(n - m), n - m).astype(F32)
    return jnp.where(jnp.logical_or(same, m < n), jnp.exp(lg * dist), 0.0)


def _decay_vectors(h):
    lg = _log_gamma(h)
    idx = lax.broadcasted_iota(jnp.int32, (SEQ_T, 1), 0).astype(F32)
    qd = jnp.exp(lg * (idx + 1.0))
    kd = jnp.exp(lg * (SEQ_T - 1.0 - idx))
    return qd, kd, math.exp(lg * SEQ_T)


def _rotate(x, cos, sin):
    return x * cos + pltpu.roll(x, RET_DK // 2, 1) * sin


def _rotate_bwd(d, cos, sin):
    return d * cos + pltpu.roll(d * sin, RET_DK // 2, 1)


def _ret_head(p_ref, cos, sin, h):
    q = p_ref[:, h * RET_DK:(h + 1) * RET_DK].astype(F32)
    k = p_ref[:, 512 + h * RET_DK:512 + (h + 1) * RET_DK].astype(F32)
    v = p_ref[:, 1024 + h * RET_DV:1024 + (h + 1) * RET_DV]
    g = p_ref[:, 2048 + h * RET_DV:2048 + (h + 1) * RET_DV].astype(F32)
    qr = _rotate(q, cos, sin)
    kr = _rotate(k, cos, sin) * (RET_DK ** -0.5)
    return qr, kr, v, g


def _group_norm(o):
    mu = jnp.mean(o, axis=-1, keepdims=True)
    oc = o - mu
    rstd = lax.rsqrt(jnp.mean(oc * oc, axis=-1, keepdims=True) + GN_EPS)
    return oc * rstd, rstd


def _retention_fwd(proj3, cos, sin, gnw, comm=None):
    b, s, _ = proj3.shape
    nc = s // SEQ_T

    def body(p_ref, cos_ref, sin_ref, gnw_ref, a_ref, st_ref, state, wtab):
        c = pl.program_id(1)

        @pl.when(jnp.logical_and(pl.program_id(0) == 0, c == 0))
        def _():
            for h in range(RET_HEADS):
                wtab[h] = _decay_matrix(h)

        @pl.when(c == 0)
        def _():
            state[...] = jnp.zeros_like(state)

        cs, sn = cos_ref[...], sin_ref[...]
        for h in range(RET_HEADS):
            qd, kd, gt = _decay_vectors(h)
            qr, kr, v, g = _ret_head(p_ref, cs, sn, h)
            st = state[h]
            st_ref[h] = st
            p = _dot_nt(_bf(qr), _bf(kr)) * wtab[h]
            o = _dot(_bf(p), _bf(v)) + _dot(_bf(qr * qd), _bf(st))
            state[h] = st * gt + _dot_tn(_bf(kr * kd), _bf(v))
            on, _ = _group_norm(o)
            gw = gnw_ref[:, h * RET_DV:(h + 1) * RET_DV]
            a_ref[:, h * RET_DV:(h + 1) * RET_DV] = _bf(on * gw * (g * _sigmoid(g)))

    tab = pl.BlockSpec((None, SEQ_T, RET_DK), lambda i, c: (i, c, 0))
    return _call(
        body, name="retention_fwd", grid=(b, nc), comm=comm,
        in_specs=[pl.BlockSpec((None, SEQ_T, 3072), lambda i, c: (i, c, 0)), tab, tab,
                  pl.BlockSpec((1, D_MODEL), lambda i, c: (0, 0))],
        out_specs=[pl.BlockSpec((None, SEQ_T, D_MODEL), lambda i, c: (i, c, 0)),
                   pl.BlockSpec((None, None, RET_HEADS, RET_DK, RET_DV), lambda i, c: (i, c, 0, 0, 0))],
        out_shape=[jax.ShapeDtypeStruct((b, s, D_MODEL), BF16),
                   jax.ShapeDtypeStruct((b, nc, RET_HEADS, RET_DK, RET_DV), F32)],
        scratch=[pltpu.VMEM((RET_HEADS, RET_DK, RET_DV), F32),
                 pltpu.VMEM((RET_HEADS, SEQ_T, SEQ_T), F32)],
    )(proj3, cos, sin, gnw)


def _retention_bwd(proj3, cos, sin, gnw, states, da3, comm=None):
    b, s, _ = proj3.shape
    nc = s // SEQ_T

    def body(p_ref, cos_ref, sin_ref, gnw_ref, st_ref, da_ref, d_ref, dgn_ref, dstate, wtab):
        c = pl.program_id(1)

        @pl.when(jnp.logical_and(pl.program_id(0) == 0, c == 0))
        def _():
            dgn_ref[...] = jnp.zeros_like(dgn_ref)
            for h in range(RET_HEADS):
                wtab[h] = _decay_matrix(h)

        @pl.when(c == 0)
        def _():
            dstate[...] = jnp.zeros_like(dstate)

        cs, sn = cos_ref[...], sin_ref[...]
        for h in range(RET_HEADS):
            qd, kd, gt = _decay_vectors(h)
            qr, kr, v, g = _ret_head(p_ref, cs, sn, h)
            st, dst, w = st_ref[h], dstate[h], wtab[h]
            qb, kb, vb = _bf(qr), _bf(kr), _bf(v)
            p = _dot_nt(qb, kb) * w
            o = _dot(_bf(p), vb) + _dot(_bf(qr * qd), _bf(st))
            on, rstd = _group_norm(o)
            gw = gnw_ref[:, h * RET_DV:(h + 1) * RET_DV]
            da = da_ref[:, h * RET_DV:(h + 1) * RET_DV].astype(F32)
            sg = _sigmoid(g)
            silu = g * sg
            dg = da * on * gw * (sg * (1.0 + g * (1.0 - sg)))
            dgn_ref[ROW_GN:ROW_GN + 1, h * RET_DV:(h + 1) * RET_DV] += jnp.sum(da * silu * on, axis=0, keepdims=True)
            don = da * silu * gw
            do = rstd * (don - jnp.mean(don, axis=-1, keepdims=True)
                         - on * jnp.mean(don * on, axis=-1, keepdims=True))
            dob = _bf(do)
            dp = _dot_nt(dob, vb) * w
            dqr = _dot(_bf(dp), kb) + _dot_nt(dob, _bf(st)) * qd
            dkr = _dot_tn(_bf(dp), qb) + _dot_nt(vb, _bf(dst)) * kd
            dv = _dot_tn(_bf(p), dob) + _dot(_bf(kr * kd), _bf(dst))
            dstate[h] = dst * gt + _dot_tn(_bf(qr * qd), dob)
            d_ref[:, h * RET_DK:(h + 1) * RET_DK] = _bf(_rotate_bwd(dqr, cs, sn))
            d_ref[:, 512 + h * RET_DK:512 + (h + 1) * RET_DK] = _bf(_rotate_bwd(dkr, cs, sn) * (RET_DK ** -0.5))
            d_ref[:, 1024 + h * RET_DV:1024 + (h + 1) * RET_DV] = _bf(dv)
            d_ref[:, 2048 + h * RET_DV:2048 + (h + 1) * RET_DV] = _bf(dg)

    rev = lambda i, c: (i, nc - 1 - c, 0)
    tab = pl.BlockSpec((None, SEQ_T, RET_DK), rev)
    return _call(
        body, name="retention_bwd", grid=(b, nc), comm=comm,
        in_specs=[pl.BlockSpec((None, SEQ_T, 3072), rev), tab, tab,
                  pl.BlockSpec((1, D_MODEL), lambda i, c: (0, 0)),
                  pl.BlockSpec((None, None, RET_HEADS, RET_DK, RET_DV), lambda i, c: (i, nc - 1 - c, 0, 0, 0)),
                  pl.BlockSpec((None, SEQ_T, D_MODEL), rev)],
        out_specs=[pl.BlockSpec((None, SEQ_T, 3072), rev),
                   pl.BlockSpec((8, D_MODEL), lambda i, c: (0, 0))],
        out_shape=[jax.ShapeDtypeStruct((b, s, 3072), BF16), jax.ShapeDtypeStruct((8, D_MODEL), F32)],
        scratch=[pltpu.VMEM((RET_HEADS, RET_DK, RET_DV), F32),
                 pltpu.VMEM((RET_HEADS, SEQ_T, SEQ_T), F32)],
    )(proj3, cos, sin, gnw, states, da3)


def _softplus_neg(lam):
    z = -lam
    u = jnp.exp(-jnp.abs(z))
    log1p_u = jnp.where(u < 0.01, u * (1.0 - u * (0.5 - u * (1.0 / 3.0))), jnp.log(1.0 + u))
    return jnp.maximum(z, 0.0) + log1p_u


def _lru_coeffs(xc, wr_ref, br_ref, wi_ref, bi_ref, lam_ref):
    rs, is_ = [], []
    for n in range(LRU_BLOCKS):
        xb = _bf(xc[:, n * LRU_BLOCK:(n + 1) * LRU_BLOCK])
        cols = slice(n * LRU_BLOCK, (n + 1) * LRU_BLOCK)
        rs.append(_sigmoid(_dot(xb, wr_ref[n]) + br_ref[:, cols]))
        is_.append(_sigmoid(_dot(xb, wi_ref[n]) + bi_ref[:, cols]))
    r = jnp.concatenate(rs, axis=1)
    i = jnp.concatenate(is_, axis=1)
    sp = _softplus_neg(lam_ref[...])
    la = -LRU_C * r * sp
    a = jnp.exp(la)
    s = jnp.sqrt(-jnp.tanh(la) * (a * a + 1.0))
    return r, i, a, s, sp


_LRU_PARAM_SPECS = [
    pl.BlockSpec((4, D_MODEL), lambda i, c: (0, 0)),
    pl.BlockSpec((1, D_MODEL), lambda i, c: (0, 0)),
    pl.BlockSpec((LRU_BLOCKS, LRU_BLOCK, LRU_BLOCK), lambda i, c: (0, 0, 0)),
    pl.BlockSpec((1, D_MODEL), lambda i, c: (0, 0)),
    pl.BlockSpec((LRU_BLOCKS, LRU_BLOCK, LRU_BLOCK), lambda i, c: (0, 0, 0)),
    pl.BlockSpec((1, D_MODEL), lambda i, c: (0, 0)),
    pl.BlockSpec((1, D_MODEL), lambda i, c: (0, 0)),
]


def _lru_fwd(proj3, params, comm=None):
    b, s, _ = proj3.shape
    nc = s // SEQ_T

    def body(x_ref, y_ref, cw, cb, wr, br, wi, bi, lam,
             o_ref, h_ref, xc_ref, a_ref, s_ref, gy_ref, hdg_ref, r_ref, i_ref, xprev, hprev):
        @pl.when(pl.program_id(1) == 0)
        def _():
            xprev[...] = jnp.zeros_like(xprev)
            hprev[...] = jnp.zeros_like(hprev)

        x = x_ref[...].astype(F32)
        prev8 = xprev[...]
        xc = cb[...] + sum(cw[j:j + 1, :] * _shift_down(x, 3 - j, prev8) for j in range(4))
        xprev[...] = x[SEQ_T - 8:]
        xc_ref[...] = xc
        r, i, a, s_, _ = _lru_coeffs(xc, wr, br, wi, bi, lam)
        a_ref[...] = a
        s_ref[...] = s_
        r_ref[...] = _bf(r)
        i_ref[...] = _bf(i)
        h = _scan_fwd(a, s_ * (i * xc), hprev[7:8, :])
        hprev[...] = h[SEQ_T - 8:]
        h_ref[...] = h
        gy, dgy = _gelu_and_grad(y_ref[...].astype(F32))
        o_ref[...] = _bf(h * gy)
        gy_ref[...] = _bf(gy)
        hdg_ref[...] = _bf(h * dgy)

    out = pl.BlockSpec((None, SEQ_T, D_MODEL), lambda i, c: (i, c, 0))
    half, full = jax.ShapeDtypeStruct((b, s, D_MODEL), BF16), jax.ShapeDtypeStruct((b, s, D_MODEL), F32)
    return _call(
        body, name="lru_fwd", grid=(b, nc), comm=comm,
        in_specs=[pl.BlockSpec((None, SEQ_T, D_MODEL), lambda i, c: (i, c, 3)),
                  pl.BlockSpec((None, SEQ_T, D_MODEL), lambda i, c: (i, c, 4))] + _LRU_PARAM_SPECS,
        out_specs=[out] * 9, out_shape=[half, full, full, full, full, half, half, half, half],
        scratch=[pltpu.VMEM((8, D_MODEL), F32), pltpu.VMEM((8, D_MODEL), F32)],
    )(proj3, proj3, *params)


def _lru_bwd(proj3, params, kept, db3, comm=None):
    b, s, _ = proj3.shape
    nc = s // SEQ_T
    blk8 = SEQ_T // 8
    hseq = kept[0]

    def body(x_ref, h_ref, xc_ref, a_ref, s_ref, gy_ref, hdg_ref, r_ref, i_ref, hp_ref, db_ref,
             cw, cb, wr, br, wi, bi, lam, d_ref, dwr_ref, dwi_ref, sm_ref, gnext, anext, dxcnext):
        c = pl.program_id(1)
        first_chunk = c == nc - 1

        @pl.when(jnp.logical_and(pl.program_id(0) == 0, c == 0))
        def _():
            dwr_ref[...] = jnp.zeros_like(dwr_ref)
            dwi_ref[...] = jnp.zeros_like(dwi_ref)
            sm_ref[...] = jnp.zeros_like(sm_ref)

        @pl.when(c == 0)
        def _():
            gnext[...] = jnp.zeros_like(gnext)
            anext[...] = jnp.zeros_like(anext)
            dxcnext[...] = jnp.zeros_like(dxcnext)

        x, xc, h = x_ref[...].astype(F32), xc_ref[...], h_ref[...]
        hprev = hp_ref[...] * jnp.where(first_chunk, 0.0, 1.0)
        r, i, a, s_ = r_ref[...].astype(F32), i_ref[...].astype(F32), a_ref[...], s_ref[...]
        sp = _softplus_neg(lam[...])
        db = db_ref[...].astype(F32)
        dy = db * hdg_ref[...].astype(F32)
        a_up = _shift_up(a, 1, anext[...])
        g = _scan_bwd(a_up, db * gy_ref[...].astype(F32), gnext[0:1, :])
        gnext[...] = g[0:8]
        anext[...] = a[0:8]
        da = g * _shift_down(h, 1, hprev)
        ixc = i * xc
        dla = da * a - (g * ixc) * (a * a) / s_
        di = g * s_ * xc
        dxc = g * s_ * i
        dzr = dla * (-LRU_C * sp) * r * (1.0 - r)
        dzi = di * i * (1.0 - i)
        lam_v = lam[...]
        _row_acc(sm_ref, 7, jnp.sum(dla * (LRU_C * r), axis=0, keepdims=True) * _sigmoid(-lam_v))
        _row_acc(sm_ref, 5, jnp.sum(dzr, axis=0, keepdims=True))
        _row_acc(sm_ref, 6, jnp.sum(dzi, axis=0, keepdims=True))
        parts = []
        for n in range(LRU_BLOCKS):
            cols = slice(n * LRU_BLOCK, (n + 1) * LRU_BLOCK)
            xb, zr, zi = _bf(xc[:, cols]), _bf(dzr[:, cols]), _bf(dzi[:, cols])
            parts.append(dxc[:, cols] + _dot_nt(zr, wr[n]) + _dot_nt(zi, wi[n]))
            dwr_ref[n] += _dot_tn(xb, zr)
            dwi_ref[n] += _dot_tn(xb, zi)
        dxc = jnp.concatenate(parts, axis=1)
        _row_acc(sm_ref, 4, jnp.sum(dxc, axis=0, keepdims=True))
        nxt = dxcnext[...]
        dx = jnp.zeros_like(x)
        for j in range(4):
            ahead = _shift_up(dxc, 3 - j, nxt)
            dx = dx + cw[j:j + 1, :] * ahead
            _row_acc(sm_ref, j, jnp.sum(ahead * x, axis=0, keepdims=True))
        dxcnext[...] = dxc[0:8]
        d_ref[:, 0:D_MODEL] = _bf(dx)
        d_ref[:, D_MODEL:2 * D_MODEL] = _bf(dy)

    rev = lambda col: (lambda i, c: (i, nc - 1 - c, col))
    prev = lambda col: (lambda i, c: (i, jnp.maximum((nc - 1 - c) * blk8 - 1, 0), col))
    return _call(
        body, name="lru_bwd", grid=(b, nc), comm=comm,
        in_specs=[pl.BlockSpec((None, SEQ_T, D_MODEL), rev(3))]
        + [pl.BlockSpec((None, SEQ_T, D_MODEL), rev(0))] * len(kept)
        + [pl.BlockSpec((None, 8, D_MODEL), prev(0)), pl.BlockSpec((None, SEQ_T, D_MODEL), rev(0))]
        + _LRU_PARAM_SPECS,
        out_specs=[pl.BlockSpec((None, SEQ_T, 2 * D_MODEL), rev(0)),
                   pl.BlockSpec((LRU_BLOCKS, LRU_BLOCK, LRU_BLOCK), lambda i, c: (0, 0, 0)),
                   pl.BlockSpec((LRU_BLOCKS, LRU_BLOCK, LRU_BLOCK), lambda i, c: (0, 0, 0)),
                   pl.BlockSpec((8, D_MODEL), lambda i, c: (0, 0))],
        out_shape=[jax.ShapeDtypeStruct((b, s, 2 * D_MODEL), BF16),
                   jax.ShapeDtypeStruct((LRU_BLOCKS, LRU_BLOCK, LRU_BLOCK), F32),
                   jax.ShapeDtypeStruct((LRU_BLOCKS, LRU_BLOCK, LRU_BLOCK), F32),
                   jax.ShapeDtypeStruct((8, D_MODEL), F32)],
        scratch=[pltpu.VMEM((8, D_MODEL), F32)] * 3,
    )(proj3, *kept, hseq, db3, *params)


def _merge_parts(a_ref, b_ref, gr_ref, gl_ref, mgb_ref, wro_ref, wlo_ref):
    ya = _dot(a_ref[...], wro_ref[...])
    yb = _dot(b_ref[...], wlo_ref[...])
    sa = _sigmoid(gr_ref[...].astype(F32) + mgb_ref[0:1, :])
    sb = _sigmoid(gl_ref[...].astype(F32) + mgb_ref[1:2, :])
    return ya, yb, sa, sb


def _merge_specs(tm):
    row = lambda col: pl.BlockSpec((tm, D_MODEL), lambda i: (i, col))
    full = pl.BlockSpec((D_MODEL, D_MODEL), lambda i: (0, 0))
    return row, full


def _merge_fwd(a_in, b_in, proj, mgb, wro, wlo, wout, x, *, tm, comm=None):
    m = x.shape[0]
    row, full = _merge_specs(tm)

    def body(a_ref, b_ref, gr_ref, gl_ref, mgb_ref, wro_ref, wlo_ref, wout_ref, x_ref,
             o_ref, mix_ref, ya_ref, yb_ref):
        ya, yb, sa, sb = _merge_parts(a_ref, b_ref, gr_ref, gl_ref, mgb_ref, wro_ref, wlo_ref)
        mix = _bf(sa * ya + sb * yb)
        o_ref[...] = x_ref[...] + _dot(mix, wout_ref[...])
        mix_ref[...] = mix
        ya_ref[...] = _bf(ya)
        yb_ref[...] = _bf(yb)

    act = jax.ShapeDtypeStruct((m, D_MODEL), BF16)
    return _call(
        body, name="merge_fwd", grid=(m // tm,), comm=comm,
        in_specs=[row(0), row(0), row(5), row(6), pl.BlockSpec((2, D_MODEL), lambda i: (0, 0)),
                  full, full, full, row(0)],
        out_specs=[row(0)] * 4,
        out_shape=[jax.ShapeDtypeStruct((m, D_MODEL), F32), act, act, act],
    )(a_in, b_in, proj, proj, mgb, wro, wlo, wout, x)


def _merge_bwd(ya, yb, proj, mgb, wro, wlo, wout, dx2, *, tm):
    m = dx2.shape[0]
    row, full = _merge_specs(tm)

    def body(ya_ref, yb_ref, gr_ref, gl_ref, mgb_ref, wro_ref, wlo_ref, wout_ref, dx_ref,
             dya_ref, dyb_ref, da_ref, db_ref, dg_ref, sm_ref):
        @pl.when(pl.program_id(0) == 0)
        def _():
            sm_ref[...] = jnp.zeros_like(sm_ref)

        ya, yb = ya_ref[...].astype(F32), yb_ref[...].astype(F32)
        sa = _sigmoid(gr_ref[...].astype(F32) + mgb_ref[0:1, :])
        sb = _sigmoid(gl_ref[...].astype(F32) + mgb_ref[1:2, :])
        dmix = _dot_nt(_bf(dx_ref[...]), wout_ref[...])
        dya, dyb = _bf(dmix * sa), _bf(dmix * sb)
        dya_ref[...] = dya
        dyb_ref[...] = dyb
        dga = dmix * ya * sa * (1.0 - sa)
        dgb = dmix * yb * sb * (1.0 - sb)
        dg_ref[:, 0:D_MODEL] = _bf(dga)
        dg_ref[:, D_MODEL:2 * D_MODEL] = _bf(dgb)
        _row_acc(sm_ref, ROW_MGB, jnp.sum(dga, axis=0, keepdims=True))
        _row_acc(sm_ref, ROW_MGB + 1, jnp.sum(dgb, axis=0, keepdims=True))
        da_ref[...] = _bf(_dot_nt(dya, wro_ref[...]))
        db_ref[...] = _bf(_dot_nt(dyb, wlo_ref[...]))

    act = jax.ShapeDtypeStruct((m, D_MODEL), BF16)
    return _call(
        body, name="merge_bwd", grid=(m // tm,),
        in_specs=[row(0), row(0), row(5), row(6), pl.BlockSpec((2, D_MODEL), lambda i: (0, 0)),
                  full, full, full, row(0)],
        out_specs=[row(0)] * 4 + [pl.BlockSpec((tm, 2 * D_MODEL), lambda i: (i, 0)),
                                  pl.BlockSpec((8, D_MODEL), lambda i: (0, 0))],
        out_shape=[act] * 4 + [jax.ShapeDtypeStruct((m, 2 * D_MODEL), BF16),
                               jax.ShapeDtypeStruct((8, D_MODEL), F32)],
    )(ya, yb, proj, proj, mgb, wro, wlo, wout, dx2)


def _ffn_act_fwd(up3, cw, cb):
    b, s, _ = up3.shape

    def body(g_ref, v_ref, cw_ref, cb_ref, o_ref, act_ref, q_ref, gprev):
        @pl.when(pl.program_id(1) == 0)
        def _():
            gprev[...] = jnp.zeros_like(gprev)

        gate, val = g_ref[...].astype(F32), v_ref[...].astype(F32)
        prev8 = gprev[...]
        gc = cb_ref[...] + sum(cw_ref[j:j + 1, :] * _shift_down(gate, 2 - j, prev8) for j in range(3))
        gprev[...] = gate[SEQ_T - 8:]
        act, dact = _gelu_and_grad(gc)
        o_ref[...] = _bf(act * val)
        act_ref[...] = _bf(act)
        q_ref[...] = _bf(dact * val)

    out = pl.BlockSpec((None, SEQ_T, D_FF), lambda i, c: (i, c, 0))
    return _call(
        body, name="ffn_act_fwd", grid=(b, s // SEQ_T),
        in_specs=[pl.BlockSpec((None, SEQ_T, D_FF), lambda i, c: (i, c, 0)),
                  pl.BlockSpec((None, SEQ_T, D_FF), lambda i, c: (i, c, 1)),
                  pl.BlockSpec((3, D_FF), lambda i, c: (0, 0)),
                  pl.BlockSpec((1, D_FF), lambda i, c: (0, 0))],
        out_specs=[out] * 3,
        out_shape=[jax.ShapeDtypeStruct((b, s, D_FF), BF16)] * 3,
        scratch=[pltpu.VMEM((8, D_FF), F32)],
    )(up3, up3, cw, cb)


def _ffn_act_bwd(up3, act3, q3, cw, df3, comm=None):
    b, s, _ = up3.shape
    nc = s // SEQ_T

    def body(g_ref, act_ref, q_ref, df_ref, cw_ref, dg_ref, dv_ref, sm_ref, dgcnext):
        c = pl.program_id(1)

        @pl.when(jnp.logical_and(pl.program_id(0) == 0, c == 0))
        def _():
            sm_ref[...] = jnp.zeros_like(sm_ref)

        @pl.when(c == 0)
        def _():
            dgcnext[...] = jnp.zeros_like(dgcnext)

        gate = g_ref[...].astype(F32)
        df = df_ref[...].astype(F32)
        dv_ref[...] = _bf(df * act_ref[...].astype(F32))
        dgc = df * q_ref[...].astype(F32)
        nxt = dgcnext[...]
        dgate = jnp.zeros_like(gate)
        for j in range(3):
            ahead = _shift_up(dgc, 2 - j, nxt)
            dgate = dgate + cw_ref[j:j + 1, :] * ahead
            _row_acc(sm_ref, j, jnp.sum(ahead * gate, axis=0, keepdims=True))
        _row_acc(sm_ref, 3, jnp.sum(dgc, axis=0, keepdims=True))
        dgcnext[...] = dgc[0:8]
        dg_ref[...] = _bf(dgate)

    rev = pl.BlockSpec((None, SEQ_T, D_FF), lambda i, c: (i, nc - 1 - c, 0))
    return _call(
        body, name="ffn_act_bwd", grid=(b, nc), comm=comm,
        in_specs=[rev, rev, rev, rev, pl.BlockSpec((3, D_FF), lambda i, c: (0, 0))],
        out_specs=[rev, rev, pl.BlockSpec((8, D_FF), lambda i, c: (0, 0))],
        out_shape=[jax.ShapeDtypeStruct((b, s, D_FF), BF16)] * 2 + [jax.ShapeDtypeStruct((8, D_FF), F32)],
        scratch=[pltpu.VMEM((8, D_FF), F32)],
    )(up3, act3, q3, df3, cw)


def _ffn_down_loss(f, wd, x2, nfw, target, *, tm):
    m, kf = f.shape
    nt = m // tm

    def body(f_ref, wd_ref, x_ref, nw_ref, t_ref, loss_ref, dx_ref, dnw_ref, lsum):
        i = pl.program_id(0)

        @pl.when(i == 0)
        def _():
            dnw_ref[...] = jnp.zeros_like(dnw_ref)
            lsum[...] = jnp.zeros_like(lsum)

        x3 = x_ref[...] + _dot(f_ref[...], wd_ref[...])
        nw = nw_ref[...]
        xh, _ = _rms(x3)
        err = xh * nw - t_ref[...]
        lsum[...] += jnp.sum(err * err, axis=0, keepdims=True)
        dx, dnw = _rms_bwd(err * (1.0 / D_MODEL), x3, nw)
        dx_ref[...] = dx
        _row_acc(dnw_ref, ROW_NF, dnw)

        @pl.when(i == nt - 1)
        def _():
            loss_ref[...] = jnp.sum(lsum[...], axis=1, keepdims=True) * (0.5 / D_MODEL)

    row = pl.BlockSpec((tm, D_MODEL), lambda i: (i, 0))
    return _call(
        body, name="ffn_down_loss", grid=(nt,),
        in_specs=[pl.BlockSpec((tm, kf), lambda i: (i, 0)),
                  pl.BlockSpec((kf, D_MODEL), lambda i: (0, 0)),
                  row, pl.BlockSpec((1, D_MODEL), lambda i: (0, 0)), row],
        out_specs=[pl.BlockSpec((1, 1), lambda i: (0, 0)), row,
                   pl.BlockSpec((8, D_MODEL), lambda i: (0, 0))],
        out_shape=[jax.ShapeDtypeStruct((1, 1), F32), jax.ShapeDtypeStruct((m, D_MODEL), F32),
                   jax.ShapeDtypeStruct((8, D_MODEL), F32)],
        scratch=[pltpu.VMEM((1, D_MODEL), F32)],
    )(f, wd, x2, nfw, target)


def _row_tile(rows):
    return next((t for t in (256, 128, 64, 32, 16, 8) if rows % t == 0), rows)


def _adamw(w, gs, m, v, *, name):
    rows, cols = w.shape
    tr = _row_tile(rows)
    ng = len(gs)

    def body(w_ref, *rest):
        g_refs, (m_ref, v_ref, g_out, d_out, m_out, v_out) = rest[:ng], rest[ng:]
        g = g_refs[0][...]
        for r in g_refs[1:]:
            g = g + r[...]
        mn = ADAM_B1 * m_ref[...] + (1.0 - ADAM_B1) * g
        vn = ADAM_B2 * v_ref[...] + (1.0 - ADAM_B2) * (g * g)
        m_hat = mn / (1.0 - ADAM_B1 ** ADAM_STEP)
        v_hat = vn / (1.0 - ADAM_B2 ** ADAM_STEP)
        g_out[...] = g
        d_out[...] = -ADAM_LR * (m_hat / (jnp.sqrt(v_hat) + ADAM_EPS) + ADAM_WD * w_ref[...])
        m_out[...] = mn
        v_out[...] = vn

    spec = pl.BlockSpec((tr, cols), lambda i: (i, 0))
    return _call(
        body, name=name, grid=(rows // tr,),
        in_specs=[spec] * (3 + ng), out_specs=[spec] * 4,
        out_shape=[jax.ShapeDtypeStruct((rows, cols), F32)] * 4,
    )(w, *gs, m, v)


def _mesh_pos():
    x, y, c = lax.axis_index("x"), lax.axis_index("y"), lax.axis_index("c")
    return x, y, c


def _other_chips(x, y, c):
    return [((1 - x, y, c), 2 * (1 - x) + y), ((x, 1 - y, c), 2 * x + 1 - y),
            ((1 - x, 1 - y, c), 2 * (1 - x) + 1 - y)]


def _region(ref, axis, size, half_axis, chip, core=None):
    idx = [slice(None)] * len(ref.shape)
    if core is None:
        idx[axis] = pl.ds(pl.multiple_of(chip * size, size), size)
    elif half_axis == axis:
        h = size // 2
        idx[axis] = pl.ds(pl.multiple_of(chip * size + core * h, h), h)
    else:
        idx[axis] = pl.ds(pl.multiple_of(chip * size, size), size)
        h = ref.shape[half_axis] // 2
        idx[half_axis] = pl.ds(pl.multiple_of(core * h, h), h)
    return ref.at[tuple(idx)]


def _half(ref, half_axis, core):
    idx = [slice(None)] * len(ref.shape)
    h = ref.shape[half_axis] // 2
    idx[half_axis] = pl.ds(pl.multiple_of(core * h, h), h)
    return ref.at[tuple(idx)]


class _Copy:
    def __init__(self, make):
        self._make = make

    def start(self):
        self._make().start()

    def wait(self):
        self._make().wait()

    def wait_send(self):
        self._make().wait_send()

    def wait_recv(self):
        self._make().wait_recv()


def _remote(src, dst, send_sem, recv_sem, dev):
    return _Copy(lambda: pltpu.make_async_remote_copy(
        src_ref=src, dst_ref=dst, send_sem=send_sem, recv_sem=recv_sem, device_id=dev, device_id_type=MESH))


def _local(src, dst, sem):
    return _Copy(lambda: pltpu.make_async_copy(src, dst, sem))


def _dma_sems(n):
    return pltpu.SemaphoreType.DMA((n,))


def _place_shard(w, chip, axis, *, name):
    shape = list(w.shape)
    shape[axis] *= N_CHIPS
    if w.ndim == 3:
        block, grid = (1,) + w.shape[1:], (w.shape[0],)
        in_map, out_map = (lambda i, chip: (i, 0, 0)), (lambda i, chip: (i, chip[0], 0))
    else:
        tr = _row_tile(w.shape[0])
        nt = w.shape[0] // tr
        block, grid = (tr, w.shape[1]), (nt,)
        in_map = lambda i, chip: (i, 0)
        out_map = (lambda i, chip: (chip[0] * nt + i, 0)) if axis == 0 else (lambda i, chip: (i, chip[0]))

    def body(chip_ref, w_ref, o_ref):
        o_ref[...] = _bf(w_ref[...])

    return _call(body, name=name, grid=grid, prefetch=1, in_specs=[pl.BlockSpec(block, in_map)],
                 out_specs=pl.BlockSpec(block, out_map),
                 out_shape=jax.ShapeDtypeStruct(tuple(shape), BF16))(chip, w)


def _ici_leg(srcs, dsts, layout, sizes, n_whole, sems):
    send_sems, recv_sems, local_sems = sems
    x, y, c = _mesh_pos()
    mine = 2 * x + y
    n_big = len(srcs) - n_whole
    local, sends, recvs = [], [], []
    for t, (src, dst) in enumerate(zip(srcs, dsts)):
        if t < n_big:
            ax, hx = layout[t]
            part = _region(src, ax, sizes[t], hx, mine, c)
            landing = lambda chip, dst=dst, ax=ax, hx=hx, size=sizes[t]: _region(dst, ax, size, hx, chip, c)
        else:
            part, landing = src, (lambda chip, dst=dst: dst.at[chip])
            local.append(_local(src, dst.at[mine], local_sems.at[t - n_big]))
        for k, (dev, chip) in enumerate(_other_chips(x, y, c)):
            sends.append(_remote(part, landing(mine), send_sems.at[3 * t + k], recv_sems.at[3 * t + k], dev))
            recvs.append(_remote(part, landing(chip), send_sems.at[3 * t + k], recv_sems.at[3 * t + k], dev))
    return local, sends, recvs


def _d2d_leg(srcs, dsts, layout, sizes, sems):
    send_sems, recv_sems = sems
    x, y, c = _mesh_pos()
    sends, recvs = [], []
    for t, (src, dst) in enumerate(zip(srcs, dsts)):
        ax, hx = layout[t]
        for k, (_, chip) in enumerate(_other_chips(x, y, c)):
            sem = (send_sems.at[3 * t + k], recv_sems.at[3 * t + k])
            sends.append(_remote(_region(src, ax, sizes[t], hx, chip, c),
                                 _region(dst, ax, sizes[t], hx, chip, c), *sem, (x, y, 1 - c)))
            recvs.append(_remote(_region(src, ax, sizes[t], hx, chip, 1 - c),
                                 _region(dst, ax, sizes[t], hx, chip, 1 - c), *sem, (x, y, 1 - c)))
    return sends, recvs


def _gather_shapes(bufs, whole):
    return ([jax.ShapeDtypeStruct(b.shape, b.dtype) for b in bufs]
            + [jax.ShapeDtypeStruct((N_CHIPS,) + w.shape, w.dtype) for w in whole])


def _gather_ici(bufs, layout):
    n = len(bufs)
    sizes = [b.shape[ax] // N_CHIPS for b, (ax, _) in zip(bufs, layout)]

    def start(ins, outs, sems):
        for cp in _ici_leg(ins, outs, layout, sizes, 0, (*sems, None))[1]:
            cp.start()

    def finish(ins, outs, sems):
        _, sends, recvs = _ici_leg(ins, outs, layout, sizes, 0, (*sems, None))
        for cp in recvs:
            cp.wait_recv()
        for cp in sends:
            cp.wait_send()

    return _Comm(bufs, _gather_shapes(bufs, ()), [_dma_sems(3 * n), _dma_sems(3 * n)], start, finish,
                 aliases={i: i for i in range(n)})


def _both(a, b):
    ni, no, ns = len(a.ins), len(a.outs), len(a.sems)

    def start(ins, outs, sems):
        a.start(ins[:ni], outs[:no], sems[:ns])
        b.start(ins[ni:], outs[no:], sems[ns:])

    def finish(ins, outs, sems):
        a.finish(ins[:ni], outs[:no], sems[:ns])
        b.finish(ins[ni:], outs[no:], sems[ns:])

    aliases = {**a.aliases, **{ni + i: no + o for i, o in b.aliases.items()}}
    return _Comm(a.ins + b.ins, a.outs + b.outs, a.sems + b.sems, start, finish, aliases)


def _gather_d2d(bufs, layout, sizes):
    n = len(bufs)

    def start(ins, outs, sems):
        for cp in _d2d_leg(ins, outs, layout, sizes, sems)[0]:
            cp.start()

    def finish(ins, outs, sems):
        sends, recvs = _d2d_leg(ins, outs, layout, sizes, sems)
        for cp in recvs:
            cp.wait_recv()
        for cp in sends:
            cp.wait_send()

    return _Comm(bufs, [jax.ShapeDtypeStruct(b.shape, b.dtype) for b in bufs],
                 [_dma_sems(3 * n), _dma_sems(3 * n)], start, finish, aliases={i: i for i in range(n)})


def _norm_bf16(x, nw, *, name, tm):
    m, d = x.shape

    def body(x_ref, nw_ref, h_ref):
        h_ref[...] = _bf(_rms(x_ref[...])[0] * nw_ref[...])

    row = pl.BlockSpec((tm, d), lambda i: (i, 0))
    return _call(body, name=name, grid=(m // tm,), in_specs=[row, pl.BlockSpec((1, d), lambda i: (0, 0))],
                 out_specs=row, out_shape=jax.ShapeDtypeStruct((m, d), BF16))(x, nw)


def _in_proj_gather(h1, w_buf, later, later_cut, small, order, *, tm):
    m, d = h1.shape
    width = w_buf.shape[1] // N_CHIPS
    nr, nl = m // tm, len(later)
    sizes = [b.shape[ax] // N_CHIPS for b, (ax, _) in zip(later, later_cut)]

    def body(order_ref, h_ref, w_in, *rest):
        later_in, small_in = rest[:nl], rest[nl]
        o_ref, w_out = rest[nl + 1], rest[nl + 2]
        later_out, small_out = rest[nl + 3:2 * nl + 3], rest[2 * nl + 3]
        wv, load_sem, ici_send, ici_recv, d2d_send, d2d_recv, l_send, l_recv, l_local = rest[2 * nl + 4:]
        s, i = pl.program_id(0), pl.program_id(1)
        x, y, c = _mesh_pos()
        mine = 2 * x + y
        peers = _other_chips(x, y, c)
        part = lambda ref, chip, core=None: _region(ref, 1, width, 0, chip, core)

        def ici(k):
            dev, chip = peers[k]
            sem = (ici_send.at[k], ici_recv.at[k])
            return (_remote(part(w_in, mine, c), part(w_out, mine, c), *sem, dev),
                    _remote(part(w_in, chip, c), part(w_out, chip, c), *sem, dev))

        def d2d(k):
            chip, sem, sib = peers[k][1], (d2d_send.at[k], d2d_recv.at[k]), (x, y, 1 - c)
            return (_remote(part(w_out, chip, c), part(w_out, chip, c), *sem, sib),
                    _remote(part(w_out, chip, 1 - c), part(w_out, chip, 1 - c), *sem, sib))

        def load(src, chip, slot):
            cp = _local(part(src, chip), wv.at[slot], load_sem.at[slot])
            cp.start()
            cp.wait()

        def others():
            return _ici_leg(list(later_in) + [small_in], list(later_out) + [small_out], later_cut, sizes, 1,
                            (l_send, l_recv, l_local))

        @pl.when(jnp.logical_and(s == 0, i == 0))
        def _():
            for k in range(3):
                ici(k)[0].start()
            local, sends, _ = others()
            for cp in local + sends:
                cp.start()
            load(w_in, mine, 0)

        o_ref[...] = _bf(_dot(h_ref[...], wv[s % 2]))

        @pl.when(i == nr - 1)
        def _():
            for k in range(3):
                @pl.when(s == k)
                def _(k=k):
                    ici(k)[1].wait_recv()
                    d2d(k)[0].start()
                    d2d(k)[1].wait_recv()
                    load(w_out, peers[k][1], (k + 1) % 2)

            @pl.when(s == 3)
            def _():
                for k in range(3):
                    ici(k)[0].wait_send()
                    d2d(k)[0].wait_send()
                local, sends, recvs = others()
                for cp in recvs:
                    cp.wait_recv()
                for cp in sends:
                    cp.wait_send()
                for cp in local:
                    cp.wait()

    any_spec = pl.BlockSpec(memory_space=pl.ANY)
    n_any = nl + 2
    outs = _call(
        body, name="in_proj", grid=(N_CHIPS, nr), prefetch=1,
        in_specs=[pl.BlockSpec((tm, d), lambda s, i, order: (i, 0))] + [any_spec] * n_any,
        out_specs=[pl.BlockSpec((tm, width), lambda s, i, order: (i, order[s]))] + [any_spec] * n_any,
        out_shape=[jax.ShapeDtypeStruct((m, w_buf.shape[1]), BF16)] + _gather_shapes([w_buf] + list(later), [small]),
        scratch=[pltpu.VMEM((2, d, width), BF16), _dma_sems(2), _dma_sems(3), _dma_sems(3), _dma_sems(3),
                 _dma_sems(3), _dma_sems(3 * (nl + 1)), _dma_sems(3 * (nl + 1)), _dma_sems(1)],
        aliases={2 + t: 1 + t for t in range(nl + 1)},
    )(order, h1, w_buf, *later, small)
    return outs[0], outs[1], list(outs[2:2 + nl]), outs[2 + nl]


def _exchange(grads, layout):
    n = len(grads)
    others = N_DEV - 1
    sizes = [g.shape[ax] // N_CHIPS for g, (ax, _) in zip(grads, layout)]
    out_shapes = []
    for g, (ax, hx), sz in zip(grads, layout, sizes):
        shape = list(g.shape)
        shape[ax] = sz
        shape[hx] //= 2
        out_shapes.append(jax.ShapeDtypeStruct((others,) + tuple(shape), g.dtype))

    def copies(ins, outs, sems):
        send_sems, recv_sems = sems
        x, y, c = _mesh_pos()
        sends, recvs = [], []
        for t, (src, dst) in enumerate(zip(ins, outs)):
            ax, hx = layout[t]
            for r in range(1, N_DEV):
                px = (1 - x) if r & 4 else x
                py = (1 - y) if r & 2 else y
                pc = (1 - c) if r & 1 else c
                sem = (send_sems.at[others * t + r - 1], recv_sems.at[others * t + r - 1])
                part = _region(src, ax, sizes[t], hx, 2 * px + py, pc)
                sends.append(_remote(part, dst.at[r - 1], *sem, (px, py, pc)))
                recvs.append(_remote(part, dst.at[r - 1], *sem, (px, py, pc)))
        return sends, recvs

    def start(ins, outs, sems):
        for cp in copies(ins, outs, sems)[0]:
            cp.start()

    def finish(ins, outs, sems):
        sends, recvs = copies(ins, outs, sems)
        for cp in recvs:
            cp.wait_recv()
        for cp in sends:
            cp.wait_send()

    return _Comm(grads, out_shapes, [_dma_sems(others * n), _dma_sems(others * n)], start, finish)


def _reduce_half(g, parts, pos, cut, *, name):
    ax, _ = cut
    others = parts.shape[0]
    if g.ndim == 3:
        nb, rows, cols = g.shape
        hb = nb // 2
        block, grid, out_shape = (1, rows // N_CHIPS, cols), (hb,), (nb, rows // N_CHIPS, cols)
        g_map = lambda i, pos: (pos[1] * hb + i, pos[0], 0)
        o_map = lambda i, pos: (pos[1] * hb + i, 0, 0)
        p_map = lambda i, pos: (0, i, 0, 0)
    elif ax == 1:
        rows, cols = g.shape
        tr = _row_tile(rows // 2)
        nt = rows // 2 // tr
        block, grid, out_shape = (tr, cols // N_CHIPS), (nt,), (rows, cols // N_CHIPS)
        g_map = lambda i, pos: (pos[1] * nt + i, pos[0])
        o_map = lambda i, pos: (pos[1] * nt + i, 0)
        p_map = lambda i, pos: (0, i, 0)
    else:
        rows, cols = g.shape
        tr = _row_tile(rows // N_CHIPS // 2)
        nt = rows // N_CHIPS // 2 // tr
        block, grid, out_shape = (tr, cols), (nt,), (rows // N_CHIPS, cols)
        g_map = lambda i, pos: (pos[0] * 2 * nt + pos[1] * nt + i, 0)
        o_map = lambda i, pos: (pos[1] * nt + i, 0)
        p_map = lambda i, pos: (0, i, 0)

    def body(pos_ref, g_ref, p_ref, o_ref):
        acc = g_ref[...].astype(F32)
        for r in range(others):
            acc = acc + p_ref[r].astype(F32)
        o_ref[...] = acc

    return _call(
        body, name=name, grid=grid, prefetch=1,
        in_specs=[pl.BlockSpec(block, g_map), pl.BlockSpec((others,) + block, p_map)],
        out_specs=pl.BlockSpec(block, o_map), out_shape=jax.ShapeDtypeStruct(out_shape, F32),
    )(pos, g, parts)


def _join_halves(bufs):
    n = len(bufs)

    def copies(ins, outs, sems):
        send_sems, recv_sems = sems
        x, y, c = _mesh_pos()
        sends = [_remote(_half(src, 0, c), _half(dst, 0, c), send_sems.at[t], recv_sems.at[t], (x, y, 1 - c))
                 for t, (src, dst) in enumerate(zip(ins, outs))]
        recvs = [_remote(_half(src, 0, 1 - c), _half(dst, 0, 1 - c), send_sems.at[t], recv_sems.at[t],
                         (x, y, 1 - c)) for t, (src, dst) in enumerate(zip(ins, outs))]
        return sends, recvs

    def start(ins, outs, sems):
        for cp in copies(ins, outs, sems)[0]:
            cp.start()

    def finish(ins, outs, sems):
        sends, recvs = copies(ins, outs, sems)
        for cp in recvs:
            cp.wait_recv()
        for cp in sends:
            cp.wait_send()

    comm = _Comm(bufs, [jax.ShapeDtypeStruct(b.shape, b.dtype) for b in bufs],
                 [_dma_sems(n), _dma_sems(n)], start, finish, aliases={i: i for i in range(n)})
    return _call(None, name="join_halves", comm=comm)()[1]


def _allreduce_small(pack):
    rows, cols = pack.shape

    def body(p_ref, o_ref, slots, send_sems, recv_sems):
        x, y, c = _mesh_pos()
        me = 4 * x + 2 * y + c
        slots[me] = p_ref[...]
        copies = []
        for r in range(1, N_DEV):
            fx, fy, fc = (r >> 2) & 1, (r >> 1) & 1, r & 1
            dev = ((1 - x) if fx else x, (1 - y) if fy else y, (1 - c) if fc else c)
            cp = pltpu.make_async_remote_copy(
                src_ref=p_ref, dst_ref=slots.at[me], send_sem=send_sems.at[r - 1],
                recv_sem=recv_sems.at[r - 1], device_id=dev, device_id_type=MESH)
            cp.start()
            copies.append(cp)
        for cp in copies:
            cp.wait_recv()
        for cp in copies:
            cp.wait_send()
        acc = slots[0]
        for d in range(1, N_DEV):
            acc = acc + slots[d]
        o_ref[...] = acc

    vmem = pl.BlockSpec(memory_space=pltpu.VMEM)
    return _call(
        body, name="allreduce_small", in_specs=[vmem], out_specs=vmem,
        out_shape=jax.ShapeDtypeStruct((rows, cols), F32),
        scratch=[pltpu.VMEM((N_DEV, rows, cols), F32), pltpu.SemaphoreType.DMA((N_DEV - 1,)),
                 pltpu.SemaphoreType.DMA((N_DEV - 1,))],
    )(pack)


def _pad_rows(a, rows=8):
    return jnp.pad(a, ((0, rows - a.shape[0]), (0, 0)))


def kernel(x, positions, norm1_w, w_in, merge_gate_b, ret_gn_w, w_ret_o, lru_conv_w, lru_conv_b, lru_w_r, lru_b_r, lru_w_i, lru_b_i, lru_lambda, w_lru_o, w_out, norm2_w, ffn_w_up, ffn_conv_w, ffn_conv_b, ffn_w_down, norm_f_w, loss_target, m_norm1_w, m_w_in, m_merge_gate_b, m_ret_gn_w, m_w_ret_o, m_lru_conv_w, m_lru_conv_b, m_lru_w_r, m_lru_b_r, m_lru_w_i, m_lru_b_i, m_lru_lambda, m_w_lru_o, m_w_out, m_norm2_w, m_ffn_w_up, m_ffn_conv_w, m_ffn_conv_b, m_ffn_w_down, m_norm_f_w, v_norm1_w, v_w_in, v_merge_gate_b, v_ret_gn_w, v_w_ret_o, v_lru_conv_w, v_lru_conv_b, v_lru_w_r, v_lru_b_r, v_lru_w_i, v_lru_b_i, v_lru_lambda, v_w_lru_o, v_w_out, v_norm2_w, v_ffn_w_up, v_ffn_conv_w, v_ffn_conv_b, v_ffn_w_down, v_norm_f_w):
    names = ["norm1_w", "w_in", "merge_gate_b", "ret_gn_w", "w_ret_o", "lru_conv_w", "lru_conv_b", "lru_w_r",
             "lru_b_r", "lru_w_i", "lru_b_i", "lru_lambda", "w_lru_o", "w_out", "norm2_w", "ffn_w_up",
             "ffn_conv_w", "ffn_conv_b", "ffn_w_down", "norm_f_w"]
    w_args = dict(zip(names, [norm1_w, w_in, merge_gate_b, ret_gn_w, w_ret_o, lru_conv_w, lru_conv_b, lru_w_r,
                              lru_b_r, lru_w_i, lru_b_i, lru_lambda, w_lru_o, w_out, norm2_w, ffn_w_up,
                              ffn_conv_w, ffn_conv_b, ffn_w_down, norm_f_w]))
    m_args = dict(zip(names, [m_norm1_w, m_w_in, m_merge_gate_b, m_ret_gn_w, m_w_ret_o, m_lru_conv_w,
                              m_lru_conv_b, m_lru_w_r, m_lru_b_r, m_lru_w_i, m_lru_b_i, m_lru_lambda, m_w_lru_o,
                              m_w_out, m_norm2_w, m_ffn_w_up, m_ffn_conv_w, m_ffn_conv_b, m_ffn_w_down,
                              m_norm_f_w]))
    v_args = dict(zip(names, [v_norm1_w, v_w_in, v_merge_gate_b, v_ret_gn_w, v_w_ret_o, v_lru_conv_w,
                              v_lru_conv_b, v_lru_w_r, v_lru_b_r, v_lru_w_i, v_lru_b_i, v_lru_lambda, v_w_lru_o,
                              v_w_out, v_norm2_w, v_ffn_w_up, v_ffn_conv_w, v_ffn_conv_b, v_ffn_w_down,
                              v_norm_f_w]))

    bsz, seq, d = x.shape
    m = bsz * seq
    tm = min(MM_ROWS, m)
    tm_fused = min(FUSED_ROWS, m)
    tm_tall = min(TALL_ROWS, m)
    chip = 2 * lax.axis_index("x") + lax.axis_index("y")

    big = ["w_in", "w_ret_o", "w_lru_o", "w_out", "lru_w_r", "lru_w_i", "ffn_w_up", "ffn_w_down"]
    cut = dict(w_in=(1, 0), w_ret_o=(0, 0), w_lru_o=(0, 0), w_out=(0, 0), lru_w_r=(1, 0), lru_w_i=(1, 0),
               ffn_w_up=(1, 0), ffn_w_down=(0, 0))
    core = lax.axis_index("c")
    chip1 = jnp.reshape(chip, (1,)).astype(jnp.int32)
    pos = jnp.stack([chip, core]).astype(jnp.int32)
    placed = {n: _place_shard(w_args[n][0], chip1, cut[n][0], name="place_" + n) for n in big}
    small_pack = jnp.concatenate([
        jnp.pad(merge_gate_b[0], ((0, 6), (0, 512))),
        jnp.pad(lru_conv_w[0], ((0, 4), (0, 512))),
        jnp.pad(lru_b_r[0], ((0, 4), (0, 704))),
        jnp.pad(lru_b_i[0], ((0, 4), (0, 704))),
        jnp.pad(ffn_conv_w[0], ((0, 5), (0, 0))),
    ], axis=0)
    x2d = x.reshape(m, d)
    mx, my = lax.axis_index("x"), lax.axis_index("y")
    order = jnp.stack([chip, 2 * (1 - mx) + my, 2 * mx + 1 - my, 2 * (1 - mx) + 1 - my]).astype(jnp.int32)
    mixer = ["w_ret_o", "w_lru_o", "w_out", "lru_w_r", "lru_w_i"]
    cuts = lambda ns: [cut[n] for n in ns]
    sizes = lambda ns: [w_args[n].shape[1 + cut[n][0]] for n in ns]
    h1 = _norm_bf16(x2d, norm1_w, name="norm1", tm=tm)
    proj, w_in_full, bufs, sp = _in_proj_gather(h1, placed["w_in"], [placed[n] for n in mixer], cuts(mixer),
                                               small_pack, order, tm=tm_tall)
    wb = {"w_in": w_in_full}
    mgb = jnp.transpose(sp[:, 0:2, 0:256], (1, 0, 2)).reshape(2, D_MODEL)
    lcw = jnp.transpose(sp[:, 8:12, 0:256], (1, 0, 2)).reshape(4, D_MODEL)
    lbr = jnp.transpose(sp[:, 16:20, 0:64], (1, 0, 2)).reshape(1, D_MODEL)
    lbi = jnp.transpose(sp[:, 24:28, 0:64], (1, 0, 2)).reshape(1, D_MODEL)
    fcw = jnp.transpose(sp[:, 32:35, :], (1, 0, 2)).reshape(3, D_FF)
    nfw = norm_f_w.reshape(1, D_MODEL)

    half = RET_DK // 2
    inv_freq = ROPE_BASE ** (-jnp.arange(half, dtype=F32) / half)
    cos, sin = _rope_tables(positions.reshape(bsz, seq, 1), jnp.concatenate([inv_freq, inv_freq]).reshape(1, RET_DK))
    proj3 = proj.reshape(bsz, seq, D_IN)
    down, up_w = ["ffn_w_down"], ["ffn_w_up"]
    (a_in3, states), bufs = _retention_fwd(
        proj3, cos, sin, ret_gn_w,
        comm=_both(_gather_d2d(bufs, cuts(mixer), sizes(mixer)), _gather_ici([placed["ffn_w_down"]], cuts(down))))
    wb.update(zip(mixer, bufs[:len(mixer)]))
    lru_params = (lcw, lru_conv_b, wb["lru_w_r"], lbr, wb["lru_w_i"], lbi, lru_lambda)
    (b_in3, *lru_kept), (up_buf, wb["ffn_w_down"]) = _lru_fwd(
        proj3, lru_params,
        comm=_both(_gather_ici([placed["ffn_w_up"]], cuts(up_w)), _gather_d2d(bufs[len(mixer):], cuts(down), sizes(down))))
    a_in, b_in = a_in3.reshape(m, d), b_in3.reshape(m, d)
    (x2, mix, ya, yb), (wb["ffn_w_up"],) = _merge_fwd(
        a_in, b_in, proj, mgb, wb["w_ret_o"], wb["w_lru_o"], wb["w_out"], x2d, tm=tm_fused,
        comm=_gather_d2d([up_buf], cuts(up_w), sizes(up_w)))
    up, h2 = _norm_matmul(x2, norm2_w, wb["ffn_w_up"], name="ffn_up", tm=tm_tall, tn=TALL_COLS)
    up3 = up.reshape(bsz, seq, 2 * D_FF)
    f3, act3, q3 = _ffn_act_fwd(up3, fcw, ffn_conv_b)
    f = f3.reshape(m, D_FF)
    loss_dev, dx3, sm_nf = _ffn_down_loss(f, wb["ffn_w_down"], x2, nfw, loss_target.reshape(m, d), tm=tm)
    loss = lax.psum(loss_dev[0, 0], ("x", "y", "c"))

    def send(*ns):
        return _exchange([g_full[n] for n in ns], [cut[n] for n in ns])

    g_full, parts = {}, {}
    df = _mm_nt(dx3, wb["ffn_w_down"], name="ffn_down_dx", tm=tm_tall, out_dtype=BF16)
    g_full["ffn_w_down"] = _mm_tn(f, [dx3], name="ffn_down_dw", tm=tm_tall)
    (dgate3, dval3, sm_ffn), (parts["ffn_w_down"],) = _ffn_act_bwd(
        up3, act3, q3, fcw, df.reshape(bsz, seq, D_FF), comm=send("ffn_w_down"))
    dup = [dgate3.reshape(m, D_FF), dval3.reshape(m, D_FF)]
    g_full["ffn_w_up"] = _mm_tn(h2, dup, name="ffn_up_dw", tm=tm_tall)
    (dx2, sm_n2), (parts["ffn_w_up"],) = _mm_nt_normbwd(
        dup, wb["ffn_w_up"], x2, norm2_w, dx3, name="ffn_up_dx", tm=tm, row=ROW_N2, comm=send("ffn_w_up"))
    dya, dyb, da_in, db_in, dgates, sm_mg = _merge_bwd(
        ya, yb, proj, mgb, wb["w_ret_o"], wb["w_lru_o"], wb["w_out"], dx2, tm=tm_fused)
    g_full["w_out"] = _mm_tn(mix, [dx2], name="out_dw", tm=tm_tall)
    g_full["w_ret_o"] = _mm_tn(a_in, [dya], name="ret_o_dw", tm=tm_tall)
    g_full["w_lru_o"] = _mm_tn(b_in, [dyb], name="lru_o_dw", tm=tm_tall)
    (dlru3, dwr, dwi, sm_lru), (parts["w_out"], parts["w_ret_o"], parts["w_lru_o"]) = _lru_bwd(
        proj3, lru_params, lru_kept, db_in.reshape(bsz, seq, d), comm=send("w_out", "w_ret_o", "w_lru_o"))
    g_full["lru_w_r"], g_full["lru_w_i"] = dwr.astype(BF16), dwi.astype(BF16)
    (dret3, sm_gn), (parts["lru_w_r"], parts["lru_w_i"]) = _retention_bwd(
        proj3, cos, sin, ret_gn_w, states, da_in.reshape(bsz, seq, d), comm=send("lru_w_r", "lru_w_i"))
    dproj = [dret3.reshape(m, 3072), dlru3.reshape(m, 2048), dgates]
    g_full["w_in"] = _mm_tn(h1, dproj, name="in_proj_dw", tm=tm_tall)
    (grad_x, sm_n1), (parts["w_in"],) = _mm_nt_normbwd(
        dproj, wb["w_in"], x2d, norm1_w, dx2, name="in_proj_dx", tm=tm, row=ROW_N1, comm=send("w_in"))

    reduced = _join_halves([_reduce_half(g_full[n], parts[n], pos, cut[n], name="sum_" + n) for n in big])
    misc = sm_n1 + sm_mg + sm_gn + sm_n2 + sm_nf
    pack = jnp.concatenate(
        [misc, sm_lru, sm_ffn[:, 0:1024], sm_ffn[:, 1024:2048], sm_ffn[:, 2048:3072]], axis=0)
    tot = _allreduce_small(pack)
    ffn_sm = jnp.concatenate([tot[16:24], tot[24:32], tot[32:40]], axis=1)
    g_small = {
        "norm1_w": tot[ROW_N1:ROW_N1 + 1], "merge_gate_b": tot[ROW_MGB:ROW_MGB + 2],
        "ret_gn_w": tot[ROW_GN:ROW_GN + 1], "norm2_w": tot[ROW_N2:ROW_N2 + 1], "norm_f_w": tot[ROW_NF:ROW_NF + 1],
        "lru_conv_w": tot[8:12], "lru_conv_b": tot[12:13], "lru_b_r": tot[13:14].reshape(4, 256),
        "lru_b_i": tot[14:15].reshape(4, 256), "lru_lambda": tot[15:16],
        "ffn_conv_w": ffn_sm[0:3], "ffn_conv_b": ffn_sm[3:4],
    }
    small_shard = dict(merge_gate_b=256, lru_conv_w=256, lru_b_r=64, lru_b_i=64, ffn_conv_w=768)

    outs = {}
    for n, g in zip(big, reduced):
        shape = w_args[n].shape
        g = g.reshape(-1, g.shape[-1])
        outs[n] = [o.reshape(shape) for o in _adamw(
            w_args[n].reshape(g.shape), [g], m_args[n].reshape(g.shape), v_args[n].reshape(g.shape),
            name="adamw_" + n)]
    for n, g in g_small.items():
        shape = w_args[n].shape
        if n in small_shard:
            g = lax.dynamic_slice_in_dim(g, chip * small_shard[n], small_shard[n], axis=1)
        w2 = w_args[n].reshape(g.shape)
        outs[n] = [o.reshape(shape) for o in _adamw(
            w2, [g], m_args[n].reshape(g.shape), v_args[n].reshape(g.shape), name="adamw_" + n)]

    result = [loss, grad_x.reshape(bsz, seq, d)]
    for k in range(4):
        result += [outs[n][k] for n in names]
    return tuple(result)
```

```python
import functools
import math

import numpy as np
import jax
import jax.numpy as jnp
from jax import lax
from jax.experimental import pallas as pl
from jax.experimental.pallas import tpu as pltpu

F32 = jnp.float32
BF16 = jnp.bfloat16

D_MODEL = 1024
RET_HEADS = 4
RET_DK = 128
RET_DV = 256
LRU_BLOCKS = 4
LRU_BLOCK = 256
LRU_C = 8.0
D_FF = 3072
D_IN = 7168
ROPE_BASE = 10000.0
RMS_EPS = 1e-6
GN_EPS = 1e-6
ADAM_LR, ADAM_B1, ADAM_B2, ADAM_EPS, ADAM_WD, ADAM_STEP = 0.001, 0.9, 0.999, 1e-08, 0.01, 10

N_CHIPS = 4
N_DEV = 8
SEQ_T = 256
REF_CHUNK = 64
COL = 1024
MM_ROWS = 1024
TALL_ROWS, TALL_COLS = 2048, 1024
FUSED_ROWS = 512
VMEM_LIMIT_BYTES = 56 * 1024 * 1024
MESH = pl.DeviceIdType.MESH
ROW_N1, ROW_MGB, ROW_GN, ROW_N2, ROW_NF, ROW_LOSS = 0, 1, 3, 4, 5, 6
GELU_K = math.sqrt(2.0 / math.pi)
GELU_C = 0.044715


class _Comm:
    def __init__(self, ins, outs, sems, start, finish, aliases=None):
        self.ins, self.outs, self.sems = list(ins), list(outs), list(sems)
        self.start, self.finish, self.aliases = start, finish, dict(aliases or {})


def _call(body, *, name, out_shape=(), grid=None, in_specs=(), out_specs=(), scratch=(), comm=None, prefetch=0,
          aliases=None):
    single = not isinstance(out_shape, (list, tuple))
    out_shape = [out_shape] if single else list(out_shape)
    out_specs = [out_specs] if single else list(out_specs)
    in_specs, scratch = list(in_specs), list(scratch)
    n_in, n_out, n_scr = len(in_specs), len(out_shape), len(scratch)
    kwargs = dict(name=name, compiler_params=pltpu.CompilerParams(vmem_limit_bytes=VMEM_LIMIT_BYTES))
    if prefetch:
        assert comm is None
        spec = pltpu.PrefetchScalarGridSpec(num_scalar_prefetch=prefetch, grid=grid, in_specs=in_specs,
                                            out_specs=out_specs, scratch_shapes=scratch)
        fn = pl.pallas_call(body, out_shape=out_shape, grid_spec=spec, input_output_aliases=dict(aliases or {}),
                            **kwargs)
        return (lambda *args: fn(*args)[0]) if single else fn
    if grid is not None:
        kwargs["grid"] = grid
    if comm is None:
        fn = pl.pallas_call(body, out_shape=out_shape, in_specs=in_specs, out_specs=out_specs,
                            scratch_shapes=scratch, **kwargs)
        return (lambda *args: fn(*args)[0]) if single else fn

    any_spec = pl.BlockSpec(memory_space=pl.ANY)
    n_cin, n_cout = len(comm.ins), len(comm.outs)

    def wrapped(*refs):
        ins, refs = refs[:n_in], refs[n_in:]
        cins, refs = refs[:n_cin], refs[n_cin:]
        outs, refs = refs[:n_out], refs[n_out:]
        couts, refs = refs[:n_cout], refs[n_cout:]
        scr, csems = refs[:n_scr], refs[n_scr:]
        if grid is None:
            comm.start(cins, couts, csems)
            comm.finish(cins, couts, csems)
            return
        ids = [pl.program_id(a) for a in range(len(grid))]
        first = functools.reduce(jnp.logical_and, [i == 0 for i in ids])
        last = functools.reduce(jnp.logical_and, [i == g - 1 for i, g in zip(ids, grid)])
        pl.when(first)(lambda: comm.start(cins, couts, csems))
        body(*ins, *outs, *scr)
        pl.when(last)(lambda: comm.finish(cins, couts, csems))

    fn = pl.pallas_call(
        wrapped, out_shape=out_shape + comm.outs, in_specs=in_specs + [any_spec] * n_cin,
        out_specs=out_specs + [any_spec] * n_cout, scratch_shapes=scratch + comm.sems,
        input_output_aliases={n_in + i: n_out + o for i, o in comm.aliases.items()}, **kwargs)

    def run(*args):
        res = fn(*args, *comm.ins)
        own = res[0] if single else list(res[:n_out])
        return own, list(res[n_out:])

    return run


def _dot(a, b):
    return jnp.dot(a, b, preferred_element_type=F32)


def _dot_nt(a, b):
    return lax.dot_general(a, b, (((1,), (1,)), ((), ())), preferred_element_type=F32)


def _dot_tn(a, b):
    return lax.dot_general(a, b, (((0,), (0,)), ((), ())), preferred_element_type=F32)


def _bf(x):
    return x.astype(BF16)


def _sigmoid(x):
    return 1.0 / (1.0 + jnp.exp(-x))


def _gelu_and_grad(x):
    x2 = x * x
    s = _sigmoid(x * (2.0 * GELU_K * GELU_C * x2 + 2.0 * GELU_K))
    g = x * s
    dg = s + g * (1.0 - s) * (6.0 * GELU_K * GELU_C * x2 + 2.0 * GELU_K)
    return g, dg


def _rms(x):
    r = lax.rsqrt(jnp.mean(x * x, axis=-1, keepdims=True) + RMS_EPS)
    return x * r, r


def _rms_bwd(dy, x, nw):
    xh, r = _rms(x)
    g = dy * nw
    dx = r * (g - xh * jnp.mean(g * xh, axis=-1, keepdims=True))
    return dx, jnp.sum(dy * xh, axis=0, keepdims=True)


def _row_acc(ref, row, val):
    ref[row:row + 1, :] = ref[row:row + 1, :] + val


def _shift_down(x, j, prev8):
    if j == 0:
        return x
    n = x.shape[0] // 8
    row = lax.broadcasted_iota(jnp.int32, prev8.shape, 0)
    turned = [pltpu.roll(prev8, j, 0)] + [pltpu.roll(x[8 * k:8 * k + 8], j, 0) for k in range(n)]
    return jnp.concatenate([jnp.where(row < j, turned[k], turned[k + 1]) for k in range(n)], axis=0)


def _shift_up(x, j, next8):
    if j == 0:
        return x
    n = x.shape[0] // 8
    row = lax.broadcasted_iota(jnp.int32, next8.shape, 0)
    turned = [pltpu.roll(x[8 * k:8 * k + 8], 8 - j, 0) for k in range(n)] + [pltpu.roll(next8, 8 - j, 0)]
    return jnp.concatenate([jnp.where(row >= 8 - j, turned[k + 1], turned[k]) for k in range(n)], axis=0)


def _scan_fwd(a, b, carry):
    row = lax.broadcasted_iota(jnp.int32, (8, a.shape[1]), 0)
    out = []
    for k in range(a.shape[0] // 8):
        ak, bk = a[8 * k:8 * k + 8], b[8 * k:8 * k + 8]
        for s in (1, 2, 4):
            keep = row >= s
            ar, br = pltpu.roll(ak, s, 0), pltpu.roll(bk, s, 0)
            bk = jnp.where(keep, ak * br + bk, bk)
            ak = jnp.where(keep, ak * ar, ak)
        hk = ak * carry + bk
        carry = hk[7:8]
        out.append(hk)
    return jnp.concatenate(out, axis=0)


def _scan_bwd(a, b, carry):
    row = lax.broadcasted_iota(jnp.int32, (8, a.shape[1]), 0)
    out = []
    for k in reversed(range(a.shape[0] // 8)):
        ak, bk = a[8 * k:8 * k + 8], b[8 * k:8 * k + 8]
        for s in (1, 2, 4):
            keep = row < 8 - s
            ar, br = pltpu.roll(ak, 8 - s, 0), pltpu.roll(bk, 8 - s, 0)
            bk = jnp.where(keep, ak * br + bk, bk)
            ak = jnp.where(keep, ak * ar, ak)
        gk = bk + ak * carry
        carry = gk[0:1]
        out.append(gk)
    return jnp.concatenate(out[::-1], axis=0)


def _norm_matmul(x, nw, w, *, name, tm, tn):
    m, d = x.shape
    n = w.shape[1]

    def body(x_ref, nw_ref, w_ref, o_ref, h_ref, h_sc):
        @pl.when(pl.program_id(1) == 0)
        def _():
            xh, _ = _rms(x_ref[...])
            h = _bf(xh * nw_ref[...])
            h_sc[...] = h
            h_ref[...] = h

        o_ref[...] = _bf(_dot(h_sc[...], w_ref[...]))

    return _call(
        body, name=name, grid=(m // tm, n // tn),
        in_specs=[pl.BlockSpec((tm, d), lambda i, j: (i, 0)),
                  pl.BlockSpec((1, d), lambda i, j: (0, 0)),
                  pl.BlockSpec((d, tn), lambda i, j: (0, j))],
        out_specs=[pl.BlockSpec((tm, tn), lambda i, j: (i, j)),
                   pl.BlockSpec((tm, d), lambda i, j: (i, 0))],
        out_shape=[jax.ShapeDtypeStruct((m, n), BF16), jax.ShapeDtypeStruct((m, d), BF16)],
        scratch=[pltpu.VMEM((tm, d), BF16)],
    )(x, nw, w)


def _mm_nt(a, w, *, name, tm, out_dtype):
    m, k = a.shape
    n = w.shape[0]

    def body(a_ref, w_ref, o_ref):
        o_ref[...] = _dot_nt(_bf(a_ref[...]), w_ref[...]).astype(out_dtype)

    return _call(
        body, name=name, grid=(m // tm, n // COL),
        in_specs=[pl.BlockSpec((tm, k), lambda i, j: (i, 0)),
                  pl.BlockSpec((COL, k), lambda i, j: (j, 0))],
        out_specs=pl.BlockSpec((tm, COL), lambda i, j: (i, j)),
        out_shape=jax.ShapeDtypeStruct((m, n), out_dtype),
    )(a, w)


def _piece_layout(pieces):
    offs, nblk, o = [], [], 0
    for p in pieces:
        offs.append(o)
        nblk.append(p.shape[1] // COL)
        o += p.shape[1] // COL
    return offs, nblk, o


def _mm_tn(a, pieces, *, name, tm, out_dtype=BF16):
    m, k = a.shape
    offs, nblk, nn = _piece_layout(pieces)

    def piece_spec(o, nb):
        def idx(ki, nj, mi):
            use = jnp.logical_and(nj >= o, nj < o + nb)
            return (jnp.where(use, mi, 0), jnp.clip(nj - o, 0, nb - 1))
        return pl.BlockSpec((tm, COL), idx)

    def body(a_ref, *rest):
        p_refs, o_ref, acc = rest[:len(pieces)], rest[len(pieces)], rest[len(pieces) + 1]
        nj, mi = pl.program_id(1), pl.program_id(2)

        @pl.when(mi == 0)
        def _():
            acc[...] = jnp.zeros_like(acc)

        for p_ref, o, nb in zip(p_refs, offs, nblk):
            @pl.when(jnp.logical_and(nj >= o, nj < o + nb))
            def _(p_ref=p_ref):
                acc[...] += _dot_tn(_bf(a_ref[...]), _bf(p_ref[...]))

        @pl.when(mi == pl.num_programs(2) - 1)
        def _():
            o_ref[...] = acc[...].astype(out_dtype)

    return _call(
        body, name=name, grid=(k // COL, nn, m // tm),
        in_specs=[pl.BlockSpec((tm, COL), lambda ki, nj, mi: (mi, ki))]
        + [piece_spec(o, nb) for o, nb in zip(offs, nblk)],
        out_specs=pl.BlockSpec((COL, COL), lambda ki, nj, mi: (ki, nj)),
        out_shape=jax.ShapeDtypeStruct((k, nn * COL), out_dtype),
        scratch=[pltpu.VMEM((COL, COL), F32)],
    )(a, *pieces)


def _mm_nt_normbwd(pieces, w, x, nw, dres, *, name, tm, row, comm=None):
    m, d = x.shape
    offs, nblk, nk = _piece_layout(pieces)

    def piece_spec(o, nb):
        return pl.BlockSpec((tm, COL), lambda i, k: (i, jnp.clip(k - o, 0, nb - 1)))

    def body(*refs):
        p_refs = refs[:len(pieces)]
        w_ref, x_ref, nw_ref, dres_ref, dx_ref, dnw_ref, acc = refs[len(pieces):]
        i, k = pl.program_id(0), pl.program_id(1)

        @pl.when(jnp.logical_and(i == 0, k == 0))
        def _():
            dnw_ref[...] = jnp.zeros_like(dnw_ref)

        @pl.when(k == 0)
        def _():
            acc[...] = jnp.zeros_like(acc)

        for p_ref, o, nb in zip(p_refs, offs, nblk):
            @pl.when(jnp.logical_and(k >= o, k < o + nb))
            def _(p_ref=p_ref):
                acc[...] += _dot_nt(_bf(p_ref[...]), w_ref[...])

        @pl.when(k == nk - 1)
        def _():
            dx, dnw = _rms_bwd(acc[...], x_ref[...], nw_ref[...])
            dx_ref[...] = dres_ref[...] + dx
            _row_acc(dnw_ref, row, dnw)

    return _call(
        body, name=name, grid=(m // tm, nk), comm=comm,
        in_specs=[piece_spec(o, nb) for o, nb in zip(offs, nblk)]
        + [pl.BlockSpec((d, COL), lambda i, k: (0, k)),
           pl.BlockSpec((tm, d), lambda i, k: (i, 0)),
           pl.BlockSpec((1, d), lambda i, k: (0, 0)),
           pl.BlockSpec((tm, d), lambda i, k: (i, 0))],
        out_specs=[pl.BlockSpec((tm, d), lambda i, k: (i, 0)),
                   pl.BlockSpec((8, d), lambda i, k: (0, 0))],
        out_shape=[jax.ShapeDtypeStruct((m, d), F32), jax.ShapeDtypeStruct((8, d), F32)],
        scratch=[pltpu.VMEM((tm, d), F32)],
    )(*pieces, w, x, nw, dres)


def _rope_tables(pos3, invf):
    b, s, _ = pos3.shape

    def body(pos_ref, invf_ref, cos_ref, sin_ref):
        half_t, half_d = SEQ_T // 2, RET_DK // 2
        pos = pos_ref[...].astype(F32)
        low = lax.broadcasted_iota(jnp.int32, (half_t, RET_DK), 1) < half_d
        ang = jnp.where(low, pos[0:half_t], pos[half_t:]) * invf_ref[...]
        c, s = jnp.cos(ang), jnp.sin(ang)
        c_turned, s_turned = pltpu.roll(c, half_d, 1), pltpu.roll(s, half_d, 1)
        sign = jnp.where(low, -1.0, 1.0)
        cos_ref[...] = jnp.concatenate([jnp.where(low, c, c_turned), jnp.where(low, c_turned, c)], axis=0)
        sin_ref[...] = jnp.concatenate([sign * jnp.where(low, s, s_turned), sign * jnp.where(low, s_turned, s)],
                                       axis=0)

    spec = pl.BlockSpec((None, SEQ_T, RET_DK), lambda i, c: (i, c, 0))
    return _call(
        body, name="rope_tables", grid=(b, s // SEQ_T),
        in_specs=[pl.BlockSpec((None, SEQ_T, 1), lambda i, c: (i, c, 0)),
                  pl.BlockSpec((1, RET_DK), lambda i, c: (0, 0))],
        out_specs=[spec, spec],
        out_shape=[jax.ShapeDtypeStruct((b, s, RET_DK), F32)] * 2,
    )(pos3, invf)


def _log_gamma(h):
    return float(np.log1p(-np.power(np.float32(2.0), np.float32(-5.0 - h))).astype(np.float32))


def _decay_matrix(h):
    lg = _log_gamma(h)
    n = lax.broadcasted_iota(jnp.int32, (SEQ_T, SEQ_T), 0)
    m = lax.broadcasted_iota(jnp.int32, (SEQ_T, SEQ_T), 1)
    same = (n // REF_CHUNK) == (m // REF_CHUNK)
    dist = jnp.where(same, jnp.abs(n - m), n - m).astype(F32)
    return jnp.where(jnp.logical_or(same, m < n), jnp.exp(lg * dist), 0.0)


def _decay_vectors(h):
    lg = _log_gamma(h)
    idx = lax.broadcasted_iota(jnp.int32, (SEQ_T, 1), 0).astype(F32)
    qd = jnp.exp(lg * (idx + 1.0))
    kd = jnp.exp(lg * (SEQ_T - 1.0 - idx))
    return qd, kd, math.exp(lg * SEQ_T)


def _rotate(x, cos, sin):
    return x * cos + pltpu.roll(x, RET_DK // 2, 1) * sin


def _rotate_bwd(d, cos, sin):
    return d * cos + pltpu.roll(d * sin, RET_DK // 2, 1)


def _ret_head(p_ref, cos, sin, h):
    q = p_ref[:, h * RET_DK:(h + 1) * RET_DK].astype(F32)
    k = p_ref[:, 512 + h * RET_DK:512 + (h + 1) * RET_DK].astype(F32)
    v = p_ref[:, 1024 + h * RET_DV:1024 + (h + 1) * RET_DV]
    g = p_ref[:, 2048 + h * RET_DV:2048 + (h + 1) * RET_DV].astype(F32)
    qr = _rotate(q, cos, sin)
    kr = _rotate(k, cos, sin) * (RET_DK ** -0.5)
    return qr, kr, v, g


def _group_norm(o):
    mu = jnp.mean(o, axis=-1, keepdims=True)
    oc = o - mu
    rstd = lax.rsqrt(jnp.mean(oc * oc, axis=-1, keepdims=True) + GN_EPS)
    return oc * rstd, rstd


def _retention_fwd(proj3, cos, sin, gnw, comm=None):
    b, s, _ = proj3.shape
    nc = s // SEQ_T

    def body(p_ref, cos_ref, sin_ref, gnw_ref, a_ref, st_ref, state, wtab):
        c = pl.program_id(1)

        @pl.when(jnp.logical_and(pl.program_id(0) == 0, c == 0))
        def _():
            for h in range(RET_HEADS):
                wtab[h] = _decay_matrix(h)

        @pl.when(c == 0)
        def _():
            state[...] = jnp.zeros_like(state)

        cs, sn = cos_ref[...], sin_ref[...]
        for h in range(RET_HEADS):
            qd, kd, gt = _decay_vectors(h)
            qr, kr, v, g = _ret_head(p_ref, cs, sn, h)
            st = state[h]
            st_ref[h] = st
            p = _dot_nt(_bf(qr), _bf(kr)) * wtab[h]
            o = _dot(_bf(p), _bf(v)) + _dot(_bf(qr * qd), _bf(st))
            state[h] = st * gt + _dot_tn(_bf(kr * kd), _bf(v))
            on, _ = _group_norm(o)
            gw = gnw_ref[:, h * RET_DV:(h + 1) * RET_DV]
            a_ref[:, h * RET_DV:(h + 1) * RET_DV] = _bf(on * gw * (g * _sigmoid(g)))

    tab = pl.BlockSpec((None, SEQ_T, RET_DK), lambda i, c: (i, c, 0))
    return _call(
        body, name="retention_fwd", grid=(b, nc), comm=comm,
        in_specs=[pl.BlockSpec((None, SEQ_T, 3072), lambda i, c: (i, c, 0)), tab, tab,
                  pl.BlockSpec((1, D_MODEL), lambda i, c: (0, 0))],
        out_specs=[pl.BlockSpec((None, SEQ_T, D_MODEL), lambda i, c: (i, c, 0)),
                   pl.BlockSpec((None, None, RET_HEADS, RET_DK, RET_DV), lambda i, c: (i, c, 0, 0, 0))],
        out_shape=[jax.ShapeDtypeStruct((b, s, D_MODEL), BF16),
                   jax.ShapeDtypeStruct((b, nc, RET_HEADS, RET_DK, RET_DV), F32)],
        scratch=[pltpu.VMEM((RET_HEADS, RET_DK, RET_DV), F32),
                 pltpu.VMEM((RET_HEADS, SEQ_T, SEQ_T), F32)],
    )(proj3, cos, sin, gnw)


def _retention_bwd(proj3, cos, sin, gnw, states, da3, comm=None):
    b, s, _ = proj3.shape
    nc = s // SEQ_T

    def body(p_ref, cos_ref, sin_ref, gnw_ref, st_ref, da_ref, d_ref, dgn_ref, dstate, wtab):
        c = pl.program_id(1)

        @pl.when(jnp.logical_and(pl.program_id(0) == 0, c == 0))
        def _():
            dgn_ref[...] = jnp.zeros_like(dgn_ref)
            for h in range(RET_HEADS):
                wtab[h] = _decay_matrix(h)

        @pl.when(c == 0)
        def _():
            dstate[...] = jnp.zeros_like(dstate)

        cs, sn = cos_ref[...], sin_ref[...]
        for h in range(RET_HEADS):
            qd, kd, gt = _decay_vectors(h)
            qr, kr, v, g = _ret_head(p_ref, cs, sn, h)
            st, dst, w = st_ref[h], dstate[h], wtab[h]
            qb, kb, vb = _bf(qr), _bf(kr), _bf(v)
            p = _dot_nt(qb, kb) * w
            o = _dot(_bf(p), vb) + _dot(_bf(qr * qd), _bf(st))
            on, rstd = _group_norm(o)
            gw = gnw_ref[:, h * RET_DV:(h + 1) * RET_DV]
            da = da_ref[:, h * RET_DV:(h + 1) * RET_DV].astype(F32)
            sg = _sigmoid(g)
            silu = g * sg
            dg = da * on * gw * (sg * (1.0 + g * (1.0 - sg)))
            dgn_ref[ROW_GN:ROW_GN + 1, h * RET_DV:(h + 1) * RET_DV] += jnp.sum(da * silu * on, axis=0, keepdims=True)
            don = da * silu * gw
            do = rstd * (don - jnp.mean(don, axis=-1, keepdims=True)
                         - on * jnp.mean(don * on, axis=-1, keepdims=True))
            dob = _bf(do)
            dp = _dot_nt(dob, vb) * w
            dqr = _dot(_bf(dp), kb) + _dot_nt(dob, _bf(st)) * qd
            dkr = _dot_tn(_bf(dp), qb) + _dot_nt(vb, _bf(dst)) * kd
            dv = _dot_tn(_bf(p), dob) + _dot(_bf(kr * kd), _bf(dst))
            dstate[h] = dst * gt + _dot_tn(_bf(qr * qd), dob)
            d_ref[:, h * RET_DK:(h + 1) * RET_DK] = _bf(_rotate_bwd(dqr, cs, sn))
            d_ref[:, 512 + h * RET_DK:512 + (h + 1) * RET_DK] = _bf(_rotate_bwd(dkr, cs, sn) * (RET_DK ** -0.5))
            d_ref[:, 1024 + h * RET_DV:1024 + (h + 1) * RET_DV] = _bf(dv)
            d_ref[:, 2048 + h * RET_DV:2048 + (h + 1) * RET_DV] = _bf(dg)

    rev = lambda i, c: (i, nc - 1 - c, 0)
    tab = pl.BlockSpec((None, SEQ_T, RET_DK), rev)
    return _call(
        body, name="retention_bwd", grid=(b, nc), comm=comm,
        in_specs=[pl.BlockSpec((None, SEQ_T, 3072), rev), tab, tab,
                  pl.BlockSpec((1, D_MODEL), lambda i, c: (0, 0)),
                  pl.BlockSpec((None, None, RET_HEADS, RET_DK, RET_DV), lambda i, c: (i, nc - 1 - c, 0, 0, 0)),
                  pl.BlockSpec((None, SEQ_T, D_MODEL), rev)],
        out_specs=[pl.BlockSpec((None, SEQ_T, 3072), rev),
                   pl.BlockSpec((8, D_MODEL), lambda i, c: (0, 0))],
        out_shape=[jax.ShapeDtypeStruct((b, s, 3072), BF16), jax.ShapeDtypeStruct((8, D_MODEL), F32)],
        scratch=[pltpu.VMEM((RET_HEADS, RET_DK, RET_DV), F32),
                 pltpu.VMEM((RET_HEADS, SEQ_T, SEQ_T), F32)],
    )(proj3, cos, sin, gnw, states, da3)


def _softplus_neg(lam):
    z = -lam
    u = jnp.exp(-jnp.abs(z))
    log1p_u = jnp.where(u < 0.01, u * (1.0 - u * (0.5 - u * (1.0 / 3.0))), jnp.log(1.0 + u))
    return jnp.maximum(z, 0.0) + log1p_u


def _lru_coeffs(xc, wr_ref, br_ref, wi_ref, bi_ref, lam_ref):
    rs, is_ = [], []
    for n in range(LRU_BLOCKS):
        xb = _bf(xc[:, n * LRU_BLOCK:(n + 1) * LRU_BLOCK])
        cols = slice(n * LRU_BLOCK, (n + 1) * LRU_BLOCK)
        rs.append(_sigmoid(_dot(xb, wr_ref[n]) + br_ref[:, cols]))
        is_.append(_sigmoid(_dot(xb, wi_ref[n]) + bi_ref[:, cols]))
    r = jnp.concatenate(rs, axis=1)
    i = jnp.concatenate(is_, axis=1)
    sp = _softplus_neg(lam_ref[...])
    la = -LRU_C * r * sp
    a = jnp.exp(la)
    s = jnp.sqrt(-jnp.tanh(la) * (a * a + 1.0))
    return r, i, a, s, sp


_LRU_PARAM_SPECS = [
    pl.BlockSpec((4, D_MODEL), lambda i, c: (0, 0)),
    pl.BlockSpec((1, D_MODEL), lambda i, c: (0, 0)),
    pl.BlockSpec((LRU_BLOCKS, LRU_BLOCK, LRU_BLOCK), lambda i, c: (0, 0, 0)),
    pl.BlockSpec((1, D_MODEL), lambda i, c: (0, 0)),
    pl.BlockSpec((LRU_BLOCKS, LRU_BLOCK, LRU_BLOCK), lambda i, c: (0, 0, 0)),
    pl.BlockSpec((1, D_MODEL), lambda i, c: (0, 0)),
    pl.BlockSpec((1, D_MODEL), lambda i, c: (0, 0)),
]


def _lru_fwd(proj3, params, comm=None):
    b, s, _ = proj3.shape
    nc = s // SEQ_T

    def body(x_ref, y_ref, cw, cb, wr, br, wi, bi, lam,
             o_ref, h_ref, xc_ref, a_ref, s_ref, gy_ref, hdg_ref, r_ref, i_ref, xprev, hprev):
        @pl.when(pl.program_id(1) == 0)
        def _():
            xprev[...] = jnp.zeros_like(xprev)
            hprev[...] = jnp.zeros_like(hprev)

        x = x_ref[...].astype(F32)
        prev8 = xprev[...]
        xc = cb[...] + sum(cw[j:j + 1, :] * _shift_down(x, 3 - j, prev8) for j in range(4))
        xprev[...] = x[SEQ_T - 8:]
        xc_ref[...] = xc
        r, i, a, s_, _ = _lru_coeffs(xc, wr, br, wi, bi, lam)
        a_ref[...] = a
        s_ref[...] = s_
        r_ref[...] = _bf(r)
        i_ref[...] = _bf(i)
        h = _scan_fwd(a, s_ * (i * xc), hprev[7:8, :])
        hprev[...] = h[SEQ_T - 8:]
        h_ref[...] = h
        gy, dgy = _gelu_and_grad(y_ref[...].astype(F32))
        o_ref[...] = _bf(h * gy)
        gy_ref[...] = _bf(gy)
        hdg_ref[...] = _bf(h * dgy)

    out = pl.BlockSpec((None, SEQ_T, D_MODEL), lambda i, c: (i, c, 0))
    half, full = jax.ShapeDtypeStruct((b, s, D_MODEL), BF16), jax.ShapeDtypeStruct((b, s, D_MODEL), F32)
    return _call(
        body, name="lru_fwd", grid=(b, nc), comm=comm,
        in_specs=[pl.BlockSpec((None, SEQ_T, D_MODEL), lambda i, c: (i, c, 3)),
                  pl.BlockSpec((None, SEQ_T, D_MODEL), lambda i, c: (i, c, 4))] + _LRU_PARAM_SPECS,
        out_specs=[out] * 9, out_shape=[half, full, full, full, full, half, half, half, half],
        scratch=[pltpu.VMEM((8, D_MODEL), F32), pltpu.VMEM((8, D_MODEL), F32)],
    )(proj3, proj3, *params)


def _lru_bwd(proj3, params, kept, db3, comm=None):
    b, s, _ = proj3.shape
    nc = s // SEQ_T
    blk8 = SEQ_T // 8
    hseq = kept[0]

    def body(x_ref, h_ref, xc_ref, a_ref, s_ref, gy_ref, hdg_ref, r_ref, i_ref, hp_ref, db_ref,
             cw, cb, wr, br, wi, bi, lam, d_ref, dwr_ref, dwi_ref, sm_ref, gnext, anext, dxcnext):
        c = pl.program_id(1)
        first_chunk = c == nc - 1

        @pl.when(jnp.logical_and(pl.program_id(0) == 0, c == 0))
        def _():
            dwr_ref[...] = jnp.zeros_like(dwr_ref)
            dwi_ref[...] = jnp.zeros_like(dwi_ref)
            sm_ref[...] = jnp.zeros_like(sm_ref)

        @pl.when(c == 0)
        def _():
            gnext[...] = jnp.zeros_like(gnext)
            anext[...] = jnp.zeros_like(anext)
            dxcnext[...] = jnp.zeros_like(dxcnext)

        x, xc, h = x_ref[...].astype(F32), xc_ref[...], h_ref[...]
        hprev = hp_ref[...] * jnp.where(first_chunk, 0.0, 1.0)
        r, i, a, s_ = r_ref[...].astype(F32), i_ref[...].astype(F32), a_ref[...], s_ref[...]
        sp = _softplus_neg(lam[...])
        db = db_ref[...].astype(F32)
        dy = db * hdg_ref[...].astype(F32)
        a_up = _shift_up(a, 1, anext[...])
        g = _scan_bwd(a_up, db * gy_ref[...].astype(F32), gnext[0:1, :])
        gnext[...] = g[0:8]
        anext[...] = a[0:8]
        da = g * _shift_down(h, 1, hprev)
        ixc = i * xc
        dla = da * a - (g * ixc) * (a * a) / s_
        di = g * s_ * xc
        dxc = g * s_ * i
        dzr = dla * (-LRU_C * sp) * r * (1.0 - r)
        dzi = di * i * (1.0 - i)
        lam_v = lam[...]
        _row_acc(sm_ref, 7, jnp.sum(dla * (LRU_C * r), axis=0, keepdims=True) * _sigmoid(-lam_v))
        _row_acc(sm_ref, 5, jnp.sum(dzr, axis=0, keepdims=True))
        _row_acc(sm_ref, 6, jnp.sum(dzi, axis=0, keepdims=True))
        parts = []
        for n in range(LRU_BLOCKS):
            cols = slice(n * LRU_BLOCK, (n + 1) * LRU_BLOCK)
            xb, zr, zi = _bf(xc[:, cols]), _bf(dzr[:, cols]), _bf(dzi[:, cols])
            parts.append(dxc[:, cols] + _dot_nt(zr, wr[n]) + _dot_nt(zi, wi[n]))
            dwr_ref[n] += _dot_tn(xb, zr)
            dwi_ref[n] += _dot_tn(xb, zi)
        dxc = jnp.concatenate(parts, axis=1)
        _row_acc(sm_ref, 4, jnp.sum(dxc, axis=0, keepdims=True))
        nxt = dxcnext[...]
        dx = jnp.zeros_like(x)
        for j in range(4):
            ahead = _shift_up(dxc, 3 - j, nxt)
            dx = dx + cw[j:j + 1, :] * ahead
            _row_acc(sm_ref, j, jnp.sum(ahead * x, axis=0, keepdims=True))
        dxcnext[...] = dxc[0:8]
        d_ref[:, 0:D_MODEL] = _bf(dx)
        d_ref[:, D_MODEL:2 * D_MODEL] = _bf(dy)

    rev = lambda col: (lambda i, c: (i, nc - 1 - c, col))
    prev = lambda col: (lambda i, c: (i, jnp.maximum((nc - 1 - c) * blk8 - 1, 0), col))
    return _call(
        body, name="lru_bwd", grid=(b, nc), comm=comm,
        in_specs=[pl.BlockSpec((None, SEQ_T, D_MODEL), rev(3))]
        + [pl.BlockSpec((None, SEQ_T, D_MODEL), rev(0))] * len(kept)
        + [pl.BlockSpec((None, 8, D_MODEL), prev(0)), pl.BlockSpec((None, SEQ_T, D_MODEL), rev(0))]
        + _LRU_PARAM_SPECS,
        out_specs=[pl.BlockSpec((None, SEQ_T, 2 * D_MODEL), rev(0)),
                   pl.BlockSpec((LRU_BLOCKS, LRU_BLOCK, LRU_BLOCK), lambda i, c: (0, 0, 0)),
                   pl.BlockSpec((LRU_BLOCKS, LRU_BLOCK, LRU_BLOCK), lambda i, c: (0, 0, 0)),
                   pl.BlockSpec((8, D_MODEL), lambda i, c: (0, 0))],
        out_shape=[jax.ShapeDtypeStruct((b, s, 2 * D_MODEL), BF16),
                   jax.ShapeDtypeStruct((LRU_BLOCKS, LRU_BLOCK, LRU_BLOCK), F32),
                   jax.ShapeDtypeStruct((LRU_BLOCKS, LRU_BLOCK, LRU_BLOCK), F32),
                   jax.ShapeDtypeStruct((8, D_MODEL), F32)],
        scratch=[pltpu.VMEM((8, D_MODEL), F32)] * 3,
    )(proj3, *kept, hseq, db3, *params)


def _merge_parts(a_ref, b_ref, gr_ref, gl_ref, mgb_ref, wro_ref, wlo_ref):
    ya = _dot(a_ref[...], wro_ref[...])
    yb = _dot(b_ref[...], wlo_ref[...])
    sa = _sigmoid(gr_ref[...].astype(F32) + mgb_ref[0:1, :])
    sb = _sigmoid(gl_ref[...].astype(F32) + mgb_ref[1:2, :])
    return ya, yb, sa, sb


def _merge_specs(tm):
    row = lambda col: pl.BlockSpec((tm, D_MODEL), lambda i: (i, col))
    full = pl.BlockSpec((D_MODEL, D_MODEL), lambda i: (0, 0))
    return row, full


def _merge_fwd(a_in, b_in, proj, mgb, wro, wlo, wout, x, *, tm, comm=None):
    m = x.shape[0]
    row, full = _merge_specs(tm)

    def body(a_ref, b_ref, gr_ref, gl_ref, mgb_ref, wro_ref, wlo_ref, wout_ref, x_ref,
             o_ref, mix_ref, ya_ref, yb_ref):
        ya, yb, sa, sb = _merge_parts(a_ref, b_ref, gr_ref, gl_ref, mgb_ref, wro_ref, wlo_ref)
        mix = _bf(sa * ya + sb * yb)
        o_ref[...] = x_ref[...] + _dot(mix, wout_ref[...])
        mix_ref[...] = mix
        ya_ref[...] = _bf(ya)
        yb_ref[...] = _bf(yb)

    act = jax.ShapeDtypeStruct((m, D_MODEL), BF16)
    return _call(
        body, name="merge_fwd", grid=(m // tm,), comm=comm,
        in_specs=[row(0), row(0), row(5), row(6), pl.BlockSpec((2, D_MODEL), lambda i: (0, 0)),
                  full, full, full, row(0)],
        out_specs=[row(0)] * 4,
        out_shape=[jax.ShapeDtypeStruct((m, D_MODEL), F32), act, act, act],
    )(a_in, b_in, proj, proj, mgb, wro, wlo, wout, x)


def _merge_bwd(ya, yb, proj, mgb, wro, wlo, wout, dx2, *, tm):
    m = dx2.shape[0]
    row, full = _merge_specs(tm)

    def body(ya_ref, yb_ref, gr_ref, gl_ref, mgb_ref, wro_ref, wlo_ref, wout_ref, dx_ref,
             dya_ref, dyb_ref, da_ref, db_ref, dg_ref, sm_ref):
        @pl.when(pl.program_id(0) == 0)
        def _():
            sm_ref[...] = jnp.zeros_like(sm_ref)

        ya, yb = ya_ref[...].astype(F32), yb_ref[...].astype(F32)
        sa = _sigmoid(gr_ref[...].astype(F32) + mgb_ref[0:1, :])
        sb = _sigmoid(gl_ref[...].astype(F32) + mgb_ref[1:2, :])
        dmix = _dot_nt(_bf(dx_ref[...]), wout_ref[...])
        dya, dyb = _bf(dmix * sa), _bf(dmix * sb)
        dya_ref[...] = dya
        dyb_ref[...] = dyb
        dga = dmix * ya * sa * (1.0 - sa)
        dgb = dmix * yb * sb * (1.0 - sb)
        dg_ref[:, 0:D_MODEL] = _bf(dga)
        dg_ref[:, D_MODEL:2 * D_MODEL] = _bf(dgb)
        _row_acc(sm_ref, ROW_MGB, jnp.sum(dga, axis=0, keepdims=True))
        _row_acc(sm_ref, ROW_MGB + 1, jnp.sum(dgb, axis=0, keepdims=True))
        da_ref[...] = _bf(_dot_nt(dya, wro_ref[...]))
        db_ref[...] = _bf(_dot_nt(dyb, wlo_ref[...]))

    act = jax.ShapeDtypeStruct((m, D_MODEL), BF16)
    return _call(
        body, name="merge_bwd", grid=(m // tm,),
        in_specs=[row(0), row(0), row(5), row(6), pl.BlockSpec((2, D_MODEL), lambda i: (0, 0)),
                  full, full, full, row(0)],
        out_specs=[row(0)] * 4 + [pl.BlockSpec((tm, 2 * D_MODEL), lambda i: (i, 0)),
                                  pl.BlockSpec((8, D_MODEL), lambda i: (0, 0))],
        out_shape=[act] * 4 + [jax.ShapeDtypeStruct((m, 2 * D_MODEL), BF16),
                               jax.ShapeDtypeStruct((8, D_MODEL), F32)],
    )(ya, yb, proj, proj, mgb, wro, wlo, wout, dx2)


def _ffn_act_fwd(up3, cw, cb):
    b, s, _ = up3.shape

    def body(g_ref, v_ref, cw_ref, cb_ref, o_ref, act_ref, q_ref, gprev):
        @pl.when(pl.program_id(1) == 0)
        def _():
            gprev[...] = jnp.zeros_like(gprev)

        gate, val = g_ref[...].astype(F32), v_ref[...].astype(F32)
        prev8 = gprev[...]
        gc = cb_ref[...] + sum(cw_ref[j:j + 1, :] * _shift_down(gate, 2 - j, prev8) for j in range(3))
        gprev[...] = gate[SEQ_T - 8:]
        act, dact = _gelu_and_grad(gc)
        o_ref[...] = _bf(act * val)
        act_ref[...] = _bf(act)
        q_ref[...] = _bf(dact * val)

    out = pl.BlockSpec((None, SEQ_T, D_FF), lambda i, c: (i, c, 0))
    return _call(
        body, name="ffn_act_fwd", grid=(b, s // SEQ_T),
        in_specs=[pl.BlockSpec((None, SEQ_T, D_FF), lambda i, c: (i, c, 0)),
                  pl.BlockSpec((None, SEQ_T, D_FF), lambda i, c: (i, c, 1)),
                  pl.BlockSpec((3, D_FF), lambda i, c: (0, 0)),
                  pl.BlockSpec((1, D_FF), lambda i, c: (0, 0))],
        out_specs=[out] * 3,
        out_shape=[jax.ShapeDtypeStruct((b, s, D_FF), BF16)] * 3,
        scratch=[pltpu.VMEM((8, D_FF), F32)],
    )(up3, up3, cw, cb)


def _ffn_act_bwd(up3, act3, q3, cw, df3, comm=None):
    b, s, _ = up3.shape
    nc = s // SEQ_T

    def body(g_ref, act_ref, q_ref, df_ref, cw_ref, dg_ref, dv_ref, sm_ref, dgcnext):
        c = pl.program_id(1)

        @pl.when(jnp.logical_and(pl.program_id(0) == 0, c == 0))
        def _():
            sm_ref[...] = jnp.zeros_like(sm_ref)

        @pl.when(c == 0)
        def _():
            dgcnext[...] = jnp.zeros_like(dgcnext)

        gate = g_ref[...].astype(F32)
        df = df_ref[...].astype(F32)
        dv_ref[...] = _bf(df * act_ref[...].astype(F32))
        dgc = df * q_ref[...].astype(F32)
        nxt = dgcnext[...]
        dgate = jnp.zeros_like(gate)
        for j in range(3):
            ahead = _shift_up(dgc, 2 - j, nxt)
            dgate = dgate + cw_ref[j:j + 1, :] * ahead
            _row_acc(sm_ref, j, jnp.sum(ahead * gate, axis=0, keepdims=True))
        _row_acc(sm_ref, 3, jnp.sum(dgc, axis=0, keepdims=True))
        dgcnext[...] = dgc[0:8]
        dg_ref[...] = _bf(dgate)

    rev = pl.BlockSpec((None, SEQ_T, D_FF), lambda i, c: (i, nc - 1 - c, 0))
    return _call(
        body, name="ffn_act_bwd", grid=(b, nc), comm=comm,
        in_specs=[rev, rev, rev, rev, pl.BlockSpec((3, D_FF), lambda i, c: (0, 0))],
        out_specs=[rev, rev, pl.BlockSpec((8, D_FF), lambda i, c: (0, 0))],
        out_shape=[jax.ShapeDtypeStruct((b, s, D_FF), BF16)] * 2 + [jax.ShapeDtypeStruct((8, D_FF), F32)],
        scratch=[pltpu.VMEM((8, D_FF), F32)],
    )(up3, act3, q3, df3, cw)


def _ffn_down_loss(f, wd, x2, nfw, target, *, tm):
    m, kf = f.shape
    nt = m // tm

    def body(f_ref, wd_ref, x_ref, nw_ref, t_ref, dx_ref, dnw_ref, lsum):
        i = pl.program_id(0)

        @pl.when(i == 0)
        def _():
            dnw_ref[...] = jnp.zeros_like(dnw_ref)
            lsum[...] = jnp.zeros_like(lsum)

        x3 = x_ref[...] + _dot(f_ref[...], wd_ref[...])
        nw = nw_ref[...]
        xh, r = _rms(x3)
        err = xh * nw - t_ref[...]
        lsum[...] += jnp.sum(err * err, axis=0, keepdims=True)
        dy = err * (1.0 / D_MODEL)
        g = dy * nw
        dx_ref[...] = r * (g - xh * jnp.mean(g * xh, axis=-1, keepdims=True))
        _row_acc(dnw_ref, ROW_NF, jnp.sum(dy * xh, axis=0, keepdims=True))

        @pl.when(i == nt - 1)
        def _():
            loss = jnp.sum(lsum[...], axis=1, keepdims=True) * (0.5 / D_MODEL)
            dnw_ref[ROW_LOSS:ROW_LOSS + 1, :] = jnp.broadcast_to(loss, (1, D_MODEL))

    row = pl.BlockSpec((tm, D_MODEL), lambda i: (i, 0))
    return _call(
        body, name="ffn_down_loss", grid=(nt,),
        in_specs=[pl.BlockSpec((tm, kf), lambda i: (i, 0)),
                  pl.BlockSpec((kf, D_MODEL), lambda i: (0, 0)),
                  row, pl.BlockSpec((1, D_MODEL), lambda i: (0, 0)), row],
        out_specs=[row, pl.BlockSpec((8, D_MODEL), lambda i: (0, 0))],
        out_shape=[jax.ShapeDtypeStruct((m, D_MODEL), F32), jax.ShapeDtypeStruct((8, D_MODEL), F32)],
        scratch=[pltpu.VMEM((1, D_MODEL), F32)],
    )(f, wd, x2, nfw, target)


def _row_tile(rows):
    return next((t for t in (256, 128, 64, 32, 16, 8) if rows % t == 0), rows)


def _adamw(w, gs, m, v, *, name):
    rows, cols = w.shape
    tr = _row_tile(rows)
    ng = len(gs)

    def body(w_ref, *rest):
        g_refs, (m_ref, v_ref, g_out, d_out, m_out, v_out) = rest[:ng], rest[ng:]
        g = g_refs[0][...]
        for r in g_refs[1:]:
            g = g + r[...]
        mn = ADAM_B1 * m_ref[...] + (1.0 - ADAM_B1) * g
        vn = ADAM_B2 * v_ref[...] + (1.0 - ADAM_B2) * (g * g)
        m_hat = mn / (1.0 - ADAM_B1 ** ADAM_STEP)
        v_hat = vn / (1.0 - ADAM_B2 ** ADAM_STEP)
        g_out[...] = g
        d_out[...] = -ADAM_LR * (m_hat / (jnp.sqrt(v_hat) + ADAM_EPS) + ADAM_WD * w_ref[...])
        m_out[...] = mn
        v_out[...] = vn

    spec = pl.BlockSpec((tr, cols), lambda i: (i, 0))
    return _call(
        body, name=name, grid=(rows // tr,),
        in_specs=[spec] * (3 + ng), out_specs=[spec] * 4,
        out_shape=[jax.ShapeDtypeStruct((rows, cols), F32)] * 4,
    )(w, *gs, m, v)


def _mesh_pos():
    x, y, c = lax.axis_index("x"), lax.axis_index("y"), lax.axis_index("c")
    return x, y, c


def _other_chips(x, y, c):
    return [((1 - x, y, c), 2 * (1 - x) + y), ((x, 1 - y, c), 2 * x + 1 - y),
            ((1 - x, 1 - y, c), 2 * (1 - x) + 1 - y)]


def _region(ref, axis, size, half_axis, chip, core=None):
    idx = [slice(None)] * len(ref.shape)
    if core is None:
        idx[axis] = pl.ds(pl.multiple_of(chip * size, size), size)
    elif half_axis == axis:
        h = size // 2
        idx[axis] = pl.ds(pl.multiple_of(chip * size + core * h, h), h)
    else:
        idx[axis] = pl.ds(pl.multiple_of(chip * size, size), size)
        h = ref.shape[half_axis] // 2
        idx[half_axis] = pl.ds(pl.multiple_of(core * h, h), h)
    return ref.at[tuple(idx)]


def _half(ref, half_axis, core):
    idx = [slice(None)] * len(ref.shape)
    h = ref.shape[half_axis] // 2
    idx[half_axis] = pl.ds(pl.multiple_of(core * h, h), h)
    return ref.at[tuple(idx)]


class _Copy:
    def __init__(self, make):
        self._make = make

    def start(self):
        self._make().start()

    def wait(self):
        self._make().wait()

    def wait_send(self):
        self._make().wait_send()

    def wait_recv(self):
        self._make().wait_recv()


def _remote(src, dst, send_sem, recv_sem, dev):
    return _Copy(lambda: pltpu.make_async_remote_copy(
        src_ref=src, dst_ref=dst, send_sem=send_sem, recv_sem=recv_sem, device_id=dev, device_id_type=MESH))


def _local(src, dst, sem):
    return _Copy(lambda: pltpu.make_async_copy(src, dst, sem))


def _dma_sems(n):
    return pltpu.SemaphoreType.DMA((n,))


def _place_shard(w, chip, axis, *, name):
    shape = list(w.shape)
    shape[axis] *= N_CHIPS
    if w.ndim == 3:
        block, grid = (1,) + w.shape[1:], (w.shape[0],)
        in_map, out_map = (lambda i, chip: (i, 0, 0)), (lambda i, chip: (i, chip[0], 0))
    else:
        tr = _row_tile(w.shape[0])
        nt = w.shape[0] // tr
        block, grid = (tr, w.shape[1]), (nt,)
        in_map = lambda i, chip: (i, 0)
        out_map = (lambda i, chip: (chip[0] * nt + i, 0)) if axis == 0 else (lambda i, chip: (i, chip[0]))

    def body(chip_ref, w_ref, o_ref):
        o_ref[...] = _bf(w_ref[...])

    return _call(body, name=name, grid=grid, prefetch=1, in_specs=[pl.BlockSpec(block, in_map)],
                 out_specs=pl.BlockSpec(block, out_map),
                 out_shape=jax.ShapeDtypeStruct(tuple(shape), BF16))(chip, w)


def _ici_leg(srcs, dsts, layout, sizes, n_whole, sems):
    send_sems, recv_sems, local_sems = sems
    x, y, c = _mesh_pos()
    mine = 2 * x + y
    n_big = len(srcs) - n_whole
    local, sends, recvs = [], [], []
    for t, (src, dst) in enumerate(zip(srcs, dsts)):
        if t < n_big:
            ax, hx = layout[t]
            part = _region(src, ax, sizes[t], hx, mine, c)
            landing = lambda chip, dst=dst, ax=ax, hx=hx, size=sizes[t]: _region(dst, ax, size, hx, chip, c)
        else:
            part, landing = src, (lambda chip, dst=dst: dst.at[chip])
            local.append(_local(src, dst.at[mine], local_sems.at[t - n_big]))
        for k, (dev, chip) in enumerate(_other_chips(x, y, c)):
            sends.append(_remote(part, landing(mine), send_sems.at[3 * t + k], recv_sems.at[3 * t + k], dev))
            recvs.append(_remote(part, landing(chip), send_sems.at[3 * t + k], recv_sems.at[3 * t + k], dev))
    return local, sends, recvs


def _d2d_leg(srcs, dsts, layout, sizes, sems):
    send_sems, recv_sems = sems
    x, y, c = _mesh_pos()
    sends, recvs = [], []
    for t, (src, dst) in enumerate(zip(srcs, dsts)):
        ax, hx = layout[t]
        for k, (_, chip) in enumerate(_other_chips(x, y, c)):
            sem = (send_sems.at[3 * t + k], recv_sems.at[3 * t + k])
            sends.append(_remote(_region(src, ax, sizes[t], hx, chip, c),
                                 _region(dst, ax, sizes[t], hx, chip, c), *sem, (x, y, 1 - c)))
            recvs.append(_remote(_region(src, ax, sizes[t], hx, chip, 1 - c),
                                 _region(dst, ax, sizes[t], hx, chip, 1 - c), *sem, (x, y, 1 - c)))
    return sends, recvs


def _gather_shapes(bufs, whole):
    return ([jax.ShapeDtypeStruct(b.shape, b.dtype) for b in bufs]
            + [jax.ShapeDtypeStruct((N_CHIPS,) + w.shape, w.dtype) for w in whole])


def _gather_ici(bufs, layout):
    n = len(bufs)
    sizes = [b.shape[ax] // N_CHIPS for b, (ax, _) in zip(bufs, layout)]

    def start(ins, outs, sems):
        for cp in _ici_leg(ins, outs, layout, sizes, 0, (*sems, None))[1]:
            cp.start()

    def finish(ins, outs, sems):
        _, sends, recvs = _ici_leg(ins, outs, layout, sizes, 0, (*sems, None))
        for cp in recvs:
            cp.wait_recv()
        for cp in sends:
            cp.wait_send()

    return _Comm(bufs, _gather_shapes(bufs, ()), [_dma_sems(3 * n), _dma_sems(3 * n)], start, finish,
                 aliases={i: i for i in range(n)})


def _both(a, b):
    ni, no, ns = len(a.ins), len(a.outs), len(a.sems)

    def start(ins, outs, sems):
        a.start(ins[:ni], outs[:no], sems[:ns])
        b.start(ins[ni:], outs[no:], sems[ns:])

    def finish(ins, outs, sems):
        a.finish(ins[:ni], outs[:no], sems[:ns])
        b.finish(ins[ni:], outs[no:], sems[ns:])

    aliases = {**a.aliases, **{ni + i: no + o for i, o in b.aliases.items()}}
    return _Comm(a.ins + b.ins, a.outs + b.outs, a.sems + b.sems, start, finish, aliases)


def _gather_d2d(bufs, layout, sizes):
    n = len(bufs)

    def start(ins, outs, sems):
        for cp in _d2d_leg(ins, outs, layout, sizes, sems)[0]:
            cp.start()

    def finish(ins, outs, sems):
        sends, recvs = _d2d_leg(ins, outs, layout, sizes, sems)
        for cp in recvs:
            cp.wait_recv()
        for cp in sends:
            cp.wait_send()

    return _Comm(bufs, [jax.ShapeDtypeStruct(b.shape, b.dtype) for b in bufs],
                 [_dma_sems(3 * n), _dma_sems(3 * n)], start, finish, aliases={i: i for i in range(n)})


def _norm_bf16(x, nw, *, name, tm):
    m, d = x.shape

    def body(x_ref, nw_ref, h_ref):
        h_ref[...] = _bf(_rms(x_ref[...])[0] * nw_ref[...])

    row = pl.BlockSpec((tm, d), lambda i: (i, 0))
    return _call(body, name=name, grid=(m // tm,), in_specs=[row, pl.BlockSpec((1, d), lambda i: (0, 0))],
                 out_specs=row, out_shape=jax.ShapeDtypeStruct((m, d), BF16))(x, nw)


def _in_proj_gather(h1, w_buf, later, later_cut, small, order, *, tm):
    m, d = h1.shape
    width = w_buf.shape[1] // N_CHIPS
    nr, nl = m // tm, len(later)
    sizes = [b.shape[ax] // N_CHIPS for b, (ax, _) in zip(later, later_cut)]

    def body(order_ref, h_ref, w_in, *rest):
        later_in, small_in = rest[:nl], rest[nl]
        o_ref, w_out = rest[nl + 1], rest[nl + 2]
        later_out, small_out = rest[nl + 3:2 * nl + 3], rest[2 * nl + 3]
        wv, load_sem, ici_send, ici_recv, d2d_send, d2d_recv, l_send, l_recv, l_local = rest[2 * nl + 4:]
        s, i = pl.program_id(0), pl.program_id(1)
        x, y, c = _mesh_pos()
        mine = 2 * x + y
        peers = _other_chips(x, y, c)
        part = lambda ref, chip, core=None: _region(ref, 1, width, 0, chip, core)

        def ici(k):
            dev, chip = peers[k]
            sem = (ici_send.at[k], ici_recv.at[k])
            return (_remote(part(w_in, mine, c), part(w_out, mine, c), *sem, dev),
                    _remote(part(w_in, chip, c), part(w_out, chip, c), *sem, dev))

        def d2d(k):
            chip, sem, sib = peers[k][1], (d2d_send.at[k], d2d_recv.at[k]), (x, y, 1 - c)
            return (_remote(part(w_out, chip, c), part(w_out, chip, c), *sem, sib),
                    _remote(part(w_out, chip, 1 - c), part(w_out, chip, 1 - c), *sem, sib))

        def load(src, chip, slot):
            cp = _local(part(src, chip), wv.at[slot], load_sem.at[slot])
            cp.start()
            cp.wait()

        def others():
            return _ici_leg(list(later_in) + [small_in], list(later_out) + [small_out], later_cut, sizes, 1,
                            (l_send, l_recv, l_local))

        @pl.when(jnp.logical_and(s == 0, i == 0))
        def _():
            for k in range(3):
                ici(k)[0].start()
            local, sends, _ = others()
            for cp in local + sends:
                cp.start()
            load(w_in, mine, 0)

        o_ref[...] = _bf(_dot(h_ref[...], wv[s % 2]))

        @pl.when(i == nr - 1)
        def _():
            for k in range(3):
                @pl.when(s == k)
                def _(k=k):
                    ici(k)[1].wait_recv()
                    d2d(k)[0].start()
                    d2d(k)[1].wait_recv()
                    load(w_out, peers[k][1], (k + 1) % 2)

            @pl.when(s == 3)
            def _():
                for k in range(3):
                    ici(k)[0].wait_send()
                    d2d(k)[0].wait_send()
                local, sends, recvs = others()
                for cp in recvs:
                    cp.wait_recv()
                for cp in sends:
                    cp.wait_send()
                for cp in local:
                    cp.wait()

    any_spec = pl.BlockSpec(memory_space=pl.ANY)
    n_any = nl + 2
    outs = _call(
        body, name="in_proj", grid=(N_CHIPS, nr), prefetch=1,
        in_specs=[pl.BlockSpec((tm, d), lambda s, i, order: (i, 0))] + [any_spec] * n_any,
        out_specs=[pl.BlockSpec((tm, width), lambda s, i, order: (i, order[s]))] + [any_spec] * n_any,
        out_shape=[jax.ShapeDtypeStruct((m, w_buf.shape[1]), BF16)] + _gather_shapes([w_buf] + list(later), [small]),
        scratch=[pltpu.VMEM((2, d, width), BF16), _dma_sems(2), _dma_sems(3), _dma_sems(3), _dma_sems(3),
                 _dma_sems(3), _dma_sems(3 * (nl + 1)), _dma_sems(3 * (nl + 1)), _dma_sems(1)],
        aliases={2 + t: 1 + t for t in range(nl + 1)},
    )(order, h1, w_buf, *later, small)
    return outs[0], outs[1], list(outs[2:2 + nl]), outs[2 + nl]


def _exchange(grads, layout):
    n = len(grads)
    others = N_DEV - 1
    sizes = [g.shape[ax] // N_CHIPS for g, (ax, _) in zip(grads, layout)]
    out_shapes = []
    for g, (ax, hx), sz in zip(grads, layout, sizes):
        shape = list(g.shape)
        shape[ax] = sz
        shape[hx] //= 2
        out_shapes.append(jax.ShapeDtypeStruct((others,) + tuple(shape), g.dtype))

    def copies(ins, outs, sems):
        send_sems, recv_sems = sems
        x, y, c = _mesh_pos()
        sends, recvs = [], []
        for t, (src, dst) in enumerate(zip(ins, outs)):
            ax, hx = layout[t]
            for r in range(1, N_DEV):
                px = (1 - x) if r & 4 else x
                py = (1 - y) if r & 2 else y
                pc = (1 - c) if r & 1 else c
                sem = (send_sems.at[others * t + r - 1], recv_sems.at[others * t + r - 1])
                part = _region(src, ax, sizes[t], hx, 2 * px + py, pc)
                sends.append(_remote(part, dst.at[r - 1], *sem, (px, py, pc)))
                recvs.append(_remote(part, dst.at[r - 1], *sem, (px, py, pc)))
        return sends, recvs

    def start(ins, outs, sems):
        for cp in copies(ins, outs, sems)[0]:
            cp.start()

    def finish(ins, outs, sems):
        sends, recvs = copies(ins, outs, sems)
        for cp in recvs:
            cp.wait_recv()
        for cp in sends:
            cp.wait_send()

    return _Comm(grads, out_shapes, [_dma_sems(others * n), _dma_sems(others * n)], start, finish)


def _reduce_half(g, parts, pos, cut, *, name):
    ax, _ = cut
    others = parts.shape[0]
    if g.ndim == 3:
        nb, rows, cols = g.shape
        hb = nb // 2
        block, grid, out_shape = (1, rows // N_CHIPS, cols), (hb,), (nb, rows // N_CHIPS, cols)
        g_map = lambda i, pos: (pos[1] * hb + i, pos[0], 0)
        o_map = lambda i, pos: (pos[1] * hb + i, 0, 0)
        p_map = lambda i, pos: (0, i, 0, 0)
    elif ax == 1:
        rows, cols = g.shape
        tr = _row_tile(rows // 2)
        nt = rows // 2 // tr
        block, grid, out_shape = (tr, cols // N_CHIPS), (nt,), (rows, cols // N_CHIPS)
        g_map = lambda i, pos: (pos[1] * nt + i, pos[0])
        o_map = lambda i, pos: (pos[1] * nt + i, 0)
        p_map = lambda i, pos: (0, i, 0)
    else:
        rows, cols = g.shape
        tr = _row_tile(rows // N_CHIPS // 2)
        nt = rows // N_CHIPS // 2 // tr
        block, grid, out_shape = (tr, cols), (nt,), (rows // N_CHIPS, cols)
        g_map = lambda i, pos: (pos[0] * 2 * nt + pos[1] * nt + i, 0)
        o_map = lambda i, pos: (pos[1] * nt + i, 0)
        p_map = lambda i, pos: (0, i, 0)

    def body(pos_ref, g_ref, p_ref, o_ref):
        acc = g_ref[...].astype(F32)
        for r in range(others):
            acc = acc + p_ref[r].astype(F32)
        o_ref[...] = acc

    return _call(
        body, name=name, grid=grid, prefetch=1,
        in_specs=[pl.BlockSpec(block, g_map), pl.BlockSpec((others,) + block, p_map)],
        out_specs=pl.BlockSpec(block, o_map), out_shape=jax.ShapeDtypeStruct(out_shape, F32),
    )(pos, g, parts)


def _join_halves(bufs):
    return _call(None, name="join_halves", comm=_join_comm(bufs))()[1]


def _join_comm(bufs):
    n = len(bufs)

    def copies(ins, outs, sems):
        send_sems, recv_sems = sems
        x, y, c = _mesh_pos()
        sends = [_remote(_half(src, 0, c), _half(dst, 0, c), send_sems.at[t], recv_sems.at[t], (x, y, 1 - c))
                 for t, (src, dst) in enumerate(zip(ins, outs))]
        recvs = [_remote(_half(src, 0, 1 - c), _half(dst, 0, 1 - c), send_sems.at[t], recv_sems.at[t],
                         (x, y, 1 - c)) for t, (src, dst) in enumerate(zip(ins, outs))]
        return sends, recvs

    def start(ins, outs, sems):
        for cp in copies(ins, outs, sems)[0]:
            cp.start()

    def finish(ins, outs, sems):
        sends, recvs = copies(ins, outs, sems)
        for cp in recvs:
            cp.wait_recv()
        for cp in sends:
            cp.wait_send()

    return _Comm(bufs, [jax.ShapeDtypeStruct(b.shape, b.dtype) for b in bufs],
                 [_dma_sems(n), _dma_sems(n)], start, finish, aliases={i: i for i in range(n)})


def _allreduce_small(pack):
    rows, cols = pack.shape

    def body(p_ref, o_ref, slots, send_sems, recv_sems):
        x, y, c = _mesh_pos()
        me = 4 * x + 2 * y + c
        slots[me] = p_ref[...]
        copies = []
        for r in range(1, N_DEV):
            fx, fy, fc = (r >> 2) & 1, (r >> 1) & 1, r & 1
            dev = ((1 - x) if fx else x, (1 - y) if fy else y, (1 - c) if fc else c)
            cp = pltpu.make_async_remote_copy(
                src_ref=p_ref, dst_ref=slots.at[me], send_sem=send_sems.at[r - 1],
                recv_sem=recv_sems.at[r - 1], device_id=dev, device_id_type=MESH)
            cp.start()
            copies.append(cp)
        for cp in copies:
            cp.wait_recv()
        for cp in copies:
            cp.wait_send()
        acc = slots[0]
        for d in range(1, N_DEV):
            acc = acc + slots[d]
        o_ref[...] = acc

    vmem = pl.BlockSpec(memory_space=pltpu.VMEM)
    return _call(
        body, name="allreduce_small", in_specs=[vmem], out_specs=vmem,
        out_shape=jax.ShapeDtypeStruct((rows, cols), F32),
        scratch=[pltpu.VMEM((N_DEV, rows, cols), F32), pltpu.SemaphoreType.DMA((N_DEV - 1,)),
                 pltpu.SemaphoreType.DMA((N_DEV - 1,))],
    )(pack)


def _pad_rows(a, rows=8):
    return jnp.pad(a, ((0, rows - a.shape[0]), (0, 0)))


def kernel(x, positions, norm1_w, w_in, merge_gate_b, ret_gn_w, w_ret_o, lru_conv_w, lru_conv_b, lru_w_r, lru_b_r, lru_w_i, lru_b_i, lru_lambda, w_lru_o, w_out, norm2_w, ffn_w_up, ffn_conv_w, ffn_conv_b, ffn_w_down, norm_f_w, loss_target, m_norm1_w, m_w_in, m_merge_gate_b, m_ret_gn_w, m_w_ret_o, m_lru_conv_w, m_lru_conv_b, m_lru_w_r, m_lru_b_r, m_lru_w_i, m_lru_b_i, m_lru_lambda, m_w_lru_o, m_w_out, m_norm2_w, m_ffn_w_up, m_ffn_conv_w, m_ffn_conv_b, m_ffn_w_down, m_norm_f_w, v_norm1_w, v_w_in, v_merge_gate_b, v_ret_gn_w, v_w_ret_o, v_lru_conv_w, v_lru_conv_b, v_lru_w_r, v_lru_b_r, v_lru_w_i, v_lru_b_i, v_lru_lambda, v_w_lru_o, v_w_out, v_norm2_w, v_ffn_w_up, v_ffn_conv_w, v_ffn_conv_b, v_ffn_w_down, v_norm_f_w):
    names = ["norm1_w", "w_in", "merge_gate_b", "ret_gn_w", "w_ret_o", "lru_conv_w", "lru_conv_b", "lru_w_r",
             "lru_b_r", "lru_w_i", "lru_b_i", "lru_lambda", "w_lru_o", "w_out", "norm2_w", "ffn_w_up",
             "ffn_conv_w", "ffn_conv_b", "ffn_w_down", "norm_f_w"]
    w_args = dict(zip(names, [norm1_w, w_in, merge_gate_b, ret_gn_w, w_ret_o, lru_conv_w, lru_conv_b, lru_w_r,
                              lru_b_r, lru_w_i, lru_b_i, lru_lambda, w_lru_o, w_out, norm2_w, ffn_w_up,
                              ffn_conv_w, ffn_conv_b, ffn_w_down, norm_f_w]))
    m_args = dict(zip(names, [m_norm1_w, m_w_in, m_merge_gate_b, m_ret_gn_w, m_w_ret_o, m_lru_conv_w,
                              m_lru_conv_b, m_lru_w_r, m_lru_b_r, m_lru_w_i, m_lru_b_i, m_lru_lambda, m_w_lru_o,
                              m_w_out, m_norm2_w, m_ffn_w_up, m_ffn_conv_w, m_ffn_conv_b, m_ffn_w_down,
                              m_norm_f_w]))
    v_args = dict(zip(names, [v_norm1_w, v_w_in, v_merge_gate_b, v_ret_gn_w, v_w_ret_o, v_lru_conv_w,
                              v_lru_conv_b, v_lru_w_r, v_lru_b_r, v_lru_w_i, v_lru_b_i, v_lru_lambda, v_w_lru_o,
                              v_w_out, v_norm2_w, v_ffn_w_up, v_ffn_conv_w, v_ffn_conv_b, v_ffn_w_down,
                              v_norm_f_w]))

    bsz, seq, d = x.shape
    m = bsz * seq
    tm = min(MM_ROWS, m)
    tm_fused = min(FUSED_ROWS, m)
    tm_tall = min(TALL_ROWS, m)
    chip = 2 * lax.axis_index("x") + lax.axis_index("y")

    big = ["w_in", "w_ret_o", "w_lru_o", "w_out", "lru_w_r", "lru_w_i", "ffn_w_up", "ffn_w_down"]
    cut = dict(w_in=(1, 0), w_ret_o=(0, 0), w_lru_o=(0, 0), w_out=(0, 0), lru_w_r=(1, 0), lru_w_i=(1, 0),
               ffn_w_up=(1, 0), ffn_w_down=(0, 0))
    core = lax.axis_index("c")
    chip1 = jnp.reshape(chip, (1,)).astype(jnp.int32)
    pos = jnp.stack([chip, core]).astype(jnp.int32)
    placed = {n: _place_shard(w_args[n][0], chip1, cut[n][0], name="place_" + n) for n in big}
    small_pack = jnp.concatenate([
        jnp.pad(merge_gate_b[0], ((0, 6), (0, 512))),
        jnp.pad(lru_conv_w[0], ((0, 4), (0, 512))),
        jnp.pad(lru_b_r[0], ((0, 4), (0, 704))),
        jnp.pad(lru_b_i[0], ((0, 4), (0, 704))),
        jnp.pad(ffn_conv_w[0], ((0, 5), (0, 0))),
    ], axis=0)
    x2d = x.reshape(m, d)
    mx, my = lax.axis_index("x"), lax.axis_index("y")
    order = jnp.stack([chip, 2 * (1 - mx) + my, 2 * mx + 1 - my, 2 * (1 - mx) + 1 - my]).astype(jnp.int32)
    mixer = ["w_ret_o", "w_lru_o", "w_out", "lru_w_r", "lru_w_i"]
    cuts = lambda ns: [cut[n] for n in ns]
    sizes = lambda ns: [w_args[n].shape[1 + cut[n][0]] for n in ns]
    h1 = _norm_bf16(x2d, norm1_w, name="norm1", tm=tm)
    proj, w_in_full, bufs, sp = _in_proj_gather(h1, placed["w_in"], [placed[n] for n in mixer], cuts(mixer),
                                               small_pack, order, tm=tm_tall)
    wb = {"w_in": w_in_full}
    mgb = jnp.transpose(sp[:, 0:2, 0:256], (1, 0, 2)).reshape(2, D_MODEL)
    lcw = jnp.transpose(sp[:, 8:12, 0:256], (1, 0, 2)).reshape(4, D_MODEL)
    lbr = jnp.transpose(sp[:, 16:20, 0:64], (1, 0, 2)).reshape(1, D_MODEL)
    lbi = jnp.transpose(sp[:, 24:28, 0:64], (1, 0, 2)).reshape(1, D_MODEL)
    fcw = jnp.transpose(sp[:, 32:35, :], (1, 0, 2)).reshape(3, D_FF)
    nfw = norm_f_w.reshape(1, D_MODEL)

    half = RET_DK // 2
    inv_freq = ROPE_BASE ** (-jnp.arange(half, dtype=F32) / half)
    cos, sin = _rope_tables(positions.reshape(bsz, seq, 1), jnp.concatenate([inv_freq, inv_freq]).reshape(1, RET_DK))
    proj3 = proj.reshape(bsz, seq, D_IN)
    down, up_w = ["ffn_w_down"], ["ffn_w_up"]
    (a_in3, states), bufs = _retention_fwd(
        proj3, cos, sin, ret_gn_w,
        comm=_both(_gather_d2d(bufs, cuts(mixer), sizes(mixer)), _gather_ici([placed["ffn_w_down"]], cuts(down))))
    wb.update(zip(mixer, bufs[:len(mixer)]))
    lru_params = (lcw, lru_conv_b, wb["lru_w_r"], lbr, wb["lru_w_i"], lbi, lru_lambda)
    (b_in3, *lru_kept), (up_buf, wb["ffn_w_down"]) = _lru_fwd(
        proj3, lru_params,
        comm=_both(_gather_ici([placed["ffn_w_up"]], cuts(up_w)), _gather_d2d(bufs[len(mixer):], cuts(down), sizes(down))))
    a_in, b_in = a_in3.reshape(m, d), b_in3.reshape(m, d)
    (x2, mix, ya, yb), (wb["ffn_w_up"],) = _merge_fwd(
        a_in, b_in, proj, mgb, wb["w_ret_o"], wb["w_lru_o"], wb["w_out"], x2d, tm=tm_fused,
        comm=_gather_d2d([up_buf], cuts(up_w), sizes(up_w)))
    up, h2 = _norm_matmul(x2, norm2_w, wb["ffn_w_up"], name="ffn_up", tm=tm_tall, tn=TALL_COLS)
    up3 = up.reshape(bsz, seq, 2 * D_FF)
    f3, act3, q3 = _ffn_act_fwd(up3, fcw, ffn_conv_b)
    f = f3.reshape(m, D_FF)
    dx3, sm_nf = _ffn_down_loss(f, wb["ffn_w_down"], x2, nfw, loss_target.reshape(m, d), tm=tm)

    def send(*ns):
        return _exchange([g_full[n] for n in ns], [cut[n] for n in ns])

    g_full, parts = {}, {}
    df = _mm_nt(dx3, wb["ffn_w_down"], name="ffn_down_dx", tm=tm_tall, out_dtype=BF16)
    g_full["ffn_w_down"] = _mm_tn(f, [dx3], name="ffn_down_dw", tm=tm_tall)
    (dgate3, dval3, sm_ffn), (parts["ffn_w_down"],) = _ffn_act_bwd(
        up3, act3, q3, fcw, df.reshape(bsz, seq, D_FF), comm=send("ffn_w_down"))
    dup = [dgate3.reshape(m, D_FF), dval3.reshape(m, D_FF)]
    g_full["ffn_w_up"] = _mm_tn(h2, dup, name="ffn_up_dw", tm=tm_tall)
    (dx2, sm_n2), (parts["ffn_w_up"],) = _mm_nt_normbwd(
        dup, wb["ffn_w_up"], x2, norm2_w, dx3, name="ffn_up_dx", tm=tm, row=ROW_N2, comm=send("ffn_w_up"))
    dya, dyb, da_in, db_in, dgates, sm_mg = _merge_bwd(
        ya, yb, proj, mgb, wb["w_ret_o"], wb["w_lru_o"], wb["w_out"], dx2, tm=tm_fused)
    g_full["w_out"] = _mm_tn(mix, [dx2], name="out_dw", tm=tm_tall)
    g_full["w_ret_o"] = _mm_tn(a_in, [dya], name="ret_o_dw", tm=tm_tall)
    g_full["w_lru_o"] = _mm_tn(b_in, [dyb], name="lru_o_dw", tm=tm_tall)
    (dlru3, dwr, dwi, sm_lru), (parts["w_out"], parts["w_ret_o"], parts["w_lru_o"]) = _lru_bwd(
        proj3, lru_params, lru_kept, db_in.reshape(bsz, seq, d), comm=send("w_out", "w_ret_o", "w_lru_o"))
    g_full["lru_w_r"], g_full["lru_w_i"] = dwr.astype(BF16), dwi.astype(BF16)
    (dret3, sm_gn), (parts["lru_w_r"], parts["lru_w_i"]) = _retention_bwd(
        proj3, cos, sin, ret_gn_w, states, da_in.reshape(bsz, seq, d), comm=send("lru_w_r", "lru_w_i"))
    dproj = [dret3.reshape(m, 3072), dlru3.reshape(m, 2048), dgates]
    g_full["w_in"] = _mm_tn(h1, dproj, name="in_proj_dw", tm=tm_tall)
    half_sum = lambda n: _reduce_half(g_full[n], parts[n], pos, cut[n], name="sum_" + n)
    (grad_x, sm_n1), (parts["w_in"], *joined) = _mm_nt_normbwd(
        dproj, wb["w_in"], x2d, norm1_w, dx2, name="in_proj_dx", tm=tm, row=ROW_N1,
        comm=_both(send("w_in"), _join_comm([half_sum(n) for n in big[1:]])))
    reduced = _join_halves([half_sum("w_in")]) + joined
    misc = sm_n1 + sm_mg + sm_gn + sm_n2 + sm_nf
    pack = jnp.concatenate(
        [misc, sm_lru, sm_ffn[:, 0:1024], sm_ffn[:, 1024:2048], sm_ffn[:, 2048:3072]], axis=0)
    tot = _allreduce_small(pack)
    ffn_sm = jnp.concatenate([tot[16:24], tot[24:32], tot[32:40]], axis=1)
    g_small = {
        "norm1_w": tot[ROW_N1:ROW_N1 + 1], "merge_gate_b": tot[ROW_MGB:ROW_MGB + 2],
        "ret_gn_w": tot[ROW_GN:ROW_GN + 1], "norm2_w": tot[ROW_N2:ROW_N2 + 1], "norm_f_w": tot[ROW_NF:ROW_NF + 1],
        "lru_conv_w": tot[8:12], "lru_conv_b": tot[12:13], "lru_b_r": tot[13:14].reshape(4, 256),
        "lru_b_i": tot[14:15].reshape(4, 256), "lru_lambda": tot[15:16],
        "ffn_conv_w": ffn_sm[0:3], "ffn_conv_b": ffn_sm[3:4],
    }
    small_shard = dict(merge_gate_b=256, lru_conv_w=256, lru_b_r=64, lru_b_i=64, ffn_conv_w=768)

    outs = {}
    for n, g in zip(big, reduced):
        shape = w_args[n].shape
        g = g.reshape(-1, g.shape[-1])
        outs[n] = [o.reshape(shape) for o in _adamw(
            w_args[n].reshape(g.shape), [g], m_args[n].reshape(g.shape), v_args[n].reshape(g.shape),
            name="adamw_" + n)]
    for n, g in g_small.items():
        shape = w_args[n].shape
        if n in small_shard:
            g = lax.dynamic_slice_in_dim(g, chip * small_shard[n], small_shard[n], axis=1)
        w2 = w_args[n].reshape(g.shape)
        outs[n] = [o.reshape(shape) for o in _adamw(
            w2, [g], m_args[n].reshape(g.shape), v_args[n].reshape(g.shape), name="adamw_" + n)]

    result = [tot[ROW_LOSS, 0], grad_x.reshape(bsz, seq, d)]
    for k in range(4):
        result += [outs[n][k] for n in names]
    return tuple(result)
```

```python
import functools
import math

import numpy as np
import jax
import jax.numpy as jnp
from jax import lax
from jax.experimental import pallas as pl
from jax.experimental.pallas import tpu as pltpu

F32 = jnp.float32
BF16 = jnp.bfloat16

D_MODEL = 1024
RET_HEADS = 4
RET_DK = 128
RET_DV = 256
LRU_BLOCKS = 4
LRU_BLOCK = 256
LRU_C = 8.0
D_FF = 3072
D_IN = 7168
ROPE_BASE = 10000.0
RMS_EPS = 1e-6
GN_EPS = 1e-6
ADAM_LR, ADAM_B1, ADAM_B2, ADAM_EPS, ADAM_WD, ADAM_STEP = 0.001, 0.9, 0.999, 1e-08, 0.01, 10

N_CHIPS = 4
N_DEV = 8
SEQ_T = 256
REF_CHUNK = 64
COL = 1024
MM_ROWS = 1024
TALL_ROWS, TALL_COLS = 2048, 1024
FUSED_ROWS = 512
VMEM_LIMIT_BYTES = 56 * 1024 * 1024
MESH = pl.DeviceIdType.MESH
ROW_N1, ROW_MGB, ROW_GN, ROW_N2, ROW_NF, ROW_LOSS = 0, 1, 3, 4, 5, 6
GELU_K = math.sqrt(2.0 / math.pi)
GELU_C = 0.044715


class _Comm:
    def __init__(self, ins, outs, sems, start, finish, aliases=None):
        self.ins, self.outs, self.sems = list(ins), list(outs), list(sems)
        self.start, self.finish, self.aliases = start, finish, dict(aliases or {})


def _call(body, *, name, out_shape=(), grid=None, in_specs=(), out_specs=(), scratch=(), comm=None, prefetch=0,
          aliases=None):
    single = not isinstance(out_shape, (list, tuple))
    out_shape = [out_shape] if single else list(out_shape)
    out_specs = [out_specs] if single else list(out_specs)
    in_specs, scratch = list(in_specs), list(scratch)
    n_in, n_out, n_scr = len(in_specs), len(out_shape), len(scratch)
    kwargs = dict(name=name, compiler_params=pltpu.CompilerParams(vmem_limit_bytes=VMEM_LIMIT_BYTES))
    if prefetch:
        assert comm is None
        spec = pltpu.PrefetchScalarGridSpec(num_scalar_prefetch=prefetch, grid=grid, in_specs=in_specs,
                                            out_specs=out_specs, scratch_shapes=scratch)
        fn = pl.pallas_call(body, out_shape=out_shape, grid_spec=spec, input_output_aliases=dict(aliases or {}),
                            **kwargs)
        return (lambda *args: fn(*args)[0]) if single else fn
    if grid is not None:
        kwargs["grid"] = grid
    if comm is None:
        fn = pl.pallas_call(body, out_shape=out_shape, in_specs=in_specs, out_specs=out_specs,
                            scratch_shapes=scratch, **kwargs)
        return (lambda *args: fn(*args)[0]) if single else fn

    any_spec = pl.BlockSpec(memory_space=pl.ANY)
    n_cin, n_cout = len(comm.ins), len(comm.outs)

    def wrapped(*refs):
        ins, refs = refs[:n_in], refs[n_in:]
        cins, refs = refs[:n_cin], refs[n_cin:]
        outs, refs = refs[:n_out], refs[n_out:]
        couts, refs = refs[:n_cout], refs[n_cout:]
        scr, csems = refs[:n_scr], refs[n_scr:]
        if grid is None:
            comm.start(cins, couts, csems)
            comm.finish(cins, couts, csems)
            return
        ids = [pl.program_id(a) for a in range(len(grid))]
        first = functools.reduce(jnp.logical_and, [i == 0 for i in ids])
        last = functools.reduce(jnp.logical_and, [i == g - 1 for i, g in zip(ids, grid)])
        pl.when(first)(lambda: comm.start(cins, couts, csems))
        body(*ins, *outs, *scr)
        pl.when(last)(lambda: comm.finish(cins, couts, csems))

    fn = pl.pallas_call(
        wrapped, out_shape=out_shape + comm.outs, in_specs=in_specs + [any_spec] * n_cin,
        out_specs=out_specs + [any_spec] * n_cout, scratch_shapes=scratch + comm.sems,
        input_output_aliases={n_in + i: n_out + o for i, o in comm.aliases.items()}, **kwargs)

    def run(*args):
        res = fn(*args, *comm.ins)
        own = res[0] if single else list(res[:n_out])
        return own, list(res[n_out:])

    return run


def _dot(a, b):
    return jnp.dot(a, b, preferred_element_type=F32)


def _dot_nt(a, b):
    return lax.dot_general(a, b, (((1,), (1,)), ((), ())), preferred_element_type=F32)


def _dot_tn(a, b):
    return lax.dot_general(a, b, (((0,), (0,)), ((), ())), preferred_element_type=F32)


def _bf(x):
    return x.astype(BF16)


def _sigmoid(x):
    return 1.0 / (1.0 + jnp.exp(-x))


def _gelu_and_grad(x):
    x2 = x * x
    s = _sigmoid(x * (2.0 * GELU_K * GELU_C * x2 + 2.0 * GELU_K))
    g = x * s
    dg = s + g * (1.0 - s) * (6.0 * GELU_K * GELU_C * x2 + 2.0 * GELU_K)
    return g, dg


def _rms(x):
    r = lax.rsqrt(jnp.mean(x * x, axis=-1, keepdims=True) + RMS_EPS)
    return x * r, r


def _rms_bwd(dy, x, nw):
    xh, r = _rms(x)
    g = dy * nw
    dx = r * (g - xh * jnp.mean(g * xh, axis=-1, keepdims=True))
    return dx, jnp.sum(dy * xh, axis=0, keepdims=True)


def _row_acc(ref, row, val):
    ref[row:row + 1, :] = ref[row:row + 1, :] + val


def _shift_down(x, j, prev8):
    if j == 0:
        return x
    n = x.shape[0] // 8
    row = lax.broadcasted_iota(jnp.int32, prev8.shape, 0)
    turned = [pltpu.roll(prev8, j, 0)] + [pltpu.roll(x[8 * k:8 * k + 8], j, 0) for k in range(n)]
    return jnp.concatenate([jnp.where(row < j, turned[k], turned[k + 1]) for k in range(n)], axis=0)


def _shift_up(x, j, next8):
    if j == 0:
        return x
    n = x.shape[0] // 8
    row = lax.broadcasted_iota(jnp.int32, next8.shape, 0)
    turned = [pltpu.roll(x[8 * k:8 * k + 8], 8 - j, 0) for k in range(n)] + [pltpu.roll(next8, 8 - j, 0)]
    return jnp.concatenate([jnp.where(row >= 8 - j, turned[k + 1], turned[k]) for k in range(n)], axis=0)


def _scan_fwd(a, b, carry):
    row = lax.broadcasted_iota(jnp.int32, (8, a.shape[1]), 0)
    out = []
    for k in range(a.shape[0] // 8):
        ak, bk = a[8 * k:8 * k + 8], b[8 * k:8 * k + 8]
        for s in (1, 2, 4):
            keep = row >= s
            ar, br = pltpu.roll(ak, s, 0), pltpu.roll(bk, s, 0)
            bk = jnp.where(keep, ak * br + bk, bk)
            ak = jnp.where(keep, ak * ar, ak)
        hk = ak * carry + bk
        carry = hk[7:8]
        out.append(hk)
    return jnp.concatenate(out, axis=0)


def _scan_bwd(a, b, carry):
    row = lax.broadcasted_iota(jnp.int32, (8, a.shape[1]), 0)
    out = []
    for k in reversed(range(a.shape[0] // 8)):
        ak, bk = a[8 * k:8 * k + 8], b[8 * k:8 * k + 8]
        for s in (1, 2, 4):
            keep = row < 8 - s
            ar, br = pltpu.roll(ak, 8 - s, 0), pltpu.roll(bk, 8 - s, 0)
            bk = jnp.where(keep, ak * br + bk, bk)
            ak = jnp.where(keep, ak * ar, ak)
        gk = bk + ak * carry
        carry = gk[0:1]
        out.append(gk)
    return jnp.concatenate(out[::-1], axis=0)


def _norm_matmul(x, nw, w, *, name, tm, tn):
    m, d = x.shape
    n = w.shape[1]

    def body(x_ref, nw_ref, w_ref, o_ref, h_ref, h_sc):
        @pl.when(pl.program_id(1) == 0)
        def _():
            xh, _ = _rms(x_ref[...])
            h = _bf(xh * nw_ref[...])
            h_sc[...] = h
            h_ref[...] = h

        o_ref[...] = _bf(_dot(h_sc[...], w_ref[...]))

    return _call(
        body, name=name, grid=(m // tm, n // tn),
        in_specs=[pl.BlockSpec((tm, d), lambda i, j: (i, 0)),
                  pl.BlockSpec((1, d), lambda i, j: (0, 0)),
                  pl.BlockSpec((d, tn), lambda i, j: (0, j))],
        out_specs=[pl.BlockSpec((tm, tn), lambda i, j: (i, j)),
                   pl.BlockSpec((tm, d), lambda i, j: (i, 0))],
        out_shape=[jax.ShapeDtypeStruct((m, n), BF16), jax.ShapeDtypeStruct((m, d), BF16)],
        scratch=[pltpu.VMEM((tm, d), BF16)],
    )(x, nw, w)


def _mm_nt(a, w, *, name, tm, out_dtype):
    m, k = a.shape
    n = w.shape[0]

    def body(a_ref, w_ref, o_ref):
        o_ref[...] = _dot_nt(_bf(a_ref[...]), w_ref[...]).astype(out_dtype)

    return _call(
        body, name=name, grid=(m // tm, n // COL),
        in_specs=[pl.BlockSpec((tm, k), lambda i, j: (i, 0)),
                  pl.BlockSpec((COL, k), lambda i, j: (j, 0))],
        out_specs=pl.BlockSpec((tm, COL), lambda i, j: (i, j)),
        out_shape=jax.ShapeDtypeStruct((m, n), out_dtype),
    )(a, w)


def _piece_layout(pieces):
    offs, nblk, o = [], [], 0
    for p in pieces:
        offs.append(o)
        nblk.append(p.shape[1] // COL)
        o += p.shape[1] // COL
    return offs, nblk, o


def _mm_tn(a, pieces, *, name, tm, out_dtype=BF16):
    m, k = a.shape
    offs, nblk, nn = _piece_layout(pieces)

    def piece_spec(o, nb):
        def idx(ki, nj, mi):
            use = jnp.logical_and(nj >= o, nj < o + nb)
            return (jnp.where(use, mi, 0), jnp.clip(nj - o, 0, nb - 1))
        return pl.BlockSpec((tm, COL), idx)

    def body(a_ref, *rest):
        p_refs, o_ref, acc = rest[:len(pieces)], rest[len(pieces)], rest[len(pieces) + 1]
        nj, mi = pl.program_id(1), pl.program_id(2)

        @pl.when(mi == 0)
        def _():
            acc[...] = jnp.zeros_like(acc)

        for p_ref, o, nb in zip(p_refs, offs, nblk):
            @pl.when(jnp.logical_and(nj >= o, nj < o + nb))
            def _(p_ref=p_ref):
                acc[...] += _dot_tn(_bf(a_ref[...]), _bf(p_ref[...]))

        @pl.when(mi == pl.num_programs(2) - 1)
        def _():
            o_ref[...] = acc[...].astype(out_dtype)

    return _call(
        body, name=name, grid=(k // COL, nn, m // tm),
        in_specs=[pl.BlockSpec((tm, COL), lambda ki, nj, mi: (mi, ki))]
        + [piece_spec(o, nb) for o, nb in zip(offs, nblk)],
        out_specs=pl.BlockSpec((COL, COL), lambda ki, nj, mi: (ki, nj)),
        out_shape=jax.ShapeDtypeStruct((k, nn * COL), out_dtype),
        scratch=[pltpu.VMEM((COL, COL), F32)],
    )(a, *pieces)


def _mm_nt_normbwd(pieces, w, x, nw, dres, *, name, tm, row, comm=None):
    m, d = x.shape
    offs, nblk, nk = _piece_layout(pieces)

    def piece_spec(o, nb):
        return pl.BlockSpec((tm, COL), lambda i, k: (i, jnp.clip(k - o, 0, nb - 1)))

    def body(*refs):
        p_refs = refs[:len(pieces)]
        w_ref, x_ref, nw_ref, dres_ref, dx_ref, dnw_ref, acc = refs[len(pieces):]
        i, k = pl.program_id(0), pl.program_id(1)

        @pl.when(jnp.logical_and(i == 0, k == 0))
        def _():
            dnw_ref[...] = jnp.zeros_like(dnw_ref)

        @pl.when(k == 0)
        def _():
            acc[...] = jnp.zeros_like(acc)

        for p_ref, o, nb in zip(p_refs, offs, nblk):
            @pl.when(jnp.logical_and(k >= o, k < o + nb))
            def _(p_ref=p_ref):
                acc[...] += _dot_nt(_bf(p_ref[...]), w_ref[...])

        @pl.when(k == nk - 1)
        def _():
            dx, dnw = _rms_bwd(acc[...], x_ref[...], nw_ref[...])
            dx_ref[...] = dres_ref[...] + dx
            _row_acc(dnw_ref, row, dnw)

    return _call(
        body, name=name, grid=(m // tm, nk), comm=comm,
        in_specs=[piece_spec(o, nb) for o, nb in zip(offs, nblk)]
        + [pl.BlockSpec((d, COL), lambda i, k: (0, k)),
           pl.BlockSpec((tm, d), lambda i, k: (i, 0)),
           pl.BlockSpec((1, d), lambda i, k: (0, 0)),
           pl.BlockSpec((tm, d), lambda i, k: (i, 0))],
        out_specs=[pl.BlockSpec((tm, d), lambda i, k: (i, 0)),
                   pl.BlockSpec((8, d), lambda i, k: (0, 0))],
        out_shape=[jax.ShapeDtypeStruct((m, d), F32), jax.ShapeDtypeStruct((8, d), F32)],
        scratch=[pltpu.VMEM((tm, d), F32)],
    )(*pieces, w, x, nw, dres)


def _rope_tables(pos3, invf):
    b, s, _ = pos3.shape

    def body(pos_ref, invf_ref, cos_ref, sin_ref):
        half_t, half_d = s // 2, RET_DK // 2
        pos = pos_ref[...].astype(F32)
        low = lax.broadcasted_iota(jnp.int32, (half_t, RET_DK), 1) < half_d
        ang = jnp.where(low, pos[0:half_t], pos[half_t:]) * invf_ref[...]
        co, si = jnp.cos(ang), jnp.sin(ang)
        co_turned, si_turned = pltpu.roll(co, half_d, 1), pltpu.roll(si, half_d, 1)
        sign = jnp.where(low, -1.0, 1.0)
        cos_ref[...] = jnp.concatenate([jnp.where(low, co, co_turned), jnp.where(low, co_turned, co)], axis=0)
        sin_ref[...] = jnp.concatenate([sign * jnp.where(low, si, si_turned), sign * jnp.where(low, si_turned, si)],
                                       axis=0)

    spec = pl.BlockSpec((None, s, RET_DK), lambda i: (i, 0, 0))
    return _call(
        body, name="rope_tables", grid=(b,),
        in_specs=[pl.BlockSpec((None, s, 1), lambda i: (i, 0, 0)),
                  pl.BlockSpec((1, RET_DK), lambda i: (0, 0))],
        out_specs=[spec, spec],
        out_shape=[jax.ShapeDtypeStruct((b, s, RET_DK), F32)] * 2,
    )(pos3, invf)


def _log_gamma(h):
    return float(np.log1p(-np.power(np.float32(2.0), np.float32(-5.0 - h))).astype(np.float32))


def _decay_matrix(h):
    lg = _log_gamma(h)
    n = lax.broadcasted_iota(jnp.int32, (SEQ_T, SEQ_T), 0)
    m = lax.broadcasted_iota(jnp.int32, (SEQ_T, SEQ_T), 1)
    same = (n // REF_CHUNK) == (m // REF_CHUNK)
    dist = jnp.where(same, jnp.abs(n - m), n - m).astype(F32)
    return jnp.where(jnp.logical_or(same, m < n), jnp.exp(lg * dist), 0.0)


def _decay_vectors(h):
    lg = _log_gamma(h)
    idx = lax.broadcasted_iota(jnp.int32, (SEQ_T, 1), 0).astype(F32)
    qd = jnp.exp(lg * (idx + 1.0))
    kd = jnp.exp(lg * (SEQ_T - 1.0 - idx))
    return qd, kd, math.exp(lg * SEQ_T)


def _rotate(x, cos, sin):
    return x * cos + pltpu.roll(x, RET_DK // 2, 1) * sin


def _rotate_bwd(d, cos, sin):
    return d * cos + pltpu.roll(d * sin, RET_DK // 2, 1)


def _ret_head(p_ref, cos, sin, h):
    q = p_ref[:, h * RET_DK:(h + 1) * RET_DK].astype(F32)
    k = p_ref[:, 512 + h * RET_DK:512 + (h + 1) * RET_DK].astype(F32)
    v = p_ref[:, 1024 + h * RET_DV:1024 + (h + 1) * RET_DV]
    g = p_ref[:, 2048 + h * RET_DV:2048 + (h + 1) * RET_DV].astype(F32)
    qr = _rotate(q, cos, sin)
    kr = _rotate(k, cos, sin) * (RET_DK ** -0.5)
    return qr, kr, v, g


def _group_norm(o):
    mu = jnp.mean(o, axis=-1, keepdims=True)
    oc = o - mu
    rstd = lax.rsqrt(jnp.mean(oc * oc, axis=-1, keepdims=True) + GN_EPS)
    return oc * rstd, rstd


def _retention_fwd(proj3, cos, sin, gnw, comm=None):
    b, s, _ = proj3.shape
    nc = s // SEQ_T

    def body(p_ref, cos_ref, sin_ref, gnw_ref, a_ref, st_ref, state, wtab):
        c = pl.program_id(1)

        @pl.when(jnp.logical_and(pl.program_id(0) == 0, c == 0))
        def _():
            for h in range(RET_HEADS):
                wtab[h] = _decay_matrix(h)

        @pl.when(c == 0)
        def _():
            state[...] = jnp.zeros_like(state)

        cs, sn = cos_ref[...], sin_ref[...]
        st_ref[...] = state[...]
        outs, states = [], []
        for h in range(RET_HEADS):
            qd, kd, gt = _decay_vectors(h)
            qr, kr, v, g = _ret_head(p_ref, cs, sn, h)
            st = state[h]
            p = _dot_nt(_bf(qr), _bf(kr)) * wtab[h]
            o = _dot(_bf(p), _bf(v)) + _dot(_bf(qr * qd), _bf(st))
            states.append(st * gt + _dot_tn(_bf(kr * kd), _bf(v)))
            on, _ = _group_norm(o)
            gw = gnw_ref[:, h * RET_DV:(h + 1) * RET_DV]
            outs.append(_bf(on * gw * (g * _sigmoid(g))))
        a_ref[...] = jnp.concatenate(outs, axis=1)
        state[...] = jnp.stack(states)

    tab = pl.BlockSpec((None, SEQ_T, RET_DK), lambda i, c: (i, c, 0))
    return _call(
        body, name="retention_fwd", grid=(b, nc), comm=comm,
        in_specs=[pl.BlockSpec((None, SEQ_T, 3072), lambda i, c: (i, c, 0)), tab, tab,
                  pl.BlockSpec((1, D_MODEL), lambda i, c: (0, 0))],
        out_specs=[pl.BlockSpec((None, SEQ_T, D_MODEL), lambda i, c: (i, c, 0)),
                   pl.BlockSpec((None, None, RET_HEADS, RET_DK, RET_DV), lambda i, c: (i, c, 0, 0, 0))],
        out_shape=[jax.ShapeDtypeStruct((b, s, D_MODEL), BF16),
                   jax.ShapeDtypeStruct((b, nc, RET_HEADS, RET_DK, RET_DV), F32)],
        scratch=[pltpu.VMEM((RET_HEADS, RET_DK, RET_DV), F32),
                 pltpu.VMEM((RET_HEADS, SEQ_T, SEQ_T), F32)],
    )(proj3, cos, sin, gnw)


def _retention_bwd(proj3, cos, sin, gnw, states, da3, comm=None):
    b, s, _ = proj3.shape
    nc = s // SEQ_T

    def body(p_ref, cos_ref, sin_ref, gnw_ref, st_ref, da_ref, d_ref, dgn_ref, dstate, wtab):
        c = pl.program_id(1)

        @pl.when(jnp.logical_and(pl.program_id(0) == 0, c == 0))
        def _():
            dgn_ref[...] = jnp.zeros_like(dgn_ref)
            for h in range(RET_HEADS):
                wtab[h] = _decay_matrix(h)

        @pl.when(c == 0)
        def _():
            dstate[...] = jnp.zeros_like(dstate)

        cs, sn = cos_ref[...], sin_ref[...]
        dqs, dks, dvs, dgs, dgns, dstates = [], [], [], [], [], []
        for h in range(RET_HEADS):
            qd, kd, gt = _decay_vectors(h)
            qr, kr, v, g = _ret_head(p_ref, cs, sn, h)
            st, dst, w = st_ref[h], dstate[h], wtab[h]
            qb, kb, vb = _bf(qr), _bf(kr), _bf(v)
            p = _dot_nt(qb, kb) * w
            o = _dot(_bf(p), vb) + _dot(_bf(qr * qd), _bf(st))
            on, rstd = _group_norm(o)
            gw = gnw_ref[:, h * RET_DV:(h + 1) * RET_DV]
            da = da_ref[:, h * RET_DV:(h + 1) * RET_DV].astype(F32)
            sg = _sigmoid(g)
            silu = g * sg
            dg = da * on * gw * (sg * (1.0 + g * (1.0 - sg)))
            dgns.append(jnp.sum(da * silu * on, axis=0, keepdims=True))
            don = da * silu * gw
            do = rstd * (don - jnp.mean(don, axis=-1, keepdims=True)
                         - on * jnp.mean(don * on, axis=-1, keepdims=True))
            dob = _bf(do)
            dp = _dot_nt(dob, vb) * w
            dqr = _dot(_bf(dp), kb) + _dot_nt(dob, _bf(st)) * qd
            dkr = _dot_tn(_bf(dp), qb) + _dot_nt(vb, _bf(dst)) * kd
            dv = _dot_tn(_bf(p), dob) + _dot(_bf(kr * kd), _bf(dst))
            dstates.append(dst * gt + _dot_tn(_bf(qr * qd), dob))
            dqs.append(_bf(_rotate_bwd(dqr, cs, sn)))
            dks.append(_bf(_rotate_bwd(dkr, cs, sn) * (RET_DK ** -0.5)))
            dvs.append(_bf(dv))
            dgs.append(_bf(dg))
        d_ref[...] = jnp.concatenate(dqs + dks + dvs + dgs, axis=1)
        _row_acc(dgn_ref, ROW_GN, jnp.concatenate(dgns, axis=1))
        dstate[...] = jnp.stack(dstates)

    rev = lambda i, c: (i, nc - 1 - c, 0)
    tab = pl.BlockSpec((None, SEQ_T, RET_DK), rev)
    return _call(
        body, name="retention_bwd", grid=(b, nc), comm=comm,
        in_specs=[pl.BlockSpec((None, SEQ_T, 3072), rev), tab, tab,
                  pl.BlockSpec((1, D_MODEL), lambda i, c: (0, 0)),
                  pl.BlockSpec((None, None, RET_HEADS, RET_DK, RET_DV), lambda i, c: (i, nc - 1 - c, 0, 0, 0)),
                  pl.BlockSpec((None, SEQ_T, D_MODEL), rev)],
        out_specs=[pl.BlockSpec((None, SEQ_T, 3072), rev),
                   pl.BlockSpec((8, D_MODEL), lambda i, c: (0, 0))],
        out_shape=[jax.ShapeDtypeStruct((b, s, 3072), BF16), jax.ShapeDtypeStruct((8, D_MODEL), F32)],
        scratch=[pltpu.VMEM((RET_HEADS, RET_DK, RET_DV), F32),
                 pltpu.VMEM((RET_HEADS, SEQ_T, SEQ_T), F32)],
    )(proj3, cos, sin, gnw, states, da3)


def _softplus_neg(lam):
    z = -lam
    u = jnp.exp(-jnp.abs(z))
    log1p_u = jnp.where(u < 0.01, u * (1.0 - u * (0.5 - u * (1.0 / 3.0))), jnp.log(1.0 + u))
    return jnp.maximum(z, 0.0) + log1p_u


def _lru_coeffs(xc, wr_ref, br_ref, wi_ref, bi_ref, lam_ref):
    rs, is_ = [], []
    for n in range(LRU_BLOCKS):
        xb = _bf(xc[:, n * LRU_BLOCK:(n + 1) * LRU_BLOCK])
        cols = slice(n * LRU_BLOCK, (n + 1) * LRU_BLOCK)
        rs.append(_sigmoid(_dot(xb, wr_ref[n]) + br_ref[:, cols]))
        is_.append(_sigmoid(_dot(xb, wi_ref[n]) + bi_ref[:, cols]))
    r = jnp.concatenate(rs, axis=1)
    i = jnp.concatenate(is_, axis=1)
    sp = _softplus_neg(lam_ref[...])
    la = -LRU_C * r * sp
    a = jnp.exp(la)
    s = jnp.sqrt(-jnp.tanh(la) * (a * a + 1.0))
    return r, i, a, s, sp


_LRU_PARAM_SPECS = [
    pl.BlockSpec((4, D_MODEL), lambda i, c: (0, 0)),
    pl.BlockSpec((1, D_MODEL), lambda i, c: (0, 0)),
    pl.BlockSpec((LRU_BLOCKS, LRU_BLOCK, LRU_BLOCK), lambda i, c: (0, 0, 0)),
    pl.BlockSpec((1, D_MODEL), lambda i, c: (0, 0)),
    pl.BlockSpec((LRU_BLOCKS, LRU_BLOCK, LRU_BLOCK), lambda i, c: (0, 0, 0)),
    pl.BlockSpec((1, D_MODEL), lambda i, c: (0, 0)),
    pl.BlockSpec((1, D_MODEL), lambda i, c: (0, 0)),
]


def _lru_fwd(proj3, params, comm=None):
    b, s, _ = proj3.shape
    nc = s // SEQ_T

    def body(x_ref, y_ref, cw, cb, wr, br, wi, bi, lam,
             o_ref, h_ref, xc_ref, a_ref, s_ref, gy_ref, hdg_ref, r_ref, i_ref, xprev, hprev):
        @pl.when(pl.program_id(1) == 0)
        def _():
            xprev[...] = jnp.zeros_like(xprev)
            hprev[...] = jnp.zeros_like(hprev)

        x = x_ref[...].astype(F32)
        prev8 = xprev[...]
        xc = cb[...] + sum(cw[j:j + 1, :] * _shift_down(x, 3 - j, prev8) for j in range(4))
        xprev[...] = x[SEQ_T - 8:]
        xc_ref[...] = xc
        r, i, a, s_, _ = _lru_coeffs(xc, wr, br, wi, bi, lam)
        a_ref[...] = a
        s_ref[...] = s_
        r_ref[...] = _bf(r)
        i_ref[...] = _bf(i)
        h = _scan_fwd(a, s_ * (i * xc), hprev[7:8, :])
        hprev[...] = h[SEQ_T - 8:]
        h_ref[...] = h
        gy, dgy = _gelu_and_grad(y_ref[...].astype(F32))
        o_ref[...] = _bf(h * gy)
        gy_ref[...] = _bf(gy)
        hdg_ref[...] = _bf(h * dgy)

    out = pl.BlockSpec((None, SEQ_T, D_MODEL), lambda i, c: (i, c, 0))
    half, full = jax.ShapeDtypeStruct((b, s, D_MODEL), BF16), jax.ShapeDtypeStruct((b, s, D_MODEL), F32)
    return _call(
        body, name="lru_fwd", grid=(b, nc), comm=comm,
        in_specs=[pl.BlockSpec((None, SEQ_T, D_MODEL), lambda i, c: (i, c, 3)),
                  pl.BlockSpec((None, SEQ_T, D_MODEL), lambda i, c: (i, c, 4))] + _LRU_PARAM_SPECS,
        out_specs=[out] * 9, out_shape=[half, full, full, full, full, half, half, half, half],
        scratch=[pltpu.VMEM((8, D_MODEL), F32), pltpu.VMEM((8, D_MODEL), F32)],
    )(proj3, proj3, *params)


def _lru_bwd(proj3, params, kept, db3, comm=None):
    b, s, _ = proj3.shape
    nc = s // SEQ_T
    blk8 = SEQ_T // 8
    hseq = kept[0]

    def body(x_ref, h_ref, xc_ref, a_ref, s_ref, gy_ref, hdg_ref, r_ref, i_ref, hp_ref, db_ref,
             cw, cb, wr, br, wi, bi, lam, d_ref, dwr_ref, dwi_ref, sm_ref, gnext, anext, dxcnext):
        c = pl.program_id(1)
        first_chunk = c == nc - 1

        @pl.when(jnp.logical_and(pl.program_id(0) == 0, c == 0))
        def _():
            dwr_ref[...] = jnp.zeros_like(dwr_ref)
            dwi_ref[...] = jnp.zeros_like(dwi_ref)
            sm_ref[...] = jnp.zeros_like(sm_ref)

        @pl.when(c == 0)
        def _():
            gnext[...] = jnp.zeros_like(gnext)
            anext[...] = jnp.zeros_like(anext)
            dxcnext[...] = jnp.zeros_like(dxcnext)

        x, xc, h = x_ref[...].astype(F32), xc_ref[...], h_ref[...]
        hprev = hp_ref[...] * jnp.where(first_chunk, 0.0, 1.0)
        r, i, a, s_ = r_ref[...].astype(F32), i_ref[...].astype(F32), a_ref[...], s_ref[...]
        sp = _softplus_neg(lam[...])
        db = db_ref[...].astype(F32)
        dy = db * hdg_ref[...].astype(F32)
        a_up = _shift_up(a, 1, anext[...])
        g = _scan_bwd(a_up, db * gy_ref[...].astype(F32), gnext[0:1, :])
        gnext[...] = g[0:8]
        anext[...] = a[0:8]
        da = g * _shift_down(h, 1, hprev)
        ixc = i * xc
        dla = da * a - (g * ixc) * (a * a) / s_
        di = g * s_ * xc
        dxc = g * s_ * i
        dzr = dla * (-LRU_C * sp) * r * (1.0 - r)
        dzi = di * i * (1.0 - i)
        lam_v = lam[...]
        _row_acc(sm_ref, 7, jnp.sum(dla * (LRU_C * r), axis=0, keepdims=True) * _sigmoid(-lam_v))
        _row_acc(sm_ref, 5, jnp.sum(dzr, axis=0, keepdims=True))
        _row_acc(sm_ref, 6, jnp.sum(dzi, axis=0, keepdims=True))
        parts, dwr_parts, dwi_parts = [], [], []
        for n in range(LRU_BLOCKS):
            cols = slice(n * LRU_BLOCK, (n + 1) * LRU_BLOCK)
            xb, zr, zi = _bf(xc[:, cols]), _bf(dzr[:, cols]), _bf(dzi[:, cols])
            parts.append(dxc[:, cols] + _dot_nt(zr, wr[n]) + _dot_nt(zi, wi[n]))
            dwr_parts.append(_dot_tn(xb, zr))
            dwi_parts.append(_dot_tn(xb, zi))
        dwr_ref[...] += jnp.stack(dwr_parts)
        dwi_ref[...] += jnp.stack(dwi_parts)
        dxc = jnp.concatenate(parts, axis=1)
        _row_acc(sm_ref, 4, jnp.sum(dxc, axis=0, keepdims=True))
        nxt = dxcnext[...]
        dx = jnp.zeros_like(x)
        for j in range(4):
            ahead = _shift_up(dxc, 3 - j, nxt)
            dx = dx + cw[j:j + 1, :] * ahead
            _row_acc(sm_ref, j, jnp.sum(ahead * x, axis=0, keepdims=True))
        dxcnext[...] = dxc[0:8]
        d_ref[:, 0:D_MODEL] = _bf(dx)
        d_ref[:, D_MODEL:2 * D_MODEL] = _bf(dy)

    rev = lambda col: (lambda i, c: (i, nc - 1 - c, col))
    prev = lambda col: (lambda i, c: (i, jnp.maximum((nc - 1 - c) * blk8 - 1, 0), col))
    return _call(
        body, name="lru_bwd", grid=(b, nc), comm=comm,
        in_specs=[pl.BlockSpec((None, SEQ_T, D_MODEL), rev(3))]
        + [pl.BlockSpec((None, SEQ_T, D_MODEL), rev(0))] * len(kept)
        + [pl.BlockSpec((None, 8, D_MODEL), prev(0)), pl.BlockSpec((None, SEQ_T, D_MODEL), rev(0))]
        + _LRU_PARAM_SPECS,
        out_specs=[pl.BlockSpec((None, SEQ_T, 2 * D_MODEL), rev(0)),
                   pl.BlockSpec((LRU_BLOCKS, LRU_BLOCK, LRU_BLOCK), lambda i, c: (0, 0, 0)),
                   pl.BlockSpec((LRU_BLOCKS, LRU_BLOCK, LRU_BLOCK), lambda i, c: (0, 0, 0)),
                   pl.BlockSpec((8, D_MODEL), lambda i, c: (0, 0))],
        out_shape=[jax.ShapeDtypeStruct((b, s, 2 * D_MODEL), BF16),
                   jax.ShapeDtypeStruct((LRU_BLOCKS, LRU_BLOCK, LRU_BLOCK), F32),
                   jax.ShapeDtypeStruct((LRU_BLOCKS, LRU_BLOCK, LRU_BLOCK), F32),
                   jax.ShapeDtypeStruct((8, D_MODEL), F32)],
        scratch=[pltpu.VMEM((8, D_MODEL), F32)] * 3,
    )(proj3, *kept, hseq, db3, *params)


def _merge_parts(a_ref, b_ref, gr_ref, gl_ref, mgb_ref, wro_ref, wlo_ref):
    ya = _dot(a_ref[...], wro_ref[...])
    yb = _dot(b_ref[...], wlo_ref[...])
    sa = _sigmoid(gr_ref[...].astype(F32) + mgb_ref[0:1, :])
    sb = _sigmoid(gl_ref[...].astype(F32) + mgb_ref[1:2, :])
    return ya, yb, sa, sb


def _merge_specs(tm):
    row = lambda col: pl.BlockSpec((tm, D_MODEL), lambda i: (i, col))
    full = pl.BlockSpec((D_MODEL, D_MODEL), lambda i: (0, 0))
    return row, full


def _merge_fwd(a_in, b_in, proj, mgb, wro, wlo, wout, x, *, tm, comm=None):
    m = x.shape[0]
    row, full = _merge_specs(tm)

    def body(a_ref, b_ref, gr_ref, gl_ref, mgb_ref, wro_ref, wlo_ref, wout_ref, x_ref,
             o_ref, mix_ref, ya_ref, yb_ref):
        ya, yb, sa, sb = _merge_parts(a_ref, b_ref, gr_ref, gl_ref, mgb_ref, wro_ref, wlo_ref)
        mix = _bf(sa * ya + sb * yb)
        o_ref[...] = x_ref[...] + _dot(mix, wout_ref[...])
        mix_ref[...] = mix
        ya_ref[...] = _bf(ya)
        yb_ref[...] = _bf(yb)

    act = jax.ShapeDtypeStruct((m, D_MODEL), BF16)
    return _call(
        body, name="merge_fwd", grid=(m // tm,), comm=comm,
        in_specs=[row(0), row(0), row(5), row(6), pl.BlockSpec((2, D_MODEL), lambda i: (0, 0)),
                  full, full, full, row(0)],
        out_specs=[row(0)] * 4,
        out_shape=[jax.ShapeDtypeStruct((m, D_MODEL), F32), act, act, act],
    )(a_in, b_in, proj, proj, mgb, wro, wlo, wout, x)


def _merge_bwd(ya, yb, proj, mgb, wro, wlo, wout, dx2, *, tm):
    m = dx2.shape[0]
    row, full = _merge_specs(tm)

    def body(ya_ref, yb_ref, gr_ref, gl_ref, mgb_ref, wro_ref, wlo_ref, wout_ref, dx_ref,
             dya_ref, dyb_ref, da_ref, db_ref, dg_ref, sm_ref):
        @pl.when(pl.program_id(0) == 0)
        def _():
            sm_ref[...] = jnp.zeros_like(sm_ref)

        ya, yb = ya_ref[...].astype(F32), yb_ref[...].astype(F32)
        sa = _sigmoid(gr_ref[...].astype(F32) + mgb_ref[0:1, :])
        sb = _sigmoid(gl_ref[...].astype(F32) + mgb_ref[1:2, :])
        dmix = _dot_nt(_bf(dx_ref[...]), wout_ref[...])
        dya, dyb = _bf(dmix * sa), _bf(dmix * sb)
        dya_ref[...] = dya
        dyb_ref[...] = dyb
        dga = dmix * ya * sa * (1.0 - sa)
        dgb = dmix * yb * sb * (1.0 - sb)
        dg_ref[:, 0:D_MODEL] = _bf(dga)
        dg_ref[:, D_MODEL:2 * D_MODEL] = _bf(dgb)
        _row_acc(sm_ref, ROW_MGB, jnp.sum(dga, axis=0, keepdims=True))
        _row_acc(sm_ref, ROW_MGB + 1, jnp.sum(dgb, axis=0, keepdims=True))
        da_ref[...] = _bf(_dot_nt(dya, wro_ref[...]))
        db_ref[...] = _bf(_dot_nt(dyb, wlo_ref[...]))

    act = jax.ShapeDtypeStruct((m, D_MODEL), BF16)
    return _call(
        body, name="merge_bwd", grid=(m // tm,),
        in_specs=[row(0), row(0), row(5), row(6), pl.BlockSpec((2, D_MODEL), lambda i: (0, 0)),
                  full, full, full, row(0)],
        out_specs=[row(0)] * 4 + [pl.BlockSpec((tm, 2 * D_MODEL), lambda i: (i, 0)),
                                  pl.BlockSpec((8, D_MODEL), lambda i: (0, 0))],
        out_shape=[act] * 4 + [jax.ShapeDtypeStruct((m, 2 * D_MODEL), BF16),
                               jax.ShapeDtypeStruct((8, D_MODEL), F32)],
    )(ya, yb, proj, proj, mgb, wro, wlo, wout, dx2)


def _ffn_act_fwd(up3, cw, cb):
    b, s, _ = up3.shape

    def body(g_ref, v_ref, cw_ref, cb_ref, o_ref, act_ref, q_ref, gprev):
        @pl.when(pl.program_id(1) == 0)
        def _():
            gprev[...] = jnp.zeros_like(gprev)

        gate, val = g_ref[...].astype(F32), v_ref[...].astype(F32)
        prev8 = gprev[...]
        gc = cb_ref[...] + sum(cw_ref[j:j + 1, :] * _shift_down(gate, 2 - j, prev8) for j in range(3))
        gprev[...] = gate[SEQ_T - 8:]
        act, dact = _gelu_and_grad(gc)
        o_ref[...] = _bf(act * val)
        act_ref[...] = _bf(act)
        q_ref[...] = _bf(dact * val)

    out = pl.BlockSpec((None, SEQ_T, D_FF), lambda i, c: (i, c, 0))
    return _call(
        body, name="ffn_act_fwd", grid=(b, s // SEQ_T),
        in_specs=[pl.BlockSpec((None, SEQ_T, D_FF), lambda i, c: (i, c, 0)),
                  pl.BlockSpec((None, SEQ_T, D_FF), lambda i, c: (i, c, 1)),
                  pl.BlockSpec((3, D_FF), lambda i, c: (0, 0)),
                  pl.BlockSpec((1, D_FF), lambda i, c: (0, 0))],
        out_specs=[out] * 3,
        out_shape=[jax.ShapeDtypeStruct((b, s, D_FF), BF16)] * 3,
        scratch=[pltpu.VMEM((8, D_FF), F32)],
    )(up3, up3, cw, cb)


def _ffn_act_bwd(up3, act3, q3, cw, df3, comm=None):
    b, s, _ = up3.shape
    nc = s // SEQ_T

    def body(g_ref, act_ref, q_ref, df_ref, cw_ref, dg_ref, dv_ref, sm_ref, dgcnext):
        c = pl.program_id(1)

        @pl.when(jnp.logical_and(pl.program_id(0) == 0, c == 0))
        def _():
            sm_ref[...] = jnp.zeros_like(sm_ref)

        @pl.when(c == 0)
        def _():
            dgcnext[...] = jnp.zeros_like(dgcnext)

        gate = g_ref[...].astype(F32)
        df = df_ref[...].astype(F32)
        dv_ref[...] = _bf(df * act_ref[...].astype(F32))
        dgc = df * q_ref[...].astype(F32)
        nxt = dgcnext[...]
        dgate = jnp.zeros_like(gate)
        for j in range(3):
            ahead = _shift_up(dgc, 2 - j, nxt)
            dgate = dgate + cw_ref[j:j + 1, :] * ahead
            _row_acc(sm_ref, j, jnp.sum(ahead * gate, axis=0, keepdims=True))
        _row_acc(sm_ref, 3, jnp.sum(dgc, axis=0, keepdims=True))
        dgcnext[...] = dgc[0:8]
        dg_ref[...] = _bf(dgate)

    rev = pl.BlockSpec((None, SEQ_T, D_FF), lambda i, c: (i, nc - 1 - c, 0))
    return _call(
        body, name="ffn_act_bwd", grid=(b, nc), comm=comm,
        in_specs=[rev, rev, rev, rev, pl.BlockSpec((3, D_FF), lambda i, c: (0, 0))],
        out_specs=[rev, rev, pl.BlockSpec((8, D_FF), lambda i, c: (0, 0))],
        out_shape=[jax.ShapeDtypeStruct((b, s, D_FF), BF16)] * 2 + [jax.ShapeDtypeStruct((8, D_FF), F32)],
        scratch=[pltpu.VMEM((8, D_FF), F32)],
    )(up3, act3, q3, df3, cw)


def _ffn_down_loss(f, wd, x2, nfw, target, *, tm):
    m, kf = f.shape
    nt = m // tm

    def body(f_ref, wd_ref, x_ref, nw_ref, t_ref, dx_ref, dnw_ref, lsum):
        i = pl.program_id(0)

        @pl.when(i == 0)
        def _():
            dnw_ref[...] = jnp.zeros_like(dnw_ref)
            lsum[...] = jnp.zeros_like(lsum)

        x3 = x_ref[...] + _dot(f_ref[...], wd_ref[...])
        nw = nw_ref[...]
        xh, r = _rms(x3)
        err = xh * nw - t_ref[...]
        lsum[...] += jnp.sum(err * err, axis=0, keepdims=True)
        dy = err * (1.0 / D_MODEL)
        g = dy * nw
        dx_ref[...] = r * (g - xh * jnp.mean(g * xh, axis=-1, keepdims=True))
        _row_acc(dnw_ref, ROW_NF, jnp.sum(dy * xh, axis=0, keepdims=True))

        @pl.when(i == nt - 1)
        def _():
            loss = jnp.sum(lsum[...], axis=1, keepdims=True) * (0.5 / D_MODEL)
            dnw_ref[ROW_LOSS:ROW_LOSS + 1, :] = jnp.broadcast_to(loss, (1, D_MODEL))

    row = pl.BlockSpec((tm, D_MODEL), lambda i: (i, 0))
    return _call(
        body, name="ffn_down_loss", grid=(nt,),
        in_specs=[pl.BlockSpec((tm, kf), lambda i: (i, 0)),
                  pl.BlockSpec((kf, D_MODEL), lambda i: (0, 0)),
                  row, pl.BlockSpec((1, D_MODEL), lambda i: (0, 0)), row],
        out_specs=[row, pl.BlockSpec((8, D_MODEL), lambda i: (0, 0))],
        out_shape=[jax.ShapeDtypeStruct((m, D_MODEL), F32), jax.ShapeDtypeStruct((8, D_MODEL), F32)],
        scratch=[pltpu.VMEM((1, D_MODEL), F32)],
    )(f, wd, x2, nfw, target)


def _row_tile(rows):
    return next((t for t in (256, 128, 64, 32, 16, 8) if rows % t == 0), rows)


def _adamw(w, gs, m, v, *, name):
    rows, cols = w.shape
    tr = _row_tile(rows)
    ng = len(gs)

    def body(w_ref, *rest):
        g_refs, (m_ref, v_ref, g_out, d_out, m_out, v_out) = rest[:ng], rest[ng:]
        g = g_refs[0][...]
        for r in g_refs[1:]:
            g = g + r[...]
        mn = ADAM_B1 * m_ref[...] + (1.0 - ADAM_B1) * g
        vn = ADAM_B2 * v_ref[...] + (1.0 - ADAM_B2) * (g * g)
        m_hat = mn / (1.0 - ADAM_B1 ** ADAM_STEP)
        v_hat = vn / (1.0 - ADAM_B2 ** ADAM_STEP)
        g_out[...] = g
        d_out[...] = -ADAM_LR * (m_hat / (jnp.sqrt(v_hat) + ADAM_EPS) + ADAM_WD * w_ref[...])
        m_out[...] = mn
        v_out[...] = vn

    spec = pl.BlockSpec((tr, cols), lambda i: (i, 0))
    return _call(
        body, name=name, grid=(rows // tr,),
        in_specs=[spec] * (3 + ng), out_specs=[spec] * 4,
        out_shape=[jax.ShapeDtypeStruct((rows, cols), F32)] * 4,
    )(w, *gs, m, v)


def _mesh_pos():
    x, y, c = lax.axis_index("x"), lax.axis_index("y"), lax.axis_index("c")
    return x, y, c


def _other_chips(x, y, c):
    return [((1 - x, y, c), 2 * (1 - x) + y), ((x, 1 - y, c), 2 * x + 1 - y),
            ((1 - x, 1 - y, c), 2 * (1 - x) + 1 - y)]


def _region(ref, axis, size, half_axis, chip, core=None):
    idx = [slice(None)] * len(ref.shape)
    if core is None:
        idx[axis] = pl.ds(pl.multiple_of(chip * size, size), size)
    elif half_axis == axis:
        h = size // 2
        idx[axis] = pl.ds(pl.multiple_of(chip * size + core * h, h), h)
    else:
        idx[axis] = pl.ds(pl.multiple_of(chip * size, size), size)
        h = ref.shape[half_axis] // 2
        idx[half_axis] = pl.ds(pl.multiple_of(core * h, h), h)
    return ref.at[tuple(idx)]


def _half(ref, half_axis, core):
    idx = [slice(None)] * len(ref.shape)
    h = ref.shape[half_axis] // 2
    idx[half_axis] = pl.ds(pl.multiple_of(core * h, h), h)
    return ref.at[tuple(idx)]


class _Copy:
    def __init__(self, make):
        self._make = make

    def start(self):
        self._make().start()

    def wait(self):
        self._make().wait()

    def wait_send(self):
        self._make().wait_send()

    def wait_recv(self):
        self._make().wait_recv()


def _remote(src, dst, send_sem, recv_sem, dev):
    return _Copy(lambda: pltpu.make_async_remote_copy(
        src_ref=src, dst_ref=dst, send_sem=send_sem, recv_sem=recv_sem, device_id=dev, device_id_type=MESH))


def _local(src, dst, sem):
    return _Copy(lambda: pltpu.make_async_copy(src, dst, sem))


def _dma_sems(n):
    return pltpu.SemaphoreType.DMA((n,))


def _place_shard(w, chip, axis, *, name):
    shape = list(w.shape)
    shape[axis] *= N_CHIPS
    if w.ndim == 3:
        block, grid = (1,) + w.shape[1:], (w.shape[0],)
        in_map, out_map = (lambda i, chip: (i, 0, 0)), (lambda i, chip: (i, chip[0], 0))
    else:
        tr = _row_tile(w.shape[0])
        nt = w.shape[0] // tr
        block, grid = (tr, w.shape[1]), (nt,)
        in_map = lambda i, chip: (i, 0)
        out_map = (lambda i, chip: (chip[0] * nt + i, 0)) if axis == 0 else (lambda i, chip: (i, chip[0]))

    def body(chip_ref, w_ref, o_ref):
        o_ref[...] = _bf(w_ref[...])

    return _call(body, name=name, grid=grid, prefetch=1, in_specs=[pl.BlockSpec(block, in_map)],
                 out_specs=pl.BlockSpec(block, out_map),
                 out_shape=jax.ShapeDtypeStruct(tuple(shape), BF16))(chip, w)


def _ici_leg(srcs, dsts, layout, sizes, n_whole, sems):
    send_sems, recv_sems, local_sems = sems
    x, y, c = _mesh_pos()
    mine = 2 * x + y
    n_big = len(srcs) - n_whole
    local, sends, recvs = [], [], []
    for t, (src, dst) in enumerate(zip(srcs, dsts)):
        if t < n_big:
            ax, hx = layout[t]
            part = _region(src, ax, sizes[t], hx, mine, c)
            landing = lambda chip, dst=dst, ax=ax, hx=hx, size=sizes[t]: _region(dst, ax, size, hx, chip, c)
        else:
            part, landing = src, (lambda chip, dst=dst: dst.at[chip])
            local.append(_local(src, dst.at[mine], local_sems.at[t - n_big]))
        for k, (dev, chip) in enumerate(_other_chips(x, y, c)):
            sends.append(_remote(part, landing(mine), send_sems.at[3 * t + k], recv_sems.at[3 * t + k], dev))
            recvs.append(_remote(part, landing(chip), send_sems.at[3 * t + k], recv_sems.at[3 * t + k], dev))
    return local, sends, recvs


def _d2d_leg(srcs, dsts, layout, sizes, sems):
    send_sems, recv_sems = sems
    x, y, c = _mesh_pos()
    sends, recvs = [], []
    for t, (src, dst) in enumerate(zip(srcs, dsts)):
        ax, hx = layout[t]
        for k, (_, chip) in enumerate(_other_chips(x, y, c)):
            sem = (send_sems.at[3 * t + k], recv_sems.at[3 * t + k])
            sends.append(_remote(_region(src, ax, sizes[t], hx, chip, c),
                                 _region(dst, ax, sizes[t], hx, chip, c), *sem, (x, y, 1 - c)))
            recvs.append(_remote(_region(src, ax, sizes[t], hx, chip, 1 - c),
                                 _region(dst, ax, sizes[t], hx, chip, 1 - c), *sem, (x, y, 1 - c)))
    return sends, recvs


def _gather_shapes(bufs, whole):
    return ([jax.ShapeDtypeStruct(b.shape, b.dtype) for b in bufs]
            + [jax.ShapeDtypeStruct((N_CHIPS,) + w.shape, w.dtype) for w in whole])


def _gather_ici(bufs, layout):
    n = len(bufs)
    sizes = [b.shape[ax] // N_CHIPS for b, (ax, _) in zip(bufs, layout)]

    def start(ins, outs, sems):
        for cp in _ici_leg(ins, outs, layout, sizes, 0, (*sems, None))[1]:
            cp.start()

    def finish(ins, outs, sems):
        _, sends, recvs = _ici_leg(ins, outs, layout, sizes, 0, (*sems, None))
        for cp in recvs:
            cp.wait_recv()
        for cp in sends:
            cp.wait_send()

    return _Comm(bufs, _gather_shapes(bufs, ()), [_dma_sems(3 * n), _dma_sems(3 * n)], start, finish,
                 aliases={i: i for i in range(n)})


def _both(a, b):
    ni, no, ns = len(a.ins), len(a.outs), len(a.sems)

    def start(ins, outs, sems):
        a.start(ins[:ni], outs[:no], sems[:ns])
        b.start(ins[ni:], outs[no:], sems[ns:])

    def finish(ins, outs, sems):
        a.finish(ins[:ni], outs[:no], sems[:ns])
        b.finish(ins[ni:], outs[no:], sems[ns:])

    aliases = {**a.aliases, **{ni + i: no + o for i, o in b.aliases.items()}}
    return _Comm(a.ins + b.ins, a.outs + b.outs, a.sems + b.sems, start, finish, aliases)


def _gather_d2d(bufs, layout, sizes):
    n = len(bufs)

    def start(ins, outs, sems):
        for cp in _d2d_leg(ins, outs, layout, sizes, sems)[0]:
            cp.start()

    def finish(ins, outs, sems):
        sends, recvs = _d2d_leg(ins, outs, layout, sizes, sems)
        for cp in recvs:
            cp.wait_recv()
        for cp in sends:
            cp.wait_send()

    return _Comm(bufs, [jax.ShapeDtypeStruct(b.shape, b.dtype) for b in bufs],
                 [_dma_sems(3 * n), _dma_sems(3 * n)], start, finish, aliases={i: i for i in range(n)})


def _norm_bf16(x, nw, *, name, tm):
    m, d = x.shape

    def body(x_ref, nw_ref, h_ref):
        h_ref[...] = _bf(_rms(x_ref[...])[0] * nw_ref[...])

    row = pl.BlockSpec((tm, d), lambda i: (i, 0))
    return _call(body, name=name, grid=(m // tm,), in_specs=[row, pl.BlockSpec((1, d), lambda i: (0, 0))],
                 out_specs=row, out_shape=jax.ShapeDtypeStruct((m, d), BF16))(x, nw)


def _in_proj_gather(h1, w_buf, later, later_cut, small, order, *, tm):
    m, d = h1.shape
    width = w_buf.shape[1] // N_CHIPS
    nr, nl = m // tm, len(later)
    sizes = [b.shape[ax] // N_CHIPS for b, (ax, _) in zip(later, later_cut)]

    def body(order_ref, h_ref, w_in, *rest):
        later_in, small_in = rest[:nl], rest[nl]
        o_ref, w_out = rest[nl + 1], rest[nl + 2]
        later_out, small_out = rest[nl + 3:2 * nl + 3], rest[2 * nl + 3]
        wv, load_sem, ici_send, ici_recv, d2d_send, d2d_recv, l_send, l_recv, l_local = rest[2 * nl + 4:]
        s, i = pl.program_id(0), pl.program_id(1)
        x, y, c = _mesh_pos()
        mine = 2 * x + y
        peers = _other_chips(x, y, c)
        part = lambda ref, chip, core=None: _region(ref, 1, width, 0, chip, core)

        def ici(k):
            dev, chip = peers[k]
            sem = (ici_send.at[k], ici_recv.at[k])
            return (_remote(part(w_in, mine, c), part(w_out, mine, c), *sem, dev),
                    _remote(part(w_in, chip, c), part(w_out, chip, c), *sem, dev))

        def d2d(k):
            chip, sem, sib = peers[k][1], (d2d_send.at[k], d2d_recv.at[k]), (x, y, 1 - c)
            return (_remote(part(w_out, chip, c), part(w_out, chip, c), *sem, sib),
                    _remote(part(w_out, chip, 1 - c), part(w_out, chip, 1 - c), *sem, sib))

        def load(src, chip, slot):
            cp = _local(part(src, chip), wv.at[slot], load_sem.at[slot])
            cp.start()
            cp.wait()

        def others():
            return _ici_leg(list(later_in) + [small_in], list(later_out) + [small_out], later_cut, sizes, 1,
                            (l_send, l_recv, l_local))

        @pl.when(jnp.logical_and(s == 0, i == 0))
        def _():
            for k in range(3):
                ici(k)[0].start()
            local, sends, _ = others()
            for cp in local + sends:
                cp.start()
            load(w_in, mine, 0)

        o_ref[...] = _bf(_dot(h_ref[...], wv[s % 2]))

        @pl.when(i == nr - 1)
        def _():
            for k in range(3):
                @pl.when(s == k)
                def _(k=k):
                    ici(k)[1].wait_recv()
                    d2d(k)[0].start()
                    d2d(k)[1].wait_recv()
                    load(w_out, peers[k][1], (k + 1) % 2)

            @pl.when(s == 3)
            def _():
                for k in range(3):
                    ici(k)[0].wait_send()
                    d2d(k)[0].wait_send()
                local, sends, recvs = others()
                for cp in recvs:
                    cp.wait_recv()
                for cp in sends:
                    cp.wait_send()
                for cp in local:
                    cp.wait()

    any_spec = pl.BlockSpec(memory_space=pl.ANY)
    n_any = nl + 2
    outs = _call(
        body, name="in_proj", grid=(N_CHIPS, nr), prefetch=1,
        in_specs=[pl.BlockSpec((tm, d), lambda s, i, order: (i, 0))] + [any_spec] * n_any,
        out_specs=[pl.BlockSpec((tm, width), lambda s, i, order: (i, order[s]))] + [any_spec] * n_any,
        out_shape=[jax.ShapeDtypeStruct((m, w_buf.shape[1]), BF16)] + _gather_shapes([w_buf] + list(later), [small]),
        scratch=[pltpu.VMEM((2, d, width), BF16), _dma_sems(2), _dma_sems(3), _dma_sems(3), _dma_sems(3),
                 _dma_sems(3), _dma_sems(3 * (nl + 1)), _dma_sems(3 * (nl + 1)), _dma_sems(1)],
        aliases={2 + t: 1 + t for t in range(nl + 1)},
    )(order, h1, w_buf, *later, small)
    return outs[0], outs[1], list(outs[2:2 + nl]), outs[2 + nl]


def _exchange(grads, layout):
    n = len(grads)
    others = N_DEV - 1
    sizes = [g.shape[ax] // N_CHIPS for g, (ax, _) in zip(grads, layout)]
    out_shapes = []
    for g, (ax, hx), sz in zip(grads, layout, sizes):
        shape = list(g.shape)
        shape[ax] = sz
        shape[hx] //= 2
        out_shapes.append(jax.ShapeDtypeStruct((others,) + tuple(shape), g.dtype))

    def copies(ins, outs, sems):
        send_sems, recv_sems = sems
        x, y, c = _mesh_pos()
        sends, recvs = [], []
        for t, (src, dst) in enumerate(zip(ins, outs)):
            ax, hx = layout[t]
            for r in range(1, N_DEV):
                px = (1 - x) if r & 4 else x
                py = (1 - y) if r & 2 else y
                pc = (1 - c) if r & 1 else c
                sem = (send_sems.at[others * t + r - 1], recv_sems.at[others * t + r - 1])
                part = _region(src, ax, sizes[t], hx, 2 * px + py, pc)
                sends.append(_remote(part, dst.at[r - 1], *sem, (px, py, pc)))
                recvs.append(_remote(part, dst.at[r - 1], *sem, (px, py, pc)))
        return sends, recvs

    def start(ins, outs, sems):
        for cp in copies(ins, outs, sems)[0]:
            cp.start()

    def finish(ins, outs, sems):
        sends, recvs = copies(ins, outs, sems)
        for cp in recvs:
            cp.wait_recv()
        for cp in sends:
            cp.wait_send()

    return _Comm(grads, out_shapes, [_dma_sems(others * n), _dma_sems(others * n)], start, finish)


def _reduce_half(g, parts, pos, cut, *, name):
    ax, _ = cut
    others = parts.shape[0]
    if g.ndim == 3:
        nb, rows, cols = g.shape
        hb = nb // 2
        block, grid, out_shape = (1, rows // N_CHIPS, cols), (hb,), (nb, rows // N_CHIPS, cols)
        g_map = lambda i, pos: (pos[1] * hb + i, pos[0], 0)
        o_map = lambda i, pos: (pos[1] * hb + i, 0, 0)
        p_map = lambda i, pos: (0, i, 0, 0)
    elif ax == 1:
        rows, cols = g.shape
        tr = _row_tile(rows // 2)
        nt = rows // 2 // tr
        block, grid, out_shape = (tr, cols // N_CHIPS), (nt,), (rows, cols // N_CHIPS)
        g_map = lambda i, pos: (pos[1] * nt + i, pos[0])
        o_map = lambda i, pos: (pos[1] * nt + i, 0)
        p_map = lambda i, pos: (0, i, 0)
    else:
        rows, cols = g.shape
        tr = _row_tile(rows // N_CHIPS // 2)
        nt = rows // N_CHIPS // 2 // tr
        block, grid, out_shape = (tr, cols), (nt,), (rows // N_CHIPS, cols)
        g_map = lambda i, pos: (pos[0] * 2 * nt + pos[1] * nt + i, 0)
        o_map = lambda i, pos: (pos[1] * nt + i, 0)
        p_map = lambda i, pos: (0, i, 0)

    def body(pos_ref, g_ref, p_ref, o_ref):
        acc = g_ref[...].astype(F32)
        for r in range(others):
            acc = acc + p_ref[r].astype(F32)
        o_ref[...] = acc

    return _call(
        body, name=name, grid=grid, prefetch=1,
        in_specs=[pl.BlockSpec(block, g_map), pl.BlockSpec((others,) + block, p_map)],
        out_specs=pl.BlockSpec(block, o_map), out_shape=jax.ShapeDtypeStruct(out_shape, F32),
    )(pos, g, parts)


def _join_halves(bufs):
    return _call(None, name="join_halves", comm=_join_comm(bufs))()[1]


def _join_comm(bufs):
    n = len(bufs)

    def copies(ins, outs, sems):
        send_sems, recv_sems = sems
        x, y, c = _mesh_pos()
        sends = [_remote(_half(src, 0, c), _half(dst, 0, c), send_sems.at[t], recv_sems.at[t], (x, y, 1 - c))
                 for t, (src, dst) in enumerate(zip(ins, outs))]
        recvs = [_remote(_half(src, 0, 1 - c), _half(dst, 0, 1 - c), send_sems.at[t], recv_sems.at[t],
                         (x, y, 1 - c)) for t, (src, dst) in enumerate(zip(ins, outs))]
        return sends, recvs

    def start(ins, outs, sems):
        for cp in copies(ins, outs, sems)[0]:
            cp.start()

    def finish(ins, outs, sems):
        sends, recvs = copies(ins, outs, sems)
        for cp in recvs:
            cp.wait_recv()
        for cp in sends:
            cp.wait_send()

    return _Comm(bufs, [jax.ShapeDtypeStruct(b.shape, b.dtype) for b in bufs],
                 [_dma_sems(n), _dma_sems(n)], start, finish, aliases={i: i for i in range(n)})


def _allreduce_small(pack):
    rows, cols = pack.shape

    def body(p_ref, o_ref, slots, send_sems, recv_sems):
        x, y, c = _mesh_pos()
        me = 4 * x + 2 * y + c
        slots[me] = p_ref[...]
        copies = []
        for r in range(1, N_DEV):
            fx, fy, fc = (r >> 2) & 1, (r >> 1) & 1, r & 1
            dev = ((1 - x) if fx else x, (1 - y) if fy else y, (1 - c) if fc else c)
            cp = pltpu.make_async_remote_copy(
                src_ref=p_ref, dst_ref=slots.at[me], send_sem=send_sems.at[r - 1],
                recv_sem=recv_sems.at[r - 1], device_id=dev, device_id_type=MESH)
            cp.start()
            copies.append(cp)
        for cp in copies:
            cp.wait_recv()
        for cp in copies:
            cp.wait_send()
        acc = slots[0]
        for d in range(1, N_DEV):
            acc = acc + slots[d]
        o_ref[...] = acc

    vmem = pl.BlockSpec(memory_space=pltpu.VMEM)
    return _call(
        body, name="allreduce_small", in_specs=[vmem], out_specs=vmem,
        out_shape=jax.ShapeDtypeStruct((rows, cols), F32),
        scratch=[pltpu.VMEM((N_DEV, rows, cols), F32), pltpu.SemaphoreType.DMA((N_DEV - 1,)),
                 pltpu.SemaphoreType.DMA((N_DEV - 1,))],
    )(pack)


def _pad_rows(a, rows=8):
    return jnp.pad(a, ((0, rows - a.shape[0]), (0, 0)))


def kernel(x, positions, norm1_w, w_in, merge_gate_b, ret_gn_w, w_ret_o, lru_conv_w, lru_conv_b, lru_w_r, lru_b_r, lru_w_i, lru_b_i, lru_lambda, w_lru_o, w_out, norm2_w, ffn_w_up, ffn_conv_w, ffn_conv_b, ffn_w_down, norm_f_w, loss_target, m_norm1_w, m_w_in, m_merge_gate_b, m_ret_gn_w, m_w_ret_o, m_lru_conv_w, m_lru_conv_b, m_lru_w_r, m_lru_b_r, m_lru_w_i, m_lru_b_i, m_lru_lambda, m_w_lru_o, m_w_out, m_norm2_w, m_ffn_w_up, m_ffn_conv_w, m_ffn_conv_b, m_ffn_w_down, m_norm_f_w, v_norm1_w, v_w_in, v_merge_gate_b, v_ret_gn_w, v_w_ret_o, v_lru_conv_w, v_lru_conv_b, v_lru_w_r, v_lru_b_r, v_lru_w_i, v_lru_b_i, v_lru_lambda, v_w_lru_o, v_w_out, v_norm2_w, v_ffn_w_up, v_ffn_conv_w, v_ffn_conv_b, v_ffn_w_down, v_norm_f_w):
    names = ["norm1_w", "w_in", "merge_gate_b", "ret_gn_w", "w_ret_o", "lru_conv_w", "lru_conv_b", "lru_w_r",
             "lru_b_r", "lru_w_i", "lru_b_i", "lru_lambda", "w_lru_o", "w_out", "norm2_w", "ffn_w_up",
             "ffn_conv_w", "ffn_conv_b", "ffn_w_down", "norm_f_w"]
    w_args = dict(zip(names, [norm1_w, w_in, merge_gate_b, ret_gn_w, w_ret_o, lru_conv_w, lru_conv_b, lru_w_r,
                              lru_b_r, lru_w_i, lru_b_i, lru_lambda, w_lru_o, w_out, norm2_w, ffn_w_up,
                              ffn_conv_w, ffn_conv_b, ffn_w_down, norm_f_w]))
    m_args = dict(zip(names, [m_norm1_w, m_w_in, m_merge_gate_b, m_ret_gn_w, m_w_ret_o, m_lru_conv_w,
                              m_lru_conv_b, m_lru_w_r, m_lru_b_r, m_lru_w_i, m_lru_b_i, m_lru_lambda, m_w_lru_o,
                              m_w_out, m_norm2_w, m_ffn_w_up, m_ffn_conv_w, m_ffn_conv_b, m_ffn_w_down,
                              m_norm_f_w]))
    v_args = dict(zip(names, [v_norm1_w, v_w_in, v_merge_gate_b, v_ret_gn_w, v_w_ret_o, v_lru_conv_w,
                              v_lru_conv_b, v_lru_w_r, v_lru_b_r, v_lru_w_i, v_lru_b_i, v_lru_lambda, v_w_lru_o,
                              v_w_out, v_norm2_w, v_ffn_w_up, v_ffn_conv_w, v_ffn_conv_b, v_ffn_w_down,
                              v_norm_f_w]))

    bsz, seq, d = x.shape
    m = bsz * seq
    tm = min(MM_ROWS, m)
    tm_fused = min(FUSED_ROWS, m)
    tm_tall = min(TALL_ROWS, m)
    chip = 2 * lax.axis_index("x") + lax.axis_index("y")

    big = ["w_in", "w_ret_o", "w_lru_o", "w_out", "lru_w_r", "lru_w_i", "ffn_w_up", "ffn_w_down"]
    cut = dict(w_in=(1, 0), w_ret_o=(0, 0), w_lru_o=(0, 0), w_out=(0, 0), lru_w_r=(1, 0), lru_w_i=(1, 0),
               ffn_w_up=(1, 0), ffn_w_down=(0, 0))
    core = lax.axis_index("c")
    chip1 = jnp.reshape(chip, (1,)).astype(jnp.int32)
    pos = jnp.stack([chip, core]).astype(jnp.int32)
    placed = {n: _place_shard(w_args[n][0], chip1, cut[n][0], name="place_" + n) for n in big}
    small_pack = jnp.concatenate([
        jnp.pad(merge_gate_b[0], ((0, 6), (0, 512))),
        jnp.pad(lru_conv_w[0], ((0, 4), (0, 512))),
        jnp.pad(lru_b_r[0], ((0, 4), (0, 704))),
        jnp.pad(lru_b_i[0], ((0, 4), (0, 704))),
        jnp.pad(ffn_conv_w[0], ((0, 5), (0, 0))),
    ], axis=0)
    x2d = x.reshape(m, d)
    mx, my = lax.axis_index("x"), lax.axis_index("y")
    order = jnp.stack([chip, 2 * (1 - mx) + my, 2 * mx + 1 - my, 2 * (1 - mx) + 1 - my]).astype(jnp.int32)
    mixer = ["w_ret_o", "w_lru_o", "w_out", "lru_w_r", "lru_w_i"]
    cuts = lambda ns: [cut[n] for n in ns]
    sizes = lambda ns: [w_args[n].shape[1 + cut[n][0]] for n in ns]
    h1 = _norm_bf16(x2d, norm1_w, name="norm1", tm=tm)
    proj, w_in_full, bufs, sp = _in_proj_gather(h1, placed["w_in"], [placed[n] for n in mixer], cuts(mixer),
                                               small_pack, order, tm=tm_tall)
    wb = {"w_in": w_in_full}
    mgb = jnp.transpose(sp[:, 0:2, 0:256], (1, 0, 2)).reshape(2, D_MODEL)
    lcw = jnp.transpose(sp[:, 8:12, 0:256], (1, 0, 2)).reshape(4, D_MODEL)
    lbr = jnp.transpose(sp[:, 16:20, 0:64], (1, 0, 2)).reshape(1, D_MODEL)
    lbi = jnp.transpose(sp[:, 24:28, 0:64], (1, 0, 2)).reshape(1, D_MODEL)
    fcw = jnp.transpose(sp[:, 32:35, :], (1, 0, 2)).reshape(3, D_FF)
    nfw = norm_f_w.reshape(1, D_MODEL)

    half = RET_DK // 2
    inv_freq = ROPE_BASE ** (-jnp.arange(half, dtype=F32) / half)
    cos, sin = _rope_tables(positions.reshape(bsz, seq, 1), jnp.concatenate([inv_freq, inv_freq]).reshape(1, RET_DK))
    proj3 = proj.reshape(bsz, seq, D_IN)
    down, up_w = ["ffn_w_down"], ["ffn_w_up"]
    (a_in3, states), bufs = _retention_fwd(
        proj3, cos, sin, ret_gn_w,
        comm=_both(_gather_d2d(bufs, cuts(mixer), sizes(mixer)), _gather_ici([placed["ffn_w_down"]], cuts(down))))
    wb.update(zip(mixer, bufs[:len(mixer)]))
    lru_params = (lcw, lru_conv_b, wb["lru_w_r"], lbr, wb["lru_w_i"], lbi, lru_lambda)
    (b_in3, *lru_kept), (up_buf, wb["ffn_w_down"]) = _lru_fwd(
        proj3, lru_params,
        comm=_both(_gather_ici([placed["ffn_w_up"]], cuts(up_w)), _gather_d2d(bufs[len(mixer):], cuts(down), sizes(down))))
    a_in, b_in = a_in3.reshape(m, d), b_in3.reshape(m, d)
    (x2, mix, ya, yb), (wb["ffn_w_up"],) = _merge_fwd(
        a_in, b_in, proj, mgb, wb["w_ret_o"], wb["w_lru_o"], wb["w_out"], x2d, tm=tm_fused,
        comm=_gather_d2d([up_buf], cuts(up_w), sizes(up_w)))
    up, h2 = _norm_matmul(x2, norm2_w, wb["ffn_w_up"], name="ffn_up", tm=tm_tall, tn=TALL_COLS)
    up3 = up.reshape(bsz, seq, 2 * D_FF)
    f3, act3, q3 = _ffn_act_fwd(up3, fcw, ffn_conv_b)
    f = f3.reshape(m, D_FF)
    dx3, sm_nf = _ffn_down_loss(f, wb["ffn_w_down"], x2, nfw, loss_target.reshape(m, d), tm=tm)

    def send(*ns):
        return _exchange([g_full[n] for n in ns], [cut[n] for n in ns])

    g_full, parts = {}, {}
    df = _mm_nt(dx3, wb["ffn_w_down"], name="ffn_down_dx", tm=tm_tall, out_dtype=BF16)
    g_full["ffn_w_down"] = _mm_tn(f, [dx3], name="ffn_down_dw", tm=tm_tall)
    (dgate3, dval3, sm_ffn), (parts["ffn_w_down"],) = _ffn_act_bwd(
        up3, act3, q3, fcw, df.reshape(bsz, seq, D_FF), comm=send("ffn_w_down"))
    dup = [dgate3.reshape(m, D_FF), dval3.reshape(m, D_FF)]
    g_full["ffn_w_up"] = _mm_tn(h2, dup, name="ffn_up_dw", tm=tm_tall)
    (dx2, sm_n2), (parts["ffn_w_up"],) = _mm_nt_normbwd(
        dup, wb["ffn_w_up"], x2, norm2_w, dx3, name="ffn_up_dx", tm=tm, row=ROW_N2, comm=send("ffn_w_up"))
    dya, dyb, da_in, db_in, dgates, sm_mg = _merge_bwd(
        ya, yb, proj, mgb, wb["w_ret_o"], wb["w_lru_o"], wb["w_out"], dx2, tm=tm_fused)
    g_full["w_out"] = _mm_tn(mix, [dx2], name="out_dw", tm=tm_tall)
    g_full["w_ret_o"] = _mm_tn(a_in, [dya], name="ret_o_dw", tm=tm_tall)
    g_full["w_lru_o"] = _mm_tn(b_in, [dyb], name="lru_o_dw", tm=tm_tall)
    (dlru3, dwr, dwi, sm_lru), (parts["w_out"], parts["w_ret_o"], parts["w_lru_o"]) = _lru_bwd(
        proj3, lru_params, lru_kept, db_in.reshape(bsz, seq, d), comm=send("w_out", "w_ret_o", "w_lru_o"))
    g_full["lru_w_r"], g_full["lru_w_i"] = dwr.astype(BF16), dwi.astype(BF16)
    (dret3, sm_gn), (parts["lru_w_r"], parts["lru_w_i"]) = _retention_bwd(
        proj3, cos, sin, ret_gn_w, states, da_in.reshape(bsz, seq, d), comm=send("lru_w_r", "lru_w_i"))
    dproj = [dret3.reshape(m, 3072), dlru3.reshape(m, 2048), dgates]
    g_full["w_in"] = _mm_tn(h1, dproj, name="in_proj_dw", tm=tm_tall)
    half_sum = lambda n: _reduce_half(g_full[n], parts[n], pos, cut[n], name="sum_" + n)
    (grad_x, sm_n1), (parts["w_in"], *joined) = _mm_nt_normbwd(
        dproj, wb["w_in"], x2d, norm1_w, dx2, name="in_proj_dx", tm=tm, row=ROW_N1,
        comm=_both(send("w_in"), _join_comm([half_sum(n) for n in big[1:]])))
    reduced = _join_halves([half_sum("w_in")]) + joined
    misc = sm_n1 + sm_mg + sm_gn + sm_n2 + sm_nf
    pack = jnp.concatenate(
        [misc, sm_lru, sm_ffn[:, 0:1024], sm_ffn[:, 1024:2048], sm_ffn[:, 2048:3072]], axis=0)
    tot = _allreduce_small(pack)
    ffn_sm = jnp.concatenate([tot[16:24], tot[24:32], tot[32:40]], axis=1)
    g_small = {
        "norm1_w": tot[ROW_N1:ROW_N1 + 1], "merge_gate_b": tot[ROW_MGB:ROW_MGB + 2],
        "ret_gn_w": tot[ROW_GN:ROW_GN + 1], "norm2_w": tot[ROW_N2:ROW_N2 + 1], "norm_f_w": tot[ROW_NF:ROW_NF + 1],
        "lru_conv_w": tot[8:12], "lru_conv_b": tot[12:13], "lru_b_r": tot[13:14].reshape(4, 256),
        "lru_b_i": tot[14:15].reshape(4, 256), "lru_lambda": tot[15:16],
        "ffn_conv_w": ffn_sm[0:3], "ffn_conv_b": ffn_sm[3:4],
    }
    small_shard = dict(merge_gate_b=256, lru_conv_w=256, lru_b_r=64, lru_b_i=64, ffn_conv_w=768)

    outs = {}
    for n, g in zip(big, reduced):
        shape = w_args[n].shape
        g = g.reshape(-1, g.shape[-1])
        outs[n] = [o.reshape(shape) for o in _adamw(
            w_args[n].reshape(g.shape), [g], m_args[n].reshape(g.shape), v_args[n].reshape(g.shape),
            name="adamw_" + n)]
    for n, g in g_small.items():
        shape = w_args[n].shape
        if n in small_shard:
            g = lax.dynamic_slice_in_dim(g, chip * small_shard[n], small_shard[n], axis=1)
        w2 = w_args[n].reshape(g.shape)
        outs[n] = [o.reshape(shape) for o in _adamw(
            w2, [g], m_args[n].reshape(g.shape), v_args[n].reshape(g.shape), name="adamw_" + n)]

    result = [tot[ROW_LOSS, 0], grad_x.reshape(bsz, seq, d)]
    for k in range(4):
        result += [outs[n][k] for n in names]
    return tuple(result)
```

```python
import functools
import math

import numpy as np
import jax
import jax.numpy as jnp
from jax import lax
from jax.experimental import pallas as pl
from jax.experimental.pallas import tpu as pltpu

F32 = jnp.float32
BF16 = jnp.bfloat16

D_MODEL = 1024
RET_HEADS = 4
RET_DK = 128
RET_DV = 256
LRU_BLOCKS = 4
LRU_BLOCK = 256
LRU_C = 8.0
D_FF = 3072
D_IN = 7168
ROPE_BASE = 10000.0
RMS_EPS = 1e-6
GN_EPS = 1e-6
ADAM_LR, ADAM_B1, ADAM_B2, ADAM_EPS, ADAM_WD, ADAM_STEP = 0.001, 0.9, 0.999, 1e-08, 0.01, 10

N_CHIPS = 4
N_DEV = 8
SEQ_T = 256
REF_CHUNK = 64
COL = 1024
MM_ROWS = 1024
TALL_ROWS, TALL_COLS = 2048, 1024
FUSED_ROWS = 512
VMEM_LIMIT_BYTES = 56 * 1024 * 1024
MESH = pl.DeviceIdType.MESH
ROW_N1, ROW_MGB, ROW_GN, ROW_N2, ROW_NF, ROW_LOSS = 0, 1, 3, 4, 5, 6
GELU_K = math.sqrt(2.0 / math.pi)
GELU_C = 0.044715


class _Comm:
    def __init__(self, ins, outs, sems, start, finish, aliases=None):
        self.ins, self.outs, self.sems = list(ins), list(outs), list(sems)
        self.start, self.finish, self.aliases = start, finish, dict(aliases or {})


def _call(body, *, name, out_shape=(), grid=None, in_specs=(), out_specs=(), scratch=(), comm=None, prefetch=0,
          aliases=None):
    single = not isinstance(out_shape, (list, tuple))
    out_shape = [out_shape] if single else list(out_shape)
    out_specs = [out_specs] if single else list(out_specs)
    in_specs, scratch = list(in_specs), list(scratch)
    n_in, n_out, n_scr = len(in_specs), len(out_shape), len(scratch)
    kwargs = dict(name=name, compiler_params=pltpu.CompilerParams(vmem_limit_bytes=VMEM_LIMIT_BYTES))
    if prefetch:
        assert comm is None
        spec = pltpu.PrefetchScalarGridSpec(num_scalar_prefetch=prefetch, grid=grid, in_specs=in_specs,
                                            out_specs=out_specs, scratch_shapes=scratch)
        fn = pl.pallas_call(body, out_shape=out_shape, grid_spec=spec, input_output_aliases=dict(aliases or {}),
                            **kwargs)
        return (lambda *args: fn(*args)[0]) if single else fn
    if grid is not None:
        kwargs["grid"] = grid
    if comm is None:
        fn = pl.pallas_call(body, out_shape=out_shape, in_specs=in_specs, out_specs=out_specs,
                            scratch_shapes=scratch, **kwargs)
        return (lambda *args: fn(*args)[0]) if single else fn

    any_spec = pl.BlockSpec(memory_space=pl.ANY)
    n_cin, n_cout = len(comm.ins), len(comm.outs)

    def wrapped(*refs):
        ins, refs = refs[:n_in], refs[n_in:]
        cins, refs = refs[:n_cin], refs[n_cin:]
        outs, refs = refs[:n_out], refs[n_out:]
        couts, refs = refs[:n_cout], refs[n_cout:]
        scr, csems = refs[:n_scr], refs[n_scr:]
        if grid is None:
            comm.start(cins, couts, csems)
            comm.finish(cins, couts, csems)
            return
        ids = [pl.program_id(a) for a in range(len(grid))]
        first = functools.reduce(jnp.logical_and, [i == 0 for i in ids])
        last = functools.reduce(jnp.logical_and, [i == g - 1 for i, g in zip(ids, grid)])
        pl.when(first)(lambda: comm.start(cins, couts, csems))
        body(*ins, *outs, *scr)
        pl.when(last)(lambda: comm.finish(cins, couts, csems))

    fn = pl.pallas_call(
        wrapped, out_shape=out_shape + comm.outs, in_specs=in_specs + [any_spec] * n_cin,
        out_specs=out_specs + [any_spec] * n_cout, scratch_shapes=scratch + comm.sems,
        input_output_aliases={n_in + i: n_out + o for i, o in comm.aliases.items()}, **kwargs)

    def run(*args):
        res = fn(*args, *comm.ins)
        own = res[0] if single else list(res[:n_out])
        return own, list(res[n_out:])

    return run


def _dot(a, b):
    return jnp.dot(a, b, preferred_element_type=F32)


def _dot_nt(a, b):
    return lax.dot_general(a, b, (((1,), (1,)), ((), ())), preferred_element_type=F32)


def _dot_tn(a, b):
    return lax.dot_general(a, b, (((0,), (0,)), ((), ())), preferred_element_type=F32)


def _bf(x):
    return x.astype(BF16)


def _sigmoid(x):
    return 1.0 / (1.0 + jnp.exp(-x))


def _gelu_and_grad(x):
    x2 = x * x
    s = _sigmoid(x * (2.0 * GELU_K * GELU_C * x2 + 2.0 * GELU_K))
    g = x * s
    dg = s + g * (1.0 - s) * (6.0 * GELU_K * GELU_C * x2 + 2.0 * GELU_K)
    return g, dg


def _rms(x):
    r = lax.rsqrt(jnp.mean(x * x, axis=-1, keepdims=True) + RMS_EPS)
    return x * r, r


def _rms_bwd(dy, x, nw):
    xh, r = _rms(x)
    g = dy * nw
    dx = r * (g - xh * jnp.mean(g * xh, axis=-1, keepdims=True))
    return dx, jnp.sum(dy * xh, axis=0, keepdims=True)


def _row_acc(ref, row, val):
    ref[row:row + 1, :] = ref[row:row + 1, :] + val


def _shift_down(x, j, prev8):
    if j == 0:
        return x
    n = x.shape[0] // 8
    row = lax.broadcasted_iota(jnp.int32, prev8.shape, 0)
    turned = [pltpu.roll(prev8, j, 0)] + [pltpu.roll(x[8 * k:8 * k + 8], j, 0) for k in range(n)]
    return jnp.concatenate([jnp.where(row < j, turned[k], turned[k + 1]) for k in range(n)], axis=0)


def _shift_up(x, j, next8):
    if j == 0:
        return x
    n = x.shape[0] // 8
    row = lax.broadcasted_iota(jnp.int32, next8.shape, 0)
    turned = [pltpu.roll(x[8 * k:8 * k + 8], 8 - j, 0) for k in range(n)] + [pltpu.roll(next8, 8 - j, 0)]
    return jnp.concatenate([jnp.where(row >= 8 - j, turned[k + 1], turned[k]) for k in range(n)], axis=0)


def _scan_fwd(a, b, carry):
    row = lax.broadcasted_iota(jnp.int32, (8, a.shape[1]), 0)
    out = []
    for k in range(a.shape[0] // 8):
        ak, bk = a[8 * k:8 * k + 8], b[8 * k:8 * k + 8]
        for s in (1, 2, 4):
            keep = row >= s
            ar, br = pltpu.roll(ak, s, 0), pltpu.roll(bk, s, 0)
            bk = jnp.where(keep, ak * br + bk, bk)
            ak = jnp.where(keep, ak * ar, ak)
        hk = ak * carry + bk
        carry = hk[7:8]
        out.append(hk)
    return jnp.concatenate(out, axis=0)


def _scan_bwd(a, b, carry):
    row = lax.broadcasted_iota(jnp.int32, (8, a.shape[1]), 0)
    out = []
    for k in reversed(range(a.shape[0] // 8)):
        ak, bk = a[8 * k:8 * k + 8], b[8 * k:8 * k + 8]
        for s in (1, 2, 4):
            keep = row < 8 - s
            ar, br = pltpu.roll(ak, 8 - s, 0), pltpu.roll(bk, 8 - s, 0)
            bk = jnp.where(keep, ak * br + bk, bk)
            ak = jnp.where(keep, ak * ar, ak)
        gk = bk + ak * carry
        carry = gk[0:1]
        out.append(gk)
    return jnp.concatenate(out[::-1], axis=0)


def _norm_matmul(x, nw, w, *, name, tm, tn):
    m, d = x.shape
    n = w.shape[1]

    def body(x_ref, nw_ref, w_ref, o_ref, h_ref, h_sc):
        @pl.when(pl.program_id(1) == 0)
        def _():
            xh, _ = _rms(x_ref[...])
            h = _bf(xh * nw_ref[...])
            h_sc[...] = h
            h_ref[...] = h

        o_ref[...] = _bf(_dot(h_sc[...], w_ref[...]))

    return _call(
        body, name=name, grid=(m // tm, n // tn),
        in_specs=[pl.BlockSpec((tm, d), lambda i, j: (i, 0)),
                  pl.BlockSpec((1, d), lambda i, j: (0, 0)),
                  pl.BlockSpec((d, tn), lambda i, j: (0, j))],
        out_specs=[pl.BlockSpec((tm, tn), lambda i, j: (i, j)),
                   pl.BlockSpec((tm, d), lambda i, j: (i, 0))],
        out_shape=[jax.ShapeDtypeStruct((m, n), BF16), jax.ShapeDtypeStruct((m, d), BF16)],
        scratch=[pltpu.VMEM((tm, d), BF16)],
    )(x, nw, w)


def _mm_nt(a, w, *, name, tm, out_dtype):
    m, k = a.shape
    n = w.shape[0]

    def body(a_ref, w_ref, o_ref):
        o_ref[...] = _dot_nt(_bf(a_ref[...]), w_ref[...]).astype(out_dtype)

    return _call(
        body, name=name, grid=(m // tm, n // COL),
        in_specs=[pl.BlockSpec((tm, k), lambda i, j: (i, 0)),
                  pl.BlockSpec((COL, k), lambda i, j: (j, 0))],
        out_specs=pl.BlockSpec((tm, COL), lambda i, j: (i, j)),
        out_shape=jax.ShapeDtypeStruct((m, n), out_dtype),
    )(a, w)


def _piece_layout(pieces):
    offs, nblk, o = [], [], 0
    for p in pieces:
        offs.append(o)
        nblk.append(p.shape[1] // COL)
        o += p.shape[1] // COL
    return offs, nblk, o


def _mm_tn(a, pieces, *, name, tm, out_dtype=BF16):
    m, k = a.shape
    offs, nblk, nn = _piece_layout(pieces)

    def piece_spec(o, nb):
        def idx(ki, nj, mi):
            use = jnp.logical_and(nj >= o, nj < o + nb)
            return (jnp.where(use, mi, 0), jnp.clip(nj - o, 0, nb - 1))
        return pl.BlockSpec((tm, COL), idx)

    def body(a_ref, *rest):
        p_refs, o_ref, acc = rest[:len(pieces)], rest[len(pieces)], rest[len(pieces) + 1]
        nj, mi = pl.program_id(1), pl.program_id(2)

        @pl.when(mi == 0)
        def _():
            acc[...] = jnp.zeros_like(acc)

        for p_ref, o, nb in zip(p_refs, offs, nblk):
            @pl.when(jnp.logical_and(nj >= o, nj < o + nb))
            def _(p_ref=p_ref):
                acc[...] += _dot_tn(_bf(a_ref[...]), _bf(p_ref[...]))

        @pl.when(mi == pl.num_programs(2) - 1)
        def _():
            o_ref[...] = acc[...].astype(out_dtype)

    return _call(
        body, name=name, grid=(k // COL, nn, m // tm),
        in_specs=[pl.BlockSpec((tm, COL), lambda ki, nj, mi: (mi, ki))]
        + [piece_spec(o, nb) for o, nb in zip(offs, nblk)],
        out_specs=pl.BlockSpec((COL, COL), lambda ki, nj, mi: (ki, nj)),
        out_shape=jax.ShapeDtypeStruct((k, nn * COL), out_dtype),
        scratch=[pltpu.VMEM((COL, COL), F32)],
    )(a, *pieces)


def _mm_nt_normbwd(pieces, w, x, nw, dres, *, name, tm, row, comm=None):
    m, d = x.shape
    nr = m // tm
    offs, nblk, nk = _piece_layout(pieces)

    def piece_spec(o, nb):
        return pl.BlockSpec((tm, COL), lambda i, k: (jnp.minimum(i, nr - 1), jnp.clip(k - o, 0, nb - 1)))

    behind = pl.BlockSpec((tm, d), lambda i, k: (jnp.maximum(i - 1, 0), 0))

    def body(*refs):
        p_refs = refs[:len(pieces)]
        w_ref, x_ref, nw_ref, dres_ref, dx_ref, dnw_ref, acc, done = refs[len(pieces):]
        i, k = pl.program_id(0), pl.program_id(1)

        def finish_previous_tile():
            dx, dnw = _rms_bwd(done[...], x_ref[...], nw_ref[...])
            dx_ref[...] = dres_ref[...] + dx
            _row_acc(dnw_ref, row, dnw)

        @pl.when(jnp.logical_and(i == 0, k == 0))
        def _():
            dnw_ref[...] = jnp.zeros_like(dnw_ref)
            done[...] = jnp.zeros_like(done)

        @pl.when(jnp.logical_and(k == 0, i < nr))
        def _():
            acc[...] = _dot_nt(_bf(p_refs[0][...]), w_ref[...])
            finish_previous_tile()

        @pl.when(jnp.logical_and(k == 0, i == nr))
        def _():
            finish_previous_tile()

        for p_ref, o, nb in zip(p_refs, offs, nblk):
            @pl.when(jnp.logical_and(i < nr, jnp.logical_and(k >= max(o, 1), k < o + nb)))
            def _(p_ref=p_ref):
                acc[...] += _dot_nt(_bf(p_ref[...]), w_ref[...])

        @pl.when(jnp.logical_and(k == nk - 1, i < nr))
        def _():
            done[...] = acc[...]

    return _call(
        body, name=name, grid=(nr + 1, nk), comm=comm,
        in_specs=[piece_spec(o, nb) for o, nb in zip(offs, nblk)]
        + [pl.BlockSpec((d, COL), lambda i, k: (0, k)), behind, pl.BlockSpec((1, d), lambda i, k: (0, 0)), behind],
        out_specs=[behind, pl.BlockSpec((8, d), lambda i, k: (0, 0))],
        out_shape=[jax.ShapeDtypeStruct((m, d), F32), jax.ShapeDtypeStruct((8, d), F32)],
        scratch=[pltpu.VMEM((tm, d), F32), pltpu.VMEM((tm, d), F32)],
    )(*pieces, w, x, nw, dres)


def _rope_tables(pos3, invf):
    b, s, _ = pos3.shape

    def body(pos_ref, invf_ref, cos_ref, sin_ref):
        half_t, half_d = s // 2, RET_DK // 2
        pos = pos_ref[...].astype(F32)
        low = lax.broadcasted_iota(jnp.int32, (half_t, RET_DK), 1) < half_d
        ang = jnp.where(low, pos[0:half_t], pos[half_t:]) * invf_ref[...]
        co, si = jnp.cos(ang), jnp.sin(ang)
        co_turned, si_turned = pltpu.roll(co, half_d, 1), pltpu.roll(si, half_d, 1)
        sign = jnp.where(low, -1.0, 1.0)
        cos_ref[...] = jnp.concatenate([jnp.where(low, co, co_turned), jnp.where(low, co_turned, co)], axis=0)
        sin_ref[...] = jnp.concatenate([sign * jnp.where(low, si, si_turned), sign * jnp.where(low, si_turned, si)],
                                       axis=0)

    spec = pl.BlockSpec((None, s, RET_DK), lambda i: (i, 0, 0))
    return _call(
        body, name="rope_tables", grid=(b,),
        in_specs=[pl.BlockSpec((None, s, 1), lambda i: (i, 0, 0)),
                  pl.BlockSpec((1, RET_DK), lambda i: (0, 0))],
        out_specs=[spec, spec],
        out_shape=[jax.ShapeDtypeStruct((b, s, RET_DK), F32)] * 2,
    )(pos3, invf)


def _log_gamma(h):
    return float(np.log1p(-np.power(np.float32(2.0), np.float32(-5.0 - h))).astype(np.float32))


def _decay_matrix(h):
    lg = _log_gamma(h)
    n = lax.broadcasted_iota(jnp.int32, (SEQ_T, SEQ_T), 0)
    m = lax.broadcasted_iota(jnp.int32, (SEQ_T, SEQ_T), 1)
    same = (n // REF_CHUNK) == (m // REF_CHUNK)
    dist = jnp.where(same, jnp.abs(n - m), n - m).astype(F32)
    return jnp.where(jnp.logical_or(same, m < n), jnp.exp(lg * dist), 0.0)


def _decay_vectors(h):
    lg = _log_gamma(h)
    idx = lax.broadcasted_iota(jnp.int32, (SEQ_T, 1), 0).astype(F32)
    qd = jnp.exp(lg * (idx + 1.0))
    kd = jnp.exp(lg * (SEQ_T - 1.0 - idx))
    return qd, kd, math.exp(lg * SEQ_T)


def _rotate(x, cos, sin):
    return x * cos + pltpu.roll(x, RET_DK // 2, 1) * sin


def _rotate_bwd(d, cos, sin):
    return d * cos + pltpu.roll(d * sin, RET_DK // 2, 1)


def _ret_head(p_ref, cos, sin, h):
    q = p_ref[:, h * RET_DK:(h + 1) * RET_DK].astype(F32)
    k = p_ref[:, 512 + h * RET_DK:512 + (h + 1) * RET_DK].astype(F32)
    v = p_ref[:, 1024 + h * RET_DV:1024 + (h + 1) * RET_DV]
    g = p_ref[:, 2048 + h * RET_DV:2048 + (h + 1) * RET_DV].astype(F32)
    qr = _rotate(q, cos, sin)
    kr = _rotate(k, cos, sin) * (RET_DK ** -0.5)
    return qr, kr, v, g


def _group_norm(o):
    mu = jnp.mean(o, axis=-1, keepdims=True)
    oc = o - mu
    rstd = lax.rsqrt(jnp.mean(oc * oc, axis=-1, keepdims=True) + GN_EPS)
    return oc * rstd, rstd


def _retention_fwd(proj3, cos, sin, gnw, comm=None):
    b, s, _ = proj3.shape
    nc = s // SEQ_T

    def body(p_ref, cos_ref, sin_ref, gnw_ref, a_ref, st_ref, state, wtab):
        c = pl.program_id(1)

        @pl.when(jnp.logical_and(pl.program_id(0) == 0, c == 0))
        def _():
            for h in range(RET_HEADS):
                wtab[h] = _decay_matrix(h)

        @pl.when(c == 0)
        def _():
            state[...] = jnp.zeros_like(state)

        cs, sn = cos_ref[...], sin_ref[...]
        st_ref[...] = state[...]
        outs, states = [], []
        for h in range(RET_HEADS):
            qd, kd, gt = _decay_vectors(h)
            qr, kr, v, g = _ret_head(p_ref, cs, sn, h)
            st = state[h]
            p = _dot_nt(_bf(qr), _bf(kr)) * wtab[h]
            o = _dot(_bf(p), _bf(v)) + _dot(_bf(qr * qd), _bf(st))
            states.append(st * gt + _dot_tn(_bf(kr * kd), _bf(v)))
            on, _ = _group_norm(o)
            gw = gnw_ref[:, h * RET_DV:(h + 1) * RET_DV]
            outs.append(_bf(on * gw * (g * _sigmoid(g))))
        a_ref[...] = jnp.concatenate(outs, axis=1)
        state[...] = jnp.stack(states)

    tab = pl.BlockSpec((None, SEQ_T, RET_DK), lambda i, c: (i, c, 0))
    return _call(
        body, name="retention_fwd", grid=(b, nc), comm=comm,
        in_specs=[pl.BlockSpec((None, SEQ_T, 3072), lambda i, c: (i, c, 0)), tab, tab,
                  pl.BlockSpec((1, D_MODEL), lambda i, c: (0, 0))],
        out_specs=[pl.BlockSpec((None, SEQ_T, D_MODEL), lambda i, c: (i, c, 0)),
                   pl.BlockSpec((None, None, RET_HEADS, RET_DK, RET_DV), lambda i, c: (i, c, 0, 0, 0))],
        out_shape=[jax.ShapeDtypeStruct((b, s, D_MODEL), BF16),
                   jax.ShapeDtypeStruct((b, nc, RET_HEADS, RET_DK, RET_DV), F32)],
        scratch=[pltpu.VMEM((RET_HEADS, RET_DK, RET_DV), F32),
                 pltpu.VMEM((RET_HEADS, SEQ_T, SEQ_T), F32)],
    )(proj3, cos, sin, gnw)


def _retention_bwd(proj3, cos, sin, gnw, states, da3, comm=None):
    b, s, _ = proj3.shape
    nc = s // SEQ_T

    def body(p_ref, cos_ref, sin_ref, gnw_ref, st_ref, da_ref, d_ref, dgn_ref, dstate, wtab):
        c = pl.program_id(1)

        @pl.when(jnp.logical_and(pl.program_id(0) == 0, c == 0))
        def _():
            dgn_ref[...] = jnp.zeros_like(dgn_ref)
            for h in range(RET_HEADS):
                wtab[h] = _decay_matrix(h)

        @pl.when(c == 0)
        def _():
            dstate[...] = jnp.zeros_like(dstate)

        cs, sn = cos_ref[...], sin_ref[...]
        dqs, dks, dvs, dgs, dgns, dstates = [], [], [], [], [], []
        for h in range(RET_HEADS):
            qd, kd, gt = _decay_vectors(h)
            qr, kr, v, g = _ret_head(p_ref, cs, sn, h)
            st, dst, w = st_ref[h], dstate[h], wtab[h]
            qb, kb, vb = _bf(qr), _bf(kr), _bf(v)
            p = _dot_nt(qb, kb) * w
            o = _dot(_bf(p), vb) + _dot(_bf(qr * qd), _bf(st))
            on, rstd = _group_norm(o)
            gw = gnw_ref[:, h * RET_DV:(h + 1) * RET_DV]
            da = da_ref[:, h * RET_DV:(h + 1) * RET_DV].astype(F32)
            sg = _sigmoid(g)
            silu = g * sg
            dg = da * on * gw * (sg * (1.0 + g * (1.0 - sg)))
            dgns.append(jnp.sum(da * silu * on, axis=0, keepdims=True))
            don = da * silu * gw
            do = rstd * (don - jnp.mean(don, axis=-1, keepdims=True)
                         - on * jnp.mean(don * on, axis=-1, keepdims=True))
            dob = _bf(do)
            dp = _dot_nt(dob, vb) * w
            dqr = _dot(_bf(dp), kb) + _dot_nt(dob, _bf(st)) * qd
            dkr = _dot_tn(_bf(dp), qb) + _dot_nt(vb, _bf(dst)) * kd
            dv = _dot_tn(_bf(p), dob) + _dot(_bf(kr * kd), _bf(dst))
            dstates.append(dst * gt + _dot_tn(_bf(qr * qd), dob))
            dqs.append(_bf(_rotate_bwd(dqr, cs, sn)))
            dks.append(_bf(_rotate_bwd(dkr, cs, sn) * (RET_DK ** -0.5)))
            dvs.append(_bf(dv))
            dgs.append(_bf(dg))
        d_ref[...] = jnp.concatenate(dqs + dks + dvs + dgs, axis=1)
        _row_acc(dgn_ref, ROW_GN, jnp.concatenate(dgns, axis=1))
        dstate[...] = jnp.stack(dstates)

    rev = lambda i, c: (i, nc - 1 - c, 0)
    tab = pl.BlockSpec((None, SEQ_T, RET_DK), rev)
    return _call(
        body, name="retention_bwd", grid=(b, nc), comm=comm,
        in_specs=[pl.BlockSpec((None, SEQ_T, 3072), rev), tab, tab,
                  pl.BlockSpec((1, D_MODEL), lambda i, c: (0, 0)),
                  pl.BlockSpec((None, None, RET_HEADS, RET_DK, RET_DV), lambda i, c: (i, nc - 1 - c, 0, 0, 0)),
                  pl.BlockSpec((None, SEQ_T, D_MODEL), rev)],
        out_specs=[pl.BlockSpec((None, SEQ_T, 3072), rev),
                   pl.BlockSpec((8, D_MODEL), lambda i, c: (0, 0))],
        out_shape=[jax.ShapeDtypeStruct((b, s, 3072), BF16), jax.ShapeDtypeStruct((8, D_MODEL), F32)],
        scratch=[pltpu.VMEM((RET_HEADS, RET_DK, RET_DV), F32),
                 pltpu.VMEM((RET_HEADS, SEQ_T, SEQ_T), F32)],
    )(proj3, cos, sin, gnw, states, da3)


def _softplus_neg(lam):
    z = -lam
    u = jnp.exp(-jnp.abs(z))
    log1p_u = jnp.where(u < 0.01, u * (1.0 - u * (0.5 - u * (1.0 / 3.0))), jnp.log(1.0 + u))
    return jnp.maximum(z, 0.0) + log1p_u


def _lru_coeffs(xc, wr_ref, br_ref, wi_ref, bi_ref, lam_ref):
    rs, is_ = [], []
    for n in range(LRU_BLOCKS):
        xb = _bf(xc[:, n * LRU_BLOCK:(n + 1) * LRU_BLOCK])
        cols = slice(n * LRU_BLOCK, (n + 1) * LRU_BLOCK)
        rs.append(_sigmoid(_dot(xb, wr_ref[n]) + br_ref[:, cols]))
        is_.append(_sigmoid(_dot(xb, wi_ref[n]) + bi_ref[:, cols]))
    r = jnp.concatenate(rs, axis=1)
    i = jnp.concatenate(is_, axis=1)
    sp = _softplus_neg(lam_ref[...])
    la = -LRU_C * r * sp
    a = jnp.exp(la)
    s = jnp.sqrt(-jnp.tanh(la) * (a * a + 1.0))
    return r, i, a, s, sp


_LRU_PARAM_SPECS = [
    pl.BlockSpec((4, D_MODEL), lambda i, c: (0, 0)),
    pl.BlockSpec((1, D_MODEL), lambda i, c: (0, 0)),
    pl.BlockSpec((LRU_BLOCKS, LRU_BLOCK, LRU_BLOCK), lambda i, c: (0, 0, 0)),
    pl.BlockSpec((1, D_MODEL), lambda i, c: (0, 0)),
    pl.BlockSpec((LRU_BLOCKS, LRU_BLOCK, LRU_BLOCK), lambda i, c: (0, 0, 0)),
    pl.BlockSpec((1, D_MODEL), lambda i, c: (0, 0)),
    pl.BlockSpec((1, D_MODEL), lambda i, c: (0, 0)),
]


def _lru_fwd(proj3, params, comm=None):
    b, s, _ = proj3.shape
    nc = s // SEQ_T

    def body(x_ref, y_ref, cw, cb, wr, br, wi, bi, lam,
             o_ref, h_ref, xc_ref, a_ref, s_ref, gy_ref, hdg_ref, r_ref, i_ref, xprev, hprev):
        @pl.when(pl.program_id(1) == 0)
        def _():
            xprev[...] = jnp.zeros_like(xprev)
            hprev[...] = jnp.zeros_like(hprev)

        x = x_ref[...].astype(F32)
        prev8 = xprev[...]
        xc = cb[...] + sum(cw[j:j + 1, :] * _shift_down(x, 3 - j, prev8) for j in range(4))
        xprev[...] = x[SEQ_T - 8:]
        xc_ref[...] = xc
        r, i, a, s_, _ = _lru_coeffs(xc, wr, br, wi, bi, lam)
        a_ref[...] = a
        s_ref[...] = s_
        r_ref[...] = _bf(r)
        i_ref[...] = _bf(i)
        h = _scan_fwd(a, s_ * (i * xc), hprev[7:8, :])
        hprev[...] = h[SEQ_T - 8:]
        h_ref[...] = h
        gy, dgy = _gelu_and_grad(y_ref[...].astype(F32))
        o_ref[...] = _bf(h * gy)
        gy_ref[...] = _bf(gy)
        hdg_ref[...] = _bf(h * dgy)

    out = pl.BlockSpec((None, SEQ_T, D_MODEL), lambda i, c: (i, c, 0))
    half, full = jax.ShapeDtypeStruct((b, s, D_MODEL), BF16), jax.ShapeDtypeStruct((b, s, D_MODEL), F32)
    return _call(
        body, name="lru_fwd", grid=(b, nc), comm=comm,
        in_specs=[pl.BlockSpec((None, SEQ_T, D_MODEL), lambda i, c: (i, c, 3)),
                  pl.BlockSpec((None, SEQ_T, D_MODEL), lambda i, c: (i, c, 4))] + _LRU_PARAM_SPECS,
        out_specs=[out] * 9, out_shape=[half, full, full, full, full, half, half, half, half],
        scratch=[pltpu.VMEM((8, D_MODEL), F32), pltpu.VMEM((8, D_MODEL), F32)],
    )(proj3, proj3, *params)


def _lru_bwd(proj3, params, kept, db3, comm=None):
    b, s, _ = proj3.shape
    nc = s // SEQ_T
    blk8 = SEQ_T // 8
    hseq = kept[0]

    def body(x_ref, h_ref, xc_ref, a_ref, s_ref, gy_ref, hdg_ref, r_ref, i_ref, hp_ref, db_ref,
             cw, cb, wr, br, wi, bi, lam, d_ref, dwr_ref, dwi_ref, sm_ref, gnext, anext, dxcnext):
        c = pl.program_id(1)
        first_chunk = c == nc - 1

        @pl.when(jnp.logical_and(pl.program_id(0) == 0, c == 0))
        def _():
            dwr_ref[...] = jnp.zeros_like(dwr_ref)
            dwi_ref[...] = jnp.zeros_like(dwi_ref)
            sm_ref[...] = jnp.zeros_like(sm_ref)

        @pl.when(c == 0)
        def _():
            gnext[...] = jnp.zeros_like(gnext)
            anext[...] = jnp.zeros_like(anext)
            dxcnext[...] = jnp.zeros_like(dxcnext)

        x, xc, h = x_ref[...].astype(F32), xc_ref[...], h_ref[...]
        hprev = hp_ref[...] * jnp.where(first_chunk, 0.0, 1.0)
        r, i, a, s_ = r_ref[...].astype(F32), i_ref[...].astype(F32), a_ref[...], s_ref[...]
        sp = _softplus_neg(lam[...])
        db = db_ref[...].astype(F32)
        dy = db * hdg_ref[...].astype(F32)
        a_up = _shift_up(a, 1, anext[...])
        g = _scan_bwd(a_up, db * gy_ref[...].astype(F32), gnext[0:1, :])
        gnext[...] = g[0:8]
        anext[...] = a[0:8]
        da = g * _shift_down(h, 1, hprev)
        ixc = i * xc
        dla = da * a - (g * ixc) * (a * a) / s_
        di = g * s_ * xc
        dxc = g * s_ * i
        dzr = dla * (-LRU_C * sp) * r * (1.0 - r)
        dzi = di * i * (1.0 - i)
        lam_v = lam[...]
        _row_acc(sm_ref, 7, jnp.sum(dla * (LRU_C * r), axis=0, keepdims=True) * _sigmoid(-lam_v))
        _row_acc(sm_ref, 5, jnp.sum(dzr, axis=0, keepdims=True))
        _row_acc(sm_ref, 6, jnp.sum(dzi, axis=0, keepdims=True))
        parts, dwr_parts, dwi_parts = [], [], []
        for n in range(LRU_BLOCKS):
            cols = slice(n * LRU_BLOCK, (n + 1) * LRU_BLOCK)
            xb, zr, zi = _bf(xc[:, cols]), _bf(dzr[:, cols]), _bf(dzi[:, cols])
            parts.append(dxc[:, cols] + _dot_nt(zr, wr[n]) + _dot_nt(zi, wi[n]))
            dwr_parts.append(_dot_tn(xb, zr))
            dwi_parts.append(_dot_tn(xb, zi))
        dwr_ref[...] += jnp.stack(dwr_parts)
        dwi_ref[...] += jnp.stack(dwi_parts)
        dxc = jnp.concatenate(parts, axis=1)
        _row_acc(sm_ref, 4, jnp.sum(dxc, axis=0, keepdims=True))
        nxt = dxcnext[...]
        dx = jnp.zeros_like(x)
        for j in range(4):
            ahead = _shift_up(dxc, 3 - j, nxt)
            dx = dx + cw[j:j + 1, :] * ahead
            _row_acc(sm_ref, j, jnp.sum(ahead * x, axis=0, keepdims=True))
        dxcnext[...] = dxc[0:8]
        d_ref[:, 0:D_MODEL] = _bf(dx)
        d_ref[:, D_MODEL:2 * D_MODEL] = _bf(dy)

    rev = lambda col: (lambda i, c: (i, nc - 1 - c, col))
    prev = lambda col: (lambda i, c: (i, jnp.maximum((nc - 1 - c) * blk8 - 1, 0), col))
    return _call(
        body, name="lru_bwd", grid=(b, nc), comm=comm,
        in_specs=[pl.BlockSpec((None, SEQ_T, D_MODEL), rev(3))]
        + [pl.BlockSpec((None, SEQ_T, D_MODEL), rev(0))] * len(kept)
        + [pl.BlockSpec((None, 8, D_MODEL), prev(0)), pl.BlockSpec((None, SEQ_T, D_MODEL), rev(0))]
        + _LRU_PARAM_SPECS,
        out_specs=[pl.BlockSpec((None, SEQ_T, 2 * D_MODEL), rev(0)),
                   pl.BlockSpec((LRU_BLOCKS, LRU_BLOCK, LRU_BLOCK), lambda i, c: (0, 0, 0)),
                   pl.BlockSpec((LRU_BLOCKS, LRU_BLOCK, LRU_BLOCK), lambda i, c: (0, 0, 0)),
                   pl.BlockSpec((8, D_MODEL), lambda i, c: (0, 0))],
        out_shape=[jax.ShapeDtypeStruct((b, s, 2 * D_MODEL), BF16),
                   jax.ShapeDtypeStruct((LRU_BLOCKS, LRU_BLOCK, LRU_BLOCK), F32),
                   jax.ShapeDtypeStruct((LRU_BLOCKS, LRU_BLOCK, LRU_BLOCK), F32),
                   jax.ShapeDtypeStruct((8, D_MODEL), F32)],
        scratch=[pltpu.VMEM((8, D_MODEL), F32)] * 3,
    )(proj3, *kept, hseq, db3, *params)


def _merge_parts(a_ref, b_ref, gr_ref, gl_ref, mgb_ref, wro_ref, wlo_ref):
    ya = _dot(a_ref[...], wro_ref[...])
    yb = _dot(b_ref[...], wlo_ref[...])
    sa = _sigmoid(gr_ref[...].astype(F32) + mgb_ref[0:1, :])
    sb = _sigmoid(gl_ref[...].astype(F32) + mgb_ref[1:2, :])
    return ya, yb, sa, sb


def _merge_specs(tm):
    row = lambda col: pl.BlockSpec((tm, D_MODEL), lambda i: (i, col))
    full = pl.BlockSpec((D_MODEL, D_MODEL), lambda i: (0, 0))
    return row, full


def _merge_fwd(a_in, b_in, proj, mgb, wro, wlo, wout, x, *, tm, comm=None):
    m = x.shape[0]
    row, full = _merge_specs(tm)

    def body(a_ref, b_ref, gr_ref, gl_ref, mgb_ref, wro_ref, wlo_ref, wout_ref, x_ref,
             o_ref, mix_ref, ya_ref, yb_ref):
        ya, yb, sa, sb = _merge_parts(a_ref, b_ref, gr_ref, gl_ref, mgb_ref, wro_ref, wlo_ref)
        mix = _bf(sa * ya + sb * yb)
        o_ref[...] = x_ref[...] + _dot(mix, wout_ref[...])
        mix_ref[...] = mix
        ya_ref[...] = _bf(ya)
        yb_ref[...] = _bf(yb)

    act = jax.ShapeDtypeStruct((m, D_MODEL), BF16)
    return _call(
        body, name="merge_fwd", grid=(m // tm,), comm=comm,
        in_specs=[row(0), row(0), row(5), row(6), pl.BlockSpec((2, D_MODEL), lambda i: (0, 0)),
                  full, full, full, row(0)],
        out_specs=[row(0)] * 4,
        out_shape=[jax.ShapeDtypeStruct((m, D_MODEL), F32), act, act, act],
    )(a_in, b_in, proj, proj, mgb, wro, wlo, wout, x)


def _merge_bwd(ya, yb, proj, mgb, wro, wlo, wout, dx2, *, tm):
    m = dx2.shape[0]
    row, full = _merge_specs(tm)

    def body(ya_ref, yb_ref, gr_ref, gl_ref, mgb_ref, wro_ref, wlo_ref, wout_ref, dx_ref,
             dya_ref, dyb_ref, da_ref, db_ref, dg_ref, sm_ref):
        @pl.when(pl.program_id(0) == 0)
        def _():
            sm_ref[...] = jnp.zeros_like(sm_ref)

        ya, yb = ya_ref[...].astype(F32), yb_ref[...].astype(F32)
        sa = _sigmoid(gr_ref[...].astype(F32) + mgb_ref[0:1, :])
        sb = _sigmoid(gl_ref[...].astype(F32) + mgb_ref[1:2, :])
        dmix = _dot_nt(_bf(dx_ref[...]), wout_ref[...])
        dya, dyb = _bf(dmix * sa), _bf(dmix * sb)
        dya_ref[...] = dya
        dyb_ref[...] = dyb
        dga = dmix * ya * sa * (1.0 - sa)
        dgb = dmix * yb * sb * (1.0 - sb)
        dg_ref[:, 0:D_MODEL] = _bf(dga)
        dg_ref[:, D_MODEL:2 * D_MODEL] = _bf(dgb)
        _row_acc(sm_ref, ROW_MGB, jnp.sum(dga, axis=0, keepdims=True))
        _row_acc(sm_ref, ROW_MGB + 1, jnp.sum(dgb, axis=0, keepdims=True))
        da_ref[...] = _bf(_dot_nt(dya, wro_ref[...]))
        db_ref[...] = _bf(_dot_nt(dyb, wlo_ref[...]))

    act = jax.ShapeDtypeStruct((m, D_MODEL), BF16)
    return _call(
        body, name="merge_bwd", grid=(m // tm,),
        in_specs=[row(0), row(0), row(5), row(6), pl.BlockSpec((2, D_MODEL), lambda i: (0, 0)),
                  full, full, full, row(0)],
        out_specs=[row(0)] * 4 + [pl.BlockSpec((tm, 2 * D_MODEL), lambda i: (i, 0)),
                                  pl.BlockSpec((8, D_MODEL), lambda i: (0, 0))],
        out_shape=[act] * 4 + [jax.ShapeDtypeStruct((m, 2 * D_MODEL), BF16),
                               jax.ShapeDtypeStruct((8, D_MODEL), F32)],
    )(ya, yb, proj, proj, mgb, wro, wlo, wout, dx2)


def _ffn_act_fwd(up3, cw, cb):
    b, s, _ = up3.shape

    def body(g_ref, v_ref, cw_ref, cb_ref, o_ref, act_ref, q_ref, gprev):
        @pl.when(pl.program_id(1) == 0)
        def _():
            gprev[...] = jnp.zeros_like(gprev)

        gate, val = g_ref[...].astype(F32), v_ref[...].astype(F32)
        prev8 = gprev[...]
        gc = cb_ref[...] + sum(cw_ref[j:j + 1, :] * _shift_down(gate, 2 - j, prev8) for j in range(3))
        gprev[...] = gate[SEQ_T - 8:]
        act, dact = _gelu_and_grad(gc)
        o_ref[...] = _bf(act * val)
        act_ref[...] = _bf(act)
        q_ref[...] = _bf(dact * val)

    out = pl.BlockSpec((None, SEQ_T, D_FF), lambda i, c: (i, c, 0))
    return _call(
        body, name="ffn_act_fwd", grid=(b, s // SEQ_T),
        in_specs=[pl.BlockSpec((None, SEQ_T, D_FF), lambda i, c: (i, c, 0)),
                  pl.BlockSpec((None, SEQ_T, D_FF), lambda i, c: (i, c, 1)),
                  pl.BlockSpec((3, D_FF), lambda i, c: (0, 0)),
                  pl.BlockSpec((1, D_FF), lambda i, c: (0, 0))],
        out_specs=[out] * 3,
        out_shape=[jax.ShapeDtypeStruct((b, s, D_FF), BF16)] * 3,
        scratch=[pltpu.VMEM((8, D_FF), F32)],
    )(up3, up3, cw, cb)


def _ffn_act_bwd(up3, act3, q3, cw, df3, comm=None):
    b, s, _ = up3.shape
    nc = s // SEQ_T

    def body(g_ref, act_ref, q_ref, df_ref, cw_ref, dg_ref, dv_ref, sm_ref, dgcnext):
        c = pl.program_id(1)

        @pl.when(jnp.logical_and(pl.program_id(0) == 0, c == 0))
        def _():
            sm_ref[...] = jnp.zeros_like(sm_ref)

        @pl.when(c == 0)
        def _():
            dgcnext[...] = jnp.zeros_like(dgcnext)

        gate = g_ref[...].astype(F32)
        df = df_ref[...].astype(F32)
        dv_ref[...] = _bf(df * act_ref[...].astype(F32))
        dgc = df * q_ref[...].astype(F32)
        nxt = dgcnext[...]
        dgate = jnp.zeros_like(gate)
        for j in range(3):
            ahead = _shift_up(dgc, 2 - j, nxt)
            dgate = dgate + cw_ref[j:j + 1, :] * ahead
            _row_acc(sm_ref, j, jnp.sum(ahead * gate, axis=0, keepdims=True))
        _row_acc(sm_ref, 3, jnp.sum(dgc, axis=0, keepdims=True))
        dgcnext[...] = dgc[0:8]
        dg_ref[...] = _bf(dgate)

    rev = pl.BlockSpec((None, SEQ_T, D_FF), lambda i, c: (i, nc - 1 - c, 0))
    return _call(
        body, name="ffn_act_bwd", grid=(b, nc), comm=comm,
        in_specs=[rev, rev, rev, rev, pl.BlockSpec((3, D_FF), lambda i, c: (0, 0))],
        out_specs=[rev, rev, pl.BlockSpec((8, D_FF), lambda i, c: (0, 0))],
        out_shape=[jax.ShapeDtypeStruct((b, s, D_FF), BF16)] * 2 + [jax.ShapeDtypeStruct((8, D_FF), F32)],
        scratch=[pltpu.VMEM((8, D_FF), F32)],
    )(up3, act3, q3, df3, cw)


def _ffn_down_loss(f, wd, x2, nfw, target, *, tm):
    m, kf = f.shape
    nt = m // tm

    def body(f_ref, wd_ref, x_ref, nw_ref, t_ref, dx_ref, dnw_ref, lsum, x3_prev):
        i = pl.program_id(0)

        @pl.when(i == 0)
        def _():
            dnw_ref[...] = jnp.zeros_like(dnw_ref)
            lsum[...] = jnp.zeros_like(lsum)
            x3_prev[...] = jnp.zeros_like(x3_prev)

        x3_new = x_ref[...] + _dot(f_ref[...], wd_ref[...])
        x3 = x3_prev[...]
        nw = nw_ref[...]
        xh, r = _rms(x3)
        err = xh * nw - t_ref[...]
        lsum[...] += jnp.sum(err * err, axis=0, keepdims=True) * jnp.where(i >= 1, 1.0, 0.0)
        dy = err * (1.0 / D_MODEL)
        g = dy * nw
        dx_ref[...] = r * (g - xh * jnp.mean(g * xh, axis=-1, keepdims=True))
        _row_acc(dnw_ref, ROW_NF, jnp.sum(dy * xh, axis=0, keepdims=True))
        x3_prev[...] = x3_new

        @pl.when(i == nt)
        def _():
            loss = jnp.sum(lsum[...], axis=1, keepdims=True) * (0.5 / D_MODEL)
            dnw_ref[ROW_LOSS:ROW_LOSS + 1, :] = jnp.broadcast_to(loss, (1, D_MODEL))

    ahead = lambda width: pl.BlockSpec((tm, width), lambda i: (jnp.minimum(i, nt - 1), 0))
    behind = pl.BlockSpec((tm, D_MODEL), lambda i: (jnp.maximum(i - 1, 0), 0))
    return _call(
        body, name="ffn_down_loss", grid=(nt + 1,),
        in_specs=[ahead(kf), pl.BlockSpec((kf, D_MODEL), lambda i: (0, 0)), ahead(D_MODEL),
                  pl.BlockSpec((1, D_MODEL), lambda i: (0, 0)), behind],
        out_specs=[behind, pl.BlockSpec((8, D_MODEL), lambda i: (0, 0))],
        out_shape=[jax.ShapeDtypeStruct((m, D_MODEL), F32), jax.ShapeDtypeStruct((8, D_MODEL), F32)],
        scratch=[pltpu.VMEM((1, D_MODEL), F32), pltpu.VMEM((tm, D_MODEL), F32)],
    )(f, wd, x2, nfw, target)


def _row_tile(rows):
    return next((t for t in (256, 128, 64, 32, 16, 8) if rows % t == 0), rows)


def _adamw(w, gs, m, v, *, name):
    rows, cols = w.shape
    tr = _row_tile(rows)
    ng = len(gs)

    def body(w_ref, *rest):
        g_refs, (m_ref, v_ref, g_out, d_out, m_out, v_out) = rest[:ng], rest[ng:]
        g = g_refs[0][...]
        for r in g_refs[1:]:
            g = g + r[...]
        mn = ADAM_B1 * m_ref[...] + (1.0 - ADAM_B1) * g
        vn = ADAM_B2 * v_ref[...] + (1.0 - ADAM_B2) * (g * g)
        m_hat = mn / (1.0 - ADAM_B1 ** ADAM_STEP)
        v_hat = vn / (1.0 - ADAM_B2 ** ADAM_STEP)
        g_out[...] = g
        d_out[...] = -ADAM_LR * (m_hat / (jnp.sqrt(v_hat) + ADAM_EPS) + ADAM_WD * w_ref[...])
        m_out[...] = mn
        v_out[...] = vn

    spec = pl.BlockSpec((tr, cols), lambda i: (i, 0))
    return _call(
        body, name=name, grid=(rows // tr,),
        in_specs=[spec] * (3 + ng), out_specs=[spec] * 4,
        out_shape=[jax.ShapeDtypeStruct((rows, cols), F32)] * 4,
    )(w, *gs, m, v)


def _mesh_pos():
    x, y, c = lax.axis_index("x"), lax.axis_index("y"), lax.axis_index("c")
    return x, y, c


def _other_chips(x, y, c):
    return [((1 - x, y, c), 2 * (1 - x) + y), ((x, 1 - y, c), 2 * x + 1 - y),
            ((1 - x, 1 - y, c), 2 * (1 - x) + 1 - y)]


def _region(ref, axis, size, half_axis, chip, core=None):
    idx = [slice(None)] * len(ref.shape)
    if core is None:
        idx[axis] = pl.ds(pl.multiple_of(chip * size, size), size)
    elif half_axis == axis:
        h = size // 2
        idx[axis] = pl.ds(pl.multiple_of(chip * size + core * h, h), h)
    else:
        idx[axis] = pl.ds(pl.multiple_of(chip * size, size), size)
        h = ref.shape[half_axis] // 2
        idx[half_axis] = pl.ds(pl.multiple_of(core * h, h), h)
    return ref.at[tuple(idx)]


def _half(ref, half_axis, core):
    idx = [slice(None)] * len(ref.shape)
    h = ref.shape[half_axis] // 2
    idx[half_axis] = pl.ds(pl.multiple_of(core * h, h), h)
    return ref.at[tuple(idx)]


class _Copy:
    def __init__(self, make):
        self._make = make

    def start(self):
        self._make().start()

    def wait(self):
        self._make().wait()

    def wait_send(self):
        self._make().wait_send()

    def wait_recv(self):
        self._make().wait_recv()


def _remote(src, dst, send_sem, recv_sem, dev):
    return _Copy(lambda: pltpu.make_async_remote_copy(
        src_ref=src, dst_ref=dst, send_sem=send_sem, recv_sem=recv_sem, device_id=dev, device_id_type=MESH))


def _local(src, dst, sem):
    return _Copy(lambda: pltpu.make_async_copy(src, dst, sem))


def _dma_sems(n):
    return pltpu.SemaphoreType.DMA((n,))


def _place_shard(w, chip, axis, *, name):
    shape = list(w.shape)
    shape[axis] *= N_CHIPS
    if w.ndim == 3:
        block, grid = (1,) + w.shape[1:], (w.shape[0],)
        in_map, out_map = (lambda i, chip: (i, 0, 0)), (lambda i, chip: (i, chip[0], 0))
    else:
        tr = _row_tile(w.shape[0])
        nt = w.shape[0] // tr
        block, grid = (tr, w.shape[1]), (nt,)
        in_map = lambda i, chip: (i, 0)
        out_map = (lambda i, chip: (chip[0] * nt + i, 0)) if axis == 0 else (lambda i, chip: (i, chip[0]))

    def body(chip_ref, w_ref, o_ref):
        o_ref[...] = _bf(w_ref[...])

    return _call(body, name=name, grid=grid, prefetch=1, in_specs=[pl.BlockSpec(block, in_map)],
                 out_specs=pl.BlockSpec(block, out_map),
                 out_shape=jax.ShapeDtypeStruct(tuple(shape), BF16))(chip, w)


def _ici_leg(srcs, dsts, layout, sizes, n_whole, sems):
    send_sems, recv_sems, local_sems = sems
    x, y, c = _mesh_pos()
    mine = 2 * x + y
    n_big = len(srcs) - n_whole
    local, sends, recvs = [], [], []
    for t, (src, dst) in enumerate(zip(srcs, dsts)):
        if t < n_big:
            ax, hx = layout[t]
            part = _region(src, ax, sizes[t], hx, mine, c)
            landing = lambda chip, dst=dst, ax=ax, hx=hx, size=sizes[t]: _region(dst, ax, size, hx, chip, c)
        else:
            part, landing = src, (lambda chip, dst=dst: dst.at[chip])
            local.append(_local(src, dst.at[mine], local_sems.at[t - n_big]))
        for k, (dev, chip) in enumerate(_other_chips(x, y, c)):
            sends.append(_remote(part, landing(mine), send_sems.at[3 * t + k], recv_sems.at[3 * t + k], dev))
            recvs.append(_remote(part, landing(chip), send_sems.at[3 * t + k], recv_sems.at[3 * t + k], dev))
    return local, sends, recvs


def _d2d_leg(srcs, dsts, layout, sizes, sems):
    send_sems, recv_sems = sems
    x, y, c = _mesh_pos()
    sends, recvs = [], []
    for t, (src, dst) in enumerate(zip(srcs, dsts)):
        ax, hx = layout[t]
        for k, (_, chip) in enumerate(_other_chips(x, y, c)):
            sem = (send_sems.at[3 * t + k], recv_sems.at[3 * t + k])
            sends.append(_remote(_region(src, ax, sizes[t], hx, chip, c),
                                 _region(dst, ax, sizes[t], hx, chip, c), *sem, (x, y, 1 - c)))
            recvs.append(_remote(_region(src, ax, sizes[t], hx, chip, 1 - c),
                                 _region(dst, ax, sizes[t], hx, chip, 1 - c), *sem, (x, y, 1 - c)))
    return sends, recvs


def _gather_shapes(bufs, whole):
    return ([jax.ShapeDtypeStruct(b.shape, b.dtype) for b in bufs]
            + [jax.ShapeDtypeStruct((N_CHIPS,) + w.shape, w.dtype) for w in whole])


def _gather_ici(bufs, layout):
    n = len(bufs)
    sizes = [b.shape[ax] // N_CHIPS for b, (ax, _) in zip(bufs, layout)]

    def start(ins, outs, sems):
        for cp in _ici_leg(ins, outs, layout, sizes, 0, (*sems, None))[1]:
            cp.start()

    def finish(ins, outs, sems):
        _, sends, recvs = _ici_leg(ins, outs, layout, sizes, 0, (*sems, None))
        for cp in recvs:
            cp.wait_recv()
        for cp in sends:
            cp.wait_send()

    return _Comm(bufs, _gather_shapes(bufs, ()), [_dma_sems(3 * n), _dma_sems(3 * n)], start, finish,
                 aliases={i: i for i in range(n)})


def _both(a, b):
    ni, no, ns = len(a.ins), len(a.outs), len(a.sems)

    def start(ins, outs, sems):
        a.start(ins[:ni], outs[:no], sems[:ns])
        b.start(ins[ni:], outs[no:], sems[ns:])

    def finish(ins, outs, sems):
        a.finish(ins[:ni], outs[:no], sems[:ns])
        b.finish(ins[ni:], outs[no:], sems[ns:])

    aliases = {**a.aliases, **{ni + i: no + o for i, o in b.aliases.items()}}
    return _Comm(a.ins + b.ins, a.outs + b.outs, a.sems + b.sems, start, finish, aliases)


def _gather_d2d(bufs, layout, sizes):
    n = len(bufs)

    def start(ins, outs, sems):
        for cp in _d2d_leg(ins, outs, layout, sizes, sems)[0]:
            cp.start()

    def finish(ins, outs, sems):
        sends, recvs = _d2d_leg(ins, outs, layout, sizes, sems)
        for cp in recvs:
            cp.wait_recv()
        for cp in sends:
            cp.wait_send()

    return _Comm(bufs, [jax.ShapeDtypeStruct(b.shape, b.dtype) for b in bufs],
                 [_dma_sems(3 * n), _dma_sems(3 * n)], start, finish, aliases={i: i for i in range(n)})


def _norm_bf16(x, nw, *, name, tm):
    m, d = x.shape

    def body(x_ref, nw_ref, h_ref):
        h_ref[...] = _bf(_rms(x_ref[...])[0] * nw_ref[...])

    row = pl.BlockSpec((tm, d), lambda i: (i, 0))
    return _call(body, name=name, grid=(m // tm,), in_specs=[row, pl.BlockSpec((1, d), lambda i: (0, 0))],
                 out_specs=row, out_shape=jax.ShapeDtypeStruct((m, d), BF16))(x, nw)


def _in_proj_gather(h1, w_buf, later, later_cut, small, order, *, tm):
    m, d = h1.shape
    width = w_buf.shape[1] // N_CHIPS
    nr, nl = m // tm, len(later)
    sizes = [b.shape[ax] // N_CHIPS for b, (ax, _) in zip(later, later_cut)]

    def body(order_ref, h_ref, w_in, *rest):
        later_in, small_in = rest[:nl], rest[nl]
        o_ref, w_out = rest[nl + 1], rest[nl + 2]
        later_out, small_out = rest[nl + 3:2 * nl + 3], rest[2 * nl + 3]
        wv, load_sem, ici_send, ici_recv, d2d_send, d2d_recv, l_send, l_recv, l_local = rest[2 * nl + 4:]
        s, i = pl.program_id(0), pl.program_id(1)
        x, y, c = _mesh_pos()
        mine = 2 * x + y
        peers = _other_chips(x, y, c)
        part = lambda ref, chip, core=None: _region(ref, 1, width, 0, chip, core)

        def ici(k):
            dev, chip = peers[k]
            sem = (ici_send.at[k], ici_recv.at[k])
            return (_remote(part(w_in, mine, c), part(w_out, mine, c), *sem, dev),
                    _remote(part(w_in, chip, c), part(w_out, chip, c), *sem, dev))

        def d2d(k):
            chip, sem, sib = peers[k][1], (d2d_send.at[k], d2d_recv.at[k]), (x, y, 1 - c)
            return (_remote(part(w_out, chip, c), part(w_out, chip, c), *sem, sib),
                    _remote(part(w_out, chip, 1 - c), part(w_out, chip, 1 - c), *sem, sib))

        def load(src, chip, slot):
            cp = _local(part(src, chip), wv.at[slot], load_sem.at[slot])
            cp.start()
            cp.wait()

        def others():
            return _ici_leg(list(later_in) + [small_in], list(later_out) + [small_out], later_cut, sizes, 1,
                            (l_send, l_recv, l_local))

        @pl.when(jnp.logical_and(s == 0, i == 0))
        def _():
            for k in range(3):
                ici(k)[0].start()
            local, sends, _ = others()
            for cp in local + sends:
                cp.start()
            load(w_in, mine, 0)

        o_ref[...] = _bf(_dot(h_ref[...], wv[s % 2]))

        @pl.when(i == nr - 1)
        def _():
            for k in range(3):
                @pl.when(s == k)
                def _(k=k):
                    ici(k)[1].wait_recv()
                    d2d(k)[0].start()
                    d2d(k)[1].wait_recv()
                    load(w_out, peers[k][1], (k + 1) % 2)

            @pl.when(s == 3)
            def _():
                for k in range(3):
                    ici(k)[0].wait_send()
                    d2d(k)[0].wait_send()
                local, sends, recvs = others()
                for cp in recvs:
                    cp.wait_recv()
                for cp in sends:
                    cp.wait_send()
                for cp in local:
                    cp.wait()

    any_spec = pl.BlockSpec(memory_space=pl.ANY)
    n_any = nl + 2
    outs = _call(
        body, name="in_proj", grid=(N_CHIPS, nr), prefetch=1,
        in_specs=[pl.BlockSpec((tm, d), lambda s, i, order: (i, 0))] + [any_spec] * n_any,
        out_specs=[pl.BlockSpec((tm, width), lambda s, i, order: (i, order[s]))] + [any_spec] * n_any,
        out_shape=[jax.ShapeDtypeStruct((m, w_buf.shape[1]), BF16)] + _gather_shapes([w_buf] + list(later), [small]),
        scratch=[pltpu.VMEM((2, d, width), BF16), _dma_sems(2), _dma_sems(3), _dma_sems(3), _dma_sems(3),
                 _dma_sems(3), _dma_sems(3 * (nl + 1)), _dma_sems(3 * (nl + 1)), _dma_sems(1)],
        aliases={2 + t: 1 + t for t in range(nl + 1)},
    )(order, h1, w_buf, *later, small)
    return outs[0], outs[1], list(outs[2:2 + nl]), outs[2 + nl]


def _exchange(grads, layout):
    n = len(grads)
    others = N_DEV - 1
    sizes = [g.shape[ax] // N_CHIPS for g, (ax, _) in zip(grads, layout)]
    out_shapes = []
    for g, (ax, hx), sz in zip(grads, layout, sizes):
        shape = list(g.shape)
        shape[ax] = sz
        shape[hx] //= 2
        out_shapes.append(jax.ShapeDtypeStruct((others,) + tuple(shape), g.dtype))

    def copies(ins, outs, sems):
        send_sems, recv_sems = sems
        x, y, c = _mesh_pos()
        sends, recvs = [], []
        for t, (src, dst) in enumerate(zip(ins, outs)):
            ax, hx = layout[t]
            for r in range(1, N_DEV):
                px = (1 - x) if r & 4 else x
                py = (1 - y) if r & 2 else y
                pc = (1 - c) if r & 1 else c
                sem = (send_sems.at[others * t + r - 1], recv_sems.at[others * t + r - 1])
                part = _region(src, ax, sizes[t], hx, 2 * px + py, pc)
                sends.append(_remote(part, dst.at[r - 1], *sem, (px, py, pc)))
                recvs.append(_remote(part, dst.at[r - 1], *sem, (px, py, pc)))
        return sends, recvs

    def start(ins, outs, sems):
        for cp in copies(ins, outs, sems)[0]:
            cp.start()

    def finish(ins, outs, sems):
        sends, recvs = copies(ins, outs, sems)
        for cp in recvs:
            cp.wait_recv()
        for cp in sends:
            cp.wait_send()

    return _Comm(grads, out_shapes, [_dma_sems(others * n), _dma_sems(others * n)], start, finish)


def _reduce_half(g, parts, pos, cut, *, name):
    ax, _ = cut
    others = parts.shape[0]
    if g.ndim == 3:
        nb, rows, cols = g.shape
        hb = nb // 2
        block, grid, out_shape = (1, rows // N_CHIPS, cols), (hb,), (nb, rows // N_CHIPS, cols)
        g_map = lambda i, pos: (pos[1] * hb + i, pos[0], 0)
        o_map = lambda i, pos: (pos[1] * hb + i, 0, 0)
        p_map = lambda i, pos: (0, i, 0, 0)
    elif ax == 1:
        rows, cols = g.shape
        tr = _row_tile(rows // 2)
        nt = rows // 2 // tr
        block, grid, out_shape = (tr, cols // N_CHIPS), (nt,), (rows, cols // N_CHIPS)
        g_map = lambda i, pos: (pos[1] * nt + i, pos[0])
        o_map = lambda i, pos: (pos[1] * nt + i, 0)
        p_map = lambda i, pos: (0, i, 0)
    else:
        rows, cols = g.shape
        tr = _row_tile(rows // N_CHIPS // 2)
        nt = rows // N_CHIPS // 2 // tr
        block, grid, out_shape = (tr, cols), (nt,), (rows // N_CHIPS, cols)
        g_map = lambda i, pos: (pos[0] * 2 * nt + pos[1] * nt + i, 0)
        o_map = lambda i, pos: (pos[1] * nt + i, 0)
        p_map = lambda i, pos: (0, i, 0)

    def body(pos_ref, g_ref, p_ref, o_ref):
        acc = g_ref[...].astype(F32)
        for r in range(others):
            acc = acc + p_ref[r].astype(F32)
        o_ref[...] = acc

    return _call(
        body, name=name, grid=grid, prefetch=1,
        in_specs=[pl.BlockSpec(block, g_map), pl.BlockSpec((others,) + block, p_map)],
        out_specs=pl.BlockSpec(block, o_map), out_shape=jax.ShapeDtypeStruct(out_shape, F32),
    )(pos, g, parts)


def _join_halves(bufs):
    return _call(None, name="join_halves", comm=_join_comm(bufs))()[1]


def _join_comm(bufs):
    n = len(bufs)

    def copies(ins, outs, sems):
        send_sems, recv_sems = sems
        x, y, c = _mesh_pos()
        sends = [_remote(_half(src, 0, c), _half(dst, 0, c), send_sems.at[t], recv_sems.at[t], (x, y, 1 - c))
                 for t, (src, dst) in enumerate(zip(ins, outs))]
        recvs = [_remote(_half(src, 0, 1 - c), _half(dst, 0, 1 - c), send_sems.at[t], recv_sems.at[t],
                         (x, y, 1 - c)) for t, (src, dst) in enumerate(zip(ins, outs))]
        return sends, recvs

    def start(ins, outs, sems):
        for cp in copies(ins, outs, sems)[0]:
            cp.start()

    def finish(ins, outs, sems):
        sends, recvs = copies(ins, outs, sems)
        for cp in recvs:
            cp.wait_recv()
        for cp in sends:
            cp.wait_send()

    return _Comm(bufs, [jax.ShapeDtypeStruct(b.shape, b.dtype) for b in bufs],
                 [_dma_sems(n), _dma_sems(n)], start, finish, aliases={i: i for i in range(n)})


def _allreduce_small(pack):
    rows, cols = pack.shape

    def body(p_ref, o_ref, slots, send_sems, recv_sems):
        x, y, c = _mesh_pos()
        me = 4 * x + 2 * y + c
        slots[me] = p_ref[...]
        copies = []
        for r in range(1, N_DEV):
            fx, fy, fc = (r >> 2) & 1, (r >> 1) & 1, r & 1
            dev = ((1 - x) if fx else x, (1 - y) if fy else y, (1 - c) if fc else c)
            cp = pltpu.make_async_remote_copy(
                src_ref=p_ref, dst_ref=slots.at[me], send_sem=send_sems.at[r - 1],
                recv_sem=recv_sems.at[r - 1], device_id=dev, device_id_type=MESH)
            cp.start()
            copies.append(cp)
        for cp in copies:
            cp.wait_recv()
        for cp in copies:
            cp.wait_send()
        acc = slots[0]
        for d in range(1, N_DEV):
            acc = acc + slots[d]
        o_ref[...] = acc

    vmem = pl.BlockSpec(memory_space=pltpu.VMEM)
    return _call(
        body, name="allreduce_small", in_specs=[vmem], out_specs=vmem,
        out_shape=jax.ShapeDtypeStruct((rows, cols), F32),
        scratch=[pltpu.VMEM((N_DEV, rows, cols), F32), pltpu.SemaphoreType.DMA((N_DEV - 1,)),
                 pltpu.SemaphoreType.DMA((N_DEV - 1,))],
    )(pack)


def _pad_rows(a, rows=8):
    return jnp.pad(a, ((0, rows - a.shape[0]), (0, 0)))


def kernel(x, positions, norm1_w, w_in, merge_gate_b, ret_gn_w, w_ret_o, lru_conv_w, lru_conv_b, lru_w_r, lru_b_r, lru_w_i, lru_b_i, lru_lambda, w_lru_o, w_out, norm2_w, ffn_w_up, ffn_conv_w, ffn_conv_b, ffn_w_down, norm_f_w, loss_target, m_norm1_w, m_w_in, m_merge_gate_b, m_ret_gn_w, m_w_ret_o, m_lru_conv_w, m_lru_conv_b, m_lru_w_r, m_lru_b_r, m_lru_w_i, m_lru_b_i, m_lru_lambda, m_w_lru_o, m_w_out, m_norm2_w, m_ffn_w_up, m_ffn_conv_w, m_ffn_conv_b, m_ffn_w_down, m_norm_f_w, v_norm1_w, v_w_in, v_merge_gate_b, v_ret_gn_w, v_w_ret_o, v_lru_conv_w, v_lru_conv_b, v_lru_w_r, v_lru_b_r, v_lru_w_i, v_lru_b_i, v_lru_lambda, v_w_lru_o, v_w_out, v_norm2_w, v_ffn_w_up, v_ffn_conv_w, v_ffn_conv_b, v_ffn_w_down, v_norm_f_w):
    names = ["norm1_w", "w_in", "merge_gate_b", "ret_gn_w", "w_ret_o", "lru_conv_w", "lru_conv_b", "lru_w_r",
             "lru_b_r", "lru_w_i", "lru_b_i", "lru_lambda", "w_lru_o", "w_out", "norm2_w", "ffn_w_up",
             "ffn_conv_w", "ffn_conv_b", "ffn_w_down", "norm_f_w"]
    w_args = dict(zip(names, [norm1_w, w_in, merge_gate_b, ret_gn_w, w_ret_o, lru_conv_w, lru_conv_b, lru_w_r,
                              lru_b_r, lru_w_i, lru_b_i, lru_lambda, w_lru_o, w_out, norm2_w, ffn_w_up,
                              ffn_conv_w, ffn_conv_b, ffn_w_down, norm_f_w]))
    m_args = dict(zip(names, [m_norm1_w, m_w_in, m_merge_gate_b, m_ret_gn_w, m_w_ret_o, m_lru_conv_w,
                              m_lru_conv_b, m_lru_w_r, m_lru_b_r, m_lru_w_i, m_lru_b_i, m_lru_lambda, m_w_lru_o,
                              m_w_out, m_norm2_w, m_ffn_w_up, m_ffn_conv_w, m_ffn_conv_b, m_ffn_w_down,
                              m_norm_f_w]))
    v_args = dict(zip(names, [v_norm1_w, v_w_in, v_merge_gate_b, v_ret_gn_w, v_w_ret_o, v_lru_conv_w,
                              v_lru_conv_b, v_lru_w_r, v_lru_b_r, v_lru_w_i, v_lru_b_i, v_lru_lambda, v_w_lru_o,
                              v_w_out, v_norm2_w, v_ffn_w_up, v_ffn_conv_w, v_ffn_conv_b, v_ffn_w_down,
                              v_norm_f_w]))

    bsz, seq, d = x.shape
    m = bsz * seq
    tm = min(MM_ROWS, m)
    tm_fused = min(FUSED_ROWS, m)
    tm_tall = min(TALL_ROWS, m)
    chip = 2 * lax.axis_index("x") + lax.axis_index("y")

    big = ["w_in", "w_ret_o", "w_lru_o", "w_out", "lru_w_r", "lru_w_i", "ffn_w_up", "ffn_w_down"]
    cut = dict(w_in=(1, 0), w_ret_o=(0, 0), w_lru_o=(0, 0), w_out=(0, 0), lru_w_r=(1, 0), lru_w_i=(1, 0),
               ffn_w_up=(1, 0), ffn_w_down=(0, 0))
    core = lax.axis_index("c")
    chip1 = jnp.reshape(chip, (1,)).astype(jnp.int32)
    pos = jnp.stack([chip, core]).astype(jnp.int32)
    placed = {n: _place_shard(w_args[n][0], chip1, cut[n][0], name="place_" + n) for n in big}
    small_pack = jnp.concatenate([
        jnp.pad(merge_gate_b[0], ((0, 6), (0, 512))),
        jnp.pad(lru_conv_w[0], ((0, 4), (0, 512))),
        jnp.pad(lru_b_r[0], ((0, 4), (0, 704))),
        jnp.pad(lru_b_i[0], ((0, 4), (0, 704))),
        jnp.pad(ffn_conv_w[0], ((0, 5), (0, 0))),
    ], axis=0)
    x2d = x.reshape(m, d)
    mx, my = lax.axis_index("x"), lax.axis_index("y")
    order = jnp.stack([chip, 2 * (1 - mx) + my, 2 * mx + 1 - my, 2 * (1 - mx) + 1 - my]).astype(jnp.int32)
    mixer = ["w_ret_o", "w_lru_o", "w_out", "lru_w_r", "lru_w_i"]
    cuts = lambda ns: [cut[n] for n in ns]
    sizes = lambda ns: [w_args[n].shape[1 + cut[n][0]] for n in ns]
    h1 = _norm_bf16(x2d, norm1_w, name="norm1", tm=tm)
    proj, w_in_full, bufs, sp = _in_proj_gather(h1, placed["w_in"], [placed[n] for n in mixer], cuts(mixer),
                                               small_pack, order, tm=tm_tall)
    wb = {"w_in": w_in_full}
    mgb = jnp.transpose(sp[:, 0:2, 0:256], (1, 0, 2)).reshape(2, D_MODEL)
    lcw = jnp.transpose(sp[:, 8:12, 0:256], (1, 0, 2)).reshape(4, D_MODEL)
    lbr = jnp.transpose(sp[:, 16:20, 0:64], (1, 0, 2)).reshape(1, D_MODEL)
    lbi = jnp.transpose(sp[:, 24:28, 0:64], (1, 0, 2)).reshape(1, D_MODEL)
    fcw = jnp.transpose(sp[:, 32:35, :], (1, 0, 2)).reshape(3, D_FF)
    nfw = norm_f_w.reshape(1, D_MODEL)

    half = RET_DK // 2
    inv_freq = ROPE_BASE ** (-jnp.arange(half, dtype=F32) / half)
    cos, sin = _rope_tables(positions.reshape(bsz, seq, 1), jnp.concatenate([inv_freq, inv_freq]).reshape(1, RET_DK))
    proj3 = proj.reshape(bsz, seq, D_IN)
    down, up_w = ["ffn_w_down"], ["ffn_w_up"]
    (a_in3, states), bufs = _retention_fwd(
        proj3, cos, sin, ret_gn_w,
        comm=_both(_gather_d2d(bufs, cuts(mixer), sizes(mixer)), _gather_ici([placed["ffn_w_down"]], cuts(down))))
    wb.update(zip(mixer, bufs[:len(mixer)]))
    lru_params = (lcw, lru_conv_b, wb["lru_w_r"], lbr, wb["lru_w_i"], lbi, lru_lambda)
    (b_in3, *lru_kept), (up_buf, wb["ffn_w_down"]) = _lru_fwd(
        proj3, lru_params,
        comm=_both(_gather_ici([placed["ffn_w_up"]], cuts(up_w)), _gather_d2d(bufs[len(mixer):], cuts(down), sizes(down))))
    a_in, b_in = a_in3.reshape(m, d), b_in3.reshape(m, d)
    (x2, mix, ya, yb), (wb["ffn_w_up"],) = _merge_fwd(
        a_in, b_in, proj, mgb, wb["w_ret_o"], wb["w_lru_o"], wb["w_out"], x2d, tm=tm_fused,
        comm=_gather_d2d([up_buf], cuts(up_w), sizes(up_w)))
    up, h2 = _norm_matmul(x2, norm2_w, wb["ffn_w_up"], name="ffn_up", tm=tm_tall, tn=TALL_COLS)
    up3 = up.reshape(bsz, seq, 2 * D_FF)
    f3, act3, q3 = _ffn_act_fwd(up3, fcw, ffn_conv_b)
    f = f3.reshape(m, D_FF)
    dx3, sm_nf = _ffn_down_loss(f, wb["ffn_w_down"], x2, nfw, loss_target.reshape(m, d), tm=tm)

    def send(*ns):
        return _exchange([g_full[n] for n in ns], [cut[n] for n in ns])

    g_full, parts = {}, {}
    df = _mm_nt(dx3, wb["ffn_w_down"], name="ffn_down_dx", tm=tm_tall, out_dtype=BF16)
    g_full["ffn_w_down"] = _mm_tn(f, [dx3], name="ffn_down_dw", tm=tm_tall)
    (dgate3, dval3, sm_ffn), (parts["ffn_w_down"],) = _ffn_act_bwd(
        up3, act3, q3, fcw, df.reshape(bsz, seq, D_FF), comm=send("ffn_w_down"))
    dup = [dgate3.reshape(m, D_FF), dval3.reshape(m, D_FF)]
    g_full["ffn_w_up"] = _mm_tn(h2, dup, name="ffn_up_dw", tm=tm_tall)
    (dx2, sm_n2), (parts["ffn_w_up"],) = _mm_nt_normbwd(
        dup, wb["ffn_w_up"], x2, norm2_w, dx3, name="ffn_up_dx", tm=tm, row=ROW_N2, comm=send("ffn_w_up"))
    dya, dyb, da_in, db_in, dgates, sm_mg = _merge_bwd(
        ya, yb, proj, mgb, wb["w_ret_o"], wb["w_lru_o"], wb["w_out"], dx2, tm=tm_fused)
    g_full["w_out"] = _mm_tn(mix, [dx2], name="out_dw", tm=tm_tall)
    g_full["w_ret_o"] = _mm_tn(a_in, [dya], name="ret_o_dw", tm=tm_tall)
    g_full["w_lru_o"] = _mm_tn(b_in, [dyb], name="lru_o_dw", tm=tm_tall)
    (dlru3, dwr, dwi, sm_lru), (parts["w_out"], parts["w_ret_o"], parts["w_lru_o"]) = _lru_bwd(
        proj3, lru_params, lru_kept, db_in.reshape(bsz, seq, d), comm=send("w_out", "w_ret_o", "w_lru_o"))
    g_full["lru_w_r"], g_full["lru_w_i"] = dwr.astype(BF16), dwi.astype(BF16)
    (dret3, sm_gn), (parts["lru_w_r"], parts["lru_w_i"]) = _retention_bwd(
        proj3, cos, sin, ret_gn_w, states, da_in.reshape(bsz, seq, d), comm=send("lru_w_r", "lru_w_i"))
    dproj = [dret3.reshape(m, 3072), dlru3.reshape(m, 2048), dgates]
    g_full["w_in"] = _mm_tn(h1, dproj, name="in_proj_dw", tm=tm_tall)
    half_sum = lambda n: _reduce_half(g_full[n], parts[n], pos, cut[n], name="sum_" + n)
    (grad_x, sm_n1), (parts["w_in"], *joined) = _mm_nt_normbwd(
        dproj, wb["w_in"], x2d, norm1_w, dx2, name="in_proj_dx", tm=tm, row=ROW_N1,
        comm=_both(send("w_in"), _join_comm([half_sum(n) for n in big[1:]])))
    reduced = _join_halves([half_sum("w_in")]) + joined
    misc = sm_n1 + sm_mg + sm_gn + sm_n2 + sm_nf
    pack = jnp.concatenate(
        [misc, sm_lru, sm_ffn[:, 0:1024], sm_ffn[:, 1024:2048], sm_ffn[:, 2048:3072]], axis=0)
    tot = _allreduce_small(pack)
    ffn_sm = jnp.concatenate([tot[16:24], tot[24:32], tot[32:40]], axis=1)
    g_small = {
        "norm1_w": tot[ROW_N1:ROW_N1 + 1], "merge_gate_b": tot[ROW_MGB:ROW_MGB + 2],
        "ret_gn_w": tot[ROW_GN:ROW_GN + 1], "norm2_w": tot[ROW_N2:ROW_N2 + 1], "norm_f_w": tot[ROW_NF:ROW_NF + 1],
        "lru_conv_w": tot[8:12], "lru_conv_b": tot[12:13], "lru_b_r": tot[13:14].reshape(4, 256),
        "lru_b_i": tot[14:15].reshape(4, 256), "lru_lambda": tot[15:16],
        "ffn_conv_w": ffn_sm[0:3], "ffn_conv_b": ffn_sm[3:4],
    }
    small_shard = dict(merge_gate_b=256, lru_conv_w=256, lru_b_r=64, lru_b_i=64, ffn_conv_w=768)

    outs = {}
    for n, g in zip(big, reduced):
        shape = w_args[n].shape
        g = g.reshape(-1, g.shape[-1])
        outs[n] = [o.reshape(shape) for o in _adamw(
            w_args[n].reshape(g.shape), [g], m_args[n].reshape(g.shape), v_args[n].reshape(g.shape),
            name="adamw_" + n)]
    for n, g in g_small.items():
        shape = w_args[n].shape
        if n in small_shard:
            g = lax.dynamic_slice_in_dim(g, chip * small_shard[n], small_shard[n], axis=1)
        w2 = w_args[n].reshape(g.shape)
        outs[n] = [o.reshape(shape) for o in _adamw(
            w2, [g], m_args[n].reshape(g.shape), v_args[n].reshape(g.shape), name="adamw_" + n)]

    result = [tot[ROW_LOSS, 0], grad_x.reshape(bsz, seq, d)]
    for k in range(4):
        result += [outs[n][k] for n in names]
    return tuple(result)
```

```python
import functools
import math

import numpy as np
import jax
import jax.numpy as jnp
from jax import lax
from jax.experimental import pallas as pl
from jax.experimental.pallas import tpu as pltpu

F32 = jnp.float32
BF16 = jnp.bfloat16

D_MODEL = 1024
RET_HEADS = 4
RET_DK = 128
RET_DV = 256
LRU_BLOCKS = 4
LRU_BLOCK = 256
LRU_C = 8.0
D_FF = 3072
D_IN = 7168
ROPE_BASE = 10000.0
RMS_EPS = 1e-6
GN_EPS = 1e-6
ADAM_LR, ADAM_B1, ADAM_B2, ADAM_EPS, ADAM_WD, ADAM_STEP = 0.001, 0.9, 0.999, 1e-08, 0.01, 10

N_CHIPS = 4
N_DEV = 8
SEQ_T = 256
REF_CHUNK = 64
COL = 1024
MM_ROWS = 1024
TALL_ROWS, TALL_COLS = 2048, 1024
FUSED_ROWS = 512
VMEM_LIMIT_BYTES = 62 * 1024 * 1024
MESH = pl.DeviceIdType.MESH
ROW_N1, ROW_MGB, ROW_GN, ROW_N2, ROW_NF, ROW_LOSS = 0, 1, 3, 4, 5, 6
GELU_K = math.sqrt(2.0 / math.pi)
GELU_C = 0.044715


class _Comm:
    def __init__(self, ins, outs, sems, start, finish, aliases=None):
        self.ins, self.outs, self.sems = list(ins), list(outs), list(sems)
        self.start, self.finish, self.aliases = start, finish, dict(aliases or {})


def _call(body, *, name, out_shape=(), grid=None, in_specs=(), out_specs=(), scratch=(), comm=None, prefetch=0,
          aliases=None):
    single = not isinstance(out_shape, (list, tuple))
    out_shape = [out_shape] if single else list(out_shape)
    out_specs = [out_specs] if single else list(out_specs)
    in_specs, scratch = list(in_specs), list(scratch)
    n_in, n_out, n_scr = len(in_specs), len(out_shape), len(scratch)
    kwargs = dict(name=name, compiler_params=pltpu.CompilerParams(vmem_limit_bytes=VMEM_LIMIT_BYTES))
    if prefetch:
        assert comm is None
        spec = pltpu.PrefetchScalarGridSpec(num_scalar_prefetch=prefetch, grid=grid, in_specs=in_specs,
                                            out_specs=out_specs, scratch_shapes=scratch)
        fn = pl.pallas_call(body, out_shape=out_shape, grid_spec=spec, input_output_aliases=dict(aliases or {}),
                            **kwargs)
        return (lambda *args: fn(*args)[0]) if single else fn
    if grid is not None:
        kwargs["grid"] = grid
    if comm is None:
        fn = pl.pallas_call(body, out_shape=out_shape, in_specs=in_specs, out_specs=out_specs,
                            scratch_shapes=scratch, **kwargs)
        return (lambda *args: fn(*args)[0]) if single else fn

    any_spec = pl.BlockSpec(memory_space=pl.ANY)
    n_cin, n_cout = len(comm.ins), len(comm.outs)

    def wrapped(*refs):
        ins, refs = refs[:n_in], refs[n_in:]
        cins, refs = refs[:n_cin], refs[n_cin:]
        outs, refs = refs[:n_out], refs[n_out:]
        couts, refs = refs[:n_cout], refs[n_cout:]
        scr, csems = refs[:n_scr], refs[n_scr:]
        if grid is None:
            comm.start(cins, couts, csems)
            comm.finish(cins, couts, csems)
            return
        ids = [pl.program_id(a) for a in range(len(grid))]
        first = functools.reduce(jnp.logical_and, [i == 0 for i in ids])
        last = functools.reduce(jnp.logical_and, [i == g - 1 for i, g in zip(ids, grid)])
        pl.when(first)(lambda: comm.start(cins, couts, csems))
        body(*ins, *outs, *scr)
        pl.when(last)(lambda: comm.finish(cins, couts, csems))

    fn = pl.pallas_call(
        wrapped, out_shape=out_shape + comm.outs, in_specs=in_specs + [any_spec] * n_cin,
        out_specs=out_specs + [any_spec] * n_cout, scratch_shapes=scratch + comm.sems,
        input_output_aliases={n_in + i: n_out + o for i, o in comm.aliases.items()}, **kwargs)

    def run(*args):
        res = fn(*args, *comm.ins)
        own = res[0] if single else list(res[:n_out])
        return own, list(res[n_out:])

    return run


def _dot(a, b):
    return jnp.dot(a, b, preferred_element_type=F32)


def _dot_nt(a, b):
    return lax.dot_general(a, b, (((1,), (1,)), ((), ())), preferred_element_type=F32)


def _dot_tn(a, b):
    return lax.dot_general(a, b, (((0,), (0,)), ((), ())), preferred_element_type=F32)


def _bf(x):
    return x.astype(BF16)


def _sigmoid(x):
    return 1.0 / (1.0 + jnp.exp(-x))


def _gelu_and_grad(x):
    x2 = x * x
    s = _sigmoid(x * (2.0 * GELU_K * GELU_C * x2 + 2.0 * GELU_K))
    g = x * s
    dg = s + g * (1.0 - s) * (6.0 * GELU_K * GELU_C * x2 + 2.0 * GELU_K)
    return g, dg


def _rms(x):
    r = lax.rsqrt(jnp.mean(x * x, axis=-1, keepdims=True) + RMS_EPS)
    return x * r, r


def _rms_bwd(dy, x, nw):
    xh, r = _rms(x)
    g = dy * nw
    dx = r * (g - xh * jnp.mean(g * xh, axis=-1, keepdims=True))
    return dx, jnp.sum(dy * xh, axis=0, keepdims=True)


def _row_acc(ref, row, val):
    ref[row:row + 1, :] = ref[row:row + 1, :] + val


def _shift_down(x, j, prev8):
    if j == 0:
        return x
    n = x.shape[0] // 8
    row = lax.broadcasted_iota(jnp.int32, prev8.shape, 0)
    turned = [pltpu.roll(prev8, j, 0)] + [pltpu.roll(x[8 * k:8 * k + 8], j, 0) for k in range(n)]
    return jnp.concatenate([jnp.where(row < j, turned[k], turned[k + 1]) for k in range(n)], axis=0)


def _shift_up(x, j, next8):
    if j == 0:
        return x
    n = x.shape[0] // 8
    row = lax.broadcasted_iota(jnp.int32, next8.shape, 0)
    turned = [pltpu.roll(x[8 * k:8 * k + 8], 8 - j, 0) for k in range(n)] + [pltpu.roll(next8, 8 - j, 0)]
    return jnp.concatenate([jnp.where(row >= 8 - j, turned[k + 1], turned[k]) for k in range(n)], axis=0)


def _scan_fwd(a, b, carry):
    row = lax.broadcasted_iota(jnp.int32, (8, a.shape[1]), 0)
    out = []
    for k in range(a.shape[0] // 8):
        ak, bk = a[8 * k:8 * k + 8], b[8 * k:8 * k + 8]
        for s in (1, 2, 4):
            keep = row >= s
            ar, br = pltpu.roll(ak, s, 0), pltpu.roll(bk, s, 0)
            bk = jnp.where(keep, ak * br + bk, bk)
            ak = jnp.where(keep, ak * ar, ak)
        hk = ak * carry + bk
        carry = hk[7:8]
        out.append(hk)
    return jnp.concatenate(out, axis=0)


def _scan_bwd(a, b, carry):
    row = lax.broadcasted_iota(jnp.int32, (8, a.shape[1]), 0)
    out = []
    for k in reversed(range(a.shape[0] // 8)):
        ak, bk = a[8 * k:8 * k + 8], b[8 * k:8 * k + 8]
        for s in (1, 2, 4):
            keep = row < 8 - s
            ar, br = pltpu.roll(ak, 8 - s, 0), pltpu.roll(bk, 8 - s, 0)
            bk = jnp.where(keep, ak * br + bk, bk)
            ak = jnp.where(keep, ak * ar, ak)
        gk = bk + ak * carry
        carry = gk[0:1]
        out.append(gk)
    return jnp.concatenate(out[::-1], axis=0)


def _norm_matmul(x, nw, w, *, name, tm, tn):
    m, d = x.shape
    n = w.shape[1]

    def body(x_ref, nw_ref, w_ref, o_ref, h_ref, h_sc):
        @pl.when(pl.program_id(1) == 0)
        def _():
            xh, _ = _rms(x_ref[...])
            h = _bf(xh * nw_ref[...])
            h_sc[...] = h
            h_ref[...] = h

        o_ref[...] = _bf(_dot(h_sc[...], w_ref[...]))

    return _call(
        body, name=name, grid=(m // tm, n // tn),
        in_specs=[pl.BlockSpec((tm, d), lambda i, j: (i, 0)),
                  pl.BlockSpec((1, d), lambda i, j: (0, 0)),
                  pl.BlockSpec((d, tn), lambda i, j: (0, j))],
        out_specs=[pl.BlockSpec((tm, tn), lambda i, j: (i, j)),
                   pl.BlockSpec((tm, d), lambda i, j: (i, 0))],
        out_shape=[jax.ShapeDtypeStruct((m, n), BF16), jax.ShapeDtypeStruct((m, d), BF16)],
        scratch=[pltpu.VMEM((tm, d), BF16)],
    )(x, nw, w)


def _mm_nt(a, w, *, name, tm, out_dtype):
    m, k = a.shape
    n = w.shape[0]

    def body(a_ref, w_ref, o_ref):
        o_ref[...] = _dot_nt(_bf(a_ref[...]), w_ref[...]).astype(out_dtype)

    return _call(
        body, name=name, grid=(m // tm, n // COL),
        in_specs=[pl.BlockSpec((tm, k), lambda i, j: (i, 0)),
                  pl.BlockSpec((COL, k), lambda i, j: (j, 0))],
        out_specs=pl.BlockSpec((tm, COL), lambda i, j: (i, j)),
        out_shape=jax.ShapeDtypeStruct((m, n), out_dtype),
    )(a, w)


def _piece_layout(pieces):
    offs, nblk, o = [], [], 0
    for p in pieces:
        offs.append(o)
        nblk.append(p.shape[1] // COL)
        o += p.shape[1] // COL
    return offs, nblk, o


def _mm_tn(a, pieces, *, name, tm, out_dtype=BF16):
    m, k = a.shape
    offs, nblk, nn = _piece_layout(pieces)

    def piece_spec(o, nb):
        def idx(ki, nj, mi):
            use = jnp.logical_and(nj >= o, nj < o + nb)
            return (jnp.where(use, mi, 0), jnp.clip(nj - o, 0, nb - 1))
        return pl.BlockSpec((tm, COL), idx)

    def body(a_ref, *rest):
        p_refs, o_ref, acc = rest[:len(pieces)], rest[len(pieces)], rest[len(pieces) + 1]
        nj, mi = pl.program_id(1), pl.program_id(2)

        @pl.when(mi == 0)
        def _():
            acc[...] = jnp.zeros_like(acc)

        for p_ref, o, nb in zip(p_refs, offs, nblk):
            @pl.when(jnp.logical_and(nj >= o, nj < o + nb))
            def _(p_ref=p_ref):
                acc[...] += _dot_tn(_bf(a_ref[...]), _bf(p_ref[...]))

        @pl.when(mi == pl.num_programs(2) - 1)
        def _():
            o_ref[...] = acc[...].astype(out_dtype)

    return _call(
        body, name=name, grid=(k // COL, nn, m // tm),
        in_specs=[pl.BlockSpec((tm, COL), lambda ki, nj, mi: (mi, ki))]
        + [piece_spec(o, nb) for o, nb in zip(offs, nblk)],
        out_specs=pl.BlockSpec((COL, COL), lambda ki, nj, mi: (ki, nj)),
        out_shape=jax.ShapeDtypeStruct((k, nn * COL), out_dtype),
        scratch=[pltpu.VMEM((COL, COL), F32)],
    )(a, *pieces)


def _mm_nt_normbwd(pieces, w, x, nw, dres, *, name, tm, row, comm=None):
    m, d = x.shape
    offs, nblk, nk = _piece_layout(pieces)

    def piece_spec(o, nb):
        return pl.BlockSpec((tm, COL), lambda i, k: (i, jnp.clip(k - o, 0, nb - 1)))

    def body(*refs):
        p_refs = refs[:len(pieces)]
        w_ref, x_ref, nw_ref, dres_ref, dx_ref, dnw_ref, acc = refs[len(pieces):]
        i, k = pl.program_id(0), pl.program_id(1)

        @pl.when(jnp.logical_and(i == 0, k == 0))
        def _():
            dnw_ref[...] = jnp.zeros_like(dnw_ref)

        @pl.when(k == 0)
        def _():
            acc[...] = jnp.zeros_like(acc)

        for p_ref, o, nb in zip(p_refs, offs, nblk):
            @pl.when(jnp.logical_and(k >= o, k < o + nb))
            def _(p_ref=p_ref):
                acc[...] += _dot_nt(_bf(p_ref[...]), w_ref[...])

        @pl.when(k == nk - 1)
        def _():
            dx, dnw = _rms_bwd(acc[...], x_ref[...], nw_ref[...])
            dx_ref[...] = dres_ref[...] + dx
            _row_acc(dnw_ref, row, dnw)

    return _call(
        body, name=name, grid=(m // tm, nk), comm=comm,
        in_specs=[piece_spec(o, nb) for o, nb in zip(offs, nblk)]
        + [pl.BlockSpec((d, COL), lambda i, k: (0, k)),
           pl.BlockSpec((tm, d), lambda i, k: (i, 0)),
           pl.BlockSpec((1, d), lambda i, k: (0, 0)),
           pl.BlockSpec((tm, d), lambda i, k: (i, 0))],
        out_specs=[pl.BlockSpec((tm, d), lambda i, k: (i, 0)),
                   pl.BlockSpec((8, d), lambda i, k: (0, 0))],
        out_shape=[jax.ShapeDtypeStruct((m, d), F32), jax.ShapeDtypeStruct((8, d), F32)],
        scratch=[pltpu.VMEM((tm, d), F32)],
    )(*pieces, w, x, nw, dres)


def _rope_tables(pos3, invf):
    b, s, _ = pos3.shape

    def body(pos_ref, invf_ref, cos_ref, sin_ref):
        half_t, half_d = s // 2, RET_DK // 2
        pos = pos_ref[...].astype(F32)
        low = lax.broadcasted_iota(jnp.int32, (half_t, RET_DK), 1) < half_d
        ang = jnp.where(low, pos[0:half_t], pos[half_t:]) * invf_ref[...]
        co, si = jnp.cos(ang), jnp.sin(ang)
        co_turned, si_turned = pltpu.roll(co, half_d, 1), pltpu.roll(si, half_d, 1)
        sign = jnp.where(low, -1.0, 1.0)
        cos_ref[...] = jnp.concatenate([jnp.where(low, co, co_turned), jnp.where(low, co_turned, co)], axis=0)
        sin_ref[...] = jnp.concatenate([sign * jnp.where(low, si, si_turned), sign * jnp.where(low, si_turned, si)],
                                       axis=0)

    spec = pl.BlockSpec((None, s, RET_DK), lambda i: (i, 0, 0))
    return _call(
        body, name="rope_tables", grid=(b,),
        in_specs=[pl.BlockSpec((None, s, 1), lambda i: (i, 0, 0)),
                  pl.BlockSpec((1, RET_DK), lambda i: (0, 0))],
        out_specs=[spec, spec],
        out_shape=[jax.ShapeDtypeStruct((b, s, RET_DK), F32)] * 2,
    )(pos3, invf)


def _log_gamma(h):
    return float(np.log1p(-np.power(np.float32(2.0), np.float32(-5.0 - h))).astype(np.float32))


def _decay_matrix(h):
    lg = _log_gamma(h)
    n = lax.broadcasted_iota(jnp.int32, (SEQ_T, SEQ_T), 0)
    m = lax.broadcasted_iota(jnp.int32, (SEQ_T, SEQ_T), 1)
    same = (n // REF_CHUNK) == (m // REF_CHUNK)
    dist = jnp.where(same, jnp.abs(n - m), n - m).astype(F32)
    return jnp.where(jnp.logical_or(same, m < n), jnp.exp(lg * dist), 0.0)


def _decay_vectors(h):
    lg = _log_gamma(h)
    idx = lax.broadcasted_iota(jnp.int32, (SEQ_T, 1), 0).astype(F32)
    qd = jnp.exp(lg * (idx + 1.0))
    kd = jnp.exp(lg * (SEQ_T - 1.0 - idx))
    return qd, kd, math.exp(lg * SEQ_T)


def _rotate(x, cos, sin):
    return x * cos + pltpu.roll(x, RET_DK // 2, 1) * sin


def _rotate_bwd(d, cos, sin):
    return d * cos + pltpu.roll(d * sin, RET_DK // 2, 1)


def _ret_head(p_ref, cos, sin, h):
    q = p_ref[:, h * RET_DK:(h + 1) * RET_DK].astype(F32)
    k = p_ref[:, 512 + h * RET_DK:512 + (h + 1) * RET_DK].astype(F32)
    v = p_ref[:, 1024 + h * RET_DV:1024 + (h + 1) * RET_DV]
    g = p_ref[:, 2048 + h * RET_DV:2048 + (h + 1) * RET_DV].astype(F32)
    qr = _rotate(q, cos, sin)
    kr = _rotate(k, cos, sin) * (RET_DK ** -0.5)
    return qr, kr, v, g


def _group_norm(o):
    mu = jnp.mean(o, axis=-1, keepdims=True)
    oc = o - mu
    rstd = lax.rsqrt(jnp.mean(oc * oc, axis=-1, keepdims=True) + GN_EPS)
    return oc * rstd, rstd


def _retention_fwd(proj3, cos, sin, gnw, comm=None):
    b, s, _ = proj3.shape
    nc = s // SEQ_T

    def body(p_ref, cos_ref, sin_ref, gnw_ref, a_ref, st_ref, state, wtab):
        c = pl.program_id(1)

        @pl.when(jnp.logical_and(pl.program_id(0) == 0, c == 0))
        def _():
            for h in range(RET_HEADS):
                wtab[h] = _decay_matrix(h)

        @pl.when(c == 0)
        def _():
            state[...] = jnp.zeros_like(state)

        cs, sn = cos_ref[...], sin_ref[...]
        st_ref[...] = state[...]
        outs, states = [], []
        for h in range(RET_HEADS):
            qd, kd, gt = _decay_vectors(h)
            qr, kr, v, g = _ret_head(p_ref, cs, sn, h)
            st = state[h]
            p = _dot_nt(_bf(qr), _bf(kr)) * wtab[h]
            o = _dot(_bf(p), _bf(v)) + _dot(_bf(qr * qd), _bf(st))
            states.append(st * gt + _dot_tn(_bf(kr * kd), _bf(v)))
            on, _ = _group_norm(o)
            gw = gnw_ref[:, h * RET_DV:(h + 1) * RET_DV]
            outs.append(_bf(on * gw * (g * _sigmoid(g))))
        a_ref[...] = jnp.concatenate(outs, axis=1)
        state[...] = jnp.stack(states)

    tab = pl.BlockSpec((None, SEQ_T, RET_DK), lambda i, c: (i, c, 0))
    return _call(
        body, name="retention_fwd", grid=(b, nc), comm=comm,
        in_specs=[pl.BlockSpec((None, SEQ_T, 3072), lambda i, c: (i, c, 0)), tab, tab,
                  pl.BlockSpec((1, D_MODEL), lambda i, c: (0, 0))],
        out_specs=[pl.BlockSpec((None, SEQ_T, D_MODEL), lambda i, c: (i, c, 0)),
                   pl.BlockSpec((None, None, RET_HEADS, RET_DK, RET_DV), lambda i, c: (i, c, 0, 0, 0))],
        out_shape=[jax.ShapeDtypeStruct((b, s, D_MODEL), BF16),
                   jax.ShapeDtypeStruct((b, nc, RET_HEADS, RET_DK, RET_DV), F32)],
        scratch=[pltpu.VMEM((RET_HEADS, RET_DK, RET_DV), F32),
                 pltpu.VMEM((RET_HEADS, SEQ_T, SEQ_T), F32)],
    )(proj3, cos, sin, gnw)


def _retention_bwd(proj3, cos, sin, gnw, states, da3, comm=None):
    b, s, _ = proj3.shape
    nc = s // SEQ_T

    def body(p_ref, cos_ref, sin_ref, gnw_ref, st_ref, da_ref, d_ref, dgn_ref, dstate, wtab):
        c = pl.program_id(1)

        @pl.when(jnp.logical_and(pl.program_id(0) == 0, c == 0))
        def _():
            dgn_ref[...] = jnp.zeros_like(dgn_ref)
            for h in range(RET_HEADS):
                wtab[h] = _decay_matrix(h)

        @pl.when(c == 0)
        def _():
            dstate[...] = jnp.zeros_like(dstate)

        cs, sn = cos_ref[...], sin_ref[...]
        dqs, dks, dvs, dgs, dgns, dstates = [], [], [], [], [], []
        for h in range(RET_HEADS):
            qd, kd, gt = _decay_vectors(h)
            qr, kr, v, g = _ret_head(p_ref, cs, sn, h)
            st, dst, w = st_ref[h], dstate[h], wtab[h]
            qb, kb, vb = _bf(qr), _bf(kr), _bf(v)
            p = _dot_nt(qb, kb) * w
            o = _dot(_bf(p), vb) + _dot(_bf(qr * qd), _bf(st))
            on, rstd = _group_norm(o)
            gw = gnw_ref[:, h * RET_DV:(h + 1) * RET_DV]
            da = da_ref[:, h * RET_DV:(h + 1) * RET_DV].astype(F32)
            sg = _sigmoid(g)
            silu = g * sg
            dg = da * on * gw * (sg * (1.0 + g * (1.0 - sg)))
            dgns.append(jnp.sum(da * silu * on, axis=0, keepdims=True))
            don = da * silu * gw
            do = rstd * (don - jnp.mean(don, axis=-1, keepdims=True)
                         - on * jnp.mean(don * on, axis=-1, keepdims=True))
            dob = _bf(do)
            dp = _dot_nt(dob, vb) * w
            dqr = _dot(_bf(dp), kb) + _dot_nt(dob, _bf(st)) * qd
            dkr = _dot_tn(_bf(dp), qb) + _dot_nt(vb, _bf(dst)) * kd
            dv = _dot_tn(_bf(p), dob) + _dot(_bf(kr * kd), _bf(dst))
            dstates.append(dst * gt + _dot_tn(_bf(qr * qd), dob))
            dqs.append(_bf(_rotate_bwd(dqr, cs, sn)))
            dks.append(_bf(_rotate_bwd(dkr, cs, sn) * (RET_DK ** -0.5)))
            dvs.append(_bf(dv))
            dgs.append(_bf(dg))
        d_ref[...] = jnp.concatenate(dqs + dks + dvs + dgs, axis=1)
        _row_acc(dgn_ref, ROW_GN, jnp.concatenate(dgns, axis=1))
        dstate[...] = jnp.stack(dstates)

    rev = lambda i, c: (i, nc - 1 - c, 0)
    tab = pl.BlockSpec((None, SEQ_T, RET_DK), rev)
    return _call(
        body, name="retention_bwd", grid=(b, nc), comm=comm,
        in_specs=[pl.BlockSpec((None, SEQ_T, 3072), rev), tab, tab,
                  pl.BlockSpec((1, D_MODEL), lambda i, c: (0, 0)),
                  pl.BlockSpec((None, None, RET_HEADS, RET_DK, RET_DV), lambda i, c: (i, nc - 1 - c, 0, 0, 0)),
                  pl.BlockSpec((None, SEQ_T, D_MODEL), rev)],
        out_specs=[pl.BlockSpec((None, SEQ_T, 3072), rev),
                   pl.BlockSpec((8, D_MODEL), lambda i, c: (0, 0))],
        out_shape=[jax.ShapeDtypeStruct((b, s, 3072), BF16), jax.ShapeDtypeStruct((8, D_MODEL), F32)],
        scratch=[pltpu.VMEM((RET_HEADS, RET_DK, RET_DV), F32),
                 pltpu.VMEM((RET_HEADS, SEQ_T, SEQ_T), F32)],
    )(proj3, cos, sin, gnw, states, da3)


def _softplus_neg(lam):
    z = -lam
    u = jnp.exp(-jnp.abs(z))
    log1p_u = jnp.where(u < 0.01, u * (1.0 - u * (0.5 - u * (1.0 / 3.0))), jnp.log(1.0 + u))
    return jnp.maximum(z, 0.0) + log1p_u


def _lru_coeffs(xc, wr_ref, br_ref, wi_ref, bi_ref, lam_ref):
    rs, is_ = [], []
    for n in range(LRU_BLOCKS):
        xb = _bf(xc[:, n * LRU_BLOCK:(n + 1) * LRU_BLOCK])
        cols = slice(n * LRU_BLOCK, (n + 1) * LRU_BLOCK)
        rs.append(_sigmoid(_dot(xb, wr_ref[n]) + br_ref[:, cols]))
        is_.append(_sigmoid(_dot(xb, wi_ref[n]) + bi_ref[:, cols]))
    r = jnp.concatenate(rs, axis=1)
    i = jnp.concatenate(is_, axis=1)
    sp = _softplus_neg(lam_ref[...])
    la = -LRU_C * r * sp
    a = jnp.exp(la)
    s = jnp.sqrt(-jnp.tanh(la) * (a * a + 1.0))
    return r, i, a, s, sp


_LRU_PARAM_SPECS = [
    pl.BlockSpec((4, D_MODEL), lambda i, c: (0, 0)),
    pl.BlockSpec((1, D_MODEL), lambda i, c: (0, 0)),
    pl.BlockSpec((LRU_BLOCKS, LRU_BLOCK, LRU_BLOCK), lambda i, c: (0, 0, 0)),
    pl.BlockSpec((1, D_MODEL), lambda i, c: (0, 0)),
    pl.BlockSpec((LRU_BLOCKS, LRU_BLOCK, LRU_BLOCK), lambda i, c: (0, 0, 0)),
    pl.BlockSpec((1, D_MODEL), lambda i, c: (0, 0)),
    pl.BlockSpec((1, D_MODEL), lambda i, c: (0, 0)),
]


def _lru_fwd(proj3, params, comm=None):
    b, s, _ = proj3.shape
    nc = s // SEQ_T

    def body(x_ref, y_ref, cw, cb, wr, br, wi, bi, lam,
             o_ref, h_ref, xc_ref, a_ref, s_ref, gy_ref, hdg_ref, r_ref, i_ref, xprev, hprev):
        @pl.when(pl.program_id(1) == 0)
        def _():
            xprev[...] = jnp.zeros_like(xprev)
            hprev[...] = jnp.zeros_like(hprev)

        x = x_ref[...].astype(F32)
        prev8 = xprev[...]
        xc = cb[...] + sum(cw[j:j + 1, :] * _shift_down(x, 3 - j, prev8) for j in range(4))
        xprev[...] = x[SEQ_T - 8:]
        xc_ref[...] = xc
        r, i, a, s_, _ = _lru_coeffs(xc, wr, br, wi, bi, lam)
        a_ref[...] = a
        s_ref[...] = s_
        r_ref[...] = _bf(r)
        i_ref[...] = _bf(i)
        h = _scan_fwd(a, s_ * (i * xc), hprev[7:8, :])
        hprev[...] = h[SEQ_T - 8:]
        h_ref[...] = h
        gy, dgy = _gelu_and_grad(y_ref[...].astype(F32))
        o_ref[...] = _bf(h * gy)
        gy_ref[...] = _bf(gy)
        hdg_ref[...] = _bf(h * dgy)

    out = pl.BlockSpec((None, SEQ_T, D_MODEL), lambda i, c: (i, c, 0))
    half, full = jax.ShapeDtypeStruct((b, s, D_MODEL), BF16), jax.ShapeDtypeStruct((b, s, D_MODEL), F32)
    return _call(
        body, name="lru_fwd", grid=(b, nc), comm=comm,
        in_specs=[pl.BlockSpec((None, SEQ_T, D_MODEL), lambda i, c: (i, c, 3)),
                  pl.BlockSpec((None, SEQ_T, D_MODEL), lambda i, c: (i, c, 4))] + _LRU_PARAM_SPECS,
        out_specs=[out] * 9, out_shape=[half, full, full, full, full, half, half, half, half],
        scratch=[pltpu.VMEM((8, D_MODEL), F32), pltpu.VMEM((8, D_MODEL), F32)],
    )(proj3, proj3, *params)


def _lru_bwd(proj3, params, kept, db3, comm=None):
    b, s, _ = proj3.shape
    nc = s // SEQ_T
    blk8 = SEQ_T // 8
    hseq = kept[0]

    def body(x_ref, h_ref, xc_ref, a_ref, s_ref, gy_ref, hdg_ref, r_ref, i_ref, hp_ref, db_ref,
             cw, cb, wr, br, wi, bi, lam, d_ref, dwr_ref, dwi_ref, sm_ref, gnext, anext, dxcnext):
        c = pl.program_id(1)
        first_chunk = c == nc - 1

        @pl.when(jnp.logical_and(pl.program_id(0) == 0, c == 0))
        def _():
            dwr_ref[...] = jnp.zeros_like(dwr_ref)
            dwi_ref[...] = jnp.zeros_like(dwi_ref)
            sm_ref[...] = jnp.zeros_like(sm_ref)

        @pl.when(c == 0)
        def _():
            gnext[...] = jnp.zeros_like(gnext)
            anext[...] = jnp.zeros_like(anext)
            dxcnext[...] = jnp.zeros_like(dxcnext)

        x, xc, h = x_ref[...].astype(F32), xc_ref[...], h_ref[...]
        hprev = hp_ref[...] * jnp.where(first_chunk, 0.0, 1.0)
        r, i, a, s_ = r_ref[...].astype(F32), i_ref[...].astype(F32), a_ref[...], s_ref[...]
        sp = _softplus_neg(lam[...])
        db = db_ref[...].astype(F32)
        dy = db * hdg_ref[...].astype(F32)
        a_up = _shift_up(a, 1, anext[...])
        g = _scan_bwd(a_up, db * gy_ref[...].astype(F32), gnext[0:1, :])
        gnext[...] = g[0:8]
        anext[...] = a[0:8]
        da = g * _shift_down(h, 1, hprev)
        ixc = i * xc
        dla = da * a - (g * ixc) * (a * a) / s_
        di = g * s_ * xc
        dxc = g * s_ * i
        dzr = dla * (-LRU_C * sp) * r * (1.0 - r)
        dzi = di * i * (1.0 - i)
        lam_v = lam[...]
        _row_acc(sm_ref, 7, jnp.sum(dla * (LRU_C * r), axis=0, keepdims=True) * _sigmoid(-lam_v))
        _row_acc(sm_ref, 5, jnp.sum(dzr, axis=0, keepdims=True))
        _row_acc(sm_ref, 6, jnp.sum(dzi, axis=0, keepdims=True))
        parts, dwr_parts, dwi_parts = [], [], []
        for n in range(LRU_BLOCKS):
            cols = slice(n * LRU_BLOCK, (n + 1) * LRU_BLOCK)
            xb, zr, zi = _bf(xc[:, cols]), _bf(dzr[:, cols]), _bf(dzi[:, cols])
            parts.append(dxc[:, cols] + _dot_nt(zr, wr[n]) + _dot_nt(zi, wi[n]))
            dwr_parts.append(_dot_tn(xb, zr))
            dwi_parts.append(_dot_tn(xb, zi))
        dwr_ref[...] += jnp.stack(dwr_parts)
        dwi_ref[...] += jnp.stack(dwi_parts)
        dxc = jnp.concatenate(parts, axis=1)
        _row_acc(sm_ref, 4, jnp.sum(dxc, axis=0, keepdims=True))
        nxt = dxcnext[...]
        dx = jnp.zeros_like(x)
        for j in range(4):
            ahead = _shift_up(dxc, 3 - j, nxt)
            dx = dx + cw[j:j + 1, :] * ahead
            _row_acc(sm_ref, j, jnp.sum(ahead * x, axis=0, keepdims=True))
        dxcnext[...] = dxc[0:8]
        d_ref[:, 0:D_MODEL] = _bf(dx)
        d_ref[:, D_MODEL:2 * D_MODEL] = _bf(dy)

    rev = lambda col: (lambda i, c: (i, nc - 1 - c, col))
    prev = lambda col: (lambda i, c: (i, jnp.maximum((nc - 1 - c) * blk8 - 1, 0), col))
    return _call(
        body, name="lru_bwd", grid=(b, nc), comm=comm,
        in_specs=[pl.BlockSpec((None, SEQ_T, D_MODEL), rev(3))]
        + [pl.BlockSpec((None, SEQ_T, D_MODEL), rev(0))] * len(kept)
        + [pl.BlockSpec((None, 8, D_MODEL), prev(0)), pl.BlockSpec((None, SEQ_T, D_MODEL), rev(0))]
        + _LRU_PARAM_SPECS,
        out_specs=[pl.BlockSpec((None, SEQ_T, 2 * D_MODEL), rev(0)),
                   pl.BlockSpec((LRU_BLOCKS, LRU_BLOCK, LRU_BLOCK), lambda i, c: (0, 0, 0)),
                   pl.BlockSpec((LRU_BLOCKS, LRU_BLOCK, LRU_BLOCK), lambda i, c: (0, 0, 0)),
                   pl.BlockSpec((8, D_MODEL), lambda i, c: (0, 0))],
        out_shape=[jax.ShapeDtypeStruct((b, s, 2 * D_MODEL), BF16),
                   jax.ShapeDtypeStruct((LRU_BLOCKS, LRU_BLOCK, LRU_BLOCK), F32),
                   jax.ShapeDtypeStruct((LRU_BLOCKS, LRU_BLOCK, LRU_BLOCK), F32),
                   jax.ShapeDtypeStruct((8, D_MODEL), F32)],
        scratch=[pltpu.VMEM((8, D_MODEL), F32)] * 3,
    )(proj3, *kept, hseq, db3, *params)


def _merge_parts(a_ref, b_ref, gr_ref, gl_ref, mgb_ref, wro_ref, wlo_ref):
    ya = _dot(a_ref[...], wro_ref[...])
    yb = _dot(b_ref[...], wlo_ref[...])
    sa = _sigmoid(gr_ref[...].astype(F32) + mgb_ref[0:1, :])
    sb = _sigmoid(gl_ref[...].astype(F32) + mgb_ref[1:2, :])
    return ya, yb, sa, sb


def _merge_specs(tm):
    row = lambda col: pl.BlockSpec((tm, D_MODEL), lambda i: (i, col))
    full = pl.BlockSpec((D_MODEL, D_MODEL), lambda i: (0, 0))
    return row, full


def _merge_fwd(a_in, b_in, proj, mgb, wro, wlo, wout, x, *, tm, comm=None):
    m = x.shape[0]
    row, full = _merge_specs(tm)

    def body(a_ref, b_ref, gr_ref, gl_ref, mgb_ref, wro_ref, wlo_ref, wout_ref, x_ref,
             o_ref, mix_ref, ya_ref, yb_ref):
        ya, yb, sa, sb = _merge_parts(a_ref, b_ref, gr_ref, gl_ref, mgb_ref, wro_ref, wlo_ref)
        mix = _bf(sa * ya + sb * yb)
        o_ref[...] = x_ref[...] + _dot(mix, wout_ref[...])
        mix_ref[...] = mix
        ya_ref[...] = _bf(ya)
        yb_ref[...] = _bf(yb)

    act = jax.ShapeDtypeStruct((m, D_MODEL), BF16)
    return _call(
        body, name="merge_fwd", grid=(m // tm,), comm=comm,
        in_specs=[row(0), row(0), row(5), row(6), pl.BlockSpec((2, D_MODEL), lambda i: (0, 0)),
                  full, full, full, row(0)],
        out_specs=[row(0)] * 4,
        out_shape=[jax.ShapeDtypeStruct((m, D_MODEL), F32), act, act, act],
    )(a_in, b_in, proj, proj, mgb, wro, wlo, wout, x)


def _merge_bwd(ya, yb, proj, mgb, wro, wlo, wout, dx2, *, tm):
    m = dx2.shape[0]
    row, full = _merge_specs(tm)

    def body(ya_ref, yb_ref, gr_ref, gl_ref, mgb_ref, wro_ref, wlo_ref, wout_ref, dx_ref,
             dya_ref, dyb_ref, da_ref, db_ref, dg_ref, sm_ref):
        @pl.when(pl.program_id(0) == 0)
        def _():
            sm_ref[...] = jnp.zeros_like(sm_ref)

        ya, yb = ya_ref[...].astype(F32), yb_ref[...].astype(F32)
        sa = _sigmoid(gr_ref[...].astype(F32) + mgb_ref[0:1, :])
        sb = _sigmoid(gl_ref[...].astype(F32) + mgb_ref[1:2, :])
        dmix = _dot_nt(_bf(dx_ref[...]), wout_ref[...])
        dya, dyb = _bf(dmix * sa), _bf(dmix * sb)
        dya_ref[...] = dya
        dyb_ref[...] = dyb
        dga = dmix * ya * sa * (1.0 - sa)
        dgb = dmix * yb * sb * (1.0 - sb)
        dg_ref[:, 0:D_MODEL] = _bf(dga)
        dg_ref[:, D_MODEL:2 * D_MODEL] = _bf(dgb)
        _row_acc(sm_ref, ROW_MGB, jnp.sum(dga, axis=0, keepdims=True))
        _row_acc(sm_ref, ROW_MGB + 1, jnp.sum(dgb, axis=0, keepdims=True))
        da_ref[...] = _bf(_dot_nt(dya, wro_ref[...]))
        db_ref[...] = _bf(_dot_nt(dyb, wlo_ref[...]))

    act = jax.ShapeDtypeStruct((m, D_MODEL), BF16)
    return _call(
        body, name="merge_bwd", grid=(m // tm,),
        in_specs=[row(0), row(0), row(5), row(6), pl.BlockSpec((2, D_MODEL), lambda i: (0, 0)),
                  full, full, full, row(0)],
        out_specs=[row(0)] * 4 + [pl.BlockSpec((tm, 2 * D_MODEL), lambda i: (i, 0)),
                                  pl.BlockSpec((8, D_MODEL), lambda i: (0, 0))],
        out_shape=[act] * 4 + [jax.ShapeDtypeStruct((m, 2 * D_MODEL), BF16),
                               jax.ShapeDtypeStruct((8, D_MODEL), F32)],
    )(ya, yb, proj, proj, mgb, wro, wlo, wout, dx2)


def _ffn_act_fwd(up3, cw, cb):
    b, s, _ = up3.shape

    def body(g_ref, v_ref, cw_ref, cb_ref, o_ref, act_ref, q_ref, gprev):
        @pl.when(pl.program_id(1) == 0)
        def _():
            gprev[...] = jnp.zeros_like(gprev)

        gate, val = g_ref[...].astype(F32), v_ref[...].astype(F32)
        prev8 = gprev[...]
        gc = cb_ref[...] + sum(cw_ref[j:j + 1, :] * _shift_down(gate, 2 - j, prev8) for j in range(3))
        gprev[...] = gate[SEQ_T - 8:]
        act, dact = _gelu_and_grad(gc)
        o_ref[...] = _bf(act * val)
        act_ref[...] = _bf(act)
        q_ref[...] = _bf(dact * val)

    out = pl.BlockSpec((None, SEQ_T, D_FF), lambda i, c: (i, c, 0))
    return _call(
        body, name="ffn_act_fwd", grid=(b, s // SEQ_T),
        in_specs=[pl.BlockSpec((None, SEQ_T, D_FF), lambda i, c: (i, c, 0)),
                  pl.BlockSpec((None, SEQ_T, D_FF), lambda i, c: (i, c, 1)),
                  pl.BlockSpec((3, D_FF), lambda i, c: (0, 0)),
                  pl.BlockSpec((1, D_FF), lambda i, c: (0, 0))],
        out_specs=[out] * 3,
        out_shape=[jax.ShapeDtypeStruct((b, s, D_FF), BF16)] * 3,
        scratch=[pltpu.VMEM((8, D_FF), F32)],
    )(up3, up3, cw, cb)


def _ffn_act_bwd(up3, act3, q3, cw, df3, comm=None):
    b, s, _ = up3.shape
    nc = s // SEQ_T

    def body(g_ref, act_ref, q_ref, df_ref, cw_ref, dg_ref, dv_ref, sm_ref, dgcnext):
        c = pl.program_id(1)

        @pl.when(jnp.logical_and(pl.program_id(0) == 0, c == 0))
        def _():
            sm_ref[...] = jnp.zeros_like(sm_ref)

        @pl.when(c == 0)
        def _():
            dgcnext[...] = jnp.zeros_like(dgcnext)

        gate = g_ref[...].astype(F32)
        df = df_ref[...].astype(F32)
        dv_ref[...] = _bf(df * act_ref[...].astype(F32))
        dgc = df * q_ref[...].astype(F32)
        nxt = dgcnext[...]
        dgate = jnp.zeros_like(gate)
        for j in range(3):
            ahead = _shift_up(dgc, 2 - j, nxt)
            dgate = dgate + cw_ref[j:j + 1, :] * ahead
            _row_acc(sm_ref, j, jnp.sum(ahead * gate, axis=0, keepdims=True))
        _row_acc(sm_ref, 3, jnp.sum(dgc, axis=0, keepdims=True))
        dgcnext[...] = dgc[0:8]
        dg_ref[...] = _bf(dgate)

    rev = pl.BlockSpec((None, SEQ_T, D_FF), lambda i, c: (i, nc - 1 - c, 0))
    return _call(
        body, name="ffn_act_bwd", grid=(b, nc), comm=comm,
        in_specs=[rev, rev, rev, rev, pl.BlockSpec((3, D_FF), lambda i, c: (0, 0))],
        out_specs=[rev, rev, pl.BlockSpec((8, D_FF), lambda i, c: (0, 0))],
        out_shape=[jax.ShapeDtypeStruct((b, s, D_FF), BF16)] * 2 + [jax.ShapeDtypeStruct((8, D_FF), F32)],
        scratch=[pltpu.VMEM((8, D_FF), F32)],
    )(up3, act3, q3, df3, cw)


def _ffn_down_loss(f, wd, x2, nfw, target, *, tm):
    m, kf = f.shape
    nt = m // tm

    def body(f_ref, wd_ref, x_ref, nw_ref, t_ref, dx_ref, dnw_ref, lsum):
        i = pl.program_id(0)

        @pl.when(i == 0)
        def _():
            dnw_ref[...] = jnp.zeros_like(dnw_ref)
            lsum[...] = jnp.zeros_like(lsum)

        x3 = x_ref[...] + _dot(f_ref[...], wd_ref[...])
        nw = nw_ref[...]
        xh, r = _rms(x3)
        err = xh * nw - t_ref[...]
        lsum[...] += jnp.sum(err * err, axis=0, keepdims=True)
        dy = err * (1.0 / D_MODEL)
        g = dy * nw
        dx_ref[...] = r * (g - xh * jnp.mean(g * xh, axis=-1, keepdims=True))
        _row_acc(dnw_ref, ROW_NF, jnp.sum(dy * xh, axis=0, keepdims=True))

        @pl.when(i == nt - 1)
        def _():
            loss = jnp.sum(lsum[...], axis=1, keepdims=True) * (0.5 / D_MODEL)
            dnw_ref[ROW_LOSS:ROW_LOSS + 1, :] = jnp.broadcast_to(loss, (1, D_MODEL))

    row = pl.BlockSpec((tm, D_MODEL), lambda i: (i, 0))
    return _call(
        body, name="ffn_down_loss", grid=(nt,),
        in_specs=[pl.BlockSpec((tm, kf), lambda i: (i, 0)),
                  pl.BlockSpec((kf, D_MODEL), lambda i: (0, 0)),
                  row, pl.BlockSpec((1, D_MODEL), lambda i: (0, 0)), row],
        out_specs=[row, pl.BlockSpec((8, D_MODEL), lambda i: (0, 0))],
        out_shape=[jax.ShapeDtypeStruct((m, D_MODEL), F32), jax.ShapeDtypeStruct((8, D_MODEL), F32)],
        scratch=[pltpu.VMEM((1, D_MODEL), F32)],
    )(f, wd, x2, nfw, target)


def _row_tile(rows):
    return next((t for t in (256, 128, 64, 32, 16, 8) if rows % t == 0), rows)


def _adamw(w, gs, m, v, *, name):
    rows, cols = w.shape
    tr = _row_tile(rows)
    ng = len(gs)

    def body(w_ref, *rest):
        g_refs, (m_ref, v_ref, g_out, d_out, m_out, v_out) = rest[:ng], rest[ng:]
        g = g_refs[0][...]
        for r in g_refs[1:]:
            g = g + r[...]
        mn = ADAM_B1 * m_ref[...] + (1.0 - ADAM_B1) * g
        vn = ADAM_B2 * v_ref[...] + (1.0 - ADAM_B2) * (g * g)
        m_hat = mn / (1.0 - ADAM_B1 ** ADAM_STEP)
        v_hat = vn / (1.0 - ADAM_B2 ** ADAM_STEP)
        g_out[...] = g
        d_out[...] = -ADAM_LR * (m_hat / (jnp.sqrt(v_hat) + ADAM_EPS) + ADAM_WD * w_ref[...])
        m_out[...] = mn
        v_out[...] = vn

    spec = pl.BlockSpec((tr, cols), lambda i: (i, 0))
    return _call(
        body, name=name, grid=(rows // tr,),
        in_specs=[spec] * (3 + ng), out_specs=[spec] * 4,
        out_shape=[jax.ShapeDtypeStruct((rows, cols), F32)] * 4,
    )(w, *gs, m, v)


def _mesh_pos():
    x, y, c = lax.axis_index("x"), lax.axis_index("y"), lax.axis_index("c")
    return x, y, c


def _other_chips(x, y, c):
    return [((1 - x, y, c), 2 * (1 - x) + y), ((x, 1 - y, c), 2 * x + 1 - y),
            ((1 - x, 1 - y, c), 2 * (1 - x) + 1 - y)]


def _region(ref, axis, size, half_axis, chip, core=None):
    idx = [slice(None)] * len(ref.shape)
    if core is None:
        idx[axis] = pl.ds(pl.multiple_of(chip * size, size), size)
    elif half_axis == axis:
        h = size // 2
        idx[axis] = pl.ds(pl.multiple_of(chip * size + core * h, h), h)
    else:
        idx[axis] = pl.ds(pl.multiple_of(chip * size, size), size)
        h = ref.shape[half_axis] // 2
        idx[half_axis] = pl.ds(pl.multiple_of(core * h, h), h)
    return ref.at[tuple(idx)]


def _half(ref, half_axis, core):
    idx = [slice(None)] * len(ref.shape)
    h = ref.shape[half_axis] // 2
    idx[half_axis] = pl.ds(pl.multiple_of(core * h, h), h)
    return ref.at[tuple(idx)]


class _Copy:
    def __init__(self, make):
        self._make = make

    def start(self):
        self._make().start()

    def wait(self):
        self._make().wait()

    def wait_send(self):
        self._make().wait_send()

    def wait_recv(self):
        self._make().wait_recv()


def _remote(src, dst, send_sem, recv_sem, dev):
    return _Copy(lambda: pltpu.make_async_remote_copy(
        src_ref=src, dst_ref=dst, send_sem=send_sem, recv_sem=recv_sem, device_id=dev, device_id_type=MESH))


def _local(src, dst, sem):
    return _Copy(lambda: pltpu.make_async_copy(src, dst, sem))


def _dma_sems(n):
    return pltpu.SemaphoreType.DMA((n,))


def _place_shard(w, chip, axis, *, name):
    shape = list(w.shape)
    shape[axis] *= N_CHIPS
    if w.ndim == 3:
        block, grid = (1,) + w.shape[1:], (w.shape[0],)
        in_map, out_map = (lambda i, chip: (i, 0, 0)), (lambda i, chip: (i, chip[0], 0))
    else:
        tr = _row_tile(w.shape[0])
        nt = w.shape[0] // tr
        block, grid = (tr, w.shape[1]), (nt,)
        in_map = lambda i, chip: (i, 0)
        out_map = (lambda i, chip: (chip[0] * nt + i, 0)) if axis == 0 else (lambda i, chip: (i, chip[0]))

    def body(chip_ref, w_ref, o_ref):
        o_ref[...] = _bf(w_ref[...])

    return _call(body, name=name, grid=grid, prefetch=1, in_specs=[pl.BlockSpec(block, in_map)],
                 out_specs=pl.BlockSpec(block, out_map),
                 out_shape=jax.ShapeDtypeStruct(tuple(shape), BF16))(chip, w)


def _ici_leg(srcs, dsts, layout, sizes, n_whole, sems):
    send_sems, recv_sems, local_sems = sems
    x, y, c = _mesh_pos()
    mine = 2 * x + y
    n_big = len(srcs) - n_whole
    local, sends, recvs = [], [], []
    for t, (src, dst) in enumerate(zip(srcs, dsts)):
        if t < n_big:
            ax, hx = layout[t]
            part = _region(src, ax, sizes[t], hx, mine, c)
            landing = lambda chip, dst=dst, ax=ax, hx=hx, size=sizes[t]: _region(dst, ax, size, hx, chip, c)
        else:
            part, landing = src, (lambda chip, dst=dst: dst.at[chip])
            local.append(_local(src, dst.at[mine], local_sems.at[t - n_big]))
        for k, (dev, chip) in enumerate(_other_chips(x, y, c)):
            sends.append(_remote(part, landing(mine), send_sems.at[3 * t + k], recv_sems.at[3 * t + k], dev))
            recvs.append(_remote(part, landing(chip), send_sems.at[3 * t + k], recv_sems.at[3 * t + k], dev))
    return local, sends, recvs


def _d2d_leg(srcs, dsts, layout, sizes, sems):
    send_sems, recv_sems = sems
    x, y, c = _mesh_pos()
    sends, recvs = [], []
    for t, (src, dst) in enumerate(zip(srcs, dsts)):
        ax, hx = layout[t]
        for k, (_, chip) in enumerate(_other_chips(x, y, c)):
            sem = (send_sems.at[3 * t + k], recv_sems.at[3 * t + k])
            sends.append(_remote(_region(src, ax, sizes[t], hx, chip, c),
                                 _region(dst, ax, sizes[t], hx, chip, c), *sem, (x, y, 1 - c)))
            recvs.append(_remote(_region(src, ax, sizes[t], hx, chip, 1 - c),
                                 _region(dst, ax, sizes[t], hx, chip, 1 - c), *sem, (x, y, 1 - c)))
    return sends, recvs


def _gather_shapes(bufs, whole):
    return ([jax.ShapeDtypeStruct(b.shape, b.dtype) for b in bufs]
            + [jax.ShapeDtypeStruct((N_CHIPS,) + w.shape, w.dtype) for w in whole])


def _gather_ici(bufs, layout):
    n = len(bufs)
    sizes = [b.shape[ax] // N_CHIPS for b, (ax, _) in zip(bufs, layout)]

    def start(ins, outs, sems):
        for cp in _ici_leg(ins, outs, layout, sizes, 0, (*sems, None))[1]:
            cp.start()

    def finish(ins, outs, sems):
        _, sends, recvs = _ici_leg(ins, outs, layout, sizes, 0, (*sems, None))
        for cp in recvs:
            cp.wait_recv()
        for cp in sends:
            cp.wait_send()

    return _Comm(bufs, _gather_shapes(bufs, ()), [_dma_sems(3 * n), _dma_sems(3 * n)], start, finish,
                 aliases={i: i for i in range(n)})


def _both(a, b):
    ni, no, ns = len(a.ins), len(a.outs), len(a.sems)

    def start(ins, outs, sems):
        a.start(ins[:ni], outs[:no], sems[:ns])
        b.start(ins[ni:], outs[no:], sems[ns:])

    def finish(ins, outs, sems):
        a.finish(ins[:ni], outs[:no], sems[:ns])
        b.finish(ins[ni:], outs[no:], sems[ns:])

    aliases = {**a.aliases, **{ni + i: no + o for i, o in b.aliases.items()}}
    return _Comm(a.ins + b.ins, a.outs + b.outs, a.sems + b.sems, start, finish, aliases)


def _gather_d2d(bufs, layout, sizes):
    n = len(bufs)

    def start(ins, outs, sems):
        for cp in _d2d_leg(ins, outs, layout, sizes, sems)[0]:
            cp.start()

    def finish(ins, outs, sems):
        sends, recvs = _d2d_leg(ins, outs, layout, sizes, sems)
        for cp in recvs:
            cp.wait_recv()
        for cp in sends:
            cp.wait_send()

    return _Comm(bufs, [jax.ShapeDtypeStruct(b.shape, b.dtype) for b in bufs],
                 [_dma_sems(3 * n), _dma_sems(3 * n)], start, finish, aliases={i: i for i in range(n)})


def _norm_bf16(x, nw, *, name, tm):
    m, d = x.shape

    def body(x_ref, nw_ref, h_ref):
        h_ref[...] = _bf(_rms(x_ref[...])[0] * nw_ref[...])

    row = pl.BlockSpec((tm, d), lambda i: (i, 0))
    return _call(body, name=name, grid=(m // tm,), in_specs=[row, pl.BlockSpec((1, d), lambda i: (0, 0))],
                 out_specs=row, out_shape=jax.ShapeDtypeStruct((m, d), BF16))(x, nw)


def _in_proj_gather(h1, w_buf, later, later_cut, small, order, *, tm):
    m, d = h1.shape
    width = w_buf.shape[1] // N_CHIPS
    nr, nl = m // tm, len(later)
    sizes = [b.shape[ax] // N_CHIPS for b, (ax, _) in zip(later, later_cut)]

    def body(order_ref, h_ref, w_in, *rest):
        later_in, small_in = rest[:nl], rest[nl]
        o_ref, w_out = rest[nl + 1], rest[nl + 2]
        later_out, small_out = rest[nl + 3:2 * nl + 3], rest[2 * nl + 3]
        wv, load_sem, ici_send, ici_recv, d2d_send, d2d_recv, l_send, l_recv, l_local = rest[2 * nl + 4:]
        s, i = pl.program_id(0), pl.program_id(1)
        x, y, c = _mesh_pos()
        mine = 2 * x + y
        peers = _other_chips(x, y, c)
        part = lambda ref, chip, core=None: _region(ref, 1, width, 0, chip, core)

        def ici(k):
            dev, chip = peers[k]
            sem = (ici_send.at[k], ici_recv.at[k])
            return (_remote(part(w_in, mine, c), part(w_out, mine, c), *sem, dev),
                    _remote(part(w_in, chip, c), part(w_out, chip, c), *sem, dev))

        def d2d(k):
            chip, sem, sib = peers[k][1], (d2d_send.at[k], d2d_recv.at[k]), (x, y, 1 - c)
            return (_remote(part(w_out, chip, c), part(w_out, chip, c), *sem, sib),
                    _remote(part(w_out, chip, 1 - c), part(w_out, chip, 1 - c), *sem, sib))

        def load(src, chip, slot):
            cp = _local(part(src, chip), wv.at[slot], load_sem.at[slot])
            cp.start()
            cp.wait()

        def others():
            return _ici_leg(list(later_in) + [small_in], list(later_out) + [small_out], later_cut, sizes, 1,
                            (l_send, l_recv, l_local))

        @pl.when(jnp.logical_and(s == 0, i == 0))
        def _():
            for k in range(3):
                ici(k)[0].start()
            local, sends, _ = others()
            for cp in local + sends:
                cp.start()
            load(w_in, mine, 0)

        o_ref[...] = _bf(_dot(h_ref[...], wv[s % 2]))

        @pl.when(i == nr - 1)
        def _():
            for k in range(3):
                @pl.when(s == k)
                def _(k=k):
                    ici(k)[1].wait_recv()
                    d2d(k)[0].start()
                    d2d(k)[1].wait_recv()
                    load(w_out, peers[k][1], (k + 1) % 2)

            @pl.when(s == 3)
            def _():
                for k in range(3):
                    ici(k)[0].wait_send()
                    d2d(k)[0].wait_send()
                local, sends, recvs = others()
                for cp in recvs:
                    cp.wait_recv()
                for cp in sends:
                    cp.wait_send()
                for cp in local:
                    cp.wait()

    any_spec = pl.BlockSpec(memory_space=pl.ANY)
    n_any = nl + 2
    outs = _call(
        body, name="in_proj", grid=(N_CHIPS, nr), prefetch=1,
        in_specs=[pl.BlockSpec((tm, d), lambda s, i, order: (i, 0))] + [any_spec] * n_any,
        out_specs=[pl.BlockSpec((tm, width), lambda s, i, order: (i, order[s]))] + [any_spec] * n_any,
        out_shape=[jax.ShapeDtypeStruct((m, w_buf.shape[1]), BF16)] + _gather_shapes([w_buf] + list(later), [small]),
        scratch=[pltpu.VMEM((2, d, width), BF16), _dma_sems(2), _dma_sems(3), _dma_sems(3), _dma_sems(3),
                 _dma_sems(3), _dma_sems(3 * (nl + 1)), _dma_sems(3 * (nl + 1)), _dma_sems(1)],
        aliases={2 + t: 1 + t for t in range(nl + 1)},
    )(order, h1, w_buf, *later, small)
    return outs[0], outs[1], list(outs[2:2 + nl]), outs[2 + nl]


def _exchange(grads, layout):
    n = len(grads)
    others = N_DEV - 1
    sizes = [g.shape[ax] // N_CHIPS for g, (ax, _) in zip(grads, layout)]
    out_shapes = []
    for g, (ax, hx), sz in zip(grads, layout, sizes):
        shape = list(g.shape)
        shape[ax] = sz
        shape[hx] //= 2
        out_shapes.append(jax.ShapeDtypeStruct((others,) + tuple(shape), g.dtype))

    def copies(ins, outs, sems):
        send_sems, recv_sems = sems
        x, y, c = _mesh_pos()
        sends, recvs = [], []
        for t, (src, dst) in enumerate(zip(ins, outs)):
            ax, hx = layout[t]
            for r in range(1, N_DEV):
                px = (1 - x) if r & 4 else x
                py = (1 - y) if r & 2 else y
                pc = (1 - c) if r & 1 else c
                sem = (send_sems.at[others * t + r - 1], recv_sems.at[others * t + r - 1])
                part = _region(src, ax, sizes[t], hx, 2 * px + py, pc)
                sends.append(_remote(part, dst.at[r - 1], *sem, (px, py, pc)))
                recvs.append(_remote(part, dst.at[r - 1], *sem, (px, py, pc)))
        return sends, recvs

    def start(ins, outs, sems):
        for cp in copies(ins, outs, sems)[0]:
            cp.start()

    def finish(ins, outs, sems):
        sends, recvs = copies(ins, outs, sems)
        for cp in recvs:
            cp.wait_recv()
        for cp in sends:
            cp.wait_send()

    return _Comm(grads, out_shapes, [_dma_sems(others * n), _dma_sems(others * n)], start, finish)


def _reduce_half(g, parts, pos, cut, *, name):
    ax, _ = cut
    others = parts.shape[0]
    if g.ndim == 3:
        nb, rows, cols = g.shape
        hb = nb // 2
        block, grid, out_shape = (1, rows // N_CHIPS, cols), (hb,), (nb, rows // N_CHIPS, cols)
        g_map = lambda i, pos: (pos[1] * hb + i, pos[0], 0)
        o_map = lambda i, pos: (pos[1] * hb + i, 0, 0)
        p_map = lambda i, pos: (0, i, 0, 0)
    elif ax == 1:
        rows, cols = g.shape
        tr = _row_tile(rows // 2)
        nt = rows // 2 // tr
        block, grid, out_shape = (tr, cols // N_CHIPS), (nt,), (rows, cols // N_CHIPS)
        g_map = lambda i, pos: (pos[1] * nt + i, pos[0])
        o_map = lambda i, pos: (pos[1] * nt + i, 0)
        p_map = lambda i, pos: (0, i, 0)
    else:
        rows, cols = g.shape
        tr = _row_tile(rows // N_CHIPS // 2)
        nt = rows // N_CHIPS // 2 // tr
        block, grid, out_shape = (tr, cols), (nt,), (rows // N_CHIPS, cols)
        g_map = lambda i, pos: (pos[0] * 2 * nt + pos[1] * nt + i, 0)
        o_map = lambda i, pos: (pos[1] * nt + i, 0)
        p_map = lambda i, pos: (0, i, 0)

    def body(pos_ref, g_ref, p_ref, o_ref):
        acc = g_ref[...].astype(F32)
        for r in range(others):
            acc = acc + p_ref[r].astype(F32)
        o_ref[...] = acc

    return _call(
        body, name=name, grid=grid, prefetch=1,
        in_specs=[pl.BlockSpec(block, g_map), pl.BlockSpec((others,) + block, p_map)],
        out_specs=pl.BlockSpec(block, o_map), out_shape=jax.ShapeDtypeStruct(out_shape, F32),
    )(pos, g, parts)


def _join_halves(bufs):
    return _call(None, name="join_halves", comm=_join_comm(bufs))()[1]


def _join_comm(bufs):
    n = len(bufs)

    def copies(ins, outs, sems):
        send_sems, recv_sems = sems
        x, y, c = _mesh_pos()
        sends = [_remote(_half(src, 0, c), _half(dst, 0, c), send_sems.at[t], recv_sems.at[t], (x, y, 1 - c))
                 for t, (src, dst) in enumerate(zip(ins, outs))]
        recvs = [_remote(_half(src, 0, 1 - c), _half(dst, 0, 1 - c), send_sems.at[t], recv_sems.at[t],
                         (x, y, 1 - c)) for t, (src, dst) in enumerate(zip(ins, outs))]
        return sends, recvs

    def start(ins, outs, sems):
        for cp in copies(ins, outs, sems)[0]:
            cp.start()

    def finish(ins, outs, sems):
        sends, recvs = copies(ins, outs, sems)
        for cp in recvs:
            cp.wait_recv()
        for cp in sends:
            cp.wait_send()

    return _Comm(bufs, [jax.ShapeDtypeStruct(b.shape, b.dtype) for b in bufs],
                 [_dma_sems(n), _dma_sems(n)], start, finish, aliases={i: i for i in range(n)})


def _allreduce_small(pack):
    rows, cols = pack.shape

    def body(p_ref, o_ref, slots, send_sems, recv_sems):
        x, y, c = _mesh_pos()
        me = 4 * x + 2 * y + c
        slots[me] = p_ref[...]
        copies = []
        for r in range(1, N_DEV):
            fx, fy, fc = (r >> 2) & 1, (r >> 1) & 1, r & 1
            dev = ((1 - x) if fx else x, (1 - y) if fy else y, (1 - c) if fc else c)
            cp = pltpu.make_async_remote_copy(
                src_ref=p_ref, dst_ref=slots.at[me], send_sem=send_sems.at[r - 1],
                recv_sem=recv_sems.at[r - 1], device_id=dev, device_id_type=MESH)
            cp.start()
            copies.append(cp)
        for cp in copies:
            cp.wait_recv()
        for cp in copies:
            cp.wait_send()
        acc = slots[0]
        for d in range(1, N_DEV):
            acc = acc + slots[d]
        o_ref[...] = acc

    vmem = pl.BlockSpec(memory_space=pltpu.VMEM)
    return _call(
        body, name="allreduce_small", in_specs=[vmem], out_specs=vmem,
        out_shape=jax.ShapeDtypeStruct((rows, cols), F32),
        scratch=[pltpu.VMEM((N_DEV, rows, cols), F32), pltpu.SemaphoreType.DMA((N_DEV - 1,)),
                 pltpu.SemaphoreType.DMA((N_DEV - 1,))],
    )(pack)


def _pad_rows(a, rows=8):
    return jnp.pad(a, ((0, rows - a.shape[0]), (0, 0)))


def kernel(x, positions, norm1_w, w_in, merge_gate_b, ret_gn_w, w_ret_o, lru_conv_w, lru_conv_b, lru_w_r, lru_b_r, lru_w_i, lru_b_i, lru_lambda, w_lru_o, w_out, norm2_w, ffn_w_up, ffn_conv_w, ffn_conv_b, ffn_w_down, norm_f_w, loss_target, m_norm1_w, m_w_in, m_merge_gate_b, m_ret_gn_w, m_w_ret_o, m_lru_conv_w, m_lru_conv_b, m_lru_w_r, m_lru_b_r, m_lru_w_i, m_lru_b_i, m_lru_lambda, m_w_lru_o, m_w_out, m_norm2_w, m_ffn_w_up, m_ffn_conv_w, m_ffn_conv_b, m_ffn_w_down, m_norm_f_w, v_norm1_w, v_w_in, v_merge_gate_b, v_ret_gn_w, v_w_ret_o, v_lru_conv_w, v_lru_conv_b, v_lru_w_r, v_lru_b_r, v_lru_w_i, v_lru_b_i, v_lru_lambda, v_w_lru_o, v_w_out, v_norm2_w, v_ffn_w_up, v_ffn_conv_w, v_ffn_conv_b, v_ffn_w_down, v_norm_f_w):
    names = ["norm1_w", "w_in", "merge_gate_b", "ret_gn_w", "w_ret_o", "lru_conv_w", "lru_conv_b", "lru_w_r",
             "lru_b_r", "lru_w_i", "lru_b_i", "lru_lambda", "w_lru_o", "w_out", "norm2_w", "ffn_w_up",
             "ffn_conv_w", "ffn_conv_b", "ffn_w_down", "norm_f_w"]
    w_args = dict(zip(names, [norm1_w, w_in, merge_gate_b, ret_gn_w, w_ret_o, lru_conv_w, lru_conv_b, lru_w_r,
                              lru_b_r, lru_w_i, lru_b_i, lru_lambda, w_lru_o, w_out, norm2_w, ffn_w_up,
                              ffn_conv_w, ffn_conv_b, ffn_w_down, norm_f_w]))
    m_args = dict(zip(names, [m_norm1_w, m_w_in, m_merge_gate_b, m_ret_gn_w, m_w_ret_o, m_lru_conv_w,
                              m_lru_conv_b, m_lru_w_r, m_lru_b_r, m_lru_w_i, m_lru_b_i, m_lru_lambda, m_w_lru_o,
                              m_w_out, m_norm2_w, m_ffn_w_up, m_ffn_conv_w, m_ffn_conv_b, m_ffn_w_down,
                              m_norm_f_w]))
    v_args = dict(zip(names, [v_norm1_w, v_w_in, v_merge_gate_b, v_ret_gn_w, v_w_ret_o, v_lru_conv_w,
                              v_lru_conv_b, v_lru_w_r, v_lru_b_r, v_lru_w_i, v_lru_b_i, v_lru_lambda, v_w_lru_o,
                              v_w_out, v_norm2_w, v_ffn_w_up, v_ffn_conv_w, v_ffn_conv_b, v_ffn_w_down,
                              v_norm_f_w]))

    bsz, seq, d = x.shape
    m = bsz * seq
    tm = min(MM_ROWS, m)
    tm_fused = min(FUSED_ROWS, m)
    tm_tall = min(TALL_ROWS, m)
    chip = 2 * lax.axis_index("x") + lax.axis_index("y")

    big = ["w_in", "w_ret_o", "w_lru_o", "w_out", "lru_w_r", "lru_w_i", "ffn_w_up", "ffn_w_down"]
    cut = dict(w_in=(1, 0), w_ret_o=(0, 0), w_lru_o=(0, 0), w_out=(0, 0), lru_w_r=(1, 0), lru_w_i=(1, 0),
               ffn_w_up=(1, 0), ffn_w_down=(0, 0))
    core = lax.axis_index("c")
    chip1 = jnp.reshape(chip, (1,)).astype(jnp.int32)
    pos = jnp.stack([chip, core]).astype(jnp.int32)
    placed = {n: _place_shard(w_args[n][0], chip1, cut[n][0], name="place_" + n) for n in big}
    small_pack = jnp.concatenate([
        jnp.pad(merge_gate_b[0], ((0, 6), (0, 512))),
        jnp.pad(lru_conv_w[0], ((0, 4), (0, 512))),
        jnp.pad(lru_b_r[0], ((0, 4), (0, 704))),
        jnp.pad(lru_b_i[0], ((0, 4), (0, 704))),
        jnp.pad(ffn_conv_w[0], ((0, 5), (0, 0))),
    ], axis=0)
    x2d = x.reshape(m, d)
    mx, my = lax.axis_index("x"), lax.axis_index("y")
    order = jnp.stack([chip, 2 * (1 - mx) + my, 2 * mx + 1 - my, 2 * (1 - mx) + 1 - my]).astype(jnp.int32)
    mixer = ["w_ret_o", "w_lru_o", "w_out", "lru_w_r", "lru_w_i"]
    cuts = lambda ns: [cut[n] for n in ns]
    sizes = lambda ns: [w_args[n].shape[1 + cut[n][0]] for n in ns]
    h1 = _norm_bf16(x2d, norm1_w, name="norm1", tm=tm)
    proj, w_in_full, bufs, sp = _in_proj_gather(h1, placed["w_in"], [placed[n] for n in mixer], cuts(mixer),
                                               small_pack, order, tm=tm_tall)
    wb = {"w_in": w_in_full}
    mgb = jnp.transpose(sp[:, 0:2, 0:256], (1, 0, 2)).reshape(2, D_MODEL)
    lcw = jnp.transpose(sp[:, 8:12, 0:256], (1, 0, 2)).reshape(4, D_MODEL)
    lbr = jnp.transpose(sp[:, 16:20, 0:64], (1, 0, 2)).reshape(1, D_MODEL)
    lbi = jnp.transpose(sp[:, 24:28, 0:64], (1, 0, 2)).reshape(1, D_MODEL)
    fcw = jnp.transpose(sp[:, 32:35, :], (1, 0, 2)).reshape(3, D_FF)
    nfw = norm_f_w.reshape(1, D_MODEL)

    half = RET_DK // 2
    inv_freq = ROPE_BASE ** (-jnp.arange(half, dtype=F32) / half)
    cos, sin = _rope_tables(positions.reshape(bsz, seq, 1), jnp.concatenate([inv_freq, inv_freq]).reshape(1, RET_DK))
    proj3 = proj.reshape(bsz, seq, D_IN)
    down, up_w = ["ffn_w_down"], ["ffn_w_up"]
    (a_in3, states), bufs = _retention_fwd(
        proj3, cos, sin, ret_gn_w,
        comm=_both(_gather_d2d(bufs, cuts(mixer), sizes(mixer)), _gather_ici([placed["ffn_w_down"]], cuts(down))))
    wb.update(zip(mixer, bufs[:len(mixer)]))
    lru_params = (lcw, lru_conv_b, wb["lru_w_r"], lbr, wb["lru_w_i"], lbi, lru_lambda)
    (b_in3, *lru_kept), (up_buf, wb["ffn_w_down"]) = _lru_fwd(
        proj3, lru_params,
        comm=_both(_gather_ici([placed["ffn_w_up"]], cuts(up_w)), _gather_d2d(bufs[len(mixer):], cuts(down), sizes(down))))
    a_in, b_in = a_in3.reshape(m, d), b_in3.reshape(m, d)
    (x2, mix, ya, yb), (wb["ffn_w_up"],) = _merge_fwd(
        a_in, b_in, proj, mgb, wb["w_ret_o"], wb["w_lru_o"], wb["w_out"], x2d, tm=tm,
        comm=_gather_d2d([up_buf], cuts(up_w), sizes(up_w)))
    up, h2 = _norm_matmul(x2, norm2_w, wb["ffn_w_up"], name="ffn_up", tm=tm_tall, tn=TALL_COLS)
    up3 = up.reshape(bsz, seq, 2 * D_FF)
    f3, act3, q3 = _ffn_act_fwd(up3, fcw, ffn_conv_b)
    f = f3.reshape(m, D_FF)
    dx3, sm_nf = _ffn_down_loss(f, wb["ffn_w_down"], x2, nfw, loss_target.reshape(m, d), tm=tm)

    def send(*ns):
        return _exchange([g_full[n] for n in ns], [cut[n] for n in ns])

    g_full, parts = {}, {}
    df = _mm_nt(dx3, wb["ffn_w_down"], name="ffn_down_dx", tm=tm_tall, out_dtype=BF16)
    g_full["ffn_w_down"] = _mm_tn(f, [dx3], name="ffn_down_dw", tm=tm_tall)
    (dgate3, dval3, sm_ffn), (parts["ffn_w_down"],) = _ffn_act_bwd(
        up3, act3, q3, fcw, df.reshape(bsz, seq, D_FF), comm=send("ffn_w_down"))
    dup = [dgate3.reshape(m, D_FF), dval3.reshape(m, D_FF)]
    g_full["ffn_w_up"] = _mm_tn(h2, dup, name="ffn_up_dw", tm=tm_tall)
    (dx2, sm_n2), (parts["ffn_w_up"],) = _mm_nt_normbwd(
        dup, wb["ffn_w_up"], x2, norm2_w, dx3, name="ffn_up_dx", tm=tm, row=ROW_N2, comm=send("ffn_w_up"))
    dya, dyb, da_in, db_in, dgates, sm_mg = _merge_bwd(
        ya, yb, proj, mgb, wb["w_ret_o"], wb["w_lru_o"], wb["w_out"], dx2, tm=tm_fused)
    g_full["w_out"] = _mm_tn(mix, [dx2], name="out_dw", tm=tm_tall)
    g_full["w_ret_o"] = _mm_tn(a_in, [dya], name="ret_o_dw", tm=tm_tall)
    g_full["w_lru_o"] = _mm_tn(b_in, [dyb], name="lru_o_dw", tm=tm_tall)
    (dlru3, dwr, dwi, sm_lru), (parts["w_out"], parts["w_ret_o"], parts["w_lru_o"]) = _lru_bwd(
        proj3, lru_params, lru_kept, db_in.reshape(bsz, seq, d), comm=send("w_out", "w_ret_o", "w_lru_o"))
    g_full["lru_w_r"], g_full["lru_w_i"] = dwr.astype(BF16), dwi.astype(BF16)
    (dret3, sm_gn), (parts["lru_w_r"], parts["lru_w_i"]) = _retention_bwd(
        proj3, cos, sin, ret_gn_w, states, da_in.reshape(bsz, seq, d), comm=send("lru_w_r", "lru_w_i"))
    dproj = [dret3.reshape(m, 3072), dlru3.reshape(m, 2048), dgates]
    g_full["w_in"] = _mm_tn(h1, dproj, name="in_proj_dw", tm=tm_tall)
    half_sum = lambda n: _reduce_half(g_full[n], parts[n], pos, cut[n], name="sum_" + n)
    (grad_x, sm_n1), (parts["w_in"], *joined) = _mm_nt_normbwd(
        dproj, wb["w_in"], x2d, norm1_w, dx2, name="in_proj_dx", tm=tm, row=ROW_N1,
        comm=_both(send("w_in"), _join_comm([half_sum(n) for n in big[1:]])))
    reduced = _join_halves([half_sum("w_in")]) + joined
    misc = sm_n1 + sm_mg + sm_gn + sm_n2 + sm_nf
    pack = jnp.concatenate(
        [misc, sm_lru, sm_ffn[:, 0:1024], sm_ffn[:, 1024:2048], sm_ffn[:, 2048:3072]], axis=0)
    tot = _allreduce_small(pack)
    ffn_sm = jnp.concatenate([tot[16:24], tot[24:32], tot[32:40]], axis=1)
    g_small = {
        "norm1_w": tot[ROW_N1:ROW_N1 + 1], "merge_gate_b": tot[ROW_MGB:ROW_MGB + 2],
        "ret_gn_w": tot[ROW_GN:ROW_GN + 1], "norm2_w": tot[ROW_N2:ROW_N2 + 1], "norm_f_w": tot[ROW_NF:ROW_NF + 1],
        "lru_conv_w": tot[8:12], "lru_conv_b": tot[12:13], "lru_b_r": tot[13:14].reshape(4, 256),
        "lru_b_i": tot[14:15].reshape(4, 256), "lru_lambda": tot[15:16],
        "ffn_conv_w": ffn_sm[0:3], "ffn_conv_b": ffn_sm[3:4],
    }
    small_shard = dict(merge_gate_b=256, lru_conv_w=256, lru_b_r=64, lru_b_i=64, ffn_conv_w=768)

    outs = {}
    for n, g in zip(big, reduced):
        shape = w_args[n].shape
        g = g.reshape(-1, g.shape[-1])
        outs[n] = [o.reshape(shape) for o in _adamw(
            w_args[n].reshape(g.shape), [g], m_args[n].reshape(g.shape), v_args[n].reshape(g.shape),
            name="adamw_" + n)]
    for n, g in g_small.items():
        shape = w_args[n].shape
        if n in small_shard:
            g = lax.dynamic_slice_in_dim(g, chip * small_shard[n], small_shard[n], axis=1)
        w2 = w_args[n].reshape(g.shape)
        outs[n] = [o.reshape(shape) for o in _adamw(
            w2, [g], m_args[n].reshape(g.shape), v_args[n].reshape(g.shape), name="adamw_" + n)]

    result = [tot[ROW_LOSS, 0], grad_x.reshape(bsz, seq, d)]
    for k in range(4):
        result += [outs[n][k] for n in names]
    return tuple(result)
```

```python
import functools
import math

import numpy as np
import jax
import jax.numpy as jnp
from jax import lax
from jax.experimental import pallas as pl
from jax.experimental.pallas import tpu as pltpu

F32 = jnp.float32
BF16 = jnp.bfloat16

D_MODEL = 1024
RET_HEADS = 4
RET_DK = 128
RET_DV = 256
LRU_BLOCKS = 4
LRU_BLOCK = 256
LRU_C = 8.0
D_FF = 3072
D_IN = 7168
ROPE_BASE = 10000.0
RMS_EPS = 1e-6
GN_EPS = 1e-6
ADAM_LR, ADAM_B1, ADAM_B2, ADAM_EPS, ADAM_WD, ADAM_STEP = 0.001, 0.9, 0.999, 1e-08, 0.01, 10

N_CHIPS = 4
N_DEV = 8
SEQ_T = 256
REF_CHUNK = 64
COL = 1024
MM_ROWS = 1024
TALL_ROWS, TALL_COLS = 2048, 1024
FUSED_ROWS = 512
VMEM_LIMIT_BYTES = 56 * 1024 * 1024
MESH = pl.DeviceIdType.MESH
ROW_N1, ROW_MGB, ROW_GN, ROW_N2, ROW_NF, ROW_LOSS = 0, 1, 3, 4, 5, 6
GELU_K = math.sqrt(2.0 / math.pi)
GELU_C = 0.044715


class _Comm:
    def __init__(self, ins, outs, sems, start, finish, aliases=None):
        self.ins, self.outs, self.sems = list(ins), list(outs), list(sems)
        self.start, self.finish, self.aliases = start, finish, dict(aliases or {})


def _call(body, *, name, out_shape=(), grid=None, in_specs=(), out_specs=(), scratch=(), comm=None, prefetch=0,
          aliases=None):
    single = not isinstance(out_shape, (list, tuple))
    out_shape = [out_shape] if single else list(out_shape)
    out_specs = [out_specs] if single else list(out_specs)
    in_specs, scratch = list(in_specs), list(scratch)
    n_in, n_out, n_scr = len(in_specs), len(out_shape), len(scratch)
    kwargs = dict(name=name, compiler_params=pltpu.CompilerParams(vmem_limit_bytes=VMEM_LIMIT_BYTES))
    if prefetch:
        assert comm is None
        spec = pltpu.PrefetchScalarGridSpec(num_scalar_prefetch=prefetch, grid=grid, in_specs=in_specs,
                                            out_specs=out_specs, scratch_shapes=scratch)
        fn = pl.pallas_call(body, out_shape=out_shape, grid_spec=spec, input_output_aliases=dict(aliases or {}),
                            **kwargs)
        return (lambda *args: fn(*args)[0]) if single else fn
    if grid is not None:
        kwargs["grid"] = grid
    if comm is None:
        fn = pl.pallas_call(body, out_shape=out_shape, in_specs=in_specs, out_specs=out_specs,
                            scratch_shapes=scratch, **kwargs)
        return (lambda *args: fn(*args)[0]) if single else fn

    any_spec = pl.BlockSpec(memory_space=pl.ANY)
    n_cin, n_cout = len(comm.ins), len(comm.outs)

    def wrapped(*refs):
        ins, refs = refs[:n_in], refs[n_in:]
        cins, refs = refs[:n_cin], refs[n_cin:]
        outs, refs = refs[:n_out], refs[n_out:]
        couts, refs = refs[:n_cout], refs[n_cout:]
        scr, csems = refs[:n_scr], refs[n_scr:]
        if grid is None:
            comm.start(cins, couts, csems)
            comm.finish(cins, couts, csems)
            return
        ids = [pl.program_id(a) for a in range(len(grid))]
        first = functools.reduce(jnp.logical_and, [i == 0 for i in ids])
        last = functools.reduce(jnp.logical_and, [i == g - 1 for i, g in zip(ids, grid)])
        pl.when(first)(lambda: comm.start(cins, couts, csems))
        body(*ins, *outs, *scr)
        pl.when(last)(lambda: comm.finish(cins, couts, csems))

    fn = pl.pallas_call(
        wrapped, out_shape=out_shape + comm.outs, in_specs=in_specs + [any_spec] * n_cin,
        out_specs=out_specs + [any_spec] * n_cout, scratch_shapes=scratch + comm.sems,
        input_output_aliases={n_in + i: n_out + o for i, o in comm.aliases.items()}, **kwargs)

    def run(*args):
        res = fn(*args, *comm.ins)
        own = res[0] if single else list(res[:n_out])
        return own, list(res[n_out:])

    return run


def _dot(a, b):
    return jnp.dot(a, b, preferred_element_type=F32)


def _dot_nt(a, b):
    return lax.dot_general(a, b, (((1,), (1,)), ((), ())), preferred_element_type=F32)


def _dot_tn(a, b):
    return lax.dot_general(a, b, (((0,), (0,)), ((), ())), preferred_element_type=F32)


def _bf(x):
    return x.astype(BF16)


def _sigmoid(x):
    return 1.0 / (1.0 + jnp.exp(-x))


def _gelu_and_grad(x):
    x2 = x * x
    s = _sigmoid(x * (2.0 * GELU_K * GELU_C * x2 + 2.0 * GELU_K))
    g = x * s
    dg = s + g * (1.0 - s) * (6.0 * GELU_K * GELU_C * x2 + 2.0 * GELU_K)
    return g, dg


def _rms(x):
    r = lax.rsqrt(jnp.mean(x * x, axis=-1, keepdims=True) + RMS_EPS)
    return x * r, r


def _rms_bwd(dy, x, nw):
    xh, r = _rms(x)
    g = dy * nw
    dx = r * (g - xh * jnp.mean(g * xh, axis=-1, keepdims=True))
    return dx, jnp.sum(dy * xh, axis=0, keepdims=True)


def _row_acc(ref, row, val):
    ref[row:row + 1, :] = ref[row:row + 1, :] + val


def _shift_down(x, j, prev8):
    if j == 0:
        return x
    n = x.shape[0] // 8
    row = lax.broadcasted_iota(jnp.int32, prev8.shape, 0)
    turned = [pltpu.roll(prev8, j, 0)] + [pltpu.roll(x[8 * k:8 * k + 8], j, 0) for k in range(n)]
    return jnp.concatenate([jnp.where(row < j, turned[k], turned[k + 1]) for k in range(n)], axis=0)


def _shift_up(x, j, next8):
    if j == 0:
        return x
    n = x.shape[0] // 8
    row = lax.broadcasted_iota(jnp.int32, next8.shape, 0)
    turned = [pltpu.roll(x[8 * k:8 * k + 8], 8 - j, 0) for k in range(n)] + [pltpu.roll(next8, 8 - j, 0)]
    return jnp.concatenate([jnp.where(row >= 8 - j, turned[k + 1], turned[k]) for k in range(n)], axis=0)


def _scan_fwd(a, b, carry):
    row = lax.broadcasted_iota(jnp.int32, (8, a.shape[1]), 0)
    out = []
    for k in range(a.shape[0] // 8):
        ak, bk = a[8 * k:8 * k + 8], b[8 * k:8 * k + 8]
        for s in (1, 2, 4):
            keep = row >= s
            ar, br = pltpu.roll(ak, s, 0), pltpu.roll(bk, s, 0)
            bk = jnp.where(keep, ak * br + bk, bk)
            ak = jnp.where(keep, ak * ar, ak)
        hk = ak * carry + bk
        carry = hk[7:8]
        out.append(hk)
    return jnp.concatenate(out, axis=0)


def _scan_bwd(a, b, carry):
    row = lax.broadcasted_iota(jnp.int32, (8, a.shape[1]), 0)
    out = []
    for k in reversed(range(a.shape[0] // 8)):
        ak, bk = a[8 * k:8 * k + 8], b[8 * k:8 * k + 8]
        for s in (1, 2, 4):
            keep = row < 8 - s
            ar, br = pltpu.roll(ak, 8 - s, 0), pltpu.roll(bk, 8 - s, 0)
            bk = jnp.where(keep, ak * br + bk, bk)
            ak = jnp.where(keep, ak * ar, ak)
        gk = bk + ak * carry
        carry = gk[0:1]
        out.append(gk)
    return jnp.concatenate(out[::-1], axis=0)


def _norm_matmul(x, nw, w, *, name, tm, tn):
    m, d = x.shape
    n = w.shape[1]

    def body(x_ref, nw_ref, w_ref, o_ref, h_ref, h_sc):
        @pl.when(pl.program_id(1) == 0)
        def _():
            xh, _ = _rms(x_ref[...])
            h = _bf(xh * nw_ref[...])
            h_sc[...] = h
            h_ref[...] = h

        o_ref[...] = _bf(_dot(h_sc[...], w_ref[...]))

    return _call(
        body, name=name, grid=(m // tm, n // tn),
        in_specs=[pl.BlockSpec((tm, d), lambda i, j: (i, 0)),
                  pl.BlockSpec((1, d), lambda i, j: (0, 0)),
                  pl.BlockSpec((d, tn), lambda i, j: (0, j))],
        out_specs=[pl.BlockSpec((tm, tn), lambda i, j: (i, j)),
                   pl.BlockSpec((tm, d), lambda i, j: (i, 0))],
        out_shape=[jax.ShapeDtypeStruct((m, n), BF16), jax.ShapeDtypeStruct((m, d), BF16)],
        scratch=[pltpu.VMEM((tm, d), BF16)],
    )(x, nw, w)


def _mm_nt(a, w, *, name, tm, out_dtype):
    m, k = a.shape
    n = w.shape[0]

    def body(a_ref, w_ref, o_ref):
        o_ref[...] = _dot_nt(_bf(a_ref[...]), w_ref[...]).astype(out_dtype)

    return _call(
        body, name=name, grid=(m // tm, n // COL),
        in_specs=[pl.BlockSpec((tm, k), lambda i, j: (i, 0)),
                  pl.BlockSpec((COL, k), lambda i, j: (j, 0))],
        out_specs=pl.BlockSpec((tm, COL), lambda i, j: (i, j)),
        out_shape=jax.ShapeDtypeStruct((m, n), out_dtype),
    )(a, w)


def _piece_layout(pieces):
    offs, nblk, o = [], [], 0
    for p in pieces:
        offs.append(o)
        nblk.append(p.shape[1] // COL)
        o += p.shape[1] // COL
    return offs, nblk, o


def _mm_tn(a, pieces, *, name, tm, out_dtype=BF16):
    m, k = a.shape
    offs, nblk, nn = _piece_layout(pieces)

    def piece_spec(o, nb):
        def idx(ki, nj, mi):
            use = jnp.logical_and(nj >= o, nj < o + nb)
            return (jnp.where(use, mi, 0), jnp.clip(nj - o, 0, nb - 1))
        return pl.BlockSpec((tm, COL), idx)

    def body(a_ref, *rest):
        p_refs, o_ref, acc = rest[:len(pieces)], rest[len(pieces)], rest[len(pieces) + 1]
        nj, mi = pl.program_id(1), pl.program_id(2)

        @pl.when(mi == 0)
        def _():
            acc[...] = jnp.zeros_like(acc)

        for p_ref, o, nb in zip(p_refs, offs, nblk):
            @pl.when(jnp.logical_and(nj >= o, nj < o + nb))
            def _(p_ref=p_ref):
                acc[...] += _dot_tn(_bf(a_ref[...]), _bf(p_ref[...]))

        @pl.when(mi == pl.num_programs(2) - 1)
        def _():
            o_ref[...] = acc[...].astype(out_dtype)

    return _call(
        body, name=name, grid=(k // COL, nn, m // tm),
        in_specs=[pl.BlockSpec((tm, COL), lambda ki, nj, mi: (mi, ki))]
        + [piece_spec(o, nb) for o, nb in zip(offs, nblk)],
        out_specs=pl.BlockSpec((COL, COL), lambda ki, nj, mi: (ki, nj)),
        out_shape=jax.ShapeDtypeStruct((k, nn * COL), out_dtype),
        scratch=[pltpu.VMEM((COL, COL), F32)],
    )(a, *pieces)


def _mm_nt_normbwd(pieces, w, x, nw, dres, *, name, tm, row, comm=None):
    m, d = x.shape
    offs, nblk, nk = _piece_layout(pieces)

    def piece_spec(o, nb):
        return pl.BlockSpec((tm, COL), lambda i, k: (i, jnp.clip(k - o, 0, nb - 1)))

    def body(*refs):
        p_refs = refs[:len(pieces)]
        w_ref, x_ref, nw_ref, dres_ref, dx_ref, dnw_ref, acc = refs[len(pieces):]
        i, k = pl.program_id(0), pl.program_id(1)

        @pl.when(jnp.logical_and(i == 0, k == 0))
        def _():
            dnw_ref[...] = jnp.zeros_like(dnw_ref)

        @pl.when(k == 0)
        def _():
            acc[...] = jnp.zeros_like(acc)

        for p_ref, o, nb in zip(p_refs, offs, nblk):
            @pl.when(jnp.logical_and(k >= o, k < o + nb))
            def _(p_ref=p_ref):
                acc[...] += _dot_nt(_bf(p_ref[...]), w_ref[...])

        @pl.when(k == nk - 1)
        def _():
            dx, dnw = _rms_bwd(acc[...], x_ref[...], nw_ref[...])
            dx_ref[...] = dres_ref[...] + dx
            _row_acc(dnw_ref, row, dnw)

    return _call(
        body, name=name, grid=(m // tm, nk), comm=comm,
        in_specs=[piece_spec(o, nb) for o, nb in zip(offs, nblk)]
        + [pl.BlockSpec((d, COL), lambda i, k: (0, k)),
           pl.BlockSpec((tm, d), lambda i, k: (i, 0)),
           pl.BlockSpec((1, d), lambda i, k: (0, 0)),
           pl.BlockSpec((tm, d), lambda i, k: (i, 0))],
        out_specs=[pl.BlockSpec((tm, d), lambda i, k: (i, 0)),
                   pl.BlockSpec((8, d), lambda i, k: (0, 0))],
        out_shape=[jax.ShapeDtypeStruct((m, d), F32), jax.ShapeDtypeStruct((8, d), F32)],
        scratch=[pltpu.VMEM((tm, d), F32)],
    )(*pieces, w, x, nw, dres)


def _rope_tables(pos3, invf):
    b, s, _ = pos3.shape

    def body(pos_ref, invf_ref, cos_ref, sin_ref):
        half_t, half_d = s // 2, RET_DK // 2
        pos = pos_ref[...].astype(F32)
        low = lax.broadcasted_iota(jnp.int32, (half_t, RET_DK), 1) < half_d
        ang = jnp.where(low, pos[0:half_t], pos[half_t:]) * invf_ref[...]
        co, si = jnp.cos(ang), jnp.sin(ang)
        co_turned, si_turned = pltpu.roll(co, half_d, 1), pltpu.roll(si, half_d, 1)
        sign = jnp.where(low, -1.0, 1.0)
        cos_ref[...] = jnp.concatenate([jnp.where(low, co, co_turned), jnp.where(low, co_turned, co)], axis=0)
        sin_ref[...] = jnp.concatenate([sign * jnp.where(low, si, si_turned), sign * jnp.where(low, si_turned, si)],
                                       axis=0)

    spec = pl.BlockSpec((None, s, RET_DK), lambda i: (i, 0, 0))
    return _call(
        body, name="rope_tables", grid=(b,),
        in_specs=[pl.BlockSpec((None, s, 1), lambda i: (i, 0, 0)),
                  pl.BlockSpec((1, RET_DK), lambda i: (0, 0))],
        out_specs=[spec, spec],
        out_shape=[jax.ShapeDtypeStruct((b, s, RET_DK), F32)] * 2,
    )(pos3, invf)


def _log_gamma(h):
    return float(np.log1p(-np.power(np.float32(2.0), np.float32(-5.0 - h))).astype(np.float32))


def _decay_matrix(h):
    lg = _log_gamma(h)
    n = lax.broadcasted_iota(jnp.int32, (SEQ_T, SEQ_T), 0)
    m = lax.broadcasted_iota(jnp.int32, (SEQ_T, SEQ_T), 1)
    same = (n // REF_CHUNK) == (m // REF_CHUNK)
    dist = jnp.where(same, jnp.abs(n - m), n - m).astype(F32)
    return jnp.where(jnp.logical_or(same, m < n), jnp.exp(lg * dist), 0.0)


def _decay_vectors(h):
    lg = _log_gamma(h)
    idx = lax.broadcasted_iota(jnp.int32, (SEQ_T, 1), 0).astype(F32)
    qd = jnp.exp(lg * (idx + 1.0))
    kd = jnp.exp(lg * (SEQ_T - 1.0 - idx))
    return qd, kd, math.exp(lg * SEQ_T)


def _rotate(x, cos, sin):
    return x * cos + pltpu.roll(x, RET_DK // 2, 1) * sin


def _rotate_bwd(d, cos, sin):
    return d * cos + pltpu.roll(d * sin, RET_DK // 2, 1)


def _ret_head(p_ref, cos, sin, h):
    q = p_ref[:, h * RET_DK:(h + 1) * RET_DK].astype(F32)
    k = p_ref[:, 512 + h * RET_DK:512 + (h + 1) * RET_DK].astype(F32)
    v = p_ref[:, 1024 + h * RET_DV:1024 + (h + 1) * RET_DV]
    g = p_ref[:, 2048 + h * RET_DV:2048 + (h + 1) * RET_DV].astype(F32)
    qr = _rotate(q, cos, sin)
    kr = _rotate(k, cos, sin) * (RET_DK ** -0.5)
    return qr, kr, v, g


def _group_norm(o):
    mu = jnp.mean(o, axis=-1, keepdims=True)
    oc = o - mu
    rstd = lax.rsqrt(jnp.mean(oc * oc, axis=-1, keepdims=True) + GN_EPS)
    return oc * rstd, rstd


def _retention_fwd(proj3, cos, sin, gnw, comm=None):
    b, s, _ = proj3.shape
    nc = s // SEQ_T

    def body(p_ref, cos_ref, sin_ref, gnw_ref, a_ref, st_ref, state, wtab):
        c = pl.program_id(1)

        @pl.when(jnp.logical_and(pl.program_id(0) == 0, c == 0))
        def _():
            for h in range(RET_HEADS):
                wtab[h] = _decay_matrix(h)

        @pl.when(c == 0)
        def _():
            state[...] = jnp.zeros_like(state)

        cs, sn = cos_ref[...], sin_ref[...]
        st_ref[...] = state[...]
        outs, states = [], []
        for h in range(RET_HEADS):
            qd, kd, gt = _decay_vectors(h)
            qr, kr, v, g = _ret_head(p_ref, cs, sn, h)
            st = state[h]
            p = _dot_nt(_bf(qr), _bf(kr)) * wtab[h]
            o = _dot(_bf(p), _bf(v)) + _dot(_bf(qr * qd), _bf(st))
            states.append(st * gt + _dot_tn(_bf(kr * kd), _bf(v)))
            on, _ = _group_norm(o)
            gw = gnw_ref[:, h * RET_DV:(h + 1) * RET_DV]
            outs.append(_bf(on * gw * (g * _sigmoid(g))))
        a_ref[...] = jnp.concatenate(outs, axis=1)
        state[...] = jnp.stack(states)

    tab = pl.BlockSpec((None, SEQ_T, RET_DK), lambda i, c: (i, c, 0))
    return _call(
        body, name="retention_fwd", grid=(b, nc), comm=comm,
        in_specs=[pl.BlockSpec((None, SEQ_T, 3072), lambda i, c: (i, c, 0)), tab, tab,
                  pl.BlockSpec((1, D_MODEL), lambda i, c: (0, 0))],
        out_specs=[pl.BlockSpec((None, SEQ_T, D_MODEL), lambda i, c: (i, c, 0)),
                   pl.BlockSpec((None, None, RET_HEADS, RET_DK, RET_DV), lambda i, c: (i, c, 0, 0, 0))],
        out_shape=[jax.ShapeDtypeStruct((b, s, D_MODEL), BF16),
                   jax.ShapeDtypeStruct((b, nc, RET_HEADS, RET_DK, RET_DV), F32)],
        scratch=[pltpu.VMEM((RET_HEADS, RET_DK, RET_DV), F32),
                 pltpu.VMEM((RET_HEADS, SEQ_T, SEQ_T), F32)],
    )(proj3, cos, sin, gnw)


def _retention_bwd(proj3, cos, sin, gnw, states, da3, comm=None):
    b, s, _ = proj3.shape
    nc = s // SEQ_T

    def body(p_ref, cos_ref, sin_ref, gnw_ref, st_ref, da_ref, d_ref, dgn_ref, dstate, wtab):
        c = pl.program_id(1)

        @pl.when(jnp.logical_and(pl.program_id(0) == 0, c == 0))
        def _():
            dgn_ref[...] = jnp.zeros_like(dgn_ref)
            for h in range(RET_HEADS):
                wtab[h] = _decay_matrix(h)

        @pl.when(c == 0)
        def _():
            dstate[...] = jnp.zeros_like(dstate)

        cs, sn = cos_ref[...], sin_ref[...]
        dqs, dks, dvs, dgs, dgns, dstates = [], [], [], [], [], []
        for h in range(RET_HEADS):
            qd, kd, gt = _decay_vectors(h)
            qr, kr, v, g = _ret_head(p_ref, cs, sn, h)
            st, dst, w = st_ref[h], dstate[h], wtab[h]
            qb, kb, vb = _bf(qr), _bf(kr), _bf(v)
            p = _dot_nt(qb, kb) * w
            o = _dot(_bf(p), vb) + _dot(_bf(qr * qd), _bf(st))
            on, rstd = _group_norm(o)
            gw = gnw_ref[:, h * RET_DV:(h + 1) * RET_DV]
            da = da_ref[:, h * RET_DV:(h + 1) * RET_DV].astype(F32)
            sg = _sigmoid(g)
            silu = g * sg
            dg = da * on * gw * (sg * (1.0 + g * (1.0 - sg)))
            dgns.append(jnp.sum(da * silu * on, axis=0, keepdims=True))
            don = da * silu * gw
            do = rstd * (don - jnp.mean(don, axis=-1, keepdims=True)
                         - on * jnp.mean(don * on, axis=-1, keepdims=True))
            dob = _bf(do)
            dp = _dot_nt(dob, vb) * w
            dqr = _dot(_bf(dp), kb) + _dot_nt(dob, _bf(st)) * qd
            dkr = _dot_tn(_bf(dp), qb) + _dot_nt(vb, _bf(dst)) * kd
            dv = _dot_tn(_bf(p), dob) + _dot(_bf(kr * kd), _bf(dst))
            dstates.append(dst * gt + _dot_tn(_bf(qr * qd), dob))
            dqs.append(_bf(_rotate_bwd(dqr, cs, sn)))
            dks.append(_bf(_rotate_bwd(dkr, cs, sn) * (RET_DK ** -0.5)))
            dvs.append(_bf(dv))
            dgs.append(_bf(dg))
        d_ref[...] = jnp.concatenate(dqs + dks + dvs + dgs, axis=1)
        _row_acc(dgn_ref, ROW_GN, jnp.concatenate(dgns, axis=1))
        dstate[...] = jnp.stack(dstates)

    rev = lambda i, c: (i, nc - 1 - c, 0)
    tab = pl.BlockSpec((None, SEQ_T, RET_DK), rev)
    return _call(
        body, name="retention_bwd", grid=(b, nc), comm=comm,
        in_specs=[pl.BlockSpec((None, SEQ_T, 3072), rev), tab, tab,
                  pl.BlockSpec((1, D_MODEL), lambda i, c: (0, 0)),
                  pl.BlockSpec((None, None, RET_HEADS, RET_DK, RET_DV), lambda i, c: (i, nc - 1 - c, 0, 0, 0)),
                  pl.BlockSpec((None, SEQ_T, D_MODEL), rev)],
        out_specs=[pl.BlockSpec((None, SEQ_T, 3072), rev),
                   pl.BlockSpec((8, D_MODEL), lambda i, c: (0, 0))],
        out_shape=[jax.ShapeDtypeStruct((b, s, 3072), BF16), jax.ShapeDtypeStruct((8, D_MODEL), F32)],
        scratch=[pltpu.VMEM((RET_HEADS, RET_DK, RET_DV), F32),
                 pltpu.VMEM((RET_HEADS, SEQ_T, SEQ_T), F32)],
    )(proj3, cos, sin, gnw, states, da3)


def _softplus_neg(lam):
    z = -lam
    u = jnp.exp(-jnp.abs(z))
    log1p_u = jnp.where(u < 0.01, u * (1.0 - u * (0.5 - u * (1.0 / 3.0))), jnp.log(1.0 + u))
    return jnp.maximum(z, 0.0) + log1p_u


def _lru_coeffs(xc, wr_ref, br_ref, wi_ref, bi_ref, lam_ref):
    rs, is_ = [], []
    for n in range(LRU_BLOCKS):
        xb = _bf(xc[:, n * LRU_BLOCK:(n + 1) * LRU_BLOCK])
        cols = slice(n * LRU_BLOCK, (n + 1) * LRU_BLOCK)
        rs.append(_sigmoid(_dot(xb, wr_ref[n]) + br_ref[:, cols]))
        is_.append(_sigmoid(_dot(xb, wi_ref[n]) + bi_ref[:, cols]))
    r = jnp.concatenate(rs, axis=1)
    i = jnp.concatenate(is_, axis=1)
    sp = _softplus_neg(lam_ref[...])
    la = -LRU_C * r * sp
    a = jnp.exp(la)
    s = jnp.sqrt(-jnp.tanh(la) * (a * a + 1.0))
    return r, i, a, s, sp


_LRU_PARAM_SPECS = [
    pl.BlockSpec((4, D_MODEL), lambda i, c: (0, 0)),
    pl.BlockSpec((1, D_MODEL), lambda i, c: (0, 0)),
    pl.BlockSpec((LRU_BLOCKS, LRU_BLOCK, LRU_BLOCK), lambda i, c: (0, 0, 0)),
    pl.BlockSpec((1, D_MODEL), lambda i, c: (0, 0)),
    pl.BlockSpec((LRU_BLOCKS, LRU_BLOCK, LRU_BLOCK), lambda i, c: (0, 0, 0)),
    pl.BlockSpec((1, D_MODEL), lambda i, c: (0, 0)),
    pl.BlockSpec((1, D_MODEL), lambda i, c: (0, 0)),
]


def _lru_fwd(proj3, params, comm=None):
    b, s, _ = proj3.shape
    nc = s // SEQ_T

    def body(x_ref, y_ref, cw, cb, wr, br, wi, bi, lam,
             o_ref, h_ref, xc_ref, a_ref, s_ref, gy_ref, hdg_ref, r_ref, i_ref, xprev, hprev):
        @pl.when(pl.program_id(1) == 0)
        def _():
            xprev[...] = jnp.zeros_like(xprev)
            hprev[...] = jnp.zeros_like(hprev)

        x = x_ref[...].astype(F32)
        prev8 = xprev[...]
        xc = cb[...] + sum(cw[j:j + 1, :] * _shift_down(x, 3 - j, prev8) for j in range(4))
        xprev[...] = x[SEQ_T - 8:]
        xc_ref[...] = xc
        r, i, a, s_, _ = _lru_coeffs(xc, wr, br, wi, bi, lam)
        a_ref[...] = a
        s_ref[...] = s_
        r_ref[...] = _bf(r)
        i_ref[...] = _bf(i)
        h = _scan_fwd(a, s_ * (i * xc), hprev[7:8, :])
        hprev[...] = h[SEQ_T - 8:]
        h_ref[...] = h
        gy, dgy = _gelu_and_grad(y_ref[...].astype(F32))
        o_ref[...] = _bf(h * gy)
        gy_ref[...] = _bf(gy)
        hdg_ref[...] = _bf(h * dgy)

    out = pl.BlockSpec((None, SEQ_T, D_MODEL), lambda i, c: (i, c, 0))
    half, full = jax.ShapeDtypeStruct((b, s, D_MODEL), BF16), jax.ShapeDtypeStruct((b, s, D_MODEL), F32)
    return _call(
        body, name="lru_fwd", grid=(b, nc), comm=comm,
        in_specs=[pl.BlockSpec((None, SEQ_T, D_MODEL), lambda i, c: (i, c, 3)),
                  pl.BlockSpec((None, SEQ_T, D_MODEL), lambda i, c: (i, c, 4))] + _LRU_PARAM_SPECS,
        out_specs=[out] * 9, out_shape=[half, full, full, full, full, half, half, half, half],
        scratch=[pltpu.VMEM((8, D_MODEL), F32), pltpu.VMEM((8, D_MODEL), F32)],
    )(proj3, proj3, *params)


def _lru_bwd(proj3, params, kept, db3, comm=None):
    b, s, _ = proj3.shape
    nc = s // SEQ_T
    blk8 = SEQ_T // 8
    hseq = kept[0]

    def body(x_ref, h_ref, xc_ref, a_ref, s_ref, gy_ref, hdg_ref, r_ref, i_ref, hp_ref, db_ref,
             cw, cb, wr, br, wi, bi, lam, d_ref, dwr_ref, dwi_ref, sm_ref, gnext, anext, dxcnext):
        c = pl.program_id(1)
        first_chunk = c == nc - 1

        @pl.when(jnp.logical_and(pl.program_id(0) == 0, c == 0))
        def _():
            dwr_ref[...] = jnp.zeros_like(dwr_ref)
            dwi_ref[...] = jnp.zeros_like(dwi_ref)
            sm_ref[...] = jnp.zeros_like(sm_ref)

        @pl.when(c == 0)
        def _():
            gnext[...] = jnp.zeros_like(gnext)
            anext[...] = jnp.zeros_like(anext)
            dxcnext[...] = jnp.zeros_like(dxcnext)

        x, xc, h = x_ref[...].astype(F32), xc_ref[...], h_ref[...]
        hprev = hp_ref[...] * jnp.where(first_chunk, 0.0, 1.0)
        r, i, a, s_ = r_ref[...].astype(F32), i_ref[...].astype(F32), a_ref[...], s_ref[...]
        sp = _softplus_neg(lam[...])
        db = db_ref[...].astype(F32)
        dy = db * hdg_ref[...].astype(F32)
        a_up = _shift_up(a, 1, anext[...])
        g = _scan_bwd(a_up, db * gy_ref[...].astype(F32), gnext[0:1, :])
        gnext[...] = g[0:8]
        anext[...] = a[0:8]
        da = g * _shift_down(h, 1, hprev)
        ixc = i * xc
        dla = da * a - (g * ixc) * (a * a) / s_
        di = g * s_ * xc
        dxc = g * s_ * i
        dzr = dla * (-LRU_C * sp) * r * (1.0 - r)
        dzi = di * i * (1.0 - i)
        lam_v = lam[...]
        _row_acc(sm_ref, 7, jnp.sum(dla * (LRU_C * r), axis=0, keepdims=True) * _sigmoid(-lam_v))
        _row_acc(sm_ref, 5, jnp.sum(dzr, axis=0, keepdims=True))
        _row_acc(sm_ref, 6, jnp.sum(dzi, axis=0, keepdims=True))
        parts, dwr_parts, dwi_parts = [], [], []
        for n in range(LRU_BLOCKS):
            cols = slice(n * LRU_BLOCK, (n + 1) * LRU_BLOCK)
            xb, zr, zi = _bf(xc[:, cols]), _bf(dzr[:, cols]), _bf(dzi[:, cols])
            parts.append(dxc[:, cols] + _dot_nt(zr, wr[n]) + _dot_nt(zi, wi[n]))
            dwr_parts.append(_dot_tn(xb, zr))
            dwi_parts.append(_dot_tn(xb, zi))
        dwr_ref[...] += jnp.stack(dwr_parts)
        dwi_ref[...] += jnp.stack(dwi_parts)
        dxc = jnp.concatenate(parts, axis=1)
        _row_acc(sm_ref, 4, jnp.sum(dxc, axis=0, keepdims=True))
        nxt = dxcnext[...]
        dx = jnp.zeros_like(x)
        for j in range(4):
            ahead = _shift_up(dxc, 3 - j, nxt)
            dx = dx + cw[j:j + 1, :] * ahead
            _row_acc(sm_ref, j, jnp.sum(ahead * x, axis=0, keepdims=True))
        dxcnext[...] = dxc[0:8]
        d_ref[:, 0:D_MODEL] = _bf(dx)
        d_ref[:, D_MODEL:2 * D_MODEL] = _bf(dy)

    rev = lambda col: (lambda i, c: (i, nc - 1 - c, col))
    prev = lambda col: (lambda i, c: (i, jnp.maximum((nc - 1 - c) * blk8 - 1, 0), col))
    return _call(
        body, name="lru_bwd", grid=(b, nc), comm=comm,
        in_specs=[pl.BlockSpec((None, SEQ_T, D_MODEL), rev(3))]
        + [pl.BlockSpec((None, SEQ_T, D_MODEL), rev(0))] * len(kept)
        + [pl.BlockSpec((None, 8, D_MODEL), prev(0)), pl.BlockSpec((None, SEQ_T, D_MODEL), rev(0))]
        + _LRU_PARAM_SPECS,
        out_specs=[pl.BlockSpec((None, SEQ_T, 2 * D_MODEL), rev(0)),
                   pl.BlockSpec((LRU_BLOCKS, LRU_BLOCK, LRU_BLOCK), lambda i, c: (0, 0, 0)),
                   pl.BlockSpec((LRU_BLOCKS, LRU_BLOCK, LRU_BLOCK), lambda i, c: (0, 0, 0)),
                   pl.BlockSpec((8, D_MODEL), lambda i, c: (0, 0))],
        out_shape=[jax.ShapeDtypeStruct((b, s, 2 * D_MODEL), BF16),
                   jax.ShapeDtypeStruct((LRU_BLOCKS, LRU_BLOCK, LRU_BLOCK), F32),
                   jax.ShapeDtypeStruct((LRU_BLOCKS, LRU_BLOCK, LRU_BLOCK), F32),
                   jax.ShapeDtypeStruct((8, D_MODEL), F32)],
        scratch=[pltpu.VMEM((8, D_MODEL), F32)] * 3,
    )(proj3, *kept, hseq, db3, *params)


def _merge_parts(a_ref, b_ref, gr_ref, gl_ref, mgb_ref, wro_ref, wlo_ref):
    ya = _dot(a_ref[...], wro_ref[...])
    yb = _dot(b_ref[...], wlo_ref[...])
    sa = _sigmoid(gr_ref[...].astype(F32) + mgb_ref[0:1, :])
    sb = _sigmoid(gl_ref[...].astype(F32) + mgb_ref[1:2, :])
    return ya, yb, sa, sb


def _merge_specs(tm):
    row = lambda col: pl.BlockSpec((tm, D_MODEL), lambda i: (i, col))
    full = pl.BlockSpec((D_MODEL, D_MODEL), lambda i: (0, 0))
    return row, full


def _merge_fwd(a_in, b_in, proj, mgb, wro, wlo, wout, x, *, tm, comm=None):
    m = x.shape[0]
    row, full = _merge_specs(tm)

    def body(a_ref, b_ref, gr_ref, gl_ref, mgb_ref, wro_ref, wlo_ref, wout_ref, x_ref,
             o_ref, mix_ref, ya_ref, yb_ref):
        ya, yb, sa, sb = _merge_parts(a_ref, b_ref, gr_ref, gl_ref, mgb_ref, wro_ref, wlo_ref)
        mix = _bf(sa * ya + sb * yb)
        o_ref[...] = x_ref[...] + _dot(mix, wout_ref[...])
        mix_ref[...] = mix
        ya_ref[...] = _bf(ya)
        yb_ref[...] = _bf(yb)

    act = jax.ShapeDtypeStruct((m, D_MODEL), BF16)
    return _call(
        body, name="merge_fwd", grid=(m // tm,), comm=comm,
        in_specs=[row(0), row(0), row(5), row(6), pl.BlockSpec((2, D_MODEL), lambda i: (0, 0)),
                  full, full, full, row(0)],
        out_specs=[row(0)] * 4,
        out_shape=[jax.ShapeDtypeStruct((m, D_MODEL), F32), act, act, act],
    )(a_in, b_in, proj, proj, mgb, wro, wlo, wout, x)


def _merge_bwd(ya, yb, proj, mgb, wro, wlo, wout, dx2, *, tm):
    m = dx2.shape[0]
    row, full = _merge_specs(tm)

    def body(ya_ref, yb_ref, gr_ref, gl_ref, mgb_ref, wro_ref, wlo_ref, wout_ref, dx_ref,
             dya_ref, dyb_ref, da_ref, db_ref, dg_ref, sm_ref):
        @pl.when(pl.program_id(0) == 0)
        def _():
            sm_ref[...] = jnp.zeros_like(sm_ref)

        ya, yb = ya_ref[...].astype(F32), yb_ref[...].astype(F32)
        sa = _sigmoid(gr_ref[...].astype(F32) + mgb_ref[0:1, :])
        sb = _sigmoid(gl_ref[...].astype(F32) + mgb_ref[1:2, :])
        dmix = _dot_nt(_bf(dx_ref[...]), wout_ref[...])
        dya, dyb = _bf(dmix * sa), _bf(dmix * sb)
        dya_ref[...] = dya
        dyb_ref[...] = dyb
        dga = dmix * ya * sa * (1.0 - sa)
        dgb = dmix * yb * sb * (1.0 - sb)
        dg_ref[:, 0:D_MODEL] = _bf(dga)
        dg_ref[:, D_MODEL:2 * D_MODEL] = _bf(dgb)
        _row_acc(sm_ref, ROW_MGB, jnp.sum(dga, axis=0, keepdims=True))
        _row_acc(sm_ref, ROW_MGB + 1, jnp.sum(dgb, axis=0, keepdims=True))
        da_ref[...] = _bf(_dot_nt(dya, wro_ref[...]))
        db_ref[...] = _bf(_dot_nt(dyb, wlo_ref[...]))

    act = jax.ShapeDtypeStruct((m, D_MODEL), BF16)
    return _call(
        body, name="merge_bwd", grid=(m // tm,),
        in_specs=[row(0), row(0), row(5), row(6), pl.BlockSpec((2, D_MODEL), lambda i: (0, 0)),
                  full, full, full, row(0)],
        out_specs=[row(0)] * 4 + [pl.BlockSpec((tm, 2 * D_MODEL), lambda i: (i, 0)),
                                  pl.BlockSpec((8, D_MODEL), lambda i: (0, 0))],
        out_shape=[act] * 4 + [jax.ShapeDtypeStruct((m, 2 * D_MODEL), BF16),
                               jax.ShapeDtypeStruct((8, D_MODEL), F32)],
    )(ya, yb, proj, proj, mgb, wro, wlo, wout, dx2)


def _ffn_act_fwd(up3, cw, cb):
    b, s, _ = up3.shape

    def body(g_ref, v_ref, cw_ref, cb_ref, o_ref, act_ref, q_ref, gprev):
        @pl.when(pl.program_id(1) == 0)
        def _():
            gprev[...] = jnp.zeros_like(gprev)

        gate, val = g_ref[...].astype(F32), v_ref[...].astype(F32)
        prev8 = gprev[...]
        gc = cb_ref[...] + sum(cw_ref[j:j + 1, :] * _shift_down(gate, 2 - j, prev8) for j in range(3))
        gprev[...] = gate[SEQ_T - 8:]
        act, dact = _gelu_and_grad(gc)
        o_ref[...] = _bf(act * val)
        act_ref[...] = _bf(act)
        q_ref[...] = _bf(dact * val)

    out = pl.BlockSpec((None, SEQ_T, D_FF), lambda i, c: (i, c, 0))
    return _call(
        body, name="ffn_act_fwd", grid=(b, s // SEQ_T),
        in_specs=[pl.BlockSpec((None, SEQ_T, D_FF), lambda i, c: (i, c, 0)),
                  pl.BlockSpec((None, SEQ_T, D_FF), lambda i, c: (i, c, 1)),
                  pl.BlockSpec((3, D_FF), lambda i, c: (0, 0)),
                  pl.BlockSpec((1, D_FF), lambda i, c: (0, 0))],
        out_specs=[out] * 3,
        out_shape=[jax.ShapeDtypeStruct((b, s, D_FF), BF16)] * 3,
        scratch=[pltpu.VMEM((8, D_FF), F32)],
    )(up3, up3, cw, cb)


def _ffn_act_bwd(up3, act3, q3, cw, df3, comm=None):
    b, s, _ = up3.shape
    nc = s // SEQ_T

    def body(g_ref, act_ref, q_ref, df_ref, cw_ref, dg_ref, dv_ref, sm_ref, dgcnext):
        c = pl.program_id(1)

        @pl.when(jnp.logical_and(pl.program_id(0) == 0, c == 0))
        def _():
            sm_ref[...] = jnp.zeros_like(sm_ref)

        @pl.when(c == 0)
        def _():
            dgcnext[...] = jnp.zeros_like(dgcnext)

        gate = g_ref[...].astype(F32)
        df = df_ref[...].astype(F32)
        dv_ref[...] = _bf(df * act_ref[...].astype(F32))
        dgc = df * q_ref[...].astype(F32)
        nxt = dgcnext[...]
        dgate = jnp.zeros_like(gate)
        for j in range(3):
            ahead = _shift_up(dgc, 2 - j, nxt)
            dgate = dgate + cw_ref[j:j + 1, :] * ahead
            _row_acc(sm_ref, j, jnp.sum(ahead * gate, axis=0, keepdims=True))
        _row_acc(sm_ref, 3, jnp.sum(dgc, axis=0, keepdims=True))
        dgcnext[...] = dgc[0:8]
        dg_ref[...] = _bf(dgate)

    rev = pl.BlockSpec((None, SEQ_T, D_FF), lambda i, c: (i, nc - 1 - c, 0))
    return _call(
        body, name="ffn_act_bwd", grid=(b, nc), comm=comm,
        in_specs=[rev, rev, rev, rev, pl.BlockSpec((3, D_FF), lambda i, c: (0, 0))],
        out_specs=[rev, rev, pl.BlockSpec((8, D_FF), lambda i, c: (0, 0))],
        out_shape=[jax.ShapeDtypeStruct((b, s, D_FF), BF16)] * 2 + [jax.ShapeDtypeStruct((8, D_FF), F32)],
        scratch=[pltpu.VMEM((8, D_FF), F32)],
    )(up3, act3, q3, df3, cw)


def _ffn_down_loss(f, wd, x2, nfw, target, *, tm):
    m, kf = f.shape
    nt = m // tm

    def body(f_ref, wd_ref, x_ref, nw_ref, t_ref, dx_ref, dnw_ref, lsum):
        i = pl.program_id(0)

        @pl.when(i == 0)
        def _():
            dnw_ref[...] = jnp.zeros_like(dnw_ref)
            lsum[...] = jnp.zeros_like(lsum)

        x3 = x_ref[...] + _dot(f_ref[...], wd_ref[...])
        nw = nw_ref[...]
        xh, r = _rms(x3)
        err = xh * nw - t_ref[...]
        lsum[...] += jnp.sum(err * err, axis=0, keepdims=True)
        dy = err * (1.0 / D_MODEL)
        g = dy * nw
        dx_ref[...] = r * (g - xh * jnp.mean(g * xh, axis=-1, keepdims=True))
        _row_acc(dnw_ref, ROW_NF, jnp.sum(dy * xh, axis=0, keepdims=True))

        @pl.when(i == nt - 1)
        def _():
            loss = jnp.sum(lsum[...], axis=1, keepdims=True) * (0.5 / D_MODEL)
            dnw_ref[ROW_LOSS:ROW_LOSS + 1, :] = jnp.broadcast_to(loss, (1, D_MODEL))

    row = pl.BlockSpec((tm, D_MODEL), lambda i: (i, 0))
    return _call(
        body, name="ffn_down_loss", grid=(nt,),
        in_specs=[pl.BlockSpec((tm, kf), lambda i: (i, 0)),
                  pl.BlockSpec((kf, D_MODEL), lambda i: (0, 0)),
                  row, pl.BlockSpec((1, D_MODEL), lambda i: (0, 0)), row],
        out_specs=[row, pl.BlockSpec((8, D_MODEL), lambda i: (0, 0))],
        out_shape=[jax.ShapeDtypeStruct((m, D_MODEL), F32), jax.ShapeDtypeStruct((8, D_MODEL), F32)],
        scratch=[pltpu.VMEM((1, D_MODEL), F32)],
    )(f, wd, x2, nfw, target)


def _row_tile(rows):
    return next((t for t in (256, 128, 64, 32, 16, 8) if rows % t == 0), rows)


def _adamw(w, gs, m, v, *, name):
    rows, cols = w.shape
    tr = _row_tile(rows)
    ng = len(gs)

    def body(w_ref, *rest):
        g_refs, (m_ref, v_ref, g_out, d_out, m_out, v_out) = rest[:ng], rest[ng:]
        g = g_refs[0][...]
        for r in g_refs[1:]:
            g = g + r[...]
        mn = ADAM_B1 * m_ref[...] + (1.0 - ADAM_B1) * g
        vn = ADAM_B2 * v_ref[...] + (1.0 - ADAM_B2) * (g * g)
        m_hat = mn / (1.0 - ADAM_B1 ** ADAM_STEP)
        v_hat = vn / (1.0 - ADAM_B2 ** ADAM_STEP)
        g_out[...] = g
        d_out[...] = -ADAM_LR * (m_hat / (jnp.sqrt(v_hat) + ADAM_EPS) + ADAM_WD * w_ref[...])
        m_out[...] = mn
        v_out[...] = vn

    spec = pl.BlockSpec((tr, cols), lambda i: (i, 0))
    return _call(
        body, name=name, grid=(rows // tr,),
        in_specs=[spec] * (3 + ng), out_specs=[spec] * 4,
        out_shape=[jax.ShapeDtypeStruct((rows, cols), F32)] * 4,
    )(w, *gs, m, v)


def _mesh_pos():
    x, y, c = lax.axis_index("x"), lax.axis_index("y"), lax.axis_index("c")
    return x, y, c


def _other_chips(x, y, c):
    return [((1 - x, y, c), 2 * (1 - x) + y), ((x, 1 - y, c), 2 * x + 1 - y),
            ((1 - x, 1 - y, c), 2 * (1 - x) + 1 - y)]


def _region(ref, axis, size, half_axis, chip, core=None):
    idx = [slice(None)] * len(ref.shape)
    if core is None:
        idx[axis] = pl.ds(pl.multiple_of(chip * size, size), size)
    elif half_axis == axis:
        h = size // 2
        idx[axis] = pl.ds(pl.multiple_of(chip * size + core * h, h), h)
    else:
        idx[axis] = pl.ds(pl.multiple_of(chip * size, size), size)
        h = ref.shape[half_axis] // 2
        idx[half_axis] = pl.ds(pl.multiple_of(core * h, h), h)
    return ref.at[tuple(idx)]


def _half(ref, half_axis, core):
    idx = [slice(None)] * len(ref.shape)
    h = ref.shape[half_axis] // 2
    idx[half_axis] = pl.ds(pl.multiple_of(core * h, h), h)
    return ref.at[tuple(idx)]


class _Copy:
    def __init__(self, make):
        self._make = make

    def start(self):
        self._make().start()

    def wait(self):
        self._make().wait()

    def wait_send(self):
        self._make().wait_send()

    def wait_recv(self):
        self._make().wait_recv()


def _remote(src, dst, send_sem, recv_sem, dev):
    return _Copy(lambda: pltpu.make_async_remote_copy(
        src_ref=src, dst_ref=dst, send_sem=send_sem, recv_sem=recv_sem, device_id=dev, device_id_type=MESH))


def _local(src, dst, sem):
    return _Copy(lambda: pltpu.make_async_copy(src, dst, sem))


def _dma_sems(n):
    return pltpu.SemaphoreType.DMA((n,))


def _place_shard(w, chip, axis, *, name):
    shape = list(w.shape)
    shape[axis] *= N_CHIPS
    if w.ndim == 3:
        block, grid = (1,) + w.shape[1:], (w.shape[0],)
        in_map, out_map = (lambda i, chip: (i, 0, 0)), (lambda i, chip: (i, chip[0], 0))
    else:
        tr = _row_tile(w.shape[0])
        nt = w.shape[0] // tr
        block, grid = (tr, w.shape[1]), (nt,)
        in_map = lambda i, chip: (i, 0)
        out_map = (lambda i, chip: (chip[0] * nt + i, 0)) if axis == 0 else (lambda i, chip: (i, chip[0]))

    def body(chip_ref, w_ref, o_ref):
        o_ref[...] = _bf(w_ref[...])

    return _call(body, name=name, grid=grid, prefetch=1, in_specs=[pl.BlockSpec(block, in_map)],
                 out_specs=pl.BlockSpec(block, out_map),
                 out_shape=jax.ShapeDtypeStruct(tuple(shape), BF16))(chip, w)


def _ici_leg(srcs, dsts, layout, sizes, n_whole, sems):
    send_sems, recv_sems, local_sems = sems
    x, y, c = _mesh_pos()
    mine = 2 * x + y
    n_big = len(srcs) - n_whole
    local, sends, recvs = [], [], []
    for t, (src, dst) in enumerate(zip(srcs, dsts)):
        if t < n_big:
            ax, hx = layout[t]
            part = _region(src, ax, sizes[t], hx, mine, c)
            landing = lambda chip, dst=dst, ax=ax, hx=hx, size=sizes[t]: _region(dst, ax, size, hx, chip, c)
        else:
            part, landing = src, (lambda chip, dst=dst: dst.at[chip])
            local.append(_local(src, dst.at[mine], local_sems.at[t - n_big]))
        for k, (dev, chip) in enumerate(_other_chips(x, y, c)):
            sends.append(_remote(part, landing(mine), send_sems.at[3 * t + k], recv_sems.at[3 * t + k], dev))
            recvs.append(_remote(part, landing(chip), send_sems.at[3 * t + k], recv_sems.at[3 * t + k], dev))
    return local, sends, recvs


def _d2d_leg(srcs, dsts, layout, sizes, sems):
    send_sems, recv_sems = sems
    x, y, c = _mesh_pos()
    sends, recvs = [], []
    for t, (src, dst) in enumerate(zip(srcs, dsts)):
        ax, hx = layout[t]
        for k, (_, chip) in enumerate(_other_chips(x, y, c)):
            sem = (send_sems.at[3 * t + k], recv_sems.at[3 * t + k])
            sends.append(_remote(_region(src, ax, sizes[t], hx, chip, c),
                                 _region(dst, ax, sizes[t], hx, chip, c), *sem, (x, y, 1 - c)))
            recvs.append(_remote(_region(src, ax, sizes[t], hx, chip, 1 - c),
                                 _region(dst, ax, sizes[t], hx, chip, 1 - c), *sem, (x, y, 1 - c)))
    return sends, recvs


def _gather_shapes(bufs, whole):
    return ([jax.ShapeDtypeStruct(b.shape, b.dtype) for b in bufs]
            + [jax.ShapeDtypeStruct((N_CHIPS,) + w.shape, w.dtype) for w in whole])


def _gather_ici(bufs, layout):
    n = len(bufs)
    sizes = [b.shape[ax] // N_CHIPS for b, (ax, _) in zip(bufs, layout)]

    def start(ins, outs, sems):
        for cp in _ici_leg(ins, outs, layout, sizes, 0, (*sems, None))[1]:
            cp.start()

    def finish(ins, outs, sems):
        _, sends, recvs = _ici_leg(ins, outs, layout, sizes, 0, (*sems, None))
        for cp in recvs:
            cp.wait_recv()
        for cp in sends:
            cp.wait_send()

    return _Comm(bufs, _gather_shapes(bufs, ()), [_dma_sems(3 * n), _dma_sems(3 * n)], start, finish,
                 aliases={i: i for i in range(n)})


def _both(a, b):
    ni, no, ns = len(a.ins), len(a.outs), len(a.sems)

    def start(ins, outs, sems):
        a.start(ins[:ni], outs[:no], sems[:ns])
        b.start(ins[ni:], outs[no:], sems[ns:])

    def finish(ins, outs, sems):
        a.finish(ins[:ni], outs[:no], sems[:ns])
        b.finish(ins[ni:], outs[no:], sems[ns:])

    aliases = {**a.aliases, **{ni + i: no + o for i, o in b.aliases.items()}}
    return _Comm(a.ins + b.ins, a.outs + b.outs, a.sems + b.sems, start, finish, aliases)


def _gather_d2d(bufs, layout, sizes):
    n = len(bufs)

    def start(ins, outs, sems):
        for cp in _d2d_leg(ins, outs, layout, sizes, sems)[0]:
            cp.start()

    def finish(ins, outs, sems):
        sends, recvs = _d2d_leg(ins, outs, layout, sizes, sems)
        for cp in recvs:
            cp.wait_recv()
        for cp in sends:
            cp.wait_send()

    return _Comm(bufs, [jax.ShapeDtypeStruct(b.shape, b.dtype) for b in bufs],
                 [_dma_sems(3 * n), _dma_sems(3 * n)], start, finish, aliases={i: i for i in range(n)})


def _norm_bf16(x, nw, *, name, tm):
    m, d = x.shape

    def body(x_ref, nw_ref, h_ref):
        h_ref[...] = _bf(_rms(x_ref[...])[0] * nw_ref[...])

    row = pl.BlockSpec((tm, d), lambda i: (i, 0))
    return _call(body, name=name, grid=(m // tm,), in_specs=[row, pl.BlockSpec((1, d), lambda i: (0, 0))],
                 out_specs=row, out_shape=jax.ShapeDtypeStruct((m, d), BF16))(x, nw)


def _in_proj_gather(h1, w_buf, later, later_cut, small, order, *, tm):
    m, d = h1.shape
    width = w_buf.shape[1] // N_CHIPS
    nr, nl = m // tm, len(later)
    sizes = [b.shape[ax] // N_CHIPS for b, (ax, _) in zip(later, later_cut)]

    def body(order_ref, h_ref, w_in, *rest):
        later_in, small_in = rest[:nl], rest[nl]
        o_ref, w_out = rest[nl + 1], rest[nl + 2]
        later_out, small_out = rest[nl + 3:2 * nl + 3], rest[2 * nl + 3]
        wv, load_sem, ici_send, ici_recv, d2d_send, d2d_recv, l_send, l_recv, l_local = rest[2 * nl + 4:]
        s, i = pl.program_id(0), pl.program_id(1)
        x, y, c = _mesh_pos()
        mine = 2 * x + y
        peers = _other_chips(x, y, c)
        part = lambda ref, chip, core=None: _region(ref, 1, width, 0, chip, core)

        def ici(k):
            dev, chip = peers[k]
            sem = (ici_send.at[k], ici_recv.at[k])
            return (_remote(part(w_in, mine, c), part(w_out, mine, c), *sem, dev),
                    _remote(part(w_in, chip, c), part(w_out, chip, c), *sem, dev))

        def d2d(k):
            chip, sem, sib = peers[k][1], (d2d_send.at[k], d2d_recv.at[k]), (x, y, 1 - c)
            return (_remote(part(w_out, chip, c), part(w_out, chip, c), *sem, sib),
                    _remote(part(w_out, chip, 1 - c), part(w_out, chip, 1 - c), *sem, sib))

        def load(src, chip, slot):
            return _local(part(src, chip), wv.at[slot], load_sem.at[slot])

        def others():
            return _ici_leg(list(later_in) + [small_in], list(later_out) + [small_out], later_cut, sizes, 1,
                            (l_send, l_recv, l_local))

        @pl.when(jnp.logical_and(s == 0, i == 0))
        def _():
            for k in range(3):
                ici(k)[0].start()
            local, sends, _ = others()
            for cp in local + sends:
                cp.start()
            load(w_in, mine, 0).start()
            load(w_in, mine, 0).wait()

        for k in range(3):
            @pl.when(jnp.logical_and(s == k + 1, i == 0))
            def _(k=k):
                load(w_out, peers[k][1], (k + 1) % 2).wait()

        o_ref[...] = _bf(_dot(h_ref[...], wv[s % 2]))

        for k in range(3):
            @pl.when(jnp.logical_and(s == k, i == (nr - 1 if k == 0 else max(nr - 2, 0))))
            def _(k=k):
                ici(k)[1].wait_recv()
                d2d(k)[0].start()

        @pl.when(i == nr - 1)
        def _():
            for k in range(3):
                @pl.when(s == k)
                def _(k=k):
                    d2d(k)[1].wait_recv()
                    load(w_out, peers[k][1], (k + 1) % 2).start()

            @pl.when(s == 3)
            def _():
                for k in range(3):
                    ici(k)[0].wait_send()
                    d2d(k)[0].wait_send()
                local, sends, recvs = others()
                for cp in recvs:
                    cp.wait_recv()
                for cp in sends:
                    cp.wait_send()
                for cp in local:
                    cp.wait()

    any_spec = pl.BlockSpec(memory_space=pl.ANY)
    n_any = nl + 2
    outs = _call(
        body, name="in_proj", grid=(N_CHIPS, nr), prefetch=1,
        in_specs=[pl.BlockSpec((tm, d), lambda s, i, order: (i, 0))] + [any_spec] * n_any,
        out_specs=[pl.BlockSpec((tm, width), lambda s, i, order: (i, order[s]))] + [any_spec] * n_any,
        out_shape=[jax.ShapeDtypeStruct((m, w_buf.shape[1]), BF16)] + _gather_shapes([w_buf] + list(later), [small]),
        scratch=[pltpu.VMEM((2, d, width), BF16), _dma_sems(2), _dma_sems(3), _dma_sems(3), _dma_sems(3),
                 _dma_sems(3), _dma_sems(3 * (nl + 1)), _dma_sems(3 * (nl + 1)), _dma_sems(1)],
        aliases={2 + t: 1 + t for t in range(nl + 1)},
    )(order, h1, w_buf, *later, small)
    return outs[0], outs[1], list(outs[2:2 + nl]), outs[2 + nl]


def _exchange(grads, layout):
    n = len(grads)
    others = N_DEV - 1
    sizes = [g.shape[ax] // N_CHIPS for g, (ax, _) in zip(grads, layout)]
    out_shapes = []
    for g, (ax, hx), sz in zip(grads, layout, sizes):
        shape = list(g.shape)
        shape[ax] = sz
        shape[hx] //= 2
        out_shapes.append(jax.ShapeDtypeStruct((others,) + tuple(shape), g.dtype))

    def copies(ins, outs, sems):
        send_sems, recv_sems = sems
        x, y, c = _mesh_pos()
        sends, recvs = [], []
        for t, (src, dst) in enumerate(zip(ins, outs)):
            ax, hx = layout[t]
            for r in range(1, N_DEV):
                px = (1 - x) if r & 4 else x
                py = (1 - y) if r & 2 else y
                pc = (1 - c) if r & 1 else c
                sem = (send_sems.at[others * t + r - 1], recv_sems.at[others * t + r - 1])
                part = _region(src, ax, sizes[t], hx, 2 * px + py, pc)
                sends.append(_remote(part, dst.at[r - 1], *sem, (px, py, pc)))
                recvs.append(_remote(part, dst.at[r - 1], *sem, (px, py, pc)))
        return sends, recvs

    def start(ins, outs, sems):
        for cp in copies(ins, outs, sems)[0]:
            cp.start()

    def finish(ins, outs, sems):
        sends, recvs = copies(ins, outs, sems)
        for cp in recvs:
            cp.wait_recv()
        for cp in sends:
            cp.wait_send()

    return _Comm(grads, out_shapes, [_dma_sems(others * n), _dma_sems(others * n)], start, finish)


def _reduce_half(g, parts, pos, cut, *, name):
    ax, _ = cut
    others = parts.shape[0]
    if g.ndim == 3:
        nb, rows, cols = g.shape
        hb = nb // 2
        block, grid, out_shape = (1, rows // N_CHIPS, cols), (hb,), (nb, rows // N_CHIPS, cols)
        g_map = lambda i, pos: (pos[1] * hb + i, pos[0], 0)
        o_map = lambda i, pos: (pos[1] * hb + i, 0, 0)
        p_map = lambda i, pos: (0, i, 0, 0)
    elif ax == 1:
        rows, cols = g.shape
        tr = _row_tile(rows // 2)
        nt = rows // 2 // tr
        block, grid, out_shape = (tr, cols // N_CHIPS), (nt,), (rows, cols // N_CHIPS)
        g_map = lambda i, pos: (pos[1] * nt + i, pos[0])
        o_map = lambda i, pos: (pos[1] * nt + i, 0)
        p_map = lambda i, pos: (0, i, 0)
    else:
        rows, cols = g.shape
        tr = _row_tile(rows // N_CHIPS // 2)
        nt = rows // N_CHIPS // 2 // tr
        block, grid, out_shape = (tr, cols), (nt,), (rows // N_CHIPS, cols)
        g_map = lambda i, pos: (pos[0] * 2 * nt + pos[1] * nt + i, 0)
        o_map = lambda i, pos: (pos[1] * nt + i, 0)
        p_map = lambda i, pos: (0, i, 0)

    def body(pos_ref, g_ref, p_ref, o_ref):
        acc = g_ref[...].astype(F32)
        for r in range(others):
            acc = acc + p_ref[r].astype(F32)
        o_ref[...] = acc

    return _call(
        body, name=name, grid=grid, prefetch=1,
        in_specs=[pl.BlockSpec(block, g_map), pl.BlockSpec((others,) + block, p_map)],
        out_specs=pl.BlockSpec(block, o_map), out_shape=jax.ShapeDtypeStruct(out_shape, F32),
    )(pos, g, parts)


def _join_halves(bufs):
    return _call(None, name="join_halves", comm=_join_comm(bufs))()[1]


def _join_comm(bufs):
    n = len(bufs)

    def copies(ins, outs, sems):
        send_sems, recv_sems = sems
        x, y, c = _mesh_pos()
        sends = [_remote(_half(src, 0, c), _half(dst, 0, c), send_sems.at[t], recv_sems.at[t], (x, y, 1 - c))
                 for t, (src, dst) in enumerate(zip(ins, outs))]
        recvs = [_remote(_half(src, 0, 1 - c), _half(dst, 0, 1 - c), send_sems.at[t], recv_sems.at[t],
                         (x, y, 1 - c)) for t, (src, dst) in enumerate(zip(ins, outs))]
        return sends, recvs

    def start(ins, outs, sems):
        for cp in copies(ins, outs, sems)[0]:
            cp.start()

    def finish(ins, outs, sems):
        sends, recvs = copies(ins, outs, sems)
        for cp in recvs:
            cp.wait_recv()
        for cp in sends:
            cp.wait_send()

    return _Comm(bufs, [jax.ShapeDtypeStruct(b.shape, b.dtype) for b in bufs],
                 [_dma_sems(n), _dma_sems(n)], start, finish, aliases={i: i for i in range(n)})


def _allreduce_small(pack):
    rows, cols = pack.shape

    def body(p_ref, o_ref, slots, send_sems, recv_sems):
        x, y, c = _mesh_pos()
        me = 4 * x + 2 * y + c
        slots[me] = p_ref[...]
        copies = []
        for r in range(1, N_DEV):
            fx, fy, fc = (r >> 2) & 1, (r >> 1) & 1, r & 1
            dev = ((1 - x) if fx else x, (1 - y) if fy else y, (1 - c) if fc else c)
            cp = pltpu.make_async_remote_copy(
                src_ref=p_ref, dst_ref=slots.at[me], send_sem=send_sems.at[r - 1],
                recv_sem=recv_sems.at[r - 1], device_id=dev, device_id_type=MESH)
            cp.start()
            copies.append(cp)
        for cp in copies:
            cp.wait_recv()
        for cp in copies:
            cp.wait_send()
        acc = slots[0]
        for d in range(1, N_DEV):
            acc = acc + slots[d]
        o_ref[...] = acc

    vmem = pl.BlockSpec(memory_space=pltpu.VMEM)
    return _call(
        body, name="allreduce_small", in_specs=[vmem], out_specs=vmem,
        out_shape=jax.ShapeDtypeStruct((rows, cols), F32),
        scratch=[pltpu.VMEM((N_DEV, rows, cols), F32), pltpu.SemaphoreType.DMA((N_DEV - 1,)),
                 pltpu.SemaphoreType.DMA((N_DEV - 1,))],
    )(pack)


def _pad_rows(a, rows=8):
    return jnp.pad(a, ((0, rows - a.shape[0]), (0, 0)))


def kernel(x, positions, norm1_w, w_in, merge_gate_b, ret_gn_w, w_ret_o, lru_conv_w, lru_conv_b, lru_w_r, lru_b_r, lru_w_i, lru_b_i, lru_lambda, w_lru_o, w_out, norm2_w, ffn_w_up, ffn_conv_w, ffn_conv_b, ffn_w_down, norm_f_w, loss_target, m_norm1_w, m_w_in, m_merge_gate_b, m_ret_gn_w, m_w_ret_o, m_lru_conv_w, m_lru_conv_b, m_lru_w_r, m_lru_b_r, m_lru_w_i, m_lru_b_i, m_lru_lambda, m_w_lru_o, m_w_out, m_norm2_w, m_ffn_w_up, m_ffn_conv_w, m_ffn_conv_b, m_ffn_w_down, m_norm_f_w, v_norm1_w, v_w_in, v_merge_gate_b, v_ret_gn_w, v_w_ret_o, v_lru_conv_w, v_lru_conv_b, v_lru_w_r, v_lru_b_r, v_lru_w_i, v_lru_b_i, v_lru_lambda, v_w_lru_o, v_w_out, v_norm2_w, v_ffn_w_up, v_ffn_conv_w, v_ffn_conv_b, v_ffn_w_down, v_norm_f_w):
    names = ["norm1_w", "w_in", "merge_gate_b", "ret_gn_w", "w_ret_o", "lru_conv_w", "lru_conv_b", "lru_w_r",
             "lru_b_r", "lru_w_i", "lru_b_i", "lru_lambda", "w_lru_o", "w_out", "norm2_w", "ffn_w_up",
             "ffn_conv_w", "ffn_conv_b", "ffn_w_down", "norm_f_w"]
    w_args = dict(zip(names, [norm1_w, w_in, merge_gate_b, ret_gn_w, w_ret_o, lru_conv_w, lru_conv_b, lru_w_r,
                              lru_b_r, lru_w_i, lru_b_i, lru_lambda, w_lru_o, w_out, norm2_w, ffn_w_up,
                              ffn_conv_w, ffn_conv_b, ffn_w_down, norm_f_w]))
    m_args = dict(zip(names, [m_norm1_w, m_w_in, m_merge_gate_b, m_ret_gn_w, m_w_ret_o, m_lru_conv_w,
                              m_lru_conv_b, m_lru_w_r, m_lru_b_r, m_lru_w_i, m_lru_b_i, m_lru_lambda, m_w_lru_o,
                              m_w_out, m_norm2_w, m_ffn_w_up, m_ffn_conv_w, m_ffn_conv_b, m_ffn_w_down,
                              m_norm_f_w]))
    v_args = dict(zip(names, [v_norm1_w, v_w_in, v_merge_gate_b, v_ret_gn_w, v_w_ret_o, v_lru_conv_w,
                              v_lru_conv_b, v_lru_w_r, v_lru_b_r, v_lru_w_i, v_lru_b_i, v_lru_lambda, v_w_lru_o,
                              v_w_out, v_norm2_w, v_ffn_w_up, v_ffn_conv_w, v_ffn_conv_b, v_ffn_w_down,
                              v_norm_f_w]))

    bsz, seq, d = x.shape
    m = bsz * seq
    tm = min(MM_ROWS, m)
    tm_fused = min(FUSED_ROWS, m)
    tm_tall = min(TALL_ROWS, m)
    chip = 2 * lax.axis_index("x") + lax.axis_index("y")

    big = ["w_in", "w_ret_o", "w_lru_o", "w_out", "lru_w_r", "lru_w_i", "ffn_w_up", "ffn_w_down"]
    cut = dict(w_in=(1, 0), w_ret_o=(0, 0), w_lru_o=(0, 0), w_out=(0, 0), lru_w_r=(1, 0), lru_w_i=(1, 0),
               ffn_w_up=(1, 0), ffn_w_down=(0, 0))
    core = lax.axis_index("c")
    chip1 = jnp.reshape(chip, (1,)).astype(jnp.int32)
    pos = jnp.stack([chip, core]).astype(jnp.int32)
    placed = {n: _place_shard(w_args[n][0], chip1, cut[n][0], name="place_" + n) for n in big}
    small_pack = jnp.concatenate([
        jnp.pad(merge_gate_b[0], ((0, 6), (0, 512))),
        jnp.pad(lru_conv_w[0], ((0, 4), (0, 512))),
        jnp.pad(lru_b_r[0], ((0, 4), (0, 704))),
        jnp.pad(lru_b_i[0], ((0, 4), (0, 704))),
        jnp.pad(ffn_conv_w[0], ((0, 5), (0, 0))),
    ], axis=0)
    x2d = x.reshape(m, d)
    mx, my = lax.axis_index("x"), lax.axis_index("y")
    order = jnp.stack([chip, 2 * (1 - mx) + my, 2 * mx + 1 - my, 2 * (1 - mx) + 1 - my]).astype(jnp.int32)
    mixer = ["w_ret_o", "w_lru_o", "w_out", "lru_w_r", "lru_w_i"]
    cuts = lambda ns: [cut[n] for n in ns]
    sizes = lambda ns: [w_args[n].shape[1 + cut[n][0]] for n in ns]
    h1 = _norm_bf16(x2d, norm1_w, name="norm1", tm=tm)
    proj, w_in_full, bufs, sp = _in_proj_gather(h1, placed["w_in"], [placed[n] for n in mixer], cuts(mixer),
                                               small_pack, order, tm=tm_tall)
    wb = {"w_in": w_in_full}
    mgb = jnp.transpose(sp[:, 0:2, 0:256], (1, 0, 2)).reshape(2, D_MODEL)
    lcw = jnp.transpose(sp[:, 8:12, 0:256], (1, 0, 2)).reshape(4, D_MODEL)
    lbr = jnp.transpose(sp[:, 16:20, 0:64], (1, 0, 2)).reshape(1, D_MODEL)
    lbi = jnp.transpose(sp[:, 24:28, 0:64], (1, 0, 2)).reshape(1, D_MODEL)
    fcw = jnp.transpose(sp[:, 32:35, :], (1, 0, 2)).reshape(3, D_FF)
    nfw = norm_f_w.reshape(1, D_MODEL)

    half = RET_DK // 2
    inv_freq = ROPE_BASE ** (-jnp.arange(half, dtype=F32) / half)
    cos, sin = _rope_tables(positions.reshape(bsz, seq, 1), jnp.concatenate([inv_freq, inv_freq]).reshape(1, RET_DK))
    proj3 = proj.reshape(bsz, seq, D_IN)
    down, up_w = ["ffn_w_down"], ["ffn_w_up"]
    (a_in3, states), bufs = _retention_fwd(
        proj3, cos, sin, ret_gn_w,
        comm=_both(_gather_d2d(bufs, cuts(mixer), sizes(mixer)), _gather_ici([placed["ffn_w_down"]], cuts(down))))
    wb.update(zip(mixer, bufs[:len(mixer)]))
    lru_params = (lcw, lru_conv_b, wb["lru_w_r"], lbr, wb["lru_w_i"], lbi, lru_lambda)
    (b_in3, *lru_kept), (up_buf, wb["ffn_w_down"]) = _lru_fwd(
        proj3, lru_params,
        comm=_both(_gather_ici([placed["ffn_w_up"]], cuts(up_w)), _gather_d2d(bufs[len(mixer):], cuts(down), sizes(down))))
    a_in, b_in = a_in3.reshape(m, d), b_in3.reshape(m, d)
    (x2, mix, ya, yb), (wb["ffn_w_up"],) = _merge_fwd(
        a_in, b_in, proj, mgb, wb["w_ret_o"], wb["w_lru_o"], wb["w_out"], x2d, tm=tm_fused,
        comm=_gather_d2d([up_buf], cuts(up_w), sizes(up_w)))
    up, h2 = _norm_matmul(x2, norm2_w, wb["ffn_w_up"], name="ffn_up", tm=tm_tall, tn=TALL_COLS)
    up3 = up.reshape(bsz, seq, 2 * D_FF)
    f3, act3, q3 = _ffn_act_fwd(up3, fcw, ffn_conv_b)
    f = f3.reshape(m, D_FF)
    dx3, sm_nf = _ffn_down_loss(f, wb["ffn_w_down"], x2, nfw, loss_target.reshape(m, d), tm=tm)

    def send(*ns):
        return _exchange([g_full[n] for n in ns], [cut[n] for n in ns])

    g_full, parts = {}, {}
    df = _mm_nt(dx3, wb["ffn_w_down"], name="ffn_down_dx", tm=tm_tall, out_dtype=BF16)
    g_full["ffn_w_down"] = _mm_tn(f, [dx3], name="ffn_down_dw", tm=tm_tall)
    (dgate3, dval3, sm_ffn), (parts["ffn_w_down"],) = _ffn_act_bwd(
        up3, act3, q3, fcw, df.reshape(bsz, seq, D_FF), comm=send("ffn_w_down"))
    dup = [dgate3.reshape(m, D_FF), dval3.reshape(m, D_FF)]
    g_full["ffn_w_up"] = _mm_tn(h2, dup, name="ffn_up_dw", tm=tm_tall)
    (dx2, sm_n2), (parts["ffn_w_up"],) = _mm_nt_normbwd(
        dup, wb["ffn_w_up"], x2, norm2_w, dx3, name="ffn_up_dx", tm=tm, row=ROW_N2, comm=send("ffn_w_up"))
    dya, dyb, da_in, db_in, dgates, sm_mg = _merge_bwd(
        ya, yb, proj, mgb, wb["w_ret_o"], wb["w_lru_o"], wb["w_out"], dx2, tm=tm_fused)
    g_full["w_out"] = _mm_tn(mix, [dx2], name="out_dw", tm=tm_tall)
    g_full["w_ret_o"] = _mm_tn(a_in, [dya], name="ret_o_dw", tm=tm_tall)
    g_full["w_lru_o"] = _mm_tn(b_in, [dyb], name="lru_o_dw", tm=tm_tall)
    (dlru3, dwr, dwi, sm_lru), (parts["w_out"], parts["w_ret_o"], parts["w_lru_o"]) = _lru_bwd(
        proj3, lru_params, lru_kept, db_in.reshape(bsz, seq, d), comm=send("w_out", "w_ret_o", "w_lru_o"))
    g_full["lru_w_r"], g_full["lru_w_i"] = dwr.astype(BF16), dwi.astype(BF16)
    (dret3, sm_gn), (parts["lru_w_r"], parts["lru_w_i"]) = _retention_bwd(
        proj3, cos, sin, ret_gn_w, states, da_in.reshape(bsz, seq, d), comm=send("lru_w_r", "lru_w_i"))
    dproj = [dret3.reshape(m, 3072), dlru3.reshape(m, 2048), dgates]
    g_full["w_in"] = _mm_tn(h1, dproj, name="in_proj_dw", tm=tm_tall)
    half_sum = lambda n: _reduce_half(g_full[n], parts[n], pos, cut[n], name="sum_" + n)
    (grad_x, sm_n1), (parts["w_in"], *joined) = _mm_nt_normbwd(
        dproj, wb["w_in"], x2d, norm1_w, dx2, name="in_proj_dx", tm=tm, row=ROW_N1,
        comm=_both(send("w_in"), _join_comm([half_sum(n) for n in big[1:]])))
    reduced = _join_halves([half_sum("w_in")]) + joined
    misc = sm_n1 + sm_mg + sm_gn + sm_n2 + sm_nf
    pack = jnp.concatenate(
        [misc, sm_lru, sm_ffn[:, 0:1024], sm_ffn[:, 1024:2048], sm_ffn[:, 2048:3072]], axis=0)
    tot = _allreduce_small(pack)
    ffn_sm = jnp.concatenate([tot[16:24], tot[24:32], tot[32:40]], axis=1)
    g_small = {
        "norm1_w": tot[ROW_N1:ROW_N1 + 1], "merge_gate_b": tot[ROW_MGB:ROW_MGB + 2],
        "ret_gn_w": tot[ROW_GN:ROW_GN + 1], "norm2_w": tot[ROW_N2:ROW_N2 + 1], "norm_f_w": tot[ROW_NF:ROW_NF + 1],
        "lru_conv_w": tot[8:12], "lru_conv_b": tot[12:13], "lru_b_r": tot[13:14].reshape(4, 256),
        "lru_b_i": tot[14:15].reshape(4, 256), "lru_lambda": tot[15:16],
        "ffn_conv_w": ffn_sm[0:3], "ffn_conv_b": ffn_sm[3:4],
    }
    small_shard = dict(merge_gate_b=256, lru_conv_w=256, lru_b_r=64, lru_b_i=64, ffn_conv_w=768)

    outs = {}
    for n, g in zip(big, reduced):
        shape = w_args[n].shape
        g = g.reshape(-1, g.shape[-1])
        outs[n] = [o.reshape(shape) for o in _adamw(
            w_args[n].reshape(g.shape), [g], m_args[n].reshape(g.shape), v_args[n].reshape(g.shape),
            name="adamw_" + n)]
    for n, g in g_small.items():
        shape = w_args[n].shape
        if n in small_shard:
            g = lax.dynamic_slice_in_dim(g, chip * small_shard[n], small_shard[n], axis=1)
        w2 = w_args[n].reshape(g.shape)
        outs[n] = [o.reshape(shape) for o in _adamw(
            w2, [g], m_args[n].reshape(g.shape), v_args[n].reshape(g.shape), name="adamw_" + n)]

    result = [tot[ROW_LOSS, 0], grad_x.reshape(bsz, seq, d)]
    for k in range(4):
        result += [outs[n][k] for n in names]
    return tuple(result)
```

```python
import functools
import math

import numpy as np
import jax
import jax.numpy as jnp
from jax import lax
from jax.experimental import pallas as pl
from jax.experimental.pallas import tpu as pltpu

F32 = jnp.float32
BF16 = jnp.bfloat16

D_MODEL = 1024
RET_HEADS = 4
RET_DK = 128
RET_DV = 256
LRU_BLOCKS = 4
LRU_BLOCK = 256
LRU_C = 8.0
D_FF = 3072
D_IN = 7168
ROPE_BASE = 10000.0
RMS_EPS = 1e-6
GN_EPS = 1e-6
ADAM_LR, ADAM_B1, ADAM_B2, ADAM_EPS, ADAM_WD, ADAM_STEP = 0.001, 0.9, 0.999, 1e-08, 0.01, 10

N_CHIPS = 4
N_DEV = 8
SEQ_T = 256
RET_T = 512
REF_CHUNK = 64
COL = 1024
MM_ROWS = 1024
TALL_ROWS, TALL_COLS = 2048, 1024
FUSED_ROWS = 512
VMEM_LIMIT_BYTES = 56 * 1024 * 1024
MESH = pl.DeviceIdType.MESH
ROW_N1, ROW_MGB, ROW_GN, ROW_N2, ROW_NF, ROW_LOSS = 0, 1, 3, 4, 5, 6
GELU_K = math.sqrt(2.0 / math.pi)
GELU_C = 0.044715


class _Comm:
    def __init__(self, ins, outs, sems, start, finish, aliases=None):
        self.ins, self.outs, self.sems = list(ins), list(outs), list(sems)
        self.start, self.finish, self.aliases = start, finish, dict(aliases or {})


def _call(body, *, name, out_shape=(), grid=None, in_specs=(), out_specs=(), scratch=(), comm=None, prefetch=0,
          aliases=None):
    single = not isinstance(out_shape, (list, tuple))
    out_shape = [out_shape] if single else list(out_shape)
    out_specs = [out_specs] if single else list(out_specs)
    in_specs, scratch = list(in_specs), list(scratch)
    n_in, n_out, n_scr = len(in_specs), len(out_shape), len(scratch)
    kwargs = dict(name=name, compiler_params=pltpu.CompilerParams(vmem_limit_bytes=VMEM_LIMIT_BYTES))
    if prefetch:
        assert comm is None
        spec = pltpu.PrefetchScalarGridSpec(num_scalar_prefetch=prefetch, grid=grid, in_specs=in_specs,
                                            out_specs=out_specs, scratch_shapes=scratch)
        fn = pl.pallas_call(body, out_shape=out_shape, grid_spec=spec, input_output_aliases=dict(aliases or {}),
                            **kwargs)
        return (lambda *args: fn(*args)[0]) if single else fn
    if grid is not None:
        kwargs["grid"] = grid
    if comm is None:
        fn = pl.pallas_call(body, out_shape=out_shape, in_specs=in_specs, out_specs=out_specs,
                            scratch_shapes=scratch, **kwargs)
        return (lambda *args: fn(*args)[0]) if single else fn

    any_spec = pl.BlockSpec(memory_space=pl.ANY)
    n_cin, n_cout = len(comm.ins), len(comm.outs)

    def wrapped(*refs):
        ins, refs = refs[:n_in], refs[n_in:]
        cins, refs = refs[:n_cin], refs[n_cin:]
        outs, refs = refs[:n_out], refs[n_out:]
        couts, refs = refs[:n_cout], refs[n_cout:]
        scr, csems = refs[:n_scr], refs[n_scr:]
        if grid is None:
            comm.start(cins, couts, csems)
            comm.finish(cins, couts, csems)
            return
        ids = [pl.program_id(a) for a in range(len(grid))]
        first = functools.reduce(jnp.logical_and, [i == 0 for i in ids])
        last = functools.reduce(jnp.logical_and, [i == g - 1 for i, g in zip(ids, grid)])
        pl.when(first)(lambda: comm.start(cins, couts, csems))
        body(*ins, *outs, *scr)
        pl.when(last)(lambda: comm.finish(cins, couts, csems))

    fn = pl.pallas_call(
        wrapped, out_shape=out_shape + comm.outs, in_specs=in_specs + [any_spec] * n_cin,
        out_specs=out_specs + [any_spec] * n_cout, scratch_shapes=scratch + comm.sems,
        input_output_aliases={n_in + i: n_out + o for i, o in comm.aliases.items()}, **kwargs)

    def run(*args):
        res = fn(*args, *comm.ins)
        own = res[0] if single else list(res[:n_out])
        return own, list(res[n_out:])

    return run


def _dot(a, b):
    return jnp.dot(a, b, preferred_element_type=F32)


def _dot_nt(a, b):
    return lax.dot_general(a, b, (((1,), (1,)), ((), ())), preferred_element_type=F32)


def _dot_tn(a, b):
    return lax.dot_general(a, b, (((0,), (0,)), ((), ())), preferred_element_type=F32)


def _bf(x):
    return x.astype(BF16)


def _sigmoid(x):
    return 1.0 / (1.0 + jnp.exp(-x))


def _gelu_and_grad(x):
    x2 = x * x
    s = _sigmoid(x * (2.0 * GELU_K * GELU_C * x2 + 2.0 * GELU_K))
    g = x * s
    dg = s + g * (1.0 - s) * (6.0 * GELU_K * GELU_C * x2 + 2.0 * GELU_K)
    return g, dg


def _rms(x):
    r = lax.rsqrt(jnp.mean(x * x, axis=-1, keepdims=True) + RMS_EPS)
    return x * r, r


def _rms_bwd(dy, x, nw):
    xh, r = _rms(x)
    g = dy * nw
    dx = r * (g - xh * jnp.mean(g * xh, axis=-1, keepdims=True))
    return dx, jnp.sum(dy * xh, axis=0, keepdims=True)


def _row_acc(ref, row, val):
    ref[row:row + 1, :] = ref[row:row + 1, :] + val


def _shift_down(x, j, prev8):
    if j == 0:
        return x
    n = x.shape[0] // 8
    row = lax.broadcasted_iota(jnp.int32, prev8.shape, 0)
    turned = [pltpu.roll(prev8, j, 0)] + [pltpu.roll(x[8 * k:8 * k + 8], j, 0) for k in range(n)]
    return jnp.concatenate([jnp.where(row < j, turned[k], turned[k + 1]) for k in range(n)], axis=0)


def _shift_up(x, j, next8):
    if j == 0:
        return x
    n = x.shape[0] // 8
    row = lax.broadcasted_iota(jnp.int32, next8.shape, 0)
    turned = [pltpu.roll(x[8 * k:8 * k + 8], 8 - j, 0) for k in range(n)] + [pltpu.roll(next8, 8 - j, 0)]
    return jnp.concatenate([jnp.where(row >= 8 - j, turned[k + 1], turned[k]) for k in range(n)], axis=0)


def _scan_fwd(a, b, carry):
    row = lax.broadcasted_iota(jnp.int32, (8, a.shape[1]), 0)
    out = []
    for k in range(a.shape[0] // 8):
        ak, bk = a[8 * k:8 * k + 8], b[8 * k:8 * k + 8]
        for s in (1, 2, 4):
            keep = row >= s
            ar, br = pltpu.roll(ak, s, 0), pltpu.roll(bk, s, 0)
            bk = jnp.where(keep, ak * br + bk, bk)
            ak = jnp.where(keep, ak * ar, ak)
        hk = ak * carry + bk
        carry = hk[7:8]
        out.append(hk)
    return jnp.concatenate(out, axis=0)


def _scan_bwd(a, b, carry):
    row = lax.broadcasted_iota(jnp.int32, (8, a.shape[1]), 0)
    out = []
    for k in reversed(range(a.shape[0] // 8)):
        ak, bk = a[8 * k:8 * k + 8], b[8 * k:8 * k + 8]
        for s in (1, 2, 4):
            keep = row < 8 - s
            ar, br = pltpu.roll(ak, 8 - s, 0), pltpu.roll(bk, 8 - s, 0)
            bk = jnp.where(keep, ak * br + bk, bk)
            ak = jnp.where(keep, ak * ar, ak)
        gk = bk + ak * carry
        carry = gk[0:1]
        out.append(gk)
    return jnp.concatenate(out[::-1], axis=0)


def _norm_matmul(x, nw, w, *, name, tm, tn):
    m, d = x.shape
    n = w.shape[1]

    def body(x_ref, nw_ref, w_ref, o_ref, h_ref, h_sc):
        @pl.when(pl.program_id(1) == 0)
        def _():
            xh, _ = _rms(x_ref[...])
            h = _bf(xh * nw_ref[...])
            h_sc[...] = h
            h_ref[...] = h

        o_ref[...] = _bf(_dot(h_sc[...], w_ref[...]))

    return _call(
        body, name=name, grid=(m // tm, n // tn),
        in_specs=[pl.BlockSpec((tm, d), lambda i, j: (i, 0)),
                  pl.BlockSpec((1, d), lambda i, j: (0, 0)),
                  pl.BlockSpec((d, tn), lambda i, j: (0, j))],
        out_specs=[pl.BlockSpec((tm, tn), lambda i, j: (i, j)),
                   pl.BlockSpec((tm, d), lambda i, j: (i, 0))],
        out_shape=[jax.ShapeDtypeStruct((m, n), BF16), jax.ShapeDtypeStruct((m, d), BF16)],
        scratch=[pltpu.VMEM((tm, d), BF16)],
    )(x, nw, w)


def _mm_nt(a, w, *, name, tm, out_dtype):
    m, k = a.shape
    n = w.shape[0]

    def body(a_ref, w_ref, o_ref):
        o_ref[...] = _dot_nt(_bf(a_ref[...]), w_ref[...]).astype(out_dtype)

    return _call(
        body, name=name, grid=(m // tm, n // COL),
        in_specs=[pl.BlockSpec((tm, k), lambda i, j: (i, 0)),
                  pl.BlockSpec((COL, k), lambda i, j: (j, 0))],
        out_specs=pl.BlockSpec((tm, COL), lambda i, j: (i, j)),
        out_shape=jax.ShapeDtypeStruct((m, n), out_dtype),
    )(a, w)


def _piece_layout(pieces):
    offs, nblk, o = [], [], 0
    for p in pieces:
        offs.append(o)
        nblk.append(p.shape[1] // COL)
        o += p.shape[1] // COL
    return offs, nblk, o


def _mm_tn(a, pieces, *, name, tm, out_dtype=BF16):
    m, k = a.shape
    offs, nblk, nn = _piece_layout(pieces)

    def piece_spec(o, nb):
        def idx(ki, nj, mi):
            use = jnp.logical_and(nj >= o, nj < o + nb)
            return (jnp.where(use, mi, 0), jnp.clip(nj - o, 0, nb - 1))
        return pl.BlockSpec((tm, COL), idx)

    def body(a_ref, *rest):
        p_refs, o_ref, acc = rest[:len(pieces)], rest[len(pieces)], rest[len(pieces) + 1]
        nj, mi = pl.program_id(1), pl.program_id(2)

        @pl.when(mi == 0)
        def _():
            acc[...] = jnp.zeros_like(acc)

        for p_ref, o, nb in zip(p_refs, offs, nblk):
            @pl.when(jnp.logical_and(nj >= o, nj < o + nb))
            def _(p_ref=p_ref):
                acc[...] += _dot_tn(_bf(a_ref[...]), _bf(p_ref[...]))

        @pl.when(mi == pl.num_programs(2) - 1)
        def _():
            o_ref[...] = acc[...].astype(out_dtype)

    return _call(
        body, name=name, grid=(k // COL, nn, m // tm),
        in_specs=[pl.BlockSpec((tm, COL), lambda ki, nj, mi: (mi, ki))]
        + [piece_spec(o, nb) for o, nb in zip(offs, nblk)],
        out_specs=pl.BlockSpec((COL, COL), lambda ki, nj, mi: (ki, nj)),
        out_shape=jax.ShapeDtypeStruct((k, nn * COL), out_dtype),
        scratch=[pltpu.VMEM((COL, COL), F32)],
    )(a, *pieces)


def _mm_nt_normbwd(pieces, w, x, nw, dres, *, name, tm, row, comm=None):
    m, d = x.shape
    offs, nblk, nk = _piece_layout(pieces)

    def piece_spec(o, nb):
        return pl.BlockSpec((tm, COL), lambda i, k: (i, jnp.clip(k - o, 0, nb - 1)))

    def body(*refs):
        p_refs = refs[:len(pieces)]
        w_ref, x_ref, nw_ref, dres_ref, dx_ref, dnw_ref, acc = refs[len(pieces):]
        i, k = pl.program_id(0), pl.program_id(1)

        @pl.when(jnp.logical_and(i == 0, k == 0))
        def _():
            dnw_ref[...] = jnp.zeros_like(dnw_ref)

        @pl.when(k == 0)
        def _():
            acc[...] = jnp.zeros_like(acc)

        for p_ref, o, nb in zip(p_refs, offs, nblk):
            @pl.when(jnp.logical_and(k >= o, k < o + nb))
            def _(p_ref=p_ref):
                acc[...] += _dot_nt(_bf(p_ref[...]), w_ref[...])

        @pl.when(k == nk - 1)
        def _():
            dx, dnw = _rms_bwd(acc[...], x_ref[...], nw_ref[...])
            dx_ref[...] = dres_ref[...] + dx
            _row_acc(dnw_ref, row, dnw)

    return _call(
        body, name=name, grid=(m // tm, nk), comm=comm,
        in_specs=[piece_spec(o, nb) for o, nb in zip(offs, nblk)]
        + [pl.BlockSpec((d, COL), lambda i, k: (0, k)),
           pl.BlockSpec((tm, d), lambda i, k: (i, 0)),
           pl.BlockSpec((1, d), lambda i, k: (0, 0)),
           pl.BlockSpec((tm, d), lambda i, k: (i, 0))],
        out_specs=[pl.BlockSpec((tm, d), lambda i, k: (i, 0)),
                   pl.BlockSpec((8, d), lambda i, k: (0, 0))],
        out_shape=[jax.ShapeDtypeStruct((m, d), F32), jax.ShapeDtypeStruct((8, d), F32)],
        scratch=[pltpu.VMEM((tm, d), F32)],
    )(*pieces, w, x, nw, dres)


def _rope_tables(pos3, invf):
    b, s, _ = pos3.shape

    def body(pos_ref, invf_ref, cos_ref, sin_ref):
        half_t, half_d = s // 2, RET_DK // 2
        pos = pos_ref[...].astype(F32)
        low = lax.broadcasted_iota(jnp.int32, (half_t, RET_DK), 1) < half_d
        ang = jnp.where(low, pos[0:half_t], pos[half_t:]) * invf_ref[...]
        co, si = jnp.cos(ang), jnp.sin(ang)
        co_turned, si_turned = pltpu.roll(co, half_d, 1), pltpu.roll(si, half_d, 1)
        sign = jnp.where(low, -1.0, 1.0)
        cos_ref[...] = jnp.concatenate([jnp.where(low, co, co_turned), jnp.where(low, co_turned, co)], axis=0)
        sin_ref[...] = jnp.concatenate([sign * jnp.where(low, si, si_turned), sign * jnp.where(low, si_turned, si)],
                                       axis=0)

    spec = pl.BlockSpec((None, s, RET_DK), lambda i: (i, 0, 0))
    return _call(
        body, name="rope_tables", grid=(b,),
        in_specs=[pl.BlockSpec((None, s, 1), lambda i: (i, 0, 0)),
                  pl.BlockSpec((1, RET_DK), lambda i: (0, 0))],
        out_specs=[spec, spec],
        out_shape=[jax.ShapeDtypeStruct((b, s, RET_DK), F32)] * 2,
    )(pos3, invf)


def _log_gamma(h):
    return float(np.log1p(-np.power(np.float32(2.0), np.float32(-5.0 - h))).astype(np.float32))


def _decay_matrix(h):
    lg = _log_gamma(h)
    n = lax.broadcasted_iota(jnp.int32, (RET_T, RET_T), 0)
    m = lax.broadcasted_iota(jnp.int32, (RET_T, RET_T), 1)
    same = (n // REF_CHUNK) == (m // REF_CHUNK)
    dist = jnp.where(same, jnp.abs(n - m), n - m).astype(F32)
    return jnp.where(jnp.logical_or(same, m < n), jnp.exp(lg * dist), 0.0)


def _decay_vectors(h):
    lg = _log_gamma(h)
    idx = lax.broadcasted_iota(jnp.int32, (RET_T, 1), 0).astype(F32)
    qd = jnp.exp(lg * (idx + 1.0))
    kd = jnp.exp(lg * (RET_T - 1.0 - idx))
    return qd, kd, math.exp(lg * RET_T)


def _rotate(x, cos, sin):
    return x * cos + pltpu.roll(x, RET_DK // 2, 1) * sin


def _rotate_bwd(d, cos, sin):
    return d * cos + pltpu.roll(d * sin, RET_DK // 2, 1)


def _ret_head(p_ref, cos, sin, h):
    q = p_ref[:, h * RET_DK:(h + 1) * RET_DK].astype(F32)
    k = p_ref[:, 512 + h * RET_DK:512 + (h + 1) * RET_DK].astype(F32)
    v = p_ref[:, 1024 + h * RET_DV:1024 + (h + 1) * RET_DV]
    g = p_ref[:, 2048 + h * RET_DV:2048 + (h + 1) * RET_DV].astype(F32)
    qr = _rotate(q, cos, sin)
    kr = _rotate(k, cos, sin) * (RET_DK ** -0.5)
    return qr, kr, v, g


def _group_norm(o):
    mu = jnp.mean(o, axis=-1, keepdims=True)
    oc = o - mu
    rstd = lax.rsqrt(jnp.mean(oc * oc, axis=-1, keepdims=True) + GN_EPS)
    return oc * rstd, rstd


def _retention_fwd(proj3, cos, sin, gnw, comm=None):
    b, s, _ = proj3.shape
    nc = s // RET_T

    def body(p_ref, cos_ref, sin_ref, gnw_ref, a_ref, st_ref, state, wtab):
        c = pl.program_id(1)

        @pl.when(jnp.logical_and(pl.program_id(0) == 0, c == 0))
        def _():
            for h in range(RET_HEADS):
                wtab[h] = _decay_matrix(h)

        @pl.when(c == 0)
        def _():
            state[...] = jnp.zeros_like(state)

        cs, sn = cos_ref[...], sin_ref[...]
        st_ref[...] = state[...]
        outs, states = [], []
        for h in range(RET_HEADS):
            qd, kd, gt = _decay_vectors(h)
            qr, kr, v, g = _ret_head(p_ref, cs, sn, h)
            st = state[h]
            p = _dot_nt(_bf(qr), _bf(kr)) * wtab[h]
            o = _dot(_bf(p), _bf(v)) + _dot(_bf(qr * qd), _bf(st))
            states.append(st * gt + _dot_tn(_bf(kr * kd), _bf(v)))
            on, _ = _group_norm(o)
            gw = gnw_ref[:, h * RET_DV:(h + 1) * RET_DV]
            outs.append(_bf(on * gw * (g * _sigmoid(g))))
        a_ref[...] = jnp.concatenate(outs, axis=1)
        state[...] = jnp.stack(states)

    tab = pl.BlockSpec((None, RET_T, RET_DK), lambda i, c: (i, c, 0))
    return _call(
        body, name="retention_fwd", grid=(b, nc), comm=comm,
        in_specs=[pl.BlockSpec((None, RET_T, 3072), lambda i, c: (i, c, 0)), tab, tab,
                  pl.BlockSpec((1, D_MODEL), lambda i, c: (0, 0))],
        out_specs=[pl.BlockSpec((None, RET_T, D_MODEL), lambda i, c: (i, c, 0)),
                   pl.BlockSpec((None, None, RET_HEADS, RET_DK, RET_DV), lambda i, c: (i, c, 0, 0, 0))],
        out_shape=[jax.ShapeDtypeStruct((b, s, D_MODEL), BF16),
                   jax.ShapeDtypeStruct((b, nc, RET_HEADS, RET_DK, RET_DV), F32)],
        scratch=[pltpu.VMEM((RET_HEADS, RET_DK, RET_DV), F32),
                 pltpu.VMEM((RET_HEADS, RET_T, RET_T), F32)],
    )(proj3, cos, sin, gnw)


def _retention_bwd(proj3, cos, sin, gnw, states, da3, comm=None):
    b, s, _ = proj3.shape
    nc = s // RET_T

    def body(p_ref, cos_ref, sin_ref, gnw_ref, st_ref, da_ref, d_ref, dgn_ref, dstate, wtab):
        c = pl.program_id(1)

        @pl.when(jnp.logical_and(pl.program_id(0) == 0, c == 0))
        def _():
            dgn_ref[...] = jnp.zeros_like(dgn_ref)
            for h in range(RET_HEADS):
                wtab[h] = _decay_matrix(h)

        @pl.when(c == 0)
        def _():
            dstate[...] = jnp.zeros_like(dstate)

        cs, sn = cos_ref[...], sin_ref[...]
        dqs, dks, dvs, dgs, dgns, dstates = [], [], [], [], [], []
        for h in range(RET_HEADS):
            qd, kd, gt = _decay_vectors(h)
            qr, kr, v, g = _ret_head(p_ref, cs, sn, h)
            st, dst, w = st_ref[h], dstate[h], wtab[h]
            qb, kb, vb = _bf(qr), _bf(kr), _bf(v)
            p = _dot_nt(qb, kb) * w
            o = _dot(_bf(p), vb) + _dot(_bf(qr * qd), _bf(st))
            on, rstd = _group_norm(o)
            gw = gnw_ref[:, h * RET_DV:(h + 1) * RET_DV]
            da = da_ref[:, h * RET_DV:(h + 1) * RET_DV].astype(F32)
            sg = _sigmoid(g)
            silu = g * sg
            dg = da * on * gw * (sg * (1.0 + g * (1.0 - sg)))
            dgns.append(jnp.sum(da * silu * on, axis=0, keepdims=True))
            don = da * silu * gw
            do = rstd * (don - jnp.mean(don, axis=-1, keepdims=True)
                         - on * jnp.mean(don * on, axis=-1, keepdims=True))
            dob = _bf(do)
            dp = _dot_nt(dob, vb) * w
            dqr = _dot(_bf(dp), kb) + _dot_nt(dob, _bf(st)) * qd
            dkr = _dot_tn(_bf(dp), qb) + _dot_nt(vb, _bf(dst)) * kd
            dv = _dot_tn(_bf(p), dob) + _dot(_bf(kr * kd), _bf(dst))
            dstates.append(dst * gt + _dot_tn(_bf(qr * qd), dob))
            dqs.append(_bf(_rotate_bwd(dqr, cs, sn)))
            dks.append(_bf(_rotate_bwd(dkr, cs, sn) * (RET_DK ** -0.5)))
            dvs.append(_bf(dv))
            dgs.append(_bf(dg))
        d_ref[...] = jnp.concatenate(dqs + dks + dvs + dgs, axis=1)
        _row_acc(dgn_ref, ROW_GN, jnp.concatenate(dgns, axis=1))
        dstate[...] = jnp.stack(dstates)

    rev = lambda i, c: (i, nc - 1 - c, 0)
    tab = pl.BlockSpec((None, RET_T, RET_DK), rev)
    return _call(
        body, name="retention_bwd", grid=(b, nc), comm=comm,
        in_specs=[pl.BlockSpec((None, RET_T, 3072), rev), tab, tab,
                  pl.BlockSpec((1, D_MODEL), lambda i, c: (0, 0)),
                  pl.BlockSpec((None, None, RET_HEADS, RET_DK, RET_DV), lambda i, c: (i, nc - 1 - c, 0, 0, 0)),
                  pl.BlockSpec((None, RET_T, D_MODEL), rev)],
        out_specs=[pl.BlockSpec((None, RET_T, 3072), rev),
                   pl.BlockSpec((8, D_MODEL), lambda i, c: (0, 0))],
        out_shape=[jax.ShapeDtypeStruct((b, s, 3072), BF16), jax.ShapeDtypeStruct((8, D_MODEL), F32)],
        scratch=[pltpu.VMEM((RET_HEADS, RET_DK, RET_DV), F32),
                 pltpu.VMEM((RET_HEADS, RET_T, RET_T), F32)],
    )(proj3, cos, sin, gnw, states, da3)


def _softplus_neg(lam):
    z = -lam
    u = jnp.exp(-jnp.abs(z))
    log1p_u = jnp.where(u < 0.01, u * (1.0 - u * (0.5 - u * (1.0 / 3.0))), jnp.log(1.0 + u))
    return jnp.maximum(z, 0.0) + log1p_u


def _lru_coeffs(xc, wr_ref, br_ref, wi_ref, bi_ref, lam_ref):
    rs, is_ = [], []
    for n in range(LRU_BLOCKS):
        xb = _bf(xc[:, n * LRU_BLOCK:(n + 1) * LRU_BLOCK])
        cols = slice(n * LRU_BLOCK, (n + 1) * LRU_BLOCK)
        rs.append(_sigmoid(_dot(xb, wr_ref[n]) + br_ref[:, cols]))
        is_.append(_sigmoid(_dot(xb, wi_ref[n]) + bi_ref[:, cols]))
    r = jnp.concatenate(rs, axis=1)
    i = jnp.concatenate(is_, axis=1)
    sp = _softplus_neg(lam_ref[...])
    la = -LRU_C * r * sp
    a = jnp.exp(la)
    s = jnp.sqrt(-jnp.tanh(la) * (a * a + 1.0))
    return r, i, a, s, sp


_LRU_PARAM_SPECS = [
    pl.BlockSpec((4, D_MODEL), lambda i, c: (0, 0)),
    pl.BlockSpec((1, D_MODEL), lambda i, c: (0, 0)),
    pl.BlockSpec((LRU_BLOCKS, LRU_BLOCK, LRU_BLOCK), lambda i, c: (0, 0, 0)),
    pl.BlockSpec((1, D_MODEL), lambda i, c: (0, 0)),
    pl.BlockSpec((LRU_BLOCKS, LRU_BLOCK, LRU_BLOCK), lambda i, c: (0, 0, 0)),
    pl.BlockSpec((1, D_MODEL), lambda i, c: (0, 0)),
    pl.BlockSpec((1, D_MODEL), lambda i, c: (0, 0)),
]


def _lru_fwd(proj3, params, comm=None):
    b, s, _ = proj3.shape
    nc = s // SEQ_T

    def body(x_ref, y_ref, cw, cb, wr, br, wi, bi, lam,
             o_ref, h_ref, xc_ref, a_ref, s_ref, gy_ref, hdg_ref, r_ref, i_ref, xprev, hprev):
        @pl.when(pl.program_id(1) == 0)
        def _():
            xprev[...] = jnp.zeros_like(xprev)
            hprev[...] = jnp.zeros_like(hprev)

        x = x_ref[...].astype(F32)
        prev8 = xprev[...]
        xc = cb[...] + sum(cw[j:j + 1, :] * _shift_down(x, 3 - j, prev8) for j in range(4))
        xprev[...] = x[SEQ_T - 8:]
        xc_ref[...] = xc
        r, i, a, s_, _ = _lru_coeffs(xc, wr, br, wi, bi, lam)
        a_ref[...] = a
        s_ref[...] = s_
        r_ref[...] = _bf(r)
        i_ref[...] = _bf(i)
        h = _scan_fwd(a, s_ * (i * xc), hprev[7:8, :])
        hprev[...] = h[SEQ_T - 8:]
        h_ref[...] = h
        gy, dgy = _gelu_and_grad(y_ref[...].astype(F32))
        o_ref[...] = _bf(h * gy)
        gy_ref[...] = _bf(gy)
        hdg_ref[...] = _bf(h * dgy)

    out = pl.BlockSpec((None, SEQ_T, D_MODEL), lambda i, c: (i, c, 0))
    half, full = jax.ShapeDtypeStruct((b, s, D_MODEL), BF16), jax.ShapeDtypeStruct((b, s, D_MODEL), F32)
    return _call(
        body, name="lru_fwd", grid=(b, nc), comm=comm,
        in_specs=[pl.BlockSpec((None, SEQ_T, D_MODEL), lambda i, c: (i, c, 3)),
                  pl.BlockSpec((None, SEQ_T, D_MODEL), lambda i, c: (i, c, 4))] + _LRU_PARAM_SPECS,
        out_specs=[out] * 9, out_shape=[half, full, full, full, full, half, half, half, half],
        scratch=[pltpu.VMEM((8, D_MODEL), F32), pltpu.VMEM((8, D_MODEL), F32)],
    )(proj3, proj3, *params)


def _lru_bwd(proj3, params, kept, db3, comm=None):
    b, s, _ = proj3.shape
    nc = s // SEQ_T
    blk8 = SEQ_T // 8
    hseq = kept[0]

    def body(x_ref, h_ref, xc_ref, a_ref, s_ref, gy_ref, hdg_ref, r_ref, i_ref, hp_ref, db_ref,
             cw, cb, wr, br, wi, bi, lam, d_ref, dwr_ref, dwi_ref, sm_ref, gnext, anext, dxcnext):
        c = pl.program_id(1)
        first_chunk = c == nc - 1

        @pl.when(jnp.logical_and(pl.program_id(0) == 0, c == 0))
        def _():
            dwr_ref[...] = jnp.zeros_like(dwr_ref)
            dwi_ref[...] = jnp.zeros_like(dwi_ref)
            sm_ref[...] = jnp.zeros_like(sm_ref)

        @pl.when(c == 0)
        def _():
            gnext[...] = jnp.zeros_like(gnext)
            anext[...] = jnp.zeros_like(anext)
            dxcnext[...] = jnp.zeros_like(dxcnext)

        x, xc, h = x_ref[...].astype(F32), xc_ref[...], h_ref[...]
        hprev = hp_ref[...] * jnp.where(first_chunk, 0.0, 1.0)
        r, i, a, s_ = r_ref[...].astype(F32), i_ref[...].astype(F32), a_ref[...], s_ref[...]
        sp = _softplus_neg(lam[...])
        db = db_ref[...].astype(F32)
        dy = db * hdg_ref[...].astype(F32)
        a_up = _shift_up(a, 1, anext[...])
        g = _scan_bwd(a_up, db * gy_ref[...].astype(F32), gnext[0:1, :])
        gnext[...] = g[0:8]
        anext[...] = a[0:8]
        da = g * _shift_down(h, 1, hprev)
        ixc = i * xc
        dla = da * a - (g * ixc) * (a * a) / s_
        di = g * s_ * xc
        dxc = g * s_ * i
        dzr = dla * (-LRU_C * sp) * r * (1.0 - r)
        dzi = di * i * (1.0 - i)
        lam_v = lam[...]
        _row_acc(sm_ref, 7, jnp.sum(dla * (LRU_C * r), axis=0, keepdims=True) * _sigmoid(-lam_v))
        _row_acc(sm_ref, 5, jnp.sum(dzr, axis=0, keepdims=True))
        _row_acc(sm_ref, 6, jnp.sum(dzi, axis=0, keepdims=True))
        parts, dwr_parts, dwi_parts = [], [], []
        for n in range(LRU_BLOCKS):
            cols = slice(n * LRU_BLOCK, (n + 1) * LRU_BLOCK)
            xb, zr, zi = _bf(xc[:, cols]), _bf(dzr[:, cols]), _bf(dzi[:, cols])
            parts.append(dxc[:, cols] + _dot_nt(zr, wr[n]) + _dot_nt(zi, wi[n]))
            dwr_parts.append(_dot_tn(xb, zr))
            dwi_parts.append(_dot_tn(xb, zi))
        dwr_ref[...] += jnp.stack(dwr_parts)
        dwi_ref[...] += jnp.stack(dwi_parts)
        dxc = jnp.concatenate(parts, axis=1)
        _row_acc(sm_ref, 4, jnp.sum(dxc, axis=0, keepdims=True))
        nxt = dxcnext[...]
        dx = jnp.zeros_like(x)
        for j in range(4):
            ahead = _shift_up(dxc, 3 - j, nxt)
            dx = dx + cw[j:j + 1, :] * ahead
            _row_acc(sm_ref, j, jnp.sum(ahead * x, axis=0, keepdims=True))
        dxcnext[...] = dxc[0:8]
        d_ref[:, 0:D_MODEL] = _bf(dx)
        d_ref[:, D_MODEL:2 * D_MODEL] = _bf(dy)

    rev = lambda col: (lambda i, c: (i, nc - 1 - c, col))
    prev = lambda col: (lambda i, c: (i, jnp.maximum((nc - 1 - c) * blk8 - 1, 0), col))
    return _call(
        body, name="lru_bwd", grid=(b, nc), comm=comm,
        in_specs=[pl.BlockSpec((None, SEQ_T, D_MODEL), rev(3))]
        + [pl.BlockSpec((None, SEQ_T, D_MODEL), rev(0))] * len(kept)
        + [pl.BlockSpec((None, 8, D_MODEL), prev(0)), pl.BlockSpec((None, SEQ_T, D_MODEL), rev(0))]
        + _LRU_PARAM_SPECS,
        out_specs=[pl.BlockSpec((None, SEQ_T, 2 * D_MODEL), rev(0)),
                   pl.BlockSpec((LRU_BLOCKS, LRU_BLOCK, LRU_BLOCK), lambda i, c: (0, 0, 0)),
                   pl.BlockSpec((LRU_BLOCKS, LRU_BLOCK, LRU_BLOCK), lambda i, c: (0, 0, 0)),
                   pl.BlockSpec((8, D_MODEL), lambda i, c: (0, 0))],
        out_shape=[jax.ShapeDtypeStruct((b, s, 2 * D_MODEL), BF16),
                   jax.ShapeDtypeStruct((LRU_BLOCKS, LRU_BLOCK, LRU_BLOCK), F32),
                   jax.ShapeDtypeStruct((LRU_BLOCKS, LRU_BLOCK, LRU_BLOCK), F32),
                   jax.ShapeDtypeStruct((8, D_MODEL), F32)],
        scratch=[pltpu.VMEM((8, D_MODEL), F32)] * 3,
    )(proj3, *kept, hseq, db3, *params)


def _merge_parts(a_ref, b_ref, gr_ref, gl_ref, mgb_ref, wro_ref, wlo_ref):
    ya = _dot(a_ref[...], wro_ref[...])
    yb = _dot(b_ref[...], wlo_ref[...])
    sa = _sigmoid(gr_ref[...].astype(F32) + mgb_ref[0:1, :])
    sb = _sigmoid(gl_ref[...].astype(F32) + mgb_ref[1:2, :])
    return ya, yb, sa, sb


def _merge_specs(tm):
    row = lambda col: pl.BlockSpec((tm, D_MODEL), lambda i: (i, col))
    full = pl.BlockSpec((D_MODEL, D_MODEL), lambda i: (0, 0))
    return row, full


def _merge_fwd(a_in, b_in, proj, mgb, wro, wlo, wout, x, *, tm, comm=None):
    m = x.shape[0]
    row, full = _merge_specs(tm)

    def body(a_ref, b_ref, gr_ref, gl_ref, mgb_ref, wro_ref, wlo_ref, wout_ref, x_ref,
             o_ref, mix_ref, ya_ref, yb_ref):
        ya, yb, sa, sb = _merge_parts(a_ref, b_ref, gr_ref, gl_ref, mgb_ref, wro_ref, wlo_ref)
        mix = _bf(sa * ya + sb * yb)
        o_ref[...] = x_ref[...] + _dot(mix, wout_ref[...])
        mix_ref[...] = mix
        ya_ref[...] = _bf(ya)
        yb_ref[...] = _bf(yb)

    act = jax.ShapeDtypeStruct((m, D_MODEL), BF16)
    return _call(
        body, name="merge_fwd", grid=(m // tm,), comm=comm,
        in_specs=[row(0), row(0), row(5), row(6), pl.BlockSpec((2, D_MODEL), lambda i: (0, 0)),
                  full, full, full, row(0)],
        out_specs=[row(0)] * 4,
        out_shape=[jax.ShapeDtypeStruct((m, D_MODEL), F32), act, act, act],
    )(a_in, b_in, proj, proj, mgb, wro, wlo, wout, x)


def _merge_bwd(ya, yb, proj, mgb, wro, wlo, wout, dx2, *, tm):
    m = dx2.shape[0]
    row, full = _merge_specs(tm)

    def body(ya_ref, yb_ref, gr_ref, gl_ref, mgb_ref, wro_ref, wlo_ref, wout_ref, dx_ref,
             dya_ref, dyb_ref, da_ref, db_ref, dg_ref, sm_ref):
        @pl.when(pl.program_id(0) == 0)
        def _():
            sm_ref[...] = jnp.zeros_like(sm_ref)

        ya, yb = ya_ref[...].astype(F32), yb_ref[...].astype(F32)
        sa = _sigmoid(gr_ref[...].astype(F32) + mgb_ref[0:1, :])
        sb = _sigmoid(gl_ref[...].astype(F32) + mgb_ref[1:2, :])
        dmix = _dot_nt(_bf(dx_ref[...]), wout_ref[...])
        dya, dyb = _bf(dmix * sa), _bf(dmix * sb)
        dya_ref[...] = dya
        dyb_ref[...] = dyb
        dga = dmix * ya * sa * (1.0 - sa)
        dgb = dmix * yb * sb * (1.0 - sb)
        dg_ref[:, 0:D_MODEL] = _bf(dga)
        dg_ref[:, D_MODEL:2 * D_MODEL] = _bf(dgb)
        _row_acc(sm_ref, ROW_MGB, jnp.sum(dga, axis=0, keepdims=True))
        _row_acc(sm_ref, ROW_MGB + 1, jnp.sum(dgb, axis=0, keepdims=True))
        da_ref[...] = _bf(_dot_nt(dya, wro_ref[...]))
        db_ref[...] = _bf(_dot_nt(dyb, wlo_ref[...]))

    act = jax.ShapeDtypeStruct((m, D_MODEL), BF16)
    return _call(
        body, name="merge_bwd", grid=(m // tm,),
        in_specs=[row(0), row(0), row(5), row(6), pl.BlockSpec((2, D_MODEL), lambda i: (0, 0)),
                  full, full, full, row(0)],
        out_specs=[row(0)] * 4 + [pl.BlockSpec((tm, 2 * D_MODEL), lambda i: (i, 0)),
                                  pl.BlockSpec((8, D_MODEL), lambda i: (0, 0))],
        out_shape=[act] * 4 + [jax.ShapeDtypeStruct((m, 2 * D_MODEL), BF16),
                               jax.ShapeDtypeStruct((8, D_MODEL), F32)],
    )(ya, yb, proj, proj, mgb, wro, wlo, wout, dx2)


def _ffn_act_fwd(up3, cw, cb):
    b, s, _ = up3.shape

    def body(g_ref, v_ref, cw_ref, cb_ref, o_ref, act_ref, q_ref, gprev):
        @pl.when(pl.program_id(1) == 0)
        def _():
            gprev[...] = jnp.zeros_like(gprev)

        gate, val = g_ref[...].astype(F32), v_ref[...].astype(F32)
        prev8 = gprev[...]
        gc = cb_ref[...] + sum(cw_ref[j:j + 1, :] * _shift_down(gate, 2 - j, prev8) for j in range(3))
        gprev[...] = gate[SEQ_T - 8:]
        act, dact = _gelu_and_grad(gc)
        o_ref[...] = _bf(act * val)
        act_ref[...] = _bf(act)
        q_ref[...] = _bf(dact * val)

    out = pl.BlockSpec((None, SEQ_T, D_FF), lambda i, c: (i, c, 0))
    return _call(
        body, name="ffn_act_fwd", grid=(b, s // SEQ_T),
        in_specs=[pl.BlockSpec((None, SEQ_T, D_FF), lambda i, c: (i, c, 0)),
                  pl.BlockSpec((None, SEQ_T, D_FF), lambda i, c: (i, c, 1)),
                  pl.BlockSpec((3, D_FF), lambda i, c: (0, 0)),
                  pl.BlockSpec((1, D_FF), lambda i, c: (0, 0))],
        out_specs=[out] * 3,
        out_shape=[jax.ShapeDtypeStruct((b, s, D_FF), BF16)] * 3,
        scratch=[pltpu.VMEM((8, D_FF), F32)],
    )(up3, up3, cw, cb)


def _ffn_act_bwd(up3, act3, q3, cw, df3, comm=None):
    b, s, _ = up3.shape
    nc = s // SEQ_T

    def body(g_ref, act_ref, q_ref, df_ref, cw_ref, dg_ref, dv_ref, sm_ref, dgcnext):
        c = pl.program_id(1)

        @pl.when(jnp.logical_and(pl.program_id(0) == 0, c == 0))
        def _():
            sm_ref[...] = jnp.zeros_like(sm_ref)

        @pl.when(c == 0)
        def _():
            dgcnext[...] = jnp.zeros_like(dgcnext)

        gate = g_ref[...].astype(F32)
        df = df_ref[...].astype(F32)
        dv_ref[...] = _bf(df * act_ref[...].astype(F32))
        dgc = df * q_ref[...].astype(F32)
        nxt = dgcnext[...]
        dgate = jnp.zeros_like(gate)
        for j in range(3):
            ahead = _shift_up(dgc, 2 - j, nxt)
            dgate = dgate + cw_ref[j:j + 1, :] * ahead
            _row_acc(sm_ref, j, jnp.sum(ahead * gate, axis=0, keepdims=True))
        _row_acc(sm_ref, 3, jnp.sum(dgc, axis=0, keepdims=True))
        dgcnext[...] = dgc[0:8]
        dg_ref[...] = _bf(dgate)

    rev = pl.BlockSpec((None, SEQ_T, D_FF), lambda i, c: (i, nc - 1 - c, 0))
    return _call(
        body, name="ffn_act_bwd", grid=(b, nc), comm=comm,
        in_specs=[rev, rev, rev, rev, pl.BlockSpec((3, D_FF), lambda i, c: (0, 0))],
        out_specs=[rev, rev, pl.BlockSpec((8, D_FF), lambda i, c: (0, 0))],
        out_shape=[jax.ShapeDtypeStruct((b, s, D_FF), BF16)] * 2 + [jax.ShapeDtypeStruct((8, D_FF), F32)],
        scratch=[pltpu.VMEM((8, D_FF), F32)],
    )(up3, act3, q3, df3, cw)


def _ffn_down_loss(f, wd, x2, nfw, target, *, tm):
    m, kf = f.shape
    nt = m // tm

    def body(f_ref, wd_ref, x_ref, nw_ref, t_ref, dx_ref, dnw_ref, lsum):
        i = pl.program_id(0)

        @pl.when(i == 0)
        def _():
            dnw_ref[...] = jnp.zeros_like(dnw_ref)
            lsum[...] = jnp.zeros_like(lsum)

        x3 = x_ref[...] + _dot(f_ref[...], wd_ref[...])
        nw = nw_ref[...]
        xh, r = _rms(x3)
        err = xh * nw - t_ref[...]
        lsum[...] += jnp.sum(err * err, axis=0, keepdims=True)
        dy = err * (1.0 / D_MODEL)
        g = dy * nw
        dx_ref[...] = r * (g - xh * jnp.mean(g * xh, axis=-1, keepdims=True))
        _row_acc(dnw_ref, ROW_NF, jnp.sum(dy * xh, axis=0, keepdims=True))

        @pl.when(i == nt - 1)
        def _():
            loss = jnp.sum(lsum[...], axis=1, keepdims=True) * (0.5 / D_MODEL)
            dnw_ref[ROW_LOSS:ROW_LOSS + 1, :] = jnp.broadcast_to(loss, (1, D_MODEL))

    row = pl.BlockSpec((tm, D_MODEL), lambda i: (i, 0))
    return _call(
        body, name="ffn_down_loss", grid=(nt,),
        in_specs=[pl.BlockSpec((tm, kf), lambda i: (i, 0)),
                  pl.BlockSpec((kf, D_MODEL), lambda i: (0, 0)),
                  row, pl.BlockSpec((1, D_MODEL), lambda i: (0, 0)), row],
        out_specs=[row, pl.BlockSpec((8, D_MODEL), lambda i: (0, 0))],
        out_shape=[jax.ShapeDtypeStruct((m, D_MODEL), F32), jax.ShapeDtypeStruct((8, D_MODEL), F32)],
        scratch=[pltpu.VMEM((1, D_MODEL), F32)],
    )(f, wd, x2, nfw, target)


def _row_tile(rows):
    return next((t for t in (256, 128, 64, 32, 16, 8) if rows % t == 0), rows)


def _adamw(w, gs, m, v, *, name):
    rows, cols = w.shape
    tr = _row_tile(rows)
    ng = len(gs)

    def body(w_ref, *rest):
        g_refs, (m_ref, v_ref, g_out, d_out, m_out, v_out) = rest[:ng], rest[ng:]
        g = g_refs[0][...]
        for r in g_refs[1:]:
            g = g + r[...]
        mn = ADAM_B1 * m_ref[...] + (1.0 - ADAM_B1) * g
        vn = ADAM_B2 * v_ref[...] + (1.0 - ADAM_B2) * (g * g)
        m_hat = mn / (1.0 - ADAM_B1 ** ADAM_STEP)
        v_hat = vn / (1.0 - ADAM_B2 ** ADAM_STEP)
        g_out[...] = g
        d_out[...] = -ADAM_LR * (m_hat / (jnp.sqrt(v_hat) + ADAM_EPS) + ADAM_WD * w_ref[...])
        m_out[...] = mn
        v_out[...] = vn

    spec = pl.BlockSpec((tr, cols), lambda i: (i, 0))
    return _call(
        body, name=name, grid=(rows // tr,),
        in_specs=[spec] * (3 + ng), out_specs=[spec] * 4,
        out_shape=[jax.ShapeDtypeStruct((rows, cols), F32)] * 4,
    )(w, *gs, m, v)


def _mesh_pos():
    x, y, c = lax.axis_index("x"), lax.axis_index("y"), lax.axis_index("c")
    return x, y, c


def _other_chips(x, y, c):
    return [((1 - x, y, c), 2 * (1 - x) + y), ((x, 1 - y, c), 2 * x + 1 - y),
            ((1 - x, 1 - y, c), 2 * (1 - x) + 1 - y)]


def _region(ref, axis, size, half_axis, chip, core=None):
    idx = [slice(None)] * len(ref.shape)
    if core is None:
        idx[axis] = pl.ds(pl.multiple_of(chip * size, size), size)
    elif half_axis == axis:
        h = size // 2
        idx[axis] = pl.ds(pl.multiple_of(chip * size + core * h, h), h)
    else:
        idx[axis] = pl.ds(pl.multiple_of(chip * size, size), size)
        h = ref.shape[half_axis] // 2
        idx[half_axis] = pl.ds(pl.multiple_of(core * h, h), h)
    return ref.at[tuple(idx)]


def _half(ref, half_axis, core):
    idx = [slice(None)] * len(ref.shape)
    h = ref.shape[half_axis] // 2
    idx[half_axis] = pl.ds(pl.multiple_of(core * h, h), h)
    return ref.at[tuple(idx)]


class _Copy:
    def __init__(self, make):
        self._make = make

    def start(self):
        self._make().start()

    def wait(self):
        self._make().wait()

    def wait_send(self):
        self._make().wait_send()

    def wait_recv(self):
        self._make().wait_recv()


def _remote(src, dst, send_sem, recv_sem, dev):
    return _Copy(lambda: pltpu.make_async_remote_copy(
        src_ref=src, dst_ref=dst, send_sem=send_sem, recv_sem=recv_sem, device_id=dev, device_id_type=MESH))


def _local(src, dst, sem):
    return _Copy(lambda: pltpu.make_async_copy(src, dst, sem))


def _dma_sems(n):
    return pltpu.SemaphoreType.DMA((n,))


def _place_shard(w, chip, axis, *, name):
    shape = list(w.shape)
    shape[axis] *= N_CHIPS
    if w.ndim == 3:
        block, grid = (1,) + w.shape[1:], (w.shape[0],)
        in_map, out_map = (lambda i, chip: (i, 0, 0)), (lambda i, chip: (i, chip[0], 0))
    else:
        tr = _row_tile(w.shape[0])
        nt = w.shape[0] // tr
        block, grid = (tr, w.shape[1]), (nt,)
        in_map = lambda i, chip: (i, 0)
        out_map = (lambda i, chip: (chip[0] * nt + i, 0)) if axis == 0 else (lambda i, chip: (i, chip[0]))

    def body(chip_ref, w_ref, o_ref):
        o_ref[...] = _bf(w_ref[...])

    return _call(body, name=name, grid=grid, prefetch=1, in_specs=[pl.BlockSpec(block, in_map)],
                 out_specs=pl.BlockSpec(block, out_map),
                 out_shape=jax.ShapeDtypeStruct(tuple(shape), BF16))(chip, w)


def _ici_leg(srcs, dsts, layout, sizes, n_whole, sems):
    send_sems, recv_sems, local_sems = sems
    x, y, c = _mesh_pos()
    mine = 2 * x + y
    n_big = len(srcs) - n_whole
    local, sends, recvs = [], [], []
    for t, (src, dst) in enumerate(zip(srcs, dsts)):
        if t < n_big:
            ax, hx = layout[t]
            part = _region(src, ax, sizes[t], hx, mine, c)
            landing = lambda chip, dst=dst, ax=ax, hx=hx, size=sizes[t]: _region(dst, ax, size, hx, chip, c)
        else:
            part, landing = src, (lambda chip, dst=dst: dst.at[chip])
            local.append(_local(src, dst.at[mine], local_sems.at[t - n_big]))
        for k, (dev, chip) in enumerate(_other_chips(x, y, c)):
            sends.append(_remote(part, landing(mine), send_sems.at[3 * t + k], recv_sems.at[3 * t + k], dev))
            recvs.append(_remote(part, landing(chip), send_sems.at[3 * t + k], recv_sems.at[3 * t + k], dev))
    return local, sends, recvs


def _d2d_leg(srcs, dsts, layout, sizes, sems):
    send_sems, recv_sems = sems
    x, y, c = _mesh_pos()
    sends, recvs = [], []
    for t, (src, dst) in enumerate(zip(srcs, dsts)):
        ax, hx = layout[t]
        for k, (_, chip) in enumerate(_other_chips(x, y, c)):
            sem = (send_sems.at[3 * t + k], recv_sems.at[3 * t + k])
            sends.append(_remote(_region(src, ax, sizes[t], hx, chip, c),
                                 _region(dst, ax, sizes[t], hx, chip, c), *sem, (x, y, 1 - c)))
            recvs.append(_remote(_region(src, ax, sizes[t], hx, chip, 1 - c),
                                 _region(dst, ax, sizes[t], hx, chip, 1 - c), *sem, (x, y, 1 - c)))
    return sends, recvs


def _gather_shapes(bufs, whole):
    return ([jax.ShapeDtypeStruct(b.shape, b.dtype) for b in bufs]
            + [jax.ShapeDtypeStruct((N_CHIPS,) + w.shape, w.dtype) for w in whole])


def _gather_ici(bufs, layout):
    n = len(bufs)
    sizes = [b.shape[ax] // N_CHIPS for b, (ax, _) in zip(bufs, layout)]

    def start(ins, outs, sems):
        for cp in _ici_leg(ins, outs, layout, sizes, 0, (*sems, None))[1]:
            cp.start()

    def finish(ins, outs, sems):
        _, sends, recvs = _ici_leg(ins, outs, layout, sizes, 0, (*sems, None))
        for cp in recvs:
            cp.wait_recv()
        for cp in sends:
            cp.wait_send()

    return _Comm(bufs, _gather_shapes(bufs, ()), [_dma_sems(3 * n), _dma_sems(3 * n)], start, finish,
                 aliases={i: i for i in range(n)})


def _both(a, b):
    ni, no, ns = len(a.ins), len(a.outs), len(a.sems)

    def start(ins, outs, sems):
        a.start(ins[:ni], outs[:no], sems[:ns])
        b.start(ins[ni:], outs[no:], sems[ns:])

    def finish(ins, outs, sems):
        a.finish(ins[:ni], outs[:no], sems[:ns])
        b.finish(ins[ni:], outs[no:], sems[ns:])

    aliases = {**a.aliases, **{ni + i: no + o for i, o in b.aliases.items()}}
    return _Comm(a.ins + b.ins, a.outs + b.outs, a.sems + b.sems, start, finish, aliases)


def _gather_d2d(bufs, layout, sizes):
    n = len(bufs)

    def start(ins, outs, sems):
        for cp in _d2d_leg(ins, outs, layout, sizes, sems)[0]:
            cp.start()

    def finish(ins, outs, sems):
        sends, recvs = _d2d_leg(ins, outs, layout, sizes, sems)
        for cp in recvs:
            cp.wait_recv()
        for cp in sends:
            cp.wait_send()

    return _Comm(bufs, [jax.ShapeDtypeStruct(b.shape, b.dtype) for b in bufs],
                 [_dma_sems(3 * n), _dma_sems(3 * n)], start, finish, aliases={i: i for i in range(n)})


def _norm_bf16(x, nw, *, name, tm):
    m, d = x.shape

    def body(x_ref, nw_ref, h_ref):
        h_ref[...] = _bf(_rms(x_ref[...])[0] * nw_ref[...])

    row = pl.BlockSpec((tm, d), lambda i: (i, 0))
    return _call(body, name=name, grid=(m // tm,), in_specs=[row, pl.BlockSpec((1, d), lambda i: (0, 0))],
                 out_specs=row, out_shape=jax.ShapeDtypeStruct((m, d), BF16))(x, nw)


def _in_proj_gather(h1, w_buf, later, later_cut, small, order, *, tm):
    m, d = h1.shape
    width = w_buf.shape[1] // N_CHIPS
    nr, nl = m // tm, len(later)
    sizes = [b.shape[ax] // N_CHIPS for b, (ax, _) in zip(later, later_cut)]

    def body(order_ref, h_ref, w_in, *rest):
        later_in, small_in = rest[:nl], rest[nl]
        o_ref, w_out = rest[nl + 1], rest[nl + 2]
        later_out, small_out = rest[nl + 3:2 * nl + 3], rest[2 * nl + 3]
        wv, load_sem, ici_send, ici_recv, d2d_send, d2d_recv, l_send, l_recv, l_local = rest[2 * nl + 4:]
        s, i = pl.program_id(0), pl.program_id(1)
        x, y, c = _mesh_pos()
        mine = 2 * x + y
        peers = _other_chips(x, y, c)
        part = lambda ref, chip, core=None: _region(ref, 1, width, 0, chip, core)

        def ici(k):
            dev, chip = peers[k]
            sem = (ici_send.at[k], ici_recv.at[k])
            return (_remote(part(w_in, mine, c), part(w_out, mine, c), *sem, dev),
                    _remote(part(w_in, chip, c), part(w_out, chip, c), *sem, dev))

        def d2d(k):
            chip, sem, sib = peers[k][1], (d2d_send.at[k], d2d_recv.at[k]), (x, y, 1 - c)
            return (_remote(part(w_out, chip, c), part(w_out, chip, c), *sem, sib),
                    _remote(part(w_out, chip, 1 - c), part(w_out, chip, 1 - c), *sem, sib))

        def load(src, chip, slot):
            cp = _local(part(src, chip), wv.at[slot], load_sem.at[slot])
            cp.start()
            cp.wait()

        def others():
            return _ici_leg(list(later_in) + [small_in], list(later_out) + [small_out], later_cut, sizes, 1,
                            (l_send, l_recv, l_local))

        @pl.when(jnp.logical_and(s == 0, i == 0))
        def _():
            for k in range(3):
                ici(k)[0].start()
            local, sends, _ = others()
            for cp in local + sends:
                cp.start()
            load(w_in, mine, 0)

        o_ref[...] = _bf(_dot(h_ref[...], wv[s % 2]))

        @pl.when(i == nr - 1)
        def _():
            for k in range(3):
                @pl.when(s == k)
                def _(k=k):
                    ici(k)[1].wait_recv()
                    d2d(k)[0].start()
                    d2d(k)[1].wait_recv()
                    load(w_out, peers[k][1], (k + 1) % 2)

            @pl.when(s == 3)
            def _():
                for k in range(3):
                    ici(k)[0].wait_send()
                    d2d(k)[0].wait_send()
                local, sends, recvs = others()
                for cp in recvs:
                    cp.wait_recv()
                for cp in sends:
                    cp.wait_send()
                for cp in local:
                    cp.wait()

    any_spec = pl.BlockSpec(memory_space=pl.ANY)
    n_any = nl + 2
    outs = _call(
        body, name="in_proj", grid=(N_CHIPS, nr), prefetch=1,
        in_specs=[pl.BlockSpec((tm, d), lambda s, i, order: (i, 0))] + [any_spec] * n_any,
        out_specs=[pl.BlockSpec((tm, width), lambda s, i, order: (i, order[s]))] + [any_spec] * n_any,
        out_shape=[jax.ShapeDtypeStruct((m, w_buf.shape[1]), BF16)] + _gather_shapes([w_buf] + list(later), [small]),
        scratch=[pltpu.VMEM((2, d, width), BF16), _dma_sems(2), _dma_sems(3), _dma_sems(3), _dma_sems(3),
                 _dma_sems(3), _dma_sems(3 * (nl + 1)), _dma_sems(3 * (nl + 1)), _dma_sems(1)],
        aliases={2 + t: 1 + t for t in range(nl + 1)},
    )(order, h1, w_buf, *later, small)
    return outs[0], outs[1], list(outs[2:2 + nl]), outs[2 + nl]


def _exchange(grads, layout):
    n = len(grads)
    others = N_DEV - 1
    sizes = [g.shape[ax] // N_CHIPS for g, (ax, _) in zip(grads, layout)]
    out_shapes = []
    for g, (ax, hx), sz in zip(grads, layout, sizes):
        shape = list(g.shape)
        shape[ax] = sz
        shape[hx] //= 2
        out_shapes.append(jax.ShapeDtypeStruct((others,) + tuple(shape), g.dtype))

    def copies(ins, outs, sems):
        send_sems, recv_sems = sems
        x, y, c = _mesh_pos()
        sends, recvs = [], []
        for t, (src, dst) in enumerate(zip(ins, outs)):
            ax, hx = layout[t]
            for r in range(1, N_DEV):
                px = (1 - x) if r & 4 else x
                py = (1 - y) if r & 2 else y
                pc = (1 - c) if r & 1 else c
                sem = (send_sems.at[others * t + r - 1], recv_sems.at[others * t + r - 1])
                part = _region(src, ax, sizes[t], hx, 2 * px + py, pc)
                sends.append(_remote(part, dst.at[r - 1], *sem, (px, py, pc)))
                recvs.append(_remote(part, dst.at[r - 1], *sem, (px, py, pc)))
        return sends, recvs

    def start(ins, outs, sems):
        for cp in copies(ins, outs, sems)[0]:
            cp.start()

    def finish(ins, outs, sems):
        sends, recvs = copies(ins, outs, sems)
        for cp in recvs:
            cp.wait_recv()
        for cp in sends:
            cp.wait_send()

    return _Comm(grads, out_shapes, [_dma_sems(others * n), _dma_sems(others * n)], start, finish)


def _reduce_half(g, parts, pos, cut, *, name):
    ax, _ = cut
    others = parts.shape[0]
    if g.ndim == 3:
        nb, rows, cols = g.shape
        hb = nb // 2
        block, grid, out_shape = (1, rows // N_CHIPS, cols), (hb,), (nb, rows // N_CHIPS, cols)
        g_map = lambda i, pos: (pos[1] * hb + i, pos[0], 0)
        o_map = lambda i, pos: (pos[1] * hb + i, 0, 0)
        p_map = lambda i, pos: (0, i, 0, 0)
    elif ax == 1:
        rows, cols = g.shape
        tr = _row_tile(rows // 2)
        nt = rows // 2 // tr
        block, grid, out_shape = (tr, cols // N_CHIPS), (nt,), (rows, cols // N_CHIPS)
        g_map = lambda i, pos: (pos[1] * nt + i, pos[0])
        o_map = lambda i, pos: (pos[1] * nt + i, 0)
        p_map = lambda i, pos: (0, i, 0)
    else:
        rows, cols = g.shape
        tr = _row_tile(rows // N_CHIPS // 2)
        nt = rows // N_CHIPS // 2 // tr
        block, grid, out_shape = (tr, cols), (nt,), (rows // N_CHIPS, cols)
        g_map = lambda i, pos: (pos[0] * 2 * nt + pos[1] * nt + i, 0)
        o_map = lambda i, pos: (pos[1] * nt + i, 0)
        p_map = lambda i, pos: (0, i, 0)

    def body(pos_ref, g_ref, p_ref, o_ref):
        acc = g_ref[...].astype(F32)
        for r in range(others):
            acc = acc + p_ref[r].astype(F32)
        o_ref[...] = acc

    return _call(
        body, name=name, grid=grid, prefetch=1,
        in_specs=[pl.BlockSpec(block, g_map), pl.BlockSpec((others,) + block, p_map)],
        out_specs=pl.BlockSpec(block, o_map), out_shape=jax.ShapeDtypeStruct(out_shape, F32),
    )(pos, g, parts)


def _join_halves(bufs):
    return _call(None, name="join_halves", comm=_join_comm(bufs))()[1]


def _join_comm(bufs):
    n = len(bufs)

    def copies(ins, outs, sems):
        send_sems, recv_sems = sems
        x, y, c = _mesh_pos()
        sends = [_remote(_half(src, 0, c), _half(dst, 0, c), send_sems.at[t], recv_sems.at[t], (x, y, 1 - c))
                 for t, (src, dst) in enumerate(zip(ins, outs))]
        recvs = [_remote(_half(src, 0, 1 - c), _half(dst, 0, 1 - c), send_sems.at[t], recv_sems.at[t],
                         (x, y, 1 - c)) for t, (src, dst) in enumerate(zip(ins, outs))]
        return sends, recvs

    def start(ins, outs, sems):
        for cp in copies(ins, outs, sems)[0]:
            cp.start()

    def finish(ins, outs, sems):
        sends, recvs = copies(ins, outs, sems)
        for cp in recvs:
            cp.wait_recv()
        for cp in sends:
            cp.wait_send()

    return _Comm(bufs, [jax.ShapeDtypeStruct(b.shape, b.dtype) for b in bufs],
                 [_dma_sems(n), _dma_sems(n)], start, finish, aliases={i: i for i in range(n)})


def _allreduce_small(pack):
    rows, cols = pack.shape

    def body(p_ref, o_ref, slots, send_sems, recv_sems):
        x, y, c = _mesh_pos()
        me = 4 * x + 2 * y + c
        slots[me] = p_ref[...]
        copies = []
        for r in range(1, N_DEV):
            fx, fy, fc = (r >> 2) & 1, (r >> 1) & 1, r & 1
            dev = ((1 - x) if fx else x, (1 - y) if fy else y, (1 - c) if fc else c)
            cp = pltpu.make_async_remote_copy(
                src_ref=p_ref, dst_ref=slots.at[me], send_sem=send_sems.at[r - 1],
                recv_sem=recv_sems.at[r - 1], device_id=dev, device_id_type=MESH)
            cp.start()
            copies.append(cp)
        for cp in copies:
            cp.wait_recv()
        for cp in copies:
            cp.wait_send()
        acc = slots[0]
        for d in range(1, N_DEV):
            acc = acc + slots[d]
        o_ref[...] = acc

    vmem = pl.BlockSpec(memory_space=pltpu.VMEM)
    return _call(
        body, name="allreduce_small", in_specs=[vmem], out_specs=vmem,
        out_shape=jax.ShapeDtypeStruct((rows, cols), F32),
        scratch=[pltpu.VMEM((N_DEV, rows, cols), F32), pltpu.SemaphoreType.DMA((N_DEV - 1,)),
                 pltpu.SemaphoreType.DMA((N_DEV - 1,))],
    )(pack)


def _pad_rows(a, rows=8):
    return jnp.pad(a, ((0, rows - a.shape[0]), (0, 0)))


def kernel(x, positions, norm1_w, w_in, merge_gate_b, ret_gn_w, w_ret_o, lru_conv_w, lru_conv_b, lru_w_r, lru_b_r, lru_w_i, lru_b_i, lru_lambda, w_lru_o, w_out, norm2_w, ffn_w_up, ffn_conv_w, ffn_conv_b, ffn_w_down, norm_f_w, loss_target, m_norm1_w, m_w_in, m_merge_gate_b, m_ret_gn_w, m_w_ret_o, m_lru_conv_w, m_lru_conv_b, m_lru_w_r, m_lru_b_r, m_lru_w_i, m_lru_b_i, m_lru_lambda, m_w_lru_o, m_w_out, m_norm2_w, m_ffn_w_up, m_ffn_conv_w, m_ffn_conv_b, m_ffn_w_down, m_norm_f_w, v_norm1_w, v_w_in, v_merge_gate_b, v_ret_gn_w, v_w_ret_o, v_lru_conv_w, v_lru_conv_b, v_lru_w_r, v_lru_b_r, v_lru_w_i, v_lru_b_i, v_lru_lambda, v_w_lru_o, v_w_out, v_norm2_w, v_ffn_w_up, v_ffn_conv_w, v_ffn_conv_b, v_ffn_w_down, v_norm_f_w):
    names = ["norm1_w", "w_in", "merge_gate_b", "ret_gn_w", "w_ret_o", "lru_conv_w", "lru_conv_b", "lru_w_r",
             "lru_b_r", "lru_w_i", "lru_b_i", "lru_lambda", "w_lru_o", "w_out", "norm2_w", "ffn_w_up",
             "ffn_conv_w", "ffn_conv_b", "ffn_w_down", "norm_f_w"]
    w_args = dict(zip(names, [norm1_w, w_in, merge_gate_b, ret_gn_w, w_ret_o, lru_conv_w, lru_conv_b, lru_w_r,
                              lru_b_r, lru_w_i, lru_b_i, lru_lambda, w_lru_o, w_out, norm2_w, ffn_w_up,
                              ffn_conv_w, ffn_conv_b, ffn_w_down, norm_f_w]))
    m_args = dict(zip(names, [m_norm1_w, m_w_in, m_merge_gate_b, m_ret_gn_w, m_w_ret_o, m_lru_conv_w,
                              m_lru_conv_b, m_lru_w_r, m_lru_b_r, m_lru_w_i, m_lru_b_i, m_lru_lambda, m_w_lru_o,
                              m_w_out, m_norm2_w, m_ffn_w_up, m_ffn_conv_w, m_ffn_conv_b, m_ffn_w_down,
                              m_norm_f_w]))
    v_args = dict(zip(names, [v_norm1_w, v_w_in, v_merge_gate_b, v_ret_gn_w, v_w_ret_o, v_lru_conv_w,
                              v_lru_conv_b, v_lru_w_r, v_lru_b_r, v_lru_w_i, v_lru_b_i, v_lru_lambda, v_w_lru_o,
                              v_w_out, v_norm2_w, v_ffn_w_up, v_ffn_conv_w, v_ffn_conv_b, v_ffn_w_down,
                              v_norm_f_w]))

    bsz, seq, d = x.shape
    m = bsz * seq
    tm = min(MM_ROWS, m)
    tm_fused = min(FUSED_ROWS, m)
    tm_tall = min(TALL_ROWS, m)
    chip = 2 * lax.axis_index("x") + lax.axis_index("y")

    big = ["w_in", "w_ret_o", "w_lru_o", "w_out", "lru_w_r", "lru_w_i", "ffn_w_up", "ffn_w_down"]
    cut = dict(w_in=(1, 0), w_ret_o=(0, 0), w_lru_o=(0, 0), w_out=(0, 0), lru_w_r=(1, 0), lru_w_i=(1, 0),
               ffn_w_up=(1, 0), ffn_w_down=(0, 0))
    core = lax.axis_index("c")
    chip1 = jnp.reshape(chip, (1,)).astype(jnp.int32)
    pos = jnp.stack([chip, core]).astype(jnp.int32)
    placed = {n: _place_shard(w_args[n][0], chip1, cut[n][0], name="place_" + n) for n in big}
    small_pack = jnp.concatenate([
        jnp.pad(merge_gate_b[0], ((0, 6), (0, 512))),
        jnp.pad(lru_conv_w[0], ((0, 4), (0, 512))),
        jnp.pad(lru_b_r[0], ((0, 4), (0, 704))),
        jnp.pad(lru_b_i[0], ((0, 4), (0, 704))),
        jnp.pad(ffn_conv_w[0], ((0, 5), (0, 0))),
    ], axis=0)
    x2d = x.reshape(m, d)
    mx, my = lax.axis_index("x"), lax.axis_index("y")
    order = jnp.stack([chip, 2 * (1 - mx) + my, 2 * mx + 1 - my, 2 * (1 - mx) + 1 - my]).astype(jnp.int32)
    mixer = ["w_ret_o", "w_lru_o", "w_out", "lru_w_r", "lru_w_i"]
    cuts = lambda ns: [cut[n] for n in ns]
    sizes = lambda ns: [w_args[n].shape[1 + cut[n][0]] for n in ns]
    h1 = _norm_bf16(x2d, norm1_w, name="norm1", tm=tm)
    proj, w_in_full, bufs, sp = _in_proj_gather(h1, placed["w_in"], [placed[n] for n in mixer], cuts(mixer),
                                               small_pack, order, tm=tm_tall)
    wb = {"w_in": w_in_full}
    mgb = jnp.transpose(sp[:, 0:2, 0:256], (1, 0, 2)).reshape(2, D_MODEL)
    lcw = jnp.transpose(sp[:, 8:12, 0:256], (1, 0, 2)).reshape(4, D_MODEL)
    lbr = jnp.transpose(sp[:, 16:20, 0:64], (1, 0, 2)).reshape(1, D_MODEL)
    lbi = jnp.transpose(sp[:, 24:28, 0:64], (1, 0, 2)).reshape(1, D_MODEL)
    fcw = jnp.transpose(sp[:, 32:35, :], (1, 0, 2)).reshape(3, D_FF)
    nfw = norm_f_w.reshape(1, D_MODEL)

    half = RET_DK // 2
    inv_freq = ROPE_BASE ** (-jnp.arange(half, dtype=F32) / half)
    cos, sin = _rope_tables(positions.reshape(bsz, seq, 1), jnp.concatenate([inv_freq, inv_freq]).reshape(1, RET_DK))
    proj3 = proj.reshape(bsz, seq, D_IN)
    down, up_w = ["ffn_w_down"], ["ffn_w_up"]
    (a_in3, states), bufs = _retention_fwd(
        proj3, cos, sin, ret_gn_w,
        comm=_both(_gather_d2d(bufs, cuts(mixer), sizes(mixer)), _gather_ici([placed["ffn_w_down"]], cuts(down))))
    wb.update(zip(mixer, bufs[:len(mixer)]))
    lru_params = (lcw, lru_conv_b, wb["lru_w_r"], lbr, wb["lru_w_i"], lbi, lru_lambda)
    (b_in3, *lru_kept), (up_buf, wb["ffn_w_down"]) = _lru_fwd(
        proj3, lru_params,
        comm=_both(_gather_ici([placed["ffn_w_up"]], cuts(up_w)), _gather_d2d(bufs[len(mixer):], cuts(down), sizes(down))))
    a_in, b_in = a_in3.reshape(m, d), b_in3.reshape(m, d)
    (x2, mix, ya, yb), (wb["ffn_w_up"],) = _merge_fwd(
        a_in, b_in, proj, mgb, wb["w_ret_o"], wb["w_lru_o"], wb["w_out"], x2d, tm=tm_fused,
        comm=_gather_d2d([up_buf], cuts(up_w), sizes(up_w)))
    up, h2 = _norm_matmul(x2, norm2_w, wb["ffn_w_up"], name="ffn_up", tm=tm_tall, tn=TALL_COLS)
    up3 = up.reshape(bsz, seq, 2 * D_FF)
    f3, act3, q3 = _ffn_act_fwd(up3, fcw, ffn_conv_b)
    f = f3.reshape(m, D_FF)
    dx3, sm_nf = _ffn_down_loss(f, wb["ffn_w_down"], x2, nfw, loss_target.reshape(m, d), tm=tm)

    def send(*ns):
        return _exchange([g_full[n] for n in ns], [cut[n] for n in ns])

    g_full, parts = {}, {}
    df = _mm_nt(dx3, wb["ffn_w_down"], name="ffn_down_dx", tm=tm_tall, out_dtype=BF16)
    g_full["ffn_w_down"] = _mm_tn(f, [dx3], name="ffn_down_dw", tm=tm_tall)
    (dgate3, dval3, sm_ffn), (parts["ffn_w_down"],) = _ffn_act_bwd(
        up3, act3, q3, fcw, df.reshape(bsz, seq, D_FF), comm=send("ffn_w_down"))
    dup = [dgate3.reshape(m, D_FF), dval3.reshape(m, D_FF)]
    g_full["ffn_w_up"] = _mm_tn(h2, dup, name="ffn_up_dw", tm=tm_tall)
    (dx2, sm_n2), (parts["ffn_w_up"],) = _mm_nt_normbwd(
        dup, wb["ffn_w_up"], x2, norm2_w, dx3, name="ffn_up_dx", tm=tm, row=ROW_N2, comm=send("ffn_w_up"))
    dya, dyb, da_in, db_in, dgates, sm_mg = _merge_bwd(
        ya, yb, proj, mgb, wb["w_ret_o"], wb["w_lru_o"], wb["w_out"], dx2, tm=tm_fused)
    g_full["w_out"] = _mm_tn(mix, [dx2], name="out_dw", tm=tm_tall)
    g_full["w_ret_o"] = _mm_tn(a_in, [dya], name="ret_o_dw", tm=tm_tall)
    g_full["w_lru_o"] = _mm_tn(b_in, [dyb], name="lru_o_dw", tm=tm_tall)
    (dlru3, dwr, dwi, sm_lru), (parts["w_out"], parts["w_ret_o"], parts["w_lru_o"]) = _lru_bwd(
        proj3, lru_params, lru_kept, db_in.reshape(bsz, seq, d), comm=send("w_out", "w_ret_o", "w_lru_o"))
    g_full["lru_w_r"], g_full["lru_w_i"] = dwr.astype(BF16), dwi.astype(BF16)
    (dret3, sm_gn), (parts["lru_w_r"], parts["lru_w_i"]) = _retention_bwd(
        proj3, cos, sin, ret_gn_w, states, da_in.reshape(bsz, seq, d), comm=send("lru_w_r", "lru_w_i"))
    dproj = [dret3.reshape(m, 3072), dlru3.reshape(m, 2048), dgates]
    g_full["w_in"] = _mm_tn(h1, dproj, name="in_proj_dw", tm=tm_tall)
    half_sum = lambda n: _reduce_half(g_full[n], parts[n], pos, cut[n], name="sum_" + n)
    (grad_x, sm_n1), (parts["w_in"], *joined) = _mm_nt_normbwd(
        dproj, wb["w_in"], x2d, norm1_w, dx2, name="in_proj_dx", tm=tm, row=ROW_N1,
        comm=_both(send("w_in"), _join_comm([half_sum(n) for n in big[1:]])))
    reduced = _join_halves([half_sum("w_in")]) + joined
    misc = sm_n1 + sm_mg + sm_gn + sm_n2 + sm_nf
    pack = jnp.concatenate(
        [misc, sm_lru, sm_ffn[:, 0:1024], sm_ffn[:, 1024:2048], sm_ffn[:, 2048:3072]], axis=0)
    tot = _allreduce_small(pack)
    ffn_sm = jnp.concatenate([tot[16:24], tot[24:32], tot[32:40]], axis=1)
    g_small = {
        "norm1_w": tot[ROW_N1:ROW_N1 + 1], "merge_gate_b": tot[ROW_MGB:ROW_MGB + 2],
        "ret_gn_w": tot[ROW_GN:ROW_GN + 1], "norm2_w": tot[ROW_N2:ROW_N2 + 1], "norm_f_w": tot[ROW_NF:ROW_NF + 1],
        "lru_conv_w": tot[8:12], "lru_conv_b": tot[12:13], "lru_b_r": tot[13:14].reshape(4, 256),
        "lru_b_i": tot[14:15].reshape(4, 256), "lru_lambda": tot[15:16],
        "ffn_conv_w": ffn_sm[0:3], "ffn_conv_b": ffn_sm[3:4],
    }
    small_shard = dict(merge_gate_b=256, lru_conv_w=256, lru_b_r=64, lru_b_i=64, ffn_conv_w=768)

    outs = {}
    for n, g in zip(big, reduced):
        shape = w_args[n].shape
        g = g.reshape(-1, g.shape[-1])
        outs[n] = [o.reshape(shape) for o in _adamw(
            w_args[n].reshape(g.shape), [g], m_args[n].reshape(g.shape), v_args[n].reshape(g.shape),
            name="adamw_" + n)]
    for n, g in g_small.items():
        shape = w_args[n].shape
        if n in small_shard:
            g = lax.dynamic_slice_in_dim(g, chip * small_shard[n], small_shard[n], axis=1)
        w2 = w_args[n].reshape(g.shape)
        outs[n] = [o.reshape(shape) for o in _adamw(
            w2, [g], m_args[n].reshape(g.shape), v_args[n].reshape(g.shape), name="adamw_" + n)]

    result = [tot[ROW_LOSS, 0], grad_x.reshape(bsz, seq, d)]
    for k in range(4):
        result += [outs[n][k] for n in names]
    return tuple(result)
```

```python
import functools
import math

import numpy as np
import jax
import jax.numpy as jnp
from jax import lax
from jax.experimental import pallas as pl
from jax.experimental.pallas import tpu as pltpu

F32 = jnp.float32
BF16 = jnp.bfloat16

D_MODEL = 1024
RET_HEADS = 4
RET_DK = 128
RET_DV = 256
LRU_BLOCKS = 4
LRU_BLOCK = 256
LRU_C = 8.0
D_FF = 3072
D_IN = 7168
ROPE_BASE = 10000.0
RMS_EPS = 1e-6
GN_EPS = 1e-6
ADAM_LR, ADAM_B1, ADAM_B2, ADAM_EPS, ADAM_WD, ADAM_STEP = 0.001, 0.9, 0.999, 1e-08, 0.01, 10

N_CHIPS = 4
N_DEV = 8
SEQ_T = 256
RET_T = 512
REF_CHUNK = 64
COL = 1024
MM_ROWS = 1024
TALL_ROWS, TALL_COLS = 2048, 1024
FUSED_ROWS = 512
VMEM_LIMIT_BYTES = 56 * 1024 * 1024
MESH = pl.DeviceIdType.MESH
ROW_N1, ROW_MGB, ROW_GN, ROW_N2, ROW_NF, ROW_LOSS = 0, 1, 3, 4, 5, 6
GELU_K = math.sqrt(2.0 / math.pi)
GELU_C = 0.044715


class _Comm:
    def __init__(self, ins, outs, sems, start, finish, aliases=None):
        self.ins, self.outs, self.sems = list(ins), list(outs), list(sems)
        self.start, self.finish, self.aliases = start, finish, dict(aliases or {})


def _call(body, *, name, out_shape=(), grid=None, in_specs=(), out_specs=(), scratch=(), comm=None, prefetch=0,
          aliases=None):
    single = not isinstance(out_shape, (list, tuple))
    out_shape = [out_shape] if single else list(out_shape)
    out_specs = [out_specs] if single else list(out_specs)
    in_specs, scratch = list(in_specs), list(scratch)
    n_in, n_out, n_scr = len(in_specs), len(out_shape), len(scratch)
    kwargs = dict(name=name, compiler_params=pltpu.CompilerParams(vmem_limit_bytes=VMEM_LIMIT_BYTES))
    if prefetch:
        assert comm is None
        spec = pltpu.PrefetchScalarGridSpec(num_scalar_prefetch=prefetch, grid=grid, in_specs=in_specs,
                                            out_specs=out_specs, scratch_shapes=scratch)
        fn = pl.pallas_call(body, out_shape=out_shape, grid_spec=spec, input_output_aliases=dict(aliases or {}),
                            **kwargs)
        return (lambda *args: fn(*args)[0]) if single else fn
    if grid is not None:
        kwargs["grid"] = grid
    if comm is None:
        fn = pl.pallas_call(body, out_shape=out_shape, in_specs=in_specs, out_specs=out_specs,
                            scratch_shapes=scratch, **kwargs)
        return (lambda *args: fn(*args)[0]) if single else fn

    any_spec = pl.BlockSpec(memory_space=pl.ANY)
    n_cin, n_cout = len(comm.ins), len(comm.outs)

    def wrapped(*refs):
        ins, refs = refs[:n_in], refs[n_in:]
        cins, refs = refs[:n_cin], refs[n_cin:]
        outs, refs = refs[:n_out], refs[n_out:]
        couts, refs = refs[:n_cout], refs[n_cout:]
        scr, csems = refs[:n_scr], refs[n_scr:]
        if grid is None:
            comm.start(cins, couts, csems)
            comm.finish(cins, couts, csems)
            return
        ids = [pl.program_id(a) for a in range(len(grid))]
        first = functools.reduce(jnp.logical_and, [i == 0 for i in ids])
        last = functools.reduce(jnp.logical_and, [i == g - 1 for i, g in zip(ids, grid)])
        pl.when(first)(lambda: comm.start(cins, couts, csems))
        body(*ins, *outs, *scr)
        pl.when(last)(lambda: comm.finish(cins, couts, csems))

    fn = pl.pallas_call(
        wrapped, out_shape=out_shape + comm.outs, in_specs=in_specs + [any_spec] * n_cin,
        out_specs=out_specs + [any_spec] * n_cout, scratch_shapes=scratch + comm.sems,
        input_output_aliases={n_in + i: n_out + o for i, o in comm.aliases.items()}, **kwargs)

    def run(*args):
        res = fn(*args, *comm.ins)
        own = res[0] if single else list(res[:n_out])
        return own, list(res[n_out:])

    return run


def _dot(a, b):
    return jnp.dot(a, b, preferred_element_type=F32)


def _dot_nt(a, b):
    return lax.dot_general(a, b, (((1,), (1,)), ((), ())), preferred_element_type=F32)


def _dot_tn(a, b):
    return lax.dot_general(a, b, (((0,), (0,)), ((), ())), preferred_element_type=F32)


def _bf(x):
    return x.astype(BF16)


def _sigmoid(x):
    return 1.0 / (1.0 + jnp.exp(-x))


def _gelu_and_grad(x):
    x2 = x * x
    s = _sigmoid(x * (2.0 * GELU_K * GELU_C * x2 + 2.0 * GELU_K))
    g = x * s
    dg = s + g * (1.0 - s) * (6.0 * GELU_K * GELU_C * x2 + 2.0 * GELU_K)
    return g, dg


def _rms(x):
    r = lax.rsqrt(jnp.mean(x * x, axis=-1, keepdims=True) + RMS_EPS)
    return x * r, r


def _rms_bwd(dy, x, nw):
    xh, r = _rms(x)
    g = dy * nw
    dx = r * (g - xh * jnp.mean(g * xh, axis=-1, keepdims=True))
    return dx, jnp.sum(dy * xh, axis=0, keepdims=True)


def _row_acc(ref, row, val):
    ref[row:row + 1, :] = ref[row:row + 1, :] + val


def _shift_down(x, j, prev8):
    if j == 0:
        return x
    n = x.shape[0] // 8
    row = lax.broadcasted_iota(jnp.int32, prev8.shape, 0)
    turned = [pltpu.roll(prev8, j, 0)] + [pltpu.roll(x[8 * k:8 * k + 8], j, 0) for k in range(n)]
    return jnp.concatenate([jnp.where(row < j, turned[k], turned[k + 1]) for k in range(n)], axis=0)


def _shift_up(x, j, next8):
    if j == 0:
        return x
    n = x.shape[0] // 8
    row = lax.broadcasted_iota(jnp.int32, next8.shape, 0)
    turned = [pltpu.roll(x[8 * k:8 * k + 8], 8 - j, 0) for k in range(n)] + [pltpu.roll(next8, 8 - j, 0)]
    return jnp.concatenate([jnp.where(row >= 8 - j, turned[k + 1], turned[k]) for k in range(n)], axis=0)


def _scan_fwd(a, b, carry):
    row = lax.broadcasted_iota(jnp.int32, (8, a.shape[1]), 0)
    out = []
    for k in range(a.shape[0] // 8):
        ak, bk = a[8 * k:8 * k + 8], b[8 * k:8 * k + 8]
        for s in (1, 2, 4):
            keep = row >= s
            ar, br = pltpu.roll(ak, s, 0), pltpu.roll(bk, s, 0)
            bk = jnp.where(keep, ak * br + bk, bk)
            ak = jnp.where(keep, ak * ar, ak)
        hk = ak * carry + bk
        carry = hk[7:8]
        out.append(hk)
    return jnp.concatenate(out, axis=0)


def _scan_bwd(a, b, carry):
    row = lax.broadcasted_iota(jnp.int32, (8, a.shape[1]), 0)
    out = []
    for k in reversed(range(a.shape[0] // 8)):
        ak, bk = a[8 * k:8 * k + 8], b[8 * k:8 * k + 8]
        for s in (1, 2, 4):
            keep = row < 8 - s
            ar, br = pltpu.roll(ak, 8 - s, 0), pltpu.roll(bk, 8 - s, 0)
            bk = jnp.where(keep, ak * br + bk, bk)
            ak = jnp.where(keep, ak * ar, ak)
        gk = bk + ak * carry
        carry = gk[0:1]
        out.append(gk)
    return jnp.concatenate(out[::-1], axis=0)


def _norm_matmul(x, nw, w, *, name, tm, tn):
    m, d = x.shape
    n = w.shape[1]

    def body(x_ref, nw_ref, w_ref, o_ref, h_ref, h_sc):
        @pl.when(pl.program_id(1) == 0)
        def _():
            xh, _ = _rms(x_ref[...])
            h = _bf(xh * nw_ref[...])
            h_sc[...] = h
            h_ref[...] = h

        o_ref[...] = _bf(_dot(h_sc[...], w_ref[...]))

    return _call(
        body, name=name, grid=(m // tm, n // tn),
        in_specs=[pl.BlockSpec((tm, d), lambda i, j: (i, 0)),
                  pl.BlockSpec((1, d), lambda i, j: (0, 0)),
                  pl.BlockSpec((d, tn), lambda i, j: (0, j))],
        out_specs=[pl.BlockSpec((tm, tn), lambda i, j: (i, j)),
                   pl.BlockSpec((tm, d), lambda i, j: (i, 0))],
        out_shape=[jax.ShapeDtypeStruct((m, n), BF16), jax.ShapeDtypeStruct((m, d), BF16)],
        scratch=[pltpu.VMEM((tm, d), BF16)],
    )(x, nw, w)


def _mm_nt(a, w, *, name, tm, out_dtype):
    m, k = a.shape
    n = w.shape[0]

    def body(a_ref, w_ref, o_ref):
        o_ref[...] = _dot_nt(_bf(a_ref[...]), w_ref[...]).astype(out_dtype)

    return _call(
        body, name=name, grid=(m // tm, n // COL),
        in_specs=[pl.BlockSpec((tm, k), lambda i, j: (i, 0)),
                  pl.BlockSpec((COL, k), lambda i, j: (j, 0))],
        out_specs=pl.BlockSpec((tm, COL), lambda i, j: (i, j)),
        out_shape=jax.ShapeDtypeStruct((m, n), out_dtype),
    )(a, w)


def _piece_layout(pieces):
    offs, nblk, o = [], [], 0
    for p in pieces:
        offs.append(o)
        nblk.append(p.shape[1] // COL)
        o += p.shape[1] // COL
    return offs, nblk, o


def _mm_tn(a, pieces, *, name, tm, out_dtype=BF16):
    m, k = a.shape
    offs, nblk, nn = _piece_layout(pieces)

    def piece_spec(o, nb):
        def idx(ki, nj, mi):
            use = jnp.logical_and(nj >= o, nj < o + nb)
            return (jnp.where(use, mi, 0), jnp.clip(nj - o, 0, nb - 1))
        return pl.BlockSpec((tm, COL), idx)

    def body(a_ref, *rest):
        p_refs, o_ref, acc = rest[:len(pieces)], rest[len(pieces)], rest[len(pieces) + 1]
        nj, mi = pl.program_id(1), pl.program_id(2)

        @pl.when(mi == 0)
        def _():
            acc[...] = jnp.zeros_like(acc)

        for p_ref, o, nb in zip(p_refs, offs, nblk):
            @pl.when(jnp.logical_and(nj >= o, nj < o + nb))
            def _(p_ref=p_ref):
                acc[...] += _dot_tn(_bf(a_ref[...]), _bf(p_ref[...]))

        @pl.when(mi == pl.num_programs(2) - 1)
        def _():
            o_ref[...] = acc[...].astype(out_dtype)

    return _call(
        body, name=name, grid=(k // COL, nn, m // tm),
        in_specs=[pl.BlockSpec((tm, COL), lambda ki, nj, mi: (mi, ki))]
        + [piece_spec(o, nb) for o, nb in zip(offs, nblk)],
        out_specs=pl.BlockSpec((COL, COL), lambda ki, nj, mi: (ki, nj)),
        out_shape=jax.ShapeDtypeStruct((k, nn * COL), out_dtype),
        scratch=[pltpu.VMEM((COL, COL), F32)],
    )(a, *pieces)


def _mm_nt_normbwd(pieces, w, x, nw, dres, *, name, tm, row, comm=None):
    m, d = x.shape
    offs, nblk, nk = _piece_layout(pieces)

    def piece_spec(o, nb):
        return pl.BlockSpec((tm, COL), lambda i, k: (i, jnp.clip(k - o, 0, nb - 1)))

    def body(*refs):
        p_refs = refs[:len(pieces)]
        w_ref, x_ref, nw_ref, dres_ref, dx_ref, dnw_ref, acc = refs[len(pieces):]
        i, k = pl.program_id(0), pl.program_id(1)

        @pl.when(jnp.logical_and(i == 0, k == 0))
        def _():
            dnw_ref[...] = jnp.zeros_like(dnw_ref)

        @pl.when(k == 0)
        def _():
            acc[...] = jnp.zeros_like(acc)

        for p_ref, o, nb in zip(p_refs, offs, nblk):
            @pl.when(jnp.logical_and(k >= o, k < o + nb))
            def _(p_ref=p_ref):
                acc[...] += _dot_nt(_bf(p_ref[...]), w_ref[...])

        @pl.when(k == nk - 1)
        def _():
            dx, dnw = _rms_bwd(acc[...], x_ref[...], nw_ref[...])
            dx_ref[...] = dres_ref[...] + dx
            _row_acc(dnw_ref, row, dnw)

    return _call(
        body, name=name, grid=(m // tm, nk), comm=comm,
        in_specs=[piece_spec(o, nb) for o, nb in zip(offs, nblk)]
        + [pl.BlockSpec((d, COL), lambda i, k: (0, k)),
           pl.BlockSpec((tm, d), lambda i, k: (i, 0)),
           pl.BlockSpec((1, d), lambda i, k: (0, 0)),
           pl.BlockSpec((tm, d), lambda i, k: (i, 0))],
        out_specs=[pl.BlockSpec((tm, d), lambda i, k: (i, 0)),
                   pl.BlockSpec((8, d), lambda i, k: (0, 0))],
        out_shape=[jax.ShapeDtypeStruct((m, d), F32), jax.ShapeDtypeStruct((8, d), F32)],
        scratch=[pltpu.VMEM((tm, d), F32)],
    )(*pieces, w, x, nw, dres)


def _rope_tables(pos3, invf):
    b, s, _ = pos3.shape

    def body(pos_ref, invf_ref, cos_ref, sin_ref):
        half_t, half_d = s // 2, RET_DK // 2
        pos = pos_ref[...].astype(F32)
        low = lax.broadcasted_iota(jnp.int32, (half_t, RET_DK), 1) < half_d
        ang = jnp.where(low, pos[0:half_t], pos[half_t:]) * invf_ref[...]
        co, si = jnp.cos(ang), jnp.sin(ang)
        co_turned, si_turned = pltpu.roll(co, half_d, 1), pltpu.roll(si, half_d, 1)
        sign = jnp.where(low, -1.0, 1.0)
        cos_ref[...] = jnp.concatenate([jnp.where(low, co, co_turned), jnp.where(low, co_turned, co)], axis=0)
        sin_ref[...] = jnp.concatenate([sign * jnp.where(low, si, si_turned), sign * jnp.where(low, si_turned, si)],
                                       axis=0)

    spec = pl.BlockSpec((None, s, RET_DK), lambda i: (i, 0, 0))
    return _call(
        body, name="rope_tables", grid=(b,),
        in_specs=[pl.BlockSpec((None, s, 1), lambda i: (i, 0, 0)),
                  pl.BlockSpec((1, RET_DK), lambda i: (0, 0))],
        out_specs=[spec, spec],
        out_shape=[jax.ShapeDtypeStruct((b, s, RET_DK), F32)] * 2,
    )(pos3, invf)


def _log_gamma(h):
    return float(np.log1p(-np.power(np.float32(2.0), np.float32(-5.0 - h))).astype(np.float32))


def _decay_matrix(h):
    lg = _log_gamma(h)
    n = lax.broadcasted_iota(jnp.int32, (RET_T, RET_T), 0)
    m = lax.broadcasted_iota(jnp.int32, (RET_T, RET_T), 1)
    same = (n // REF_CHUNK) == (m // REF_CHUNK)
    dist = jnp.where(same, jnp.abs(n - m), n - m).astype(F32)
    return jnp.where(jnp.logical_or(same, m < n), jnp.exp(lg * dist), 0.0)


def _decay_vectors(h):
    lg = _log_gamma(h)
    idx = lax.broadcasted_iota(jnp.int32, (RET_T, 1), 0).astype(F32)
    qd = jnp.exp(lg * (idx + 1.0))
    kd = jnp.exp(lg * (RET_T - 1.0 - idx))
    return qd, kd, math.exp(lg * RET_T)


def _rotate(x, cos, sin):
    return x * cos + pltpu.roll(x, RET_DK // 2, 1) * sin


def _rotate_bwd(d, cos, sin):
    return d * cos + pltpu.roll(d * sin, RET_DK // 2, 1)


def _ret_head(p_ref, cos, sin, h):
    q = p_ref[:, h * RET_DK:(h + 1) * RET_DK].astype(F32)
    k = p_ref[:, 512 + h * RET_DK:512 + (h + 1) * RET_DK].astype(F32)
    v = p_ref[:, 1024 + h * RET_DV:1024 + (h + 1) * RET_DV]
    g = p_ref[:, 2048 + h * RET_DV:2048 + (h + 1) * RET_DV].astype(F32)
    qr = _rotate(q, cos, sin)
    kr = _rotate(k, cos, sin) * (RET_DK ** -0.5)
    return qr, kr, v, g


def _group_norm(o):
    mu = jnp.mean(o, axis=-1, keepdims=True)
    oc = o - mu
    rstd = lax.rsqrt(jnp.mean(oc * oc, axis=-1, keepdims=True) + GN_EPS)
    return oc * rstd, rstd


def _retention_fwd(proj3, cos, sin, gnw, comm=None):
    b, s, _ = proj3.shape
    nc = s // RET_T

    def body(p_ref, cos_ref, sin_ref, gnw_ref, a_ref, st_ref, state, wtab):
        c = pl.program_id(1)

        @pl.when(jnp.logical_and(pl.program_id(0) == 0, c == 0))
        def _():
            for h in range(RET_HEADS):
                wtab[h] = _decay_matrix(h)

        @pl.when(c == 0)
        def _():
            state[...] = jnp.zeros_like(state)

        cs, sn = cos_ref[...], sin_ref[...]
        st_ref[...] = state[...]
        outs, states = [], []
        for h in range(RET_HEADS):
            qd, kd, gt = _decay_vectors(h)
            qr, kr, v, g = _ret_head(p_ref, cs, sn, h)
            st = state[h]
            p = _dot_nt(_bf(qr), _bf(kr)) * wtab[h]
            o = _dot(_bf(p), _bf(v)) + _dot(_bf(qr * qd), _bf(st))
            states.append(st * gt + _dot_tn(_bf(kr * kd), _bf(v)))
            on, _ = _group_norm(o)
            gw = gnw_ref[:, h * RET_DV:(h + 1) * RET_DV]
            outs.append(_bf(on * gw * (g * _sigmoid(g))))
        a_ref[...] = jnp.concatenate(outs, axis=1)
        state[...] = jnp.stack(states)

    tab = pl.BlockSpec((None, RET_T, RET_DK), lambda i, c: (i, c, 0))
    return _call(
        body, name="retention_fwd", grid=(b, nc), comm=comm,
        in_specs=[pl.BlockSpec((None, RET_T, 3072), lambda i, c: (i, c, 0)), tab, tab,
                  pl.BlockSpec((1, D_MODEL), lambda i, c: (0, 0))],
        out_specs=[pl.BlockSpec((None, RET_T, D_MODEL), lambda i, c: (i, c, 0)),
                   pl.BlockSpec((None, None, RET_HEADS, RET_DK, RET_DV), lambda i, c: (i, c, 0, 0, 0))],
        out_shape=[jax.ShapeDtypeStruct((b, s, D_MODEL), BF16),
                   jax.ShapeDtypeStruct((b, nc, RET_HEADS, RET_DK, RET_DV), F32)],
        scratch=[pltpu.VMEM((RET_HEADS, RET_DK, RET_DV), F32),
                 pltpu.VMEM((RET_HEADS, RET_T, RET_T), F32)],
    )(proj3, cos, sin, gnw)


def _retention_bwd(proj3, cos, sin, gnw, states, da3, comm=None):
    b, s, _ = proj3.shape
    nc = s // RET_T

    def body(p_ref, cos_ref, sin_ref, gnw_ref, st_ref, da_ref, d_ref, dgn_ref, dstate, wtab):
        c = pl.program_id(1)

        @pl.when(jnp.logical_and(pl.program_id(0) == 0, c == 0))
        def _():
            dgn_ref[...] = jnp.zeros_like(dgn_ref)
            for h in range(RET_HEADS):
                wtab[h] = _decay_matrix(h)

        @pl.when(c == 0)
        def _():
            dstate[...] = jnp.zeros_like(dstate)

        cs, sn = cos_ref[...], sin_ref[...]
        dqs, dks, dvs, dgs, dgns, dstates = [], [], [], [], [], []
        for h in range(RET_HEADS):
            qd, kd, gt = _decay_vectors(h)
            qr, kr, v, g = _ret_head(p_ref, cs, sn, h)
            st, dst, w = st_ref[h], dstate[h], wtab[h]
            qb, kb, vb = _bf(qr), _bf(kr), _bf(v)
            p = _dot_nt(qb, kb) * w
            o = _dot(_bf(p), vb) + _dot(_bf(qr * qd), _bf(st))
            on, rstd = _group_norm(o)
            gw = gnw_ref[:, h * RET_DV:(h + 1) * RET_DV]
            da = da_ref[:, h * RET_DV:(h + 1) * RET_DV].astype(F32)
            sg = _sigmoid(g)
            silu = g * sg
            dg = da * on * gw * (sg * (1.0 + g * (1.0 - sg)))
            dgns.append(jnp.sum(da * silu * on, axis=0, keepdims=True))
            don = da * silu * gw
            do = rstd * (don - jnp.mean(don, axis=-1, keepdims=True)
                         - on * jnp.mean(don * on, axis=-1, keepdims=True))
            dob = _bf(do)
            dp = _dot_nt(dob, vb) * w
            dqr = _dot(_bf(dp), kb) + _dot_nt(dob, _bf(st)) * qd
            dkr = _dot_tn(_bf(dp), qb) + _dot_nt(vb, _bf(dst)) * kd
            dv = _dot_tn(_bf(p), dob) + _dot(_bf(kr * kd), _bf(dst))
            dstates.append(dst * gt + _dot_tn(_bf(qr * qd), dob))
            dqs.append(_bf(_rotate_bwd(dqr, cs, sn)))
            dks.append(_bf(_rotate_bwd(dkr, cs, sn) * (RET_DK ** -0.5)))
            dvs.append(_bf(dv))
            dgs.append(_bf(dg))
        d_ref[...] = jnp.concatenate(dqs + dks + dvs + dgs, axis=1)
        _row_acc(dgn_ref, ROW_GN, jnp.concatenate(dgns, axis=1))
        dstate[...] = jnp.stack(dstates)

    rev = lambda i, c: (i, nc - 1 - c, 0)
    tab = pl.BlockSpec((None, RET_T, RET_DK), rev)
    return _call(
        body, name="retention_bwd", grid=(b, nc), comm=comm,
        in_specs=[pl.BlockSpec((None, RET_T, 3072), rev), tab, tab,
                  pl.BlockSpec((1, D_MODEL), lambda i, c: (0, 0)),
                  pl.BlockSpec((None, None, RET_HEADS, RET_DK, RET_DV), lambda i, c: (i, nc - 1 - c, 0, 0, 0)),
                  pl.BlockSpec((None, RET_T, D_MODEL), rev)],
        out_specs=[pl.BlockSpec((None, RET_T, 3072), rev),
                   pl.BlockSpec((8, D_MODEL), lambda i, c: (0, 0))],
        out_shape=[jax.ShapeDtypeStruct((b, s, 3072), BF16), jax.ShapeDtypeStruct((8, D_MODEL), F32)],
        scratch=[pltpu.VMEM((RET_HEADS, RET_DK, RET_DV), F32),
                 pltpu.VMEM((RET_HEADS, RET_T, RET_T), F32)],
    )(proj3, cos, sin, gnw, states, da3)


def _softplus_neg(lam):
    z = -lam
    u = jnp.exp(-jnp.abs(z))
    log1p_u = jnp.where(u < 0.01, u * (1.0 - u * (0.5 - u * (1.0 / 3.0))), jnp.log(1.0 + u))
    return jnp.maximum(z, 0.0) + log1p_u


def _lru_coeffs(xc, wr_ref, br_ref, wi_ref, bi_ref, lam_ref):
    rs, is_ = [], []
    for n in range(LRU_BLOCKS):
        xb = _bf(xc[:, n * LRU_BLOCK:(n + 1) * LRU_BLOCK])
        cols = slice(n * LRU_BLOCK, (n + 1) * LRU_BLOCK)
        rs.append(_sigmoid(_dot(xb, wr_ref[n]) + br_ref[:, cols]))
        is_.append(_sigmoid(_dot(xb, wi_ref[n]) + bi_ref[:, cols]))
    r = jnp.concatenate(rs, axis=1)
    i = jnp.concatenate(is_, axis=1)
    sp = _softplus_neg(lam_ref[...])
    la = -LRU_C * r * sp
    a = jnp.exp(la)
    s = jnp.sqrt(-jnp.tanh(la) * (a * a + 1.0))
    return r, i, a, s, sp


_LRU_PARAM_SPECS = [
    pl.BlockSpec((4, D_MODEL), lambda i, c: (0, 0)),
    pl.BlockSpec((1, D_MODEL), lambda i, c: (0, 0)),
    pl.BlockSpec((LRU_BLOCKS, LRU_BLOCK, LRU_BLOCK), lambda i, c: (0, 0, 0)),
    pl.BlockSpec((1, D_MODEL), lambda i, c: (0, 0)),
    pl.BlockSpec((LRU_BLOCKS, LRU_BLOCK, LRU_BLOCK), lambda i, c: (0, 0, 0)),
    pl.BlockSpec((1, D_MODEL), lambda i, c: (0, 0)),
    pl.BlockSpec((1, D_MODEL), lambda i, c: (0, 0)),
]


def _lru_fwd(proj3, params, comm=None):
    b, s, _ = proj3.shape
    rows = min(2 * SEQ_T, s)
    nc = s // rows

    def body(x_ref, y_ref, cw, cb, wr, br, wi, bi, lam,
             o_ref, h_ref, xc_ref, a_ref, s_ref, gy_ref, hdg_ref, r_ref, i_ref, xprev, hprev):
        @pl.when(pl.program_id(1) == 0)
        def _():
            xprev[...] = jnp.zeros_like(xprev)
            hprev[...] = jnp.zeros_like(hprev)

        x = x_ref[...].astype(F32)
        prev8 = xprev[...]
        xc = cb[...] + sum(cw[j:j + 1, :] * _shift_down(x, 3 - j, prev8) for j in range(4))
        xprev[...] = x[rows - 8:]
        xc_ref[...] = xc
        r, i, a, s_, _ = _lru_coeffs(xc, wr, br, wi, bi, lam)
        a_ref[...] = a
        s_ref[...] = s_
        r_ref[...] = _bf(r)
        i_ref[...] = _bf(i)
        h = _scan_fwd(a, s_ * (i * xc), hprev[7:8, :])
        hprev[...] = h[rows - 8:]
        h_ref[...] = h
        gy, dgy = _gelu_and_grad(y_ref[...].astype(F32))
        o_ref[...] = _bf(h * gy)
        gy_ref[...] = _bf(gy)
        hdg_ref[...] = _bf(h * dgy)

    out = pl.BlockSpec((None, rows, D_MODEL), lambda i, c: (i, c, 0))
    half, full = jax.ShapeDtypeStruct((b, s, D_MODEL), BF16), jax.ShapeDtypeStruct((b, s, D_MODEL), F32)
    return _call(
        body, name="lru_fwd", grid=(b, nc), comm=comm,
        in_specs=[pl.BlockSpec((None, rows, D_MODEL), lambda i, c: (i, c, 3)),
                  pl.BlockSpec((None, rows, D_MODEL), lambda i, c: (i, c, 4))] + _LRU_PARAM_SPECS,
        out_specs=[out] * 9, out_shape=[half, full, full, full, full, half, half, half, half],
        scratch=[pltpu.VMEM((8, D_MODEL), F32), pltpu.VMEM((8, D_MODEL), F32)],
    )(proj3, proj3, *params)


def _lru_bwd(proj3, params, kept, db3, comm=None):
    b, s, _ = proj3.shape
    nc = s // SEQ_T
    blk8 = SEQ_T // 8
    hseq = kept[0]

    def body(x_ref, h_ref, xc_ref, a_ref, s_ref, gy_ref, hdg_ref, r_ref, i_ref, hp_ref, db_ref,
             cw, cb, wr, br, wi, bi, lam, d_ref, dwr_ref, dwi_ref, sm_ref, gnext, anext, dxcnext):
        c = pl.program_id(1)
        first_chunk = c == nc - 1

        @pl.when(jnp.logical_and(pl.program_id(0) == 0, c == 0))
        def _():
            dwr_ref[...] = jnp.zeros_like(dwr_ref)
            dwi_ref[...] = jnp.zeros_like(dwi_ref)
            sm_ref[...] = jnp.zeros_like(sm_ref)

        @pl.when(c == 0)
        def _():
            gnext[...] = jnp.zeros_like(gnext)
            anext[...] = jnp.zeros_like(anext)
            dxcnext[...] = jnp.zeros_like(dxcnext)

        x, xc, h = x_ref[...].astype(F32), xc_ref[...], h_ref[...]
        hprev = hp_ref[...] * jnp.where(first_chunk, 0.0, 1.0)
        r, i, a, s_ = r_ref[...].astype(F32), i_ref[...].astype(F32), a_ref[...], s_ref[...]
        sp = _softplus_neg(lam[...])
        db = db_ref[...].astype(F32)
        dy = db * hdg_ref[...].astype(F32)
        a_up = _shift_up(a, 1, anext[...])
        g = _scan_bwd(a_up, db * gy_ref[...].astype(F32), gnext[0:1, :])
        gnext[...] = g[0:8]
        anext[...] = a[0:8]
        da = g * _shift_down(h, 1, hprev)
        ixc = i * xc
        dla = da * a - (g * ixc) * (a * a) / s_
        di = g * s_ * xc
        dxc = g * s_ * i
        dzr = dla * (-LRU_C * sp) * r * (1.0 - r)
        dzi = di * i * (1.0 - i)
        lam_v = lam[...]
        _row_acc(sm_ref, 7, jnp.sum(dla * (LRU_C * r), axis=0, keepdims=True) * _sigmoid(-lam_v))
        _row_acc(sm_ref, 5, jnp.sum(dzr, axis=0, keepdims=True))
        _row_acc(sm_ref, 6, jnp.sum(dzi, axis=0, keepdims=True))
        parts, dwr_parts, dwi_parts = [], [], []
        for n in range(LRU_BLOCKS):
            cols = slice(n * LRU_BLOCK, (n + 1) * LRU_BLOCK)
            xb, zr, zi = _bf(xc[:, cols]), _bf(dzr[:, cols]), _bf(dzi[:, cols])
            parts.append(dxc[:, cols] + _dot_nt(zr, wr[n]) + _dot_nt(zi, wi[n]))
            dwr_parts.append(_dot_tn(xb, zr))
            dwi_parts.append(_dot_tn(xb, zi))
        dwr_ref[...] += jnp.stack(dwr_parts)
        dwi_ref[...] += jnp.stack(dwi_parts)
        dxc = jnp.concatenate(parts, axis=1)
        _row_acc(sm_ref, 4, jnp.sum(dxc, axis=0, keepdims=True))
        nxt = dxcnext[...]
        dx = jnp.zeros_like(x)
        for j in range(4):
            ahead = _shift_up(dxc, 3 - j, nxt)
            dx = dx + cw[j:j + 1, :] * ahead
            _row_acc(sm_ref, j, jnp.sum(ahead * x, axis=0, keepdims=True))
        dxcnext[...] = dxc[0:8]
        d_ref[:, 0:D_MODEL] = _bf(dx)
        d_ref[:, D_MODEL:2 * D_MODEL] = _bf(dy)

    rev = lambda col: (lambda i, c: (i, nc - 1 - c, col))
    prev = lambda col: (lambda i, c: (i, jnp.maximum((nc - 1 - c) * blk8 - 1, 0), col))
    return _call(
        body, name="lru_bwd", grid=(b, nc), comm=comm,
        in_specs=[pl.BlockSpec((None, SEQ_T, D_MODEL), rev(3))]
        + [pl.BlockSpec((None, SEQ_T, D_MODEL), rev(0))] * len(kept)
        + [pl.BlockSpec((None, 8, D_MODEL), prev(0)), pl.BlockSpec((None, SEQ_T, D_MODEL), rev(0))]
        + _LRU_PARAM_SPECS,
        out_specs=[pl.BlockSpec((None, SEQ_T, 2 * D_MODEL), rev(0)),
                   pl.BlockSpec((LRU_BLOCKS, LRU_BLOCK, LRU_BLOCK), lambda i, c: (0, 0, 0)),
                   pl.BlockSpec((LRU_BLOCKS, LRU_BLOCK, LRU_BLOCK), lambda i, c: (0, 0, 0)),
                   pl.BlockSpec((8, D_MODEL), lambda i, c: (0, 0))],
        out_shape=[jax.ShapeDtypeStruct((b, s, 2 * D_MODEL), BF16),
                   jax.ShapeDtypeStruct((LRU_BLOCKS, LRU_BLOCK, LRU_BLOCK), F32),
                   jax.ShapeDtypeStruct((LRU_BLOCKS, LRU_BLOCK, LRU_BLOCK), F32),
                   jax.ShapeDtypeStruct((8, D_MODEL), F32)],
        scratch=[pltpu.VMEM((8, D_MODEL), F32)] * 3,
    )(proj3, *kept, hseq, db3, *params)


def _merge_parts(a_ref, b_ref, gr_ref, gl_ref, mgb_ref, wro_ref, wlo_ref):
    ya = _dot(a_ref[...], wro_ref[...])
    yb = _dot(b_ref[...], wlo_ref[...])
    sa = _sigmoid(gr_ref[...].astype(F32) + mgb_ref[0:1, :])
    sb = _sigmoid(gl_ref[...].astype(F32) + mgb_ref[1:2, :])
    return ya, yb, sa, sb


def _merge_specs(tm):
    row = lambda col: pl.BlockSpec((tm, D_MODEL), lambda i: (i, col))
    full = pl.BlockSpec((D_MODEL, D_MODEL), lambda i: (0, 0))
    return row, full


def _merge_fwd(a_in, b_in, proj, mgb, wro, wlo, wout, x, *, tm, comm=None):
    m = x.shape[0]
    row, full = _merge_specs(tm)

    def body(a_ref, b_ref, gr_ref, gl_ref, mgb_ref, wro_ref, wlo_ref, wout_ref, x_ref,
             o_ref, mix_ref, ya_ref, yb_ref):
        ya, yb, sa, sb = _merge_parts(a_ref, b_ref, gr_ref, gl_ref, mgb_ref, wro_ref, wlo_ref)
        mix = _bf(sa * ya + sb * yb)
        o_ref[...] = x_ref[...] + _dot(mix, wout_ref[...])
        mix_ref[...] = mix
        ya_ref[...] = _bf(ya)
        yb_ref[...] = _bf(yb)

    act = jax.ShapeDtypeStruct((m, D_MODEL), BF16)
    return _call(
        body, name="merge_fwd", grid=(m // tm,), comm=comm,
        in_specs=[row(0), row(0), row(5), row(6), pl.BlockSpec((2, D_MODEL), lambda i: (0, 0)),
                  full, full, full, row(0)],
        out_specs=[row(0)] * 4,
        out_shape=[jax.ShapeDtypeStruct((m, D_MODEL), F32), act, act, act],
    )(a_in, b_in, proj, proj, mgb, wro, wlo, wout, x)


def _merge_bwd(ya, yb, proj, mgb, wro, wlo, wout, dx2, *, tm):
    m = dx2.shape[0]
    row, full = _merge_specs(tm)

    def body(ya_ref, yb_ref, gr_ref, gl_ref, mgb_ref, wro_ref, wlo_ref, wout_ref, dx_ref,
             dya_ref, dyb_ref, da_ref, db_ref, dg_ref, sm_ref):
        @pl.when(pl.program_id(0) == 0)
        def _():
            sm_ref[...] = jnp.zeros_like(sm_ref)

        ya, yb = ya_ref[...].astype(F32), yb_ref[...].astype(F32)
        sa = _sigmoid(gr_ref[...].astype(F32) + mgb_ref[0:1, :])
        sb = _sigmoid(gl_ref[...].astype(F32) + mgb_ref[1:2, :])
        dmix = _dot_nt(_bf(dx_ref[...]), wout_ref[...])
        dya, dyb = _bf(dmix * sa), _bf(dmix * sb)
        dya_ref[...] = dya
        dyb_ref[...] = dyb
        dga = dmix * ya * sa * (1.0 - sa)
        dgb = dmix * yb * sb * (1.0 - sb)
        dg_ref[:, 0:D_MODEL] = _bf(dga)
        dg_ref[:, D_MODEL:2 * D_MODEL] = _bf(dgb)
        _row_acc(sm_ref, ROW_MGB, jnp.sum(dga, axis=0, keepdims=True))
        _row_acc(sm_ref, ROW_MGB + 1, jnp.sum(dgb, axis=0, keepdims=True))
        da_ref[...] = _bf(_dot_nt(dya, wro_ref[...]))
        db_ref[...] = _bf(_dot_nt(dyb, wlo_ref[...]))

    act = jax.ShapeDtypeStruct((m, D_MODEL), BF16)
    return _call(
        body, name="merge_bwd", grid=(m // tm,),
        in_specs=[row(0), row(0), row(5), row(6), pl.BlockSpec((2, D_MODEL), lambda i: (0, 0)),
                  full, full, full, row(0)],
        out_specs=[row(0)] * 4 + [pl.BlockSpec((tm, 2 * D_MODEL), lambda i: (i, 0)),
                                  pl.BlockSpec((8, D_MODEL), lambda i: (0, 0))],
        out_shape=[act] * 4 + [jax.ShapeDtypeStruct((m, 2 * D_MODEL), BF16),
                               jax.ShapeDtypeStruct((8, D_MODEL), F32)],
    )(ya, yb, proj, proj, mgb, wro, wlo, wout, dx2)


def _ffn_act_fwd(up3, cw, cb):
    b, s, _ = up3.shape

    def body(g_ref, v_ref, cw_ref, cb_ref, o_ref, act_ref, q_ref, gprev):
        @pl.when(pl.program_id(1) == 0)
        def _():
            gprev[...] = jnp.zeros_like(gprev)

        gate, val = g_ref[...].astype(F32), v_ref[...].astype(F32)
        prev8 = gprev[...]
        gc = cb_ref[...] + sum(cw_ref[j:j + 1, :] * _shift_down(gate, 2 - j, prev8) for j in range(3))
        gprev[...] = gate[SEQ_T - 8:]
        act, dact = _gelu_and_grad(gc)
        o_ref[...] = _bf(act * val)
        act_ref[...] = _bf(act)
        q_ref[...] = _bf(dact * val)

    out = pl.BlockSpec((None, SEQ_T, D_FF), lambda i, c: (i, c, 0))
    return _call(
        body, name="ffn_act_fwd", grid=(b, s // SEQ_T),
        in_specs=[pl.BlockSpec((None, SEQ_T, D_FF), lambda i, c: (i, c, 0)),
                  pl.BlockSpec((None, SEQ_T, D_FF), lambda i, c: (i, c, 1)),
                  pl.BlockSpec((3, D_FF), lambda i, c: (0, 0)),
                  pl.BlockSpec((1, D_FF), lambda i, c: (0, 0))],
        out_specs=[out] * 3,
        out_shape=[jax.ShapeDtypeStruct((b, s, D_FF), BF16)] * 3,
        scratch=[pltpu.VMEM((8, D_FF), F32)],
    )(up3, up3, cw, cb)


def _ffn_act_bwd(up3, act3, q3, cw, df3, comm=None):
    b, s, _ = up3.shape
    nc = s // SEQ_T

    def body(g_ref, act_ref, q_ref, df_ref, cw_ref, dg_ref, dv_ref, sm_ref, dgcnext):
        c = pl.program_id(1)

        @pl.when(jnp.logical_and(pl.program_id(0) == 0, c == 0))
        def _():
            sm_ref[...] = jnp.zeros_like(sm_ref)

        @pl.when(c == 0)
        def _():
            dgcnext[...] = jnp.zeros_like(dgcnext)

        gate = g_ref[...].astype(F32)
        df = df_ref[...].astype(F32)
        dv_ref[...] = _bf(df * act_ref[...].astype(F32))
        dgc = df * q_ref[...].astype(F32)
        nxt = dgcnext[...]
        dgate = jnp.zeros_like(gate)
        for j in range(3):
            ahead = _shift_up(dgc, 2 - j, nxt)
            dgate = dgate + cw_ref[j:j + 1, :] * ahead
            _row_acc(sm_ref, j, jnp.sum(ahead * gate, axis=0, keepdims=True))
        _row_acc(sm_ref, 3, jnp.sum(dgc, axis=0, keepdims=True))
        dgcnext[...] = dgc[0:8]
        dg_ref[...] = _bf(dgate)

    rev = pl.BlockSpec((None, SEQ_T, D_FF), lambda i, c: (i, nc - 1 - c, 0))
    return _call(
        body, name="ffn_act_bwd", grid=(b, nc), comm=comm,
        in_specs=[rev, rev, rev, rev, pl.BlockSpec((3, D_FF), lambda i, c: (0, 0))],
        out_specs=[rev, rev, pl.BlockSpec((8, D_FF), lambda i, c: (0, 0))],
        out_shape=[jax.ShapeDtypeStruct((b, s, D_FF), BF16)] * 2 + [jax.ShapeDtypeStruct((8, D_FF), F32)],
        scratch=[pltpu.VMEM((8, D_FF), F32)],
    )(up3, act3, q3, df3, cw)


def _ffn_down_loss(f, wd, x2, nfw, target, *, tm):
    m, kf = f.shape
    nt = m // tm

    def body(f_ref, wd_ref, x_ref, nw_ref, t_ref, dx_ref, dnw_ref, lsum):
        i = pl.program_id(0)

        @pl.when(i == 0)
        def _():
            dnw_ref[...] = jnp.zeros_like(dnw_ref)
            lsum[...] = jnp.zeros_like(lsum)

        x3 = x_ref[...] + _dot(f_ref[...], wd_ref[...])
        nw = nw_ref[...]
        xh, r = _rms(x3)
        err = xh * nw - t_ref[...]
        lsum[...] += jnp.sum(err * err, axis=0, keepdims=True)
        dy = err * (1.0 / D_MODEL)
        g = dy * nw
        dx_ref[...] = r * (g - xh * jnp.mean(g * xh, axis=-1, keepdims=True))
        _row_acc(dnw_ref, ROW_NF, jnp.sum(dy * xh, axis=0, keepdims=True))

        @pl.when(i == nt - 1)
        def _():
            loss = jnp.sum(lsum[...], axis=1, keepdims=True) * (0.5 / D_MODEL)
            dnw_ref[ROW_LOSS:ROW_LOSS + 1, :] = jnp.broadcast_to(loss, (1, D_MODEL))

    row = pl.BlockSpec((tm, D_MODEL), lambda i: (i, 0))
    return _call(
        body, name="ffn_down_loss", grid=(nt,),
        in_specs=[pl.BlockSpec((tm, kf), lambda i: (i, 0)),
                  pl.BlockSpec((kf, D_MODEL), lambda i: (0, 0)),
                  row, pl.BlockSpec((1, D_MODEL), lambda i: (0, 0)), row],
        out_specs=[row, pl.BlockSpec((8, D_MODEL), lambda i: (0, 0))],
        out_shape=[jax.ShapeDtypeStruct((m, D_MODEL), F32), jax.ShapeDtypeStruct((8, D_MODEL), F32)],
        scratch=[pltpu.VMEM((1, D_MODEL), F32)],
    )(f, wd, x2, nfw, target)


def _row_tile(rows):
    return next((t for t in (256, 128, 64, 32, 16, 8) if rows % t == 0), rows)


def _adamw(w, gs, m, v, *, name):
    rows, cols = w.shape
    tr = _row_tile(rows)
    ng = len(gs)

    def body(w_ref, *rest):
        g_refs, (m_ref, v_ref, g_out, d_out, m_out, v_out) = rest[:ng], rest[ng:]
        g = g_refs[0][...]
        for r in g_refs[1:]:
            g = g + r[...]
        mn = ADAM_B1 * m_ref[...] + (1.0 - ADAM_B1) * g
        vn = ADAM_B2 * v_ref[...] + (1.0 - ADAM_B2) * (g * g)
        m_hat = mn / (1.0 - ADAM_B1 ** ADAM_STEP)
        v_hat = vn / (1.0 - ADAM_B2 ** ADAM_STEP)
        g_out[...] = g
        d_out[...] = -ADAM_LR * (m_hat / (jnp.sqrt(v_hat) + ADAM_EPS) + ADAM_WD * w_ref[...])
        m_out[...] = mn
        v_out[...] = vn

    spec = pl.BlockSpec((tr, cols), lambda i: (i, 0))
    return _call(
        body, name=name, grid=(rows // tr,),
        in_specs=[spec] * (3 + ng), out_specs=[spec] * 4,
        out_shape=[jax.ShapeDtypeStruct((rows, cols), F32)] * 4,
    )(w, *gs, m, v)


def _mesh_pos():
    x, y, c = lax.axis_index("x"), lax.axis_index("y"), lax.axis_index("c")
    return x, y, c


def _other_chips(x, y, c):
    return [((1 - x, y, c), 2 * (1 - x) + y), ((x, 1 - y, c), 2 * x + 1 - y),
            ((1 - x, 1 - y, c), 2 * (1 - x) + 1 - y)]


def _region(ref, axis, size, half_axis, chip, core=None):
    idx = [slice(None)] * len(ref.shape)
    if core is None:
        idx[axis] = pl.ds(pl.multiple_of(chip * size, size), size)
    elif half_axis == axis:
        h = size // 2
        idx[axis] = pl.ds(pl.multiple_of(chip * size + core * h, h), h)
    else:
        idx[axis] = pl.ds(pl.multiple_of(chip * size, size), size)
        h = ref.shape[half_axis] // 2
        idx[half_axis] = pl.ds(pl.multiple_of(core * h, h), h)
    return ref.at[tuple(idx)]


def _half(ref, half_axis, core):
    idx = [slice(None)] * len(ref.shape)
    h = ref.shape[half_axis] // 2
    idx[half_axis] = pl.ds(pl.multiple_of(core * h, h), h)
    return ref.at[tuple(idx)]


class _Copy:
    def __init__(self, make):
        self._make = make

    def start(self):
        self._make().start()

    def wait(self):
        self._make().wait()

    def wait_send(self):
        self._make().wait_send()

    def wait_recv(self):
        self._make().wait_recv()


def _remote(src, dst, send_sem, recv_sem, dev):
    return _Copy(lambda: pltpu.make_async_remote_copy(
        src_ref=src, dst_ref=dst, send_sem=send_sem, recv_sem=recv_sem, device_id=dev, device_id_type=MESH))


def _local(src, dst, sem):
    return _Copy(lambda: pltpu.make_async_copy(src, dst, sem))


def _dma_sems(n):
    return pltpu.SemaphoreType.DMA((n,))


def _place_shard(w, chip, axis, *, name):
    shape = list(w.shape)
    shape[axis] *= N_CHIPS
    if w.ndim == 3:
        block, grid = (1,) + w.shape[1:], (w.shape[0],)
        in_map, out_map = (lambda i, chip: (i, 0, 0)), (lambda i, chip: (i, chip[0], 0))
    else:
        tr = _row_tile(w.shape[0])
        nt = w.shape[0] // tr
        block, grid = (tr, w.shape[1]), (nt,)
        in_map = lambda i, chip: (i, 0)
        out_map = (lambda i, chip: (chip[0] * nt + i, 0)) if axis == 0 else (lambda i, chip: (i, chip[0]))

    def body(chip_ref, w_ref, o_ref):
        o_ref[...] = _bf(w_ref[...])

    return _call(body, name=name, grid=grid, prefetch=1, in_specs=[pl.BlockSpec(block, in_map)],
                 out_specs=pl.BlockSpec(block, out_map),
                 out_shape=jax.ShapeDtypeStruct(tuple(shape), BF16))(chip, w)


def _ici_leg(srcs, dsts, layout, sizes, n_whole, sems):
    send_sems, recv_sems, local_sems = sems
    x, y, c = _mesh_pos()
    mine = 2 * x + y
    n_big = len(srcs) - n_whole
    local, sends, recvs = [], [], []
    for t, (src, dst) in enumerate(zip(srcs, dsts)):
        if t < n_big:
            ax, hx = layout[t]
            part = _region(src, ax, sizes[t], hx, mine, c)
            landing = lambda chip, dst=dst, ax=ax, hx=hx, size=sizes[t]: _region(dst, ax, size, hx, chip, c)
        else:
            part, landing = src, (lambda chip, dst=dst: dst.at[chip])
            local.append(_local(src, dst.at[mine], local_sems.at[t - n_big]))
        for k, (dev, chip) in enumerate(_other_chips(x, y, c)):
            sends.append(_remote(part, landing(mine), send_sems.at[3 * t + k], recv_sems.at[3 * t + k], dev))
            recvs.append(_remote(part, landing(chip), send_sems.at[3 * t + k], recv_sems.at[3 * t + k], dev))
    return local, sends, recvs


def _d2d_leg(srcs, dsts, layout, sizes, sems):
    send_sems, recv_sems = sems
    x, y, c = _mesh_pos()
    sends, recvs = [], []
    for t, (src, dst) in enumerate(zip(srcs, dsts)):
        ax, hx = layout[t]
        for k, (_, chip) in enumerate(_other_chips(x, y, c)):
            sem = (send_sems.at[3 * t + k], recv_sems.at[3 * t + k])
            sends.append(_remote(_region(src, ax, sizes[t], hx, chip, c),
                                 _region(dst, ax, sizes[t], hx, chip, c), *sem, (x, y, 1 - c)))
            recvs.append(_remote(_region(src, ax, sizes[t], hx, chip, 1 - c),
                                 _region(dst, ax, sizes[t], hx, chip, 1 - c), *sem, (x, y, 1 - c)))
    return sends, recvs


def _gather_shapes(bufs, whole):
    return ([jax.ShapeDtypeStruct(b.shape, b.dtype) for b in bufs]
            + [jax.ShapeDtypeStruct((N_CHIPS,) + w.shape, w.dtype) for w in whole])


def _gather_ici(bufs, layout):
    n = len(bufs)
    sizes = [b.shape[ax] // N_CHIPS for b, (ax, _) in zip(bufs, layout)]

    def start(ins, outs, sems):
        for cp in _ici_leg(ins, outs, layout, sizes, 0, (*sems, None))[1]:
            cp.start()

    def finish(ins, outs, sems):
        _, sends, recvs = _ici_leg(ins, outs, layout, sizes, 0, (*sems, None))
        for cp in recvs:
            cp.wait_recv()
        for cp in sends:
            cp.wait_send()

    return _Comm(bufs, _gather_shapes(bufs, ()), [_dma_sems(3 * n), _dma_sems(3 * n)], start, finish,
                 aliases={i: i for i in range(n)})


def _both(a, b):
    ni, no, ns = len(a.ins), len(a.outs), len(a.sems)

    def start(ins, outs, sems):
        a.start(ins[:ni], outs[:no], sems[:ns])
        b.start(ins[ni:], outs[no:], sems[ns:])

    def finish(ins, outs, sems):
        a.finish(ins[:ni], outs[:no], sems[:ns])
        b.finish(ins[ni:], outs[no:], sems[ns:])

    aliases = {**a.aliases, **{ni + i: no + o for i, o in b.aliases.items()}}
    return _Comm(a.ins + b.ins, a.outs + b.outs, a.sems + b.sems, start, finish, aliases)


def _gather_d2d(bufs, layout, sizes):
    n = len(bufs)

    def start(ins, outs, sems):
        for cp in _d2d_leg(ins, outs, layout, sizes, sems)[0]:
            cp.start()

    def finish(ins, outs, sems):
        sends, recvs = _d2d_leg(ins, outs, layout, sizes, sems)
        for cp in recvs:
            cp.wait_recv()
        for cp in sends:
            cp.wait_send()

    return _Comm(bufs, [jax.ShapeDtypeStruct(b.shape, b.dtype) for b in bufs],
                 [_dma_sems(3 * n), _dma_sems(3 * n)], start, finish, aliases={i: i for i in range(n)})


def _norm_bf16(x, nw, *, name, tm):
    m, d = x.shape

    def body(x_ref, nw_ref, h_ref):
        h_ref[...] = _bf(_rms(x_ref[...])[0] * nw_ref[...])

    row = pl.BlockSpec((tm, d), lambda i: (i, 0))
    return _call(body, name=name, grid=(m // tm,), in_specs=[row, pl.BlockSpec((1, d), lambda i: (0, 0))],
                 out_specs=row, out_shape=jax.ShapeDtypeStruct((m, d), BF16))(x, nw)


def _in_proj_gather(h1, w_buf, later, later_cut, small, order, *, tm):
    m, d = h1.shape
    width = w_buf.shape[1] // N_CHIPS
    nr, nl = m // tm, len(later)
    sizes = [b.shape[ax] // N_CHIPS for b, (ax, _) in zip(later, later_cut)]

    def body(order_ref, h_ref, w_in, *rest):
        later_in, small_in = rest[:nl], rest[nl]
        o_ref, w_out = rest[nl + 1], rest[nl + 2]
        later_out, small_out = rest[nl + 3:2 * nl + 3], rest[2 * nl + 3]
        wv, load_sem, ici_send, ici_recv, d2d_send, d2d_recv, l_send, l_recv, l_local = rest[2 * nl + 4:]
        s, i = pl.program_id(0), pl.program_id(1)
        x, y, c = _mesh_pos()
        mine = 2 * x + y
        peers = _other_chips(x, y, c)
        part = lambda ref, chip, core=None: _region(ref, 1, width, 0, chip, core)

        def ici(k):
            dev, chip = peers[k]
            sem = (ici_send.at[k], ici_recv.at[k])
            return (_remote(part(w_in, mine, c), part(w_out, mine, c), *sem, dev),
                    _remote(part(w_in, chip, c), part(w_out, chip, c), *sem, dev))

        def d2d(k):
            chip, sem, sib = peers[k][1], (d2d_send.at[k], d2d_recv.at[k]), (x, y, 1 - c)
            return (_remote(part(w_out, chip, c), part(w_out, chip, c), *sem, sib),
                    _remote(part(w_out, chip, 1 - c), part(w_out, chip, 1 - c), *sem, sib))

        def load(src, chip, slot):
            cp = _local(part(src, chip), wv.at[slot], load_sem.at[slot])
            cp.start()
            cp.wait()

        def others():
            return _ici_leg(list(later_in) + [small_in], list(later_out) + [small_out], later_cut, sizes, 1,
                            (l_send, l_recv, l_local))

        @pl.when(jnp.logical_and(s == 0, i == 0))
        def _():
            for k in range(3):
                ici(k)[0].start()
            local, sends, _ = others()
            for cp in local + sends:
                cp.start()
            load(w_in, mine, 0)

        o_ref[...] = _bf(_dot(h_ref[...], wv[s % 2]))

        @pl.when(i == nr - 1)
        def _():
            for k in range(3):
                @pl.when(s == k)
                def _(k=k):
                    ici(k)[1].wait_recv()
                    d2d(k)[0].start()
                    d2d(k)[1].wait_recv()
                    load(w_out, peers[k][1], (k + 1) % 2)

            @pl.when(s == 3)
            def _():
                for k in range(3):
                    ici(k)[0].wait_send()
                    d2d(k)[0].wait_send()
                local, sends, recvs = others()
                for cp in recvs:
                    cp.wait_recv()
                for cp in sends:
                    cp.wait_send()
                for cp in local:
                    cp.wait()

    any_spec = pl.BlockSpec(memory_space=pl.ANY)
    n_any = nl + 2
    outs = _call(
        body, name="in_proj", grid=(N_CHIPS, nr), prefetch=1,
        in_specs=[pl.BlockSpec((tm, d), lambda s, i, order: (i, 0))] + [any_spec] * n_any,
        out_specs=[pl.BlockSpec((tm, width), lambda s, i, order: (i, order[s]))] + [any_spec] * n_any,
        out_shape=[jax.ShapeDtypeStruct((m, w_buf.shape[1]), BF16)] + _gather_shapes([w_buf] + list(later), [small]),
        scratch=[pltpu.VMEM((2, d, width), BF16), _dma_sems(2), _dma_sems(3), _dma_sems(3), _dma_sems(3),
                 _dma_sems(3), _dma_sems(3 * (nl + 1)), _dma_sems(3 * (nl + 1)), _dma_sems(1)],
        aliases={2 + t: 1 + t for t in range(nl + 1)},
    )(order, h1, w_buf, *later, small)
    return outs[0], outs[1], list(outs[2:2 + nl]), outs[2 + nl]


def _exchange(grads, layout):
    n = len(grads)
    others = N_DEV - 1
    sizes = [g.shape[ax] // N_CHIPS for g, (ax, _) in zip(grads, layout)]
    out_shapes = []
    for g, (ax, hx), sz in zip(grads, layout, sizes):
        shape = list(g.shape)
        shape[ax] = sz
        shape[hx] //= 2
        out_shapes.append(jax.ShapeDtypeStruct((others,) + tuple(shape), g.dtype))

    def copies(ins, outs, sems):
        send_sems, recv_sems = sems
        x, y, c = _mesh_pos()
        sends, recvs = [], []
        for t, (src, dst) in enumerate(zip(ins, outs)):
            ax, hx = layout[t]
            for r in range(1, N_DEV):
                px = (1 - x) if r & 4 else x
                py = (1 - y) if r & 2 else y
                pc = (1 - c) if r & 1 else c
                sem = (send_sems.at[others * t + r - 1], recv_sems.at[others * t + r - 1])
                part = _region(src, ax, sizes[t], hx, 2 * px + py, pc)
                sends.append(_remote(part, dst.at[r - 1], *sem, (px, py, pc)))
                recvs.append(_remote(part, dst.at[r - 1], *sem, (px, py, pc)))
        return sends, recvs

    def start(ins, outs, sems):
        for cp in copies(ins, outs, sems)[0]:
            cp.start()

    def finish(ins, outs, sems):
        sends, recvs = copies(ins, outs, sems)
        for cp in recvs:
            cp.wait_recv()
        for cp in sends:
            cp.wait_send()

    return _Comm(grads, out_shapes, [_dma_sems(others * n), _dma_sems(others * n)], start, finish)


def _reduce_half(g, parts, pos, cut, *, name):
    ax, _ = cut
    others = parts.shape[0]
    if g.ndim == 3:
        nb, rows, cols = g.shape
        hb = nb // 2
        block, grid, out_shape = (1, rows // N_CHIPS, cols), (hb,), (nb, rows // N_CHIPS, cols)
        g_map = lambda i, pos: (pos[1] * hb + i, pos[0], 0)
        o_map = lambda i, pos: (pos[1] * hb + i, 0, 0)
        p_map = lambda i, pos: (0, i, 0, 0)
    elif ax == 1:
        rows, cols = g.shape
        tr = _row_tile(rows // 2)
        nt = rows // 2 // tr
        block, grid, out_shape = (tr, cols // N_CHIPS), (nt,), (rows, cols // N_CHIPS)
        g_map = lambda i, pos: (pos[1] * nt + i, pos[0])
        o_map = lambda i, pos: (pos[1] * nt + i, 0)
        p_map = lambda i, pos: (0, i, 0)
    else:
        rows, cols = g.shape
        tr = _row_tile(rows // N_CHIPS // 2)
        nt = rows // N_CHIPS // 2 // tr
        block, grid, out_shape = (tr, cols), (nt,), (rows // N_CHIPS, cols)
        g_map = lambda i, pos: (pos[0] * 2 * nt + pos[1] * nt + i, 0)
        o_map = lambda i, pos: (pos[1] * nt + i, 0)
        p_map = lambda i, pos: (0, i, 0)

    def body(pos_ref, g_ref, p_ref, o_ref):
        acc = g_ref[...].astype(F32)
        for r in range(others):
            acc = acc + p_ref[r].astype(F32)
        o_ref[...] = acc

    return _call(
        body, name=name, grid=grid, prefetch=1,
        in_specs=[pl.BlockSpec(block, g_map), pl.BlockSpec((others,) + block, p_map)],
        out_specs=pl.BlockSpec(block, o_map), out_shape=jax.ShapeDtypeStruct(out_shape, F32),
    )(pos, g, parts)


def _join_halves(bufs):
    return _call(None, name="join_halves", comm=_join_comm(bufs))()[1]


def _join_comm(bufs):
    n = len(bufs)

    def copies(ins, outs, sems):
        send_sems, recv_sems = sems
        x, y, c = _mesh_pos()
        sends = [_remote(_half(src, 0, c), _half(dst, 0, c), send_sems.at[t], recv_sems.at[t], (x, y, 1 - c))
                 for t, (src, dst) in enumerate(zip(ins, outs))]
        recvs = [_remote(_half(src, 0, 1 - c), _half(dst, 0, 1 - c), send_sems.at[t], recv_sems.at[t],
                         (x, y, 1 - c)) for t, (src, dst) in enumerate(zip(ins, outs))]
        return sends, recvs

    def start(ins, outs, sems):
        for cp in copies(ins, outs, sems)[0]:
            cp.start()

    def finish(ins, outs, sems):
        sends, recvs = copies(ins, outs, sems)
        for cp in recvs:
            cp.wait_recv()
        for cp in sends:
            cp.wait_send()

    return _Comm(bufs, [jax.ShapeDtypeStruct(b.shape, b.dtype) for b in bufs],
                 [_dma_sems(n), _dma_sems(n)], start, finish, aliases={i: i for i in range(n)})


def _allreduce_small(pack):
    rows, cols = pack.shape

    def body(p_ref, o_ref, slots, send_sems, recv_sems):
        x, y, c = _mesh_pos()
        me = 4 * x + 2 * y + c
        slots[me] = p_ref[...]
        copies = []
        for r in range(1, N_DEV):
            fx, fy, fc = (r >> 2) & 1, (r >> 1) & 1, r & 1
            dev = ((1 - x) if fx else x, (1 - y) if fy else y, (1 - c) if fc else c)
            cp = pltpu.make_async_remote_copy(
                src_ref=p_ref, dst_ref=slots.at[me], send_sem=send_sems.at[r - 1],
                recv_sem=recv_sems.at[r - 1], device_id=dev, device_id_type=MESH)
            cp.start()
            copies.append(cp)
        for cp in copies:
            cp.wait_recv()
        for cp in copies:
            cp.wait_send()
        acc = slots[0]
        for d in range(1, N_DEV):
            acc = acc + slots[d]
        o_ref[...] = acc

    vmem = pl.BlockSpec(memory_space=pltpu.VMEM)
    return _call(
        body, name="allreduce_small", in_specs=[vmem], out_specs=vmem,
        out_shape=jax.ShapeDtypeStruct((rows, cols), F32),
        scratch=[pltpu.VMEM((N_DEV, rows, cols), F32), pltpu.SemaphoreType.DMA((N_DEV - 1,)),
                 pltpu.SemaphoreType.DMA((N_DEV - 1,))],
    )(pack)


def _pad_rows(a, rows=8):
    return jnp.pad(a, ((0, rows - a.shape[0]), (0, 0)))


def kernel(x, positions, norm1_w, w_in, merge_gate_b, ret_gn_w, w_ret_o, lru_conv_w, lru_conv_b, lru_w_r, lru_b_r, lru_w_i, lru_b_i, lru_lambda, w_lru_o, w_out, norm2_w, ffn_w_up, ffn_conv_w, ffn_conv_b, ffn_w_down, norm_f_w, loss_target, m_norm1_w, m_w_in, m_merge_gate_b, m_ret_gn_w, m_w_ret_o, m_lru_conv_w, m_lru_conv_b, m_lru_w_r, m_lru_b_r, m_lru_w_i, m_lru_b_i, m_lru_lambda, m_w_lru_o, m_w_out, m_norm2_w, m_ffn_w_up, m_ffn_conv_w, m_ffn_conv_b, m_ffn_w_down, m_norm_f_w, v_norm1_w, v_w_in, v_merge_gate_b, v_ret_gn_w, v_w_ret_o, v_lru_conv_w, v_lru_conv_b, v_lru_w_r, v_lru_b_r, v_lru_w_i, v_lru_b_i, v_lru_lambda, v_w_lru_o, v_w_out, v_norm2_w, v_ffn_w_up, v_ffn_conv_w, v_ffn_conv_b, v_ffn_w_down, v_norm_f_w):
    names = ["norm1_w", "w_in", "merge_gate_b", "ret_gn_w", "w_ret_o", "lru_conv_w", "lru_conv_b", "lru_w_r",
             "lru_b_r", "lru_w_i", "lru_b_i", "lru_lambda", "w_lru_o", "w_out", "norm2_w", "ffn_w_up",
             "ffn_conv_w", "ffn_conv_b", "ffn_w_down", "norm_f_w"]
    w_args = dict(zip(names, [norm1_w, w_in, merge_gate_b, ret_gn_w, w_ret_o, lru_conv_w, lru_conv_b, lru_w_r,
                              lru_b_r, lru_w_i, lru_b_i, lru_lambda, w_lru_o, w_out, norm2_w, ffn_w_up,
                              ffn_conv_w, ffn_conv_b, ffn_w_down, norm_f_w]))
    m_args = dict(zip(names, [m_norm1_w, m_w_in, m_merge_gate_b, m_ret_gn_w, m_w_ret_o, m_lru_conv_w,
                              m_lru_conv_b, m_lru_w_r, m_lru_b_r, m_lru_w_i, m_lru_b_i, m_lru_lambda, m_w_lru_o,
                              m_w_out, m_norm2_w, m_ffn_w_up, m_ffn_conv_w, m_ffn_conv_b, m_ffn_w_down,
                              m_norm_f_w]))
    v_args = dict(zip(names, [v_norm1_w, v_w_in, v_merge_gate_b, v_ret_gn_w, v_w_ret_o, v_lru_conv_w,
                              v_lru_conv_b, v_lru_w_r, v_lru_b_r, v_lru_w_i, v_lru_b_i, v_lru_lambda, v_w_lru_o,
                              v_w_out, v_norm2_w, v_ffn_w_up, v_ffn_conv_w, v_ffn_conv_b, v_ffn_w_down,
                              v_norm_f_w]))

    bsz, seq, d = x.shape
    m = bsz * seq
    tm = min(MM_ROWS, m)
    tm_fused = min(FUSED_ROWS, m)
    tm_tall = min(TALL_ROWS, m)
    chip = 2 * lax.axis_index("x") + lax.axis_index("y")

    big = ["w_in", "w_ret_o", "w_lru_o", "w_out", "lru_w_r", "lru_w_i", "ffn_w_up", "ffn_w_down"]
    cut = dict(w_in=(1, 0), w_ret_o=(0, 0), w_lru_o=(0, 0), w_out=(0, 0), lru_w_r=(1, 0), lru_w_i=(1, 0),
               ffn_w_up=(1, 0), ffn_w_down=(0, 0))
    core = lax.axis_index("c")
    chip1 = jnp.reshape(chip, (1,)).astype(jnp.int32)
    pos = jnp.stack([chip, core]).astype(jnp.int32)
    placed = {n: _place_shard(w_args[n][0], chip1, cut[n][0], name="place_" + n) for n in big}
    small_pack = jnp.concatenate([
        jnp.pad(merge_gate_b[0], ((0, 6), (0, 512))),
        jnp.pad(lru_conv_w[0], ((0, 4), (0, 512))),
        jnp.pad(lru_b_r[0], ((0, 4), (0, 704))),
        jnp.pad(lru_b_i[0], ((0, 4), (0, 704))),
        jnp.pad(ffn_conv_w[0], ((0, 5), (0, 0))),
    ], axis=0)
    x2d = x.reshape(m, d)
    mx, my = lax.axis_index("x"), lax.axis_index("y")
    order = jnp.stack([chip, 2 * (1 - mx) + my, 2 * mx + 1 - my, 2 * (1 - mx) + 1 - my]).astype(jnp.int32)
    mixer = ["w_ret_o", "w_lru_o", "w_out", "lru_w_r", "lru_w_i"]
    cuts = lambda ns: [cut[n] for n in ns]
    sizes = lambda ns: [w_args[n].shape[1 + cut[n][0]] for n in ns]
    h1 = _norm_bf16(x2d, norm1_w, name="norm1", tm=tm)
    proj, w_in_full, bufs, sp = _in_proj_gather(h1, placed["w_in"], [placed[n] for n in mixer], cuts(mixer),
                                               small_pack, order, tm=tm_tall)
    wb = {"w_in": w_in_full}
    mgb = jnp.transpose(sp[:, 0:2, 0:256], (1, 0, 2)).reshape(2, D_MODEL)
    lcw = jnp.transpose(sp[:, 8:12, 0:256], (1, 0, 2)).reshape(4, D_MODEL)
    lbr = jnp.transpose(sp[:, 16:20, 0:64], (1, 0, 2)).reshape(1, D_MODEL)
    lbi = jnp.transpose(sp[:, 24:28, 0:64], (1, 0, 2)).reshape(1, D_MODEL)
    fcw = jnp.transpose(sp[:, 32:35, :], (1, 0, 2)).reshape(3, D_FF)
    nfw = norm_f_w.reshape(1, D_MODEL)

    half = RET_DK // 2
    inv_freq = ROPE_BASE ** (-jnp.arange(half, dtype=F32) / half)
    cos, sin = _rope_tables(positions.reshape(bsz, seq, 1), jnp.concatenate([inv_freq, inv_freq]).reshape(1, RET_DK))
    proj3 = proj.reshape(bsz, seq, D_IN)
    down, up_w = ["ffn_w_down"], ["ffn_w_up"]
    (a_in3, states), bufs = _retention_fwd(
        proj3, cos, sin, ret_gn_w,
        comm=_both(_gather_d2d(bufs, cuts(mixer), sizes(mixer)), _gather_ici([placed["ffn_w_down"]], cuts(down))))
    wb.update(zip(mixer, bufs[:len(mixer)]))
    lru_params = (lcw, lru_conv_b, wb["lru_w_r"], lbr, wb["lru_w_i"], lbi, lru_lambda)
    (b_in3, *lru_kept), (up_buf, wb["ffn_w_down"]) = _lru_fwd(
        proj3, lru_params,
        comm=_both(_gather_ici([placed["ffn_w_up"]], cuts(up_w)), _gather_d2d(bufs[len(mixer):], cuts(down), sizes(down))))
    a_in, b_in = a_in3.reshape(m, d), b_in3.reshape(m, d)
    (x2, mix, ya, yb), (wb["ffn_w_up"],) = _merge_fwd(
        a_in, b_in, proj, mgb, wb["w_ret_o"], wb["w_lru_o"], wb["w_out"], x2d, tm=tm_fused,
        comm=_gather_d2d([up_buf], cuts(up_w), sizes(up_w)))
    up, h2 = _norm_matmul(x2, norm2_w, wb["ffn_w_up"], name="ffn_up", tm=tm_tall, tn=TALL_COLS)
    up3 = up.reshape(bsz, seq, 2 * D_FF)
    f3, act3, q3 = _ffn_act_fwd(up3, fcw, ffn_conv_b)
    f = f3.reshape(m, D_FF)
    dx3, sm_nf = _ffn_down_loss(f, wb["ffn_w_down"], x2, nfw, loss_target.reshape(m, d), tm=tm)

    def send(*ns):
        return _exchange([g_full[n] for n in ns], [cut[n] for n in ns])

    g_full, parts = {}, {}
    df = _mm_nt(dx3, wb["ffn_w_down"], name="ffn_down_dx", tm=tm_tall, out_dtype=BF16)
    g_full["ffn_w_down"] = _mm_tn(f, [dx3], name="ffn_down_dw", tm=tm_tall)
    (dgate3, dval3, sm_ffn), (parts["ffn_w_down"],) = _ffn_act_bwd(
        up3, act3, q3, fcw, df.reshape(bsz, seq, D_FF), comm=send("ffn_w_down"))
    dup = [dgate3.reshape(m, D_FF), dval3.reshape(m, D_FF)]
    g_full["ffn_w_up"] = _mm_tn(h2, dup, name="ffn_up_dw", tm=tm_tall)
    (dx2, sm_n2), (parts["ffn_w_up"],) = _mm_nt_normbwd(
        dup, wb["ffn_w_up"], x2, norm2_w, dx3, name="ffn_up_dx", tm=tm, row=ROW_N2, comm=send("ffn_w_up"))
    dya, dyb, da_in, db_in, dgates, sm_mg = _merge_bwd(
        ya, yb, proj, mgb, wb["w_ret_o"], wb["w_lru_o"], wb["w_out"], dx2, tm=tm_fused)
    g_full["w_out"] = _mm_tn(mix, [dx2], name="out_dw", tm=tm_tall)
    g_full["w_ret_o"] = _mm_tn(a_in, [dya], name="ret_o_dw", tm=tm_tall)
    g_full["w_lru_o"] = _mm_tn(b_in, [dyb], name="lru_o_dw", tm=tm_tall)
    (dlru3, dwr, dwi, sm_lru), (parts["w_out"], parts["w_ret_o"], parts["w_lru_o"]) = _lru_bwd(
        proj3, lru_params, lru_kept, db_in.reshape(bsz, seq, d), comm=send("w_out", "w_ret_o", "w_lru_o"))
    g_full["lru_w_r"], g_full["lru_w_i"] = dwr.astype(BF16), dwi.astype(BF16)
    (dret3, sm_gn), (parts["lru_w_r"], parts["lru_w_i"]) = _retention_bwd(
        proj3, cos, sin, ret_gn_w, states, da_in.reshape(bsz, seq, d), comm=send("lru_w_r", "lru_w_i"))
    dproj = [dret3.reshape(m, 3072), dlru3.reshape(m, 2048), dgates]
    g_full["w_in"] = _mm_tn(h1, dproj, name="in_proj_dw", tm=tm_tall)
    half_sum = lambda n: _reduce_half(g_full[n], parts[n], pos, cut[n], name="sum_" + n)
    (grad_x, sm_n1), (parts["w_in"], *joined) = _mm_nt_normbwd(
        dproj, wb["w_in"], x2d, norm1_w, dx2, name="in_proj_dx", tm=tm, row=ROW_N1,
        comm=_both(send("w_in"), _join_comm([half_sum(n) for n in big[1:]])))
    reduced = _join_halves([half_sum("w_in")]) + joined
    misc = sm_n1 + sm_mg + sm_gn + sm_n2 + sm_nf
    pack = jnp.concatenate(
        [misc, sm_lru, sm_ffn[:, 0:1024], sm_ffn[:, 1024:2048], sm_ffn[:, 2048:3072]], axis=0)
    tot = _allreduce_small(pack)
    ffn_sm = jnp.concatenate([tot[16:24], tot[24:32], tot[32:40]], axis=1)
    g_small = {
        "norm1_w": tot[ROW_N1:ROW_N1 + 1], "merge_gate_b": tot[ROW_MGB:ROW_MGB + 2],
        "ret_gn_w": tot[ROW_GN:ROW_GN + 1], "norm2_w": tot[ROW_N2:ROW_N2 + 1], "norm_f_w": tot[ROW_NF:ROW_NF + 1],
        "lru_conv_w": tot[8:12], "lru_conv_b": tot[12:13], "lru_b_r": tot[13:14].reshape(4, 256),
        "lru_b_i": tot[14:15].reshape(4, 256), "lru_lambda": tot[15:16],
        "ffn_conv_w": ffn_sm[0:3], "ffn_conv_b": ffn_sm[3:4],
    }
    small_shard = dict(merge_gate_b=256, lru_conv_w=256, lru_b_r=64, lru_b_i=64, ffn_conv_w=768)

    outs = {}
    for n, g in zip(big, reduced):
        shape = w_args[n].shape
        g = g.reshape(-1, g.shape[-1])
        outs[n] = [o.reshape(shape) for o in _adamw(
            w_args[n].reshape(g.shape), [g], m_args[n].reshape(g.shape), v_args[n].reshape(g.shape),
            name="adamw_" + n)]
    for n, g in g_small.items():
        shape = w_args[n].shape
        if n in small_shard:
            g = lax.dynamic_slice_in_dim(g, chip * small_shard[n], small_shard[n], axis=1)
        w2 = w_args[n].reshape(g.shape)
        outs[n] = [o.reshape(shape) for o in _adamw(
            w2, [g], m_args[n].reshape(g.shape), v_args[n].reshape(g.shape), name="adamw_" + n)]

    result = [tot[ROW_LOSS, 0], grad_x.reshape(bsz, seq, d)]
    for k in range(4):
        result += [outs[n][k] for n in names]
    return tuple(result)
```

```python
import functools
import math

import numpy as np
import jax
import jax.numpy as jnp
from jax import lax
from jax.experimental import pallas as pl
from jax.experimental.pallas import tpu as pltpu

F32 = jnp.float32
BF16 = jnp.bfloat16

D_MODEL = 1024
RET_HEADS = 4
RET_DK = 128
RET_DV = 256
LRU_BLOCKS = 4
LRU_BLOCK = 256
LRU_C = 8.0
D_FF = 3072
D_IN = 7168
ROPE_BASE = 10000.0
RMS_EPS = 1e-6
GN_EPS = 1e-6
ADAM_LR, ADAM_B1, ADAM_B2, ADAM_EPS, ADAM_WD, ADAM_STEP = 0.001, 0.9, 0.999, 1e-08, 0.01, 10

N_CHIPS = 4
N_DEV = 8
SEQ_T = 256
FFN_STRIP = 1024
RET_T = 512
REF_CHUNK = 64
COL = 1024
MM_ROWS = 1024
TALL_ROWS, TALL_COLS = 2048, 1024
FUSED_ROWS = 512
VMEM_LIMIT_BYTES = 56 * 1024 * 1024
MESH = pl.DeviceIdType.MESH
ROW_N1, ROW_MGB, ROW_GN, ROW_N2, ROW_NF, ROW_LOSS = 0, 1, 3, 4, 5, 6
GELU_K = math.sqrt(2.0 / math.pi)
GELU_C = 0.044715


class _Comm:
    def __init__(self, ins, outs, sems, start, finish, aliases=None):
        self.ins, self.outs, self.sems = list(ins), list(outs), list(sems)
        self.start, self.finish, self.aliases = start, finish, dict(aliases or {})


def _call(body, *, name, out_shape=(), grid=None, in_specs=(), out_specs=(), scratch=(), comm=None, prefetch=0,
          aliases=None):
    single = not isinstance(out_shape, (list, tuple))
    out_shape = [out_shape] if single else list(out_shape)
    out_specs = [out_specs] if single else list(out_specs)
    in_specs, scratch = list(in_specs), list(scratch)
    n_in, n_out, n_scr = len(in_specs), len(out_shape), len(scratch)
    kwargs = dict(name=name, compiler_params=pltpu.CompilerParams(vmem_limit_bytes=VMEM_LIMIT_BYTES))
    if prefetch:
        assert comm is None
        spec = pltpu.PrefetchScalarGridSpec(num_scalar_prefetch=prefetch, grid=grid, in_specs=in_specs,
                                            out_specs=out_specs, scratch_shapes=scratch)
        fn = pl.pallas_call(body, out_shape=out_shape, grid_spec=spec, input_output_aliases=dict(aliases or {}),
                            **kwargs)
        return (lambda *args: fn(*args)[0]) if single else fn
    if grid is not None:
        kwargs["grid"] = grid
    if comm is None:
        fn = pl.pallas_call(body, out_shape=out_shape, in_specs=in_specs, out_specs=out_specs,
                            scratch_shapes=scratch, **kwargs)
        return (lambda *args: fn(*args)[0]) if single else fn

    any_spec = pl.BlockSpec(memory_space=pl.ANY)
    n_cin, n_cout = len(comm.ins), len(comm.outs)

    def wrapped(*refs):
        ins, refs = refs[:n_in], refs[n_in:]
        cins, refs = refs[:n_cin], refs[n_cin:]
        outs, refs = refs[:n_out], refs[n_out:]
        couts, refs = refs[:n_cout], refs[n_cout:]
        scr, csems = refs[:n_scr], refs[n_scr:]
        if grid is None:
            comm.start(cins, couts, csems)
            comm.finish(cins, couts, csems)
            return
        ids = [pl.program_id(a) for a in range(len(grid))]
        first = functools.reduce(jnp.logical_and, [i == 0 for i in ids])
        last = functools.reduce(jnp.logical_and, [i == g - 1 for i, g in zip(ids, grid)])
        pl.when(first)(lambda: comm.start(cins, couts, csems))
        body(*ins, *outs, *scr)
        pl.when(last)(lambda: comm.finish(cins, couts, csems))

    fn = pl.pallas_call(
        wrapped, out_shape=out_shape + comm.outs, in_specs=in_specs + [any_spec] * n_cin,
        out_specs=out_specs + [any_spec] * n_cout, scratch_shapes=scratch + comm.sems,
        input_output_aliases={n_in + i: n_out + o for i, o in comm.aliases.items()}, **kwargs)

    def run(*args):
        res = fn(*args, *comm.ins)
        own = res[0] if single else list(res[:n_out])
        return own, list(res[n_out:])

    return run


def _dot(a, b):
    return jnp.dot(a, b, preferred_element_type=F32)


def _dot_nt(a, b):
    return lax.dot_general(a, b, (((1,), (1,)), ((), ())), preferred_element_type=F32)


def _dot_tn(a, b):
    return lax.dot_general(a, b, (((0,), (0,)), ((), ())), preferred_element_type=F32)


def _bf(x):
    return x.astype(BF16)


def _sigmoid(x):
    return 1.0 / (1.0 + jnp.exp(-x))


def _gelu_and_grad(x):
    x2 = x * x
    s = _sigmoid(x * (2.0 * GELU_K * GELU_C * x2 + 2.0 * GELU_K))
    g = x * s
    dg = s + g * (1.0 - s) * (6.0 * GELU_K * GELU_C * x2 + 2.0 * GELU_K)
    return g, dg


def _rms(x):
    r = lax.rsqrt(jnp.mean(x * x, axis=-1, keepdims=True) + RMS_EPS)
    return x * r, r


def _rms_bwd(dy, x, nw):
    xh, r = _rms(x)
    g = dy * nw
    dx = r * (g - xh * jnp.mean(g * xh, axis=-1, keepdims=True))
    return dx, jnp.sum(dy * xh, axis=0, keepdims=True)


def _row_acc(ref, row, val):
    ref[row:row + 1, :] = ref[row:row + 1, :] + val


def _shift_down(x, j, prev8):
    if j == 0:
        return x
    n = x.shape[0] // 8
    row = lax.broadcasted_iota(jnp.int32, prev8.shape, 0)
    turned = [pltpu.roll(prev8, j, 0)] + [pltpu.roll(x[8 * k:8 * k + 8], j, 0) for k in range(n)]
    return jnp.concatenate([jnp.where(row < j, turned[k], turned[k + 1]) for k in range(n)], axis=0)


def _shift_up(x, j, next8):
    if j == 0:
        return x
    n = x.shape[0] // 8
    row = lax.broadcasted_iota(jnp.int32, next8.shape, 0)
    turned = [pltpu.roll(x[8 * k:8 * k + 8], 8 - j, 0) for k in range(n)] + [pltpu.roll(next8, 8 - j, 0)]
    return jnp.concatenate([jnp.where(row >= 8 - j, turned[k + 1], turned[k]) for k in range(n)], axis=0)


def _scan_fwd(a, b, carry):
    row = lax.broadcasted_iota(jnp.int32, (8, a.shape[1]), 0)
    out = []
    for k in range(a.shape[0] // 8):
        ak, bk = a[8 * k:8 * k + 8], b[8 * k:8 * k + 8]
        for s in (1, 2, 4):
            keep = row >= s
            ar, br = pltpu.roll(ak, s, 0), pltpu.roll(bk, s, 0)
            bk = jnp.where(keep, ak * br + bk, bk)
            ak = jnp.where(keep, ak * ar, ak)
        hk = ak * carry + bk
        carry = hk[7:8]
        out.append(hk)
    return jnp.concatenate(out, axis=0)


def _scan_bwd(a, b, carry):
    row = lax.broadcasted_iota(jnp.int32, (8, a.shape[1]), 0)
    out = []
    for k in reversed(range(a.shape[0] // 8)):
        ak, bk = a[8 * k:8 * k + 8], b[8 * k:8 * k + 8]
        for s in (1, 2, 4):
            keep = row < 8 - s
            ar, br = pltpu.roll(ak, 8 - s, 0), pltpu.roll(bk, 8 - s, 0)
            bk = jnp.where(keep, ak * br + bk, bk)
            ak = jnp.where(keep, ak * ar, ak)
        gk = bk + ak * carry
        carry = gk[0:1]
        out.append(gk)
    return jnp.concatenate(out[::-1], axis=0)


def _norm_matmul(x, nw, w, *, name, tm, tn):
    m, d = x.shape
    n = w.shape[1]

    def body(x_ref, nw_ref, w_ref, o_ref, h_ref, h_sc):
        @pl.when(pl.program_id(1) == 0)
        def _():
            xh, _ = _rms(x_ref[...])
            h = _bf(xh * nw_ref[...])
            h_sc[...] = h
            h_ref[...] = h

        o_ref[...] = _bf(_dot(h_sc[...], w_ref[...]))

    return _call(
        body, name=name, grid=(m // tm, n // tn),
        in_specs=[pl.BlockSpec((tm, d), lambda i, j: (i, 0)),
                  pl.BlockSpec((1, d), lambda i, j: (0, 0)),
                  pl.BlockSpec((d, tn), lambda i, j: (0, j))],
        out_specs=[pl.BlockSpec((tm, tn), lambda i, j: (i, j)),
                   pl.BlockSpec((tm, d), lambda i, j: (i, 0))],
        out_shape=[jax.ShapeDtypeStruct((m, n), BF16), jax.ShapeDtypeStruct((m, d), BF16)],
        scratch=[pltpu.VMEM((tm, d), BF16)],
    )(x, nw, w)


def _mm_nt(a, w, *, name, tm, out_dtype):
    m, k = a.shape
    n = w.shape[0]

    def body(a_ref, w_ref, o_ref):
        o_ref[...] = _dot_nt(_bf(a_ref[...]), w_ref[...]).astype(out_dtype)

    return _call(
        body, name=name, grid=(m // tm, n // COL),
        in_specs=[pl.BlockSpec((tm, k), lambda i, j: (i, 0)),
                  pl.BlockSpec((COL, k), lambda i, j: (j, 0))],
        out_specs=pl.BlockSpec((tm, COL), lambda i, j: (i, j)),
        out_shape=jax.ShapeDtypeStruct((m, n), out_dtype),
    )(a, w)


def _piece_layout(pieces):
    offs, nblk, o = [], [], 0
    for p in pieces:
        offs.append(o)
        nblk.append(p.shape[1] // COL)
        o += p.shape[1] // COL
    return offs, nblk, o


def _mm_tn(a, pieces, *, name, tm, out_dtype=BF16):
    m, k = a.shape
    offs, nblk, nn = _piece_layout(pieces)

    def piece_spec(o, nb):
        def idx(ki, nj, mi):
            use = jnp.logical_and(nj >= o, nj < o + nb)
            return (jnp.where(use, mi, 0), jnp.clip(nj - o, 0, nb - 1))
        return pl.BlockSpec((tm, COL), idx)

    def body(a_ref, *rest):
        p_refs, o_ref, acc = rest[:len(pieces)], rest[len(pieces)], rest[len(pieces) + 1]
        nj, mi = pl.program_id(1), pl.program_id(2)

        @pl.when(mi == 0)
        def _():
            acc[...] = jnp.zeros_like(acc)

        for p_ref, o, nb in zip(p_refs, offs, nblk):
            @pl.when(jnp.logical_and(nj >= o, nj < o + nb))
            def _(p_ref=p_ref):
                acc[...] += _dot_tn(_bf(a_ref[...]), _bf(p_ref[...]))

        @pl.when(mi == pl.num_programs(2) - 1)
        def _():
            o_ref[...] = acc[...].astype(out_dtype)

    return _call(
        body, name=name, grid=(k // COL, nn, m // tm),
        in_specs=[pl.BlockSpec((tm, COL), lambda ki, nj, mi: (mi, ki))]
        + [piece_spec(o, nb) for o, nb in zip(offs, nblk)],
        out_specs=pl.BlockSpec((COL, COL), lambda ki, nj, mi: (ki, nj)),
        out_shape=jax.ShapeDtypeStruct((k, nn * COL), out_dtype),
        scratch=[pltpu.VMEM((COL, COL), F32)],
    )(a, *pieces)


def _mm_nt_normbwd(pieces, w, x, nw, dres, *, name, tm, row, comm=None):
    m, d = x.shape
    offs, nblk, nk = _piece_layout(pieces)

    def piece_spec(o, nb):
        return pl.BlockSpec((tm, COL), lambda i, k: (i, jnp.clip(k - o, 0, nb - 1)))

    def body(*refs):
        p_refs = refs[:len(pieces)]
        w_ref, x_ref, nw_ref, dres_ref, dx_ref, dnw_ref, acc = refs[len(pieces):]
        i, k = pl.program_id(0), pl.program_id(1)

        @pl.when(jnp.logical_and(i == 0, k == 0))
        def _():
            dnw_ref[...] = jnp.zeros_like(dnw_ref)

        @pl.when(k == 0)
        def _():
            acc[...] = jnp.zeros_like(acc)

        for p_ref, o, nb in zip(p_refs, offs, nblk):
            @pl.when(jnp.logical_and(k >= o, k < o + nb))
            def _(p_ref=p_ref):
                acc[...] += _dot_nt(_bf(p_ref[...]), w_ref[...])

        @pl.when(k == nk - 1)
        def _():
            dx, dnw = _rms_bwd(acc[...], x_ref[...], nw_ref[...])
            dx_ref[...] = dres_ref[...] + dx
            _row_acc(dnw_ref, row, dnw)

    return _call(
        body, name=name, grid=(m // tm, nk), comm=comm,
        in_specs=[piece_spec(o, nb) for o, nb in zip(offs, nblk)]
        + [pl.BlockSpec((d, COL), lambda i, k: (0, k)),
           pl.BlockSpec((tm, d), lambda i, k: (i, 0)),
           pl.BlockSpec((1, d), lambda i, k: (0, 0)),
           pl.BlockSpec((tm, d), lambda i, k: (i, 0))],
        out_specs=[pl.BlockSpec((tm, d), lambda i, k: (i, 0)),
                   pl.BlockSpec((8, d), lambda i, k: (0, 0))],
        out_shape=[jax.ShapeDtypeStruct((m, d), F32), jax.ShapeDtypeStruct((8, d), F32)],
        scratch=[pltpu.VMEM((tm, d), F32)],
    )(*pieces, w, x, nw, dres)


def _rope_tables(pos3, invf):
    b, s, _ = pos3.shape

    def body(pos_ref, invf_ref, cos_ref, sin_ref):
        half_t, half_d = s // 2, RET_DK // 2
        pos = pos_ref[...].astype(F32)
        low = lax.broadcasted_iota(jnp.int32, (half_t, RET_DK), 1) < half_d
        ang = jnp.where(low, pos[0:half_t], pos[half_t:]) * invf_ref[...]
        co, si = jnp.cos(ang), jnp.sin(ang)
        co_turned, si_turned = pltpu.roll(co, half_d, 1), pltpu.roll(si, half_d, 1)
        sign = jnp.where(low, -1.0, 1.0)
        cos_ref[...] = jnp.concatenate([jnp.where(low, co, co_turned), jnp.where(low, co_turned, co)], axis=0)
        sin_ref[...] = jnp.concatenate([sign * jnp.where(low, si, si_turned), sign * jnp.where(low, si_turned, si)],
                                       axis=0)

    spec = pl.BlockSpec((None, s, RET_DK), lambda i: (i, 0, 0))
    return _call(
        body, name="rope_tables", grid=(b,),
        in_specs=[pl.BlockSpec((None, s, 1), lambda i: (i, 0, 0)),
                  pl.BlockSpec((1, RET_DK), lambda i: (0, 0))],
        out_specs=[spec, spec],
        out_shape=[jax.ShapeDtypeStruct((b, s, RET_DK), F32)] * 2,
    )(pos3, invf)


def _log_gamma(h):
    return float(np.log1p(-np.power(np.float32(2.0), np.float32(-5.0 - h))).astype(np.float32))


def _decay_matrix(h):
    lg = _log_gamma(h)
    n = lax.broadcasted_iota(jnp.int32, (RET_T, RET_T), 0)
    m = lax.broadcasted_iota(jnp.int32, (RET_T, RET_T), 1)
    same = (n // REF_CHUNK) == (m // REF_CHUNK)
    dist = jnp.where(same, jnp.abs(n - m), n - m).astype(F32)
    return jnp.where(jnp.logical_or(same, m < n), jnp.exp(lg * dist), 0.0)


def _decay_vectors(h):
    lg = _log_gamma(h)
    idx = lax.broadcasted_iota(jnp.int32, (RET_T, 1), 0).astype(F32)
    qd = jnp.exp(lg * (idx + 1.0))
    kd = jnp.exp(lg * (RET_T - 1.0 - idx))
    return qd, kd, math.exp(lg * RET_T)


def _rotate(x, cos, sin):
    return x * cos + pltpu.roll(x, RET_DK // 2, 1) * sin


def _rotate_bwd(d, cos, sin):
    return d * cos + pltpu.roll(d * sin, RET_DK // 2, 1)


def _ret_head(p_ref, cos, sin, h):
    q = p_ref[:, h * RET_DK:(h + 1) * RET_DK].astype(F32)
    k = p_ref[:, 512 + h * RET_DK:512 + (h + 1) * RET_DK].astype(F32)
    v = p_ref[:, 1024 + h * RET_DV:1024 + (h + 1) * RET_DV]
    g = p_ref[:, 2048 + h * RET_DV:2048 + (h + 1) * RET_DV].astype(F32)
    qr = _rotate(q, cos, sin)
    kr = _rotate(k, cos, sin) * (RET_DK ** -0.5)
    return qr, kr, v, g


def _group_norm(o):
    mu = jnp.mean(o, axis=-1, keepdims=True)
    oc = o - mu
    rstd = lax.rsqrt(jnp.mean(oc * oc, axis=-1, keepdims=True) + GN_EPS)
    return oc * rstd, rstd


def _retention_fwd(proj3, cos, sin, gnw, comm=None):
    b, s, _ = proj3.shape
    nc = s // RET_T

    def body(p_ref, cos_ref, sin_ref, gnw_ref, a_ref, st_ref, state, wtab):
        c = pl.program_id(1)

        @pl.when(jnp.logical_and(pl.program_id(0) == 0, c == 0))
        def _():
            for h in range(RET_HEADS):
                wtab[h] = _decay_matrix(h)

        @pl.when(c == 0)
        def _():
            state[...] = jnp.zeros_like(state)

        cs, sn = cos_ref[...], sin_ref[...]
        st_ref[...] = state[...]
        outs, states = [], []
        for h in range(RET_HEADS):
            qd, kd, gt = _decay_vectors(h)
            qr, kr, v, g = _ret_head(p_ref, cs, sn, h)
            st = state[h]
            p = _dot_nt(_bf(qr), _bf(kr)) * wtab[h]
            o = _dot(_bf(p), _bf(v)) + _dot(_bf(qr * qd), _bf(st))
            states.append(st * gt + _dot_tn(_bf(kr * kd), _bf(v)))
            on, _ = _group_norm(o)
            gw = gnw_ref[:, h * RET_DV:(h + 1) * RET_DV]
            outs.append(_bf(on * gw * (g * _sigmoid(g))))
        a_ref[...] = jnp.concatenate(outs, axis=1)
        state[...] = jnp.stack(states)

    tab = pl.BlockSpec((None, RET_T, RET_DK), lambda i, c: (i, c, 0))
    return _call(
        body, name="retention_fwd", grid=(b, nc), comm=comm,
        in_specs=[pl.BlockSpec((None, RET_T, 3072), lambda i, c: (i, c, 0)), tab, tab,
                  pl.BlockSpec((1, D_MODEL), lambda i, c: (0, 0))],
        out_specs=[pl.BlockSpec((None, RET_T, D_MODEL), lambda i, c: (i, c, 0)),
                   pl.BlockSpec((None, None, RET_HEADS, RET_DK, RET_DV), lambda i, c: (i, c, 0, 0, 0))],
        out_shape=[jax.ShapeDtypeStruct((b, s, D_MODEL), BF16),
                   jax.ShapeDtypeStruct((b, nc, RET_HEADS, RET_DK, RET_DV), F32)],
        scratch=[pltpu.VMEM((RET_HEADS, RET_DK, RET_DV), F32),
                 pltpu.VMEM((RET_HEADS, RET_T, RET_T), F32)],
    )(proj3, cos, sin, gnw)


def _retention_bwd(proj3, cos, sin, gnw, states, da3, comm=None):
    b, s, _ = proj3.shape
    nc = s // RET_T

    def body(p_ref, cos_ref, sin_ref, gnw_ref, st_ref, da_ref, d_ref, dgn_ref, dstate, wtab):
        c = pl.program_id(1)

        @pl.when(jnp.logical_and(pl.program_id(0) == 0, c == 0))
        def _():
            dgn_ref[...] = jnp.zeros_like(dgn_ref)
            for h in range(RET_HEADS):
                wtab[h] = _decay_matrix(h)

        @pl.when(c == 0)
        def _():
            dstate[...] = jnp.zeros_like(dstate)

        cs, sn = cos_ref[...], sin_ref[...]
        dqs, dks, dvs, dgs, dgns, dstates = [], [], [], [], [], []
        for h in range(RET_HEADS):
            qd, kd, gt = _decay_vectors(h)
            qr, kr, v, g = _ret_head(p_ref, cs, sn, h)
            st, dst, w = st_ref[h], dstate[h], wtab[h]
            qb, kb, vb = _bf(qr), _bf(kr), _bf(v)
            p = _dot_nt(qb, kb) * w
            o = _dot(_bf(p), vb) + _dot(_bf(qr * qd), _bf(st))
            on, rstd = _group_norm(o)
            gw = gnw_ref[:, h * RET_DV:(h + 1) * RET_DV]
            da = da_ref[:, h * RET_DV:(h + 1) * RET_DV].astype(F32)
            sg = _sigmoid(g)
            silu = g * sg
            dg = da * on * gw * (sg * (1.0 + g * (1.0 - sg)))
            dgns.append(jnp.sum(da * silu * on, axis=0, keepdims=True))
            don = da * silu * gw
            do = rstd * (don - jnp.mean(don, axis=-1, keepdims=True)
                         - on * jnp.mean(don * on, axis=-1, keepdims=True))
            dob = _bf(do)
            dp = _dot_nt(dob, vb) * w
            dqr = _dot(_bf(dp), kb) + _dot_nt(dob, _bf(st)) * qd
            dkr = _dot_tn(_bf(dp), qb) + _dot_nt(vb, _bf(dst)) * kd
            dv = _dot_tn(_bf(p), dob) + _dot(_bf(kr * kd), _bf(dst))
            dstates.append(dst * gt + _dot_tn(_bf(qr * qd), dob))
            dqs.append(_bf(_rotate_bwd(dqr, cs, sn)))
            dks.append(_bf(_rotate_bwd(dkr, cs, sn) * (RET_DK ** -0.5)))
            dvs.append(_bf(dv))
            dgs.append(_bf(dg))
        d_ref[...] = jnp.concatenate(dqs + dks + dvs + dgs, axis=1)
        _row_acc(dgn_ref, ROW_GN, jnp.concatenate(dgns, axis=1))
        dstate[...] = jnp.stack(dstates)

    rev = lambda i, c: (i, nc - 1 - c, 0)
    tab = pl.BlockSpec((None, RET_T, RET_DK), rev)
    return _call(
        body, name="retention_bwd", grid=(b, nc), comm=comm,
        in_specs=[pl.BlockSpec((None, RET_T, 3072), rev), tab, tab,
                  pl.BlockSpec((1, D_MODEL), lambda i, c: (0, 0)),
                  pl.BlockSpec((None, None, RET_HEADS, RET_DK, RET_DV), lambda i, c: (i, nc - 1 - c, 0, 0, 0)),
                  pl.BlockSpec((None, RET_T, D_MODEL), rev)],
        out_specs=[pl.BlockSpec((None, RET_T, 3072), rev),
                   pl.BlockSpec((8, D_MODEL), lambda i, c: (0, 0))],
        out_shape=[jax.ShapeDtypeStruct((b, s, 3072), BF16), jax.ShapeDtypeStruct((8, D_MODEL), F32)],
        scratch=[pltpu.VMEM((RET_HEADS, RET_DK, RET_DV), F32),
                 pltpu.VMEM((RET_HEADS, RET_T, RET_T), F32)],
    )(proj3, cos, sin, gnw, states, da3)


def _softplus_neg(lam):
    z = -lam
    u = jnp.exp(-jnp.abs(z))
    log1p_u = jnp.where(u < 0.01, u * (1.0 - u * (0.5 - u * (1.0 / 3.0))), jnp.log(1.0 + u))
    return jnp.maximum(z, 0.0) + log1p_u


def _lru_coeffs(xc, wr_ref, br_ref, wi_ref, bi_ref, lam_ref):
    rs, is_ = [], []
    for n in range(LRU_BLOCKS):
        xb = _bf(xc[:, n * LRU_BLOCK:(n + 1) * LRU_BLOCK])
        cols = slice(n * LRU_BLOCK, (n + 1) * LRU_BLOCK)
        rs.append(_sigmoid(_dot(xb, wr_ref[n]) + br_ref[:, cols]))
        is_.append(_sigmoid(_dot(xb, wi_ref[n]) + bi_ref[:, cols]))
    r = jnp.concatenate(rs, axis=1)
    i = jnp.concatenate(is_, axis=1)
    sp = _softplus_neg(lam_ref[...])
    la = -LRU_C * r * sp
    a = jnp.exp(la)
    s = jnp.sqrt(-jnp.tanh(la) * (a * a + 1.0))
    return r, i, a, s, sp


_LRU_PARAM_SPECS = [
    pl.BlockSpec((4, D_MODEL), lambda i, c: (0, 0)),
    pl.BlockSpec((1, D_MODEL), lambda i, c: (0, 0)),
    pl.BlockSpec((LRU_BLOCKS, LRU_BLOCK, LRU_BLOCK), lambda i, c: (0, 0, 0)),
    pl.BlockSpec((1, D_MODEL), lambda i, c: (0, 0)),
    pl.BlockSpec((LRU_BLOCKS, LRU_BLOCK, LRU_BLOCK), lambda i, c: (0, 0, 0)),
    pl.BlockSpec((1, D_MODEL), lambda i, c: (0, 0)),
    pl.BlockSpec((1, D_MODEL), lambda i, c: (0, 0)),
]


def _lru_fwd(proj3, params, comm=None):
    b, s, _ = proj3.shape
    nc = s // SEQ_T

    def body(x_ref, y_ref, cw, cb, wr, br, wi, bi, lam,
             o_ref, h_ref, xc_ref, a_ref, s_ref, gy_ref, hdg_ref, r_ref, i_ref, xprev, hprev):
        @pl.when(pl.program_id(1) == 0)
        def _():
            xprev[...] = jnp.zeros_like(xprev)
            hprev[...] = jnp.zeros_like(hprev)

        x = x_ref[...].astype(F32)
        prev8 = xprev[...]
        xc = cb[...] + sum(cw[j:j + 1, :] * _shift_down(x, 3 - j, prev8) for j in range(4))
        xprev[...] = x[SEQ_T - 8:]
        xc_ref[...] = xc
        r, i, a, s_, _ = _lru_coeffs(xc, wr, br, wi, bi, lam)
        a_ref[...] = a
        s_ref[...] = s_
        r_ref[...] = _bf(r)
        i_ref[...] = _bf(i)
        h = _scan_fwd(a, s_ * (i * xc), hprev[7:8, :])
        hprev[...] = h[SEQ_T - 8:]
        h_ref[...] = h
        gy, dgy = _gelu_and_grad(y_ref[...].astype(F32))
        o_ref[...] = _bf(h * gy)
        gy_ref[...] = _bf(gy)
        hdg_ref[...] = _bf(h * dgy)

    out = pl.BlockSpec((None, SEQ_T, D_MODEL), lambda i, c: (i, c, 0))
    half, full = jax.ShapeDtypeStruct((b, s, D_MODEL), BF16), jax.ShapeDtypeStruct((b, s, D_MODEL), F32)
    return _call(
        body, name="lru_fwd", grid=(b, nc), comm=comm,
        in_specs=[pl.BlockSpec((None, SEQ_T, D_MODEL), lambda i, c: (i, c, 3)),
                  pl.BlockSpec((None, SEQ_T, D_MODEL), lambda i, c: (i, c, 4))] + _LRU_PARAM_SPECS,
        out_specs=[out] * 9, out_shape=[half, full, full, full, full, half, half, half, half],
        scratch=[pltpu.VMEM((8, D_MODEL), F32), pltpu.VMEM((8, D_MODEL), F32)],
    )(proj3, proj3, *params)


def _lru_bwd(proj3, params, kept, db3, comm=None):
    b, s, _ = proj3.shape
    nc = s // SEQ_T
    blk8 = SEQ_T // 8
    hseq = kept[0]

    def body(x_ref, h_ref, xc_ref, a_ref, s_ref, gy_ref, hdg_ref, r_ref, i_ref, hp_ref, db_ref,
             cw, cb, wr, br, wi, bi, lam, d_ref, dwr_ref, dwi_ref, sm_ref, gnext, anext, dxcnext):
        c = pl.program_id(1)
        first_chunk = c == nc - 1

        @pl.when(jnp.logical_and(pl.program_id(0) == 0, c == 0))
        def _():
            dwr_ref[...] = jnp.zeros_like(dwr_ref)
            dwi_ref[...] = jnp.zeros_like(dwi_ref)
            sm_ref[...] = jnp.zeros_like(sm_ref)

        @pl.when(c == 0)
        def _():
            gnext[...] = jnp.zeros_like(gnext)
            anext[...] = jnp.zeros_like(anext)
            dxcnext[...] = jnp.zeros_like(dxcnext)

        x, xc, h = x_ref[...].astype(F32), xc_ref[...], h_ref[...]
        hprev = hp_ref[...] * jnp.where(first_chunk, 0.0, 1.0)
        r, i, a, s_ = r_ref[...].astype(F32), i_ref[...].astype(F32), a_ref[...], s_ref[...]
        sp = _softplus_neg(lam[...])
        db = db_ref[...].astype(F32)
        dy = db * hdg_ref[...].astype(F32)
        a_up = _shift_up(a, 1, anext[...])
        g = _scan_bwd(a_up, db * gy_ref[...].astype(F32), gnext[0:1, :])
        gnext[...] = g[0:8]
        anext[...] = a[0:8]
        da = g * _shift_down(h, 1, hprev)
        ixc = i * xc
        dla = da * a - (g * ixc) * (a * a) / s_
        di = g * s_ * xc
        dxc = g * s_ * i
        dzr = dla * (-LRU_C * sp) * r * (1.0 - r)
        dzi = di * i * (1.0 - i)
        lam_v = lam[...]
        _row_acc(sm_ref, 7, jnp.sum(dla * (LRU_C * r), axis=0, keepdims=True) * _sigmoid(-lam_v))
        _row_acc(sm_ref, 5, jnp.sum(dzr, axis=0, keepdims=True))
        _row_acc(sm_ref, 6, jnp.sum(dzi, axis=0, keepdims=True))
        parts, dwr_parts, dwi_parts = [], [], []
        for n in range(LRU_BLOCKS):
            cols = slice(n * LRU_BLOCK, (n + 1) * LRU_BLOCK)
            xb, zr, zi = _bf(xc[:, cols]), _bf(dzr[:, cols]), _bf(dzi[:, cols])
            parts.append(dxc[:, cols] + _dot_nt(zr, wr[n]) + _dot_nt(zi, wi[n]))
            dwr_parts.append(_dot_tn(xb, zr))
            dwi_parts.append(_dot_tn(xb, zi))
        dwr_ref[...] += jnp.stack(dwr_parts)
        dwi_ref[...] += jnp.stack(dwi_parts)
        dxc = jnp.concatenate(parts, axis=1)
        _row_acc(sm_ref, 4, jnp.sum(dxc, axis=0, keepdims=True))
        nxt = dxcnext[...]
        dx = jnp.zeros_like(x)
        for j in range(4):
            ahead = _shift_up(dxc, 3 - j, nxt)
            dx = dx + cw[j:j + 1, :] * ahead
            _row_acc(sm_ref, j, jnp.sum(ahead * x, axis=0, keepdims=True))
        dxcnext[...] = dxc[0:8]
        d_ref[:, 0:D_MODEL] = _bf(dx)
        d_ref[:, D_MODEL:2 * D_MODEL] = _bf(dy)

    rev = lambda col: (lambda i, c: (i, nc - 1 - c, col))
    prev = lambda col: (lambda i, c: (i, jnp.maximum((nc - 1 - c) * blk8 - 1, 0), col))
    return _call(
        body, name="lru_bwd", grid=(b, nc), comm=comm,
        in_specs=[pl.BlockSpec((None, SEQ_T, D_MODEL), rev(3))]
        + [pl.BlockSpec((None, SEQ_T, D_MODEL), rev(0))] * len(kept)
        + [pl.BlockSpec((None, 8, D_MODEL), prev(0)), pl.BlockSpec((None, SEQ_T, D_MODEL), rev(0))]
        + _LRU_PARAM_SPECS,
        out_specs=[pl.BlockSpec((None, SEQ_T, 2 * D_MODEL), rev(0)),
                   pl.BlockSpec((LRU_BLOCKS, LRU_BLOCK, LRU_BLOCK), lambda i, c: (0, 0, 0)),
                   pl.BlockSpec((LRU_BLOCKS, LRU_BLOCK, LRU_BLOCK), lambda i, c: (0, 0, 0)),
                   pl.BlockSpec((8, D_MODEL), lambda i, c: (0, 0))],
        out_shape=[jax.ShapeDtypeStruct((b, s, 2 * D_MODEL), BF16),
                   jax.ShapeDtypeStruct((LRU_BLOCKS, LRU_BLOCK, LRU_BLOCK), F32),
                   jax.ShapeDtypeStruct((LRU_BLOCKS, LRU_BLOCK, LRU_BLOCK), F32),
                   jax.ShapeDtypeStruct((8, D_MODEL), F32)],
        scratch=[pltpu.VMEM((8, D_MODEL), F32)] * 3,
    )(proj3, *kept, hseq, db3, *params)


def _merge_parts(a_ref, b_ref, gr_ref, gl_ref, mgb_ref, wro_ref, wlo_ref):
    ya = _dot(a_ref[...], wro_ref[...])
    yb = _dot(b_ref[...], wlo_ref[...])
    sa = _sigmoid(gr_ref[...].astype(F32) + mgb_ref[0:1, :])
    sb = _sigmoid(gl_ref[...].astype(F32) + mgb_ref[1:2, :])
    return ya, yb, sa, sb


def _merge_specs(tm):
    row = lambda col: pl.BlockSpec((tm, D_MODEL), lambda i: (i, col))
    full = pl.BlockSpec((D_MODEL, D_MODEL), lambda i: (0, 0))
    return row, full


def _merge_fwd(a_in, b_in, proj, mgb, wro, wlo, wout, x, *, tm, comm=None):
    m = x.shape[0]
    row, full = _merge_specs(tm)

    def body(a_ref, b_ref, gr_ref, gl_ref, mgb_ref, wro_ref, wlo_ref, wout_ref, x_ref,
             o_ref, mix_ref, ya_ref, yb_ref):
        ya, yb, sa, sb = _merge_parts(a_ref, b_ref, gr_ref, gl_ref, mgb_ref, wro_ref, wlo_ref)
        mix = _bf(sa * ya + sb * yb)
        o_ref[...] = x_ref[...] + _dot(mix, wout_ref[...])
        mix_ref[...] = mix
        ya_ref[...] = _bf(ya)
        yb_ref[...] = _bf(yb)

    act = jax.ShapeDtypeStruct((m, D_MODEL), BF16)
    return _call(
        body, name="merge_fwd", grid=(m // tm,), comm=comm,
        in_specs=[row(0), row(0), row(5), row(6), pl.BlockSpec((2, D_MODEL), lambda i: (0, 0)),
                  full, full, full, row(0)],
        out_specs=[row(0)] * 4,
        out_shape=[jax.ShapeDtypeStruct((m, D_MODEL), F32), act, act, act],
    )(a_in, b_in, proj, proj, mgb, wro, wlo, wout, x)


def _merge_bwd(ya, yb, proj, mgb, wro, wlo, wout, dx2, *, tm):
    m = dx2.shape[0]
    row, full = _merge_specs(tm)

    def body(ya_ref, yb_ref, gr_ref, gl_ref, mgb_ref, wro_ref, wlo_ref, wout_ref, dx_ref,
             dya_ref, dyb_ref, da_ref, db_ref, dg_ref, sm_ref):
        @pl.when(pl.program_id(0) == 0)
        def _():
            sm_ref[...] = jnp.zeros_like(sm_ref)

        ya, yb = ya_ref[...].astype(F32), yb_ref[...].astype(F32)
        sa = _sigmoid(gr_ref[...].astype(F32) + mgb_ref[0:1, :])
        sb = _sigmoid(gl_ref[...].astype(F32) + mgb_ref[1:2, :])
        dmix = _dot_nt(_bf(dx_ref[...]), wout_ref[...])
        dya, dyb = _bf(dmix * sa), _bf(dmix * sb)
        dya_ref[...] = dya
        dyb_ref[...] = dyb
        dga = dmix * ya * sa * (1.0 - sa)
        dgb = dmix * yb * sb * (1.0 - sb)
        dg_ref[:, 0:D_MODEL] = _bf(dga)
        dg_ref[:, D_MODEL:2 * D_MODEL] = _bf(dgb)
        _row_acc(sm_ref, ROW_MGB, jnp.sum(dga, axis=0, keepdims=True))
        _row_acc(sm_ref, ROW_MGB + 1, jnp.sum(dgb, axis=0, keepdims=True))
        da_ref[...] = _bf(_dot_nt(dya, wro_ref[...]))
        db_ref[...] = _bf(_dot_nt(dyb, wlo_ref[...]))

    act = jax.ShapeDtypeStruct((m, D_MODEL), BF16)
    return _call(
        body, name="merge_bwd", grid=(m // tm,),
        in_specs=[row(0), row(0), row(5), row(6), pl.BlockSpec((2, D_MODEL), lambda i: (0, 0)),
                  full, full, full, row(0)],
        out_specs=[row(0)] * 4 + [pl.BlockSpec((tm, 2 * D_MODEL), lambda i: (i, 0)),
                                  pl.BlockSpec((8, D_MODEL), lambda i: (0, 0))],
        out_shape=[act] * 4 + [jax.ShapeDtypeStruct((m, 2 * D_MODEL), BF16),
                               jax.ShapeDtypeStruct((8, D_MODEL), F32)],
    )(ya, yb, proj, proj, mgb, wro, wlo, wout, dx2)


def _ffn_act_fwd(up3, cw, cb):
    b, s, _ = up3.shape

    def body(g_ref, v_ref, cw_ref, cb_ref, o_ref, act_ref, q_ref, gprev):
        @pl.when(pl.program_id(1) == 0)
        def _():
            gprev[...] = jnp.zeros_like(gprev)

        for lo in range(0, D_FF, FFN_STRIP):
            cols = slice(lo, lo + FFN_STRIP)
            gate, val = g_ref[:, cols].astype(F32), v_ref[:, cols].astype(F32)
            prev8 = gprev[:, cols]
            gc = cb_ref[:, cols] + sum(cw_ref[j:j + 1, cols] * _shift_down(gate, 2 - j, prev8) for j in range(3))
            gprev[:, cols] = gate[SEQ_T - 8:]
            act, dact = _gelu_and_grad(gc)
            o_ref[:, cols] = _bf(act * val)
            act_ref[:, cols] = _bf(act)
            q_ref[:, cols] = _bf(dact * val)

    out = pl.BlockSpec((None, SEQ_T, D_FF), lambda i, c: (i, c, 0))
    return _call(
        body, name="ffn_act_fwd", grid=(b, s // SEQ_T),
        in_specs=[pl.BlockSpec((None, SEQ_T, D_FF), lambda i, c: (i, c, 0)),
                  pl.BlockSpec((None, SEQ_T, D_FF), lambda i, c: (i, c, 1)),
                  pl.BlockSpec((3, D_FF), lambda i, c: (0, 0)),
                  pl.BlockSpec((1, D_FF), lambda i, c: (0, 0))],
        out_specs=[out] * 3,
        out_shape=[jax.ShapeDtypeStruct((b, s, D_FF), BF16)] * 3,
        scratch=[pltpu.VMEM((8, D_FF), F32)],
    )(up3, up3, cw, cb)


def _ffn_act_bwd(up3, act3, q3, cw, df3, comm=None):
    b, s, _ = up3.shape
    nc = s // SEQ_T

    def body(g_ref, act_ref, q_ref, df_ref, cw_ref, dg_ref, dv_ref, sm_ref, dgcnext):
        c = pl.program_id(1)

        @pl.when(jnp.logical_and(pl.program_id(0) == 0, c == 0))
        def _():
            sm_ref[...] = jnp.zeros_like(sm_ref)

        @pl.when(c == 0)
        def _():
            dgcnext[...] = jnp.zeros_like(dgcnext)

        for lo in range(0, D_FF, FFN_STRIP):
            cols = slice(lo, lo + FFN_STRIP)
            gate = g_ref[:, cols].astype(F32)
            df = df_ref[:, cols].astype(F32)
            dv_ref[:, cols] = _bf(df * act_ref[:, cols].astype(F32))
            dgc = df * q_ref[:, cols].astype(F32)
            nxt = dgcnext[:, cols]
            dgate = jnp.zeros_like(gate)
            for j in range(3):
                ahead = _shift_up(dgc, 2 - j, nxt)
                dgate = dgate + cw_ref[j:j + 1, cols] * ahead
                sm_ref[j:j + 1, cols] += jnp.sum(ahead * gate, axis=0, keepdims=True)
            sm_ref[3:4, cols] += jnp.sum(dgc, axis=0, keepdims=True)
            dgcnext[:, cols] = dgc[0:8]
            dg_ref[:, cols] = _bf(dgate)

    rev = pl.BlockSpec((None, SEQ_T, D_FF), lambda i, c: (i, nc - 1 - c, 0))
    return _call(
        body, name="ffn_act_bwd", grid=(b, nc), comm=comm,
        in_specs=[rev, rev, rev, rev, pl.BlockSpec((3, D_FF), lambda i, c: (0, 0))],
        out_specs=[rev, rev, pl.BlockSpec((8, D_FF), lambda i, c: (0, 0))],
        out_shape=[jax.ShapeDtypeStruct((b, s, D_FF), BF16)] * 2 + [jax.ShapeDtypeStruct((8, D_FF), F32)],
        scratch=[pltpu.VMEM((8, D_FF), F32)],
    )(up3, act3, q3, df3, cw)


def _ffn_down_loss(f, wd, x2, nfw, target, *, tm):
    m, kf = f.shape
    nt = m // tm

    def body(f_ref, wd_ref, x_ref, nw_ref, t_ref, dx_ref, dnw_ref, lsum):
        i = pl.program_id(0)

        @pl.when(i == 0)
        def _():
            dnw_ref[...] = jnp.zeros_like(dnw_ref)
            lsum[...] = jnp.zeros_like(lsum)

        x3 = x_ref[...] + _dot(f_ref[...], wd_ref[...])
        nw = nw_ref[...]
        xh, r = _rms(x3)
        err = xh * nw - t_ref[...]
        lsum[...] += jnp.sum(err * err, axis=0, keepdims=True)
        dy = err * (1.0 / D_MODEL)
        g = dy * nw
        dx_ref[...] = r * (g - xh * jnp.mean(g * xh, axis=-1, keepdims=True))
        _row_acc(dnw_ref, ROW_NF, jnp.sum(dy * xh, axis=0, keepdims=True))

        @pl.when(i == nt - 1)
        def _():
            loss = jnp.sum(lsum[...], axis=1, keepdims=True) * (0.5 / D_MODEL)
            dnw_ref[ROW_LOSS:ROW_LOSS + 1, :] = jnp.broadcast_to(loss, (1, D_MODEL))

    row = pl.BlockSpec((tm, D_MODEL), lambda i: (i, 0))
    return _call(
        body, name="ffn_down_loss", grid=(nt,),
        in_specs=[pl.BlockSpec((tm, kf), lambda i: (i, 0)),
                  pl.BlockSpec((kf, D_MODEL), lambda i: (0, 0)),
                  row, pl.BlockSpec((1, D_MODEL), lambda i: (0, 0)), row],
        out_specs=[row, pl.BlockSpec((8, D_MODEL), lambda i: (0, 0))],
        out_shape=[jax.ShapeDtypeStruct((m, D_MODEL), F32), jax.ShapeDtypeStruct((8, D_MODEL), F32)],
        scratch=[pltpu.VMEM((1, D_MODEL), F32)],
    )(f, wd, x2, nfw, target)


def _row_tile(rows):
    return next((t for t in (256, 128, 64, 32, 16, 8) if rows % t == 0), rows)


def _adamw(w, gs, m, v, *, name):
    rows, cols = w.shape
    tr = _row_tile(rows)
    ng = len(gs)

    def body(w_ref, *rest):
        g_refs, (m_ref, v_ref, g_out, d_out, m_out, v_out) = rest[:ng], rest[ng:]
        g = g_refs[0][...]
        for r in g_refs[1:]:
            g = g + r[...]
        mn = ADAM_B1 * m_ref[...] + (1.0 - ADAM_B1) * g
        vn = ADAM_B2 * v_ref[...] + (1.0 - ADAM_B2) * (g * g)
        m_hat = mn / (1.0 - ADAM_B1 ** ADAM_STEP)
        v_hat = vn / (1.0 - ADAM_B2 ** ADAM_STEP)
        g_out[...] = g
        d_out[...] = -ADAM_LR * (m_hat / (jnp.sqrt(v_hat) + ADAM_EPS) + ADAM_WD * w_ref[...])
        m_out[...] = mn
        v_out[...] = vn

    spec = pl.BlockSpec((tr, cols), lambda i: (i, 0))
    return _call(
        body, name=name, grid=(rows // tr,),
        in_specs=[spec] * (3 + ng), out_specs=[spec] * 4,
        out_shape=[jax.ShapeDtypeStruct((rows, cols), F32)] * 4,
    )(w, *gs, m, v)


def _mesh_pos():
    x, y, c = lax.axis_index("x"), lax.axis_index("y"), lax.axis_index("c")
    return x, y, c


def _other_chips(x, y, c):
    return [((1 - x, y, c), 2 * (1 - x) + y), ((x, 1 - y, c), 2 * x + 1 - y),
            ((1 - x, 1 - y, c), 2 * (1 - x) + 1 - y)]


def _region(ref, axis, size, half_axis, chip, core=None):
    idx = [slice(None)] * len(ref.shape)
    if core is None:
        idx[axis] = pl.ds(pl.multiple_of(chip * size, size), size)
    elif half_axis == axis:
        h = size // 2
        idx[axis] = pl.ds(pl.multiple_of(chip * size + core * h, h), h)
    else:
        idx[axis] = pl.ds(pl.multiple_of(chip * size, size), size)
        h = ref.shape[half_axis] // 2
        idx[half_axis] = pl.ds(pl.multiple_of(core * h, h), h)
    return ref.at[tuple(idx)]


def _half(ref, half_axis, core):
    idx = [slice(None)] * len(ref.shape)
    h = ref.shape[half_axis] // 2
    idx[half_axis] = pl.ds(pl.multiple_of(core * h, h), h)
    return ref.at[tuple(idx)]


class _Copy:
    def __init__(self, make):
        self._make = make

    def start(self):
        self._make().start()

    def wait(self):
        self._make().wait()

    def wait_send(self):
        self._make().wait_send()

    def wait_recv(self):
        self._make().wait_recv()


def _remote(src, dst, send_sem, recv_sem, dev):
    return _Copy(lambda: pltpu.make_async_remote_copy(
        src_ref=src, dst_ref=dst, send_sem=send_sem, recv_sem=recv_sem, device_id=dev, device_id_type=MESH))


def _local(src, dst, sem):
    return _Copy(lambda: pltpu.make_async_copy(src, dst, sem))


def _dma_sems(n):
    return pltpu.SemaphoreType.DMA((n,))


def _place_shard(w, chip, axis, *, name):
    shape = list(w.shape)
    shape[axis] *= N_CHIPS
    if w.ndim == 3:
        block, grid = (1,) + w.shape[1:], (w.shape[0],)
        in_map, out_map = (lambda i, chip: (i, 0, 0)), (lambda i, chip: (i, chip[0], 0))
    else:
        tr = _row_tile(w.shape[0])
        nt = w.shape[0] // tr
        block, grid = (tr, w.shape[1]), (nt,)
        in_map = lambda i, chip: (i, 0)
        out_map = (lambda i, chip: (chip[0] * nt + i, 0)) if axis == 0 else (lambda i, chip: (i, chip[0]))

    def body(chip_ref, w_ref, o_ref):
        o_ref[...] = _bf(w_ref[...])

    return _call(body, name=name, grid=grid, prefetch=1, in_specs=[pl.BlockSpec(block, in_map)],
                 out_specs=pl.BlockSpec(block, out_map),
                 out_shape=jax.ShapeDtypeStruct(tuple(shape), BF16))(chip, w)


def _ici_leg(srcs, dsts, layout, sizes, n_whole, sems):
    send_sems, recv_sems, local_sems = sems
    x, y, c = _mesh_pos()
    mine = 2 * x + y
    n_big = len(srcs) - n_whole
    local, sends, recvs = [], [], []
    for t, (src, dst) in enumerate(zip(srcs, dsts)):
        if t < n_big:
            ax, hx = layout[t]
            part = _region(src, ax, sizes[t], hx, mine, c)
            landing = lambda chip, dst=dst, ax=ax, hx=hx, size=sizes[t]: _region(dst, ax, size, hx, chip, c)
        else:
            part, landing = src, (lambda chip, dst=dst: dst.at[chip])
            local.append(_local(src, dst.at[mine], local_sems.at[t - n_big]))
        for k, (dev, chip) in enumerate(_other_chips(x, y, c)):
            sends.append(_remote(part, landing(mine), send_sems.at[3 * t + k], recv_sems.at[3 * t + k], dev))
            recvs.append(_remote(part, landing(chip), send_sems.at[3 * t + k], recv_sems.at[3 * t + k], dev))
    return local, sends, recvs


def _d2d_leg(srcs, dsts, layout, sizes, sems):
    send_sems, recv_sems = sems
    x, y, c = _mesh_pos()
    sends, recvs = [], []
    for t, (src, dst) in enumerate(zip(srcs, dsts)):
        ax, hx = layout[t]
        for k, (_, chip) in enumerate(_other_chips(x, y, c)):
            sem = (send_sems.at[3 * t + k], recv_sems.at[3 * t + k])
            sends.append(_remote(_region(src, ax, sizes[t], hx, chip, c),
                                 _region(dst, ax, sizes[t], hx, chip, c), *sem, (x, y, 1 - c)))
            recvs.append(_remote(_region(src, ax, sizes[t], hx, chip, 1 - c),
                                 _region(dst, ax, sizes[t], hx, chip, 1 - c), *sem, (x, y, 1 - c)))
    return sends, recvs


def _gather_shapes(bufs, whole):
    return ([jax.ShapeDtypeStruct(b.shape, b.dtype) for b in bufs]
            + [jax.ShapeDtypeStruct((N_CHIPS,) + w.shape, w.dtype) for w in whole])


def _gather_ici(bufs, layout):
    n = len(bufs)
    sizes = [b.shape[ax] // N_CHIPS for b, (ax, _) in zip(bufs, layout)]

    def start(ins, outs, sems):
        for cp in _ici_leg(ins, outs, layout, sizes, 0, (*sems, None))[1]:
            cp.start()

    def finish(ins, outs, sems):
        _, sends, recvs = _ici_leg(ins, outs, layout, sizes, 0, (*sems, None))
        for cp in recvs:
            cp.wait_recv()
        for cp in sends:
            cp.wait_send()

    return _Comm(bufs, _gather_shapes(bufs, ()), [_dma_sems(3 * n), _dma_sems(3 * n)], start, finish,
                 aliases={i: i for i in range(n)})


def _both(a, b):
    ni, no, ns = len(a.ins), len(a.outs), len(a.sems)

    def start(ins, outs, sems):
        a.start(ins[:ni], outs[:no], sems[:ns])
        b.start(ins[ni:], outs[no:], sems[ns:])

    def finish(ins, outs, sems):
        a.finish(ins[:ni], outs[:no], sems[:ns])
        b.finish(ins[ni:], outs[no:], sems[ns:])

    aliases = {**a.aliases, **{ni + i: no + o for i, o in b.aliases.items()}}
    return _Comm(a.ins + b.ins, a.outs + b.outs, a.sems + b.sems, start, finish, aliases)


def _gather_d2d(bufs, layout, sizes):
    n = len(bufs)

    def start(ins, outs, sems):
        for cp in _d2d_leg(ins, outs, layout, sizes, sems)[0]:
            cp.start()

    def finish(ins, outs, sems):
        sends, recvs = _d2d_leg(ins, outs, layout, sizes, sems)
        for cp in recvs:
            cp.wait_recv()
        for cp in sends:
            cp.wait_send()

    return _Comm(bufs, [jax.ShapeDtypeStruct(b.shape, b.dtype) for b in bufs],
                 [_dma_sems(3 * n), _dma_sems(3 * n)], start, finish, aliases={i: i for i in range(n)})


def _norm_bf16(x, nw, *, name, tm):
    m, d = x.shape

    def body(x_ref, nw_ref, h_ref):
        h_ref[...] = _bf(_rms(x_ref[...])[0] * nw_ref[...])

    row = pl.BlockSpec((tm, d), lambda i: (i, 0))
    return _call(body, name=name, grid=(m // tm,), in_specs=[row, pl.BlockSpec((1, d), lambda i: (0, 0))],
                 out_specs=row, out_shape=jax.ShapeDtypeStruct((m, d), BF16))(x, nw)


def _in_proj_gather(h1, w_buf, later, later_cut, small, order, *, tm):
    m, d = h1.shape
    width = w_buf.shape[1] // N_CHIPS
    nr, nl = m // tm, len(later)
    sizes = [b.shape[ax] // N_CHIPS for b, (ax, _) in zip(later, later_cut)]

    def body(order_ref, h_ref, w_in, *rest):
        later_in, small_in = rest[:nl], rest[nl]
        o_ref, w_out = rest[nl + 1], rest[nl + 2]
        later_out, small_out = rest[nl + 3:2 * nl + 3], rest[2 * nl + 3]
        wv, load_sem, ici_send, ici_recv, d2d_send, d2d_recv, l_send, l_recv, l_local = rest[2 * nl + 4:]
        s, i = pl.program_id(0), pl.program_id(1)
        x, y, c = _mesh_pos()
        mine = 2 * x + y
        peers = _other_chips(x, y, c)
        part = lambda ref, chip, core=None: _region(ref, 1, width, 0, chip, core)

        def ici(k):
            dev, chip = peers[k]
            sem = (ici_send.at[k], ici_recv.at[k])
            return (_remote(part(w_in, mine, c), part(w_out, mine, c), *sem, dev),
                    _remote(part(w_in, chip, c), part(w_out, chip, c), *sem, dev))

        def d2d(k):
            chip, sem, sib = peers[k][1], (d2d_send.at[k], d2d_recv.at[k]), (x, y, 1 - c)
            return (_remote(part(w_out, chip, c), part(w_out, chip, c), *sem, sib),
                    _remote(part(w_out, chip, 1 - c), part(w_out, chip, 1 - c), *sem, sib))

        def load(src, chip, slot):
            cp = _local(part(src, chip), wv.at[slot], load_sem.at[slot])
            cp.start()
            cp.wait()

        def others():
            return _ici_leg(list(later_in) + [small_in], list(later_out) + [small_out], later_cut, sizes, 1,
                            (l_send, l_recv, l_local))

        @pl.when(jnp.logical_and(s == 0, i == 0))
        def _():
            for k in range(3):
                ici(k)[0].start()
            local, sends, _ = others()
            for cp in local + sends:
                cp.start()
            load(w_in, mine, 0)

        o_ref[...] = _bf(_dot(h_ref[...], wv[s % 2]))

        @pl.when(i == nr - 1)
        def _():
            for k in range(3):
                @pl.when(s == k)
                def _(k=k):
                    ici(k)[1].wait_recv()
                    d2d(k)[0].start()
                    d2d(k)[1].wait_recv()
                    load(w_out, peers[k][1], (k + 1) % 2)

            @pl.when(s == 3)
            def _():
                for k in range(3):
                    ici(k)[0].wait_send()
                    d2d(k)[0].wait_send()
                local, sends, recvs = others()
                for cp in recvs:
                    cp.wait_recv()
                for cp in sends:
                    cp.wait_send()
                for cp in local:
                    cp.wait()

    any_spec = pl.BlockSpec(memory_space=pl.ANY)
    n_any = nl + 2
    outs = _call(
        body, name="in_proj", grid=(N_CHIPS, nr), prefetch=1,
        in_specs=[pl.BlockSpec((tm, d), lambda s, i, order: (i, 0))] + [any_spec] * n_any,
        out_specs=[pl.BlockSpec((tm, width), lambda s, i, order: (i, order[s]))] + [any_spec] * n_any,
        out_shape=[jax.ShapeDtypeStruct((m, w_buf.shape[1]), BF16)] + _gather_shapes([w_buf] + list(later), [small]),
        scratch=[pltpu.VMEM((2, d, width), BF16), _dma_sems(2), _dma_sems(3), _dma_sems(3), _dma_sems(3),
                 _dma_sems(3), _dma_sems(3 * (nl + 1)), _dma_sems(3 * (nl + 1)), _dma_sems(1)],
        aliases={2 + t: 1 + t for t in range(nl + 1)},
    )(order, h1, w_buf, *later, small)
    return outs[0], outs[1], list(outs[2:2 + nl]), outs[2 + nl]


def _exchange(grads, layout):
    n = len(grads)
    others = N_DEV - 1
    sizes = [g.shape[ax] // N_CHIPS for g, (ax, _) in zip(grads, layout)]
    out_shapes = []
    for g, (ax, hx), sz in zip(grads, layout, sizes):
        shape = list(g.shape)
        shape[ax] = sz
        shape[hx] //= 2
        out_shapes.append(jax.ShapeDtypeStruct((others,) + tuple(shape), g.dtype))

    def copies(ins, outs, sems):
        send_sems, recv_sems = sems
        x, y, c = _mesh_pos()
        sends, recvs = [], []
        for t, (src, dst) in enumerate(zip(ins, outs)):
            ax, hx = layout[t]
            for r in range(1, N_DEV):
                px = (1 - x) if r & 4 else x
                py = (1 - y) if r & 2 else y
                pc = (1 - c) if r & 1 else c
                sem = (send_sems.at[others * t + r - 1], recv_sems.at[others * t + r - 1])
                part = _region(src, ax, sizes[t], hx, 2 * px + py, pc)
                sends.append(_remote(part, dst.at[r - 1], *sem, (px, py, pc)))
                recvs.append(_remote(part, dst.at[r - 1], *sem, (px, py, pc)))
        return sends, recvs

    def start(ins, outs, sems):
        for cp in copies(ins, outs, sems)[0]:
            cp.start()

    def finish(ins, outs, sems):
        sends, recvs = copies(ins, outs, sems)
        for cp in recvs:
            cp.wait_recv()
        for cp in sends:
            cp.wait_send()

    return _Comm(grads, out_shapes, [_dma_sems(others * n), _dma_sems(others * n)], start, finish)


def _reduce_half(g, parts, pos, cut, *, name):
    ax, _ = cut
    others = parts.shape[0]
    if g.ndim == 3:
        nb, rows, cols = g.shape
        hb = nb // 2
        block, grid, out_shape = (1, rows // N_CHIPS, cols), (hb,), (nb, rows // N_CHIPS, cols)
        g_map = lambda i, pos: (pos[1] * hb + i, pos[0], 0)
        o_map = lambda i, pos: (pos[1] * hb + i, 0, 0)
        p_map = lambda i, pos: (0, i, 0, 0)
    elif ax == 1:
        rows, cols = g.shape
        tr = _row_tile(rows // 2)
        nt = rows // 2 // tr
        block, grid, out_shape = (tr, cols // N_CHIPS), (nt,), (rows, cols // N_CHIPS)
        g_map = lambda i, pos: (pos[1] * nt + i, pos[0])
        o_map = lambda i, pos: (pos[1] * nt + i, 0)
        p_map = lambda i, pos: (0, i, 0)
    else:
        rows, cols = g.shape
        tr = _row_tile(rows // N_CHIPS // 2)
        nt = rows // N_CHIPS // 2 // tr
        block, grid, out_shape = (tr, cols), (nt,), (rows // N_CHIPS, cols)
        g_map = lambda i, pos: (pos[0] * 2 * nt + pos[1] * nt + i, 0)
        o_map = lambda i, pos: (pos[1] * nt + i, 0)
        p_map = lambda i, pos: (0, i, 0)

    def body(pos_ref, g_ref, p_ref, o_ref):
        acc = g_ref[...].astype(F32)
        for r in range(others):
            acc = acc + p_ref[r].astype(F32)
        o_ref[...] = acc

    return _call(
        body, name=name, grid=grid, prefetch=1,
        in_specs=[pl.BlockSpec(block, g_map), pl.BlockSpec((others,) + block, p_map)],
        out_specs=pl.BlockSpec(block, o_map), out_shape=jax.ShapeDtypeStruct(out_shape, F32),
    )(pos, g, parts)


def _join_halves(bufs):
    return _call(None, name="join_halves", comm=_join_comm(bufs))()[1]


def _join_comm(bufs):
    n = len(bufs)

    def copies(ins, outs, sems):
        send_sems, recv_sems = sems
        x, y, c = _mesh_pos()
        sends = [_remote(_half(src, 0, c), _half(dst, 0, c), send_sems.at[t], recv_sems.at[t], (x, y, 1 - c))
                 for t, (src, dst) in enumerate(zip(ins, outs))]
        recvs = [_remote(_half(src, 0, 1 - c), _half(dst, 0, 1 - c), send_sems.at[t], recv_sems.at[t],
                         (x, y, 1 - c)) for t, (src, dst) in enumerate(zip(ins, outs))]
        return sends, recvs

    def start(ins, outs, sems):
        for cp in copies(ins, outs, sems)[0]:
            cp.start()

    def finish(ins, outs, sems):
        sends, recvs = copies(ins, outs, sems)
        for cp in recvs:
            cp.wait_recv()
        for cp in sends:
            cp.wait_send()

    return _Comm(bufs, [jax.ShapeDtypeStruct(b.shape, b.dtype) for b in bufs],
                 [_dma_sems(n), _dma_sems(n)], start, finish, aliases={i: i for i in range(n)})


def _allreduce_small(pack):
    rows, cols = pack.shape

    def body(p_ref, o_ref, slots, send_sems, recv_sems):
        x, y, c = _mesh_pos()
        me = 4 * x + 2 * y + c
        slots[me] = p_ref[...]
        copies = []
        for r in range(1, N_DEV):
            fx, fy, fc = (r >> 2) & 1, (r >> 1) & 1, r & 1
            dev = ((1 - x) if fx else x, (1 - y) if fy else y, (1 - c) if fc else c)
            cp = pltpu.make_async_remote_copy(
                src_ref=p_ref, dst_ref=slots.at[me], send_sem=send_sems.at[r - 1],
                recv_sem=recv_sems.at[r - 1], device_id=dev, device_id_type=MESH)
            cp.start()
            copies.append(cp)
        for cp in copies:
            cp.wait_recv()
        for cp in copies:
            cp.wait_send()
        acc = slots[0]
        for d in range(1, N_DEV):
            acc = acc + slots[d]
        o_ref[...] = acc

    vmem = pl.BlockSpec(memory_space=pltpu.VMEM)
    return _call(
        body, name="allreduce_small", in_specs=[vmem], out_specs=vmem,
        out_shape=jax.ShapeDtypeStruct((rows, cols), F32),
        scratch=[pltpu.VMEM((N_DEV, rows, cols), F32), pltpu.SemaphoreType.DMA((N_DEV - 1,)),
                 pltpu.SemaphoreType.DMA((N_DEV - 1,))],
    )(pack)


def _pad_rows(a, rows=8):
    return jnp.pad(a, ((0, rows - a.shape[0]), (0, 0)))


def kernel(x, positions, norm1_w, w_in, merge_gate_b, ret_gn_w, w_ret_o, lru_conv_w, lru_conv_b, lru_w_r, lru_b_r, lru_w_i, lru_b_i, lru_lambda, w_lru_o, w_out, norm2_w, ffn_w_up, ffn_conv_w, ffn_conv_b, ffn_w_down, norm_f_w, loss_target, m_norm1_w, m_w_in, m_merge_gate_b, m_ret_gn_w, m_w_ret_o, m_lru_conv_w, m_lru_conv_b, m_lru_w_r, m_lru_b_r, m_lru_w_i, m_lru_b_i, m_lru_lambda, m_w_lru_o, m_w_out, m_norm2_w, m_ffn_w_up, m_ffn_conv_w, m_ffn_conv_b, m_ffn_w_down, m_norm_f_w, v_norm1_w, v_w_in, v_merge_gate_b, v_ret_gn_w, v_w_ret_o, v_lru_conv_w, v_lru_conv_b, v_lru_w_r, v_lru_b_r, v_lru_w_i, v_lru_b_i, v_lru_lambda, v_w_lru_o, v_w_out, v_norm2_w, v_ffn_w_up, v_ffn_conv_w, v_ffn_conv_b, v_ffn_w_down, v_norm_f_w):
    names = ["norm1_w", "w_in", "merge_gate_b", "ret_gn_w", "w_ret_o", "lru_conv_w", "lru_conv_b", "lru_w_r",
             "lru_b_r", "lru_w_i", "lru_b_i", "lru_lambda", "w_lru_o", "w_out", "norm2_w", "ffn_w_up",
             "ffn_conv_w", "ffn_conv_b", "ffn_w_down", "norm_f_w"]
    w_args = dict(zip(names, [norm1_w, w_in, merge_gate_b, ret_gn_w, w_ret_o, lru_conv_w, lru_conv_b, lru_w_r,
                              lru_b_r, lru_w_i, lru_b_i, lru_lambda, w_lru_o, w_out, norm2_w, ffn_w_up,
                              ffn_conv_w, ffn_conv_b, ffn_w_down, norm_f_w]))
    m_args = dict(zip(names, [m_norm1_w, m_w_in, m_merge_gate_b, m_ret_gn_w, m_w_ret_o, m_lru_conv_w,
                              m_lru_conv_b, m_lru_w_r, m_lru_b_r, m_lru_w_i, m_lru_b_i, m_lru_lambda, m_w_lru_o,
                              m_w_out, m_norm2_w, m_ffn_w_up, m_ffn_conv_w, m_ffn_conv_b, m_ffn_w_down,
                              m_norm_f_w]))
    v_args = dict(zip(names, [v_norm1_w, v_w_in, v_merge_gate_b, v_ret_gn_w, v_w_ret_o, v_lru_conv_w,
                              v_lru_conv_b, v_lru_w_r, v_lru_b_r, v_lru_w_i, v_lru_b_i, v_lru_lambda, v_w_lru_o,
                              v_w_out, v_norm2_w, v_ffn_w_up, v_ffn_conv_w, v_ffn_conv_b, v_ffn_w_down,
                              v_norm_f_w]))

    bsz, seq, d = x.shape
    m = bsz * seq
    tm = min(MM_ROWS, m)
    tm_fused = min(FUSED_ROWS, m)
    tm_tall = min(TALL_ROWS, m)
    chip = 2 * lax.axis_index("x") + lax.axis_index("y")

    big = ["w_in", "w_ret_o", "w_lru_o", "w_out", "lru_w_r", "lru_w_i", "ffn_w_up", "ffn_w_down"]
    cut = dict(w_in=(1, 0), w_ret_o=(0, 0), w_lru_o=(0, 0), w_out=(0, 0), lru_w_r=(1, 0), lru_w_i=(1, 0),
               ffn_w_up=(1, 0), ffn_w_down=(0, 0))
    core = lax.axis_index("c")
    chip1 = jnp.reshape(chip, (1,)).astype(jnp.int32)
    pos = jnp.stack([chip, core]).astype(jnp.int32)
    placed = {n: _place_shard(w_args[n][0], chip1, cut[n][0], name="place_" + n) for n in big}
    small_pack = jnp.concatenate([
        jnp.pad(merge_gate_b[0], ((0, 6), (0, 512))),
        jnp.pad(lru_conv_w[0], ((0, 4), (0, 512))),
        jnp.pad(lru_b_r[0], ((0, 4), (0, 704))),
        jnp.pad(lru_b_i[0], ((0, 4), (0, 704))),
        jnp.pad(ffn_conv_w[0], ((0, 5), (0, 0))),
    ], axis=0)
    x2d = x.reshape(m, d)
    mx, my = lax.axis_index("x"), lax.axis_index("y")
    order = jnp.stack([chip, 2 * (1 - mx) + my, 2 * mx + 1 - my, 2 * (1 - mx) + 1 - my]).astype(jnp.int32)
    mixer = ["w_ret_o", "w_lru_o", "w_out", "lru_w_r", "lru_w_i"]
    cuts = lambda ns: [cut[n] for n in ns]
    sizes = lambda ns: [w_args[n].shape[1 + cut[n][0]] for n in ns]
    h1 = _norm_bf16(x2d, norm1_w, name="norm1", tm=tm)
    proj, w_in_full, bufs, sp = _in_proj_gather(h1, placed["w_in"], [placed[n] for n in mixer], cuts(mixer),
                                               small_pack, order, tm=tm_tall)
    wb = {"w_in": w_in_full}
    mgb = jnp.transpose(sp[:, 0:2, 0:256], (1, 0, 2)).reshape(2, D_MODEL)
    lcw = jnp.transpose(sp[:, 8:12, 0:256], (1, 0, 2)).reshape(4, D_MODEL)
    lbr = jnp.transpose(sp[:, 16:20, 0:64], (1, 0, 2)).reshape(1, D_MODEL)
    lbi = jnp.transpose(sp[:, 24:28, 0:64], (1, 0, 2)).reshape(1, D_MODEL)
    fcw = jnp.transpose(sp[:, 32:35, :], (1, 0, 2)).reshape(3, D_FF)
    nfw = norm_f_w.reshape(1, D_MODEL)

    half = RET_DK // 2
    inv_freq = ROPE_BASE ** (-jnp.arange(half, dtype=F32) / half)
    cos, sin = _rope_tables(positions.reshape(bsz, seq, 1), jnp.concatenate([inv_freq, inv_freq]).reshape(1, RET_DK))
    proj3 = proj.reshape(bsz, seq, D_IN)
    down, up_w = ["ffn_w_down"], ["ffn_w_up"]
    (a_in3, states), bufs = _retention_fwd(
        proj3, cos, sin, ret_gn_w,
        comm=_both(_gather_d2d(bufs, cuts(mixer), sizes(mixer)), _gather_ici([placed["ffn_w_down"]], cuts(down))))
    wb.update(zip(mixer, bufs[:len(mixer)]))
    lru_params = (lcw, lru_conv_b, wb["lru_w_r"], lbr, wb["lru_w_i"], lbi, lru_lambda)
    (b_in3, *lru_kept), (up_buf, wb["ffn_w_down"]) = _lru_fwd(
        proj3, lru_params,
        comm=_both(_gather_ici([placed["ffn_w_up"]], cuts(up_w)), _gather_d2d(bufs[len(mixer):], cuts(down), sizes(down))))
    a_in, b_in = a_in3.reshape(m, d), b_in3.reshape(m, d)
    (x2, mix, ya, yb), (wb["ffn_w_up"],) = _merge_fwd(
        a_in, b_in, proj, mgb, wb["w_ret_o"], wb["w_lru_o"], wb["w_out"], x2d, tm=tm_fused,
        comm=_gather_d2d([up_buf], cuts(up_w), sizes(up_w)))
    up, h2 = _norm_matmul(x2, norm2_w, wb["ffn_w_up"], name="ffn_up", tm=tm_tall, tn=TALL_COLS)
    up3 = up.reshape(bsz, seq, 2 * D_FF)
    f3, act3, q3 = _ffn_act_fwd(up3, fcw, ffn_conv_b)
    f = f3.reshape(m, D_FF)
    dx3, sm_nf = _ffn_down_loss(f, wb["ffn_w_down"], x2, nfw, loss_target.reshape(m, d), tm=tm)

    def send(*ns):
        return _exchange([g_full[n] for n in ns], [cut[n] for n in ns])

    g_full, parts = {}, {}
    df = _mm_nt(dx3, wb["ffn_w_down"], name="ffn_down_dx", tm=tm_tall, out_dtype=BF16)
    g_full["ffn_w_down"] = _mm_tn(f, [dx3], name="ffn_down_dw", tm=tm_tall)
    (dgate3, dval3, sm_ffn), (parts["ffn_w_down"],) = _ffn_act_bwd(
        up3, act3, q3, fcw, df.reshape(bsz, seq, D_FF), comm=send("ffn_w_down"))
    dup = [dgate3.reshape(m, D_FF), dval3.reshape(m, D_FF)]
    g_full["ffn_w_up"] = _mm_tn(h2, dup, name="ffn_up_dw", tm=tm_tall)
    (dx2, sm_n2), (parts["ffn_w_up"],) = _mm_nt_normbwd(
        dup, wb["ffn_w_up"], x2, norm2_w, dx3, name="ffn_up_dx", tm=tm, row=ROW_N2, comm=send("ffn_w_up"))
    dya, dyb, da_in, db_in, dgates, sm_mg = _merge_bwd(
        ya, yb, proj, mgb, wb["w_ret_o"], wb["w_lru_o"], wb["w_out"], dx2, tm=tm_fused)
    g_full["w_out"] = _mm_tn(mix, [dx2], name="out_dw", tm=tm_tall)
    g_full["w_ret_o"] = _mm_tn(a_in, [dya], name="ret_o_dw", tm=tm_tall)
    g_full["w_lru_o"] = _mm_tn(b_in, [dyb], name="lru_o_dw", tm=tm_tall)
    (dlru3, dwr, dwi, sm_lru), (parts["w_out"], parts["w_ret_o"], parts["w_lru_o"]) = _lru_bwd(
        proj3, lru_params, lru_kept, db_in.reshape(bsz, seq, d), comm=send("w_out", "w_ret_o", "w_lru_o"))
    g_full["lru_w_r"], g_full["lru_w_i"] = dwr.astype(BF16), dwi.astype(BF16)
    (dret3, sm_gn), (parts["lru_w_r"], parts["lru_w_i"]) = _retention_bwd(
        proj3, cos, sin, ret_gn_w, states, da_in.reshape(bsz, seq, d), comm=send("lru_w_r", "lru_w_i"))
    dproj = [dret3.reshape(m, 3072), dlru3.reshape(m, 2048), dgates]
    g_full["w_in"] = _mm_tn(h1, dproj, name="in_proj_dw", tm=tm_tall)
    half_sum = lambda n: _reduce_half(g_full[n], parts[n], pos, cut[n], name="sum_" + n)
    (grad_x, sm_n1), (parts["w_in"], *joined) = _mm_nt_normbwd(
        dproj, wb["w_in"], x2d, norm1_w, dx2, name="in_proj_dx", tm=tm, row=ROW_N1,
        comm=_both(send("w_in"), _join_comm([half_sum(n) for n in big[1:]])))
    reduced = _join_halves([half_sum("w_in")]) + joined
    misc = sm_n1 + sm_mg + sm_gn + sm_n2 + sm_nf
    pack = jnp.concatenate(
        [misc, sm_lru, sm_ffn[:, 0:1024], sm_ffn[:, 1024:2048], sm_ffn[:, 2048:3072]], axis=0)
    tot = _allreduce_small(pack)
    ffn_sm = jnp.concatenate([tot[16:24], tot[24:32], tot[32:40]], axis=1)
    g_small = {
        "norm1_w": tot[ROW_N1:ROW_N1 + 1], "merge_gate_b": tot[ROW_MGB:ROW_MGB + 2],
        "ret_gn_w": tot[ROW_GN:ROW_GN + 1], "norm2_w": tot[ROW_N2:ROW_N2 + 1], "norm_f_w": tot[ROW_NF:ROW_NF + 1],
        "lru_conv_w": tot[8:12], "lru_conv_b": tot[12:13], "lru_b_r": tot[13:14].reshape(4, 256),
        "lru_b_i": tot[14:15].reshape(4, 256), "lru_lambda": tot[15:16],
        "ffn_conv_w": ffn_sm[0:3], "ffn_conv_b": ffn_sm[3:4],
    }
    small_shard = dict(merge_gate_b=256, lru_conv_w=256, lru_b_r=64, lru_b_i=64, ffn_conv_w=768)

    outs = {}
    for n, g in zip(big, reduced):
        shape = w_args[n].shape
        g = g.reshape(-1, g.shape[-1])
        outs[n] = [o.reshape(shape) for o in _adamw(
            w_args[n].reshape(g.shape), [g], m_args[n].reshape(g.shape), v_args[n].reshape(g.shape),
            name="adamw_" + n)]
    for n, g in g_small.items():
        shape = w_args[n].shape
        if n in small_shard:
            g = lax.dynamic_slice_in_dim(g, chip * small_shard[n], small_shard[n], axis=1)
        w2 = w_args[n].reshape(g.shape)
        outs[n] = [o.reshape(shape) for o in _adamw(
            w2, [g], m_args[n].reshape(g.shape), v_args[n].reshape(g.shape), name="adamw_" + n)]

    result = [tot[ROW_LOSS, 0], grad_x.reshape(bsz, seq, d)]
    for k in range(4):
        result += [outs[n][k] for n in names]
    return tuple(result)
```
